```python
import math
import jax
import jax.numpy as jnp
from jax import lax
import numpy as np

D_MODEL = 1024
BATCH = 16
SEQ = 2048
DEPTH = 1
DEC_BATCH = 32
DEC_SEQ = 1
PAST_LEN = 16384
PAGE_SIZE = 128

HEAD_DIM = 64
D_MIX = D_MODEL
SSD_HEADS = 8
SSD_D = SSD_HEADS * HEAD_DIM
SSD_GROUPS = 2
SSD_HEADS_PER_GROUP = SSD_HEADS // SSD_GROUPS
SSD_STATE = 128
SSD_CONV = 4
SSD_CONV_CH = SSD_D + 2 * SSD_GROUPS * SSD_STATE
SSD_CHUNK = 128
NSA_HEADS = 8
NSA_KV_HEADS = 2
NSA_GQA = NSA_HEADS // NSA_KV_HEADS
NSA_D = NSA_HEADS * HEAD_DIM
KV_D = NSA_KV_HEADS * HEAD_DIM
CMP_BLOCK = 64
CMP_HIDDEN = 128
SEL_BLOCK = 64
TOP_N = 16
WINDOW = 512
Q_BLOCK = 128
ROT_DIM = HEAD_DIM // 4
ROPE_THETA = 500000.0
N_EXPERTS = 64
TOP_K = 6
N_EXPERT_GROUPS = 8
TOPK_GROUPS = 4
D_EXPERT = 256
D_SHARED = 256
ROUTED_SCALE = 2.5
MOE_BLOCK = 128
DEEPNORM_ALPHA = (2.0 * DEPTH) ** 0.25
DEEPNORM_BETA = (8.0 * DEPTH) ** -0.25
LN_EPS = 1e-5
RMS_EPS = 1e-5
NEG = -1e30
FORCED_SCORE = 1e4
IN_SIZES = (SSD_D, SSD_CONV_CH, SSD_HEADS, NSA_D, KV_D, KV_D, KV_D, KV_D, KV_D, KV_D, 3 * NSA_HEADS)
D_IN = sum(IN_SIZES)

kernel_name = 'hymba_ssd_nsa_moe_deepnorm_step'


def layer_norm(x, g, b):
    xf = x.astype(jnp.float32)
    mu = jnp.mean(xf, -1, keepdims=True)
    var = jnp.mean(jnp.square(xf - mu), -1, keepdims=True)
    return ((xf - mu) * lax.rsqrt(var + LN_EPS) * g + b).astype(x.dtype)


def in_proj_split(h, w):
    u = jnp.einsum('btd,df->btf', h, w)
    return jnp.split(u, np.cumsum(IN_SIZES)[:-1].tolist(), axis=-1)


def to_heads(a, n):
    return a.reshape(a.shape[0], a.shape[1], n, HEAD_DIM)


def group_heads(q):
    return q.reshape(q.shape[0], q.shape[1], NSA_KV_HEADS, NSA_GQA, HEAD_DIM)


def partial_rope(x, pos):
    half = ROT_DIM // 2
    inv = ROPE_THETA ** (-jnp.arange(half, dtype=jnp.float32) / half)
    ang = pos.astype(jnp.float32)[:, None] * inv
    cos = jnp.cos(ang)[:, None, :]
    sin = jnp.sin(ang)[:, None, :]
    xr = x[..., :ROT_DIM].astype(jnp.float32)
    x1, x2 = xr[..., :half], xr[..., half:]
    rot = jnp.concatenate([x1 * cos - x2 * sin, x2 * cos + x1 * sin], -1).astype(x.dtype)
    return jnp.concatenate([rot, x[..., ROT_DIM:]], -1)


def masked_softmax(s, mask):
    p = jax.nn.softmax(jnp.where(mask, s, NEG), axis=-1)
    return jnp.where(mask, p, 0.0)


def attend_shared(q, k, v, mask):
    s = jnp.einsum('bthgd,bkhd->bthgk', q, k, preferred_element_type=jnp.float32) * HEAD_DIM ** -0.5
    p = masked_softmax(s, mask)
    return jnp.einsum('bthgk,bkhd->bthgd', p.astype(v.dtype), v), p


def attend_gathered(q, k, v, mask):
    s = jnp.einsum('...hgd,...hkd->...hgk', q, k, preferred_element_type=jnp.float32) * HEAD_DIM ** -0.5
    p = masked_softmax(s, mask[..., None, :])
    return jnp.einsum('...hgk,...hkd->...hgd', p.astype(v.dtype), v)


def compress_blocks(raw, pe, w1, b1, w2, b2):
    bn, length = raw.shape[0], raw.shape[1]
    nc = length // CMP_BLOCK
    blk = raw[:, :nc * CMP_BLOCK].reshape(bn, nc, CMP_BLOCK, NSA_KV_HEADS, HEAD_DIM) + pe[:, None, :]
    hid = jax.nn.silu(jnp.einsum('bclhd,ldf->bchf', blk, w1) + b1)
    return jnp.einsum('bchf,fd->bchd', hid, w2) + b2


def compress_kv(raw_k, raw_v, cmp_pe, cmp_w1, cmp_b1, cmp_w2, cmp_b2):
    kc = compress_blocks(raw_k, cmp_pe[0], cmp_w1[0], cmp_b1[0], cmp_w2[0], cmp_b2[0])
    vc = compress_blocks(raw_v, cmp_pe[1], cmp_w1[1], cmp_b1[1], cmp_w2[1], cmp_b2[1])
    return kc, vc


def compressed_branch(q_c, raw_k, raw_v, pos, cmp_args):
    kc, vc = compress_kv(raw_k, raw_v, *cmp_args)
    nc = kc.shape[1]
    vis = (jnp.arange(nc)[None, :] + 1) * CMP_BLOCK - 1 <= pos[:, None]
    return attend_shared(q_c, kc, vc, vis[None, :, None, None, :])


def select_blocks(p_cmp, pos, length):
    imp = jnp.sum(p_cmp, axis=3)
    nc = imp.shape[-1]
    n_cand = max(-(-length // SEL_BLOCK), TOP_N)
    imp = jnp.pad(imp, ((0, 0), (0, 0), (0, 0), (0, n_cand - nc)))
    j = jnp.arange(n_cand)[None, :]
    cur = (pos // SEL_BLOCK)[:, None]
    future = (j > cur)[:, None, :]
    forced = ((j == 0) | (j == cur) | (j == cur - 1))[:, None, :]
    score = jnp.where(future, NEG, jnp.where(forced, FORCED_SCORE, imp))
    top_s, idx = lax.top_k(score, TOP_N)
    return idx, top_s > 0.5 * NEG


def block_rows(idx):
    rows = idx[..., None] * SEL_BLOCK + jnp.arange(SEL_BLOCK)
    return rows.reshape(idx.shape[:-1] + (TOP_N * SEL_BLOCK,))


def combine_branches(gate_logits, o_cmp, o_slc, o_win):
    bn, t = gate_logits.shape[0], gate_logits.shape[1]
    g = jax.nn.sigmoid(gate_logits.astype(jnp.float32)).reshape(bn, t, NSA_KV_HEADS, NSA_GQA, 3).astype(o_cmp.dtype)
    o = g[..., 0:1] * o_cmp + g[..., 1:2] * o_slc + g[..., 2:3] * o_win
    return o.reshape(bn, t, NSA_D)


def nsa_prompt(q, k_cmp, v_cmp, k_sel, v_sel, k_win, v_win, gate_logits, cmp_args):
    bn, t = q.shape[0], q.shape[1]
    pos = jnp.arange(t)
    q_c = group_heads(q)
    q_r = group_heads(partial_rope(q, pos))
    k_sel = partial_rope(k_sel, pos)
    k_win = partial_rope(k_win, pos)
    o_cmp, p_cmp = compressed_branch(q_c, k_cmp, v_cmp, pos, cmp_args)
    idx, valid = select_blocks(p_cmp, pos, t)
    nqb = t // Q_BLOCK
    hk = jnp.arange(NSA_KV_HEADS)[None, :, None]

    def sel_item(item):
        b, qb, q_i, idx_i, val_i = item
        qpos = qb * Q_BLOCK + jnp.arange(Q_BLOCK)
        rows = jnp.minimum(block_rows(idx_i), t - 1)
        mask = jnp.repeat(val_i, SEL_BLOCK, axis=-1) & (rows <= qpos[:, None, None])
        return attend_gathered(q_i, k_sel[b, rows, hk], v_sel[b, rows, hk], mask)

    items = (jnp.repeat(jnp.arange(bn), nqb), jnp.tile(jnp.arange(nqb), bn),
             q_r.reshape(bn * nqb, Q_BLOCK, NSA_KV_HEADS, NSA_GQA, HEAD_DIM),
             idx.reshape(bn * nqb, Q_BLOCK, NSA_KV_HEADS, TOP_N),
             valid.reshape(bn * nqb, Q_BLOCK, NSA_KV_HEADS, TOP_N))
    o_slc = lax.map(sel_item, items).reshape(bn, t, NSA_KV_HEADS, NSA_GQA, HEAD_DIM)
    nkw = WINDOW + Q_BLOCK
    kw_pad = jnp.pad(k_win, ((0, 0), (WINDOW, 0), (0, 0), (0, 0)))
    vw_pad = jnp.pad(v_win, ((0, 0), (WINDOW, 0), (0, 0), (0, 0)))

    def win_block(i):
        s0 = i * Q_BLOCK
        qpos = s0 + jnp.arange(Q_BLOCK)
        kpos = s0 - WINDOW + jnp.arange(nkw)
        mask = (kpos[None] <= qpos[:, None]) & (kpos[None] >= qpos[:, None] - WINDOW) & (kpos[None] >= 0)
        o, _ = attend_shared(lax.dynamic_slice_in_dim(q_r, s0, Q_BLOCK, 1),
                             lax.dynamic_slice_in_dim(kw_pad, s0, nkw, 1),
                             lax.dynamic_slice_in_dim(vw_pad, s0, nkw, 1),
                             mask[None, :, None, None, :])
        return o

    o_win = jnp.moveaxis(lax.map(win_block, jnp.arange(nqb)), 0, 1).reshape(bn, t, NSA_KV_HEADS, NSA_GQA, HEAD_DIM)
    out = combine_branches(gate_logits, o_cmp, o_slc, o_win)
    n_keep = min(WINDOW, t)
    return (out, jnp.stack([k_cmp, v_cmp], 2), jnp.stack([k_sel, v_sel], 2),
            jnp.stack([k_win, v_win], 2)[:, t - n_keep:])


def fetch_rows(rows, pool, new, page_table, kind):
    bi = jnp.arange(rows.shape[0])[:, None, None, None]
    hk = jnp.arange(NSA_KV_HEADS)[None, None, :, None]
    past = jnp.minimum(rows, PAST_LEN - 1)
    phys = page_table[bi, past // PAGE_SIZE]
    from_pool = pool[phys, past % PAGE_SIZE, kind, hk]
    from_new = new[bi, jnp.clip(rows - PAST_LEN, 0, new.shape[1] - 1), kind, hk]
    return jnp.where((rows < PAST_LEN)[..., None], from_pool, from_new)


def nsa_sample(q, k_cmp, v_cmp, k_sel, v_sel, k_win, v_win, gate_logits,
               pool_cmp, pool_sel, page_table, buf_win, cmp_args):
    bn, t = q.shape[0], q.shape[1]
    length = PAST_LEN + t
    pos = PAST_LEN + jnp.arange(t)
    q_c = group_heads(q)
    q_r = group_heads(partial_rope(q, pos))
    new_cmp = jnp.stack([k_cmp, v_cmp], 2)
    new_sel = jnp.stack([partial_rope(k_sel, pos), v_sel], 2)
    new_win = jnp.stack([partial_rope(k_win, pos), v_win], 2)
    past_cmp = pool_cmp[page_table].reshape(bn, -1, 2, NSA_KV_HEADS, HEAD_DIM)
    full_cmp = jnp.concatenate([past_cmp, new_cmp], 1)
    o_cmp, p_cmp = compressed_branch(q_c, full_cmp[:, :, 0], full_cmp[:, :, 1], pos, cmp_args)
    idx, valid = select_blocks(p_cmp, pos, length)
    rows = jnp.minimum(block_rows(idx), length - 1)
    k_g = fetch_rows(rows, pool_sel, new_sel, page_table, 0)
    v_g = fetch_rows(rows, pool_sel, new_sel, page_table, 1)
    mask = jnp.repeat(valid, SEL_BLOCK, axis=-1) & (rows <= pos[None, :, None, None])
    o_slc = attend_gathered(q_r, k_g, v_g, mask)
    w = buf_win.shape[1]
    kv_w = jnp.concatenate([buf_win, new_win], 1)
    kpos = PAST_LEN - w + jnp.arange(w + t)
    mask_w = (kpos[None] <= pos[:, None]) & (kpos[None] >= pos[:, None] - WINDOW)
    o_win, _ = attend_shared(q_r, kv_w[:, :, 0], kv_w[:, :, 1], mask_w[None, :, None, None, :])
    out = combine_branches(gate_logits, o_cmp, o_slc, o_win)
    n_keep = min(WINDOW, length)
    return out, new_cmp, new_sel, kv_w[:, w + t - n_keep:]


def segsum(a):
    t = a.shape[-1]
    rep = jnp.broadcast_to(a[..., None], a.shape + (t,))
    cs = jnp.cumsum(jnp.where(jnp.tril(jnp.ones((t, t), bool), -1), rep, 0.0), axis=-2)
    return jnp.where(jnp.tril(jnp.ones((t, t), bool)), cs, -jnp.inf)


def ssd_chunked(x, a, bm, cm, h0, chunk):
    b, t, h, p = x.shape
    c = t // chunk
    x = x.reshape(b, c, chunk, h, p)
    bm = bm.reshape(b, c, chunk, h, -1)
    cm = cm.reshape(b, c, chunk, h, -1)
    a = a.reshape(b, c, chunk, h).transpose(0, 3, 1, 2)
    a_cum = jnp.cumsum(a, -1)

    def to_bclh(m):
        return m.transpose(0, 2, 3, 1)[..., None]

    cb = jnp.einsum('bclhn,bcshn->bhcls', cm, bm) * jnp.exp(segsum(a))
    y_diag = jnp.einsum('bhcls,bcshp->bclhp', cb, x)
    states = jnp.einsum('bclhn,bclhp->bchpn', bm, x * to_bclh(jnp.exp(a_cum[..., -1:] - a_cum)))
    states = jnp.concatenate([h0[:, None], states], 1)
    decay_chunk = jnp.exp(segsum(jnp.pad(a_cum[..., -1], ((0, 0), (0, 0), (1, 0)))))
    states = jnp.einsum('bhzc,bchpn->bzhpn', decay_chunk, states)
    y_off = jnp.einsum('bclhn,bchpn->bclhp', cm, states[:, :-1]) * to_bclh(jnp.exp(a_cum))
    return (y_diag + y_off).reshape(b, t, h, p), states[:, -1]


def ssd_mixer(z, xbc, dt_raw, conv_in, h0, conv_w, conv_b, dt_bias, a_log, d_skip, norm_w):
    bn, t = z.shape[0], z.shape[1]
    ext = jnp.concatenate([conv_in.astype(xbc.dtype), xbc], 1)
    xc = lax.conv_general_dilated(ext, conv_w[:, None, :].astype(xbc.dtype), (1,), 'VALID',
                                  dimension_numbers=('NWC', 'WIO', 'NWC'),
                                  feature_group_count=SSD_CONV_CH)
    xc = jax.nn.silu(xc + conv_b)
    new_conv = ext[:, ext.shape[1] - (SSD_CONV - 1):]
    xs, bm, cm = jnp.split(xc.astype(jnp.float32), [SSD_D, SSD_D + SSD_GROUPS * SSD_STATE], axis=-1)
    x = xs.reshape(bn, t, SSD_HEADS, HEAD_DIM)
    bm = jnp.repeat(bm.reshape(bn, t, SSD_GROUPS, SSD_STATE), SSD_HEADS_PER_GROUP, axis=2)
    cm = jnp.repeat(cm.reshape(bn, t, SSD_GROUPS, SSD_STATE), SSD_HEADS_PER_GROUP, axis=2)
    dt = jax.nn.softplus(dt_raw.astype(jnp.float32) + dt_bias)
    a = -jnp.exp(a_log.astype(jnp.float32))
    chunk = SSD_CHUNK if t % SSD_CHUNK == 0 else t
    y, h_new = ssd_chunked(x * dt[..., None], dt * a, bm, cm, h0.astype(jnp.float32), chunk)
    y = (y + d_skip[:, None] * x).reshape(bn, t, SSD_D) * jax.nn.silu(z.astype(jnp.float32))
    yg = y.reshape(bn, t, SSD_GROUPS, -1)
    yg = yg * lax.rsqrt(jnp.mean(jnp.square(yg), -1, keepdims=True) + RMS_EPS)
    y = yg.reshape(bn, t, SSD_D) * norm_w
    return y.astype(z.dtype), h_new.astype(z.dtype), new_conv


def swiglu(x, wg, wu, wd):
    return jnp.dot(jax.nn.silu(jnp.dot(x, wg)) * jnp.dot(x, wu), wd)


def moe_ffn(h, router_w, router_bias, w_gate, w_up, w_down, sh_gate, sh_up, sh_down):
    n, d = h.shape
    scores = jax.nn.sigmoid(jnp.dot(h, router_w, preferred_element_type=jnp.float32))
    biased = scores + router_bias.astype(jnp.float32)
    grp_score = jnp.sum(lax.top_k(biased.reshape(n, N_EXPERT_GROUPS, -1), 2)[0], -1)
    _, grp_idx = lax.top_k(grp_score, TOPK_GROUPS)
    grp_keep = jnp.any(grp_idx[:, :, None] == jnp.arange(N_EXPERT_GROUPS), axis=1)
    keep = jnp.repeat(grp_keep, N_EXPERTS // N_EXPERT_GROUPS, axis=1)
    _, idx = lax.top_k(jnp.where(keep, biased, NEG), TOP_K)
    w = jnp.take_along_axis(scores, idx, axis=1)
    w = w / jnp.sum(w, -1, keepdims=True) * ROUTED_SCALE
    nk = n * TOP_K
    flat_e = idx.reshape(nk)
    order = jnp.argsort(flat_e)
    e_sorted = flat_e[order]
    tok_sorted = (order // TOP_K).astype(jnp.int32)
    w_sorted = w.reshape(nk)[order]
    counts = jnp.bincount(flat_e, length=N_EXPERTS)
    padded = (counts + MOE_BLOCK - 1) // MOE_BLOCK * MOE_BLOCK
    pad_end = jnp.cumsum(padded)
    pad_start = pad_end - padded
    start = jnp.cumsum(counts) - counts
    dest = pad_start[e_sorted] + jnp.arange(nk) - start[e_sorted]
    n_blocks = -(-(nk + N_EXPERTS * (MOE_BLOCK - 1)) // MOE_BLOCK)
    cap = n_blocks * MOE_BLOCK
    slot_tok = jnp.full((cap,), n, jnp.int32).at[dest].set(tok_sorted).reshape(n_blocks, MOE_BLOCK)
    slot_w = jnp.zeros((cap,), jnp.float32).at[dest].set(w_sorted).reshape(n_blocks, MOE_BLOCK)
    block_e = jnp.minimum(jnp.searchsorted(pad_end, jnp.arange(n_blocks) * MOE_BLOCK, side='right'), N_EXPERTS - 1)
    h_pad = jnp.concatenate([h, jnp.zeros((1, d), h.dtype)], 0)

    def expert_block(i, acc):
        tok = slot_tok[i]
        e = block_e[i]
        y = swiglu(h_pad[tok], w_gate[e], w_up[e], w_down[e])
        return acc.at[tok].add(y.astype(jnp.float32) * slot_w[i][:, None])

    acc = lax.fori_loop(0, n_blocks, expert_block, jnp.zeros((n + 1, d), jnp.float32))
    return (acc[:n] + swiglu(h, sh_gate, sh_up, sh_down).astype(jnp.float32)).astype(h.dtype)


def setup_inputs(seed: int = 0) -> dict:
    keys = list(jax.random.split(jax.random.key(seed), 40))
    f32 = jnp.float32

    def nrm(shape, scale=1.0):
        return jax.random.normal(keys.pop(), shape, f32) * scale

    n_pages = PAST_LEN // PAGE_SIZE
    n_pool = (DEC_BATCH * n_pages * 5 + 3) // 4
    win_rows = min(WINDOW, PAST_LEN)
    x_prompt = nrm((BATCH, SEQ, D_MODEL))
    x_sample = nrm((DEC_BATCH, DEC_SEQ, D_MODEL))
    cache_kv_cmp = nrm((DEPTH, n_pool, PAGE_SIZE, 2, NSA_KV_HEADS, HEAD_DIM))
    cache_kv_sel = nrm((DEPTH, n_pool, PAGE_SIZE, 2, NSA_KV_HEADS, HEAD_DIM))
    page_table = jax.random.permutation(keys.pop(), n_pool)[:DEC_BATCH * n_pages].reshape(DEC_BATCH, n_pages).astype(jnp.int32)
    cache_kv_win = nrm((DEPTH, DEC_BATCH, win_rows, 2, NSA_KV_HEADS, HEAD_DIM))
    state_ssm = nrm((DEPTH, DEC_BATCH, SSD_HEADS, HEAD_DIM, SSD_STATE), 0.5)
    state_conv = nrm((DEPTH, DEC_BATCH, SSD_CONV - 1, SSD_CONV_CH))
    dt0 = jnp.exp(jax.random.uniform(keys.pop(), (DEPTH, SSD_HEADS), f32, math.log(1e-3), math.log(1e-1)))
    a0 = jax.random.uniform(keys.pop(), (DEPTH, SSD_HEADS), f32, 1.0, 16.0)
    return {
        'x_prompt': x_prompt,
        'x_sample': x_sample,
        'cache_kv_cmp': cache_kv_cmp,
        'cache_kv_sel': cache_kv_sel,
        'page_table': page_table,
        'cache_kv_win': cache_kv_win,
        'state_ssm': state_ssm,
        'state_conv': state_conv,
        'emb_ln_g': 1.0 + nrm((D_MODEL,), 0.02),
        'emb_ln_b': nrm((D_MODEL,), 0.01),
        'w_in': nrm((DEPTH, D_MODEL, D_IN), D_MODEL ** -0.5),
        'conv_w': nrm((DEPTH, SSD_CONV, SSD_CONV_CH), SSD_CONV ** -0.5),
        'conv_b': nrm((DEPTH, SSD_CONV_CH), 0.01),
        'dt_bias': dt0 + jnp.log(-jnp.expm1(-dt0)),
        'a_log': jnp.log(a0),
        'd_skip': 1.0 + nrm((DEPTH, SSD_HEADS), 0.1),
        'ssd_norm_w': 1.0 + nrm((DEPTH, SSD_D), 0.02),
        'cmp_pe': nrm((DEPTH, 2, CMP_BLOCK, HEAD_DIM), 0.1),
        'cmp_w1': nrm((DEPTH, 2, CMP_BLOCK, HEAD_DIM, CMP_HIDDEN), (CMP_BLOCK * HEAD_DIM) ** -0.5),
        'cmp_b1': nrm((DEPTH, 2, CMP_HIDDEN), 0.01),
        'cmp_w2': nrm((DEPTH, 2, CMP_HIDDEN, HEAD_DIM), CMP_HIDDEN ** -0.5),
        'cmp_b2': nrm((DEPTH, 2, HEAD_DIM), 0.01),
        'w_out': nrm((DEPTH, D_MIX, D_MODEL), DEEPNORM_BETA * D_MIX ** -0.5),
        'ln1_g': 1.0 + nrm((DEPTH, D_MODEL), 0.02),
        'ln1_b': nrm((DEPTH, D_MODEL), 0.01),
        'router_w': nrm((DEPTH, D_MODEL, N_EXPERTS), D_MODEL ** -0.5),
        'router_bias': nrm((DEPTH, N_EXPERTS), 0.01),
        'exp_w_gate': nrm((DEPTH, N_EXPERTS, D_MODEL, D_EXPERT), D_MODEL ** -0.5),
        'exp_w_up': nrm((DEPTH, N_EXPERTS, D_MODEL, D_EXPERT), D_MODEL ** -0.5),
        'exp_w_down': nrm((DEPTH, N_EXPERTS, D_EXPERT, D_MODEL), DEEPNORM_BETA * D_EXPERT ** -0.5),
        'sh_w_gate': nrm((DEPTH, D_MODEL, D_SHARED), D_MODEL ** -0.5),
        'sh_w_up': nrm((DEPTH, D_MODEL, D_SHARED), D_MODEL ** -0.5),
        'sh_w_down': nrm((DEPTH, D_SHARED, D_MODEL), DEEPNORM_BETA * D_SHARED ** -0.5),
        'ln2_g': 1.0 + nrm((DEPTH, D_MODEL), 0.02),
        'ln2_b': nrm((DEPTH, D_MODEL), 0.01),
    }


def reference(x_prompt, x_sample, cache_kv_cmp, cache_kv_sel, page_table, cache_kv_win, state_ssm, state_conv,
              emb_ln_g, emb_ln_b, w_in, conv_w, conv_b, dt_bias, a_log, d_skip, ssd_norm_w,
              cmp_pe, cmp_w1, cmp_b1, cmp_w2, cmp_b2, w_out, ln1_g, ln1_b,
              router_w, router_bias, exp_w_gate, exp_w_up, exp_w_down,
              sh_w_gate, sh_w_up, sh_w_down, ln2_g, ln2_b):
    hp = layer_norm(x_prompt, emb_ln_g, emb_ln_b)
    hs = layer_norm(x_sample, emb_ln_g, emb_ln_b)
    bp, tp = hp.shape[0], hp.shape[1]
    bs, ts = hs.shape[0], hs.shape[1]
    n_prompt = bp * tp
    kvc_p, kvs_p, kvw_p, ssm_p, conv_p = [], [], [], [], []
    kvc_s, kvs_s, kvw_s, ssm_s, conv_s = [], [], [], [], []
    for l in range(DEPTH):
        cmp_args = (cmp_pe[l], cmp_w1[l], cmp_b1[l], cmp_w2[l], cmp_b2[l])
        ssd_args = (conv_w[l], conv_b[l], dt_bias[l], a_log[l], d_skip[l], ssd_norm_w[l])
        z, xbc, dt, q, kc, vc, ks, vs, kw, vw, g = in_proj_split(hp, w_in[l])
        y_ssd, h_new, c_new = ssd_mixer(z, xbc, dt, jnp.zeros((bp, SSD_CONV - 1, SSD_CONV_CH), xbc.dtype),
                                        jnp.zeros((bp, SSD_HEADS, HEAD_DIM, SSD_STATE), jnp.float32), *ssd_args)
        y_nsa, n_cmp, n_sel, n_win = nsa_prompt(to_heads(q, NSA_HEADS), to_heads(kc, NSA_KV_HEADS), to_heads(vc, NSA_KV_HEADS),
                                                to_heads(ks, NSA_KV_HEADS), to_heads(vs, NSA_KV_HEADS),
                                                to_heads(kw, NSA_KV_HEADS), to_heads(vw, NSA_KV_HEADS), g, cmp_args)
        mix = jnp.einsum('btf,fd->btd', jnp.concatenate([y_ssd, y_nsa], -1), w_out[l])
        hp = layer_norm(DEEPNORM_ALPHA * hp + mix, ln1_g[l], ln1_b[l])
        kvc_p.append(n_cmp)
        kvs_p.append(n_sel)
        kvw_p.append(n_win)
        ssm_p.append(h_new)
        conv_p.append(c_new)
        z, xbc, dt, q, kc, vc, ks, vs, kw, vw, g = in_proj_split(hs, w_in[l])
        y_ssd, h_new, c_new = ssd_mixer(z, xbc, dt, state_conv[l], state_ssm[l], *ssd_args)
        y_nsa, n_cmp, n_sel, n_win = nsa_sample(to_heads(q, NSA_HEADS), to_heads(kc, NSA_KV_HEADS), to_heads(vc, NSA_KV_HEADS),
                                                to_heads(ks, NSA_KV_HEADS), to_heads(vs, NSA_KV_HEADS),
                                                to_heads(kw, NSA_KV_HEADS), to_heads(vw, NSA_KV_HEADS), g,
                                                cache_kv_cmp[l], cache_kv_sel[l], page_table, cache_kv_win[l], cmp_args)
        mix = jnp.einsum('btf,fd->btd', jnp.concatenate([y_ssd, y_nsa], -1), w_out[l])
        hs = layer_norm(DEEPNORM_ALPHA * hs + mix, ln1_g[l], ln1_b[l])
        kvc_s.append(n_cmp)
        kvs_s.append(n_sel)
        kvw_s.append(n_win)
        ssm_s.append(h_new)
        conv_s.append(c_new)
        tok = jnp.concatenate([hp.reshape(n_prompt, D_MODEL), hs.reshape(bs * ts, D_MODEL)], 0)
        f = moe_ffn(tok, router_w[l], router_bias[l], exp_w_gate[l], exp_w_up[l], exp_w_down[l],
                    sh_w_gate[l], sh_w_up[l], sh_w_down[l])
        hp = layer_norm(DEEPNORM_ALPHA * hp + f[:n_prompt].reshape(hp.shape), ln2_g[l], ln2_b[l])
        hs = layer_norm(DEEPNORM_ALPHA * hs + f[n_prompt:].reshape(hs.shape), ln2_g[l], ln2_b[l])
    return (hp, hs, jnp.stack(kvc_p), jnp.stack(kvs_p), jnp.stack(kvw_p), jnp.stack(ssm_p), jnp.stack(conv_p),
            jnp.stack(kvc_s), jnp.stack(kvs_s), jnp.stack(kvw_s), jnp.stack(ssm_s), jnp.stack(conv_s))
```

```python
import functools
import math

import jax
import jax.numpy as jnp
import numpy as np
from jax import lax
from jax.experimental import pallas as pl
from jax.experimental.pallas import tpu as pltpu

D_MODEL = 1024
HEAD_DIM = 64
SSD_HEADS = 8
SSD_D = SSD_HEADS * HEAD_DIM
SSD_GROUPS = 2
SSD_STATE = 128
SSD_CONV = 4
SSD_CONV_CH = SSD_D + 2 * SSD_GROUPS * SSD_STATE
SSD_CHUNK = 128
NSA_HEADS = 8
NSA_KV_HEADS = 2
NSA_D = NSA_HEADS * HEAD_DIM
KV_D = NSA_KV_HEADS * HEAD_DIM
CMP_BLOCK = 64
CMP_HIDDEN = 128
SEL_BLOCK = 64
TOP_N = 16
WINDOW = 512
Q_BLOCK = 128
ROT_DIM = HEAD_DIM // 4
ROPE_THETA = 500000.0
N_EXPERTS = 64
TOP_K = 6
N_EXPERT_GROUPS = 8
EXPERTS_PER_GROUP = N_EXPERTS // N_EXPERT_GROUPS
TOPK_GROUPS = 4
D_EXPERT = 256
D_SHARED = 256
ROUTED_SCALE = 2.5
MOE_BLOCK = 128
DEPTH = 1
DEEPNORM_ALPHA = (2.0 * DEPTH) ** 0.25
LN_EPS = 1e-5
RMS_EPS = 1e-5
NEG = -1e30
FORCED_SCORE = 1e4
PAGE_SIZE = 128

LANES = 128
SUBLANES = 8
VMEM_LIMIT_BYTES = 56 * 1024 * 1024

U_Z = 0
U_XBC = U_Z + SSD_D
U_Q = U_XBC + SSD_CONV_CH
U_KVC = U_Q + NSA_D
U_KVS = U_KVC + 2 * KV_D
U_KVW = U_KVS + 2 * KV_D
U_DTG = U_KVW + 2 * KV_D
U_TOTAL = U_DTG + LANES
GATE_COL0 = SSD_HEADS

BF16 = jnp.bfloat16
F32 = jnp.float32


def _cparams(sem):
    return pltpu.CompilerParams(dimension_semantics=sem, vmem_limit_bytes=VMEM_LIMIT_BYTES)


def _bdot(a, b):
    return jnp.dot(a.astype(BF16), b.astype(BF16), preferred_element_type=F32)


def _bdot_nt(a, b):
    return lax.dot_general(a.astype(BF16), b.astype(BF16), (((1,), (1,)), ((), ())),
                           preferred_element_type=F32)


def _hdot(a, b):
    return jnp.dot(a, b, preferred_element_type=F32, precision=lax.Precision.HIGHEST)


def _sigmoid(x):
    return 1.0 / (1.0 + jnp.exp(-x))


def _silu(x):
    return x * _sigmoid(x)


def _layer_norm(x, g, b):
    mu = jnp.mean(x, axis=-1, keepdims=True)
    xc = x - mu
    var = jnp.mean(xc * xc, axis=-1, keepdims=True)
    return xc * lax.rsqrt(var + LN_EPS) * g + b


def _rope_tile(x, cos, sa, sb):
    return x * cos + pltpu.roll(x, LANES - ROT_DIM // 2, 1) * sa + pltpu.roll(x, ROT_DIM // 2, 1) * sb


def _inproj_kernel(x_ref, g_ref, b_ref, w_ref, rope_ref,
                   h_ref, z_ref, xbc_ref, qc_ref, qr_ref, kvc_ref, kvs_ref, kvw_ref, dtg_ref):
    h = _layer_norm(x_ref[...], g_ref[...], b_ref[...])
    h_ref[...] = h
    u = jnp.dot(h.astype(BF16), w_ref[...], preferred_element_type=F32)
    cos = rope_ref[:, 0:LANES]
    sa = rope_ref[:, LANES:2 * LANES]
    sb = rope_ref[:, 2 * LANES:3 * LANES]
    z_ref[...] = u[:, U_Z:U_XBC]
    xbc_ref[...] = u[:, U_XBC:U_Q]
    qc_ref[...] = u[:, U_Q:U_KVC]
    for c in range(NSA_D // LANES):
        qr_ref[:, c * LANES:(c + 1) * LANES] = _rope_tile(u[:, U_Q + c * LANES:U_Q + (c + 1) * LANES], cos, sa, sb)
    kvc_ref[...] = u[:, U_KVC:U_KVS]
    kvs_ref[:, 0:KV_D] = _rope_tile(u[:, U_KVS:U_KVS + KV_D], cos, sa, sb)
    kvs_ref[:, KV_D:2 * KV_D] = u[:, U_KVS + KV_D:U_KVW]
    kvw_ref[:, 0:KV_D] = _rope_tile(u[:, U_KVW:U_KVW + KV_D], cos, sa, sb)
    kvw_ref[:, KV_D:2 * KV_D] = u[:, U_KVW + KV_D:U_DTG]
    dtg_ref[...] = u[:, U_DTG:U_TOTAL]


def _rope_tables(pos):
    half = ROT_DIM // 2
    inv = ROPE_THETA ** (-jnp.arange(half, dtype=F32) / half)
    ang = pos.astype(F32)[:, None] * inv
    cos, sin = jnp.cos(ang), jnp.sin(ang)
    ones = jnp.ones((pos.shape[0], HEAD_DIM - ROT_DIM), F32)
    zeros = jnp.zeros((pos.shape[0], HEAD_DIM - ROT_DIM), F32)
    zh = jnp.zeros_like(sin)
    c = jnp.concatenate([cos, cos, ones], 1)
    sa = jnp.concatenate([-sin, zh, zeros], 1)
    sb = jnp.concatenate([zh, sin, zeros], 1)
    return jnp.concatenate([jnp.tile(t, (1, LANES // HEAD_DIM)) for t in (c, sa, sb)], 1)


def _permute_w_in(w):
    sizes = (SSD_D, SSD_CONV_CH, SSD_HEADS, NSA_D, KV_D, KV_D, KV_D, KV_D, KV_D, KV_D, 3 * NSA_HEADS)
    offs = np.concatenate([[0], np.cumsum(sizes)])
    seg = [w[:, offs[i]:offs[i + 1]] for i in range(len(sizes))]
    pad = jnp.zeros((w.shape[0], LANES - SSD_HEADS - 3 * NSA_HEADS), w.dtype)
    out = jnp.concatenate([seg[0], seg[1], seg[3], seg[4], seg[5], seg[6], seg[7], seg[8], seg[9],
                           seg[2], seg[10], pad], 1)
    return out.astype(BF16)


def _inproj(x, ln_g, ln_b, w_perm, rope_tab, tm):
    n = x.shape[0]
    nt = n // tm
    n_rope_blocks = rope_tab.shape[0] // tm
    row = lambda w: pl.BlockSpec((tm, w), lambda i: (i, 0))
    const = lambda a: pl.BlockSpec(a.shape, lambda i: (0,) * a.ndim)
    widths = (D_MODEL, SSD_D, SSD_CONV_CH, NSA_D, NSA_D, 2 * KV_D, 2 * KV_D, 2 * KV_D, LANES)
    return pl.pallas_call(
        _inproj_kernel,
        grid=(nt,),
        in_specs=[row(D_MODEL), const(ln_g), const(ln_b), const(w_perm),
                  pl.BlockSpec((tm, 3 * LANES), lambda i: (i % n_rope_blocks, 0))],
        out_specs=[row(w) for w in widths],
        out_shape=[jax.ShapeDtypeStruct((n, w), F32) for w in widths],
        compiler_params=_cparams(("parallel",)),
        name="inproj",
    )(x, ln_g, ln_b, w_perm, rope_tab)


def _softplus(x):
    return jnp.maximum(x, 0.0) + jnp.log1p(jnp.exp(-jnp.abs(x)))


def _gated_group_norm(y, z, norm_w):
    y = y * _silu(z)
    gw = SSD_D // SSD_GROUPS
    parts = []
    for g in range(SSD_GROUPS):
        yg = y[:, g * gw:(g + 1) * gw]
        ms = jnp.mean(yg * yg, axis=-1, keepdims=True)
        parts.append(yg * lax.rsqrt(ms + RMS_EPS))
    return jnp.concatenate(parts, axis=1) * norm_w


def _ssd_prompt_kernel(xbc_ref, z_ref, dtg_ref, convw_ref, convb_ref, dtb_ref, alog_ref, dskip_ref, normw_ref,
                       y_ref, state_ref, conv_ref, ext_ref, s_ref):
    c = pl.program_id(1)
    nc = pl.num_programs(1)
    L = SSD_CHUNK
    halo = SUBLANES

    @pl.when(c == 0)
    def _():
        ext_ref[0:halo, :] = jnp.zeros((halo, SSD_CONV_CH), F32)
        s_ref[...] = jnp.zeros_like(s_ref)

    xin = xbc_ref[...]
    ext_ref[halo:halo + L, :] = xin
    xc = convw_ref[SSD_CONV - 1:SSD_CONV, :] * xin
    for k in range(SSD_CONV - 1):
        off = halo - (SSD_CONV - 1) + k
        xc = xc + convw_ref[k:k + 1, :] * ext_ref[off:off + L, :]
    ext_ref[0:halo, :] = ext_ref[L:L + halo, :]
    xc = _silu(xc + convb_ref[...])
    xs = xc[:, 0:SSD_D]
    ns = SSD_GROUPS * SSD_STATE
    bm = xc[:, SSD_D:SSD_D + ns]
    cm = xc[:, SSD_D + ns:SSD_D + 2 * ns]

    dt = _softplus(dtg_ref[...] + dtb_ref[...])
    da = dt * (-jnp.exp(alog_ref[...]))
    row = lax.broadcasted_iota(jnp.int32, (L, L), 0)
    col = lax.broadcasted_iota(jnp.int32, (L, L), 1)
    tril = row >= col
    acum = _hdot(tril.astype(F32), da)
    acum_t = acum.T
    eacum = jnp.exp(acum)
    alast = acum[L - 1:L, :]
    edecay = jnp.exp(alast - acum)
    elast = jnp.exp(alast)

    dt_full = jnp.concatenate([jnp.broadcast_to(dt[:, h:h + 1], (L, HEAD_DIM)) for h in range(SSD_HEADS)], 1)
    dec_full = jnp.concatenate([jnp.broadcast_to(edecay[:, h:h + 1], (L, HEAD_DIM)) for h in range(SSD_HEADS)], 1)
    xdt = xs * dt_full
    xdec_t = (xdt * dec_full).T

    hpg = SSD_HEADS // SSD_GROUPS
    y_parts = []
    for h in range(SSD_HEADS):
        g = h // hpg
        b_g = bm[:, g * SSD_STATE:(g + 1) * SSD_STATE]
        c_g = cm[:, g * SSD_STATE:(g + 1) * SSD_STATE]
        if h % hpg == 0:
            cb = _bdot_nt(c_g, b_g)
        seg = acum[:, h:h + 1] - acum_t[h:h + 1, :]
        lmat = jnp.where(tril, jnp.exp(jnp.where(tril, seg, 0.0)), 0.0)
        xdt_h = xdt[:, h * HEAD_DIM:(h + 1) * HEAD_DIM]
        y_h = _bdot(cb * lmat, xdt_h)
        s_prev = s_ref[h]
        y_h = y_h + _bdot_nt(c_g, s_prev) * eacum[:, h:h + 1]
        y_h = y_h + dskip_ref[:, h * HEAD_DIM:(h + 1) * HEAD_DIM] * xs[:, h * HEAD_DIM:(h + 1) * HEAD_DIM]
        y_parts.append(y_h)
        s_ref[h] = elast[:, h:h + 1] * s_prev + _bdot(xdec_t[h * HEAD_DIM:(h + 1) * HEAD_DIM, :], b_g)
    y = jnp.concatenate(y_parts, axis=1)
    y_ref[...] = _gated_group_norm(y, z_ref[...], normw_ref[...])

    @pl.when(c == nc - 1)
    def _():
        state_ref[0] = s_ref[...]
        conv_ref[0] = xin[L - (SSD_CONV - 1):L, :]


def _ssd_prompt(xbc, z, dtg, conv_w, conv_b, dt_bias_pad, a_log_pad, d_skip_full, norm_w, bn, t):
    nc = t // SSD_CHUNK
    row = lambda w: pl.BlockSpec((SSD_CHUNK, w), lambda b, c: (b * nc + c, 0))
    const = lambda a: pl.BlockSpec(a.shape, lambda b, c: (0,) * a.ndim)
    return pl.pallas_call(
        _ssd_prompt_kernel,
        grid=(bn, nc),
        in_specs=[row(SSD_CONV_CH), row(SSD_D), row(LANES), const(conv_w), const(conv_b), const(dt_bias_pad),
                  const(a_log_pad), const(d_skip_full), const(norm_w)],
        out_specs=[row(SSD_D),
                   pl.BlockSpec((1, SSD_HEADS, HEAD_DIM, SSD_STATE), lambda b, c: (b, 0, 0, 0)),
                   pl.BlockSpec((1, SSD_CONV - 1, SSD_CONV_CH), lambda b, c: (b, 0, 0))],
        out_shape=[jax.ShapeDtypeStruct((bn * t, SSD_D), F32),
                   jax.ShapeDtypeStruct((bn, SSD_HEADS, HEAD_DIM, SSD_STATE), F32),
                   jax.ShapeDtypeStruct((bn, SSD_CONV - 1, SSD_CONV_CH), F32)],
        scratch_shapes=[pltpu.VMEM((SSD_CHUNK + 2 * SUBLANES, SSD_CONV_CH), F32),
                        pltpu.VMEM((SSD_HEADS, HEAD_DIM, SSD_STATE), F32)],
        compiler_params=_cparams(("parallel", "arbitrary")),
        name="ssd_prompt",
    )(xbc, z, dtg, conv_w, conv_b, dt_bias_pad, a_log_pad, d_skip_full, norm_w)


def _pad_lanes(v, fill=0.0):
    return jnp.concatenate([v.astype(F32), jnp.full((LANES - v.shape[0],), fill, F32)])[None]


def _compress_rows(k_ref, v_ref, pe_ref, w1k_ref, w1v_ref, b1_ref, w2k_ref, w2v_ref, b2_ref, nb):
    acck = jnp.zeros((nb, 2 * CMP_HIDDEN), F32)
    accv = jnp.zeros((nb, 2 * CMP_HIDDEN), F32)
    for l in range(CMP_BLOCK):
        xk = k_ref[pl.ds(l, nb, stride=CMP_BLOCK), :] + pe_ref[l:l + 1, 0:KV_D]
        xv = v_ref[pl.ds(l, nb, stride=CMP_BLOCK), :] + pe_ref[l:l + 1, KV_D:2 * KV_D]
        acck = acck + jnp.dot(xk.astype(BF16), w1k_ref[l], preferred_element_type=F32)
        accv = accv + jnp.dot(xv.astype(BF16), w1v_ref[l], preferred_element_type=F32)
    hk = _silu(acck + b1_ref[:, 0:2 * CMP_HIDDEN])
    hv = _silu(accv + b1_ref[:, 2 * CMP_HIDDEN:4 * CMP_HIDDEN])
    ok = jnp.dot(hk.astype(BF16), w2k_ref[...], preferred_element_type=F32) + b2_ref[:, 0:KV_D]
    ov = jnp.dot(hv.astype(BF16), w2v_ref[...], preferred_element_type=F32) + b2_ref[:, KV_D:2 * KV_D]
    return jnp.concatenate([ok, ov], axis=1)


def _compress_kernel(k_ref, v_ref, pe_ref, w1k_ref, w1v_ref, b1_ref, w2k_ref, w2v_ref, b2_ref, o_ref, *, nb):
    o_ref[...] = _compress_rows(k_ref, v_ref, pe_ref, w1k_ref, w1v_ref, b1_ref, w2k_ref, w2v_ref, b2_ref, nb)


def _block_diag2(w):
    z = jnp.zeros_like(w)
    return jnp.concatenate([jnp.concatenate([w, z], -1), jnp.concatenate([z, w], -1)], -2)


def _compress_consts(cmp_pe, cmp_w1, cmp_b1, cmp_w2, cmp_b2):
    pe = jnp.concatenate([cmp_pe[0], cmp_pe[0], cmp_pe[1], cmp_pe[1]], -1)
    w1k = _block_diag2(cmp_w1[0]).astype(BF16)
    w1v = _block_diag2(cmp_w1[1]).astype(BF16)
    b1 = jnp.concatenate([cmp_b1[0], cmp_b1[0], cmp_b1[1], cmp_b1[1]])[None]
    w2k = _block_diag2(cmp_w2[0]).astype(BF16)
    w2v = _block_diag2(cmp_w2[1]).astype(BF16)
    b2 = jnp.concatenate([cmp_b2[0], cmp_b2[0], cmp_b2[1], cmp_b2[1]])[None]
    return pe, w1k, w1v, b1, w2k, w2v, b2


def _compress_prompt(kvc, consts, rows_per_step):
    n = kvc.shape[0]
    nb = rows_per_step // CMP_BLOCK
    const = lambda a: pl.BlockSpec(a.shape, lambda i: (0,) * a.ndim)
    return pl.pallas_call(
        functools.partial(_compress_kernel, nb=nb),
        grid=(n // rows_per_step,),
        in_specs=[pl.BlockSpec((rows_per_step, KV_D), lambda i: (i, 0)),
                  pl.BlockSpec((rows_per_step, KV_D), lambda i: (i, 1))] + [const(a) for a in consts],
        out_specs=pl.BlockSpec((nb, 2 * KV_D), lambda i: (i, 0)),
        out_shape=jax.ShapeDtypeStruct((n // CMP_BLOCK, 2 * KV_D), F32),
        compiler_params=_cparams(("parallel",)),
        name="compress_prompt",
    )(kvc, kvc, *consts)


SEL_KEY_TILE = 256
WIN_KEYS = WINDOW + Q_BLOCK


def _swap_heads(x):
    return jnp.concatenate([pltpu.roll(x[:, 0:KV_D], HEAD_DIM, 1), pltpu.roll(x[:, KV_D:2 * KV_D], HEAD_DIM, 1)], 1)


def _masked_softmax(s, mask):
    sm = jnp.where(mask, s, NEG)
    ex = jnp.where(mask, jnp.exp(sm - jnp.max(sm, axis=-1, keepdims=True)), 0.0)
    den = jnp.sum(ex, axis=-1, keepdims=True)
    return ex / jnp.where(den > 0.0, den, 1.0)


def _select_blocks(imp, cur, n_top):
    n_cand = imp.shape[1]
    j = lax.broadcasted_iota(jnp.int32, imp.shape, 1)
    future = j > cur
    forced = (j == 0) | (j == cur) | (j == cur - 1)
    score = jnp.where(future, NEG, jnp.where(forced, FORCED_SCORE, imp))
    rank = jnp.zeros(imp.shape, F32)
    for c in range(n_cand):
        col = score[:, c:c + 1]
        beats = (col > score) | ((col == score) & (j > c))
        rank = rank + beats.astype(F32)
    return ((rank < n_top) & (score > 0.5 * NEG)).astype(F32)


def _nsa_prompt_kernel(qc_ref, qr_ref, dtg_ref, cmp_ref, kvs_ref, kvw_ref, o_ref,
                       cmp_b, kvs_b, kvw_b, selk_ref, *, t):
    qb = pl.program_id(1)
    nbk = t // SEL_BLOCK
    tq = Q_BLOCK
    tk = SEL_KEY_TILE
    scale = HEAD_DIM ** -0.5

    @pl.when(qb == 0)
    def _():
        for src, dst in ((cmp_ref, cmp_b), (kvs_ref, kvs_b), (kvw_ref, kvw_b)):
            x = src[...]
            dst[0] = x.astype(BF16)
            dst[1] = _swap_heads(x).astype(BF16)

    t0 = qb * tq
    rows = t0 + lax.broadcasted_iota(jnp.int32, (tq, 1), 0)
    lane = lax.broadcasted_iota(jnp.int32, (tq, LANES), 1)
    half_mask = (lane < HEAD_DIM, lane >= HEAD_DIM)
    sig = _sigmoid(dtg_ref[...])
    cidx = lax.broadcasted_iota(jnp.int32, (tq, nbk), 1)
    vis = (cidx + 1) * CMP_BLOCK - 1 <= rows
    expand = (lax.broadcasted_iota(jnp.int32, (nbk, t), 1) // SEL_BLOCK
              == lax.broadcasted_iota(jnp.int32, (nbk, t), 0)).astype(BF16)
    win_start = pl.multiple_of(jnp.maximum(t0 - WINDOW, 0), tq)
    wpos = win_start + lax.broadcasted_iota(jnp.int32, (tq, WIN_KEYS), 1)
    win_mask = (wpos <= rows) & (wpos >= rows - WINDOW)
    n_kt = (t0 + tq + tk - 1) // tk
    hpg = NSA_HEADS // NSA_KV_HEADS

    for hk in range(NSA_KV_HEADS):
        imp = jnp.zeros((tq, nbk), F32)
        o_cmp = []
        for hh in range(hpg):
            head = hk * hpg + hh
            p, e = head // 2, head % 2
            v = 0 if e == hk else 1
            q = jnp.where(half_mask[e], qc_ref[:, p * LANES:(p + 1) * LANES], 0.0).astype(BF16)
            s = lax.dot_general(q, cmp_b[v, :, 0:KV_D], (((1,), (1,)), ((), ())), preferred_element_type=F32) * scale
            pc = _masked_softmax(s, vis)
            imp = imp + pc
            o_cmp.append(jnp.dot(pc.astype(BF16), cmp_b[v, :, KV_D:2 * KV_D], preferred_element_type=F32))
        sel = _select_blocks(imp, rows // SEL_BLOCK, TOP_N)
        selk = jnp.dot(sel.astype(BF16), expand, preferred_element_type=F32)
        for kt in range(t // tk):
            selk_ref[kt] = selk[:, kt * tk:(kt + 1) * tk]

        for hh in range(hpg):
            head = hk * hpg + hh
            p, e = head // 2, head % 2
            v = 0 if e == hk else 1
            q = jnp.where(half_mask[e], qr_ref[:, p * LANES:(p + 1) * LANES], 0.0).astype(BF16)

            def sel_step(kt, carry):
                m, l, acc = carry
                k0 = pl.multiple_of(kt * tk, tk)
                kblk = kvs_b[v, pl.ds(k0, tk), 0:KV_D]
                vblk = kvs_b[v, pl.ds(k0, tk), KV_D:2 * KV_D]
                s = lax.dot_general(q, kblk, (((1,), (1,)), ((), ())), preferred_element_type=F32) * scale
                kpos = k0 + lax.broadcasted_iota(jnp.int32, (tq, tk), 1)
                msk = (selk_ref[kt] > 0.5) & (kpos <= rows)
                sm = jnp.where(msk, s, NEG)
                m_new = jnp.maximum(m, jnp.max(sm, axis=-1, keepdims=True))
                alpha = jnp.exp(m - m_new)
                pe = jnp.where(msk, jnp.exp(sm - m_new), 0.0)
                l = alpha * l + jnp.sum(pe, axis=-1, keepdims=True)
                acc = alpha * acc + jnp.dot(pe.astype(BF16), vblk, preferred_element_type=F32)
                return m_new, l, acc

            m0 = jnp.full((tq, 1), NEG, F32)
            l0 = jnp.zeros((tq, 1), F32)
            a0 = jnp.zeros((tq, LANES), F32)
            _, l_f, acc_f = lax.fori_loop(0, n_kt, sel_step, (m0, l0, a0))
            o_slc = acc_f / l_f

            kw = kvw_b[v, pl.ds(win_start, WIN_KEYS), 0:KV_D]
            vw = kvw_b[v, pl.ds(win_start, WIN_KEYS), KV_D:2 * KV_D]
            sw = lax.dot_general(q, kw, (((1,), (1,)), ((), ())), preferred_element_type=F32) * scale
            pw = _masked_softmax(sw, win_mask)
            o_win = jnp.dot(pw.astype(BF16), vw, preferred_element_type=F32)

            c0 = GATE_COL0 + head * 3
            mix = sig[:, c0:c0 + 1] * o_cmp[hh] + sig[:, c0 + 1:c0 + 2] * o_slc + sig[:, c0 + 2:c0 + 3] * o_win
            if e == 0:
                mix_even = mix
            else:
                o_ref[:, p * LANES:(p + 1) * LANES] = jnp.where(half_mask[0], mix_even, mix)


def _nsa_prompt(qc, qr, dtg, kvcmp, kvs, kvw, bn, t):
    nq = t // Q_BLOCK
    nbk = t // SEL_BLOCK
    assert nbk >= TOP_N and t >= WIN_KEYS and t % SEL_KEY_TILE == 0
    qrow = lambda w: pl.BlockSpec((Q_BLOCK, w), lambda b, i: (b * nq + i, 0))
    seq = lambda r: pl.BlockSpec((r, 2 * KV_D), lambda b, i: (b, 0))
    return pl.pallas_call(
        functools.partial(_nsa_prompt_kernel, t=t),
        grid=(bn, nq),
        in_specs=[qrow(NSA_D), qrow(NSA_D), qrow(LANES), seq(nbk), seq(t), seq(t)],
        out_specs=qrow(NSA_D),
        out_shape=jax.ShapeDtypeStruct((bn * t, NSA_D), F32),
        scratch_shapes=[pltpu.VMEM((2, nbk, 2 * KV_D), BF16),
                        pltpu.VMEM((2, t, 2 * KV_D), BF16),
                        pltpu.VMEM((2, t, 2 * KV_D), BF16),
                        pltpu.VMEM((t // SEL_KEY_TILE, Q_BLOCK, SEL_KEY_TILE), F32)],
        compiler_params=_cparams(("parallel", "arbitrary")),
        name="nsa_prompt",
    )(qc, qr, dtg, kvcmp, kvs, kvw)


def _outproj_kernel(ys_ref, yn_ref, h_ref, ws_ref, wn_ref, g_ref, b_ref, o_ref):
    mix = jnp.dot(ys_ref[...].astype(BF16), ws_ref[...], preferred_element_type=F32)
    mix = mix + jnp.dot(yn_ref[...].astype(BF16), wn_ref[...], preferred_element_type=F32)
    o_ref[...] = _layer_norm(DEEPNORM_ALPHA * h_ref[...] + mix, g_ref[...], b_ref[...])


def _outproj(y_ssd, y_nsa, h, w_ssd, w_nsa, ln_g, ln_b, tm):
    n = h.shape[0]
    row = lambda w: pl.BlockSpec((tm, w), lambda i: (i, 0))
    const = lambda a: pl.BlockSpec(a.shape, lambda i: (0,) * a.ndim)
    return pl.pallas_call(
        _outproj_kernel,
        grid=(n // tm,),
        in_specs=[row(SSD_D), row(NSA_D), row(D_MODEL), const(w_ssd), const(w_nsa), const(ln_g), const(ln_b)],
        out_specs=row(D_MODEL),
        out_shape=jax.ShapeDtypeStruct((n, D_MODEL), F32),
        compiler_params=_cparams(("parallel",)),
        name="outproj",
    )(y_ssd, y_nsa, h, w_ssd, w_nsa, ln_g, ln_b)


MOE_TOKENS = 256
ROUTE_ROWS = 8


def _rank_rows(x):
    n = x.shape[0]
    idx = lax.broadcasted_iota(jnp.int32, x.shape, 0)
    rank = jnp.zeros(x.shape, F32)
    for r in range(n):
        row = x[r:r + 1, :]
        rank = rank + ((row > x) | ((row == x) & (idx > r))).astype(F32)
    return rank


def _route_kernel(h_ref, rw_ref, rb_ref, idx_ref, pos_ref, wtok_ref, cnt_ref, carry_ref, *, n_valid):
    i = pl.program_id(0)
    tm = MOE_TOKENS

    @pl.when(i == 0)
    def _():
        carry_ref[...] = jnp.zeros_like(carry_ref)

    logits = lax.dot_general(rw_ref[...], h_ref[...].astype(BF16), (((1,), (1,)), ((), ())),
                             preferred_element_type=F32)
    scores = _sigmoid(logits)
    biased = scores + rb_ref[:, 0:1]
    b3 = biased.reshape(N_EXPERT_GROUPS, EXPERTS_PER_GROUP, tm)
    sidx = lax.broadcasted_iota(jnp.int32, b3.shape, 1)
    m1 = jnp.max(b3, axis=1, keepdims=True)
    first = jnp.min(jnp.where(b3 == m1, sidx, EXPERTS_PER_GROUP), axis=1, keepdims=True)
    m2 = jnp.max(jnp.where(sidx == first, -jnp.inf, b3), axis=1, keepdims=True)
    grp_score = (m1 + m2).reshape(N_EXPERT_GROUPS, tm)
    grp_keep = _rank_rows(grp_score) < TOPK_GROUPS
    masked = jnp.where(grp_keep.reshape(N_EXPERT_GROUPS, 1, tm), b3, NEG).reshape(N_EXPERTS, tm)
    rank = _rank_rows(masked)
    tok = i * tm + lax.broadcasted_iota(jnp.int32, (1, tm), 1)
    valid = tok < n_valid
    sel = (rank < TOP_K) & valid
    self32 = sel.astype(F32)
    wsel = self32 * scores
    wsum = jnp.sum(wsel, axis=0, keepdims=True)
    w = wsel / jnp.where(wsum > 0.0, wsum, 1.0) * ROUTED_SCALE

    upper = (lax.broadcasted_iota(jnp.int32, (tm, tm), 0) < lax.broadcasted_iota(jnp.int32, (tm, tm), 1))
    pos = jnp.dot(sel.astype(BF16), upper.astype(BF16), preferred_element_type=F32) + carry_ref[:, 0:1]
    carry_ref[...] = carry_ref[...] + jnp.sum(self32, axis=1, keepdims=True)

    eidx = lax.broadcasted_iota(jnp.int32, (N_EXPERTS, tm), 0).astype(F32)
    idx_rows, pos_rows, w_rows = [], [], []
    for k in range(TOP_K):
        hit = (rank == k) & sel
        idx_rows.append(jnp.sum(jnp.where(hit, eidx, 0.0), axis=0, keepdims=True))
        pos_rows.append(jnp.sum(jnp.where(hit, pos, 0.0), axis=0, keepdims=True))
        w_rows.append(jnp.sum(jnp.where(hit, w, 0.0), axis=0, keepdims=True))
    zrow = jnp.zeros((ROUTE_ROWS - TOP_K, tm), F32)
    idx_ref[...] = jnp.concatenate(idx_rows + [zrow], 0).astype(jnp.int32)
    pos_ref[...] = jnp.concatenate(pos_rows + [zrow], 0).astype(jnp.int32)
    wtok_ref[...] = jnp.concatenate(w_rows + [jnp.zeros((LANES - TOP_K, tm), F32)], 0).T

    @pl.when(i == pl.num_programs(0) - 1)
    def _():
        cnt_ref[...] = jnp.broadcast_to(carry_ref[:, 0:1], cnt_ref.shape)


def _route(h, router_wt, router_bias_col, n_valid):
    n = h.shape[0]
    tm = MOE_TOKENS
    const = lambda a: pl.BlockSpec(a.shape, lambda i: (0,) * a.ndim)
    return pl.pallas_call(
        functools.partial(_route_kernel, n_valid=n_valid),
        grid=(n // tm,),
        in_specs=[pl.BlockSpec((tm, D_MODEL), lambda i: (i, 0)), const(router_wt), const(router_bias_col)],
        out_specs=[pl.BlockSpec((ROUTE_ROWS, tm), lambda i: (0, i)),
                   pl.BlockSpec((ROUTE_ROWS, tm), lambda i: (0, i)),
                   pl.BlockSpec((tm, LANES), lambda i: (i, 0)),
                   pl.BlockSpec((N_EXPERTS, LANES), lambda i: (0, 0))],
        out_shape=[jax.ShapeDtypeStruct((ROUTE_ROWS, n), jnp.int32),
                   jax.ShapeDtypeStruct((ROUTE_ROWS, n), jnp.int32),
                   jax.ShapeDtypeStruct((n, LANES), F32),
                   jax.ShapeDtypeStruct((N_EXPERTS, LANES), F32)],
        scratch_shapes=[pltpu.VMEM((N_EXPERTS, LANES), F32)],
        compiler_params=_cparams(("arbitrary",)),
        name="moe_route",
    )(h, router_wt, router_bias_col)


def _row_copy(src_ref, src_row, dst_ref, dst_row, sem):
    return pltpu.make_async_copy(src_ref.at[pl.ds(src_row, 1)], dst_ref.at[pl.ds(dst_row, 1)], sem)


def _dispatch_kernel(start_ref, cnt_ref, idx_ref, pos_ref, x_ref, xs_ref, zero_ref, sem, zsem, *, n_valid, cap):
    i = pl.program_id(0)
    tm = MOE_TOKENS
    ntok = jnp.minimum(tm, n_valid - i * tm)

    def scatter_row(t, carry):
        for k in range(TOP_K):
            dst = start_ref[idx_ref[k, t]] + pos_ref[k, t]
            _row_copy(x_ref, t, xs_ref, dst, sem).start()
        return carry

    lax.fori_loop(0, ntok, scatter_row, 0)

    @pl.when(i == 0)
    def _():
        zero_ref[...] = jnp.zeros_like(zero_ref)

        def fill_expert(e, total):
            lo = start_ref[e] + cnt_ref[e]
            hi = jnp.where(e == N_EXPERTS - 1, cap, start_ref[jnp.minimum(e + 1, N_EXPERTS - 1)])

            def fill_row(r, c):
                _row_copy(zero_ref, 0, xs_ref, r, zsem).start()
                return c

            lax.fori_loop(lo, hi, fill_row, 0)
            return total + (hi - lo)

        n_fill = lax.fori_loop(0, N_EXPERTS, fill_expert, 0)

        def wait_fill(r, c):
            _row_copy(zero_ref, 0, xs_ref, 0, zsem).wait()
            return c

        lax.fori_loop(0, n_fill, wait_fill, 0)

    def wait_row(t, carry):
        for k in range(TOP_K):
            _row_copy(x_ref, 0, xs_ref, 0, sem).wait()
        return carry

    lax.fori_loop(0, ntok, wait_row, 0)


def _dispatch(h, idx_t, pos_t, seg_start, counts, n_valid, cap):
    n = h.shape[0]
    tm = MOE_TOKENS
    smem_rows = pl.BlockSpec((ROUTE_ROWS, tm), lambda i, *_: (0, i), memory_space=pltpu.SMEM)
    return pl.pallas_call(
        functools.partial(_dispatch_kernel, n_valid=n_valid, cap=cap),
        grid_spec=pltpu.PrefetchScalarGridSpec(
            num_scalar_prefetch=2,
            grid=(pl.cdiv(n_valid, tm),),
            in_specs=[smem_rows, smem_rows, pl.BlockSpec((tm, D_MODEL), lambda i, *_: (i, 0))],
            out_specs=pl.BlockSpec(memory_space=pl.ANY),
            scratch_shapes=[pltpu.VMEM((SUBLANES, D_MODEL), F32), pltpu.SemaphoreType.DMA, pltpu.SemaphoreType.DMA]),
        out_shape=jax.ShapeDtypeStruct((cap, D_MODEL), F32),
        compiler_params=_cparams(("arbitrary",)),
        name="moe_dispatch",
    )(seg_start, counts, idx_t, pos_t, h)


def _swiglu(x, wg, wu, wd):
    xb = x.astype(BF16)
    g = jnp.dot(xb, wg.astype(BF16), preferred_element_type=F32)
    u = jnp.dot(xb, wu.astype(BF16), preferred_element_type=F32)
    return jnp.dot((_silu(g) * u).astype(BF16), wd.astype(BF16), preferred_element_type=F32)


def _experts_kernel(be_ref, x_ref, wg_ref, wu_ref, wd_ref, y_ref):
    y_ref[...] = _swiglu(x_ref[...], wg_ref[0], wu_ref[0], wd_ref[0])


def _experts(xs, block_expert, w_gate, w_up, w_down):
    cap = xs.shape[0]
    return pl.pallas_call(
        _experts_kernel,
        grid_spec=pltpu.PrefetchScalarGridSpec(
            num_scalar_prefetch=1,
            grid=(cap // MOE_BLOCK,),
            in_specs=[pl.BlockSpec((MOE_BLOCK, D_MODEL), lambda i, be: (i, 0)),
                      pl.BlockSpec((1, D_MODEL, D_EXPERT), lambda i, be: (be[i], 0, 0)),
                      pl.BlockSpec((1, D_MODEL, D_EXPERT), lambda i, be: (be[i], 0, 0)),
                      pl.BlockSpec((1, D_EXPERT, D_MODEL), lambda i, be: (be[i], 0, 0))],
            out_specs=pl.BlockSpec((MOE_BLOCK, D_MODEL), lambda i, be: (i, 0))),
        out_shape=jax.ShapeDtypeStruct((cap, D_MODEL), F32),
        compiler_params=_cparams(("arbitrary",)),
        name="moe_experts",
    )(block_expert, xs, w_gate, w_up, w_down)


def _combine_kernel(start_ref, idx_ref, pos_ref, h_ref, wtok_ref, sg_ref, su_ref, sd_ref, g_ref, b_ref, ys_ref,
                    o_ref, buf_ref, sem):
    tm = MOE_TOKENS

    def gather_row(t, carry):
        for k in range(TOP_K):
            src = start_ref[idx_ref[k, t]] + pos_ref[k, t]
            _row_copy(ys_ref, src, buf_ref.at[k], t, sem).start()
        return carry

    lax.fori_loop(0, tm, gather_row, 0)
    h = h_ref[...]
    f = _swiglu(h, sg_ref[...], su_ref[...], sd_ref[...])

    def wait_row(t, carry):
        for k in range(TOP_K):
            _row_copy(ys_ref, 0, buf_ref.at[k], 0, sem).wait()
        return carry

    lax.fori_loop(0, tm, wait_row, 0)
    wtok = wtok_ref[...]
    acc = jnp.zeros((tm, D_MODEL), F32)
    for k in range(TOP_K):
        acc = acc + wtok[:, k:k + 1] * buf_ref[k]
    o_ref[...] = _layer_norm(DEEPNORM_ALPHA * h + (acc + f), g_ref[...], b_ref[...])


def _combine(h, ys, idx_t, pos_t, wtok, seg_start, sh_gate, sh_up, sh_down, ln_g, ln_b):
    n = h.shape[0]
    tm = MOE_TOKENS
    smem_rows = pl.BlockSpec((ROUTE_ROWS, tm), lambda i, *_: (0, i), memory_space=pltpu.SMEM)
    const = lambda a: pl.BlockSpec(a.shape, lambda i, *_: (0,) * a.ndim)
    return pl.pallas_call(
        _combine_kernel,
        grid_spec=pltpu.PrefetchScalarGridSpec(
            num_scalar_prefetch=1,
            grid=(n // tm,),
            in_specs=[smem_rows, smem_rows, pl.BlockSpec((tm, D_MODEL), lambda i, *_: (i, 0)),
                      pl.BlockSpec((tm, LANES), lambda i, *_: (i, 0)),
                      const(sh_gate), const(sh_up), const(sh_down), const(ln_g), const(ln_b),
                      pl.BlockSpec(memory_space=pl.ANY)],
            out_specs=pl.BlockSpec((tm, D_MODEL), lambda i, *_: (i, 0)),
            scratch_shapes=[pltpu.VMEM((TOP_K, tm, D_MODEL), F32), pltpu.SemaphoreType.DMA]),
        out_shape=jax.ShapeDtypeStruct((n, D_MODEL), F32),
        compiler_params=_cparams(("arbitrary",)),
        name="moe_combine",
    )(seg_start, idx_t, pos_t, h, wtok, sh_gate, sh_up, sh_down, ln_g, ln_b, ys)


def _moe_ln(h, n_valid, router_w, router_bias, w_gate, w_up, w_down, sh_gate, sh_up, sh_down, ln_g, ln_b):
    idx_t, pos_t, wtok, cnt = _route(h, router_w.T.astype(BF16),
                                     jnp.broadcast_to(router_bias.astype(F32)[:, None], (N_EXPERTS, LANES)), n_valid)
    counts = cnt[:, 0].astype(jnp.int32)
    padded = (counts + MOE_BLOCK - 1) // MOE_BLOCK * MOE_BLOCK
    seg_end = jnp.cumsum(padded)
    seg_start = seg_end - padded
    n_blocks = -(-(n_valid * TOP_K + N_EXPERTS * (MOE_BLOCK - 1)) // MOE_BLOCK)
    cap = n_blocks * MOE_BLOCK
    block_expert = jnp.minimum(jnp.searchsorted(seg_end, jnp.arange(n_blocks) * MOE_BLOCK, side='right'),
                               N_EXPERTS - 1).astype(jnp.int32)
    xs = _dispatch(h, idx_t, pos_t, seg_start, counts, n_valid, cap)
    ys = _experts(xs, block_expert, w_gate, w_up, w_down)
    return _combine(h, ys, idx_t, pos_t, wtok, seg_start, sh_gate.astype(BF16), sh_up.astype(BF16),
                    sh_down.astype(BF16), ln_g, ln_b)


def _ssd_sample_kernel(xbc_ref, z_ref, dtg_ref, sconv_ref, s0_ref, convw_ref, convb_ref, dtb_ref, alog_ref,
                       dskip_ref, normw_ref, y_ref, s_ref, conv_out_ref, xc_ref, dt_ref, da_ref):
    b = pl.program_id(0)

    @pl.when(b == 0)
    def _():
        xin = xbc_ref[...]
        xc = convw_ref[SSD_CONV - 1:SSD_CONV, :] * xin
        for k in range(SSD_CONV - 1):
            xc = xc + convw_ref[k:k + 1, :] * sconv_ref[k]
        xc_ref[...] = _silu(xc + convb_ref[...])
        dt = _softplus(dtg_ref[...] + dtb_ref[...])
        dt_ref[...] = dt
        da_ref[...] = jnp.exp(dt * (-jnp.exp(alog_ref[...])))
        for k in range(SSD_CONV - 2):
            conv_out_ref[k] = sconv_ref[k + 1]
        conv_out_ref[SSD_CONV - 2] = xin

    xc = xc_ref[pl.ds(b, 1), :]
    dt = dt_ref[pl.ds(b, 1), :]
    da = da_ref[pl.ds(b, 1), :]
    ns = SSD_GROUPS * SSD_STATE
    eye = (lax.broadcasted_iota(jnp.int32, (HEAD_DIM, HEAD_DIM), 0)
           == lax.broadcasted_iota(jnp.int32, (HEAD_DIM, HEAD_DIM), 1))
    hpg = SSD_HEADS // SSD_GROUPS
    y_parts = []
    for h in range(SSD_HEADS):
        g = h // hpg
        x_h = xc[:, h * HEAD_DIM:(h + 1) * HEAD_DIM]
        b_g = xc[:, SSD_D + g * SSD_STATE:SSD_D + (g + 1) * SSD_STATE]
        c_g = xc[:, SSD_D + ns + g * SSD_STATE:SSD_D + ns + (g + 1) * SSD_STATE]
        xdt_col = jnp.sum(jnp.where(eye, x_h * dt[:, h:h + 1], 0.0), axis=1, keepdims=True)
        s_new = da[:, h:h + 1] * s0_ref[0, h] + xdt_col * b_g
        s_ref[0, h] = s_new
        y_h = _bdot_nt(c_g, s_new) + dskip_ref[:, h * HEAD_DIM:(h + 1) * HEAD_DIM] * x_h
        y_parts.append(y_h)
    y = jnp.concatenate(y_parts, axis=1)
    y_ref[pl.ds(b, 1), :] = _gated_group_norm(y, z_ref[pl.ds(b, 1), :], normw_ref[...])


def _ssd_sample(xbc, z, dtg, state_conv_t, state_ssm, conv_w, conv_b, dt_bias_pad, a_log_pad, d_skip_full, norm_w):
    bs = xbc.shape[0]
    const = lambda a: pl.BlockSpec(a.shape, lambda b: (0,) * a.ndim)
    state_spec = pl.BlockSpec((1, SSD_HEADS, HEAD_DIM, SSD_STATE), lambda b: (b, 0, 0, 0))
    return pl.pallas_call(
        _ssd_sample_kernel,
        grid=(bs,),
        in_specs=[const(xbc), const(z), const(dtg), const(state_conv_t), state_spec, const(conv_w), const(conv_b),
                  const(dt_bias_pad), const(a_log_pad), const(d_skip_full), const(norm_w)],
        out_specs=[pl.BlockSpec((bs, SSD_D), lambda b: (0, 0)), state_spec,
                   pl.BlockSpec((SSD_CONV - 1, bs, SSD_CONV_CH), lambda b: (0, 0, 0))],
        out_shape=[jax.ShapeDtypeStruct((bs, SSD_D), F32),
                   jax.ShapeDtypeStruct(state_ssm.shape, F32),
                   jax.ShapeDtypeStruct((SSD_CONV - 1, bs, SSD_CONV_CH), F32)],
        scratch_shapes=[pltpu.VMEM((bs, SSD_CONV_CH), F32), pltpu.VMEM((bs, LANES), F32),
                        pltpu.VMEM((bs, LANES), F32)],
        compiler_params=_cparams(("arbitrary",)),
        name="ssd_sample",
    )(xbc, z, dtg, state_conv_t, state_ssm, conv_w, conv_b, dt_bias_pad, a_log_pad, d_skip_full, norm_w)


PAGES_PER_STEP = 8


def _compress_paged_kernel(pt_ref, *refs, n_pages):
    pages = refs[:PAGES_PER_STEP]
    consts = refs[PAGES_PER_STEP:PAGES_PER_STEP + 7]
    o_ref, kbuf, vbuf = refs[PAGES_PER_STEP + 7:]
    s = pl.program_id(1)
    for j in range(PAGES_PER_STEP):
        r0 = pl.multiple_of((s * PAGES_PER_STEP + j) * PAGE_SIZE, PAGE_SIZE)
        kbuf[pl.ds(r0, PAGE_SIZE), :] = pages[j][0, :, 0:KV_D]
        vbuf[pl.ds(r0, PAGE_SIZE), :] = pages[j][0, :, KV_D:2 * KV_D]

    @pl.when(s == pl.num_programs(1) - 1)
    def _():
        nb = n_pages * PAGE_SIZE // CMP_BLOCK
        o_ref[...] = _compress_rows(kbuf, vbuf, *consts, nb)


def _compress_paged(pool, page_table, consts):
    bs, n_pages = page_table.shape
    nb = n_pages * PAGE_SIZE // CMP_BLOCK
    const = lambda a: pl.BlockSpec(a.shape, lambda b, s, pt: (0,) * a.ndim)

    def page_spec(j):
        return pl.BlockSpec((1, PAGE_SIZE, 2 * KV_D), lambda b, s, pt: (pt[b, s * PAGES_PER_STEP + j], 0, 0))

    return pl.pallas_call(
        functools.partial(_compress_paged_kernel, n_pages=n_pages),
        grid_spec=pltpu.PrefetchScalarGridSpec(
            num_scalar_prefetch=1,
            grid=(bs, n_pages // PAGES_PER_STEP),
            in_specs=[page_spec(j) for j in range(PAGES_PER_STEP)] + [const(a) for a in consts],
            out_specs=pl.BlockSpec((nb, 2 * KV_D), lambda b, s, pt: (b, 0)),
            scratch_shapes=[pltpu.VMEM((n_pages * PAGE_SIZE, KV_D), F32), pltpu.VMEM((n_pages * PAGE_SIZE, KV_D), F32)]),
        out_shape=jax.ShapeDtypeStruct((bs * nb, 2 * KV_D), F32),
        compiler_params=_cparams(("arbitrary", "arbitrary")),
        name="compress_paged",
    )(page_table, *([pool] * PAGES_PER_STEP), *consts)


SEL_PAST = TOP_N - 1


def _head_rows(q_row, hk):
    w = NSA_D // NSA_KV_HEADS
    q = jnp.broadcast_to(q_row[:, hk * w:(hk + 1) * w], (SUBLANES, w))
    own = (lax.broadcasted_iota(jnp.int32, (SUBLANES, w), 1) // HEAD_DIM
           == lax.broadcasted_iota(jnp.int32, (SUBLANES, w), 0))
    return jnp.where(own, q, 0.0), own


def _tile_kv_head(x, hk):
    sw = pltpu.roll(x, HEAD_DIM, 1)
    low = lax.broadcasted_iota(jnp.int32, x.shape, 1) < HEAD_DIM
    t = jnp.where(low, x, sw) if hk == 0 else jnp.where(low, sw, x)
    return jnp.concatenate([t, t], axis=1)


def _nsa_sample_cmp_kernel(qc_ref, cmp_ref, ocmp_ref, idx_ref, *, nc):
    b = pl.program_id(0)
    scale = HEAD_DIM ** -0.5
    q_row = qc_ref[pl.ds(b, 1), :]
    kc = cmp_ref[:, 0:KV_D]
    vc = cmp_ref[:, KV_D:2 * KV_D]
    lane = lax.broadcasted_iota(jnp.int32, (1, LANES), 1)
    o_parts = []
    for hk in range(NSA_KV_HEADS):
        qg, own = _head_rows(q_row, hk)
        s = _bdot_nt(qg, _tile_kv_head(kc, hk)) * scale
        ex = jnp.exp(s - jnp.max(s, axis=-1, keepdims=True))
        p = ex / jnp.sum(ex, axis=-1, keepdims=True)
        o = _bdot(p, _tile_kv_head(vc, hk))
        o_parts.append(jnp.sum(jnp.where(own, o, 0.0), axis=0, keepdims=True))
        hrow = lax.broadcasted_iota(jnp.int32, p.shape, 0) < NSA_HEADS // NSA_KV_HEADS
        imp = jnp.sum(jnp.where(hrow, p, 0.0), axis=0, keepdims=True)
        j = lax.broadcasted_iota(jnp.int32, (1, nc), 1)
        score = jnp.where((j == 0) | (j == nc - 1), FORCED_SCORE, imp)
        score_col = jnp.concatenate([score, jnp.zeros((LANES - 1, nc), F32)], 0).T[:, 0:1]
        jc = lax.broadcasted_iota(jnp.int32, (nc, nc), 0)
        jr = lax.broadcasted_iota(jnp.int32, (nc, nc), 1)
        beats = (score_col > score) | ((score_col == score) & (jc < jr))
        rank = jnp.sum(beats.astype(F32), axis=0, keepdims=True)
        row = jnp.zeros((1, LANES), F32)
        jf = j.astype(F32)
        for k in range(SEL_PAST):
            blk = jnp.sum(jnp.where(rank == k, jf, 0.0), axis=1, keepdims=True)
            row = jnp.where(lane == k, blk, row)
        idx_ref[pl.ds(b * NSA_KV_HEADS + hk, 1), :] = row.astype(jnp.int32)
    ocmp_ref[pl.ds(b, 1), :] = jnp.concatenate(o_parts, axis=1)


def _nsa_sample_cmp(qc, kvcmp, bs):
    nc = kvcmp.shape[0] // bs
    return pl.pallas_call(
        functools.partial(_nsa_sample_cmp_kernel, nc=nc),
        grid=(bs,),
        in_specs=[pl.BlockSpec((bs, NSA_D), lambda b: (0, 0)), pl.BlockSpec((nc, 2 * KV_D), lambda b: (b, 0))],
        out_specs=[pl.BlockSpec((bs, NSA_D), lambda b: (0, 0)),
                   pl.BlockSpec((bs * NSA_KV_HEADS, LANES), lambda b: (0, 0))],
        out_shape=[jax.ShapeDtypeStruct((bs, NSA_D), F32),
                   jax.ShapeDtypeStruct((bs * NSA_KV_HEADS, LANES), jnp.int32)],
        compiler_params=_cparams(("arbitrary",)),
        name="nsa_sample_cmp",
    )(qc, kvcmp)


def _sel_block_copy(pool_ref, pt_ref, sel_ref, buf_ref, sem, b, hk, k):
    per_page = PAGE_SIZE // SEL_BLOCK
    blk = sel_ref[b * NSA_KV_HEADS + hk, k]
    page = pt_ref[b, lax.shift_right_logical(blk, int(math.log2(per_page)))]
    row0 = pl.multiple_of((blk & (per_page - 1)) * SEL_BLOCK, SEL_BLOCK)
    return pltpu.make_async_copy(pool_ref.at[page, pl.ds(row0, SEL_BLOCK)], buf_ref.at[hk * SEL_PAST + k], sem)


def _nsa_sample_attn_kernel(pt_ref, sel_ref, qr_ref, new_sel_ref, new_win_ref, win_ref, dtg_ref, ocmp_ref, pool_ref,
                            o_ref, buf_ref, sem):
    b = pl.program_id(0)
    for hk in range(NSA_KV_HEADS):
        for k in range(SEL_PAST):
            _sel_block_copy(pool_ref, pt_ref, sel_ref, buf_ref, sem, b, hk, k).start()
    for hk in range(NSA_KV_HEADS):
        for k in range(SEL_PAST):
            _sel_block_copy(pool_ref, pt_ref, sel_ref, buf_ref, sem, b, hk, k).wait()
    scale = HEAD_DIM ** -0.5
    q_row = qr_ref[pl.ds(b, 1), :]
    sig = _sigmoid(dtg_ref[pl.ds(b, 1), :])
    new_sel = new_sel_ref[pl.ds(b, 1), :]
    new_win = new_win_ref[pl.ds(b, 1), :]
    win = win_ref[0]
    o_slc, o_win = [], []
    for hk in range(NSA_KV_HEADS):
        qg, own = _head_rows(q_row, hk)
        qb = qg.astype(BF16)

        def attend(kv_rows, kv_new, n_new):
            s = lax.dot_general(qb, _tile_kv_head(kv_rows[:, 0:KV_D], hk).astype(BF16), (((1,), (1,)), ((), ())),
                                preferred_element_type=F32) * scale
            k_new = _tile_kv_head(kv_new[:, 0:KV_D], hk).astype(BF16).astype(F32)
            s_new = jnp.sum(qb.astype(F32) * k_new, axis=1, keepdims=True) * scale
            m = jnp.maximum(jnp.max(s, axis=-1, keepdims=True), s_new)
            ex = jnp.exp(s - m)
            ex_new = jnp.exp(s_new - m) * n_new
            den = jnp.sum(ex, axis=-1, keepdims=True) + ex_new
            v_new = _tile_kv_head(kv_new[:, KV_D:2 * KV_D], hk).astype(BF16).astype(F32)
            o = _bdot(ex / den, _tile_kv_head(kv_rows[:, KV_D:2 * KV_D], hk))
            o = o + (ex_new / den).astype(BF16).astype(F32) * v_new
            return jnp.sum(jnp.where(own, o, 0.0), axis=0, keepdims=True)

        past = jnp.concatenate([buf_ref[hk * SEL_PAST + k] for k in range(SEL_PAST)], axis=0)
        o_slc.append(attend(past, new_sel, float(SEL_BLOCK)))
        o_win.append(attend(win, new_win, 1.0))
    o_slc = jnp.concatenate(o_slc, axis=1)
    o_win = jnp.concatenate(o_win, axis=1)
    gates = []
    for br in range(3):
        gates.append(jnp.concatenate(
            [jnp.broadcast_to(sig[:, GATE_COL0 + h * 3 + br:GATE_COL0 + h * 3 + br + 1], (1, HEAD_DIM))
             for h in range(NSA_HEADS)], axis=1))
    o_ref[pl.ds(b, 1), :] = gates[0] * ocmp_ref[pl.ds(b, 1), :] + gates[1] * o_slc + gates[2] * o_win


def _nsa_sample_attn(qr, new_sel, new_win, buf_win, dtg, o_cmp, pool_sel, page_table, sel_idx):
    bs = qr.shape[0]
    const = lambda a: pl.BlockSpec(a.shape, lambda b, pt, sel: (0,) * a.ndim)
    return pl.pallas_call(
        _nsa_sample_attn_kernel,
        grid_spec=pltpu.PrefetchScalarGridSpec(
            num_scalar_prefetch=2,
            grid=(bs,),
            in_specs=[const(qr), const(new_sel), const(new_win),
                      pl.BlockSpec((1,) + buf_win.shape[1:], lambda b, pt, sel: (b, 0, 0)),
                      const(dtg), const(o_cmp), pl.BlockSpec(memory_space=pl.ANY)],
            out_specs=pl.BlockSpec((bs, NSA_D), lambda b, pt, sel: (0, 0)),
            scratch_shapes=[pltpu.VMEM((NSA_KV_HEADS * SEL_PAST, SEL_BLOCK, 2 * KV_D), F32),
                            pltpu.SemaphoreType.DMA]),
        out_shape=jax.ShapeDtypeStruct((bs, NSA_D), F32),
        compiler_params=_cparams(("arbitrary",)),
        name="nsa_sample_attn",
    )(page_table, sel_idx, qr, new_sel, new_win, buf_win, dtg, o_cmp, pool_sel)


def kernel(x_prompt, x_sample, cache_kv_cmp, cache_kv_sel, page_table, cache_kv_win, state_ssm, state_conv,
           emb_ln_g, emb_ln_b, w_in, conv_w, conv_b, dt_bias, a_log, d_skip, ssd_norm_w,
           cmp_pe, cmp_w1, cmp_b1, cmp_w2, cmp_b2, w_out, ln1_g, ln1_b,
           router_w, router_bias, exp_w_gate, exp_w_up, exp_w_down,
           sh_w_gate, sh_w_up, sh_w_down, ln2_g, ln2_b):
    bp, tp, _ = x_prompt.shape
    bs, ts, _ = x_sample.shape
    assert ts == 1 and DEPTH == 1
    n_prompt = bp * tp
    past_len = page_table.shape[1] * PAGE_SIZE
    l = 0
    w_perm = _permute_w_in(w_in[l])
    ln0_g, ln0_b = emb_ln_g[None], emb_ln_b[None]
    ssd_consts = (conv_w[l], conv_b[l][None], _pad_lanes(dt_bias[l]), _pad_lanes(a_log[l]),
                  jnp.repeat(d_skip[l], HEAD_DIM)[None], ssd_norm_w[l][None])
    cmp_consts = _compress_consts(cmp_pe[l], cmp_w1[l], cmp_b1[l], cmp_w2[l], cmp_b2[l])
    w_o = w_out[l].astype(BF16)
    w_o_ssd, w_o_nsa = w_o[:SSD_D], w_o[SSD_D:]
    ln1 = (ln1_g[l][None], ln1_b[l][None])
    kv_shape = (2, NSA_KV_HEADS, HEAD_DIM)

    hp, z, xbc, qc, qr, kvc, kvs, kvw, dtg = _inproj(
        x_prompt.reshape(n_prompt, D_MODEL), ln0_g, ln0_b, w_perm, _rope_tables(jnp.arange(tp)), 256)
    y_ssd, ssm_p, conv_p = _ssd_prompt(xbc, z, dtg, *ssd_consts, bp, tp)
    kvcmp = _compress_prompt(kvc, cmp_consts, tp)
    y_nsa = _nsa_prompt(qc, qr, dtg, kvcmp, kvs, kvw, bp, tp)
    h1p = _outproj(y_ssd, y_nsa, hp, w_o_ssd, w_o_nsa, *ln1, 256)
    n_keep = min(WINDOW, tp)
    kvc_p = kvc.reshape((1, bp, tp) + kv_shape)
    kvs_p = kvs.reshape((1, bp, tp) + kv_shape)
    kvw_p = kvw.reshape((1, bp, tp) + kv_shape)[:, :, tp - n_keep:]

    s_hs, s_z, s_xbc, s_qc, s_qr, s_kvc, s_kvs, s_kvw, s_dtg = _inproj(
        x_sample.reshape(bs, D_MODEL), ln0_g, ln0_b, w_perm, _rope_tables(jnp.full((bs,), past_len)), bs)
    s_y_ssd, ssm_s, conv_s_t = _ssd_sample(s_xbc, s_z, s_dtg, jnp.swapaxes(state_conv[l], 0, 1), state_ssm[l],
                                           *ssd_consts)
    n_pool = cache_kv_cmp.shape[1]
    s_kvcmp = _compress_paged(cache_kv_cmp[l].reshape(n_pool, PAGE_SIZE, 2 * KV_D), page_table, cmp_consts)
    s_o_cmp, s_sel = _nsa_sample_cmp(s_qc, s_kvcmp, bs)
    buf_win = cache_kv_win[l].reshape(bs, -1, 2 * KV_D)
    s_y_nsa = _nsa_sample_attn(
        s_qr, s_kvs, s_kvw, buf_win, s_dtg, s_o_cmp,
        cache_kv_sel[l].reshape(n_pool, PAGE_SIZE, 2 * KV_D), page_table, s_sel)
    h1s = _outproj(s_y_ssd, s_y_nsa, s_hs, w_o_ssd, w_o_nsa, *ln1, bs)
    win_all = jnp.concatenate([buf_win, s_kvw[:, None, :]], 1)
    n_keep_s = min(WINDOW, past_len + ts)
    kvw_s = win_all[:, win_all.shape[1] - n_keep_s:].reshape((1, bs, n_keep_s) + kv_shape)
    kvc_s = s_kvc.reshape((1, bs, ts) + kv_shape)
    kvs_s = s_kvs.reshape((1, bs, ts) + kv_shape)

    n_tok = n_prompt + bs * ts
    n_pad = -(-n_tok // MOE_TOKENS) * MOE_TOKENS
    tok = jnp.concatenate([h1p, h1s, jnp.zeros((n_pad - n_tok, D_MODEL), F32)], 0)
    out = _moe_ln(tok, n_tok, router_w[l], router_bias[l], exp_w_gate[l], exp_w_up[l], exp_w_down[l],
                  sh_w_gate[l], sh_w_up[l], sh_w_down[l], ln2_g[l][None], ln2_b[l][None])
    y_prompt = out[:n_prompt].reshape(bp, tp, D_MODEL)
    y_sample = out[n_prompt:n_tok].reshape(bs, ts, D_MODEL)
    return (y_prompt, y_sample, kvc_p, kvs_p, kvw_p, ssm_p[None], conv_p[None],
            kvc_s, kvs_s, kvw_s, ssm_s[None], jnp.swapaxes(conv_s_t, 0, 1)[None])
```

```python
import functools
import math

import jax
import jax.numpy as jnp
import numpy as np
from jax import lax
from jax.experimental import pallas as pl
from jax.experimental.pallas import tpu as pltpu

D_MODEL = 1024
HEAD_DIM = 64
SSD_HEADS = 8
SSD_D = SSD_HEADS * HEAD_DIM
SSD_GROUPS = 2
SSD_STATE = 128
SSD_CONV = 4
SSD_CONV_CH = SSD_D + 2 * SSD_GROUPS * SSD_STATE
SSD_CHUNK = 128
NSA_HEADS = 8
NSA_KV_HEADS = 2
NSA_D = NSA_HEADS * HEAD_DIM
KV_D = NSA_KV_HEADS * HEAD_DIM
CMP_BLOCK = 64
CMP_HIDDEN = 128
SEL_BLOCK = 64
TOP_N = 16
WINDOW = 512
Q_BLOCK = 128
ROT_DIM = HEAD_DIM // 4
ROPE_THETA = 500000.0
N_EXPERTS = 64
TOP_K = 6
N_EXPERT_GROUPS = 8
EXPERTS_PER_GROUP = N_EXPERTS // N_EXPERT_GROUPS
TOPK_GROUPS = 4
D_EXPERT = 256
D_SHARED = 256
ROUTED_SCALE = 2.5
MOE_BLOCK = 256
DEPTH = 1
DEEPNORM_ALPHA = (2.0 * DEPTH) ** 0.25
LN_EPS = 1e-5
RMS_EPS = 1e-5
NEG = -1e30
FORCED_SCORE = 1e4
PAGE_SIZE = 128

LANES = 128
SUBLANES = 8
VMEM_LIMIT_BYTES = 56 * 1024 * 1024

U_Z = 0
U_XBC = U_Z + SSD_D
U_Q = U_XBC + SSD_CONV_CH
U_KVC = U_Q + NSA_D
U_KVS = U_KVC + 2 * KV_D
U_KVW = U_KVS + 2 * KV_D
U_DTG = U_KVW + 2 * KV_D
U_TOTAL = U_DTG + LANES
GATE_COL0 = SSD_HEADS

BF16 = jnp.bfloat16
F32 = jnp.float32


def _cparams(sem):
    return pltpu.CompilerParams(dimension_semantics=sem, vmem_limit_bytes=VMEM_LIMIT_BYTES)


def _bdot(a, b):
    return jnp.dot(a.astype(BF16), b.astype(BF16), preferred_element_type=F32)


def _bdot_nt(a, b):
    return lax.dot_general(a.astype(BF16), b.astype(BF16), (((1,), (1,)), ((), ())),
                           preferred_element_type=F32)


def _hdot(a, b):
    return jnp.dot(a, b, preferred_element_type=F32, precision=lax.Precision.HIGHEST)


def _sigmoid(x):
    return 1.0 / (1.0 + jnp.exp(-x))


def _silu(x):
    return x * _sigmoid(x)


def _layer_norm(x, g, b):
    mu = jnp.mean(x, axis=-1, keepdims=True)
    xc = x - mu
    var = jnp.mean(xc * xc, axis=-1, keepdims=True)
    return xc * lax.rsqrt(var + LN_EPS) * g + b


def _rope_tile(x, cos, sa, sb):
    return x * cos + pltpu.roll(x, LANES - ROT_DIM // 2, 1) * sa + pltpu.roll(x, ROT_DIM // 2, 1) * sb


def _inproj_kernel(x_ref, g_ref, b_ref, w_ref, rope_ref,
                   h_ref, z_ref, xbc_ref, qc_ref, qr_ref, kvc_ref, kvs_ref, kvw_ref, dtg_ref):
    h = _layer_norm(x_ref[...], g_ref[...], b_ref[...])
    h_ref[...] = h
    u = jnp.dot(h.astype(BF16), w_ref[...], preferred_element_type=F32)
    cos = rope_ref[:, 0:LANES]
    sa = rope_ref[:, LANES:2 * LANES]
    sb = rope_ref[:, 2 * LANES:3 * LANES]
    z_ref[...] = u[:, U_Z:U_XBC]
    xbc_ref[...] = u[:, U_XBC:U_Q]
    qc_ref[...] = u[:, U_Q:U_KVC]
    for c in range(NSA_D // LANES):
        qr_ref[:, c * LANES:(c + 1) * LANES] = _rope_tile(u[:, U_Q + c * LANES:U_Q + (c + 1) * LANES], cos, sa, sb)
    kvc_ref[...] = u[:, U_KVC:U_KVS]
    kvs_ref[:, 0:KV_D] = _rope_tile(u[:, U_KVS:U_KVS + KV_D], cos, sa, sb)
    kvs_ref[:, KV_D:2 * KV_D] = u[:, U_KVS + KV_D:U_KVW]
    kvw_ref[:, 0:KV_D] = _rope_tile(u[:, U_KVW:U_KVW + KV_D], cos, sa, sb)
    kvw_ref[:, KV_D:2 * KV_D] = u[:, U_KVW + KV_D:U_DTG]
    dtg_ref[...] = u[:, U_DTG:U_TOTAL]


def _rope_tables(pos):
    half = ROT_DIM // 2
    inv = ROPE_THETA ** (-jnp.arange(half, dtype=F32) / half)
    ang = pos.astype(F32)[:, None] * inv
    cos, sin = jnp.cos(ang), jnp.sin(ang)
    ones = jnp.ones((pos.shape[0], HEAD_DIM - ROT_DIM), F32)
    zeros = jnp.zeros((pos.shape[0], HEAD_DIM - ROT_DIM), F32)
    zh = jnp.zeros_like(sin)
    c = jnp.concatenate([cos, cos, ones], 1)
    sa = jnp.concatenate([-sin, zh, zeros], 1)
    sb = jnp.concatenate([zh, sin, zeros], 1)
    return jnp.concatenate([jnp.tile(t, (1, LANES // HEAD_DIM)) for t in (c, sa, sb)], 1)


def _permute_w_in(w):
    sizes = (SSD_D, SSD_CONV_CH, SSD_HEADS, NSA_D, KV_D, KV_D, KV_D, KV_D, KV_D, KV_D, 3 * NSA_HEADS)
    offs = np.concatenate([[0], np.cumsum(sizes)])
    seg = [w[:, offs[i]:offs[i + 1]] for i in range(len(sizes))]
    pad = jnp.zeros((w.shape[0], LANES - SSD_HEADS - 3 * NSA_HEADS), w.dtype)
    out = jnp.concatenate([seg[0], seg[1], seg[3], seg[4], seg[5], seg[6], seg[7], seg[8], seg[9],
                           seg[2], seg[10], pad], 1)
    return out.astype(BF16)


def _inproj(x, ln_g, ln_b, w_perm, rope_tab, tm):
    n = x.shape[0]
    nt = n // tm
    n_rope_blocks = rope_tab.shape[0] // tm
    row = lambda w: pl.BlockSpec((tm, w), lambda i: (i, 0))
    const = lambda a: pl.BlockSpec(a.shape, lambda i: (0,) * a.ndim)
    widths = (D_MODEL, SSD_D, SSD_CONV_CH, NSA_D, NSA_D, 2 * KV_D, 2 * KV_D, 2 * KV_D, LANES)
    return pl.pallas_call(
        _inproj_kernel,
        grid=(nt,),
        in_specs=[row(D_MODEL), const(ln_g), const(ln_b), const(w_perm),
                  pl.BlockSpec((tm, 3 * LANES), lambda i: (i % n_rope_blocks, 0))],
        out_specs=[row(w) for w in widths],
        out_shape=[jax.ShapeDtypeStruct((n, w), F32) for w in widths],
        compiler_params=_cparams(("parallel",)),
        name="inproj",
    )(x, ln_g, ln_b, w_perm, rope_tab)


def _softplus(x):
    return jnp.maximum(x, 0.0) + jnp.log1p(jnp.exp(-jnp.abs(x)))


def _gated_group_norm(y, z, norm_w):
    y = y * _silu(z)
    gw = SSD_D // SSD_GROUPS
    parts = []
    for g in range(SSD_GROUPS):
        yg = y[:, g * gw:(g + 1) * gw]
        ms = jnp.mean(yg * yg, axis=-1, keepdims=True)
        parts.append(yg * lax.rsqrt(ms + RMS_EPS))
    return jnp.concatenate(parts, axis=1) * norm_w


def _ssd_prompt_kernel(xbc_ref, z_ref, dtg_ref, convw_ref, convb_ref, dtb_ref, alog_ref, dskip_ref, normw_ref,
                       y_ref, state_ref, conv_ref, ext_ref, s_ref):
    c = pl.program_id(1)
    nc = pl.num_programs(1)
    L = SSD_CHUNK
    halo = SUBLANES

    @pl.when(c == 0)
    def _():
        ext_ref[0:halo, :] = jnp.zeros((halo, SSD_CONV_CH), F32)
        s_ref[...] = jnp.zeros_like(s_ref)

    xin = xbc_ref[...]
    ext_ref[halo:halo + L, :] = xin
    xc = convw_ref[SSD_CONV - 1:SSD_CONV, :] * xin
    for k in range(SSD_CONV - 1):
        off = halo - (SSD_CONV - 1) + k
        xc = xc + convw_ref[k:k + 1, :] * ext_ref[off:off + L, :]
    ext_ref[0:halo, :] = ext_ref[L:L + halo, :]
    xc = _silu(xc + convb_ref[...])
    xs = xc[:, 0:SSD_D]
    ns = SSD_GROUPS * SSD_STATE
    bm = xc[:, SSD_D:SSD_D + ns]
    cm = xc[:, SSD_D + ns:SSD_D + 2 * ns]

    dt = _softplus(dtg_ref[...] + dtb_ref[...])
    da = dt * (-jnp.exp(alog_ref[...]))
    row = lax.broadcasted_iota(jnp.int32, (L, L), 0)
    col = lax.broadcasted_iota(jnp.int32, (L, L), 1)
    tril = row >= col
    acum = _hdot(tril.astype(F32), da)
    acum_t = acum.T
    eacum = jnp.exp(acum)
    alast = acum[L - 1:L, :]
    edecay = jnp.exp(alast - acum)
    elast = jnp.exp(alast)

    dt_full = jnp.concatenate([jnp.broadcast_to(dt[:, h:h + 1], (L, HEAD_DIM)) for h in range(SSD_HEADS)], 1)
    dec_full = jnp.concatenate([jnp.broadcast_to(edecay[:, h:h + 1], (L, HEAD_DIM)) for h in range(SSD_HEADS)], 1)
    xdt = xs * dt_full
    xdec_t = (xdt * dec_full).T

    hpg = SSD_HEADS // SSD_GROUPS
    y_parts = []
    for h in range(SSD_HEADS):
        g = h // hpg
        b_g = bm[:, g * SSD_STATE:(g + 1) * SSD_STATE]
        c_g = cm[:, g * SSD_STATE:(g + 1) * SSD_STATE]
        if h % hpg == 0:
            cb = _bdot_nt(c_g, b_g)
        seg = acum[:, h:h + 1] - acum_t[h:h + 1, :]
        lmat = jnp.where(tril, jnp.exp(jnp.where(tril, seg, 0.0)), 0.0)
        xdt_h = xdt[:, h * HEAD_DIM:(h + 1) * HEAD_DIM]
        y_h = _bdot(cb * lmat, xdt_h)
        s_prev = s_ref[h]
        y_h = y_h + _bdot_nt(c_g, s_prev) * eacum[:, h:h + 1]
        y_h = y_h + dskip_ref[:, h * HEAD_DIM:(h + 1) * HEAD_DIM] * xs[:, h * HEAD_DIM:(h + 1) * HEAD_DIM]
        y_parts.append(y_h)
        s_ref[h] = elast[:, h:h + 1] * s_prev + _bdot(xdec_t[h * HEAD_DIM:(h + 1) * HEAD_DIM, :], b_g)
    y = jnp.concatenate(y_parts, axis=1)
    y_ref[...] = _gated_group_norm(y, z_ref[...], normw_ref[...])

    @pl.when(c == nc - 1)
    def _():
        state_ref[0] = s_ref[...]
        conv_ref[0] = xin[L - (SSD_CONV - 1):L, :]


def _ssd_prompt(xbc, z, dtg, conv_w, conv_b, dt_bias_pad, a_log_pad, d_skip_full, norm_w, bn, t):
    nc = t // SSD_CHUNK
    row = lambda w: pl.BlockSpec((SSD_CHUNK, w), lambda b, c: (b * nc + c, 0))
    const = lambda a: pl.BlockSpec(a.shape, lambda b, c: (0,) * a.ndim)
    return pl.pallas_call(
        _ssd_prompt_kernel,
        grid=(bn, nc),
        in_specs=[row(SSD_CONV_CH), row(SSD_D), row(LANES), const(conv_w), const(conv_b), const(dt_bias_pad),
                  const(a_log_pad), const(d_skip_full), const(norm_w)],
        out_specs=[row(SSD_D),
                   pl.BlockSpec((1, SSD_HEADS, HEAD_DIM, SSD_STATE), lambda b, c: (b, 0, 0, 0)),
                   pl.BlockSpec((1, SSD_CONV - 1, SSD_CONV_CH), lambda b, c: (b, 0, 0))],
        out_shape=[jax.ShapeDtypeStruct((bn * t, SSD_D), F32),
                   jax.ShapeDtypeStruct((bn, SSD_HEADS, HEAD_DIM, SSD_STATE), F32),
                   jax.ShapeDtypeStruct((bn, SSD_CONV - 1, SSD_CONV_CH), F32)],
        scratch_shapes=[pltpu.VMEM((SSD_CHUNK + 2 * SUBLANES, SSD_CONV_CH), F32),
                        pltpu.VMEM((SSD_HEADS, HEAD_DIM, SSD_STATE), F32)],
        compiler_params=_cparams(("parallel", "arbitrary")),
        name="ssd_prompt",
    )(xbc, z, dtg, conv_w, conv_b, dt_bias_pad, a_log_pad, d_skip_full, norm_w)


def _pad_lanes(v, fill=0.0):
    return jnp.concatenate([v.astype(F32), jnp.full((LANES - v.shape[0],), fill, F32)])[None]


def _compress_rows(k_ref, v_ref, pe_ref, w1k_ref, w1v_ref, b1_ref, w2k_ref, w2v_ref, b2_ref, nb):
    acck = jnp.zeros((nb, 2 * CMP_HIDDEN), F32)
    accv = jnp.zeros((nb, 2 * CMP_HIDDEN), F32)
    for l in range(CMP_BLOCK):
        xk = k_ref[pl.ds(l, nb, stride=CMP_BLOCK), :] + pe_ref[l:l + 1, 0:KV_D]
        xv = v_ref[pl.ds(l, nb, stride=CMP_BLOCK), :] + pe_ref[l:l + 1, KV_D:2 * KV_D]
        acck = acck + jnp.dot(xk.astype(BF16), w1k_ref[l], preferred_element_type=F32)
        accv = accv + jnp.dot(xv.astype(BF16), w1v_ref[l], preferred_element_type=F32)
    hk = _silu(acck + b1_ref[:, 0:2 * CMP_HIDDEN])
    hv = _silu(accv + b1_ref[:, 2 * CMP_HIDDEN:4 * CMP_HIDDEN])
    ok = jnp.dot(hk.astype(BF16), w2k_ref[...], preferred_element_type=F32) + b2_ref[:, 0:KV_D]
    ov = jnp.dot(hv.astype(BF16), w2v_ref[...], preferred_element_type=F32) + b2_ref[:, KV_D:2 * KV_D]
    return jnp.concatenate([ok, ov], axis=1)


def _compress_kernel(k_ref, v_ref, pe_ref, w1k_ref, w1v_ref, b1_ref, w2k_ref, w2v_ref, b2_ref, o_ref, *, nb):
    o_ref[...] = _compress_rows(k_ref, v_ref, pe_ref, w1k_ref, w1v_ref, b1_ref, w2k_ref, w2v_ref, b2_ref, nb)


def _block_diag2(w):
    z = jnp.zeros_like(w)
    return jnp.concatenate([jnp.concatenate([w, z], -1), jnp.concatenate([z, w], -1)], -2)


def _compress_consts(cmp_pe, cmp_w1, cmp_b1, cmp_w2, cmp_b2):
    pe = jnp.concatenate([cmp_pe[0], cmp_pe[0], cmp_pe[1], cmp_pe[1]], -1)
    w1k = _block_diag2(cmp_w1[0]).astype(BF16)
    w1v = _block_diag2(cmp_w1[1]).astype(BF16)
    b1 = jnp.concatenate([cmp_b1[0], cmp_b1[0], cmp_b1[1], cmp_b1[1]])[None]
    w2k = _block_diag2(cmp_w2[0]).astype(BF16)
    w2v = _block_diag2(cmp_w2[1]).astype(BF16)
    b2 = jnp.concatenate([cmp_b2[0], cmp_b2[0], cmp_b2[1], cmp_b2[1]])[None]
    return pe, w1k, w1v, b1, w2k, w2v, b2


def _compress_prompt(kvc, consts, rows_per_step):
    n = kvc.shape[0]
    nb = rows_per_step // CMP_BLOCK
    const = lambda a: pl.BlockSpec(a.shape, lambda i: (0,) * a.ndim)
    return pl.pallas_call(
        functools.partial(_compress_kernel, nb=nb),
        grid=(n // rows_per_step,),
        in_specs=[pl.BlockSpec((rows_per_step, KV_D), lambda i: (i, 0)),
                  pl.BlockSpec((rows_per_step, KV_D), lambda i: (i, 1))] + [const(a) for a in consts],
        out_specs=pl.BlockSpec((nb, 2 * KV_D), lambda i: (i, 0)),
        out_shape=jax.ShapeDtypeStruct((n // CMP_BLOCK, 2 * KV_D), F32),
        compiler_params=_cparams(("parallel",)),
        name="compress_prompt",
    )(kvc, kvc, *consts)


SEL_KEY_TILE = 256
WIN_KEYS = WINDOW + Q_BLOCK


def _dup_head(x, hk):
    sw = pltpu.roll(x, HEAD_DIM, 1)
    low = lax.broadcasted_iota(jnp.int32, x.shape, 1) < HEAD_DIM
    return jnp.where(low, x, sw) if hk == 0 else jnp.where(low, sw, x)


def _masked_softmax(s, mask):
    sm = jnp.where(mask, s, NEG)
    ex = jnp.where(mask, jnp.exp(sm - jnp.max(sm, axis=-1, keepdims=True)), 0.0)
    den = jnp.sum(ex, axis=-1, keepdims=True)
    return ex / jnp.where(den > 0.0, den, 1.0)


def _select_blocks(imp, cur, n_top):
    n_cand = imp.shape[1]
    j = lax.broadcasted_iota(jnp.int32, imp.shape, 1)
    future = j > cur
    forced = (j == 0) | (j == cur) | (j == cur - 1)
    score = jnp.where(future, NEG, jnp.where(forced, FORCED_SCORE, imp))
    rank = jnp.zeros(imp.shape, F32)
    for c in range(n_cand):
        col = score[:, c:c + 1]
        beats = (col > score) | ((col == score) & (j > c))
        rank = rank + beats.astype(F32)
    return ((rank < n_top) & (score > 0.5 * NEG)).astype(F32)


def _nsa_prompt_kernel(qc_ref, qr_ref, dtg_ref, cmp_ref, kvs_ref, kvw_ref, o_ref,
                       cmp_d, kvs_d, kvw_d, bias_ref, m_ref, l_ref, acc_ref, *, t):
    qb = pl.program_id(1)
    nbk = t // SEL_BLOCK
    tq = Q_BLOCK
    tk = SEL_KEY_TILE
    hpg = NSA_HEADS // NSA_KV_HEADS
    scale = HEAD_DIM ** -0.5

    @pl.when(qb == 0)
    def _():
        for src, dst in ((cmp_ref, cmp_d), (kvs_ref, kvs_d), (kvw_ref, kvw_d)):
            x = src[...]
            for hk in range(NSA_KV_HEADS):
                dst[hk, :, 0:KV_D] = _dup_head(x[:, 0:KV_D], hk).astype(BF16)
                dst[hk, :, KV_D:2 * KV_D] = _dup_head(x[:, KV_D:2 * KV_D], hk).astype(BF16)

    t0 = qb * tq
    rows = t0 + lax.broadcasted_iota(jnp.int32, (tq, 1), 0)
    lane = lax.broadcasted_iota(jnp.int32, (tq, LANES), 1)
    half_mask = (lane < HEAD_DIM, lane >= HEAD_DIM)
    sig = _sigmoid(dtg_ref[...])
    cidx = lax.broadcasted_iota(jnp.int32, (tq, nbk), 1)
    vis = (cidx + 1) * CMP_BLOCK - 1 <= rows
    expand = (lax.broadcasted_iota(jnp.int32, (nbk, t), 1) // SEL_BLOCK
              == lax.broadcasted_iota(jnp.int32, (nbk, t), 0)).astype(BF16)
    win_start = pl.multiple_of(jnp.maximum(t0 - WINDOW, 0), tq)
    wpos = win_start + lax.broadcasted_iota(jnp.int32, (tq, WIN_KEYS), 1)
    win_bias = jnp.where((wpos <= rows) & (wpos >= rows - WINDOW), 0.0, NEG)
    n_kt = (t0 + tq + tk - 1) // tk

    def stack_heads(ref, hk):
        parts = []
        for hh in range(hpg):
            head = hk * hpg + hh
            p, e = head // 2, head % 2
            parts.append(jnp.where(half_mask[e], ref[:, p * LANES:(p + 1) * LANES] * scale, 0.0))
        return jnp.concatenate(parts, axis=0).astype(BF16)

    for hk in range(NSA_KV_HEADS):
        qcs = stack_heads(qc_ref, hk)
        s = lax.dot_general(qcs, cmp_d[hk, :, 0:KV_D], (((1,), (1,)), ((), ())), preferred_element_type=F32)
        pc = _masked_softmax(s.reshape(hpg, tq, nbk), vis[None])
        imp = jnp.sum(pc, axis=0)
        o_cmp = jnp.dot(pc.reshape(hpg * tq, nbk).astype(BF16), cmp_d[hk, :, KV_D:2 * KV_D],
                        preferred_element_type=F32).reshape(hpg, tq, LANES)

        sel = _select_blocks(imp, rows // SEL_BLOCK, TOP_N)
        selk = jnp.dot(sel.astype(BF16), expand, preferred_element_type=F32)
        for kt in range(t // tk):
            kpos = kt * tk + lax.broadcasted_iota(jnp.int32, (tq, tk), 1)
            bias_ref[kt] = jnp.where((selk[:, kt * tk:(kt + 1) * tk] > 0.5) & (kpos <= rows), 0.0, NEG)

        qrs = stack_heads(qr_ref, hk)
        m_ref[...] = jnp.full(m_ref.shape, NEG, F32)
        l_ref[...] = jnp.zeros(l_ref.shape, F32)
        acc_ref[...] = jnp.zeros(acc_ref.shape, F32)

        def sel_step(kt, carry):
            k0 = pl.multiple_of(kt * tk, tk)
            kblk = kvs_d[hk, pl.ds(k0, tk), 0:KV_D]
            vblk = kvs_d[hk, pl.ds(k0, tk), KV_D:2 * KV_D]
            s = lax.dot_general(qrs, kblk, (((1,), (1,)), ((), ())), preferred_element_type=F32)
            s = s.reshape(hpg, tq, tk) + bias_ref[kt][None]
            m_old = m_ref[...]
            m_new = jnp.maximum(m_old, jnp.max(s, axis=-1, keepdims=True))
            alpha = jnp.exp(m_old - m_new)
            pe = jnp.exp(s - jnp.concatenate([m_new] * (tk // LANES), axis=-1))
            l_ref[...] = alpha * l_ref[...] + jnp.sum(pe, axis=-1, keepdims=True)
            pv = jnp.dot(pe.reshape(hpg * tq, tk).astype(BF16), vblk, preferred_element_type=F32)
            acc_ref[...] = alpha * acc_ref[...] + pv.reshape(hpg, tq, LANES)
            m_ref[...] = m_new
            return carry

        lax.fori_loop(0, n_kt, sel_step, 0)
        o_slc = acc_ref[...] / l_ref[...]

        kw = kvw_d[hk, pl.ds(win_start, WIN_KEYS), 0:KV_D]
        vw = kvw_d[hk, pl.ds(win_start, WIN_KEYS), KV_D:2 * KV_D]
        sw = lax.dot_general(qrs, kw, (((1,), (1,)), ((), ())), preferred_element_type=F32)
        sw = sw.reshape(hpg, tq, WIN_KEYS) + win_bias[None]
        pw = jnp.exp(sw - jnp.max(sw, axis=-1, keepdims=True))
        den = jnp.sum(pw, axis=-1, keepdims=True)
        o_win = jnp.dot(pw.reshape(hpg * tq, WIN_KEYS).astype(BF16), vw,
                        preferred_element_type=F32).reshape(hpg, tq, LANES) / den

        for hh in range(hpg):
            head = hk * hpg + hh
            p, e = head // 2, head % 2
            c0 = GATE_COL0 + head * 3
            mix = (sig[:, c0:c0 + 1] * o_cmp[hh] + sig[:, c0 + 1:c0 + 2] * o_slc[hh]
                   + sig[:, c0 + 2:c0 + 3] * o_win[hh])
            if e == 0:
                mix_even = mix
            else:
                o_ref[:, p * LANES:(p + 1) * LANES] = jnp.where(half_mask[0], mix_even, mix)


def _nsa_prompt(qc, qr, dtg, kvcmp, kvs, kvw, bn, t):
    nq = t // Q_BLOCK
    nbk = t // SEL_BLOCK
    hpg = NSA_HEADS // NSA_KV_HEADS
    assert nbk >= TOP_N and t >= WIN_KEYS and t % SEL_KEY_TILE == 0
    qrow = lambda w: pl.BlockSpec((Q_BLOCK, w), lambda b, i: (b * nq + i, 0))
    seq = lambda r: pl.BlockSpec((r, 2 * KV_D), lambda b, i: (b, 0))
    return pl.pallas_call(
        functools.partial(_nsa_prompt_kernel, t=t),
        grid=(bn, nq),
        in_specs=[qrow(NSA_D), qrow(NSA_D), qrow(LANES), seq(nbk), seq(t), seq(t)],
        out_specs=qrow(NSA_D),
        out_shape=jax.ShapeDtypeStruct((bn * t, NSA_D), F32),
        scratch_shapes=[pltpu.VMEM((NSA_KV_HEADS, nbk, 2 * KV_D), BF16),
                        pltpu.VMEM((NSA_KV_HEADS, t, 2 * KV_D), BF16),
                        pltpu.VMEM((NSA_KV_HEADS, t, 2 * KV_D), BF16),
                        pltpu.VMEM((t // SEL_KEY_TILE, Q_BLOCK, SEL_KEY_TILE), F32),
                        pltpu.VMEM((hpg, Q_BLOCK, LANES), F32),
                        pltpu.VMEM((hpg, Q_BLOCK, LANES), F32),
                        pltpu.VMEM((hpg, Q_BLOCK, LANES), F32)],
        compiler_params=_cparams(("parallel", "arbitrary")),
        name="nsa_prompt",
    )(qc, qr, dtg, kvcmp, kvs, kvw)


def _outproj_kernel(ys_ref, yn_ref, h_ref, ws_ref, wn_ref, g_ref, b_ref, o_ref):
    mix = jnp.dot(ys_ref[...].astype(BF16), ws_ref[...], preferred_element_type=F32)
    mix = mix + jnp.dot(yn_ref[...].astype(BF16), wn_ref[...], preferred_element_type=F32)
    o_ref[...] = _layer_norm(DEEPNORM_ALPHA * h_ref[...] + mix, g_ref[...], b_ref[...])


def _outproj(y_ssd, y_nsa, h, w_ssd, w_nsa, ln_g, ln_b, tm):
    n = h.shape[0]
    row = lambda w: pl.BlockSpec((tm, w), lambda i: (i, 0))
    const = lambda a: pl.BlockSpec(a.shape, lambda i: (0,) * a.ndim)
    return pl.pallas_call(
        _outproj_kernel,
        grid=(n // tm,),
        in_specs=[row(SSD_D), row(NSA_D), row(D_MODEL), const(w_ssd), const(w_nsa), const(ln_g), const(ln_b)],
        out_specs=row(D_MODEL),
        out_shape=jax.ShapeDtypeStruct((n, D_MODEL), F32),
        compiler_params=_cparams(("parallel",)),
        name="outproj",
    )(y_ssd, y_nsa, h, w_ssd, w_nsa, ln_g, ln_b)


MOE_TOKENS = 256
ROUTE_ROWS = 8


def _rank_rows(x):
    n = x.shape[0]
    idx = lax.broadcasted_iota(jnp.int32, x.shape, 0)
    rank = jnp.zeros(x.shape, F32)
    for r in range(n):
        row = x[r:r + 1, :]
        rank = rank + ((row > x) | ((row == x) & (idx > r))).astype(F32)
    return rank


def _route_kernel(h_ref, rw_ref, rb_ref, idx_ref, pos_ref, wtok_ref, cnt_ref, carry_ref, *, n_valid):
    i = pl.program_id(0)
    tm = MOE_TOKENS

    @pl.when(i == 0)
    def _():
        carry_ref[...] = jnp.zeros_like(carry_ref)

    logits = lax.dot_general(rw_ref[...], h_ref[...].astype(BF16), (((1,), (1,)), ((), ())),
                             preferred_element_type=F32)
    scores = _sigmoid(logits)
    biased = scores + rb_ref[:, 0:1]
    b3 = biased.reshape(N_EXPERT_GROUPS, EXPERTS_PER_GROUP, tm)
    sidx = lax.broadcasted_iota(jnp.int32, b3.shape, 1)
    m1 = jnp.max(b3, axis=1, keepdims=True)
    first = jnp.min(jnp.where(b3 == m1, sidx, EXPERTS_PER_GROUP), axis=1, keepdims=True)
    m2 = jnp.max(jnp.where(sidx == first, -jnp.inf, b3), axis=1, keepdims=True)
    grp_score = (m1 + m2).reshape(N_EXPERT_GROUPS, tm)
    grp_keep = _rank_rows(grp_score) < TOPK_GROUPS
    masked = jnp.where(grp_keep.reshape(N_EXPERT_GROUPS, 1, tm), b3, NEG).reshape(N_EXPERTS, tm)
    rank = _rank_rows(masked)
    tok = i * tm + lax.broadcasted_iota(jnp.int32, (1, tm), 1)
    valid = tok < n_valid
    sel = (rank < TOP_K) & valid
    self32 = sel.astype(F32)
    wsel = self32 * scores
    wsum = jnp.sum(wsel, axis=0, keepdims=True)
    w = wsel / jnp.where(wsum > 0.0, wsum, 1.0) * ROUTED_SCALE

    upper = (lax.broadcasted_iota(jnp.int32, (tm, tm), 0) < lax.broadcasted_iota(jnp.int32, (tm, tm), 1))
    pos = jnp.dot(sel.astype(BF16), upper.astype(BF16), preferred_element_type=F32) + carry_ref[:, 0:1]
    carry_ref[...] = carry_ref[...] + jnp.sum(self32, axis=1, keepdims=True)

    eidx = lax.broadcasted_iota(jnp.int32, (N_EXPERTS, tm), 0).astype(F32)
    idx_rows, pos_rows, w_rows = [], [], []
    for k in range(TOP_K):
        hit = (rank == k) & sel
        idx_rows.append(jnp.sum(jnp.where(hit, eidx, 0.0), axis=0, keepdims=True))
        pos_rows.append(jnp.sum(jnp.where(hit, pos, 0.0), axis=0, keepdims=True))
        w_rows.append(jnp.sum(jnp.where(hit, w, 0.0), axis=0, keepdims=True))
    zrow = jnp.zeros((ROUTE_ROWS - TOP_K, tm), F32)
    idx_ref[...] = jnp.concatenate(idx_rows + [zrow], 0).astype(jnp.int32)
    pos_ref[...] = jnp.concatenate(pos_rows + [zrow], 0).astype(jnp.int32)
    wtok_ref[...] = jnp.concatenate(w_rows + [jnp.zeros((LANES - TOP_K, tm), F32)], 0).T

    @pl.when(i == pl.num_programs(0) - 1)
    def _():
        cnt_ref[...] = jnp.broadcast_to(carry_ref[:, 0:1], cnt_ref.shape)


def _route(h, router_wt, router_bias_col, n_valid):
    n = h.shape[0]
    tm = MOE_TOKENS
    const = lambda a: pl.BlockSpec(a.shape, lambda i: (0,) * a.ndim)
    return pl.pallas_call(
        functools.partial(_route_kernel, n_valid=n_valid),
        grid=(n // tm,),
        in_specs=[pl.BlockSpec((tm, D_MODEL), lambda i: (i, 0)), const(router_wt), const(router_bias_col)],
        out_specs=[pl.BlockSpec((ROUTE_ROWS, tm), lambda i: (0, i)),
                   pl.BlockSpec((ROUTE_ROWS, tm), lambda i: (0, i)),
                   pl.BlockSpec((tm, LANES), lambda i: (i, 0)),
                   pl.BlockSpec((N_EXPERTS, LANES), lambda i: (0, 0))],
        out_shape=[jax.ShapeDtypeStruct((ROUTE_ROWS, n), jnp.int32),
                   jax.ShapeDtypeStruct((ROUTE_ROWS, n), jnp.int32),
                   jax.ShapeDtypeStruct((n, LANES), F32),
                   jax.ShapeDtypeStruct((N_EXPERTS, LANES), F32)],
        scratch_shapes=[pltpu.VMEM((N_EXPERTS, LANES), F32)],
        compiler_params=_cparams(("arbitrary",)),
        name="moe_route",
    )(h, router_wt, router_bias_col)


def _row_copy(src_ref, src_row, dst_ref, dst_row, sem):
    return pltpu.make_async_copy(src_ref.at[pl.ds(src_row, 1)], dst_ref.at[pl.ds(dst_row, 1)], sem)


def _dispatch_kernel(start_ref, cnt_ref, idx_ref, pos_ref, x_ref, xs_ref, zero_ref, sem, zsem, *, n_valid, cap):
    i = pl.program_id(0)
    tm = MOE_TOKENS
    ntok = jnp.minimum(tm, n_valid - i * tm)

    def scatter_row(t, carry):
        for k in range(TOP_K):
            dst = start_ref[idx_ref[k, t]] + pos_ref[k, t]
            _row_copy(x_ref, t, xs_ref, dst, sem).start()
        return carry

    lax.fori_loop(0, ntok, scatter_row, 0)

    @pl.when(i == 0)
    def _():
        zero_ref[...] = jnp.zeros_like(zero_ref)

        def fill_expert(e, total):
            lo = start_ref[e] + cnt_ref[e]
            hi = jnp.where(e == N_EXPERTS - 1, cap, start_ref[jnp.minimum(e + 1, N_EXPERTS - 1)])

            def fill_row(r, c):
                _row_copy(zero_ref, 0, xs_ref, r, zsem).start()
                return c

            lax.fori_loop(lo, hi, fill_row, 0)
            return total + (hi - lo)

        n_fill = lax.fori_loop(0, N_EXPERTS, fill_expert, 0)

        def wait_fill(r, c):
            _row_copy(zero_ref, 0, xs_ref, 0, zsem).wait()
            return c

        lax.fori_loop(0, n_fill, wait_fill, 0)

    def wait_row(t, carry):
        for k in range(TOP_K):
            _row_copy(x_ref, 0, xs_ref, 0, sem).wait()
        return carry

    lax.fori_loop(0, ntok, wait_row, 0)


def _dispatch(h, idx_t, pos_t, seg_start, counts, n_valid, cap):
    n = h.shape[0]
    tm = MOE_TOKENS
    smem_rows = pl.BlockSpec((ROUTE_ROWS, tm), lambda i, *_: (0, i), memory_space=pltpu.SMEM)
    return pl.pallas_call(
        functools.partial(_dispatch_kernel, n_valid=n_valid, cap=cap),
        grid_spec=pltpu.PrefetchScalarGridSpec(
            num_scalar_prefetch=2,
            grid=(pl.cdiv(n_valid, tm),),
            in_specs=[smem_rows, smem_rows, pl.BlockSpec((tm, D_MODEL), lambda i, *_: (i, 0))],
            out_specs=pl.BlockSpec(memory_space=pl.ANY),
            scratch_shapes=[pltpu.VMEM((SUBLANES, D_MODEL), F32), pltpu.SemaphoreType.DMA, pltpu.SemaphoreType.DMA]),
        out_shape=jax.ShapeDtypeStruct((cap, D_MODEL), F32),
        compiler_params=_cparams(("arbitrary",)),
        name="moe_dispatch",
    )(seg_start, counts, idx_t, pos_t, h)


def _swiglu(x, wg, wu, wd):
    xb = x.astype(BF16)
    g = jnp.dot(xb, wg.astype(BF16), preferred_element_type=F32)
    u = jnp.dot(xb, wu.astype(BF16), preferred_element_type=F32)
    return jnp.dot((_silu(g) * u).astype(BF16), wd.astype(BF16), preferred_element_type=F32)


def _experts_kernel(be_ref, x_ref, wg_ref, wu_ref, wd_ref, y_ref):
    y_ref[...] = _swiglu(x_ref[...], wg_ref[0], wu_ref[0], wd_ref[0])


def _experts(xs, block_expert, w_gate, w_up, w_down):
    cap = xs.shape[0]
    return pl.pallas_call(
        _experts_kernel,
        grid_spec=pltpu.PrefetchScalarGridSpec(
            num_scalar_prefetch=1,
            grid=(cap // MOE_BLOCK,),
            in_specs=[pl.BlockSpec((MOE_BLOCK, D_MODEL), lambda i, be: (i, 0)),
                      pl.BlockSpec((1, D_MODEL, D_EXPERT), lambda i, be: (be[i], 0, 0)),
                      pl.BlockSpec((1, D_MODEL, D_EXPERT), lambda i, be: (be[i], 0, 0)),
                      pl.BlockSpec((1, D_EXPERT, D_MODEL), lambda i, be: (be[i], 0, 0))],
            out_specs=pl.BlockSpec((MOE_BLOCK, D_MODEL), lambda i, be: (i, 0))),
        out_shape=jax.ShapeDtypeStruct((cap, D_MODEL), F32),
        compiler_params=_cparams(("arbitrary",)),
        name="moe_experts",
    )(block_expert, xs, w_gate, w_up, w_down)


def _combine_kernel(start_ref, idx_ref, pos_ref, h_ref, wtok_ref, sg_ref, su_ref, sd_ref, g_ref, b_ref, ys_ref,
                    o_ref, buf_ref, sem):
    tm = MOE_TOKENS

    def gather_row(t, carry):
        for k in range(TOP_K):
            src = start_ref[idx_ref[k, t]] + pos_ref[k, t]
            _row_copy(ys_ref, src, buf_ref.at[k], t, sem).start()
        return carry

    lax.fori_loop(0, tm, gather_row, 0)
    h = h_ref[...]
    f = _swiglu(h, sg_ref[...], su_ref[...], sd_ref[...])

    def wait_row(t, carry):
        for k in range(TOP_K):
            _row_copy(ys_ref, 0, buf_ref.at[k], 0, sem).wait()
        return carry

    lax.fori_loop(0, tm, wait_row, 0)
    wtok = wtok_ref[...]
    acc = jnp.zeros((tm, D_MODEL), F32)
    for k in range(TOP_K):
        acc = acc + wtok[:, k:k + 1] * buf_ref[k]
    o_ref[...] = _layer_norm(DEEPNORM_ALPHA * h + (acc + f), g_ref[...], b_ref[...])


def _combine(h, ys, idx_t, pos_t, wtok, seg_start, sh_gate, sh_up, sh_down, ln_g, ln_b):
    n = h.shape[0]
    tm = MOE_TOKENS
    smem_rows = pl.BlockSpec((ROUTE_ROWS, tm), lambda i, *_: (0, i), memory_space=pltpu.SMEM)
    const = lambda a: pl.BlockSpec(a.shape, lambda i, *_: (0,) * a.ndim)
    return pl.pallas_call(
        _combine_kernel,
        grid_spec=pltpu.PrefetchScalarGridSpec(
            num_scalar_prefetch=1,
            grid=(n // tm,),
            in_specs=[smem_rows, smem_rows, pl.BlockSpec((tm, D_MODEL), lambda i, *_: (i, 0)),
                      pl.BlockSpec((tm, LANES), lambda i, *_: (i, 0)),
                      const(sh_gate), const(sh_up), const(sh_down), const(ln_g), const(ln_b),
                      pl.BlockSpec(memory_space=pl.ANY)],
            out_specs=pl.BlockSpec((tm, D_MODEL), lambda i, *_: (i, 0)),
            scratch_shapes=[pltpu.VMEM((TOP_K, tm, D_MODEL), F32), pltpu.SemaphoreType.DMA]),
        out_shape=jax.ShapeDtypeStruct((n, D_MODEL), F32),
        compiler_params=_cparams(("arbitrary",)),
        name="moe_combine",
    )(seg_start, idx_t, pos_t, h, wtok, sh_gate, sh_up, sh_down, ln_g, ln_b, ys)


def _moe_ln(h, n_valid, router_w, router_bias, w_gate, w_up, w_down, sh_gate, sh_up, sh_down, ln_g, ln_b):
    idx_t, pos_t, wtok, cnt = _route(h, router_w.T.astype(BF16),
                                     jnp.broadcast_to(router_bias.astype(F32)[:, None], (N_EXPERTS, LANES)), n_valid)
    counts = cnt[:, 0].astype(jnp.int32)
    padded = (counts + MOE_BLOCK - 1) // MOE_BLOCK * MOE_BLOCK
    seg_end = jnp.cumsum(padded)
    seg_start = seg_end - padded
    n_blocks = -(-(n_valid * TOP_K + N_EXPERTS * (MOE_BLOCK - 1)) // MOE_BLOCK)
    cap = n_blocks * MOE_BLOCK
    block_first_row = jnp.arange(n_blocks, dtype=jnp.int32) * MOE_BLOCK
    block_expert = jnp.minimum(jnp.sum((seg_end[None, :] <= block_first_row[:, None]).astype(jnp.int32), axis=1),
                               N_EXPERTS - 1)
    xs = _dispatch(h, idx_t, pos_t, seg_start, counts, n_valid, cap)
    ys = _experts(xs, block_expert, w_gate, w_up, w_down)
    return _combine(h, ys, idx_t, pos_t, wtok, seg_start, sh_gate.astype(BF16), sh_up.astype(BF16),
                    sh_down.astype(BF16), ln_g, ln_b)


def _ssd_sample_kernel(xbc_ref, z_ref, dtg_ref, sconv_ref, s0_ref, convw_ref, convb_ref, dtb_ref, alog_ref,
                       dskip_ref, normw_ref, y_ref, s_ref, conv_out_ref, xc_ref, dt_ref, da_ref):
    b = pl.program_id(0)

    @pl.when(b == 0)
    def _():
        xin = xbc_ref[...]
        xc = convw_ref[SSD_CONV - 1:SSD_CONV, :] * xin
        for k in range(SSD_CONV - 1):
            xc = xc + convw_ref[k:k + 1, :] * sconv_ref[k]
        xc_ref[...] = _silu(xc + convb_ref[...])
        dt = _softplus(dtg_ref[...] + dtb_ref[...])
        dt_ref[...] = dt
        da_ref[...] = jnp.exp(dt * (-jnp.exp(alog_ref[...])))
        for k in range(SSD_CONV - 2):
            conv_out_ref[k] = sconv_ref[k + 1]
        conv_out_ref[SSD_CONV - 2] = xin

    xc = xc_ref[pl.ds(b, 1), :]
    dt = dt_ref[pl.ds(b, 1), :]
    da = da_ref[pl.ds(b, 1), :]
    ns = SSD_GROUPS * SSD_STATE
    eye = (lax.broadcasted_iota(jnp.int32, (HEAD_DIM, HEAD_DIM), 0)
           == lax.broadcasted_iota(jnp.int32, (HEAD_DIM, HEAD_DIM), 1))
    hpg = SSD_HEADS // SSD_GROUPS
    y_parts = []
    for h in range(SSD_HEADS):
        g = h // hpg
        x_h = xc[:, h * HEAD_DIM:(h + 1) * HEAD_DIM]
        b_g = xc[:, SSD_D + g * SSD_STATE:SSD_D + (g + 1) * SSD_STATE]
        c_g = xc[:, SSD_D + ns + g * SSD_STATE:SSD_D + ns + (g + 1) * SSD_STATE]
        xdt_col = jnp.sum(jnp.where(eye, x_h * dt[:, h:h + 1], 0.0), axis=1, keepdims=True)
        s_new = da[:, h:h + 1] * s0_ref[0, h] + xdt_col * b_g
        s_ref[0, h] = s_new
        y_h = _bdot_nt(c_g, s_new) + dskip_ref[:, h * HEAD_DIM:(h + 1) * HEAD_DIM] * x_h
        y_parts.append(y_h)
    y = jnp.concatenate(y_parts, axis=1)
    y_ref[pl.ds(b, 1), :] = _gated_group_norm(y, z_ref[pl.ds(b, 1), :], normw_ref[...])


def _ssd_sample(xbc, z, dtg, state_conv_t, state_ssm, conv_w, conv_b, dt_bias_pad, a_log_pad, d_skip_full, norm_w):
    bs = xbc.shape[0]
    const = lambda a: pl.BlockSpec(a.shape, lambda b: (0,) * a.ndim)
    state_spec = pl.BlockSpec((1, SSD_HEADS, HEAD_DIM, SSD_STATE), lambda b: (b, 0, 0, 0))
    return pl.pallas_call(
        _ssd_sample_kernel,
        grid=(bs,),
        in_specs=[const(xbc), const(z), const(dtg), const(state_conv_t), state_spec, const(conv_w), const(conv_b),
                  const(dt_bias_pad), const(a_log_pad), const(d_skip_full), const(norm_w)],
        out_specs=[pl.BlockSpec((bs, SSD_D), lambda b: (0, 0)), state_spec,
                   pl.BlockSpec((SSD_CONV - 1, bs, SSD_CONV_CH), lambda b: (0, 0, 0))],
        out_shape=[jax.ShapeDtypeStruct((bs, SSD_D), F32),
                   jax.ShapeDtypeStruct(state_ssm.shape, F32),
                   jax.ShapeDtypeStruct((SSD_CONV - 1, bs, SSD_CONV_CH), F32)],
        scratch_shapes=[pltpu.VMEM((bs, SSD_CONV_CH), F32), pltpu.VMEM((bs, LANES), F32),
                        pltpu.VMEM((bs, LANES), F32)],
        compiler_params=_cparams(("arbitrary",)),
        name="ssd_sample",
    )(xbc, z, dtg, state_conv_t, state_ssm, conv_w, conv_b, dt_bias_pad, a_log_pad, d_skip_full, norm_w)


PAGES_PER_STEP = 8


def _compress_paged_kernel(pt_ref, *refs, n_pages):
    pages = refs[:PAGES_PER_STEP]
    consts = refs[PAGES_PER_STEP:PAGES_PER_STEP + 7]
    o_ref, kbuf, vbuf = refs[PAGES_PER_STEP + 7:]
    s = pl.program_id(1)
    for j in range(PAGES_PER_STEP):
        r0 = pl.multiple_of((s * PAGES_PER_STEP + j) * PAGE_SIZE, PAGE_SIZE)
        kbuf[pl.ds(r0, PAGE_SIZE), :] = pages[j][0, :, 0:KV_D]
        vbuf[pl.ds(r0, PAGE_SIZE), :] = pages[j][0, :, KV_D:2 * KV_D]

    @pl.when(s == pl.num_programs(1) - 1)
    def _():
        nb = n_pages * PAGE_SIZE // CMP_BLOCK
        o_ref[...] = _compress_rows(kbuf, vbuf, *consts, nb)


def _compress_paged(pool, page_table, consts):
    bs, n_pages = page_table.shape
    nb = n_pages * PAGE_SIZE // CMP_BLOCK
    const = lambda a: pl.BlockSpec(a.shape, lambda b, s, pt: (0,) * a.ndim)

    def page_spec(j):
        return pl.BlockSpec((1, PAGE_SIZE, 2 * KV_D), lambda b, s, pt: (pt[b, s * PAGES_PER_STEP + j], 0, 0))

    return pl.pallas_call(
        functools.partial(_compress_paged_kernel, n_pages=n_pages),
        grid_spec=pltpu.PrefetchScalarGridSpec(
            num_scalar_prefetch=1,
            grid=(bs, n_pages // PAGES_PER_STEP),
            in_specs=[page_spec(j) for j in range(PAGES_PER_STEP)] + [const(a) for a in consts],
            out_specs=pl.BlockSpec((nb, 2 * KV_D), lambda b, s, pt: (b, 0)),
            scratch_shapes=[pltpu.VMEM((n_pages * PAGE_SIZE, KV_D), F32), pltpu.VMEM((n_pages * PAGE_SIZE, KV_D), F32)]),
        out_shape=jax.ShapeDtypeStruct((bs * nb, 2 * KV_D), F32),
        compiler_params=_cparams(("arbitrary", "arbitrary")),
        name="compress_paged",
    )(page_table, *([pool] * PAGES_PER_STEP), *consts)


SEL_PAST = TOP_N - 1


def _head_rows(q_row, hk):
    w = NSA_D // NSA_KV_HEADS
    q = jnp.broadcast_to(q_row[:, hk * w:(hk + 1) * w], (SUBLANES, w))
    own = (lax.broadcasted_iota(jnp.int32, (SUBLANES, w), 1) // HEAD_DIM
           == lax.broadcasted_iota(jnp.int32, (SUBLANES, w), 0))
    return jnp.where(own, q, 0.0), own


def _tile_kv_head(x, hk):
    sw = pltpu.roll(x, HEAD_DIM, 1)
    low = lax.broadcasted_iota(jnp.int32, x.shape, 1) < HEAD_DIM
    t = jnp.where(low, x, sw) if hk == 0 else jnp.where(low, sw, x)
    return jnp.concatenate([t, t], axis=1)


def _nsa_sample_cmp_kernel(qc_ref, cmp_ref, ocmp_ref, idx_ref, *, nc):
    b = pl.program_id(0)
    scale = HEAD_DIM ** -0.5
    q_row = qc_ref[pl.ds(b, 1), :]
    kc = cmp_ref[:, 0:KV_D]
    vc = cmp_ref[:, KV_D:2 * KV_D]
    lane = lax.broadcasted_iota(jnp.int32, (1, LANES), 1)
    o_parts = []
    for hk in range(NSA_KV_HEADS):
        qg, own = _head_rows(q_row, hk)
        s = _bdot_nt(qg, _tile_kv_head(kc, hk)) * scale
        ex = jnp.exp(s - jnp.max(s, axis=-1, keepdims=True))
        p = ex / jnp.sum(ex, axis=-1, keepdims=True)
        o = _bdot(p, _tile_kv_head(vc, hk))
        o_parts.append(jnp.sum(jnp.where(own, o, 0.0), axis=0, keepdims=True))
        hrow = lax.broadcasted_iota(jnp.int32, p.shape, 0) < NSA_HEADS // NSA_KV_HEADS
        imp = jnp.sum(jnp.where(hrow, p, 0.0), axis=0, keepdims=True)
        j = lax.broadcasted_iota(jnp.int32, (1, nc), 1)
        score = jnp.where((j == 0) | (j == nc - 1), FORCED_SCORE, imp)
        score_col = jnp.concatenate([score, jnp.zeros((LANES - 1, nc), F32)], 0).T[:, 0:1]
        jc = lax.broadcasted_iota(jnp.int32, (nc, nc), 0)
        jr = lax.broadcasted_iota(jnp.int32, (nc, nc), 1)
        beats = (score_col > score) | ((score_col == score) & (jc < jr))
        rank = jnp.sum(beats.astype(F32), axis=0, keepdims=True)
        row = jnp.zeros((1, LANES), F32)
        jf = j.astype(F32)
        for k in range(SEL_PAST):
            blk = jnp.sum(jnp.where(rank == k, jf, 0.0), axis=1, keepdims=True)
            row = jnp.where(lane == k, blk, row)
        idx_ref[pl.ds(b * NSA_KV_HEADS + hk, 1), :] = row.astype(jnp.int32)
    ocmp_ref[pl.ds(b, 1), :] = jnp.concatenate(o_parts, axis=1)


def _nsa_sample_cmp(qc, kvcmp, bs):
    nc = kvcmp.shape[0] // bs
    return pl.pallas_call(
        functools.partial(_nsa_sample_cmp_kernel, nc=nc),
        grid=(bs,),
        in_specs=[pl.BlockSpec((bs, NSA_D), lambda b: (0, 0)), pl.BlockSpec((nc, 2 * KV_D), lambda b: (b, 0))],
        out_specs=[pl.BlockSpec((bs, NSA_D), lambda b: (0, 0)),
                   pl.BlockSpec((bs * NSA_KV_HEADS, LANES), lambda b: (0, 0))],
        out_shape=[jax.ShapeDtypeStruct((bs, NSA_D), F32),
                   jax.ShapeDtypeStruct((bs * NSA_KV_HEADS, LANES), jnp.int32)],
        compiler_params=_cparams(("arbitrary",)),
        name="nsa_sample_cmp",
    )(qc, kvcmp)


def _sel_block_copy(pool_ref, pt_ref, sel_ref, buf_ref, sem, b, hk, k):
    per_page = PAGE_SIZE // SEL_BLOCK
    blk = sel_ref[b * NSA_KV_HEADS + hk, k]
    page = pt_ref[b, lax.shift_right_logical(blk, int(math.log2(per_page)))]
    row0 = pl.multiple_of((blk & (per_page - 1)) * SEL_BLOCK, SEL_BLOCK)
    return pltpu.make_async_copy(pool_ref.at[page, pl.ds(row0, SEL_BLOCK)], buf_ref.at[hk * SEL_PAST + k], sem)


def _nsa_sample_attn_kernel(pt_ref, sel_ref, qr_ref, new_sel_ref, new_win_ref, win_ref, dtg_ref, ocmp_ref, pool_ref,
                            o_ref, buf_ref, sem):
    b = pl.program_id(0)
    for hk in range(NSA_KV_HEADS):
        for k in range(SEL_PAST):
            _sel_block_copy(pool_ref, pt_ref, sel_ref, buf_ref, sem, b, hk, k).start()
    for hk in range(NSA_KV_HEADS):
        for k in range(SEL_PAST):
            _sel_block_copy(pool_ref, pt_ref, sel_ref, buf_ref, sem, b, hk, k).wait()
    scale = HEAD_DIM ** -0.5
    q_row = qr_ref[pl.ds(b, 1), :]
    sig = _sigmoid(dtg_ref[pl.ds(b, 1), :])
    new_sel = new_sel_ref[pl.ds(b, 1), :]
    new_win = new_win_ref[pl.ds(b, 1), :]
    win = win_ref[0]
    o_slc, o_win = [], []
    for hk in range(NSA_KV_HEADS):
        qg, own = _head_rows(q_row, hk)
        qb = qg.astype(BF16)

        def attend(kv_rows, kv_new, n_new):
            s = lax.dot_general(qb, _tile_kv_head(kv_rows[:, 0:KV_D], hk).astype(BF16), (((1,), (1,)), ((), ())),
                                preferred_element_type=F32) * scale
            k_new = _tile_kv_head(kv_new[:, 0:KV_D], hk).astype(BF16).astype(F32)
            s_new = jnp.sum(qb.astype(F32) * k_new, axis=1, keepdims=True) * scale
            m = jnp.maximum(jnp.max(s, axis=-1, keepdims=True), s_new)
            ex = jnp.exp(s - m)
            ex_new = jnp.exp(s_new - m) * n_new
            den = jnp.sum(ex, axis=-1, keepdims=True) + ex_new
            v_new = _tile_kv_head(kv_new[:, KV_D:2 * KV_D], hk).astype(BF16).astype(F32)
            o = _bdot(ex / den, _tile_kv_head(kv_rows[:, KV_D:2 * KV_D], hk))
            o = o + (ex_new / den).astype(BF16).astype(F32) * v_new
            return jnp.sum(jnp.where(own, o, 0.0), axis=0, keepdims=True)

        past = jnp.concatenate([buf_ref[hk * SEL_PAST + k] for k in range(SEL_PAST)], axis=0)
        o_slc.append(attend(past, new_sel, float(SEL_BLOCK)))
        o_win.append(attend(win, new_win, 1.0))
    o_slc = jnp.concatenate(o_slc, axis=1)
    o_win = jnp.concatenate(o_win, axis=1)
    gates = []
    for br in range(3):
        gates.append(jnp.concatenate(
            [jnp.broadcast_to(sig[:, GATE_COL0 + h * 3 + br:GATE_COL0 + h * 3 + br + 1], (1, HEAD_DIM))
             for h in range(NSA_HEADS)], axis=1))
    o_ref[pl.ds(b, 1), :] = gates[0] * ocmp_ref[pl.ds(b, 1), :] + gates[1] * o_slc + gates[2] * o_win


def _nsa_sample_attn(qr, new_sel, new_win, buf_win, dtg, o_cmp, pool_sel, page_table, sel_idx):
    bs = qr.shape[0]
    const = lambda a: pl.BlockSpec(a.shape, lambda b, pt, sel: (0,) * a.ndim)
    return pl.pallas_call(
        _nsa_sample_attn_kernel,
        grid_spec=pltpu.PrefetchScalarGridSpec(
            num_scalar_prefetch=2,
            grid=(bs,),
            in_specs=[const(qr), const(new_sel), const(new_win),
                      pl.BlockSpec((1,) + buf_win.shape[1:], lambda b, pt, sel: (b, 0, 0)),
                      const(dtg), const(o_cmp), pl.BlockSpec(memory_space=pl.ANY)],
            out_specs=pl.BlockSpec((bs, NSA_D), lambda b, pt, sel: (0, 0)),
            scratch_shapes=[pltpu.VMEM((NSA_KV_HEADS * SEL_PAST, SEL_BLOCK, 2 * KV_D), F32),
                            pltpu.SemaphoreType.DMA]),
        out_shape=jax.ShapeDtypeStruct((bs, NSA_D), F32),
        compiler_params=_cparams(("arbitrary",)),
        name="nsa_sample_attn",
    )(page_table, sel_idx, qr, new_sel, new_win, buf_win, dtg, o_cmp, pool_sel)


def kernel(x_prompt, x_sample, cache_kv_cmp, cache_kv_sel, page_table, cache_kv_win, state_ssm, state_conv,
           emb_ln_g, emb_ln_b, w_in, conv_w, conv_b, dt_bias, a_log, d_skip, ssd_norm_w,
           cmp_pe, cmp_w1, cmp_b1, cmp_w2, cmp_b2, w_out, ln1_g, ln1_b,
           router_w, router_bias, exp_w_gate, exp_w_up, exp_w_down,
           sh_w_gate, sh_w_up, sh_w_down, ln2_g, ln2_b):
    bp, tp, _ = x_prompt.shape
    bs, ts, _ = x_sample.shape
    assert ts == 1 and DEPTH == 1
    n_prompt = bp * tp
    past_len = page_table.shape[1] * PAGE_SIZE
    l = 0
    w_perm = _permute_w_in(w_in[l])
    ln0_g, ln0_b = emb_ln_g[None], emb_ln_b[None]
    ssd_consts = (conv_w[l], conv_b[l][None], _pad_lanes(dt_bias[l]), _pad_lanes(a_log[l]),
                  jnp.repeat(d_skip[l], HEAD_DIM)[None], ssd_norm_w[l][None])
    cmp_consts = _compress_consts(cmp_pe[l], cmp_w1[l], cmp_b1[l], cmp_w2[l], cmp_b2[l])
    w_o = w_out[l].astype(BF16)
    w_o_ssd, w_o_nsa = w_o[:SSD_D], w_o[SSD_D:]
    ln1 = (ln1_g[l][None], ln1_b[l][None])
    kv_shape = (2, NSA_KV_HEADS, HEAD_DIM)

    hp, z, xbc, qc, qr, kvc, kvs, kvw, dtg = _inproj(
        x_prompt.reshape(n_prompt, D_MODEL), ln0_g, ln0_b, w_perm, _rope_tables(jnp.arange(tp)), 256)
    y_ssd, ssm_p, conv_p = _ssd_prompt(xbc, z, dtg, *ssd_consts, bp, tp)
    kvcmp = _compress_prompt(kvc, cmp_consts, tp)
    y_nsa = _nsa_prompt(qc, qr, dtg, kvcmp, kvs, kvw, bp, tp)
    h1p = _outproj(y_ssd, y_nsa, hp, w_o_ssd, w_o_nsa, *ln1, 256)
    n_keep = min(WINDOW, tp)
    kvc_p = kvc.reshape((1, bp, tp) + kv_shape)
    kvs_p = kvs.reshape((1, bp, tp) + kv_shape)
    kvw_p = kvw.reshape((1, bp, tp) + kv_shape)[:, :, tp - n_keep:]

    s_hs, s_z, s_xbc, s_qc, s_qr, s_kvc, s_kvs, s_kvw, s_dtg = _inproj(
        x_sample.reshape(bs, D_MODEL), ln0_g, ln0_b, w_perm, _rope_tables(jnp.full((bs,), past_len)), bs)
    s_y_ssd, ssm_s, conv_s_t = _ssd_sample(s_xbc, s_z, s_dtg, jnp.swapaxes(state_conv[l], 0, 1), state_ssm[l],
                                           *ssd_consts)
    n_pool = cache_kv_cmp.shape[1]
    s_kvcmp = _compress_paged(cache_kv_cmp[l].reshape(n_pool, PAGE_SIZE, 2 * KV_D), page_table, cmp_consts)
    s_o_cmp, s_sel = _nsa_sample_cmp(s_qc, s_kvcmp, bs)
    buf_win = cache_kv_win[l].reshape(bs, -1, 2 * KV_D)
    s_y_nsa = _nsa_sample_attn(
        s_qr, s_kvs, s_kvw, buf_win, s_dtg, s_o_cmp,
        cache_kv_sel[l].reshape(n_pool, PAGE_SIZE, 2 * KV_D), page_table, s_sel)
    h1s = _outproj(s_y_ssd, s_y_nsa, s_hs, w_o_ssd, w_o_nsa, *ln1, bs)
    win_all = jnp.concatenate([buf_win, s_kvw[:, None, :]], 1)
    n_keep_s = min(WINDOW, past_len + ts)
    kvw_s = win_all[:, win_all.shape[1] - n_keep_s:].reshape((1, bs, n_keep_s) + kv_shape)
    kvc_s = s_kvc.reshape((1, bs, ts) + kv_shape)
    kvs_s = s_kvs.reshape((1, bs, ts) + kv_shape)

    n_tok = n_prompt + bs * ts
    n_pad = -(-n_tok // MOE_TOKENS) * MOE_TOKENS
    tok = jnp.concatenate([h1p, h1s, jnp.zeros((n_pad - n_tok, D_MODEL), F32)], 0)
    out = _moe_ln(tok, n_tok, router_w[l], router_bias[l], exp_w_gate[l], exp_w_up[l], exp_w_down[l],
                  sh_w_gate[l], sh_w_up[l], sh_w_down[l], ln2_g[l][None], ln2_b[l][None])
    y_prompt = out[:n_prompt].reshape(bp, tp, D_MODEL)
    y_sample = out[n_prompt:n_tok].reshape(bs, ts, D_MODEL)
    return (y_prompt, y_sample, kvc_p, kvs_p, kvw_p, ssm_p[None], conv_p[None],
            kvc_s, kvs_s, kvw_s, ssm_s[None], jnp.swapaxes(conv_s_t, 0, 1)[None])
```

```python
import functools
import math

import jax
import jax.numpy as jnp
import numpy as np
from jax import lax
from jax.experimental import pallas as pl
from jax.experimental.pallas import tpu as pltpu

D_MODEL = 1024
HEAD_DIM = 64
SSD_HEADS = 8
SSD_D = SSD_HEADS * HEAD_DIM
SSD_GROUPS = 2
SSD_STATE = 128
SSD_CONV = 4
SSD_CONV_CH = SSD_D + 2 * SSD_GROUPS * SSD_STATE
SSD_CHUNK = 128
NSA_HEADS = 8
NSA_KV_HEADS = 2
NSA_D = NSA_HEADS * HEAD_DIM
KV_D = NSA_KV_HEADS * HEAD_DIM
CMP_BLOCK = 64
CMP_HIDDEN = 128
SEL_BLOCK = 64
TOP_N = 16
WINDOW = 512
Q_BLOCK = 128
ROT_DIM = HEAD_DIM // 4
ROPE_THETA = 500000.0
N_EXPERTS = 64
TOP_K = 6
N_EXPERT_GROUPS = 8
EXPERTS_PER_GROUP = N_EXPERTS // N_EXPERT_GROUPS
TOPK_GROUPS = 4
D_EXPERT = 256
D_SHARED = 256
ROUTED_SCALE = 2.5
MOE_BLOCK = 256
DEPTH = 1
DEEPNORM_ALPHA = (2.0 * DEPTH) ** 0.25
LN_EPS = 1e-5
RMS_EPS = 1e-5
NEG = -1e30
FORCED_SCORE = 1e4
PAGE_SIZE = 128

LANES = 128
SUBLANES = 8
VMEM_LIMIT_BYTES = 56 * 1024 * 1024

U_Z = 0
U_XBC = U_Z + SSD_D
U_Q = U_XBC + SSD_CONV_CH
U_KVC = U_Q + NSA_D
U_KVS = U_KVC + 2 * KV_D
U_KVW = U_KVS + 2 * KV_D
U_DTG = U_KVW + 2 * KV_D
U_TOTAL = U_DTG + LANES
GATE_COL0 = SSD_HEADS

BF16 = jnp.bfloat16
F32 = jnp.float32


def _cparams(sem):
    return pltpu.CompilerParams(dimension_semantics=sem, vmem_limit_bytes=VMEM_LIMIT_BYTES)


def _bdot(a, b):
    return jnp.dot(a.astype(BF16), b.astype(BF16), preferred_element_type=F32)


def _bdot_nt(a, b):
    return lax.dot_general(a.astype(BF16), b.astype(BF16), (((1,), (1,)), ((), ())),
                           preferred_element_type=F32)


def _hdot(a, b):
    return jnp.dot(a, b, preferred_element_type=F32, precision=lax.Precision.HIGHEST)


def _sigmoid(x):
    return 1.0 / (1.0 + jnp.exp(-x))


def _silu(x):
    return x * _sigmoid(x)


def _layer_norm(x, g, b):
    mu = jnp.mean(x, axis=-1, keepdims=True)
    xc = x - mu
    var = jnp.mean(xc * xc, axis=-1, keepdims=True)
    return xc * lax.rsqrt(var + LN_EPS) * g + b


def _rope_tile(x, cos, sa, sb):
    return x * cos + pltpu.roll(x, LANES - ROT_DIM // 2, 1) * sa + pltpu.roll(x, ROT_DIM // 2, 1) * sb


def _inproj_kernel(x_ref, g_ref, b_ref, w_ref, rope_ref,
                   h_ref, z_ref, xbc_ref, qc_ref, qr_ref, kvc_ref, kvs_ref, kvw_ref, dtg_ref):
    h = _layer_norm(x_ref[...], g_ref[...], b_ref[...])
    h_ref[...] = h
    u = jnp.dot(h.astype(BF16), w_ref[...], preferred_element_type=F32)
    cos = rope_ref[:, 0:LANES]
    sa = rope_ref[:, LANES:2 * LANES]
    sb = rope_ref[:, 2 * LANES:3 * LANES]
    z_ref[...] = u[:, U_Z:U_XBC]
    xbc_ref[...] = u[:, U_XBC:U_Q]
    qc_ref[...] = u[:, U_Q:U_KVC]
    for c in range(NSA_D // LANES):
        qr_ref[:, c * LANES:(c + 1) * LANES] = _rope_tile(u[:, U_Q + c * LANES:U_Q + (c + 1) * LANES], cos, sa, sb)
    kvc_ref[...] = u[:, U_KVC:U_KVS]
    kvs_ref[:, 0:KV_D] = _rope_tile(u[:, U_KVS:U_KVS + KV_D], cos, sa, sb)
    kvs_ref[:, KV_D:2 * KV_D] = u[:, U_KVS + KV_D:U_KVW]
    kvw_ref[:, 0:KV_D] = _rope_tile(u[:, U_KVW:U_KVW + KV_D], cos, sa, sb)
    kvw_ref[:, KV_D:2 * KV_D] = u[:, U_KVW + KV_D:U_DTG]
    dtg_ref[...] = u[:, U_DTG:U_TOTAL]


def _rope_tables(pos):
    half = ROT_DIM // 2
    inv = ROPE_THETA ** (-jnp.arange(half, dtype=F32) / half)
    ang = pos.astype(F32)[:, None] * inv
    cos, sin = jnp.cos(ang), jnp.sin(ang)
    ones = jnp.ones((pos.shape[0], HEAD_DIM - ROT_DIM), F32)
    zeros = jnp.zeros((pos.shape[0], HEAD_DIM - ROT_DIM), F32)
    zh = jnp.zeros_like(sin)
    c = jnp.concatenate([cos, cos, ones], 1)
    sa = jnp.concatenate([-sin, zh, zeros], 1)
    sb = jnp.concatenate([zh, sin, zeros], 1)
    return jnp.concatenate([jnp.tile(t, (1, LANES // HEAD_DIM)) for t in (c, sa, sb)], 1)


def _permute_w_in(w):
    sizes = (SSD_D, SSD_CONV_CH, SSD_HEADS, NSA_D, KV_D, KV_D, KV_D, KV_D, KV_D, KV_D, 3 * NSA_HEADS)
    offs = np.concatenate([[0], np.cumsum(sizes)])
    seg = [w[:, offs[i]:offs[i + 1]] for i in range(len(sizes))]
    pad = jnp.zeros((w.shape[0], LANES - SSD_HEADS - 3 * NSA_HEADS), w.dtype)
    out = jnp.concatenate([seg[0], seg[1], seg[3], seg[4], seg[5], seg[6], seg[7], seg[8], seg[9],
                           seg[2], seg[10], pad], 1)
    return out.astype(BF16)


def _inproj(x, ln_g, ln_b, w_perm, rope_tab, tm):
    n = x.shape[0]
    nt = n // tm
    n_rope_blocks = rope_tab.shape[0] // tm
    row = lambda w: pl.BlockSpec((tm, w), lambda i: (i, 0))
    const = lambda a: pl.BlockSpec(a.shape, lambda i: (0,) * a.ndim)
    widths = (D_MODEL, SSD_D, SSD_CONV_CH, NSA_D, NSA_D, 2 * KV_D, 2 * KV_D, 2 * KV_D, LANES)
    return pl.pallas_call(
        _inproj_kernel,
        grid=(nt,),
        in_specs=[row(D_MODEL), const(ln_g), const(ln_b), const(w_perm),
                  pl.BlockSpec((tm, 3 * LANES), lambda i: (i % n_rope_blocks, 0))],
        out_specs=[row(w) for w in widths],
        out_shape=[jax.ShapeDtypeStruct((n, w), F32) for w in widths],
        compiler_params=_cparams(("parallel",)),
        name="inproj",
    )(x, ln_g, ln_b, w_perm, rope_tab)


def _softplus(x):
    return jnp.maximum(x, 0.0) + jnp.log1p(jnp.exp(-jnp.abs(x)))


def _gated_group_norm(y, z, norm_w):
    y = y * _silu(z)
    gw = SSD_D // SSD_GROUPS
    parts = []
    for g in range(SSD_GROUPS):
        yg = y[:, g * gw:(g + 1) * gw]
        ms = jnp.mean(yg * yg, axis=-1, keepdims=True)
        parts.append(yg * lax.rsqrt(ms + RMS_EPS))
    return jnp.concatenate(parts, axis=1) * norm_w


def _ssd_prompt_kernel(xbc_ref, z_ref, dtg_ref, convw_ref, convb_ref, dtb_ref, alog_ref, dskip_ref, normw_ref,
                       y_ref, state_ref, conv_ref, ext_ref, s_ref):
    c = pl.program_id(1)
    nc = pl.num_programs(1)
    L = SSD_CHUNK
    halo = SUBLANES

    @pl.when(c == 0)
    def _():
        ext_ref[0:halo, :] = jnp.zeros((halo, SSD_CONV_CH), F32)
        s_ref[...] = jnp.zeros_like(s_ref)

    xin = xbc_ref[...]
    ext_ref[halo:halo + L, :] = xin
    xc = convw_ref[SSD_CONV - 1:SSD_CONV, :] * xin
    for k in range(SSD_CONV - 1):
        off = halo - (SSD_CONV - 1) + k
        xc = xc + convw_ref[k:k + 1, :] * ext_ref[off:off + L, :]
    ext_ref[0:halo, :] = ext_ref[L:L + halo, :]
    xc = _silu(xc + convb_ref[...])
    xs = xc[:, 0:SSD_D]
    ns = SSD_GROUPS * SSD_STATE
    bm = xc[:, SSD_D:SSD_D + ns]
    cm = xc[:, SSD_D + ns:SSD_D + 2 * ns]

    dt = _softplus(dtg_ref[...] + dtb_ref[...])
    da = dt * (-jnp.exp(alog_ref[...]))
    row = lax.broadcasted_iota(jnp.int32, (L, L), 0)
    col = lax.broadcasted_iota(jnp.int32, (L, L), 1)
    tril = row >= col
    acum = _hdot(tril.astype(F32), da)
    acum_t = acum.T
    eacum = jnp.exp(acum)
    alast = acum[L - 1:L, :]
    edecay = jnp.exp(alast - acum)
    elast = jnp.exp(alast)

    dt_full = jnp.concatenate([jnp.broadcast_to(dt[:, h:h + 1], (L, HEAD_DIM)) for h in range(SSD_HEADS)], 1)
    dec_full = jnp.concatenate([jnp.broadcast_to(edecay[:, h:h + 1], (L, HEAD_DIM)) for h in range(SSD_HEADS)], 1)
    xdt = xs * dt_full
    xdec_t = (xdt * dec_full).T

    hpg = SSD_HEADS // SSD_GROUPS
    y_parts = []
    for h in range(SSD_HEADS):
        g = h // hpg
        b_g = bm[:, g * SSD_STATE:(g + 1) * SSD_STATE]
        c_g = cm[:, g * SSD_STATE:(g + 1) * SSD_STATE]
        if h % hpg == 0:
            cb = _bdot_nt(c_g, b_g)
        seg = acum[:, h:h + 1] - acum_t[h:h + 1, :]
        lmat = jnp.where(tril, jnp.exp(jnp.where(tril, seg, 0.0)), 0.0)
        xdt_h = xdt[:, h * HEAD_DIM:(h + 1) * HEAD_DIM]
        y_h = _bdot(cb * lmat, xdt_h)
        s_prev = s_ref[h]
        y_h = y_h + _bdot_nt(c_g, s_prev) * eacum[:, h:h + 1]
        y_h = y_h + dskip_ref[:, h * HEAD_DIM:(h + 1) * HEAD_DIM] * xs[:, h * HEAD_DIM:(h + 1) * HEAD_DIM]
        y_parts.append(y_h)
        s_ref[h] = elast[:, h:h + 1] * s_prev + _bdot(xdec_t[h * HEAD_DIM:(h + 1) * HEAD_DIM, :], b_g)
    y = jnp.concatenate(y_parts, axis=1)
    y_ref[...] = _gated_group_norm(y, z_ref[...], normw_ref[...])

    @pl.when(c == nc - 1)
    def _():
        state_ref[0] = s_ref[...]
        conv_ref[0] = xin[L - (SSD_CONV - 1):L, :]


def _ssd_prompt(xbc, z, dtg, conv_w, conv_b, dt_bias_pad, a_log_pad, d_skip_full, norm_w, bn, t):
    nc = t // SSD_CHUNK
    row = lambda w: pl.BlockSpec((SSD_CHUNK, w), lambda b, c: (b * nc + c, 0))
    const = lambda a: pl.BlockSpec(a.shape, lambda b, c: (0,) * a.ndim)
    return pl.pallas_call(
        _ssd_prompt_kernel,
        grid=(bn, nc),
        in_specs=[row(SSD_CONV_CH), row(SSD_D), row(LANES), const(conv_w), const(conv_b), const(dt_bias_pad),
                  const(a_log_pad), const(d_skip_full), const(norm_w)],
        out_specs=[row(SSD_D),
                   pl.BlockSpec((1, SSD_HEADS, HEAD_DIM, SSD_STATE), lambda b, c: (b, 0, 0, 0)),
                   pl.BlockSpec((1, SSD_CONV - 1, SSD_CONV_CH), lambda b, c: (b, 0, 0))],
        out_shape=[jax.ShapeDtypeStruct((bn * t, SSD_D), F32),
                   jax.ShapeDtypeStruct((bn, SSD_HEADS, HEAD_DIM, SSD_STATE), F32),
                   jax.ShapeDtypeStruct((bn, SSD_CONV - 1, SSD_CONV_CH), F32)],
        scratch_shapes=[pltpu.VMEM((SSD_CHUNK + 2 * SUBLANES, SSD_CONV_CH), F32),
                        pltpu.VMEM((SSD_HEADS, HEAD_DIM, SSD_STATE), F32)],
        compiler_params=_cparams(("parallel", "arbitrary")),
        name="ssd_prompt",
    )(xbc, z, dtg, conv_w, conv_b, dt_bias_pad, a_log_pad, d_skip_full, norm_w)


def _pad_lanes(v, fill=0.0):
    return jnp.concatenate([v.astype(F32), jnp.full((LANES - v.shape[0],), fill, F32)])[None]


def _compress_rows(k_ref, v_ref, pe_ref, w1k_ref, w1v_ref, b1_ref, w2k_ref, w2v_ref, b2_ref, nb):
    acck = jnp.zeros((nb, 2 * CMP_HIDDEN), F32)
    accv = jnp.zeros((nb, 2 * CMP_HIDDEN), F32)
    for l in range(CMP_BLOCK):
        xk = k_ref[pl.ds(l, nb, stride=CMP_BLOCK), :] + pe_ref[l:l + 1, 0:KV_D]
        xv = v_ref[pl.ds(l, nb, stride=CMP_BLOCK), :] + pe_ref[l:l + 1, KV_D:2 * KV_D]
        acck = acck + jnp.dot(xk.astype(BF16), w1k_ref[l], preferred_element_type=F32)
        accv = accv + jnp.dot(xv.astype(BF16), w1v_ref[l], preferred_element_type=F32)
    hk = _silu(acck + b1_ref[:, 0:2 * CMP_HIDDEN])
    hv = _silu(accv + b1_ref[:, 2 * CMP_HIDDEN:4 * CMP_HIDDEN])
    ok = jnp.dot(hk.astype(BF16), w2k_ref[...], preferred_element_type=F32) + b2_ref[:, 0:KV_D]
    ov = jnp.dot(hv.astype(BF16), w2v_ref[...], preferred_element_type=F32) + b2_ref[:, KV_D:2 * KV_D]
    return jnp.concatenate([ok, ov], axis=1)


def _compress_kernel(k_ref, v_ref, pe_ref, w1k_ref, w1v_ref, b1_ref, w2k_ref, w2v_ref, b2_ref, o_ref, *, nb):
    o_ref[...] = _compress_rows(k_ref, v_ref, pe_ref, w1k_ref, w1v_ref, b1_ref, w2k_ref, w2v_ref, b2_ref, nb)


def _block_diag2(w):
    z = jnp.zeros_like(w)
    return jnp.concatenate([jnp.concatenate([w, z], -1), jnp.concatenate([z, w], -1)], -2)


def _compress_consts(cmp_pe, cmp_w1, cmp_b1, cmp_w2, cmp_b2):
    pe = jnp.concatenate([cmp_pe[0], cmp_pe[0], cmp_pe[1], cmp_pe[1]], -1)
    w1k = _block_diag2(cmp_w1[0]).astype(BF16)
    w1v = _block_diag2(cmp_w1[1]).astype(BF16)
    b1 = jnp.concatenate([cmp_b1[0], cmp_b1[0], cmp_b1[1], cmp_b1[1]])[None]
    w2k = _block_diag2(cmp_w2[0]).astype(BF16)
    w2v = _block_diag2(cmp_w2[1]).astype(BF16)
    b2 = jnp.concatenate([cmp_b2[0], cmp_b2[0], cmp_b2[1], cmp_b2[1]])[None]
    return pe, w1k, w1v, b1, w2k, w2v, b2


def _compress_prompt(kvc, consts, rows_per_step):
    n = kvc.shape[0]
    nb = rows_per_step // CMP_BLOCK
    const = lambda a: pl.BlockSpec(a.shape, lambda i: (0,) * a.ndim)
    return pl.pallas_call(
        functools.partial(_compress_kernel, nb=nb),
        grid=(n // rows_per_step,),
        in_specs=[pl.BlockSpec((rows_per_step, KV_D), lambda i: (i, 0)),
                  pl.BlockSpec((rows_per_step, KV_D), lambda i: (i, 1))] + [const(a) for a in consts],
        out_specs=pl.BlockSpec((nb, 2 * KV_D), lambda i: (i, 0)),
        out_shape=jax.ShapeDtypeStruct((n // CMP_BLOCK, 2 * KV_D), F32),
        compiler_params=_cparams(("parallel",)),
        name="compress_prompt",
    )(kvc, kvc, *consts)


SEL_KEY_TILE = 256
WIN_KEYS = WINDOW + Q_BLOCK


def _dup_head(x, hk):
    sw = pltpu.roll(x, HEAD_DIM, 1)
    low = lax.broadcasted_iota(jnp.int32, x.shape, 1) < HEAD_DIM
    return jnp.where(low, x, sw) if hk == 0 else jnp.where(low, sw, x)


def _masked_softmax(s, mask):
    sm = jnp.where(mask, s, NEG)
    ex = jnp.where(mask, jnp.exp(sm - jnp.max(sm, axis=-1, keepdims=True)), 0.0)
    den = jnp.sum(ex, axis=-1, keepdims=True)
    return ex / jnp.where(den > 0.0, den, 1.0)


def _select_blocks(imp, cur, n_top):
    n_cand = imp.shape[1]
    j = lax.broadcasted_iota(jnp.int32, imp.shape, 1)
    future = j > cur
    forced = (j == 0) | (j == cur) | (j == cur - 1)
    score = jnp.where(future, NEG, jnp.where(forced, FORCED_SCORE, imp))
    rank = jnp.zeros(imp.shape, F32)
    for c in range(n_cand):
        col = score[:, c:c + 1]
        beats = (col > score) | ((col == score) & (j > c))
        rank = rank + beats.astype(F32)
    return ((rank < n_top) & (score > 0.5 * NEG)).astype(F32)


def _nsa_prompt_kernel(qc_ref, qr_ref, dtg_ref, cmp_ref, kvs_ref, kvw_ref, o_ref,
                       cmp_d, kvs_d, kvw_d, bias_ref, m_ref, l_ref, acc_ref, *, t):
    qb = pl.program_id(1)
    nbk = t // SEL_BLOCK
    tq = Q_BLOCK
    tk = SEL_KEY_TILE
    hpg = NSA_HEADS // NSA_KV_HEADS
    scale = HEAD_DIM ** -0.5

    @pl.when(qb == 0)
    def _():
        for src, dst in ((cmp_ref, cmp_d), (kvs_ref, kvs_d), (kvw_ref, kvw_d)):
            x = src[...]
            for hk in range(NSA_KV_HEADS):
                dst[hk, :, 0:KV_D] = _dup_head(x[:, 0:KV_D], hk).astype(BF16)
                dst[hk, :, KV_D:2 * KV_D] = _dup_head(x[:, KV_D:2 * KV_D], hk).astype(BF16)

    t0 = qb * tq
    rows = t0 + lax.broadcasted_iota(jnp.int32, (tq, 1), 0)
    lane = lax.broadcasted_iota(jnp.int32, (tq, LANES), 1)
    half_mask = (lane < HEAD_DIM, lane >= HEAD_DIM)
    sig = _sigmoid(dtg_ref[...])
    cidx = lax.broadcasted_iota(jnp.int32, (tq, nbk), 1)
    vis = (cidx + 1) * CMP_BLOCK - 1 <= rows
    expand = (lax.broadcasted_iota(jnp.int32, (nbk, t), 1) // SEL_BLOCK
              == lax.broadcasted_iota(jnp.int32, (nbk, t), 0)).astype(BF16)
    win_start = pl.multiple_of(jnp.maximum(t0 - WINDOW, 0), tq)
    wpos = win_start + lax.broadcasted_iota(jnp.int32, (tq, WIN_KEYS), 1)
    win_bias = jnp.where((wpos <= rows) & (wpos >= rows - WINDOW), 0.0, NEG)
    n_kt = (t0 + tq + tk - 1) // tk

    def stack_heads(ref, hk):
        parts = []
        for hh in range(hpg):
            head = hk * hpg + hh
            p, e = head // 2, head % 2
            parts.append(jnp.where(half_mask[e], ref[:, p * LANES:(p + 1) * LANES] * scale, 0.0))
        return jnp.concatenate(parts, axis=0).astype(BF16)

    for hk in range(NSA_KV_HEADS):
        qcs = stack_heads(qc_ref, hk)
        s = lax.dot_general(qcs, cmp_d[hk, :, 0:KV_D], (((1,), (1,)), ((), ())), preferred_element_type=F32)
        pc = _masked_softmax(s.reshape(hpg, tq, nbk), vis[None])
        imp = jnp.sum(pc, axis=0)
        o_cmp = jnp.dot(pc.reshape(hpg * tq, nbk).astype(BF16), cmp_d[hk, :, KV_D:2 * KV_D],
                        preferred_element_type=F32).reshape(hpg, tq, LANES)

        sel = _select_blocks(imp, rows // SEL_BLOCK, TOP_N)
        selk = jnp.dot(sel.astype(BF16), expand, preferred_element_type=F32)
        for kt in range(t // tk):
            kpos = kt * tk + lax.broadcasted_iota(jnp.int32, (tq, tk), 1)
            bias_ref[kt] = jnp.where((selk[:, kt * tk:(kt + 1) * tk] > 0.5) & (kpos <= rows), 0.0, NEG)

        qrs = stack_heads(qr_ref, hk)
        m_ref[...] = jnp.full(m_ref.shape, NEG, F32)
        l_ref[...] = jnp.zeros(l_ref.shape, F32)
        acc_ref[...] = jnp.zeros(acc_ref.shape, F32)

        def sel_step(kt, carry):
            k0 = pl.multiple_of(kt * tk, tk)
            kblk = kvs_d[hk, pl.ds(k0, tk), 0:KV_D]
            vblk = kvs_d[hk, pl.ds(k0, tk), KV_D:2 * KV_D]
            s = lax.dot_general(qrs, kblk, (((1,), (1,)), ((), ())), preferred_element_type=F32)
            s = s.reshape(hpg, tq, tk) + bias_ref[kt][None]
            m_old = m_ref[...]
            m_new = jnp.maximum(m_old, jnp.max(s, axis=-1, keepdims=True))
            alpha = jnp.exp(m_old - m_new)
            pe = jnp.exp(s - jnp.concatenate([m_new] * (tk // LANES), axis=-1))
            l_ref[...] = alpha * l_ref[...] + jnp.sum(pe, axis=-1, keepdims=True)
            pv = jnp.dot(pe.reshape(hpg * tq, tk).astype(BF16), vblk, preferred_element_type=F32)
            acc_ref[...] = alpha * acc_ref[...] + pv.reshape(hpg, tq, LANES)
            m_ref[...] = m_new
            return carry

        lax.fori_loop(0, n_kt, sel_step, 0)
        o_slc = acc_ref[...] / l_ref[...]

        kw = kvw_d[hk, pl.ds(win_start, WIN_KEYS), 0:KV_D]
        vw = kvw_d[hk, pl.ds(win_start, WIN_KEYS), KV_D:2 * KV_D]
        sw = lax.dot_general(qrs, kw, (((1,), (1,)), ((), ())), preferred_element_type=F32)
        sw = sw.reshape(hpg, tq, WIN_KEYS) + win_bias[None]
        pw = jnp.exp(sw - jnp.max(sw, axis=-1, keepdims=True))
        den = jnp.sum(pw, axis=-1, keepdims=True)
        o_win = jnp.dot(pw.reshape(hpg * tq, WIN_KEYS).astype(BF16), vw,
                        preferred_element_type=F32).reshape(hpg, tq, LANES) / den

        for hh in range(hpg):
            head = hk * hpg + hh
            p, e = head // 2, head % 2
            c0 = GATE_COL0 + head * 3
            mix = (sig[:, c0:c0 + 1] * o_cmp[hh] + sig[:, c0 + 1:c0 + 2] * o_slc[hh]
                   + sig[:, c0 + 2:c0 + 3] * o_win[hh])
            if e == 0:
                mix_even = mix
            else:
                o_ref[:, p * LANES:(p + 1) * LANES] = jnp.where(half_mask[0], mix_even, mix)


def _nsa_prompt(qc, qr, dtg, kvcmp, kvs, kvw, bn, t):
    nq = t // Q_BLOCK
    nbk = t // SEL_BLOCK
    hpg = NSA_HEADS // NSA_KV_HEADS
    assert nbk >= TOP_N and t >= WIN_KEYS and t % SEL_KEY_TILE == 0
    qrow = lambda w: pl.BlockSpec((Q_BLOCK, w), lambda b, i: (b * nq + i, 0))
    seq = lambda r: pl.BlockSpec((r, 2 * KV_D), lambda b, i: (b, 0))
    return pl.pallas_call(
        functools.partial(_nsa_prompt_kernel, t=t),
        grid=(bn, nq),
        in_specs=[qrow(NSA_D), qrow(NSA_D), qrow(LANES), seq(nbk), seq(t), seq(t)],
        out_specs=qrow(NSA_D),
        out_shape=jax.ShapeDtypeStruct((bn * t, NSA_D), F32),
        scratch_shapes=[pltpu.VMEM((NSA_KV_HEADS, nbk, 2 * KV_D), BF16),
                        pltpu.VMEM((NSA_KV_HEADS, t, 2 * KV_D), BF16),
                        pltpu.VMEM((NSA_KV_HEADS, t, 2 * KV_D), BF16),
                        pltpu.VMEM((t // SEL_KEY_TILE, Q_BLOCK, SEL_KEY_TILE), F32),
                        pltpu.VMEM((hpg, Q_BLOCK, LANES), F32),
                        pltpu.VMEM((hpg, Q_BLOCK, LANES), F32),
                        pltpu.VMEM((hpg, Q_BLOCK, LANES), F32)],
        compiler_params=_cparams(("parallel", "arbitrary")),
        name="nsa_prompt",
    )(qc, qr, dtg, kvcmp, kvs, kvw)


def _outproj_kernel(ys_ref, yn_ref, h_ref, ws_ref, wn_ref, g_ref, b_ref, o_ref):
    mix = jnp.dot(ys_ref[...].astype(BF16), ws_ref[...], preferred_element_type=F32)
    mix = mix + jnp.dot(yn_ref[...].astype(BF16), wn_ref[...], preferred_element_type=F32)
    o_ref[...] = _layer_norm(DEEPNORM_ALPHA * h_ref[...] + mix, g_ref[...], b_ref[...])


def _outproj(y_ssd, y_nsa, h, w_ssd, w_nsa, ln_g, ln_b, tm):
    n = h.shape[0]
    row = lambda w: pl.BlockSpec((tm, w), lambda i: (i, 0))
    const = lambda a: pl.BlockSpec(a.shape, lambda i: (0,) * a.ndim)
    return pl.pallas_call(
        _outproj_kernel,
        grid=(n // tm,),
        in_specs=[row(SSD_D), row(NSA_D), row(D_MODEL), const(w_ssd), const(w_nsa), const(ln_g), const(ln_b)],
        out_specs=row(D_MODEL),
        out_shape=jax.ShapeDtypeStruct((n, D_MODEL), F32),
        compiler_params=_cparams(("parallel",)),
        name="outproj",
    )(y_ssd, y_nsa, h, w_ssd, w_nsa, ln_g, ln_b)


MOE_TOKENS = 256
ROUTE_ROWS = 8


def _rank_rows(x):
    n = x.shape[0]
    idx = lax.broadcasted_iota(jnp.int32, x.shape, 0)
    rank = jnp.zeros(x.shape, F32)
    for r in range(n):
        row = x[r:r + 1, :]
        rank = rank + ((row > x) | ((row == x) & (idx > r))).astype(F32)
    return rank


def _route_kernel(h_ref, rw_ref, rb_ref, slot_ref, tokinfo_ref, meta_ref, cnt_ref, carry_ref, carry_row_ref, *,
                  n_valid):
    i = pl.program_id(0)
    tm = MOE_TOKENS

    @pl.when(i == 0)
    def _():
        carry_ref[...] = jnp.zeros_like(carry_ref)
        carry_row_ref[...] = jnp.zeros_like(carry_row_ref)

    logits = lax.dot_general(rw_ref[...], h_ref[...].astype(BF16), (((1,), (1,)), ((), ())),
                             preferred_element_type=F32)
    scores = _sigmoid(logits)
    biased = scores + rb_ref[:, 0:1]
    b3 = biased.reshape(N_EXPERT_GROUPS, EXPERTS_PER_GROUP, tm)
    sidx = lax.broadcasted_iota(jnp.int32, b3.shape, 1)
    m1 = jnp.max(b3, axis=1, keepdims=True)
    first = jnp.min(jnp.where(b3 == m1, sidx, EXPERTS_PER_GROUP), axis=1, keepdims=True)
    m2 = jnp.max(jnp.where(sidx == first, -jnp.inf, b3), axis=1, keepdims=True)
    grp_score = (m1 + m2).reshape(N_EXPERT_GROUPS, tm)
    grp_keep = _rank_rows(grp_score) < TOPK_GROUPS
    masked = jnp.where(grp_keep.reshape(N_EXPERT_GROUPS, 1, tm), b3, NEG).reshape(N_EXPERTS, tm)
    rank = _rank_rows(masked)
    tok = i * tm + lax.broadcasted_iota(jnp.int32, (1, tm), 1)
    valid = tok < n_valid
    sel = (rank < TOP_K) & valid
    self32 = sel.astype(F32)
    wsel = self32 * scores
    wsum = jnp.sum(wsel, axis=0, keepdims=True)
    w = wsel / jnp.where(wsum > 0.0, wsum, 1.0) * ROUTED_SCALE

    selb = sel.astype(BF16)
    tri = lambda n, strict_upper: (
        (lax.broadcasted_iota(jnp.int32, (n, n), 0) < lax.broadcasted_iota(jnp.int32, (n, n), 1))
        if strict_upper else
        (lax.broadcasted_iota(jnp.int32, (n, n), 0) > lax.broadcasted_iota(jnp.int32, (n, n), 1))).astype(BF16)
    pad8 = lambda c: jnp.floor((c + (SUBLANES - 1.0)) * (1.0 / SUBLANES)) * SUBLANES
    pos_tile = jnp.dot(selb, tri(tm, True), preferred_element_type=F32)
    cnt_col = pad8(jnp.sum(self32, axis=1, keepdims=True))
    first_col = jnp.dot(tri(N_EXPERTS, False), jnp.broadcast_to(cnt_col, (N_EXPERTS, LANES)).astype(BF16),
                        preferred_element_type=F32)[:, 0:1]
    slot = first_col + pos_tile

    sel_pad = jnp.concatenate([selb, jnp.zeros((LANES - N_EXPERTS, tm), BF16)], axis=0)
    cnt_row = pad8(lax.dot_general(jnp.ones((SUBLANES, tm), BF16), sel_pad, (((1,), (1,)), ((), ())),
                                   preferred_element_type=F32))
    first_row = jnp.dot(cnt_row.astype(BF16), tri(LANES, True), preferred_element_type=F32)
    prev_row = carry_row_ref[...]
    meta = jnp.concatenate([cnt_row[0:1], first_row[0:1], prev_row[0:1], jnp.zeros((SUBLANES - 3, LANES), F32)], 0)
    meta_ref[0] = meta.astype(jnp.int32)
    carry_row_ref[...] = prev_row + cnt_row
    carry_ref[...] = carry_ref[...] + cnt_col

    slot_rows, w_rows = [], []
    for k in range(TOP_K):
        hit = (rank == k) & sel
        slot_rows.append(jnp.sum(jnp.where(hit, slot, 0.0), axis=0, keepdims=True))
        w_rows.append(jnp.sum(jnp.where(hit, w, 0.0), axis=0, keepdims=True))
    slot_rows = [jnp.where(valid, r, -1.0) for r in slot_rows]
    pad2 = jnp.zeros((ROUTE_ROWS - TOP_K, tm), F32)
    slot_ref[...] = jnp.concatenate(slot_rows + [pad2 - 1.0], 0).astype(jnp.int32)
    info = jnp.concatenate(w_rows + [pad2] + slot_rows + [jnp.zeros((LANES - ROUTE_ROWS - TOP_K, tm), F32)], 0)
    tokinfo_ref[...] = info.T

    @pl.when(i == pl.num_programs(0) - 1)
    def _():
        cnt_ref[...] = jnp.broadcast_to(carry_ref[:, 0:1], cnt_ref.shape)


def _route(h, router_wt, router_bias_col, n_valid):
    n = h.shape[0]
    tm = MOE_TOKENS
    const = lambda a: pl.BlockSpec(a.shape, lambda i: (0,) * a.ndim)
    return pl.pallas_call(
        functools.partial(_route_kernel, n_valid=n_valid),
        grid=(n // tm,),
        in_specs=[pl.BlockSpec((tm, D_MODEL), lambda i: (i, 0)), const(router_wt), const(router_bias_col)],
        out_specs=[pl.BlockSpec((ROUTE_ROWS, tm), lambda i: (0, i)),
                   pl.BlockSpec((tm, LANES), lambda i: (i, 0)),
                   pl.BlockSpec((1, SUBLANES, LANES), lambda i: (i, 0, 0)),
                   pl.BlockSpec((N_EXPERTS, LANES), lambda i: (0, 0))],
        out_shape=[jax.ShapeDtypeStruct((ROUTE_ROWS, n), jnp.int32),
                   jax.ShapeDtypeStruct((n, LANES), F32),
                   jax.ShapeDtypeStruct((n // tm, SUBLANES, LANES), jnp.int32),
                   jax.ShapeDtypeStruct((N_EXPERTS, LANES), F32)],
        scratch_shapes=[pltpu.VMEM((N_EXPERTS, LANES), F32), pltpu.VMEM((SUBLANES, LANES), F32)],
        compiler_params=_cparams(("arbitrary",)),
        name="moe_route",
    )(h, router_wt, router_bias_col)


TILE_SLOTS = MOE_TOKENS * TOP_K + N_EXPERTS * SUBLANES
RUN_CHUNKS = tuple(1 << b for b in range(int(math.log2(MOE_TOKENS)), int(math.log2(SUBLANES)) - 1, -1))


def _run_copy(src_ref, src_row, dst_ref, dst_row, rows, sem):
    return pltpu.make_async_copy(src_ref.at[pl.ds(pl.multiple_of(src_row, SUBLANES), rows)],
                                 dst_ref.at[pl.ds(pl.multiple_of(dst_row, SUBLANES), rows)], sem)


def _start_run(src_ref, src_row, dst_ref, dst_row, n, sem, started):
    off = jnp.int32(0)
    out = []
    for c, rows in enumerate(RUN_CHUNKS):
        take = (n & rows) != 0

        @pl.when(take)
        def _(off=off, rows=rows):
            _run_copy(src_ref, src_row + off, dst_ref, dst_row + off, rows, sem).start()

        inc = take.astype(jnp.int32)
        off = off + inc * rows
        out.append(started[c] + inc)
    return tuple(out)


def _wait_runs(src_ref, dst_ref, sem, started):
    for c, rows in enumerate(RUN_CHUNKS):
        def wait_one(j, carry, rows=rows):
            _run_copy(src_ref, 0, dst_ref, 0, rows, sem).wait()
            return carry

        lax.fori_loop(0, started[c], wait_one, 0)


def _dispatch_kernel(start_ref, cnt_ref, meta_ref, slot_ref, x_ref, xs_ref, sorted_ref, zero_ref, sem, zsem, *, cap):
    i = pl.program_id(0)
    tm = MOE_TOKENS

    @pl.when(i == 0)
    def _():
        zero_ref[...] = jnp.zeros_like(zero_ref)

        def fill_expert(e, started):
            lo = start_ref[e] + cnt_ref[e]
            hi = jnp.where(e == N_EXPERTS - 1, cap, start_ref[jnp.minimum(e + 1, N_EXPERTS - 1)])
            n_full = (hi - lo) // tm

            def fill_full(j, st):
                return _start_run(zero_ref, 0, xs_ref, lo + j * tm, jnp.int32(tm), zsem, st)

            started = lax.fori_loop(0, n_full, fill_full, started)
            return _start_run(zero_ref, 0, xs_ref, lo + n_full * tm, (hi - lo) - n_full * tm, zsem, started)

        filled = lax.fori_loop(0, N_EXPERTS, fill_expert, tuple(jnp.int32(0) for _ in RUN_CHUNKS))
        _wait_runs(zero_ref, xs_ref, zsem, filled)

    srow = lax.broadcasted_iota(jnp.int32, (TILE_SLOTS, tm), 0)
    onehot = srow == slot_ref[0:1, :]
    for k in range(1, TOP_K):
        onehot = onehot | (srow == slot_ref[k:k + 1, :])
    sorted_ref[...] = jnp.dot(onehot.astype(BF16), x_ref[...].astype(BF16), preferred_element_type=F32)

    def copy_expert(e, started):
        n = meta_ref[0, 0, e]
        return _start_run(sorted_ref, meta_ref[0, 1, e], xs_ref, start_ref[e] + meta_ref[0, 2, e], n, sem, started)

    started = lax.fori_loop(0, N_EXPERTS, copy_expert, tuple(jnp.int32(0) for _ in RUN_CHUNKS))
    _wait_runs(sorted_ref, xs_ref, sem, started)


def _dispatch(h, slot_t, meta, seg_start, counts, cap):
    n = h.shape[0]
    tm = MOE_TOKENS
    return pl.pallas_call(
        functools.partial(_dispatch_kernel, cap=cap),
        grid_spec=pltpu.PrefetchScalarGridSpec(
            num_scalar_prefetch=2,
            grid=(n // tm,),
            in_specs=[pl.BlockSpec((1, SUBLANES, LANES), lambda i, *_: (i, 0, 0), memory_space=pltpu.SMEM),
                      pl.BlockSpec((ROUTE_ROWS, tm), lambda i, *_: (0, i)),
                      pl.BlockSpec((tm, D_MODEL), lambda i, *_: (i, 0))],
            out_specs=pl.BlockSpec(memory_space=pl.ANY),
            scratch_shapes=[pltpu.VMEM((TILE_SLOTS, D_MODEL), F32), pltpu.VMEM((tm, D_MODEL), F32),
                            pltpu.SemaphoreType.DMA, pltpu.SemaphoreType.DMA]),
        out_shape=jax.ShapeDtypeStruct((cap, D_MODEL), F32),
        compiler_params=_cparams(("arbitrary",)),
        name="moe_dispatch",
    )(seg_start, counts, meta, slot_t, h)


def _swiglu(x, wg, wu, wd):
    xb = x.astype(BF16)
    g = jnp.dot(xb, wg.astype(BF16), preferred_element_type=F32)
    u = jnp.dot(xb, wu.astype(BF16), preferred_element_type=F32)
    return jnp.dot((_silu(g) * u).astype(BF16), wd.astype(BF16), preferred_element_type=F32)


def _experts_kernel(be_ref, used_ref, x_ref, wg_ref, wu_ref, wd_ref, y_ref):
    i = pl.program_id(0)

    @pl.when(i < used_ref[0])
    def _():
        y_ref[...] = _swiglu(x_ref[...], wg_ref[0], wu_ref[0], wd_ref[0])

    @pl.when(i >= used_ref[0])
    def _():
        y_ref[...] = jnp.zeros_like(y_ref)


def _experts(xs, block_expert, used_blocks, w_gate, w_up, w_down):
    cap = xs.shape[0]
    return pl.pallas_call(
        _experts_kernel,
        grid_spec=pltpu.PrefetchScalarGridSpec(
            num_scalar_prefetch=2,
            grid=(cap // MOE_BLOCK,),
            in_specs=[pl.BlockSpec((MOE_BLOCK, D_MODEL), lambda i, be, used: (i, 0)),
                      pl.BlockSpec((1, D_MODEL, D_EXPERT), lambda i, be, used: (be[i], 0, 0)),
                      pl.BlockSpec((1, D_MODEL, D_EXPERT), lambda i, be, used: (be[i], 0, 0)),
                      pl.BlockSpec((1, D_EXPERT, D_MODEL), lambda i, be, used: (be[i], 0, 0))],
            out_specs=pl.BlockSpec((MOE_BLOCK, D_MODEL), lambda i, be, used: (i, 0))),
        out_shape=jax.ShapeDtypeStruct((cap, D_MODEL), F32),
        compiler_params=_cparams(("arbitrary",)),
        name="moe_experts",
    )(block_expert, used_blocks, xs, w_gate, w_up, w_down)


def _combine_kernel(start_ref, meta_ref, h_ref, info_ref, sg_ref, su_ref, sd_ref, g_ref, b_ref, ys_ref,
                    o_ref, buf_ref, sem):
    i = pl.program_id(0)
    tm = MOE_TOKENS

    @pl.when(i == 0)
    def _():
        buf_ref[...] = jnp.zeros_like(buf_ref)

    def fetch_expert(e, started):
        n = meta_ref[0, 0, e]
        return _start_run(ys_ref, start_ref[e] + meta_ref[0, 2, e], buf_ref, meta_ref[0, 1, e], n, sem, started)

    started = lax.fori_loop(0, N_EXPERTS, fetch_expert, tuple(jnp.int32(0) for _ in RUN_CHUNKS))
    h = h_ref[...]
    f = _swiglu(h, sg_ref[...], su_ref[...], sd_ref[...])
    info = info_ref[...]
    scol = lax.broadcasted_iota(jnp.int32, (tm, TILE_SLOTS), 1).astype(F32)
    mix = jnp.zeros((tm, TILE_SLOTS), F32)
    for k in range(TOP_K):
        mix = mix + jnp.where(info[:, ROUTE_ROWS + k:ROUTE_ROWS + k + 1] == scol, info[:, k:k + 1], 0.0)
    _wait_runs(ys_ref, buf_ref, sem, started)
    acc = jnp.dot(mix.astype(BF16), buf_ref[...].astype(BF16), preferred_element_type=F32)
    o_ref[...] = _layer_norm(DEEPNORM_ALPHA * h + (acc + f), g_ref[...], b_ref[...])


def _combine(h, ys, meta, tokinfo, seg_start, sh_gate, sh_up, sh_down, ln_g, ln_b):
    n = h.shape[0]
    tm = MOE_TOKENS
    const = lambda a: pl.BlockSpec(a.shape, lambda i, *_: (0,) * a.ndim)
    return pl.pallas_call(
        _combine_kernel,
        grid_spec=pltpu.PrefetchScalarGridSpec(
            num_scalar_prefetch=1,
            grid=(n // tm,),
            in_specs=[pl.BlockSpec((1, SUBLANES, LANES), lambda i, *_: (i, 0, 0), memory_space=pltpu.SMEM),
                      pl.BlockSpec((tm, D_MODEL), lambda i, *_: (i, 0)),
                      pl.BlockSpec((tm, LANES), lambda i, *_: (i, 0)),
                      const(sh_gate), const(sh_up), const(sh_down), const(ln_g), const(ln_b),
                      pl.BlockSpec(memory_space=pl.ANY)],
            out_specs=pl.BlockSpec((tm, D_MODEL), lambda i, *_: (i, 0)),
            scratch_shapes=[pltpu.VMEM((TILE_SLOTS, D_MODEL), F32), pltpu.SemaphoreType.DMA]),
        out_shape=jax.ShapeDtypeStruct((n, D_MODEL), F32),
        compiler_params=_cparams(("arbitrary",)),
        name="moe_combine",
    )(seg_start, meta, h, tokinfo, sh_gate, sh_up, sh_down, ln_g, ln_b, ys)


def _moe_ln(h, n_valid, router_w, router_bias, w_gate, w_up, w_down, sh_gate, sh_up, sh_down, ln_g, ln_b):
    slot_t, tokinfo, meta, cnt = _route(h, router_w.T.astype(BF16),
                                        jnp.broadcast_to(router_bias.astype(F32)[:, None], (N_EXPERTS, LANES)), n_valid)
    counts = cnt[:, 0].astype(jnp.int32)
    padded = (counts + MOE_BLOCK - 1) // MOE_BLOCK * MOE_BLOCK
    seg_end = jnp.cumsum(padded)
    seg_start = seg_end - padded
    run_pad = (h.shape[0] // MOE_TOKENS) * N_EXPERTS * (SUBLANES - 1)
    n_blocks = -(-(n_valid * TOP_K + run_pad + N_EXPERTS * (MOE_BLOCK - 1)) // MOE_BLOCK)
    cap = n_blocks * MOE_BLOCK
    block_first_row = jnp.arange(n_blocks, dtype=jnp.int32) * MOE_BLOCK
    block_expert = jnp.minimum(jnp.sum((seg_end[None, :] <= block_first_row[:, None]).astype(jnp.int32), axis=1),
                               N_EXPERTS - 1)
    xs = _dispatch(h, slot_t, meta, seg_start, counts, cap)
    used_blocks = (seg_end[N_EXPERTS - 1:] // MOE_BLOCK).astype(jnp.int32)
    ys = _experts(xs, block_expert, used_blocks, w_gate, w_up, w_down)
    return _combine(h, ys, meta, tokinfo, seg_start, sh_gate.astype(BF16), sh_up.astype(BF16),
                    sh_down.astype(BF16), ln_g, ln_b)


def _ssd_sample_kernel(xbc_ref, z_ref, dtg_ref, sconv_ref, s0_ref, convw_ref, convb_ref, dtb_ref, alog_ref,
                       dskip_ref, normw_ref, y_ref, s_ref, conv_out_ref, xc_ref, dt_ref, da_ref):
    b = pl.program_id(0)

    @pl.when(b == 0)
    def _():
        xin = xbc_ref[...]
        xc = convw_ref[SSD_CONV - 1:SSD_CONV, :] * xin
        for k in range(SSD_CONV - 1):
            xc = xc + convw_ref[k:k + 1, :] * sconv_ref[k]
        xc_ref[...] = _silu(xc + convb_ref[...])
        dt = _softplus(dtg_ref[...] + dtb_ref[...])
        dt_ref[...] = dt
        da_ref[...] = jnp.exp(dt * (-jnp.exp(alog_ref[...])))
        for k in range(SSD_CONV - 2):
            conv_out_ref[k] = sconv_ref[k + 1]
        conv_out_ref[SSD_CONV - 2] = xin

    xc = xc_ref[pl.ds(b, 1), :]
    dt = dt_ref[pl.ds(b, 1), :]
    da = da_ref[pl.ds(b, 1), :]
    ns = SSD_GROUPS * SSD_STATE
    eye = (lax.broadcasted_iota(jnp.int32, (HEAD_DIM, HEAD_DIM), 0)
           == lax.broadcasted_iota(jnp.int32, (HEAD_DIM, HEAD_DIM), 1))
    hpg = SSD_HEADS // SSD_GROUPS
    y_parts = []
    for h in range(SSD_HEADS):
        g = h // hpg
        x_h = xc[:, h * HEAD_DIM:(h + 1) * HEAD_DIM]
        b_g = xc[:, SSD_D + g * SSD_STATE:SSD_D + (g + 1) * SSD_STATE]
        c_g = xc[:, SSD_D + ns + g * SSD_STATE:SSD_D + ns + (g + 1) * SSD_STATE]
        xdt_col = jnp.sum(jnp.where(eye, x_h * dt[:, h:h + 1], 0.0), axis=1, keepdims=True)
        s_new = da[:, h:h + 1] * s0_ref[0, h] + xdt_col * b_g
        s_ref[0, h] = s_new
        y_h = _bdot_nt(c_g, s_new) + dskip_ref[:, h * HEAD_DIM:(h + 1) * HEAD_DIM] * x_h
        y_parts.append(y_h)
    y = jnp.concatenate(y_parts, axis=1)
    y_ref[pl.ds(b, 1), :] = _gated_group_norm(y, z_ref[pl.ds(b, 1), :], normw_ref[...])


def _ssd_sample(xbc, z, dtg, state_conv_t, state_ssm, conv_w, conv_b, dt_bias_pad, a_log_pad, d_skip_full, norm_w):
    bs = xbc.shape[0]
    const = lambda a: pl.BlockSpec(a.shape, lambda b: (0,) * a.ndim)
    state_spec = pl.BlockSpec((1, SSD_HEADS, HEAD_DIM, SSD_STATE), lambda b: (b, 0, 0, 0))
    return pl.pallas_call(
        _ssd_sample_kernel,
        grid=(bs,),
        in_specs=[const(xbc), const(z), const(dtg), const(state_conv_t), state_spec, const(conv_w), const(conv_b),
                  const(dt_bias_pad), const(a_log_pad), const(d_skip_full), const(norm_w)],
        out_specs=[pl.BlockSpec((bs, SSD_D), lambda b: (0, 0)), state_spec,
                   pl.BlockSpec((SSD_CONV - 1, bs, SSD_CONV_CH), lambda b: (0, 0, 0))],
        out_shape=[jax.ShapeDtypeStruct((bs, SSD_D), F32),
                   jax.ShapeDtypeStruct(state_ssm.shape, F32),
                   jax.ShapeDtypeStruct((SSD_CONV - 1, bs, SSD_CONV_CH), F32)],
        scratch_shapes=[pltpu.VMEM((bs, SSD_CONV_CH), F32), pltpu.VMEM((bs, LANES), F32),
                        pltpu.VMEM((bs, LANES), F32)],
        compiler_params=_cparams(("arbitrary",)),
        name="ssd_sample",
    )(xbc, z, dtg, state_conv_t, state_ssm, conv_w, conv_b, dt_bias_pad, a_log_pad, d_skip_full, norm_w)


PAGES_PER_STEP = 8


def _compress_paged_kernel(pt_ref, *refs, n_pages):
    pages = refs[:PAGES_PER_STEP]
    consts = refs[PAGES_PER_STEP:PAGES_PER_STEP + 7]
    o_ref, kbuf, vbuf = refs[PAGES_PER_STEP + 7:]
    s = pl.program_id(1)
    for j in range(PAGES_PER_STEP):
        r0 = pl.multiple_of((s * PAGES_PER_STEP + j) * PAGE_SIZE, PAGE_SIZE)
        kbuf[pl.ds(r0, PAGE_SIZE), :] = pages[j][0, :, 0:KV_D]
        vbuf[pl.ds(r0, PAGE_SIZE), :] = pages[j][0, :, KV_D:2 * KV_D]

    @pl.when(s == pl.num_programs(1) - 1)
    def _():
        nb = n_pages * PAGE_SIZE // CMP_BLOCK
        o_ref[...] = _compress_rows(kbuf, vbuf, *consts, nb)


def _compress_paged(pool, page_table, consts):
    bs, n_pages = page_table.shape
    nb = n_pages * PAGE_SIZE // CMP_BLOCK
    const = lambda a: pl.BlockSpec(a.shape, lambda b, s, pt: (0,) * a.ndim)

    def page_spec(j):
        return pl.BlockSpec((1, PAGE_SIZE, 2 * KV_D), lambda b, s, pt: (pt[b, s * PAGES_PER_STEP + j], 0, 0))

    return pl.pallas_call(
        functools.partial(_compress_paged_kernel, n_pages=n_pages),
        grid_spec=pltpu.PrefetchScalarGridSpec(
            num_scalar_prefetch=1,
            grid=(bs, n_pages // PAGES_PER_STEP),
            in_specs=[page_spec(j) for j in range(PAGES_PER_STEP)] + [const(a) for a in consts],
            out_specs=pl.BlockSpec((nb, 2 * KV_D), lambda b, s, pt: (b, 0)),
            scratch_shapes=[pltpu.VMEM((n_pages * PAGE_SIZE, KV_D), F32), pltpu.VMEM((n_pages * PAGE_SIZE, KV_D), F32)]),
        out_shape=jax.ShapeDtypeStruct((bs * nb, 2 * KV_D), F32),
        compiler_params=_cparams(("arbitrary", "arbitrary")),
        name="compress_paged",
    )(page_table, *([pool] * PAGES_PER_STEP), *consts)


SEL_PAST = TOP_N - 1


def _head_rows(q_row, hk):
    w = NSA_D // NSA_KV_HEADS
    q = jnp.broadcast_to(q_row[:, hk * w:(hk + 1) * w], (SUBLANES, w))
    own = (lax.broadcasted_iota(jnp.int32, (SUBLANES, w), 1) // HEAD_DIM
           == lax.broadcasted_iota(jnp.int32, (SUBLANES, w), 0))
    return jnp.where(own, q, 0.0), own


def _tile_kv_head(x, hk):
    sw = pltpu.roll(x, HEAD_DIM, 1)
    low = lax.broadcasted_iota(jnp.int32, x.shape, 1) < HEAD_DIM
    t = jnp.where(low, x, sw) if hk == 0 else jnp.where(low, sw, x)
    return jnp.concatenate([t, t], axis=1)


def _nsa_sample_cmp_kernel(qc_ref, cmp_ref, ocmp_ref, idx_ref, *, nc):
    b = pl.program_id(0)
    scale = HEAD_DIM ** -0.5
    q_row = qc_ref[pl.ds(b, 1), :]
    kc = cmp_ref[:, 0:KV_D]
    vc = cmp_ref[:, KV_D:2 * KV_D]
    lane = lax.broadcasted_iota(jnp.int32, (1, LANES), 1)
    o_parts = []
    for hk in range(NSA_KV_HEADS):
        qg, own = _head_rows(q_row, hk)
        s = _bdot_nt(qg, _tile_kv_head(kc, hk)) * scale
        ex = jnp.exp(s - jnp.max(s, axis=-1, keepdims=True))
        p = ex / jnp.sum(ex, axis=-1, keepdims=True)
        o = _bdot(p, _tile_kv_head(vc, hk))
        o_parts.append(jnp.sum(jnp.where(own, o, 0.0), axis=0, keepdims=True))
        hrow = lax.broadcasted_iota(jnp.int32, p.shape, 0) < NSA_HEADS // NSA_KV_HEADS
        imp = jnp.sum(jnp.where(hrow, p, 0.0), axis=0, keepdims=True)
        j = lax.broadcasted_iota(jnp.int32, (1, nc), 1)
        score = jnp.where((j == 0) | (j == nc - 1), FORCED_SCORE, imp)
        score_col = jnp.concatenate([score, jnp.zeros((LANES - 1, nc), F32)], 0).T[:, 0:1]
        jc = lax.broadcasted_iota(jnp.int32, (nc, nc), 0)
        jr = lax.broadcasted_iota(jnp.int32, (nc, nc), 1)
        beats = (score_col > score) | ((score_col == score) & (jc < jr))
        rank = jnp.sum(beats.astype(F32), axis=0, keepdims=True)
        row = jnp.zeros((1, LANES), F32)
        jf = j.astype(F32)
        for k in range(SEL_PAST):
            blk = jnp.sum(jnp.where(rank == k, jf, 0.0), axis=1, keepdims=True)
            row = jnp.where(lane == k, blk, row)
        idx_ref[pl.ds(b * NSA_KV_HEADS + hk, 1), :] = row.astype(jnp.int32)
    ocmp_ref[pl.ds(b, 1), :] = jnp.concatenate(o_parts, axis=1)


def _nsa_sample_cmp(qc, kvcmp, bs):
    nc = kvcmp.shape[0] // bs
    return pl.pallas_call(
        functools.partial(_nsa_sample_cmp_kernel, nc=nc),
        grid=(bs,),
        in_specs=[pl.BlockSpec((bs, NSA_D), lambda b: (0, 0)), pl.BlockSpec((nc, 2 * KV_D), lambda b: (b, 0))],
        out_specs=[pl.BlockSpec((bs, NSA_D), lambda b: (0, 0)),
                   pl.BlockSpec((bs * NSA_KV_HEADS, LANES), lambda b: (0, 0))],
        out_shape=[jax.ShapeDtypeStruct((bs, NSA_D), F32),
                   jax.ShapeDtypeStruct((bs * NSA_KV_HEADS, LANES), jnp.int32)],
        compiler_params=_cparams(("arbitrary",)),
        name="nsa_sample_cmp",
    )(qc, kvcmp)


def _sel_block_copy(pool_ref, pt_ref, sel_ref, buf_ref, sem, b, hk, k):
    per_page = PAGE_SIZE // SEL_BLOCK
    blk = sel_ref[b * NSA_KV_HEADS + hk, k]
    page = pt_ref[b, lax.shift_right_logical(blk, int(math.log2(per_page)))]
    row0 = pl.multiple_of((blk & (per_page - 1)) * SEL_BLOCK, SEL_BLOCK)
    return pltpu.make_async_copy(pool_ref.at[page, pl.ds(row0, SEL_BLOCK)], buf_ref.at[hk * SEL_PAST + k], sem)


def _nsa_sample_attn_kernel(pt_ref, sel_ref, qr_ref, new_sel_ref, new_win_ref, win_ref, dtg_ref, ocmp_ref, pool_ref,
                            o_ref, buf_ref, sem):
    b = pl.program_id(0)
    for hk in range(NSA_KV_HEADS):
        for k in range(SEL_PAST):
            _sel_block_copy(pool_ref, pt_ref, sel_ref, buf_ref, sem, b, hk, k).start()
    for hk in range(NSA_KV_HEADS):
        for k in range(SEL_PAST):
            _sel_block_copy(pool_ref, pt_ref, sel_ref, buf_ref, sem, b, hk, k).wait()
    scale = HEAD_DIM ** -0.5
    q_row = qr_ref[pl.ds(b, 1), :]
    sig = _sigmoid(dtg_ref[pl.ds(b, 1), :])
    new_sel = new_sel_ref[pl.ds(b, 1), :]
    new_win = new_win_ref[pl.ds(b, 1), :]
    win = win_ref[0]
    o_slc, o_win = [], []
    for hk in range(NSA_KV_HEADS):
        qg, own = _head_rows(q_row, hk)
        qb = qg.astype(BF16)

        def attend(kv_rows, kv_new, n_new):
            s = lax.dot_general(qb, _tile_kv_head(kv_rows[:, 0:KV_D], hk).astype(BF16), (((1,), (1,)), ((), ())),
                                preferred_element_type=F32) * scale
            k_new = _tile_kv_head(kv_new[:, 0:KV_D], hk).astype(BF16).astype(F32)
            s_new = jnp.sum(qb.astype(F32) * k_new, axis=1, keepdims=True) * scale
            m = jnp.maximum(jnp.max(s, axis=-1, keepdims=True), s_new)
            ex = jnp.exp(s - m)
            ex_new = jnp.exp(s_new - m) * n_new
            den = jnp.sum(ex, axis=-1, keepdims=True) + ex_new
            v_new = _tile_kv_head(kv_new[:, KV_D:2 * KV_D], hk).astype(BF16).astype(F32)
            o = _bdot(ex / den, _tile_kv_head(kv_rows[:, KV_D:2 * KV_D], hk))
            o = o + (ex_new / den).astype(BF16).astype(F32) * v_new
            return jnp.sum(jnp.where(own, o, 0.0), axis=0, keepdims=True)

        past = jnp.concatenate([buf_ref[hk * SEL_PAST + k] for k in range(SEL_PAST)], axis=0)
        o_slc.append(attend(past, new_sel, float(SEL_BLOCK)))
        o_win.append(attend(win, new_win, 1.0))
    o_slc = jnp.concatenate(o_slc, axis=1)
    o_win = jnp.concatenate(o_win, axis=1)
    gates = []
    for br in range(3):
        gates.append(jnp.concatenate(
            [jnp.broadcast_to(sig[:, GATE_COL0 + h * 3 + br:GATE_COL0 + h * 3 + br + 1], (1, HEAD_DIM))
             for h in range(NSA_HEADS)], axis=1))
    o_ref[pl.ds(b, 1), :] = gates[0] * ocmp_ref[pl.ds(b, 1), :] + gates[1] * o_slc + gates[2] * o_win


def _nsa_sample_attn(qr, new_sel, new_win, buf_win, dtg, o_cmp, pool_sel, page_table, sel_idx):
    bs = qr.shape[0]
    const = lambda a: pl.BlockSpec(a.shape, lambda b, pt, sel: (0,) * a.ndim)
    return pl.pallas_call(
        _nsa_sample_attn_kernel,
        grid_spec=pltpu.PrefetchScalarGridSpec(
            num_scalar_prefetch=2,
            grid=(bs,),
            in_specs=[const(qr), const(new_sel), const(new_win),
                      pl.BlockSpec((1,) + buf_win.shape[1:], lambda b, pt, sel: (b, 0, 0)),
                      const(dtg), const(o_cmp), pl.BlockSpec(memory_space=pl.ANY)],
            out_specs=pl.BlockSpec((bs, NSA_D), lambda b, pt, sel: (0, 0)),
            scratch_shapes=[pltpu.VMEM((NSA_KV_HEADS * SEL_PAST, SEL_BLOCK, 2 * KV_D), F32),
                            pltpu.SemaphoreType.DMA]),
        out_shape=jax.ShapeDtypeStruct((bs, NSA_D), F32),
        compiler_params=_cparams(("arbitrary",)),
        name="nsa_sample_attn",
    )(page_table, sel_idx, qr, new_sel, new_win, buf_win, dtg, o_cmp, pool_sel)


def kernel(x_prompt, x_sample, cache_kv_cmp, cache_kv_sel, page_table, cache_kv_win, state_ssm, state_conv,
           emb_ln_g, emb_ln_b, w_in, conv_w, conv_b, dt_bias, a_log, d_skip, ssd_norm_w,
           cmp_pe, cmp_w1, cmp_b1, cmp_w2, cmp_b2, w_out, ln1_g, ln1_b,
           router_w, router_bias, exp_w_gate, exp_w_up, exp_w_down,
           sh_w_gate, sh_w_up, sh_w_down, ln2_g, ln2_b):
    bp, tp, _ = x_prompt.shape
    bs, ts, _ = x_sample.shape
    assert ts == 1 and DEPTH == 1
    n_prompt = bp * tp
    past_len = page_table.shape[1] * PAGE_SIZE
    l = 0
    w_perm = _permute_w_in(w_in[l])
    ln0_g, ln0_b = emb_ln_g[None], emb_ln_b[None]
    ssd_consts = (conv_w[l], conv_b[l][None], _pad_lanes(dt_bias[l]), _pad_lanes(a_log[l]),
                  jnp.repeat(d_skip[l], HEAD_DIM)[None], ssd_norm_w[l][None])
    cmp_consts = _compress_consts(cmp_pe[l], cmp_w1[l], cmp_b1[l], cmp_w2[l], cmp_b2[l])
    w_o = w_out[l].astype(BF16)
    w_o_ssd, w_o_nsa = w_o[:SSD_D], w_o[SSD_D:]
    ln1 = (ln1_g[l][None], ln1_b[l][None])
    kv_shape = (2, NSA_KV_HEADS, HEAD_DIM)

    hp, z, xbc, qc, qr, kvc, kvs, kvw, dtg = _inproj(
        x_prompt.reshape(n_prompt, D_MODEL), ln0_g, ln0_b, w_perm, _rope_tables(jnp.arange(tp)), 256)
    y_ssd, ssm_p, conv_p = _ssd_prompt(xbc, z, dtg, *ssd_consts, bp, tp)
    kvcmp = _compress_prompt(kvc, cmp_consts, tp)
    y_nsa = _nsa_prompt(qc, qr, dtg, kvcmp, kvs, kvw, bp, tp)
    h1p = _outproj(y_ssd, y_nsa, hp, w_o_ssd, w_o_nsa, *ln1, 256)
    n_keep = min(WINDOW, tp)
    kvc_p = kvc.reshape((1, bp, tp) + kv_shape)
    kvs_p = kvs.reshape((1, bp, tp) + kv_shape)
    kvw_p = kvw.reshape((1, bp, tp) + kv_shape)[:, :, tp - n_keep:]

    s_hs, s_z, s_xbc, s_qc, s_qr, s_kvc, s_kvs, s_kvw, s_dtg = _inproj(
        x_sample.reshape(bs, D_MODEL), ln0_g, ln0_b, w_perm, _rope_tables(jnp.full((bs,), past_len)), bs)
    s_y_ssd, ssm_s, conv_s_t = _ssd_sample(s_xbc, s_z, s_dtg, jnp.swapaxes(state_conv[l], 0, 1), state_ssm[l],
                                           *ssd_consts)
    n_pool = cache_kv_cmp.shape[1]
    s_kvcmp = _compress_paged(cache_kv_cmp[l].reshape(n_pool, PAGE_SIZE, 2 * KV_D), page_table, cmp_consts)
    s_o_cmp, s_sel = _nsa_sample_cmp(s_qc, s_kvcmp, bs)
    buf_win = cache_kv_win[l].reshape(bs, -1, 2 * KV_D)
    s_y_nsa = _nsa_sample_attn(
        s_qr, s_kvs, s_kvw, buf_win, s_dtg, s_o_cmp,
        cache_kv_sel[l].reshape(n_pool, PAGE_SIZE, 2 * KV_D), page_table, s_sel)
    h1s = _outproj(s_y_ssd, s_y_nsa, s_hs, w_o_ssd, w_o_nsa, *ln1, bs)
    win_all = jnp.concatenate([buf_win, s_kvw[:, None, :]], 1)
    n_keep_s = min(WINDOW, past_len + ts)
    kvw_s = win_all[:, win_all.shape[1] - n_keep_s:].reshape((1, bs, n_keep_s) + kv_shape)
    kvc_s = s_kvc.reshape((1, bs, ts) + kv_shape)
    kvs_s = s_kvs.reshape((1, bs, ts) + kv_shape)

    n_tok = n_prompt + bs * ts
    n_pad = -(-n_tok // MOE_TOKENS) * MOE_TOKENS
    tok = jnp.concatenate([h1p, h1s, jnp.zeros((n_pad - n_tok, D_MODEL), F32)], 0)
    out = _moe_ln(tok, n_tok, router_w[l], router_bias[l], exp_w_gate[l], exp_w_up[l], exp_w_down[l],
                  sh_w_gate[l], sh_w_up[l], sh_w_down[l], ln2_g[l][None], ln2_b[l][None])
    y_prompt = out[:n_prompt].reshape(bp, tp, D_MODEL)
    y_sample = out[n_prompt:n_tok].reshape(bs, ts, D_MODEL)
    return (y_prompt, y_sample, kvc_p, kvs_p, kvw_p, ssm_p[None], conv_p[None],
            kvc_s, kvs_s, kvw_s, ssm_s[None], jnp.swapaxes(conv_s_t, 0, 1)[None])
```

```python
import functools
import math

import jax
import jax.numpy as jnp
import numpy as np
from jax import lax
from jax.experimental import pallas as pl
from jax.experimental.pallas import tpu as pltpu

D_MODEL = 1024
HEAD_DIM = 64
SSD_HEADS = 8
SSD_D = SSD_HEADS * HEAD_DIM
SSD_GROUPS = 2
SSD_STATE = 128
SSD_CONV = 4
SSD_CONV_CH = SSD_D + 2 * SSD_GROUPS * SSD_STATE
SSD_CHUNK = 128
NSA_HEADS = 8
NSA_KV_HEADS = 2
NSA_D = NSA_HEADS * HEAD_DIM
KV_D = NSA_KV_HEADS * HEAD_DIM
CMP_BLOCK = 64
CMP_HIDDEN = 128
SEL_BLOCK = 64
TOP_N = 16
WINDOW = 512
Q_BLOCK = 128
ROT_DIM = HEAD_DIM // 4
ROPE_THETA = 500000.0
N_EXPERTS = 64
TOP_K = 6
N_EXPERT_GROUPS = 8
EXPERTS_PER_GROUP = N_EXPERTS // N_EXPERT_GROUPS
TOPK_GROUPS = 4
D_EXPERT = 256
D_SHARED = 256
ROUTED_SCALE = 2.5
MOE_BLOCK = 256
DEPTH = 1
DEEPNORM_ALPHA = (2.0 * DEPTH) ** 0.25
LN_EPS = 1e-5
RMS_EPS = 1e-5
NEG = -1e30
FORCED_SCORE = 1e4
PAGE_SIZE = 128

LANES = 128
SUBLANES = 8
VMEM_LIMIT_BYTES = 56 * 1024 * 1024

U_Z = 0
U_XBC = U_Z + SSD_D
U_Q = U_XBC + SSD_CONV_CH
U_KVC = U_Q + NSA_D
U_KVS = U_KVC + 2 * KV_D
U_KVW = U_KVS + 2 * KV_D
U_DTG = U_KVW + 2 * KV_D
U_TOTAL = U_DTG + LANES
GATE_COL0 = SSD_HEADS

BF16 = jnp.bfloat16
F32 = jnp.float32


def _cparams(sem):
    return pltpu.CompilerParams(dimension_semantics=sem, vmem_limit_bytes=VMEM_LIMIT_BYTES)


def _bdot(a, b):
    return jnp.dot(a.astype(BF16), b.astype(BF16), preferred_element_type=F32)


def _bdot_nt(a, b):
    return lax.dot_general(a.astype(BF16), b.astype(BF16), (((1,), (1,)), ((), ())),
                           preferred_element_type=F32)


def _hdot(a, b):
    return jnp.dot(a, b, preferred_element_type=F32, precision=lax.Precision.HIGHEST)


def _sigmoid(x):
    return 1.0 / (1.0 + jnp.exp(-x))


def _silu(x):
    return x * _sigmoid(x)


def _layer_norm(x, g, b):
    mu = jnp.mean(x, axis=-1, keepdims=True)
    xc = x - mu
    var = jnp.mean(xc * xc, axis=-1, keepdims=True)
    return xc * lax.rsqrt(var + LN_EPS) * g + b


def _rope_tile(x, cos, sa, sb):
    return x * cos + pltpu.roll(x, LANES - ROT_DIM // 2, 1) * sa + pltpu.roll(x, ROT_DIM // 2, 1) * sb


def _inproj_kernel(x_ref, g_ref, b_ref, w_ref, rope_ref,
                   h_ref, z_ref, xbc_ref, qc_ref, qr_ref, kvc_ref, kvs_ref, kvw_ref, dtg_ref):
    h = _layer_norm(x_ref[...], g_ref[...], b_ref[...])
    h_ref[...] = h
    u = jnp.dot(h.astype(BF16), w_ref[...], preferred_element_type=F32)
    cos = rope_ref[:, 0:LANES]
    sa = rope_ref[:, LANES:2 * LANES]
    sb = rope_ref[:, 2 * LANES:3 * LANES]
    z_ref[...] = u[:, U_Z:U_XBC]
    xbc_ref[...] = u[:, U_XBC:U_Q]
    qc_ref[...] = u[:, U_Q:U_KVC]
    for c in range(NSA_D // LANES):
        qr_ref[:, c * LANES:(c + 1) * LANES] = _rope_tile(u[:, U_Q + c * LANES:U_Q + (c + 1) * LANES], cos, sa, sb)
    kvc_ref[...] = u[:, U_KVC:U_KVS]
    kvs_ref[:, 0:KV_D] = _rope_tile(u[:, U_KVS:U_KVS + KV_D], cos, sa, sb)
    kvs_ref[:, KV_D:2 * KV_D] = u[:, U_KVS + KV_D:U_KVW]
    kvw_ref[:, 0:KV_D] = _rope_tile(u[:, U_KVW:U_KVW + KV_D], cos, sa, sb)
    kvw_ref[:, KV_D:2 * KV_D] = u[:, U_KVW + KV_D:U_DTG]
    dtg_ref[...] = u[:, U_DTG:U_TOTAL]


def _rope_tables(pos):
    half = ROT_DIM // 2
    inv = ROPE_THETA ** (-jnp.arange(half, dtype=F32) / half)
    ang = pos.astype(F32)[:, None] * inv
    cos, sin = jnp.cos(ang), jnp.sin(ang)
    ones = jnp.ones((pos.shape[0], HEAD_DIM - ROT_DIM), F32)
    zeros = jnp.zeros((pos.shape[0], HEAD_DIM - ROT_DIM), F32)
    zh = jnp.zeros_like(sin)
    c = jnp.concatenate([cos, cos, ones], 1)
    sa = jnp.concatenate([-sin, zh, zeros], 1)
    sb = jnp.concatenate([zh, sin, zeros], 1)
    return jnp.concatenate([jnp.tile(t, (1, LANES // HEAD_DIM)) for t in (c, sa, sb)], 1)


def _permute_w_in(w):
    sizes = (SSD_D, SSD_CONV_CH, SSD_HEADS, NSA_D, KV_D, KV_D, KV_D, KV_D, KV_D, KV_D, 3 * NSA_HEADS)
    offs = np.concatenate([[0], np.cumsum(sizes)])
    seg = [w[:, offs[i]:offs[i + 1]] for i in range(len(sizes))]
    pad = jnp.zeros((w.shape[0], LANES - SSD_HEADS - 3 * NSA_HEADS), w.dtype)
    out = jnp.concatenate([seg[0], seg[1], seg[3], seg[4], seg[5], seg[6], seg[7], seg[8], seg[9],
                           seg[2], seg[10], pad], 1)
    return out.astype(BF16)


def _inproj(x, ln_g, ln_b, w_perm, rope_tab, tm):
    n = x.shape[0]
    nt = n // tm
    n_rope_blocks = rope_tab.shape[0] // tm
    row = lambda w: pl.BlockSpec((tm, w), lambda i: (i, 0))
    const = lambda a: pl.BlockSpec(a.shape, lambda i: (0,) * a.ndim)
    widths = (D_MODEL, SSD_D, SSD_CONV_CH, NSA_D, NSA_D, 2 * KV_D, 2 * KV_D, 2 * KV_D, LANES)
    return pl.pallas_call(
        _inproj_kernel,
        grid=(nt,),
        in_specs=[row(D_MODEL), const(ln_g), const(ln_b), const(w_perm),
                  pl.BlockSpec((tm, 3 * LANES), lambda i: (i % n_rope_blocks, 0))],
        out_specs=[row(w) for w in widths],
        out_shape=[jax.ShapeDtypeStruct((n, w), F32) for w in widths],
        compiler_params=_cparams(("parallel",)),
        name="inproj",
    )(x, ln_g, ln_b, w_perm, rope_tab)


def _softplus(x):
    return jnp.maximum(x, 0.0) + jnp.log1p(jnp.exp(-jnp.abs(x)))


def _gated_group_norm(y, z, norm_w):
    y = y * _silu(z)
    gw = SSD_D // SSD_GROUPS
    parts = []
    for g in range(SSD_GROUPS):
        yg = y[:, g * gw:(g + 1) * gw]
        ms = jnp.mean(yg * yg, axis=-1, keepdims=True)
        parts.append(yg * lax.rsqrt(ms + RMS_EPS))
    return jnp.concatenate(parts, axis=1) * norm_w


def _ssd_prompt_kernel(xbc_ref, z_ref, dtg_ref, convw_ref, convb_ref, dtb_ref, alog_ref, dskip_ref, normw_ref,
                       y_ref, state_ref, conv_ref, ext_ref, s_ref):
    c = pl.program_id(1)
    nc = pl.num_programs(1)
    L = SSD_CHUNK
    halo = SUBLANES

    @pl.when(c == 0)
    def _():
        ext_ref[0:halo, :] = jnp.zeros((halo, SSD_CONV_CH), F32)
        s_ref[...] = jnp.zeros_like(s_ref)

    xin = xbc_ref[...]
    ext_ref[halo:halo + L, :] = xin
    xc = convw_ref[SSD_CONV - 1:SSD_CONV, :] * xin
    for k in range(SSD_CONV - 1):
        off = halo - (SSD_CONV - 1) + k
        xc = xc + convw_ref[k:k + 1, :] * ext_ref[off:off + L, :]
    ext_ref[0:halo, :] = ext_ref[L:L + halo, :]
    xc = _silu(xc + convb_ref[...])
    xs = xc[:, 0:SSD_D]
    ns = SSD_GROUPS * SSD_STATE
    bm = xc[:, SSD_D:SSD_D + ns]
    cm = xc[:, SSD_D + ns:SSD_D + 2 * ns]

    dt = _softplus(dtg_ref[...] + dtb_ref[...])
    da = dt * (-jnp.exp(alog_ref[...]))
    row = lax.broadcasted_iota(jnp.int32, (L, L), 0)
    col = lax.broadcasted_iota(jnp.int32, (L, L), 1)
    tril = row >= col
    acum = _hdot(tril.astype(F32), da)
    acum_t = acum.T
    eacum = jnp.exp(acum)
    alast = acum[L - 1:L, :]
    edecay = jnp.exp(alast - acum)
    elast = jnp.exp(alast)

    dt_full = jnp.concatenate([jnp.broadcast_to(dt[:, h:h + 1], (L, HEAD_DIM)) for h in range(SSD_HEADS)], 1)
    dec_full = jnp.concatenate([jnp.broadcast_to(edecay[:, h:h + 1], (L, HEAD_DIM)) for h in range(SSD_HEADS)], 1)
    xdt = xs * dt_full
    xdec_t = (xdt * dec_full).T

    hpg = SSD_HEADS // SSD_GROUPS
    y_parts = []
    for h in range(SSD_HEADS):
        g = h // hpg
        b_g = bm[:, g * SSD_STATE:(g + 1) * SSD_STATE]
        c_g = cm[:, g * SSD_STATE:(g + 1) * SSD_STATE]
        if h % hpg == 0:
            cb = _bdot_nt(c_g, b_g)
        seg = acum[:, h:h + 1] - acum_t[h:h + 1, :]
        lmat = jnp.where(tril, jnp.exp(jnp.where(tril, seg, 0.0)), 0.0)
        xdt_h = xdt[:, h * HEAD_DIM:(h + 1) * HEAD_DIM]
        y_h = _bdot(cb * lmat, xdt_h)
        s_prev = s_ref[h]
        y_h = y_h + _bdot_nt(c_g, s_prev) * eacum[:, h:h + 1]
        y_h = y_h + dskip_ref[:, h * HEAD_DIM:(h + 1) * HEAD_DIM] * xs[:, h * HEAD_DIM:(h + 1) * HEAD_DIM]
        y_parts.append(y_h)
        s_ref[h] = elast[:, h:h + 1] * s_prev + _bdot(xdec_t[h * HEAD_DIM:(h + 1) * HEAD_DIM, :], b_g)
    y = jnp.concatenate(y_parts, axis=1)
    y_ref[...] = _gated_group_norm(y, z_ref[...], normw_ref[...])

    @pl.when(c == nc - 1)
    def _():
        state_ref[0] = s_ref[...]
        conv_ref[0] = xin[L - (SSD_CONV - 1):L, :]


def _ssd_prompt(xbc, z, dtg, conv_w, conv_b, dt_bias_pad, a_log_pad, d_skip_full, norm_w, bn, t):
    nc = t // SSD_CHUNK
    row = lambda w: pl.BlockSpec((SSD_CHUNK, w), lambda b, c: (b * nc + c, 0))
    const = lambda a: pl.BlockSpec(a.shape, lambda b, c: (0,) * a.ndim)
    return pl.pallas_call(
        _ssd_prompt_kernel,
        grid=(bn, nc),
        in_specs=[row(SSD_CONV_CH), row(SSD_D), row(LANES), const(conv_w), const(conv_b), const(dt_bias_pad),
                  const(a_log_pad), const(d_skip_full), const(norm_w)],
        out_specs=[row(SSD_D),
                   pl.BlockSpec((1, SSD_HEADS, HEAD_DIM, SSD_STATE), lambda b, c: (b, 0, 0, 0)),
                   pl.BlockSpec((1, SSD_CONV - 1, SSD_CONV_CH), lambda b, c: (b, 0, 0))],
        out_shape=[jax.ShapeDtypeStruct((bn * t, SSD_D), F32),
                   jax.ShapeDtypeStruct((bn, SSD_HEADS, HEAD_DIM, SSD_STATE), F32),
                   jax.ShapeDtypeStruct((bn, SSD_CONV - 1, SSD_CONV_CH), F32)],
        scratch_shapes=[pltpu.VMEM((SSD_CHUNK + 2 * SUBLANES, SSD_CONV_CH), F32),
                        pltpu.VMEM((SSD_HEADS, HEAD_DIM, SSD_STATE), F32)],
        compiler_params=_cparams(("parallel", "arbitrary")),
        name="ssd_prompt",
    )(xbc, z, dtg, conv_w, conv_b, dt_bias_pad, a_log_pad, d_skip_full, norm_w)


def _pad_lanes(v, fill=0.0):
    return jnp.concatenate([v.astype(F32), jnp.full((LANES - v.shape[0],), fill, F32)])[None]


def _compress_rows(k_ref, v_ref, pe_ref, w1k_ref, w1v_ref, b1_ref, w2k_ref, w2v_ref, b2_ref, nb):
    acck = jnp.zeros((nb, 2 * CMP_HIDDEN), F32)
    accv = jnp.zeros((nb, 2 * CMP_HIDDEN), F32)
    for l in range(CMP_BLOCK):
        xk = k_ref[pl.ds(l, nb, stride=CMP_BLOCK), :] + pe_ref[l:l + 1, 0:KV_D]
        xv = v_ref[pl.ds(l, nb, stride=CMP_BLOCK), :] + pe_ref[l:l + 1, KV_D:2 * KV_D]
        acck = acck + jnp.dot(xk.astype(BF16), w1k_ref[l], preferred_element_type=F32)
        accv = accv + jnp.dot(xv.astype(BF16), w1v_ref[l], preferred_element_type=F32)
    hk = _silu(acck + b1_ref[:, 0:2 * CMP_HIDDEN])
    hv = _silu(accv + b1_ref[:, 2 * CMP_HIDDEN:4 * CMP_HIDDEN])
    ok = jnp.dot(hk.astype(BF16), w2k_ref[...], preferred_element_type=F32) + b2_ref[:, 0:KV_D]
    ov = jnp.dot(hv.astype(BF16), w2v_ref[...], preferred_element_type=F32) + b2_ref[:, KV_D:2 * KV_D]
    return jnp.concatenate([ok, ov], axis=1)


def _compress_kernel(k_ref, v_ref, pe_ref, w1k_ref, w1v_ref, b1_ref, w2k_ref, w2v_ref, b2_ref, o_ref, *, nb):
    o_ref[...] = _compress_rows(k_ref, v_ref, pe_ref, w1k_ref, w1v_ref, b1_ref, w2k_ref, w2v_ref, b2_ref, nb)


def _block_diag2(w):
    z = jnp.zeros_like(w)
    return jnp.concatenate([jnp.concatenate([w, z], -1), jnp.concatenate([z, w], -1)], -2)


def _compress_consts(cmp_pe, cmp_w1, cmp_b1, cmp_w2, cmp_b2):
    pe = jnp.concatenate([cmp_pe[0], cmp_pe[0], cmp_pe[1], cmp_pe[1]], -1)
    w1k = _block_diag2(cmp_w1[0]).astype(BF16)
    w1v = _block_diag2(cmp_w1[1]).astype(BF16)
    b1 = jnp.concatenate([cmp_b1[0], cmp_b1[0], cmp_b1[1], cmp_b1[1]])[None]
    w2k = _block_diag2(cmp_w2[0]).astype(BF16)
    w2v = _block_diag2(cmp_w2[1]).astype(BF16)
    b2 = jnp.concatenate([cmp_b2[0], cmp_b2[0], cmp_b2[1], cmp_b2[1]])[None]
    return pe, w1k, w1v, b1, w2k, w2v, b2


def _compress_prompt(kvc, consts, rows_per_step):
    n = kvc.shape[0]
    nb = rows_per_step // CMP_BLOCK
    const = lambda a: pl.BlockSpec(a.shape, lambda i: (0,) * a.ndim)
    return pl.pallas_call(
        functools.partial(_compress_kernel, nb=nb),
        grid=(n // rows_per_step,),
        in_specs=[pl.BlockSpec((rows_per_step, KV_D), lambda i: (i, 0)),
                  pl.BlockSpec((rows_per_step, KV_D), lambda i: (i, 1))] + [const(a) for a in consts],
        out_specs=pl.BlockSpec((nb, 2 * KV_D), lambda i: (i, 0)),
        out_shape=jax.ShapeDtypeStruct((n // CMP_BLOCK, 2 * KV_D), F32),
        compiler_params=_cparams(("parallel",)),
        name="compress_prompt",
    )(kvc, kvc, *consts)


SEL_KEY_TILE = 256
WIN_KEYS = WINDOW + Q_BLOCK


def _dup_head(x, hk):
    sw = pltpu.roll(x, HEAD_DIM, 1)
    low = lax.broadcasted_iota(jnp.int32, x.shape, 1) < HEAD_DIM
    return jnp.where(low, x, sw) if hk == 0 else jnp.where(low, sw, x)


def _masked_softmax(s, mask):
    sm = jnp.where(mask, s, NEG)
    ex = jnp.where(mask, jnp.exp(sm - jnp.max(sm, axis=-1, keepdims=True)), 0.0)
    den = jnp.sum(ex, axis=-1, keepdims=True)
    return ex / jnp.where(den > 0.0, den, 1.0)


def _select_blocks(imp, cur, n_top):
    n_cand = imp.shape[1]
    j = lax.broadcasted_iota(jnp.int32, imp.shape, 1)
    future = j > cur
    forced = (j == 0) | (j == cur) | (j == cur - 1)
    score = jnp.where(future, NEG, jnp.where(forced, FORCED_SCORE, imp))
    rank = jnp.zeros(imp.shape, F32)
    for c in range(n_cand):
        col = score[:, c:c + 1]
        beats = (col > score) | ((col == score) & (j > c))
        rank = rank + beats.astype(F32)
    return ((rank < n_top) & (score > 0.5 * NEG)).astype(F32)


def _nsa_prompt_kernel(qc_ref, qr_ref, dtg_ref, cmp_ref, kvs_ref, kvw_ref, o_ref,
                       cmp_d, kvs_d, kvw_d, bias_ref, m_ref, l_ref, acc_ref, *, t):
    qb = pl.program_id(1)
    nbk = t // SEL_BLOCK
    tq = Q_BLOCK
    tk = SEL_KEY_TILE
    hpg = NSA_HEADS // NSA_KV_HEADS
    scale = HEAD_DIM ** -0.5

    @pl.when(qb == 0)
    def _():
        for src, dst in ((cmp_ref, cmp_d), (kvs_ref, kvs_d), (kvw_ref, kvw_d)):
            x = src[...]
            for hk in range(NSA_KV_HEADS):
                dst[hk, :, 0:KV_D] = _dup_head(x[:, 0:KV_D], hk).astype(BF16)
                dst[hk, :, KV_D:2 * KV_D] = _dup_head(x[:, KV_D:2 * KV_D], hk).astype(BF16)

    t0 = qb * tq
    rows = t0 + lax.broadcasted_iota(jnp.int32, (tq, 1), 0)
    lane = lax.broadcasted_iota(jnp.int32, (tq, LANES), 1)
    half_mask = (lane < HEAD_DIM, lane >= HEAD_DIM)
    sig = _sigmoid(dtg_ref[...])
    cidx = lax.broadcasted_iota(jnp.int32, (tq, nbk), 1)
    vis = (cidx + 1) * CMP_BLOCK - 1 <= rows
    expand = (lax.broadcasted_iota(jnp.int32, (nbk, t), 1) // SEL_BLOCK
              == lax.broadcasted_iota(jnp.int32, (nbk, t), 0)).astype(BF16)
    win_start = pl.multiple_of(jnp.maximum(t0 - WINDOW, 0), tq)
    wpos = win_start + lax.broadcasted_iota(jnp.int32, (tq, WIN_KEYS), 1)
    win_bias = jnp.where((wpos <= rows) & (wpos >= rows - WINDOW), 0.0, NEG)
    n_kt = (t0 + tq + tk - 1) // tk

    def stack_heads(ref, hk):
        parts = []
        for hh in range(hpg):
            head = hk * hpg + hh
            p, e = head // 2, head % 2
            parts.append(jnp.where(half_mask[e], ref[:, p * LANES:(p + 1) * LANES] * scale, 0.0))
        return jnp.concatenate(parts, axis=0).astype(BF16)

    for hk in range(NSA_KV_HEADS):
        qcs = stack_heads(qc_ref, hk)
        s = lax.dot_general(qcs, cmp_d[hk, :, 0:KV_D], (((1,), (1,)), ((), ())), preferred_element_type=F32)
        pc = _masked_softmax(s.reshape(hpg, tq, nbk), vis[None])
        imp = jnp.sum(pc, axis=0)
        o_cmp = jnp.dot(pc.reshape(hpg * tq, nbk).astype(BF16), cmp_d[hk, :, KV_D:2 * KV_D],
                        preferred_element_type=F32).reshape(hpg, tq, LANES)

        sel = _select_blocks(imp, rows // SEL_BLOCK, TOP_N)
        selk = jnp.dot(sel.astype(BF16), expand, preferred_element_type=F32)
        for kt in range(t // tk):
            kpos = kt * tk + lax.broadcasted_iota(jnp.int32, (tq, tk), 1)
            bias_ref[kt] = jnp.where((selk[:, kt * tk:(kt + 1) * tk] > 0.5) & (kpos <= rows), 0.0, NEG)

        qrs = stack_heads(qr_ref, hk)
        m_ref[...] = jnp.full(m_ref.shape, NEG, F32)
        l_ref[...] = jnp.zeros(l_ref.shape, F32)
        acc_ref[...] = jnp.zeros(acc_ref.shape, F32)

        def sel_step(kt, carry):
            k0 = pl.multiple_of(kt * tk, tk)
            kblk = kvs_d[hk, pl.ds(k0, tk), 0:KV_D]
            vblk = kvs_d[hk, pl.ds(k0, tk), KV_D:2 * KV_D]
            s = lax.dot_general(qrs, kblk, (((1,), (1,)), ((), ())), preferred_element_type=F32)
            s = s.reshape(hpg, tq, tk) + bias_ref[kt][None]
            m_old = m_ref[...]
            m_new = jnp.maximum(m_old, jnp.max(s, axis=-1, keepdims=True))
            alpha = jnp.exp(m_old - m_new)
            pe = jnp.exp(s - jnp.concatenate([m_new] * (tk // LANES), axis=-1))
            l_ref[...] = alpha * l_ref[...] + jnp.sum(pe, axis=-1, keepdims=True)
            pv = jnp.dot(pe.reshape(hpg * tq, tk).astype(BF16), vblk, preferred_element_type=F32)
            acc_ref[...] = alpha * acc_ref[...] + pv.reshape(hpg, tq, LANES)
            m_ref[...] = m_new
            return carry

        lax.fori_loop(0, n_kt, sel_step, 0)
        o_slc = acc_ref[...] / l_ref[...]

        kw = kvw_d[hk, pl.ds(win_start, WIN_KEYS), 0:KV_D]
        vw = kvw_d[hk, pl.ds(win_start, WIN_KEYS), KV_D:2 * KV_D]
        sw = lax.dot_general(qrs, kw, (((1,), (1,)), ((), ())), preferred_element_type=F32)
        sw = sw.reshape(hpg, tq, WIN_KEYS) + win_bias[None]
        pw = jnp.exp(sw - jnp.max(sw, axis=-1, keepdims=True))
        den = jnp.sum(pw, axis=-1, keepdims=True)
        o_win = jnp.dot(pw.reshape(hpg * tq, WIN_KEYS).astype(BF16), vw,
                        preferred_element_type=F32).reshape(hpg, tq, LANES) / den

        for hh in range(hpg):
            head = hk * hpg + hh
            p, e = head // 2, head % 2
            c0 = GATE_COL0 + head * 3
            mix = (sig[:, c0:c0 + 1] * o_cmp[hh] + sig[:, c0 + 1:c0 + 2] * o_slc[hh]
                   + sig[:, c0 + 2:c0 + 3] * o_win[hh])
            if e == 0:
                mix_even = mix
            else:
                o_ref[:, p * LANES:(p + 1) * LANES] = jnp.where(half_mask[0], mix_even, mix)


def _nsa_prompt(qc, qr, dtg, kvcmp, kvs, kvw, bn, t):
    nq = t // Q_BLOCK
    nbk = t // SEL_BLOCK
    hpg = NSA_HEADS // NSA_KV_HEADS
    assert nbk >= TOP_N and t >= WIN_KEYS and t % SEL_KEY_TILE == 0
    qrow = lambda w: pl.BlockSpec((Q_BLOCK, w), lambda b, i: (b * nq + i, 0))
    seq = lambda r: pl.BlockSpec((r, 2 * KV_D), lambda b, i: (b, 0))
    return pl.pallas_call(
        functools.partial(_nsa_prompt_kernel, t=t),
        grid=(bn, nq),
        in_specs=[qrow(NSA_D), qrow(NSA_D), qrow(LANES), seq(nbk), seq(t), seq(t)],
        out_specs=qrow(NSA_D),
        out_shape=jax.ShapeDtypeStruct((bn * t, NSA_D), F32),
        scratch_shapes=[pltpu.VMEM((NSA_KV_HEADS, nbk, 2 * KV_D), BF16),
                        pltpu.VMEM((NSA_KV_HEADS, t, 2 * KV_D), BF16),
                        pltpu.VMEM((NSA_KV_HEADS, t, 2 * KV_D), BF16),
                        pltpu.VMEM((t // SEL_KEY_TILE, Q_BLOCK, SEL_KEY_TILE), F32),
                        pltpu.VMEM((hpg, Q_BLOCK, LANES), F32),
                        pltpu.VMEM((hpg, Q_BLOCK, LANES), F32),
                        pltpu.VMEM((hpg, Q_BLOCK, LANES), F32)],
        compiler_params=_cparams(("parallel", "arbitrary")),
        name="nsa_prompt",
    )(qc, qr, dtg, kvcmp, kvs, kvw)


def _outproj_kernel(ys_ref, yn_ref, h_ref, ws_ref, wn_ref, g_ref, b_ref, o_ref):
    mix = jnp.dot(ys_ref[...].astype(BF16), ws_ref[...], preferred_element_type=F32)
    mix = mix + jnp.dot(yn_ref[...].astype(BF16), wn_ref[...], preferred_element_type=F32)
    o_ref[...] = _layer_norm(DEEPNORM_ALPHA * h_ref[...] + mix, g_ref[...], b_ref[...])


def _outproj(y_ssd, y_nsa, h, w_ssd, w_nsa, ln_g, ln_b, tm):
    n = h.shape[0]
    row = lambda w: pl.BlockSpec((tm, w), lambda i: (i, 0))
    const = lambda a: pl.BlockSpec(a.shape, lambda i: (0,) * a.ndim)
    return pl.pallas_call(
        _outproj_kernel,
        grid=(n // tm,),
        in_specs=[row(SSD_D), row(NSA_D), row(D_MODEL), const(w_ssd), const(w_nsa), const(ln_g), const(ln_b)],
        out_specs=row(D_MODEL),
        out_shape=jax.ShapeDtypeStruct((n, D_MODEL), F32),
        compiler_params=_cparams(("parallel",)),
        name="outproj",
    )(y_ssd, y_nsa, h, w_ssd, w_nsa, ln_g, ln_b)


MOE_TOKENS = 256
ROUTE_ROWS = 8


def _rank_rows(x):
    n = x.shape[0]
    idx = lax.broadcasted_iota(jnp.int32, x.shape, 0)
    rank = jnp.zeros(x.shape, F32)
    for r in range(n):
        row = x[r:r + 1, :]
        rank = rank + ((row > x) | ((row == x) & (idx > r))).astype(F32)
    return rank


def _route_kernel(h_ref, rw_ref, rb_ref, slot_ref, tokinfo_ref, meta_ref, cnt_ref, carry_ref, carry_row_ref, *,
                  n_valid):
    i = pl.program_id(0)
    tm = MOE_TOKENS

    @pl.when(i == 0)
    def _():
        carry_ref[...] = jnp.zeros_like(carry_ref)
        carry_row_ref[...] = jnp.zeros_like(carry_row_ref)

    logits = lax.dot_general(rw_ref[...], h_ref[...].astype(BF16), (((1,), (1,)), ((), ())),
                             preferred_element_type=F32)
    scores = _sigmoid(logits)
    biased = scores + rb_ref[:, 0:1]
    b3 = biased.reshape(N_EXPERT_GROUPS, EXPERTS_PER_GROUP, tm)
    sidx = lax.broadcasted_iota(jnp.int32, b3.shape, 1)
    m1 = jnp.max(b3, axis=1, keepdims=True)
    first = jnp.min(jnp.where(b3 == m1, sidx, EXPERTS_PER_GROUP), axis=1, keepdims=True)
    m2 = jnp.max(jnp.where(sidx == first, -jnp.inf, b3), axis=1, keepdims=True)
    grp_score = (m1 + m2).reshape(N_EXPERT_GROUPS, tm)
    grp_keep = _rank_rows(grp_score) < TOPK_GROUPS
    masked = jnp.where(grp_keep.reshape(N_EXPERT_GROUPS, 1, tm), b3, NEG).reshape(N_EXPERTS, tm)
    rank = _rank_rows(masked)
    tok = i * tm + lax.broadcasted_iota(jnp.int32, (1, tm), 1)
    valid = tok < n_valid
    sel = (rank < TOP_K) & valid
    self32 = sel.astype(F32)
    wsel = self32 * scores
    wsum = jnp.sum(wsel, axis=0, keepdims=True)
    w = wsel / jnp.where(wsum > 0.0, wsum, 1.0) * ROUTED_SCALE

    selb = sel.astype(BF16)
    tri = lambda n, strict_upper: (
        (lax.broadcasted_iota(jnp.int32, (n, n), 0) < lax.broadcasted_iota(jnp.int32, (n, n), 1))
        if strict_upper else
        (lax.broadcasted_iota(jnp.int32, (n, n), 0) > lax.broadcasted_iota(jnp.int32, (n, n), 1))).astype(BF16)
    pad8 = lambda c: jnp.floor((c + (SUBLANES - 1.0)) * (1.0 / SUBLANES)) * SUBLANES
    pos_tile = jnp.dot(selb, tri(tm, True), preferred_element_type=F32)
    cnt_col = pad8(jnp.sum(self32, axis=1, keepdims=True))
    first_col = jnp.dot(tri(N_EXPERTS, False), jnp.broadcast_to(cnt_col, (N_EXPERTS, LANES)).astype(BF16),
                        preferred_element_type=F32)[:, 0:1]
    slot = first_col + pos_tile

    sel_pad = jnp.concatenate([selb, jnp.zeros((LANES - N_EXPERTS, tm), BF16)], axis=0)
    cnt_row = pad8(lax.dot_general(jnp.ones((SUBLANES, tm), BF16), sel_pad, (((1,), (1,)), ((), ())),
                                   preferred_element_type=F32))
    first_row = jnp.dot(cnt_row.astype(BF16), tri(LANES, True), preferred_element_type=F32)
    prev_row = carry_row_ref[...]
    meta = jnp.concatenate([cnt_row[0:1], first_row[0:1], prev_row[0:1], jnp.zeros((SUBLANES - 3, LANES), F32)], 0)
    meta_ref[0] = meta.astype(jnp.int32)
    carry_row_ref[...] = prev_row + cnt_row
    carry_ref[...] = carry_ref[...] + cnt_col

    slot_rows, w_rows = [], []
    for k in range(TOP_K):
        hit = (rank == k) & sel
        slot_rows.append(jnp.sum(jnp.where(hit, slot, 0.0), axis=0, keepdims=True))
        w_rows.append(jnp.sum(jnp.where(hit, w, 0.0), axis=0, keepdims=True))
    slot_rows = [jnp.where(valid, r, -1.0) for r in slot_rows]
    pad2 = jnp.zeros((ROUTE_ROWS - TOP_K, tm), F32)
    slot_ref[...] = jnp.concatenate(slot_rows + [pad2 - 1.0], 0).astype(jnp.int32)
    info = jnp.concatenate(w_rows + [pad2] + slot_rows + [jnp.zeros((LANES - ROUTE_ROWS - TOP_K, tm), F32)], 0)
    tokinfo_ref[...] = info.T

    @pl.when(i == pl.num_programs(0) - 1)
    def _():
        cnt_ref[...] = jnp.broadcast_to(carry_ref[:, 0:1], cnt_ref.shape)


def _route(h, router_wt, router_bias_col, n_valid):
    n = h.shape[0]
    tm = MOE_TOKENS
    const = lambda a: pl.BlockSpec(a.shape, lambda i: (0,) * a.ndim)
    return pl.pallas_call(
        functools.partial(_route_kernel, n_valid=n_valid),
        grid=(n // tm,),
        in_specs=[pl.BlockSpec((tm, D_MODEL), lambda i: (i, 0)), const(router_wt), const(router_bias_col)],
        out_specs=[pl.BlockSpec((ROUTE_ROWS, tm), lambda i: (0, i)),
                   pl.BlockSpec((tm, LANES), lambda i: (i, 0)),
                   pl.BlockSpec((1, SUBLANES, LANES), lambda i: (i, 0, 0)),
                   pl.BlockSpec((N_EXPERTS, LANES), lambda i: (0, 0))],
        out_shape=[jax.ShapeDtypeStruct((ROUTE_ROWS, n), jnp.int32),
                   jax.ShapeDtypeStruct((n, LANES), F32),
                   jax.ShapeDtypeStruct((n // tm, SUBLANES, LANES), jnp.int32),
                   jax.ShapeDtypeStruct((N_EXPERTS, LANES), F32)],
        scratch_shapes=[pltpu.VMEM((N_EXPERTS, LANES), F32), pltpu.VMEM((SUBLANES, LANES), F32)],
        compiler_params=_cparams(("arbitrary",)),
        name="moe_route",
    )(h, router_wt, router_bias_col)


TILE_SLOTS = MOE_TOKENS * TOP_K + N_EXPERTS * SUBLANES
RUN_CHUNKS = tuple(1 << b for b in range(int(math.log2(MOE_TOKENS)), int(math.log2(SUBLANES)) - 1, -1))


def _run_copy(src_ref, src_row, dst_ref, dst_row, rows, sem):
    return pltpu.make_async_copy(src_ref.at[pl.ds(pl.multiple_of(src_row, SUBLANES), rows)],
                                 dst_ref.at[pl.ds(pl.multiple_of(dst_row, SUBLANES), rows)], sem)


def _start_run(src_ref, src_row, dst_ref, dst_row, n, sem, started):
    off = jnp.int32(0)
    out = []
    for c, rows in enumerate(RUN_CHUNKS):
        take = (n & rows) != 0

        @pl.when(take)
        def _(off=off, rows=rows):
            _run_copy(src_ref, src_row + off, dst_ref, dst_row + off, rows, sem).start()

        inc = take.astype(jnp.int32)
        off = off + inc * rows
        out.append(started[c] + inc)
    return tuple(out)


def _wait_runs(src_ref, dst_ref, sem, started):
    for c, rows in enumerate(RUN_CHUNKS):
        def wait_one(j, carry, rows=rows):
            _run_copy(src_ref, 0, dst_ref, 0, rows, sem).wait()
            return carry

        lax.fori_loop(0, started[c], wait_one, 0)


def _dispatch_kernel(start_ref, cnt_ref, meta_ref, slot_ref, x_ref, xs_ref, sorted_ref, zero_ref, sem, zsem, *, cap):
    i = pl.program_id(0)
    tm = MOE_TOKENS

    @pl.when(i == 0)
    def _():
        zero_ref[...] = jnp.zeros_like(zero_ref)

        def fill_expert(e, started):
            lo = start_ref[e] + cnt_ref[e]
            hi = jnp.where(e == N_EXPERTS - 1, cap, start_ref[jnp.minimum(e + 1, N_EXPERTS - 1)])
            n_full = (hi - lo) // tm

            def fill_full(j, st):
                return _start_run(zero_ref, 0, xs_ref, lo + j * tm, jnp.int32(tm), zsem, st)

            started = lax.fori_loop(0, n_full, fill_full, started)
            return _start_run(zero_ref, 0, xs_ref, lo + n_full * tm, (hi - lo) - n_full * tm, zsem, started)

        filled = lax.fori_loop(0, N_EXPERTS, fill_expert, tuple(jnp.int32(0) for _ in RUN_CHUNKS))
        _wait_runs(zero_ref, xs_ref, zsem, filled)

    srow = lax.broadcasted_iota(jnp.int32, (TILE_SLOTS, tm), 0)
    onehot = srow == slot_ref[0:1, :]
    for k in range(1, TOP_K):
        onehot = onehot | (srow == slot_ref[k:k + 1, :])
    sorted_ref[...] = jnp.dot(onehot.astype(BF16), x_ref[...].astype(BF16), preferred_element_type=F32)

    def copy_expert(e, started):
        n = meta_ref[0, 0, e]
        return _start_run(sorted_ref, meta_ref[0, 1, e], xs_ref, start_ref[e] + meta_ref[0, 2, e], n, sem, started)

    started = lax.fori_loop(0, N_EXPERTS, copy_expert, tuple(jnp.int32(0) for _ in RUN_CHUNKS))
    _wait_runs(sorted_ref, xs_ref, sem, started)


def _dispatch(h, slot_t, meta, seg_start, counts, cap):
    n = h.shape[0]
    tm = MOE_TOKENS
    return pl.pallas_call(
        functools.partial(_dispatch_kernel, cap=cap),
        grid_spec=pltpu.PrefetchScalarGridSpec(
            num_scalar_prefetch=2,
            grid=(n // tm,),
            in_specs=[pl.BlockSpec((1, SUBLANES, LANES), lambda i, *_: (i, 0, 0), memory_space=pltpu.SMEM),
                      pl.BlockSpec((ROUTE_ROWS, tm), lambda i, *_: (0, i)),
                      pl.BlockSpec((tm, D_MODEL), lambda i, *_: (i, 0))],
            out_specs=pl.BlockSpec(memory_space=pl.ANY),
            scratch_shapes=[pltpu.VMEM((TILE_SLOTS, D_MODEL), F32), pltpu.VMEM((tm, D_MODEL), F32),
                            pltpu.SemaphoreType.DMA, pltpu.SemaphoreType.DMA]),
        out_shape=jax.ShapeDtypeStruct((cap, D_MODEL), F32),
        compiler_params=_cparams(("arbitrary",)),
        name="moe_dispatch",
    )(seg_start, counts, meta, slot_t, h)


def _swiglu(x, wg, wu, wd):
    xb = x.astype(BF16)
    g = jnp.dot(xb, wg.astype(BF16), preferred_element_type=F32)
    u = jnp.dot(xb, wu.astype(BF16), preferred_element_type=F32)
    return jnp.dot((_silu(g) * u).astype(BF16), wd.astype(BF16), preferred_element_type=F32)


def _experts_kernel(be_ref, used_ref, x_ref, wg_ref, wu_ref, wd_ref, y_ref):
    i = pl.program_id(0)

    @pl.when(i < used_ref[0])
    def _():
        y_ref[...] = _swiglu(x_ref[...], wg_ref[0], wu_ref[0], wd_ref[0])

    @pl.when(i >= used_ref[0])
    def _():
        y_ref[...] = jnp.zeros_like(y_ref)


def _experts(xs, block_expert, used_blocks, w_gate, w_up, w_down):
    cap = xs.shape[0]
    return pl.pallas_call(
        _experts_kernel,
        grid_spec=pltpu.PrefetchScalarGridSpec(
            num_scalar_prefetch=2,
            grid=(cap // MOE_BLOCK,),
            in_specs=[pl.BlockSpec((MOE_BLOCK, D_MODEL), lambda i, be, used: (i, 0)),
                      pl.BlockSpec((1, D_MODEL, D_EXPERT), lambda i, be, used: (be[i], 0, 0)),
                      pl.BlockSpec((1, D_MODEL, D_EXPERT), lambda i, be, used: (be[i], 0, 0)),
                      pl.BlockSpec((1, D_EXPERT, D_MODEL), lambda i, be, used: (be[i], 0, 0))],
            out_specs=pl.BlockSpec((MOE_BLOCK, D_MODEL), lambda i, be, used: (i, 0))),
        out_shape=jax.ShapeDtypeStruct((cap, D_MODEL), F32),
        compiler_params=_cparams(("arbitrary",)),
        name="moe_experts",
    )(block_expert, used_blocks, xs, w_gate, w_up, w_down)


def _combine_kernel(start_ref, meta_ref, h_ref, info_ref, sg_ref, su_ref, sd_ref, g_ref, b_ref, ys_ref,
                    o_ref, buf_ref, sem):
    i = pl.program_id(0)
    tm = MOE_TOKENS

    @pl.when(i == 0)
    def _():
        buf_ref[...] = jnp.zeros_like(buf_ref)

    def fetch_expert(e, started):
        n = meta_ref[0, 0, e]
        return _start_run(ys_ref, start_ref[e] + meta_ref[0, 2, e], buf_ref, meta_ref[0, 1, e], n, sem, started)

    started = lax.fori_loop(0, N_EXPERTS, fetch_expert, tuple(jnp.int32(0) for _ in RUN_CHUNKS))
    h = h_ref[...]
    f = _swiglu(h, sg_ref[...], su_ref[...], sd_ref[...])
    info = info_ref[...]
    scol = lax.broadcasted_iota(jnp.int32, (tm, TILE_SLOTS), 1).astype(F32)
    mix = jnp.zeros((tm, TILE_SLOTS), F32)
    for k in range(TOP_K):
        mix = mix + jnp.where(info[:, ROUTE_ROWS + k:ROUTE_ROWS + k + 1] == scol, info[:, k:k + 1], 0.0)
    _wait_runs(ys_ref, buf_ref, sem, started)
    acc = jnp.dot(mix.astype(BF16), buf_ref[...].astype(BF16), preferred_element_type=F32)
    o_ref[...] = _layer_norm(DEEPNORM_ALPHA * h + (acc + f), g_ref[...], b_ref[...])


def _combine(h, ys, meta, tokinfo, seg_start, sh_gate, sh_up, sh_down, ln_g, ln_b):
    n = h.shape[0]
    tm = MOE_TOKENS
    const = lambda a: pl.BlockSpec(a.shape, lambda i, *_: (0,) * a.ndim)
    return pl.pallas_call(
        _combine_kernel,
        grid_spec=pltpu.PrefetchScalarGridSpec(
            num_scalar_prefetch=1,
            grid=(n // tm,),
            in_specs=[pl.BlockSpec((1, SUBLANES, LANES), lambda i, *_: (i, 0, 0), memory_space=pltpu.SMEM),
                      pl.BlockSpec((tm, D_MODEL), lambda i, *_: (i, 0)),
                      pl.BlockSpec((tm, LANES), lambda i, *_: (i, 0)),
                      const(sh_gate), const(sh_up), const(sh_down), const(ln_g), const(ln_b),
                      pl.BlockSpec(memory_space=pl.ANY)],
            out_specs=pl.BlockSpec((tm, D_MODEL), lambda i, *_: (i, 0)),
            scratch_shapes=[pltpu.VMEM((TILE_SLOTS, D_MODEL), F32), pltpu.SemaphoreType.DMA]),
        out_shape=jax.ShapeDtypeStruct((n, D_MODEL), F32),
        compiler_params=_cparams(("arbitrary",)),
        name="moe_combine",
    )(seg_start, meta, h, tokinfo, sh_gate, sh_up, sh_down, ln_g, ln_b, ys)


def _moe_ln(h, n_valid, router_w, router_bias, w_gate, w_up, w_down, sh_gate, sh_up, sh_down, ln_g, ln_b):
    slot_t, tokinfo, meta, cnt = _route(h, router_w.T.astype(BF16),
                                        jnp.broadcast_to(router_bias.astype(F32)[:, None], (N_EXPERTS, LANES)), n_valid)
    counts = cnt[:, 0].astype(jnp.int32)
    padded = (counts + MOE_BLOCK - 1) // MOE_BLOCK * MOE_BLOCK
    seg_end = jnp.cumsum(padded)
    seg_start = seg_end - padded
    run_pad = (h.shape[0] // MOE_TOKENS) * N_EXPERTS * (SUBLANES - 1)
    n_blocks = -(-(n_valid * TOP_K + run_pad + N_EXPERTS * (MOE_BLOCK - 1)) // MOE_BLOCK)
    cap = n_blocks * MOE_BLOCK
    block_first_row = jnp.arange(n_blocks, dtype=jnp.int32) * MOE_BLOCK
    block_expert = jnp.minimum(jnp.sum((seg_end[None, :] <= block_first_row[:, None]).astype(jnp.int32), axis=1),
                               N_EXPERTS - 1)
    xs = _dispatch(h, slot_t, meta, seg_start, counts, cap)
    used_blocks = (seg_end[N_EXPERTS - 1:] // MOE_BLOCK).astype(jnp.int32)
    ys = _experts(xs, block_expert, used_blocks, w_gate, w_up, w_down)
    return _combine(h, ys, meta, tokinfo, seg_start, sh_gate.astype(BF16), sh_up.astype(BF16),
                    sh_down.astype(BF16), ln_g, ln_b)


def _ssd_sample_kernel(xbc_ref, z_ref, dtg_ref, sconv_ref, s0_ref, convw_ref, convb_ref, dtb_ref, alog_ref,
                       dskip_ref, normw_ref, y_ref, s_ref, conv_out_ref, xc_ref, dt_ref, da_ref):
    b = pl.program_id(0)

    @pl.when(b == 0)
    def _():
        xin = xbc_ref[...]
        xc = convw_ref[SSD_CONV - 1:SSD_CONV, :] * xin
        for k in range(SSD_CONV - 1):
            xc = xc + convw_ref[k:k + 1, :] * sconv_ref[k]
        xc_ref[...] = _silu(xc + convb_ref[...])
        dt = _softplus(dtg_ref[...] + dtb_ref[...])
        dt_ref[...] = dt
        da_ref[...] = jnp.exp(dt * (-jnp.exp(alog_ref[...])))
        for k in range(SSD_CONV - 2):
            conv_out_ref[k] = sconv_ref[k + 1]
        conv_out_ref[SSD_CONV - 2] = xin

    xc = xc_ref[pl.ds(b, 1), :]
    dt = dt_ref[pl.ds(b, 1), :]
    da = da_ref[pl.ds(b, 1), :]
    ns = SSD_GROUPS * SSD_STATE
    eye = (lax.broadcasted_iota(jnp.int32, (HEAD_DIM, HEAD_DIM), 0)
           == lax.broadcasted_iota(jnp.int32, (HEAD_DIM, HEAD_DIM), 1))
    hpg = SSD_HEADS // SSD_GROUPS
    y_parts = []
    for h in range(SSD_HEADS):
        g = h // hpg
        x_h = xc[:, h * HEAD_DIM:(h + 1) * HEAD_DIM]
        b_g = xc[:, SSD_D + g * SSD_STATE:SSD_D + (g + 1) * SSD_STATE]
        c_g = xc[:, SSD_D + ns + g * SSD_STATE:SSD_D + ns + (g + 1) * SSD_STATE]
        xdt_col = jnp.sum(jnp.where(eye, x_h * dt[:, h:h + 1], 0.0), axis=1, keepdims=True)
        s_new = da[:, h:h + 1] * s0_ref[0, h] + xdt_col * b_g
        s_ref[0, h] = s_new
        y_h = _bdot_nt(c_g, s_new) + dskip_ref[:, h * HEAD_DIM:(h + 1) * HEAD_DIM] * x_h
        y_parts.append(y_h)
    y = jnp.concatenate(y_parts, axis=1)
    y_ref[pl.ds(b, 1), :] = _gated_group_norm(y, z_ref[pl.ds(b, 1), :], normw_ref[...])


def _ssd_sample(xbc, z, dtg, state_conv_t, state_ssm, conv_w, conv_b, dt_bias_pad, a_log_pad, d_skip_full, norm_w):
    bs = xbc.shape[0]
    const = lambda a: pl.BlockSpec(a.shape, lambda b: (0,) * a.ndim)
    state_spec = pl.BlockSpec((1, SSD_HEADS, HEAD_DIM, SSD_STATE), lambda b: (b, 0, 0, 0))
    return pl.pallas_call(
        _ssd_sample_kernel,
        grid=(bs,),
        in_specs=[const(xbc), const(z), const(dtg), const(state_conv_t), state_spec, const(conv_w), const(conv_b),
                  const(dt_bias_pad), const(a_log_pad), const(d_skip_full), const(norm_w)],
        out_specs=[pl.BlockSpec((bs, SSD_D), lambda b: (0, 0)), state_spec,
                   pl.BlockSpec((SSD_CONV - 1, bs, SSD_CONV_CH), lambda b: (0, 0, 0))],
        out_shape=[jax.ShapeDtypeStruct((bs, SSD_D), F32),
                   jax.ShapeDtypeStruct(state_ssm.shape, F32),
                   jax.ShapeDtypeStruct((SSD_CONV - 1, bs, SSD_CONV_CH), F32)],
        scratch_shapes=[pltpu.VMEM((bs, SSD_CONV_CH), F32), pltpu.VMEM((bs, LANES), F32),
                        pltpu.VMEM((bs, LANES), F32)],
        compiler_params=_cparams(("arbitrary",)),
        name="ssd_sample",
    )(xbc, z, dtg, state_conv_t, state_ssm, conv_w, conv_b, dt_bias_pad, a_log_pad, d_skip_full, norm_w)


PAGES_PER_STEP = 8


def _compress_paged_kernel(pt_ref, *refs, n_pages):
    pages = refs[:PAGES_PER_STEP]
    consts = refs[PAGES_PER_STEP:PAGES_PER_STEP + 7]
    o_ref, kbuf, vbuf = refs[PAGES_PER_STEP + 7:]
    s = pl.program_id(1)
    for j in range(PAGES_PER_STEP):
        r0 = pl.multiple_of((s * PAGES_PER_STEP + j) * PAGE_SIZE, PAGE_SIZE)
        kbuf[pl.ds(r0, PAGE_SIZE), :] = pages[j][0, :, 0:KV_D]
        vbuf[pl.ds(r0, PAGE_SIZE), :] = pages[j][0, :, KV_D:2 * KV_D]

    @pl.when(s == pl.num_programs(1) - 1)
    def _():
        nb = n_pages * PAGE_SIZE // CMP_BLOCK
        o_ref[...] = _compress_rows(kbuf, vbuf, *consts, nb)


def _compress_paged(pool, page_table, consts):
    bs, n_pages = page_table.shape
    nb = n_pages * PAGE_SIZE // CMP_BLOCK
    const = lambda a: pl.BlockSpec(a.shape, lambda b, s, pt: (0,) * a.ndim)

    def page_spec(j):
        return pl.BlockSpec((1, PAGE_SIZE, 2 * KV_D), lambda b, s, pt: (pt[b, s * PAGES_PER_STEP + j], 0, 0))

    return pl.pallas_call(
        functools.partial(_compress_paged_kernel, n_pages=n_pages),
        grid_spec=pltpu.PrefetchScalarGridSpec(
            num_scalar_prefetch=1,
            grid=(bs, n_pages // PAGES_PER_STEP),
            in_specs=[page_spec(j) for j in range(PAGES_PER_STEP)] + [const(a) for a in consts],
            out_specs=pl.BlockSpec((nb, 2 * KV_D), lambda b, s, pt: (b, 0)),
            scratch_shapes=[pltpu.VMEM((n_pages * PAGE_SIZE, KV_D), F32), pltpu.VMEM((n_pages * PAGE_SIZE, KV_D), F32)]),
        out_shape=jax.ShapeDtypeStruct((bs * nb, 2 * KV_D), F32),
        compiler_params=_cparams(("arbitrary", "arbitrary")),
        name="compress_paged",
    )(page_table, *([pool] * PAGES_PER_STEP), *consts)


SEL_PAST = TOP_N - 1


def _head_rows(q_row, hk):
    w = NSA_D // NSA_KV_HEADS
    q = jnp.broadcast_to(q_row[:, hk * w:(hk + 1) * w], (SUBLANES, w))
    own = (lax.broadcasted_iota(jnp.int32, (SUBLANES, w), 1) // HEAD_DIM
           == lax.broadcasted_iota(jnp.int32, (SUBLANES, w), 0))
    return jnp.where(own, q, 0.0), own


def _tile_kv_head(x, hk):
    sw = pltpu.roll(x, HEAD_DIM, 1)
    low = lax.broadcasted_iota(jnp.int32, x.shape, 1) < HEAD_DIM
    t = jnp.where(low, x, sw) if hk == 0 else jnp.where(low, sw, x)
    return jnp.concatenate([t, t], axis=1)


def _nsa_sample_cmp_kernel(qc_ref, cmp_ref, ocmp_ref, idx_ref, *, nc):
    b = pl.program_id(0)
    scale = HEAD_DIM ** -0.5
    q_row = qc_ref[pl.ds(b, 1), :]
    kc = cmp_ref[:, 0:KV_D]
    vc = cmp_ref[:, KV_D:2 * KV_D]
    lane = lax.broadcasted_iota(jnp.int32, (1, LANES), 1)
    o_parts = []
    for hk in range(NSA_KV_HEADS):
        qg, own = _head_rows(q_row, hk)
        s = _bdot_nt(qg, _tile_kv_head(kc, hk)) * scale
        ex = jnp.exp(s - jnp.max(s, axis=-1, keepdims=True))
        p = ex / jnp.sum(ex, axis=-1, keepdims=True)
        o = _bdot(p, _tile_kv_head(vc, hk))
        o_parts.append(jnp.sum(jnp.where(own, o, 0.0), axis=0, keepdims=True))
        hrow = lax.broadcasted_iota(jnp.int32, p.shape, 0) < NSA_HEADS // NSA_KV_HEADS
        imp = jnp.sum(jnp.where(hrow, p, 0.0), axis=0, keepdims=True)
        j = lax.broadcasted_iota(jnp.int32, (1, nc), 1)
        score = jnp.where((j == 0) | (j == nc - 1), FORCED_SCORE, imp)
        score_col = jnp.concatenate([score, jnp.zeros((LANES - 1, nc), F32)], 0).T[:, 0:1]
        jc = lax.broadcasted_iota(jnp.int32, (nc, nc), 0)
        jr = lax.broadcasted_iota(jnp.int32, (nc, nc), 1)
        beats = (score_col > score) | ((score_col == score) & (jc < jr))
        rank = jnp.sum(beats.astype(F32), axis=0, keepdims=True)
        row = jnp.zeros((1, LANES), F32)
        jf = j.astype(F32)
        for k in range(SEL_PAST):
            blk = jnp.sum(jnp.where(rank == k, jf, 0.0), axis=1, keepdims=True)
            row = jnp.where(lane == k, blk, row)
        idx_ref[pl.ds(b * NSA_KV_HEADS + hk, 1), :] = row.astype(jnp.int32)
    ocmp_ref[pl.ds(b, 1), :] = jnp.concatenate(o_parts, axis=1)


def _nsa_sample_cmp(qc, kvcmp, bs):
    nc = kvcmp.shape[0] // bs
    return pl.pallas_call(
        functools.partial(_nsa_sample_cmp_kernel, nc=nc),
        grid=(bs,),
        in_specs=[pl.BlockSpec((bs, NSA_D), lambda b: (0, 0)), pl.BlockSpec((nc, 2 * KV_D), lambda b: (b, 0))],
        out_specs=[pl.BlockSpec((bs, NSA_D), lambda b: (0, 0)),
                   pl.BlockSpec((bs * NSA_KV_HEADS, LANES), lambda b: (0, 0))],
        out_shape=[jax.ShapeDtypeStruct((bs, NSA_D), F32),
                   jax.ShapeDtypeStruct((bs * NSA_KV_HEADS, LANES), jnp.int32)],
        compiler_params=_cparams(("arbitrary",)),
        name="nsa_sample_cmp",
    )(qc, kvcmp)


def _sel_block_copy(pool_ref, pt_ref, sel_ref, buf_ref, sem, b, hk, k):
    per_page = PAGE_SIZE // SEL_BLOCK
    blk = sel_ref[b * NSA_KV_HEADS + hk, k]
    page = pt_ref[b, lax.shift_right_logical(blk, int(math.log2(per_page)))]
    row0 = pl.multiple_of((blk & (per_page - 1)) * SEL_BLOCK, SEL_BLOCK)
    return pltpu.make_async_copy(pool_ref.at[page, pl.ds(row0, SEL_BLOCK)], buf_ref.at[hk * SEL_PAST + k], sem)


def _nsa_sample_attn_kernel(pt_ref, sel_ref, qr_ref, new_sel_ref, new_win_ref, win_ref, dtg_ref, ocmp_ref, pool_ref,
                            o_ref, buf_ref, sem):
    b = pl.program_id(0)
    for hk in range(NSA_KV_HEADS):
        for k in range(SEL_PAST):
            _sel_block_copy(pool_ref, pt_ref, sel_ref, buf_ref, sem, b, hk, k).start()
    for hk in range(NSA_KV_HEADS):
        for k in range(SEL_PAST):
            _sel_block_copy(pool_ref, pt_ref, sel_ref, buf_ref, sem, b, hk, k).wait()
    scale = HEAD_DIM ** -0.5
    q_row = qr_ref[pl.ds(b, 1), :]
    sig = _sigmoid(dtg_ref[pl.ds(b, 1), :])
    new_sel = new_sel_ref[pl.ds(b, 1), :]
    new_win = new_win_ref[pl.ds(b, 1), :]
    win = win_ref[0]
    o_slc, o_win = [], []
    for hk in range(NSA_KV_HEADS):
        qg, own = _head_rows(q_row, hk)
        qb = qg.astype(BF16)

        def attend(kv_rows, kv_new, n_new):
            s = lax.dot_general(qb, _tile_kv_head(kv_rows[:, 0:KV_D], hk).astype(BF16), (((1,), (1,)), ((), ())),
                                preferred_element_type=F32) * scale
            k_new = _tile_kv_head(kv_new[:, 0:KV_D], hk).astype(BF16).astype(F32)
            s_new = jnp.sum(qb.astype(F32) * k_new, axis=1, keepdims=True) * scale
            m = jnp.maximum(jnp.max(s, axis=-1, keepdims=True), s_new)
            ex = jnp.exp(s - m)
            ex_new = jnp.exp(s_new - m) * n_new
            den = jnp.sum(ex, axis=-1, keepdims=True) + ex_new
            v_new = _tile_kv_head(kv_new[:, KV_D:2 * KV_D], hk).astype(BF16).astype(F32)
            o = _bdot(ex / den, _tile_kv_head(kv_rows[:, KV_D:2 * KV_D], hk))
            o = o + (ex_new / den).astype(BF16).astype(F32) * v_new
            return jnp.sum(jnp.where(own, o, 0.0), axis=0, keepdims=True)

        past = jnp.concatenate([buf_ref[hk * SEL_PAST + k] for k in range(SEL_PAST)], axis=0)
        o_slc.append(attend(past, new_sel, float(SEL_BLOCK)))
        o_win.append(attend(win, new_win, 1.0))
    o_slc = jnp.concatenate(o_slc, axis=1)
    o_win = jnp.concatenate(o_win, axis=1)
    gates = []
    for br in range(3):
        gates.append(jnp.concatenate(
            [jnp.broadcast_to(sig[:, GATE_COL0 + h * 3 + br:GATE_COL0 + h * 3 + br + 1], (1, HEAD_DIM))
             for h in range(NSA_HEADS)], axis=1))
    o_ref[pl.ds(b, 1), :] = gates[0] * ocmp_ref[pl.ds(b, 1), :] + gates[1] * o_slc + gates[2] * o_win


def _nsa_sample_attn(qr, new_sel, new_win, buf_win, dtg, o_cmp, pool_sel, page_table, sel_idx):
    bs = qr.shape[0]
    const = lambda a: pl.BlockSpec(a.shape, lambda b, pt, sel: (0,) * a.ndim)
    return pl.pallas_call(
        _nsa_sample_attn_kernel,
        grid_spec=pltpu.PrefetchScalarGridSpec(
            num_scalar_prefetch=2,
            grid=(bs,),
            in_specs=[const(qr), const(new_sel), const(new_win),
                      pl.BlockSpec((1,) + buf_win.shape[1:], lambda b, pt, sel: (b, 0, 0)),
                      const(dtg), const(o_cmp), pl.BlockSpec(memory_space=pl.ANY)],
            out_specs=pl.BlockSpec((bs, NSA_D), lambda b, pt, sel: (0, 0)),
            scratch_shapes=[pltpu.VMEM((NSA_KV_HEADS * SEL_PAST, SEL_BLOCK, 2 * KV_D), F32),
                            pltpu.SemaphoreType.DMA]),
        out_shape=jax.ShapeDtypeStruct((bs, NSA_D), F32),
        compiler_params=_cparams(("arbitrary",)),
        name="nsa_sample_attn",
    )(page_table, sel_idx, qr, new_sel, new_win, buf_win, dtg, o_cmp, pool_sel)


BLOCKS_PER_PAGE = PAGE_SIZE // CMP_BLOCK
KV_FEATS = 2 * KV_D


def _compress_consts_t(cmp_pe, cmp_w1, cmp_b1, cmp_w2, cmp_b2):
    pe_t = jnp.stack([jnp.tile(cmp_pe[k].T, (1, BLOCKS_PER_PAGE)) for k in range(2)])
    w1_t = jnp.stack([_block_diag2(jnp.swapaxes(cmp_w1[k], 0, 1)) for k in range(2)]).astype(BF16)
    b1_t = jnp.stack([jnp.tile(cmp_b1[k], BLOCKS_PER_PAGE) for k in range(2)])[:, None, :]
    w2_t = jnp.stack([_block_diag2(cmp_w2[k]) for k in range(2)]).astype(BF16)
    b2_t = jnp.stack([jnp.tile(cmp_b2[k], BLOCKS_PER_PAGE) for k in range(2)])[:, None, :]
    return pe_t, w1_t, b1_t, w2_t, b2_t


def _compress_pages_kernel(pt_ref, *refs, n_pages):
    pages = refs[:PAGES_PER_STEP]
    pe_ref, w1_ref, b1_ref, w2_ref, b2_ref, o_ref, buf = refs[PAGES_PER_STEP:]
    s = pl.program_id(1)
    for j in range(PAGES_PER_STEP):
        r0 = pl.multiple_of((s * PAGES_PER_STEP + j) * KV_FEATS, KV_FEATS)
        buf[pl.ds(r0, KV_FEATS), :] = pages[j][0]

    @pl.when(s == pl.num_programs(1) - 1)
    def _():
        for kind in range(2):
            for h in range(NSA_KV_HEADS):
                base = kind * KV_D + h * HEAD_DIM

                def add_feature(d, acc):
                    x = buf[pl.ds(base + d, n_pages, stride=KV_FEATS), :] + pe_ref[kind, pl.ds(d, 1), :]
                    return acc + jnp.dot(x.astype(BF16), w1_ref[kind, d], preferred_element_type=F32)

                acc = lax.fori_loop(0, HEAD_DIM, add_feature,
                                    jnp.zeros((n_pages, BLOCKS_PER_PAGE * CMP_HIDDEN), F32), unroll=8)
                hid = _silu(acc + b1_ref[kind])
                o_ref[0, kind * NSA_KV_HEADS + h] = (
                    jnp.dot(hid.astype(BF16), w2_ref[kind], preferred_element_type=F32) + b2_ref[kind])


def _compress_pages(pool_t, page_table, consts):
    bs, n_pages = page_table.shape
    const = lambda a: pl.BlockSpec(a.shape, lambda b, s, pt: (0,) * a.ndim)

    def page_spec(j):
        return pl.BlockSpec((1, KV_FEATS, PAGE_SIZE), lambda b, s, pt: (pt[b, s * PAGES_PER_STEP + j], 0, 0))

    return pl.pallas_call(
        functools.partial(_compress_pages_kernel, n_pages=n_pages),
        grid_spec=pltpu.PrefetchScalarGridSpec(
            num_scalar_prefetch=1,
            grid=(bs, n_pages // PAGES_PER_STEP),
            in_specs=[page_spec(j) for j in range(PAGES_PER_STEP)] + [const(a) for a in consts],
            out_specs=pl.BlockSpec((1, 2 * NSA_KV_HEADS, n_pages, LANES), lambda b, s, pt: (b, 0, 0, 0)),
            scratch_shapes=[pltpu.VMEM((n_pages * KV_FEATS, PAGE_SIZE), F32)]),
        out_shape=jax.ShapeDtypeStruct((bs, 2 * NSA_KV_HEADS, n_pages, LANES), F32),
        compiler_params=_cparams(("arbitrary", "arbitrary")),
        name="compress_pages",
    )(page_table, *([pool_t] * PAGES_PER_STEP), *consts)


def _group_heads(q_row, hk):
    hpg = NSA_HEADS // NSA_KV_HEADS
    low = lax.broadcasted_iota(jnp.int32, (1, LANES), 1) < HEAD_DIM
    rows = []
    for r in range(hpg):
        head = hk * hpg + r
        tile = q_row[:, (head // 2) * LANES:(head // 2 + 1) * LANES]
        if head % 2 == 1:
            tile = pltpu.roll(tile, HEAD_DIM, 1)
        rows.append(jnp.where(low, tile, 0.0))
    return jnp.concatenate(rows + [jnp.zeros((SUBLANES - hpg, LANES), F32)], axis=0)


def _spread_heads(o_groups):
    hpg = NSA_HEADS // NSA_KV_HEADS
    return jnp.concatenate([o[r:r + 1, 0:HEAD_DIM] for o in o_groups for r in range(hpg)], axis=1)


def _nsa_sample_cmp_t_kernel(qc_ref, cmp_ref, ocmp_ref, idx_ref, *, n_pages):
    b = pl.program_id(0)
    nc = n_pages * BLOCKS_PER_PAGE
    scale = HEAD_DIM ** -0.5
    hpg = NSA_HEADS // NSA_KV_HEADS
    q_row = qc_ref[pl.ds(b, 1), :] * scale
    lane = lax.broadcasted_iota(jnp.int32, (1, LANES), 1)
    pos_r = lax.broadcasted_iota(jnp.int32, (1, nc), 1)
    bid_r = (pos_r % n_pages) * BLOCKS_PER_PAGE + pos_r // n_pages
    pos_c = lax.broadcasted_iota(jnp.int32, (nc, 1), 0)
    bid_c = (pos_c % n_pages) * BLOCKS_PER_PAGE + pos_c // n_pages
    o_groups = []
    for hk in range(NSA_KV_HEADS):
        kc = cmp_ref[0, hk].astype(BF16)
        vc = cmp_ref[0, NSA_KV_HEADS + hk].astype(BF16)
        qh = _group_heads(q_row, hk)
        s = jnp.concatenate(
            [lax.dot_general(pltpu.roll(qh, c * HEAD_DIM, 1).astype(BF16) if c else qh.astype(BF16), kc,
                             (((1,), (1,)), ((), ())), preferred_element_type=F32)
             for c in range(BLOCKS_PER_PAGE)], axis=1)
        ex = jnp.exp(s - jnp.max(s, axis=-1, keepdims=True))
        p = ex / jnp.sum(ex, axis=-1, keepdims=True)
        o = jnp.dot(p[:, 0:n_pages].astype(BF16), vc, preferred_element_type=F32)
        for c in range(1, BLOCKS_PER_PAGE):
            oc = jnp.dot(p[:, c * n_pages:(c + 1) * n_pages].astype(BF16), vc, preferred_element_type=F32)
            o = o + pltpu.roll(oc, LANES - c * HEAD_DIM, 1)
        o_groups.append(o)
        hrow = lax.broadcasted_iota(jnp.int32, p.shape, 0) < hpg
        imp = jnp.sum(jnp.where(hrow, p, 0.0), axis=0, keepdims=True)
        score = jnp.where((bid_r == 0) | (bid_r == nc - 1), FORCED_SCORE, imp)
        score_col = jnp.concatenate([score, jnp.zeros((LANES - 1, nc), F32)], 0).T[:, 0:1]
        beats = (score_col > score) | ((score_col == score) & (bid_c < bid_r))
        rank = jnp.sum(beats.astype(F32), axis=0, keepdims=True)
        row = jnp.zeros((1, LANES), F32)
        bid_f = bid_r.astype(F32)
        for k in range(SEL_PAST):
            blk = jnp.sum(jnp.where(rank == k, bid_f, 0.0), axis=1, keepdims=True)
            row = jnp.where(lane == k, blk, row)
        idx_ref[pl.ds(b * NSA_KV_HEADS + hk, 1), :] = row.astype(jnp.int32)
    ocmp_ref[pl.ds(b, 1), :] = _spread_heads(o_groups)


def _nsa_sample_cmp_t(qc, kvcmp_t):
    bs, _, n_pages, _ = kvcmp_t.shape
    return pl.pallas_call(
        functools.partial(_nsa_sample_cmp_t_kernel, n_pages=n_pages),
        grid=(bs,),
        in_specs=[pl.BlockSpec((bs, NSA_D), lambda b: (0, 0)),
                  pl.BlockSpec((1, 2 * NSA_KV_HEADS, n_pages, LANES), lambda b: (b, 0, 0, 0))],
        out_specs=[pl.BlockSpec((bs, NSA_D), lambda b: (0, 0)),
                   pl.BlockSpec((bs * NSA_KV_HEADS, LANES), lambda b: (0, 0))],
        out_shape=[jax.ShapeDtypeStruct((bs, NSA_D), F32),
                   jax.ShapeDtypeStruct((bs * NSA_KV_HEADS, LANES), jnp.int32)],
        compiler_params=_cparams(("arbitrary",)),
        name="nsa_sample_cmp",
    )(qc, kvcmp_t)


def _sel_block_copies(pool_ref, pt_ref, sel_ref, kbuf, vbuf, sem, b, hk, k):
    blk = sel_ref[b * NSA_KV_HEADS + hk, k]
    page = pt_ref[b, lax.shift_right_logical(blk, int(math.log2(BLOCKS_PER_PAGE)))]
    j = hk * SEL_PAST + k
    return (pltpu.make_async_copy(pool_ref.at[page, pl.ds(hk * HEAD_DIM, HEAD_DIM)], kbuf.at[j], sem),
            pltpu.make_async_copy(pool_ref.at[page, pl.ds(KV_D + hk * HEAD_DIM, HEAD_DIM)], vbuf.at[j], sem))


def _nsa_sample_attn_t_kernel(pt_ref, sel_ref, qr_ref, new_sel_ref, new_win_ref, win_ref, dtg_ref, ocmp_ref,
                              pool_ref, o_ref, kbuf, vbuf, sem):
    b = pl.program_id(0)
    for hk in range(NSA_KV_HEADS):
        for k in range(SEL_PAST):
            for cp in _sel_block_copies(pool_ref, pt_ref, sel_ref, kbuf, vbuf, sem, b, hk, k):
                cp.start()
    for hk in range(NSA_KV_HEADS):
        for k in range(SEL_PAST):
            for cp in _sel_block_copies(pool_ref, pt_ref, sel_ref, kbuf, vbuf, sem, b, hk, k):
                cp.wait()
    scale = HEAD_DIM ** -0.5
    q_row = qr_ref[pl.ds(b, 1), :] * scale
    sig = _sigmoid(dtg_ref[pl.ds(b, 1), :])
    lane = lax.broadcasted_iota(jnp.int32, (1, PAGE_SIZE), 1)
    o_slc, o_win = [], []
    for hk in range(NSA_KV_HEADS):
        qh = _group_heads(q_row, hk)[:, 0:HEAD_DIM].astype(BF16)

        def new_row(ref, kind):
            t = ref[pl.ds(b, 1), :][:, kind * KV_D:(kind + 1) * KV_D]
            if hk == 1:
                t = pltpu.roll(t, HEAD_DIM, 1)
            return t[:, 0:HEAD_DIM].astype(BF16).astype(F32)

        def attend(kt, vt, mask, new_ref, n_new):
            s = jnp.dot(qh, kt.astype(BF16), preferred_element_type=F32)
            if mask is not None:
                s = jnp.where(mask, s, NEG)
            s_new = jnp.sum(qh.astype(F32) * new_row(new_ref, 0), axis=1, keepdims=True)
            m = jnp.maximum(jnp.max(s, axis=-1, keepdims=True), s_new)
            ex = jnp.exp(s - m)
            ex_new = jnp.exp(s_new - m) * n_new
            den = jnp.sum(ex, axis=-1, keepdims=True) + ex_new
            o = lax.dot_general((ex / den).astype(BF16), vt.astype(BF16), (((1,), (1,)), ((), ())),
                                preferred_element_type=F32)
            return o + (ex_new / den).astype(BF16).astype(F32) * new_row(new_ref, 1)

        kt = jnp.concatenate([kbuf[hk * SEL_PAST + k] for k in range(SEL_PAST)], axis=1)
        vt = jnp.concatenate([vbuf[hk * SEL_PAST + k] for k in range(SEL_PAST)], axis=1)
        mask = jnp.concatenate(
            [lane // SEL_BLOCK == (sel_ref[b * NSA_KV_HEADS + hk, k] & (BLOCKS_PER_PAGE - 1))
             for k in range(SEL_PAST)], axis=1)
        o_slc.append(attend(kt, vt, mask, new_sel_ref, float(SEL_BLOCK)))
        o_win.append(attend(win_ref[0, hk * HEAD_DIM:(hk + 1) * HEAD_DIM, :],
                            win_ref[0, KV_D + hk * HEAD_DIM:KV_D + (hk + 1) * HEAD_DIM, :], None, new_win_ref, 1.0))
    gates = []
    for br in range(3):
        gates.append(jnp.concatenate(
            [jnp.broadcast_to(sig[:, GATE_COL0 + h * 3 + br:GATE_COL0 + h * 3 + br + 1], (1, HEAD_DIM))
             for h in range(NSA_HEADS)], axis=1))
    o_ref[pl.ds(b, 1), :] = (gates[0] * ocmp_ref[pl.ds(b, 1), :] + gates[1] * _spread_heads(o_slc)
                             + gates[2] * _spread_heads(o_win))


def _nsa_sample_attn_t(qr, new_sel, new_win, win_t, dtg, o_cmp, pool_sel_t, page_table, sel_idx):
    bs = qr.shape[0]
    const = lambda a: pl.BlockSpec(a.shape, lambda b, pt, sel: (0,) * a.ndim)
    n_buf = NSA_KV_HEADS * SEL_PAST
    return pl.pallas_call(
        _nsa_sample_attn_t_kernel,
        grid_spec=pltpu.PrefetchScalarGridSpec(
            num_scalar_prefetch=2,
            grid=(bs,),
            in_specs=[const(qr), const(new_sel), const(new_win),
                      pl.BlockSpec((1,) + win_t.shape[1:], lambda b, pt, sel: (b, 0, 0)),
                      const(dtg), const(o_cmp), pl.BlockSpec(memory_space=pl.ANY)],
            out_specs=pl.BlockSpec((bs, NSA_D), lambda b, pt, sel: (0, 0)),
            scratch_shapes=[pltpu.VMEM((n_buf, HEAD_DIM, PAGE_SIZE), F32), pltpu.VMEM((n_buf, HEAD_DIM, PAGE_SIZE), F32),
                            pltpu.SemaphoreType.DMA]),
        out_shape=jax.ShapeDtypeStruct((bs, NSA_D), F32),
        compiler_params=_cparams(("arbitrary",)),
        name="nsa_sample_attn",
    )(page_table, sel_idx, qr, new_sel, new_win, win_t, dtg, o_cmp, pool_sel_t)


def kernel(x_prompt, x_sample, cache_kv_cmp, cache_kv_sel, page_table, cache_kv_win, state_ssm, state_conv,
           emb_ln_g, emb_ln_b, w_in, conv_w, conv_b, dt_bias, a_log, d_skip, ssd_norm_w,
           cmp_pe, cmp_w1, cmp_b1, cmp_w2, cmp_b2, w_out, ln1_g, ln1_b,
           router_w, router_bias, exp_w_gate, exp_w_up, exp_w_down,
           sh_w_gate, sh_w_up, sh_w_down, ln2_g, ln2_b):
    bp, tp, _ = x_prompt.shape
    bs, ts, _ = x_sample.shape
    assert ts == 1 and DEPTH == 1
    n_prompt = bp * tp
    past_len = page_table.shape[1] * PAGE_SIZE
    l = 0
    w_perm = _permute_w_in(w_in[l])
    ln0_g, ln0_b = emb_ln_g[None], emb_ln_b[None]
    ssd_consts = (conv_w[l], conv_b[l][None], _pad_lanes(dt_bias[l]), _pad_lanes(a_log[l]),
                  jnp.repeat(d_skip[l], HEAD_DIM)[None], ssd_norm_w[l][None])
    cmp_consts = _compress_consts(cmp_pe[l], cmp_w1[l], cmp_b1[l], cmp_w2[l], cmp_b2[l])
    w_o = w_out[l].astype(BF16)
    w_o_ssd, w_o_nsa = w_o[:SSD_D], w_o[SSD_D:]
    ln1 = (ln1_g[l][None], ln1_b[l][None])
    kv_shape = (2, NSA_KV_HEADS, HEAD_DIM)

    hp, z, xbc, qc, qr, kvc, kvs, kvw, dtg = _inproj(
        x_prompt.reshape(n_prompt, D_MODEL), ln0_g, ln0_b, w_perm, _rope_tables(jnp.arange(tp)), 256)
    y_ssd, ssm_p, conv_p = _ssd_prompt(xbc, z, dtg, *ssd_consts, bp, tp)
    kvcmp = _compress_prompt(kvc, cmp_consts, tp)
    y_nsa = _nsa_prompt(qc, qr, dtg, kvcmp, kvs, kvw, bp, tp)
    h1p = _outproj(y_ssd, y_nsa, hp, w_o_ssd, w_o_nsa, *ln1, 256)
    n_keep = min(WINDOW, tp)
    kvc_p = kvc.reshape((1, bp, tp) + kv_shape)
    kvs_p = kvs.reshape((1, bp, tp) + kv_shape)
    kvw_p = kvw.reshape((1, bp, tp) + kv_shape)[:, :, tp - n_keep:]

    s_hs, s_z, s_xbc, s_qc, s_qr, s_kvc, s_kvs, s_kvw, s_dtg = _inproj(
        x_sample.reshape(bs, D_MODEL), ln0_g, ln0_b, w_perm, _rope_tables(jnp.full((bs,), past_len)), bs)
    s_y_ssd, ssm_s, conv_s_t = _ssd_sample(s_xbc, s_z, s_dtg, jnp.swapaxes(state_conv[l], 0, 1), state_ssm[l],
                                           *ssd_consts)
    n_pool = cache_kv_cmp.shape[1]
    feature_major = lambda c, rows: jnp.swapaxes(c.reshape(-1, rows, 2 * KV_D), 1, 2)
    s_kvcmp = _compress_pages(feature_major(cache_kv_cmp[l], PAGE_SIZE), page_table,
                              _compress_consts_t(cmp_pe[l], cmp_w1[l], cmp_b1[l], cmp_w2[l], cmp_b2[l]))
    s_o_cmp, s_sel = _nsa_sample_cmp_t(s_qc, s_kvcmp)
    buf_win = cache_kv_win[l].reshape(bs, -1, 2 * KV_D)
    s_y_nsa = _nsa_sample_attn_t(
        s_qr, s_kvs, s_kvw, feature_major(cache_kv_win[l], buf_win.shape[1]), s_dtg, s_o_cmp,
        feature_major(cache_kv_sel[l], PAGE_SIZE), page_table, s_sel)
    h1s = _outproj(s_y_ssd, s_y_nsa, s_hs, w_o_ssd, w_o_nsa, *ln1, bs)
    win_all = jnp.concatenate([buf_win, s_kvw[:, None, :]], 1)
    n_keep_s = min(WINDOW, past_len + ts)
    kvw_s = win_all[:, win_all.shape[1] - n_keep_s:].reshape((1, bs, n_keep_s) + kv_shape)
    kvc_s = s_kvc.reshape((1, bs, ts) + kv_shape)
    kvs_s = s_kvs.reshape((1, bs, ts) + kv_shape)

    n_tok = n_prompt + bs * ts
    n_pad = -(-n_tok // MOE_TOKENS) * MOE_TOKENS
    tok = jnp.concatenate([h1p, h1s, jnp.zeros((n_pad - n_tok, D_MODEL), F32)], 0)
    out = _moe_ln(tok, n_tok, router_w[l], router_bias[l], exp_w_gate[l], exp_w_up[l], exp_w_down[l],
                  sh_w_gate[l], sh_w_up[l], sh_w_down[l], ln2_g[l][None], ln2_b[l][None])
    y_prompt = out[:n_prompt].reshape(bp, tp, D_MODEL)
    y_sample = out[n_prompt:n_tok].reshape(bs, ts, D_MODEL)
    return (y_prompt, y_sample, kvc_p, kvs_p, kvw_p, ssm_p[None], conv_p[None],
            kvc_s, kvs_s, kvw_s, ssm_s[None], jnp.swapaxes(conv_s_t, 0, 1)[None])
```

```python
import functools
import math

import jax
import jax.numpy as jnp
import numpy as np
from jax import lax
from jax.experimental import pallas as pl
from jax.experimental.pallas import tpu as pltpu

D_MODEL = 1024
HEAD_DIM = 64
SSD_HEADS = 8
SSD_D = SSD_HEADS * HEAD_DIM
SSD_GROUPS = 2
SSD_STATE = 128
SSD_CONV = 4
SSD_CONV_CH = SSD_D + 2 * SSD_GROUPS * SSD_STATE
SSD_CHUNK = 128
NSA_HEADS = 8
NSA_KV_HEADS = 2
NSA_D = NSA_HEADS * HEAD_DIM
KV_D = NSA_KV_HEADS * HEAD_DIM
CMP_BLOCK = 64
CMP_HIDDEN = 128
SEL_BLOCK = 64
TOP_N = 16
WINDOW = 512
Q_BLOCK = 128
ROT_DIM = HEAD_DIM // 4
ROPE_THETA = 500000.0
N_EXPERTS = 64
TOP_K = 6
N_EXPERT_GROUPS = 8
EXPERTS_PER_GROUP = N_EXPERTS // N_EXPERT_GROUPS
TOPK_GROUPS = 4
D_EXPERT = 256
D_SHARED = 256
ROUTED_SCALE = 2.5
MOE_BLOCK = 256
DEPTH = 1
DEEPNORM_ALPHA = (2.0 * DEPTH) ** 0.25
LN_EPS = 1e-5
RMS_EPS = 1e-5
NEG = -1e30
FORCED_SCORE = 1e4
PAGE_SIZE = 128

LANES = 128
SUBLANES = 8
VMEM_LIMIT_BYTES = 56 * 1024 * 1024

U_Z = 0
U_XBC = U_Z + SSD_D
U_Q = U_XBC + SSD_CONV_CH
U_KVC = U_Q + NSA_D
U_KVS = U_KVC + 2 * KV_D
U_KVW = U_KVS + 2 * KV_D
U_DTG = U_KVW + 2 * KV_D
U_TOTAL = U_DTG + LANES
GATE_COL0 = SSD_HEADS

BF16 = jnp.bfloat16
F32 = jnp.float32


def _cparams(sem):
    return pltpu.CompilerParams(dimension_semantics=sem, vmem_limit_bytes=VMEM_LIMIT_BYTES)


def _bdot(a, b):
    return jnp.dot(a.astype(BF16), b.astype(BF16), preferred_element_type=F32)


def _bdot_nt(a, b):
    return lax.dot_general(a.astype(BF16), b.astype(BF16), (((1,), (1,)), ((), ())),
                           preferred_element_type=F32)


def _hdot(a, b):
    return jnp.dot(a, b, preferred_element_type=F32, precision=lax.Precision.HIGHEST)


def _sigmoid(x):
    return 1.0 / (1.0 + jnp.exp(-x))


def _silu(x):
    return x * _sigmoid(x)


def _layer_norm(x, g, b):
    mu = jnp.mean(x, axis=-1, keepdims=True)
    xc = x - mu
    var = jnp.mean(xc * xc, axis=-1, keepdims=True)
    return xc * lax.rsqrt(var + LN_EPS) * g + b


def _rope_tile(x, cos, sa, sb):
    return x * cos + pltpu.roll(x, LANES - ROT_DIM // 2, 1) * sa + pltpu.roll(x, ROT_DIM // 2, 1) * sb


def _inproj_kernel(x_ref, g_ref, b_ref, w_ref, rope_ref,
                   h_ref, z_ref, xbc_ref, qc_ref, qr_ref, kvc_ref, kvs_ref, kvw_ref, dtg_ref):
    h = _layer_norm(x_ref[...], g_ref[...], b_ref[...])
    h_ref[...] = h
    u = jnp.dot(h.astype(BF16), w_ref[...], preferred_element_type=F32)
    cos = rope_ref[:, 0:LANES]
    sa = rope_ref[:, LANES:2 * LANES]
    sb = rope_ref[:, 2 * LANES:3 * LANES]
    z_ref[...] = u[:, U_Z:U_XBC]
    xbc_ref[...] = u[:, U_XBC:U_Q]
    qc_ref[...] = u[:, U_Q:U_KVC]
    for c in range(NSA_D // LANES):
        qr_ref[:, c * LANES:(c + 1) * LANES] = _rope_tile(u[:, U_Q + c * LANES:U_Q + (c + 1) * LANES], cos, sa, sb)
    kvc_ref[...] = u[:, U_KVC:U_KVS]
    kvs_ref[:, 0:KV_D] = _rope_tile(u[:, U_KVS:U_KVS + KV_D], cos, sa, sb)
    kvs_ref[:, KV_D:2 * KV_D] = u[:, U_KVS + KV_D:U_KVW]
    kvw_ref[:, 0:KV_D] = _rope_tile(u[:, U_KVW:U_KVW + KV_D], cos, sa, sb)
    kvw_ref[:, KV_D:2 * KV_D] = u[:, U_KVW + KV_D:U_DTG]
    dtg_ref[...] = u[:, U_DTG:U_TOTAL]


def _rope_tables(pos):
    half = ROT_DIM // 2
    inv = ROPE_THETA ** (-jnp.arange(half, dtype=F32) / half)
    ang = pos.astype(F32)[:, None] * inv
    cos, sin = jnp.cos(ang), jnp.sin(ang)
    ones = jnp.ones((pos.shape[0], HEAD_DIM - ROT_DIM), F32)
    zeros = jnp.zeros((pos.shape[0], HEAD_DIM - ROT_DIM), F32)
    zh = jnp.zeros_like(sin)
    c = jnp.concatenate([cos, cos, ones], 1)
    sa = jnp.concatenate([-sin, zh, zeros], 1)
    sb = jnp.concatenate([zh, sin, zeros], 1)
    return jnp.concatenate([jnp.tile(t, (1, LANES // HEAD_DIM)) for t in (c, sa, sb)], 1)


def _permute_w_in(w):
    sizes = (SSD_D, SSD_CONV_CH, SSD_HEADS, NSA_D, KV_D, KV_D, KV_D, KV_D, KV_D, KV_D, 3 * NSA_HEADS)
    offs = np.concatenate([[0], np.cumsum(sizes)])
    seg = [w[:, offs[i]:offs[i + 1]] for i in range(len(sizes))]
    pad = jnp.zeros((w.shape[0], LANES - SSD_HEADS - 3 * NSA_HEADS), w.dtype)
    out = jnp.concatenate([seg[0], seg[1], seg[3], seg[4], seg[5], seg[6], seg[7], seg[8], seg[9],
                           seg[2], seg[10], pad], 1)
    return out.astype(BF16)


def _inproj(x, ln_g, ln_b, w_perm, rope_tab, tm):
    n = x.shape[0]
    nt = n // tm
    n_rope_blocks = rope_tab.shape[0] // tm
    row = lambda w: pl.BlockSpec((tm, w), lambda i: (i, 0))
    const = lambda a: pl.BlockSpec(a.shape, lambda i: (0,) * a.ndim)
    widths = (D_MODEL, SSD_D, SSD_CONV_CH, NSA_D, NSA_D, 2 * KV_D, 2 * KV_D, 2 * KV_D, LANES)
    return pl.pallas_call(
        _inproj_kernel,
        grid=(nt,),
        in_specs=[row(D_MODEL), const(ln_g), const(ln_b), const(w_perm),
                  pl.BlockSpec((tm, 3 * LANES), lambda i: (i % n_rope_blocks, 0))],
        out_specs=[row(w) for w in widths],
        out_shape=[jax.ShapeDtypeStruct((n, w), F32) for w in widths],
        compiler_params=_cparams(("parallel",)),
        name="inproj",
    )(x, ln_g, ln_b, w_perm, rope_tab)


def _softplus(x):
    return jnp.maximum(x, 0.0) + jnp.log1p(jnp.exp(-jnp.abs(x)))


def _gated_group_norm(y, z, norm_w):
    y = y * _silu(z)
    gw = SSD_D // SSD_GROUPS
    parts = []
    for g in range(SSD_GROUPS):
        yg = y[:, g * gw:(g + 1) * gw]
        ms = jnp.mean(yg * yg, axis=-1, keepdims=True)
        parts.append(yg * lax.rsqrt(ms + RMS_EPS))
    return jnp.concatenate(parts, axis=1) * norm_w


def _ssd_prompt_kernel(xbc_ref, z_ref, dtg_ref, convw_ref, convb_ref, dtb_ref, alog_ref, dskip_ref, normw_ref,
                       y_ref, state_ref, conv_ref, ext_ref, s_ref):
    c = pl.program_id(1)
    nc = pl.num_programs(1)
    L = SSD_CHUNK
    halo = SUBLANES

    @pl.when(c == 0)
    def _():
        ext_ref[0:halo, :] = jnp.zeros((halo, SSD_CONV_CH), F32)
        s_ref[...] = jnp.zeros_like(s_ref)

    xin = xbc_ref[...]
    ext_ref[halo:halo + L, :] = xin
    xc = convw_ref[SSD_CONV - 1:SSD_CONV, :] * xin
    for k in range(SSD_CONV - 1):
        off = halo - (SSD_CONV - 1) + k
        xc = xc + convw_ref[k:k + 1, :] * ext_ref[off:off + L, :]
    ext_ref[0:halo, :] = ext_ref[L:L + halo, :]
    xc = _silu(xc + convb_ref[...])
    xs = xc[:, 0:SSD_D]
    ns = SSD_GROUPS * SSD_STATE
    bm = xc[:, SSD_D:SSD_D + ns]
    cm = xc[:, SSD_D + ns:SSD_D + 2 * ns]

    dt = _softplus(dtg_ref[...] + dtb_ref[...])
    da = dt * (-jnp.exp(alog_ref[...]))
    row = lax.broadcasted_iota(jnp.int32, (L, L), 0)
    col = lax.broadcasted_iota(jnp.int32, (L, L), 1)
    tril = row >= col
    acum = _hdot(tril.astype(F32), da)
    acum_t = acum.T
    eacum = jnp.exp(acum)
    alast = acum[L - 1:L, :]
    edecay = jnp.exp(alast - acum)
    elast = jnp.exp(alast)

    dt_full = jnp.concatenate([jnp.broadcast_to(dt[:, h:h + 1], (L, HEAD_DIM)) for h in range(SSD_HEADS)], 1)
    dec_full = jnp.concatenate([jnp.broadcast_to(edecay[:, h:h + 1], (L, HEAD_DIM)) for h in range(SSD_HEADS)], 1)
    xdt = xs * dt_full
    xdec_t = (xdt * dec_full).T

    hpg = SSD_HEADS // SSD_GROUPS
    y_parts = []
    for h in range(SSD_HEADS):
        g = h // hpg
        b_g = bm[:, g * SSD_STATE:(g + 1) * SSD_STATE]
        c_g = cm[:, g * SSD_STATE:(g + 1) * SSD_STATE]
        if h % hpg == 0:
            cb = _bdot_nt(c_g, b_g)
        seg = acum[:, h:h + 1] - acum_t[h:h + 1, :]
        lmat = jnp.where(tril, jnp.exp(jnp.where(tril, seg, 0.0)), 0.0)
        xdt_h = xdt[:, h * HEAD_DIM:(h + 1) * HEAD_DIM]
        y_h = _bdot(cb * lmat, xdt_h)
        s_prev = s_ref[h]
        y_h = y_h + _bdot_nt(c_g, s_prev) * eacum[:, h:h + 1]
        y_h = y_h + dskip_ref[:, h * HEAD_DIM:(h + 1) * HEAD_DIM] * xs[:, h * HEAD_DIM:(h + 1) * HEAD_DIM]
        y_parts.append(y_h)
        s_ref[h] = elast[:, h:h + 1] * s_prev + _bdot(xdec_t[h * HEAD_DIM:(h + 1) * HEAD_DIM, :], b_g)
    y = jnp.concatenate(y_parts, axis=1)
    y_ref[...] = _gated_group_norm(y, z_ref[...], normw_ref[...])

    @pl.when(c == nc - 1)
    def _():
        state_ref[0] = s_ref[...]
        conv_ref[0] = xin[L - (SSD_CONV - 1):L, :]


def _ssd_prompt(xbc, z, dtg, conv_w, conv_b, dt_bias_pad, a_log_pad, d_skip_full, norm_w, bn, t):
    nc = t // SSD_CHUNK
    row = lambda w: pl.BlockSpec((SSD_CHUNK, w), lambda b, c: (b * nc + c, 0))
    const = lambda a: pl.BlockSpec(a.shape, lambda b, c: (0,) * a.ndim)
    return pl.pallas_call(
        _ssd_prompt_kernel,
        grid=(bn, nc),
        in_specs=[row(SSD_CONV_CH), row(SSD_D), row(LANES), const(conv_w), const(conv_b), const(dt_bias_pad),
                  const(a_log_pad), const(d_skip_full), const(norm_w)],
        out_specs=[row(SSD_D),
                   pl.BlockSpec((1, SSD_HEADS, HEAD_DIM, SSD_STATE), lambda b, c: (b, 0, 0, 0)),
                   pl.BlockSpec((1, SSD_CONV - 1, SSD_CONV_CH), lambda b, c: (b, 0, 0))],
        out_shape=[jax.ShapeDtypeStruct((bn * t, SSD_D), F32),
                   jax.ShapeDtypeStruct((bn, SSD_HEADS, HEAD_DIM, SSD_STATE), F32),
                   jax.ShapeDtypeStruct((bn, SSD_CONV - 1, SSD_CONV_CH), F32)],
        scratch_shapes=[pltpu.VMEM((SSD_CHUNK + 2 * SUBLANES, SSD_CONV_CH), F32),
                        pltpu.VMEM((SSD_HEADS, HEAD_DIM, SSD_STATE), F32)],
        compiler_params=_cparams(("parallel", "arbitrary")),
        name="ssd_prompt",
    )(xbc, z, dtg, conv_w, conv_b, dt_bias_pad, a_log_pad, d_skip_full, norm_w)


def _pad_lanes(v, fill=0.0):
    return jnp.concatenate([v.astype(F32), jnp.full((LANES - v.shape[0],), fill, F32)])[None]


def _compress_rows(k_ref, v_ref, pe_ref, w1k_ref, w1v_ref, b1_ref, w2k_ref, w2v_ref, b2_ref, nb):
    acck = jnp.zeros((nb, 2 * CMP_HIDDEN), F32)
    accv = jnp.zeros((nb, 2 * CMP_HIDDEN), F32)
    for l in range(CMP_BLOCK):
        xk = k_ref[pl.ds(l, nb, stride=CMP_BLOCK), :] + pe_ref[l:l + 1, 0:KV_D]
        xv = v_ref[pl.ds(l, nb, stride=CMP_BLOCK), :] + pe_ref[l:l + 1, KV_D:2 * KV_D]
        acck = acck + jnp.dot(xk.astype(BF16), w1k_ref[l], preferred_element_type=F32)
        accv = accv + jnp.dot(xv.astype(BF16), w1v_ref[l], preferred_element_type=F32)
    hk = _silu(acck + b1_ref[:, 0:2 * CMP_HIDDEN])
    hv = _silu(accv + b1_ref[:, 2 * CMP_HIDDEN:4 * CMP_HIDDEN])
    ok = jnp.dot(hk.astype(BF16), w2k_ref[...], preferred_element_type=F32) + b2_ref[:, 0:KV_D]
    ov = jnp.dot(hv.astype(BF16), w2v_ref[...], preferred_element_type=F32) + b2_ref[:, KV_D:2 * KV_D]
    return jnp.concatenate([ok, ov], axis=1)


def _compress_kernel(k_ref, v_ref, pe_ref, w1k_ref, w1v_ref, b1_ref, w2k_ref, w2v_ref, b2_ref, o_ref, *, nb):
    o_ref[...] = _compress_rows(k_ref, v_ref, pe_ref, w1k_ref, w1v_ref, b1_ref, w2k_ref, w2v_ref, b2_ref, nb)


def _block_diag2(w):
    z = jnp.zeros_like(w)
    return jnp.concatenate([jnp.concatenate([w, z], -1), jnp.concatenate([z, w], -1)], -2)


def _compress_consts(cmp_pe, cmp_w1, cmp_b1, cmp_w2, cmp_b2):
    pe = jnp.concatenate([cmp_pe[0], cmp_pe[0], cmp_pe[1], cmp_pe[1]], -1)
    w1k = _block_diag2(cmp_w1[0]).astype(BF16)
    w1v = _block_diag2(cmp_w1[1]).astype(BF16)
    b1 = jnp.concatenate([cmp_b1[0], cmp_b1[0], cmp_b1[1], cmp_b1[1]])[None]
    w2k = _block_diag2(cmp_w2[0]).astype(BF16)
    w2v = _block_diag2(cmp_w2[1]).astype(BF16)
    b2 = jnp.concatenate([cmp_b2[0], cmp_b2[0], cmp_b2[1], cmp_b2[1]])[None]
    return pe, w1k, w1v, b1, w2k, w2v, b2


def _compress_prompt(kvc, consts, rows_per_step):
    n = kvc.shape[0]
    nb = rows_per_step // CMP_BLOCK
    const = lambda a: pl.BlockSpec(a.shape, lambda i: (0,) * a.ndim)
    return pl.pallas_call(
        functools.partial(_compress_kernel, nb=nb),
        grid=(n // rows_per_step,),
        in_specs=[pl.BlockSpec((rows_per_step, KV_D), lambda i: (i, 0)),
                  pl.BlockSpec((rows_per_step, KV_D), lambda i: (i, 1))] + [const(a) for a in consts],
        out_specs=pl.BlockSpec((nb, 2 * KV_D), lambda i: (i, 0)),
        out_shape=jax.ShapeDtypeStruct((n // CMP_BLOCK, 2 * KV_D), F32),
        compiler_params=_cparams(("parallel",)),
        name="compress_prompt",
    )(kvc, kvc, *consts)


SEL_KEY_TILE = 512
WIN_KEYS = WINDOW + Q_BLOCK


def _dup_head(x, hk):
    sw = pltpu.roll(x, HEAD_DIM, 1)
    low = lax.broadcasted_iota(jnp.int32, x.shape, 1) < HEAD_DIM
    return jnp.where(low, x, sw) if hk == 0 else jnp.where(low, sw, x)


def _masked_softmax(s, mask):
    sm = jnp.where(mask, s, NEG)
    ex = jnp.where(mask, jnp.exp(sm - jnp.max(sm, axis=-1, keepdims=True)), 0.0)
    den = jnp.sum(ex, axis=-1, keepdims=True)
    return ex / jnp.where(den > 0.0, den, 1.0)


def _select_blocks_t(imp, cur, n_top):
    j = lax.broadcasted_iota(jnp.int32, imp.shape, 0)
    future = j > cur
    forced = (j == 0) | (j == cur) | (j == cur - 1)
    score = jnp.where(future, NEG, jnp.where(forced, FORCED_SCORE, imp))
    return ((_rank_rows(score) < n_top) & (score > 0.5 * NEG)).astype(F32)


def _nsa_prompt_kernel(qc_ref, qr_ref, dtg_ref, cmp_ref, kvs_ref, kvw_ref, o_ref,
                       cmp_d, kvs_d, kvw_d, bias_ref, qrs_ref, m_ref, l_ref, acc_ref, *, t):
    qb = pl.program_id(1)
    nbk = t // SEL_BLOCK
    tq = Q_BLOCK
    tk = SEL_KEY_TILE
    hpg = NSA_HEADS // NSA_KV_HEADS
    scale = HEAD_DIM ** -0.5

    @pl.when(qb == 0)
    def _():
        cmp_d[...] = jnp.zeros_like(cmp_d)
        for src, dst, n in ((cmp_ref, cmp_d, nbk), (kvs_ref, kvs_d, t), (kvw_ref, kvw_d, t)):
            x = src[...]
            for hk in range(NSA_KV_HEADS):
                dst[hk, 0:n, 0:KV_D] = _dup_head(x[:, 0:KV_D], hk).astype(BF16)
                dst[hk, 0:n, KV_D:2 * KV_D] = _dup_head(x[:, KV_D:2 * KV_D], hk).astype(BF16)

    t0 = qb * tq
    rows = t0 + lax.broadcasted_iota(jnp.int32, (tq, 1), 0)
    lane = lax.broadcasted_iota(jnp.int32, (tq, LANES), 1)
    half_mask = (lane < HEAD_DIM, lane >= HEAD_DIM)
    sig = _sigmoid(dtg_ref[...])
    vis = (lane + 1) * CMP_BLOCK - 1 <= rows
    cur_l = (t0 + lax.broadcasted_iota(jnp.int32, (1, tq), 1)) // SEL_BLOCK
    expand = (lax.broadcasted_iota(jnp.int32, (LANES, t), 1) // SEL_BLOCK
              == lax.broadcasted_iota(jnp.int32, (LANES, t), 0)).astype(BF16)
    win_start = pl.multiple_of(jnp.maximum(t0 - WINDOW, 0), tq)
    wpos = win_start + lax.broadcasted_iota(jnp.int32, (tq, WIN_KEYS), 1)
    win_bias = jnp.where((wpos <= rows) & (wpos >= rows - WINDOW), 0.0, NEG)
    n_kt = (t0 + tq + tk - 1) // tk

    def stack_heads(ref, hk):
        parts = []
        for hh in range(hpg):
            head = hk * hpg + hh
            p, e = head // 2, head % 2
            parts.append(jnp.where(half_mask[e], ref[:, p * LANES:(p + 1) * LANES] * scale, 0.0))
        return jnp.concatenate(parts, axis=0).astype(BF16)

    o_cmp_g = []
    for hk in range(NSA_KV_HEADS):
        qcs = stack_heads(qc_ref, hk)
        s = lax.dot_general(qcs, cmp_d[hk, :, 0:KV_D], (((1,), (1,)), ((), ())), preferred_element_type=F32)
        pc = _masked_softmax(s.reshape(hpg, tq, LANES), vis[None])
        imp = jnp.sum(pc, axis=0)
        o_cmp_g.append(jnp.dot(pc.reshape(hpg * tq, LANES).astype(BF16), cmp_d[hk, :, KV_D:2 * KV_D],
                               preferred_element_type=F32).reshape(hpg, tq, LANES))

        sel_t = _select_blocks_t(imp.T[0:nbk, :], cur_l, TOP_N)
        sel = jnp.concatenate([sel_t, jnp.zeros((LANES - nbk, tq), F32)], axis=0).T
        selk = jnp.dot(sel.astype(BF16), expand, preferred_element_type=F32)
        for kt in range(t // tk):
            @pl.when(kt < n_kt)
            def _(kt=kt, hk=hk, selk=selk):
                kpos = kt * tk + lax.broadcasted_iota(jnp.int32, (tq, tk), 1)
                bias_ref[hk, kt] = jnp.where((selk[:, kt * tk:(kt + 1) * tk] > 0.5) & (kpos <= rows), 0.0, NEG)

        qrs_ref[hk] = stack_heads(qr_ref, hk)

    m_ref[...] = jnp.full(m_ref.shape, NEG, F32)
    l_ref[...] = jnp.zeros(l_ref.shape, F32)
    acc_ref[...] = jnp.zeros(acc_ref.shape, F32)

    def sel_step(kt, carry):
        k0 = pl.multiple_of(kt * tk, tk)
        for hk in range(NSA_KV_HEADS):
            kblk = kvs_d[hk, pl.ds(k0, tk), 0:KV_D]
            vblk = kvs_d[hk, pl.ds(k0, tk), KV_D:2 * KV_D]
            s = lax.dot_general(qrs_ref[hk], kblk, (((1,), (1,)), ((), ())), preferred_element_type=F32)
            s = s.reshape(hpg, tq, tk) + bias_ref[hk, kt][None]
            m_old = m_ref[hk]
            m_new = jnp.maximum(m_old, jnp.max(s, axis=-1, keepdims=True))
            alpha = jnp.exp(m_old - m_new)
            pe = jnp.exp(s - jnp.concatenate([m_new] * (tk // LANES), axis=-1))
            l_ref[hk] = alpha * l_ref[hk] + jnp.sum(pe, axis=-1, keepdims=True)
            pv = jnp.dot(pe.reshape(hpg * tq, tk).astype(BF16), vblk, preferred_element_type=F32)
            acc_ref[hk] = alpha * acc_ref[hk] + pv.reshape(hpg, tq, LANES)
            m_ref[hk] = m_new
        return carry

    lax.fori_loop(0, n_kt, sel_step, 0)

    for hk in range(NSA_KV_HEADS):
        o_cmp = o_cmp_g[hk]
        o_slc = acc_ref[hk] / l_ref[hk]
        kw = kvw_d[hk, pl.ds(win_start, WIN_KEYS), 0:KV_D]
        vw = kvw_d[hk, pl.ds(win_start, WIN_KEYS), KV_D:2 * KV_D]
        sw = lax.dot_general(qrs_ref[hk], kw, (((1,), (1,)), ((), ())), preferred_element_type=F32)
        sw = sw.reshape(hpg, tq, WIN_KEYS) + win_bias[None]
        pw = jnp.exp(sw - jnp.max(sw, axis=-1, keepdims=True))
        den = jnp.sum(pw, axis=-1, keepdims=True)
        o_win = jnp.dot(pw.reshape(hpg * tq, WIN_KEYS).astype(BF16), vw,
                        preferred_element_type=F32).reshape(hpg, tq, LANES) / den

        for hh in range(hpg):
            head = hk * hpg + hh
            p, e = head // 2, head % 2
            c0 = GATE_COL0 + head * 3
            mix = (sig[:, c0:c0 + 1] * o_cmp[hh] + sig[:, c0 + 1:c0 + 2] * o_slc[hh]
                   + sig[:, c0 + 2:c0 + 3] * o_win[hh])
            if e == 0:
                mix_even = mix
            else:
                o_ref[:, p * LANES:(p + 1) * LANES] = jnp.where(half_mask[0], mix_even, mix)


def _nsa_prompt(qc, qr, dtg, kvcmp, kvs, kvw, bn, t):
    nq = t // Q_BLOCK
    nbk = t // SEL_BLOCK
    hpg = NSA_HEADS // NSA_KV_HEADS
    assert nbk >= TOP_N and t >= WIN_KEYS and t % SEL_KEY_TILE == 0
    qrow = lambda w: pl.BlockSpec((Q_BLOCK, w), lambda b, i: (b * nq + i, 0))
    seq = lambda r: pl.BlockSpec((r, 2 * KV_D), lambda b, i: (b, 0))
    return pl.pallas_call(
        functools.partial(_nsa_prompt_kernel, t=t),
        grid=(bn, nq),
        in_specs=[qrow(NSA_D), qrow(NSA_D), qrow(LANES), seq(nbk), seq(t), seq(t)],
        out_specs=qrow(NSA_D),
        out_shape=jax.ShapeDtypeStruct((bn * t, NSA_D), F32),
        scratch_shapes=[pltpu.VMEM((NSA_KV_HEADS, LANES, 2 * KV_D), BF16),
                        pltpu.VMEM((NSA_KV_HEADS, t, 2 * KV_D), BF16),
                        pltpu.VMEM((NSA_KV_HEADS, t, 2 * KV_D), BF16),
                        pltpu.VMEM((NSA_KV_HEADS, t // SEL_KEY_TILE, Q_BLOCK, SEL_KEY_TILE), F32),
                        pltpu.VMEM((NSA_KV_HEADS, hpg * Q_BLOCK, LANES), BF16),
                        pltpu.VMEM((NSA_KV_HEADS, hpg, Q_BLOCK, LANES), F32),
                        pltpu.VMEM((NSA_KV_HEADS, hpg, Q_BLOCK, LANES), F32),
                        pltpu.VMEM((NSA_KV_HEADS, hpg, Q_BLOCK, LANES), F32)],
        compiler_params=_cparams(("parallel", "arbitrary")),
        name="nsa_prompt",
    )(qc, qr, dtg, kvcmp, kvs, kvw)


def _outproj_kernel(ys_ref, yn_ref, h_ref, ws_ref, wn_ref, g_ref, b_ref, o_ref):
    mix = jnp.dot(ys_ref[...].astype(BF16), ws_ref[...], preferred_element_type=F32)
    mix = mix + jnp.dot(yn_ref[...].astype(BF16), wn_ref[...], preferred_element_type=F32)
    o_ref[...] = _layer_norm(DEEPNORM_ALPHA * h_ref[...] + mix, g_ref[...], b_ref[...])


def _outproj(y_ssd, y_nsa, h, w_ssd, w_nsa, ln_g, ln_b, tm):
    n = h.shape[0]
    row = lambda w: pl.BlockSpec((tm, w), lambda i: (i, 0))
    const = lambda a: pl.BlockSpec(a.shape, lambda i: (0,) * a.ndim)
    return pl.pallas_call(
        _outproj_kernel,
        grid=(n // tm,),
        in_specs=[row(SSD_D), row(NSA_D), row(D_MODEL), const(w_ssd), const(w_nsa), const(ln_g), const(ln_b)],
        out_specs=row(D_MODEL),
        out_shape=jax.ShapeDtypeStruct((n, D_MODEL), F32),
        compiler_params=_cparams(("parallel",)),
        name="outproj",
    )(y_ssd, y_nsa, h, w_ssd, w_nsa, ln_g, ln_b)


MOE_TOKENS = 256
ROUTE_ROWS = 8


def _rank_rows(x):
    n = x.shape[0]
    idx = lax.broadcasted_iota(jnp.int32, x.shape, 0)
    rank = jnp.zeros(x.shape, F32)
    for r in range(n):
        row = x[r:r + 1, :]
        rank = rank + ((row > x) | ((row == x) & (idx > r))).astype(F32)
    return rank


def _route_kernel(h_ref, rw_ref, rb_ref, slot_ref, tokinfo_ref, meta_ref, cnt_ref, carry_ref, carry_row_ref, *,
                  n_valid):
    i = pl.program_id(0)
    tm = MOE_TOKENS

    @pl.when(i == 0)
    def _():
        carry_ref[...] = jnp.zeros_like(carry_ref)
        carry_row_ref[...] = jnp.zeros_like(carry_row_ref)

    logits = lax.dot_general(rw_ref[...], h_ref[...].astype(BF16), (((1,), (1,)), ((), ())),
                             preferred_element_type=F32)
    scores = _sigmoid(logits)
    biased = scores + rb_ref[:, 0:1]
    b3 = biased.reshape(N_EXPERT_GROUPS, EXPERTS_PER_GROUP, tm)
    sidx = lax.broadcasted_iota(jnp.int32, b3.shape, 1)
    m1 = jnp.max(b3, axis=1, keepdims=True)
    first = jnp.min(jnp.where(b3 == m1, sidx, EXPERTS_PER_GROUP), axis=1, keepdims=True)
    m2 = jnp.max(jnp.where(sidx == first, -jnp.inf, b3), axis=1, keepdims=True)
    grp_score = (m1 + m2).reshape(N_EXPERT_GROUPS, tm)
    grp_keep = _rank_rows(grp_score) < TOPK_GROUPS
    masked = jnp.where(grp_keep.reshape(N_EXPERT_GROUPS, 1, tm), b3, NEG).reshape(N_EXPERTS, tm)
    rank = _rank_rows(masked)
    tok = i * tm + lax.broadcasted_iota(jnp.int32, (1, tm), 1)
    valid = tok < n_valid
    sel = (rank < TOP_K) & valid
    self32 = sel.astype(F32)
    wsel = self32 * scores
    wsum = jnp.sum(wsel, axis=0, keepdims=True)
    w = wsel / jnp.where(wsum > 0.0, wsum, 1.0) * ROUTED_SCALE

    selb = sel.astype(BF16)
    tri = lambda n, strict_upper: (
        (lax.broadcasted_iota(jnp.int32, (n, n), 0) < lax.broadcasted_iota(jnp.int32, (n, n), 1))
        if strict_upper else
        (lax.broadcasted_iota(jnp.int32, (n, n), 0) > lax.broadcasted_iota(jnp.int32, (n, n), 1))).astype(BF16)
    pad8 = lambda c: jnp.floor((c + (SUBLANES - 1.0)) * (1.0 / SUBLANES)) * SUBLANES
    pos_tile = jnp.dot(selb, tri(tm, True), preferred_element_type=F32)
    cnt_col = pad8(jnp.sum(self32, axis=1, keepdims=True))
    first_col = jnp.dot(tri(N_EXPERTS, False), jnp.broadcast_to(cnt_col, (N_EXPERTS, LANES)).astype(BF16),
                        preferred_element_type=F32)[:, 0:1]
    slot = first_col + pos_tile

    sel_pad = jnp.concatenate([selb, jnp.zeros((LANES - N_EXPERTS, tm), BF16)], axis=0)
    cnt_row = pad8(lax.dot_general(jnp.ones((SUBLANES, tm), BF16), sel_pad, (((1,), (1,)), ((), ())),
                                   preferred_element_type=F32))
    first_row = jnp.dot(cnt_row.astype(BF16), tri(LANES, True), preferred_element_type=F32)
    prev_row = carry_row_ref[...]
    meta = jnp.concatenate([cnt_row[0:1], first_row[0:1], prev_row[0:1], jnp.zeros((SUBLANES - 3, LANES), F32)], 0)
    meta_ref[0] = meta.astype(jnp.int32)
    carry_row_ref[...] = prev_row + cnt_row
    carry_ref[...] = carry_ref[...] + cnt_col

    slot_rows, w_rows = [], []
    for k in range(TOP_K):
        hit = (rank == k) & sel
        slot_rows.append(jnp.sum(jnp.where(hit, slot, 0.0), axis=0, keepdims=True))
        w_rows.append(jnp.sum(jnp.where(hit, w, 0.0), axis=0, keepdims=True))
    slot_rows = [jnp.where(valid, r, -1.0) for r in slot_rows]
    pad2 = jnp.zeros((ROUTE_ROWS - TOP_K, tm), F32)
    slot_ref[...] = jnp.concatenate(slot_rows + [pad2 - 1.0], 0).astype(jnp.int32)
    info = jnp.concatenate(w_rows + [pad2] + slot_rows + [jnp.zeros((LANES - ROUTE_ROWS - TOP_K, tm), F32)], 0)
    tokinfo_ref[...] = info.T

    @pl.when(i == pl.num_programs(0) - 1)
    def _():
        cnt_ref[...] = jnp.broadcast_to(carry_ref[:, 0:1], cnt_ref.shape)


def _route(h, router_wt, router_bias_col, n_valid):
    n = h.shape[0]
    tm = MOE_TOKENS
    const = lambda a: pl.BlockSpec(a.shape, lambda i: (0,) * a.ndim)
    return pl.pallas_call(
        functools.partial(_route_kernel, n_valid=n_valid),
        grid=(n // tm,),
        in_specs=[pl.BlockSpec((tm, D_MODEL), lambda i: (i, 0)), const(router_wt), const(router_bias_col)],
        out_specs=[pl.BlockSpec((ROUTE_ROWS, tm), lambda i: (0, i)),
                   pl.BlockSpec((tm, LANES), lambda i: (i, 0)),
                   pl.BlockSpec((1, SUBLANES, LANES), lambda i: (i, 0, 0)),
                   pl.BlockSpec((N_EXPERTS, LANES), lambda i: (0, 0))],
        out_shape=[jax.ShapeDtypeStruct((ROUTE_ROWS, n), jnp.int32),
                   jax.ShapeDtypeStruct((n, LANES), F32),
                   jax.ShapeDtypeStruct((n // tm, SUBLANES, LANES), jnp.int32),
                   jax.ShapeDtypeStruct((N_EXPERTS, LANES), F32)],
        scratch_shapes=[pltpu.VMEM((N_EXPERTS, LANES), F32), pltpu.VMEM((SUBLANES, LANES), F32)],
        compiler_params=_cparams(("arbitrary",)),
        name="moe_route",
    )(h, router_wt, router_bias_col)


TILE_SLOTS = MOE_TOKENS * TOP_K + N_EXPERTS * SUBLANES
RUN_CHUNKS = tuple(1 << b for b in range(int(math.log2(MOE_TOKENS)), int(math.log2(SUBLANES)) - 1, -1))


def _run_copy(src_ref, src_row, dst_ref, dst_row, rows, sem):
    return pltpu.make_async_copy(src_ref.at[pl.ds(pl.multiple_of(src_row, SUBLANES), rows)],
                                 dst_ref.at[pl.ds(pl.multiple_of(dst_row, SUBLANES), rows)], sem)


def _start_run(src_ref, src_row, dst_ref, dst_row, n, sem, started):
    off = jnp.int32(0)
    out = []
    for c, rows in enumerate(RUN_CHUNKS):
        take = (n & rows) != 0

        @pl.when(take)
        def _(off=off, rows=rows):
            _run_copy(src_ref, src_row + off, dst_ref, dst_row + off, rows, sem).start()

        inc = take.astype(jnp.int32)
        off = off + inc * rows
        out.append(started[c] + inc)
    return tuple(out)


def _wait_runs(src_ref, dst_ref, sem, started):
    for c, rows in enumerate(RUN_CHUNKS):
        def wait_one(j, carry, rows=rows):
            _run_copy(src_ref, 0, dst_ref, 0, rows, sem).wait()
            return carry

        lax.fori_loop(0, started[c], wait_one, 0)


def _dispatch_kernel(start_ref, cnt_ref, meta_ref, slot_ref, x_ref, xs_ref, sorted_ref, zero_ref, sem, zsem, *, cap):
    i = pl.program_id(0)
    tm = MOE_TOKENS

    @pl.when(i == 0)
    def _():
        zero_ref[...] = jnp.zeros_like(zero_ref)

        def fill_expert(e, started):
            lo = start_ref[e] + cnt_ref[e]
            hi = jnp.where(e == N_EXPERTS - 1, cap, start_ref[jnp.minimum(e + 1, N_EXPERTS - 1)])
            n_full = (hi - lo) // tm

            def fill_full(j, st):
                return _start_run(zero_ref, 0, xs_ref, lo + j * tm, jnp.int32(tm), zsem, st)

            started = lax.fori_loop(0, n_full, fill_full, started)
            return _start_run(zero_ref, 0, xs_ref, lo + n_full * tm, (hi - lo) - n_full * tm, zsem, started)

        filled = lax.fori_loop(0, N_EXPERTS, fill_expert, tuple(jnp.int32(0) for _ in RUN_CHUNKS))
        _wait_runs(zero_ref, xs_ref, zsem, filled)

    srow = lax.broadcasted_iota(jnp.int32, (TILE_SLOTS, tm), 0)
    onehot = srow == slot_ref[0:1, :]
    for k in range(1, TOP_K):
        onehot = onehot | (srow == slot_ref[k:k + 1, :])
    sorted_ref[...] = jnp.dot(onehot.astype(BF16), x_ref[...].astype(BF16), preferred_element_type=F32)

    def copy_expert(e, started):
        n = meta_ref[0, 0, e]
        return _start_run(sorted_ref, meta_ref[0, 1, e], xs_ref, start_ref[e] + meta_ref[0, 2, e], n, sem, started)

    started = lax.fori_loop(0, N_EXPERTS, copy_expert, tuple(jnp.int32(0) for _ in RUN_CHUNKS))
    _wait_runs(sorted_ref, xs_ref, sem, started)


def _dispatch(h, slot_t, meta, seg_start, counts, cap):
    n = h.shape[0]
    tm = MOE_TOKENS
    return pl.pallas_call(
        functools.partial(_dispatch_kernel, cap=cap),
        grid_spec=pltpu.PrefetchScalarGridSpec(
            num_scalar_prefetch=2,
            grid=(n // tm,),
            in_specs=[pl.BlockSpec((1, SUBLANES, LANES), lambda i, *_: (i, 0, 0), memory_space=pltpu.SMEM),
                      pl.BlockSpec((ROUTE_ROWS, tm), lambda i, *_: (0, i)),
                      pl.BlockSpec((tm, D_MODEL), lambda i, *_: (i, 0))],
            out_specs=pl.BlockSpec(memory_space=pl.ANY),
            scratch_shapes=[pltpu.VMEM((TILE_SLOTS, D_MODEL), F32), pltpu.VMEM((tm, D_MODEL), F32),
                            pltpu.SemaphoreType.DMA, pltpu.SemaphoreType.DMA]),
        out_shape=jax.ShapeDtypeStruct((cap, D_MODEL), F32),
        compiler_params=_cparams(("arbitrary",)),
        name="moe_dispatch",
    )(seg_start, counts, meta, slot_t, h)


def _swiglu(x, wg, wu, wd):
    xb = x.astype(BF16)
    g = jnp.dot(xb, wg.astype(BF16), preferred_element_type=F32)
    u = jnp.dot(xb, wu.astype(BF16), preferred_element_type=F32)
    return jnp.dot((_silu(g) * u).astype(BF16), wd.astype(BF16), preferred_element_type=F32)


def _experts_kernel(be_ref, used_ref, x_ref, wg_ref, wu_ref, wd_ref, y_ref):
    i = pl.program_id(0)

    @pl.when(i < used_ref[0])
    def _():
        y_ref[...] = _swiglu(x_ref[...], wg_ref[0], wu_ref[0], wd_ref[0])

    @pl.when(i >= used_ref[0])
    def _():
        y_ref[...] = jnp.zeros_like(y_ref)


def _experts(xs, block_expert, used_blocks, w_gate, w_up, w_down):
    cap = xs.shape[0]
    return pl.pallas_call(
        _experts_kernel,
        grid_spec=pltpu.PrefetchScalarGridSpec(
            num_scalar_prefetch=2,
            grid=(cap // MOE_BLOCK,),
            in_specs=[pl.BlockSpec((MOE_BLOCK, D_MODEL), lambda i, be, used: (i, 0)),
                      pl.BlockSpec((1, D_MODEL, D_EXPERT), lambda i, be, used: (be[i], 0, 0)),
                      pl.BlockSpec((1, D_MODEL, D_EXPERT), lambda i, be, used: (be[i], 0, 0)),
                      pl.BlockSpec((1, D_EXPERT, D_MODEL), lambda i, be, used: (be[i], 0, 0))],
            out_specs=pl.BlockSpec((MOE_BLOCK, D_MODEL), lambda i, be, used: (i, 0))),
        out_shape=jax.ShapeDtypeStruct((cap, D_MODEL), F32),
        compiler_params=_cparams(("arbitrary",)),
        name="moe_experts",
    )(block_expert, used_blocks, xs, w_gate, w_up, w_down)


def _combine_kernel(start_ref, meta_ref, h_ref, info_ref, sg_ref, su_ref, sd_ref, g_ref, b_ref, ys_ref,
                    o_ref, buf_ref, sem):
    i = pl.program_id(0)
    tm = MOE_TOKENS

    @pl.when(i == 0)
    def _():
        buf_ref[...] = jnp.zeros_like(buf_ref)

    def fetch_expert(e, started):
        n = meta_ref[0, 0, e]
        return _start_run(ys_ref, start_ref[e] + meta_ref[0, 2, e], buf_ref, meta_ref[0, 1, e], n, sem, started)

    started = lax.fori_loop(0, N_EXPERTS, fetch_expert, tuple(jnp.int32(0) for _ in RUN_CHUNKS))
    h = h_ref[...]
    f = _swiglu(h, sg_ref[...], su_ref[...], sd_ref[...])
    info = info_ref[...]
    scol = lax.broadcasted_iota(jnp.int32, (tm, TILE_SLOTS), 1).astype(F32)
    mix = jnp.zeros((tm, TILE_SLOTS), F32)
    for k in range(TOP_K):
        mix = mix + jnp.where(info[:, ROUTE_ROWS + k:ROUTE_ROWS + k + 1] == scol, info[:, k:k + 1], 0.0)
    _wait_runs(ys_ref, buf_ref, sem, started)
    acc = jnp.dot(mix.astype(BF16), buf_ref[...].astype(BF16), preferred_element_type=F32)
    o_ref[...] = _layer_norm(DEEPNORM_ALPHA * h + (acc + f), g_ref[...], b_ref[...])


def _combine(h, ys, meta, tokinfo, seg_start, sh_gate, sh_up, sh_down, ln_g, ln_b):
    n = h.shape[0]
    tm = MOE_TOKENS
    const = lambda a: pl.BlockSpec(a.shape, lambda i, *_: (0,) * a.ndim)
    return pl.pallas_call(
        _combine_kernel,
        grid_spec=pltpu.PrefetchScalarGridSpec(
            num_scalar_prefetch=1,
            grid=(n // tm,),
            in_specs=[pl.BlockSpec((1, SUBLANES, LANES), lambda i, *_: (i, 0, 0), memory_space=pltpu.SMEM),
                      pl.BlockSpec((tm, D_MODEL), lambda i, *_: (i, 0)),
                      pl.BlockSpec((tm, LANES), lambda i, *_: (i, 0)),
                      const(sh_gate), const(sh_up), const(sh_down), const(ln_g), const(ln_b),
                      pl.BlockSpec(memory_space=pl.ANY)],
            out_specs=pl.BlockSpec((tm, D_MODEL), lambda i, *_: (i, 0)),
            scratch_shapes=[pltpu.VMEM((TILE_SLOTS, D_MODEL), F32), pltpu.SemaphoreType.DMA]),
        out_shape=jax.ShapeDtypeStruct((n, D_MODEL), F32),
        compiler_params=_cparams(("arbitrary",)),
        name="moe_combine",
    )(seg_start, meta, h, tokinfo, sh_gate, sh_up, sh_down, ln_g, ln_b, ys)


def _moe_ln(h, n_valid, router_w, router_bias, w_gate, w_up, w_down, sh_gate, sh_up, sh_down, ln_g, ln_b):
    slot_t, tokinfo, meta, cnt = _route(h, router_w.T.astype(BF16),
                                        jnp.broadcast_to(router_bias.astype(F32)[:, None], (N_EXPERTS, LANES)), n_valid)
    counts = cnt[:, 0].astype(jnp.int32)
    padded = (counts + MOE_BLOCK - 1) // MOE_BLOCK * MOE_BLOCK
    seg_end = jnp.cumsum(padded)
    seg_start = seg_end - padded
    run_pad = (h.shape[0] // MOE_TOKENS) * N_EXPERTS * (SUBLANES - 1)
    n_blocks = -(-(n_valid * TOP_K + run_pad + N_EXPERTS * (MOE_BLOCK - 1)) // MOE_BLOCK)
    cap = n_blocks * MOE_BLOCK
    block_first_row = jnp.arange(n_blocks, dtype=jnp.int32) * MOE_BLOCK
    block_expert = jnp.minimum(jnp.sum((seg_end[None, :] <= block_first_row[:, None]).astype(jnp.int32), axis=1),
                               N_EXPERTS - 1)
    xs = _dispatch(h, slot_t, meta, seg_start, counts, cap)
    used_blocks = (seg_end[N_EXPERTS - 1:] // MOE_BLOCK).astype(jnp.int32)
    ys = _experts(xs, block_expert, used_blocks, w_gate, w_up, w_down)
    return _combine(h, ys, meta, tokinfo, seg_start, sh_gate.astype(BF16), sh_up.astype(BF16),
                    sh_down.astype(BF16), ln_g, ln_b)


def _ssd_sample_kernel(xbc_ref, z_ref, dtg_ref, sconv_ref, s0_ref, convw_ref, convb_ref, dtb_ref, alog_ref,
                       dskip_ref, normw_ref, y_ref, s_ref, conv_out_ref, xc_ref, dt_ref, da_ref):
    b = pl.program_id(0)

    @pl.when(b == 0)
    def _():
        xin = xbc_ref[...]
        xc = convw_ref[SSD_CONV - 1:SSD_CONV, :] * xin
        for k in range(SSD_CONV - 1):
            xc = xc + convw_ref[k:k + 1, :] * sconv_ref[k]
        xc_ref[...] = _silu(xc + convb_ref[...])
        dt = _softplus(dtg_ref[...] + dtb_ref[...])
        dt_ref[...] = dt
        da_ref[...] = jnp.exp(dt * (-jnp.exp(alog_ref[...])))
        for k in range(SSD_CONV - 2):
            conv_out_ref[k] = sconv_ref[k + 1]
        conv_out_ref[SSD_CONV - 2] = xin

    xc = xc_ref[pl.ds(b, 1), :]
    dt = dt_ref[pl.ds(b, 1), :]
    da = da_ref[pl.ds(b, 1), :]
    ns = SSD_GROUPS * SSD_STATE
    eye = (lax.broadcasted_iota(jnp.int32, (HEAD_DIM, HEAD_DIM), 0)
           == lax.broadcasted_iota(jnp.int32, (HEAD_DIM, HEAD_DIM), 1))
    hpg = SSD_HEADS // SSD_GROUPS
    y_parts = []
    for h in range(SSD_HEADS):
        g = h // hpg
        x_h = xc[:, h * HEAD_DIM:(h + 1) * HEAD_DIM]
        b_g = xc[:, SSD_D + g * SSD_STATE:SSD_D + (g + 1) * SSD_STATE]
        c_g = xc[:, SSD_D + ns + g * SSD_STATE:SSD_D + ns + (g + 1) * SSD_STATE]
        xdt_col = jnp.sum(jnp.where(eye, x_h * dt[:, h:h + 1], 0.0), axis=1, keepdims=True)
        s_new = da[:, h:h + 1] * s0_ref[0, h] + xdt_col * b_g
        s_ref[0, h] = s_new
        y_h = _bdot_nt(c_g, s_new) + dskip_ref[:, h * HEAD_DIM:(h + 1) * HEAD_DIM] * x_h
        y_parts.append(y_h)
    y = jnp.concatenate(y_parts, axis=1)
    y_ref[pl.ds(b, 1), :] = _gated_group_norm(y, z_ref[pl.ds(b, 1), :], normw_ref[...])


def _ssd_sample(xbc, z, dtg, state_conv_t, state_ssm, conv_w, conv_b, dt_bias_pad, a_log_pad, d_skip_full, norm_w):
    bs = xbc.shape[0]
    const = lambda a: pl.BlockSpec(a.shape, lambda b: (0,) * a.ndim)
    state_spec = pl.BlockSpec((1, SSD_HEADS, HEAD_DIM, SSD_STATE), lambda b: (b, 0, 0, 0))
    return pl.pallas_call(
        _ssd_sample_kernel,
        grid=(bs,),
        in_specs=[const(xbc), const(z), const(dtg), const(state_conv_t), state_spec, const(conv_w), const(conv_b),
                  const(dt_bias_pad), const(a_log_pad), const(d_skip_full), const(norm_w)],
        out_specs=[pl.BlockSpec((bs, SSD_D), lambda b: (0, 0)), state_spec,
                   pl.BlockSpec((SSD_CONV - 1, bs, SSD_CONV_CH), lambda b: (0, 0, 0))],
        out_shape=[jax.ShapeDtypeStruct((bs, SSD_D), F32),
                   jax.ShapeDtypeStruct(state_ssm.shape, F32),
                   jax.ShapeDtypeStruct((SSD_CONV - 1, bs, SSD_CONV_CH), F32)],
        scratch_shapes=[pltpu.VMEM((bs, SSD_CONV_CH), F32), pltpu.VMEM((bs, LANES), F32),
                        pltpu.VMEM((bs, LANES), F32)],
        compiler_params=_cparams(("arbitrary",)),
        name="ssd_sample",
    )(xbc, z, dtg, state_conv_t, state_ssm, conv_w, conv_b, dt_bias_pad, a_log_pad, d_skip_full, norm_w)


PAGES_PER_STEP = 8
SEL_PAST = TOP_N - 1
BLOCKS_PER_PAGE = PAGE_SIZE // CMP_BLOCK
KV_FEATS = 2 * KV_D


def _compress_consts_t(cmp_pe, cmp_w1, cmp_b1, cmp_w2, cmp_b2):
    pe_t = jnp.stack([jnp.tile(cmp_pe[k].T, (1, BLOCKS_PER_PAGE)) for k in range(2)])
    w1_t = jnp.stack([_block_diag2(jnp.swapaxes(cmp_w1[k], 0, 1)) for k in range(2)]).astype(BF16)
    b1_t = jnp.stack([jnp.tile(cmp_b1[k], BLOCKS_PER_PAGE) for k in range(2)])[:, None, :]
    w2_t = jnp.stack([_block_diag2(cmp_w2[k]) for k in range(2)]).astype(BF16)
    b2_t = jnp.stack([jnp.tile(cmp_b2[k], BLOCKS_PER_PAGE) for k in range(2)])[:, None, :]
    return pe_t, w1_t, b1_t, w2_t, b2_t


def _compress_pages_kernel(pt_ref, *refs, n_pages):
    pages = refs[:PAGES_PER_STEP]
    pe_ref, w1_ref, b1_ref, w2_ref, b2_ref, o_ref, buf = refs[PAGES_PER_STEP:]
    s = pl.program_id(1)
    for j in range(PAGES_PER_STEP):
        r0 = pl.multiple_of((s * PAGES_PER_STEP + j) * KV_FEATS, KV_FEATS)
        buf[pl.ds(r0, KV_FEATS), :] = pages[j][0]

    @pl.when(s == pl.num_programs(1) - 1)
    def _():
        for kind in range(2):
            for h in range(NSA_KV_HEADS):
                base = kind * KV_D + h * HEAD_DIM

                def add_feature(d, acc):
                    x = buf[pl.ds(base + d, n_pages, stride=KV_FEATS), :] + pe_ref[kind, pl.ds(d, 1), :]
                    return acc + jnp.dot(x.astype(BF16), w1_ref[kind, d], preferred_element_type=F32)

                acc = lax.fori_loop(0, HEAD_DIM, add_feature,
                                    jnp.zeros((n_pages, BLOCKS_PER_PAGE * CMP_HIDDEN), F32), unroll=8)
                hid = _silu(acc + b1_ref[kind])
                o_ref[0, kind * NSA_KV_HEADS + h] = (
                    jnp.dot(hid.astype(BF16), w2_ref[kind], preferred_element_type=F32) + b2_ref[kind])


def _compress_pages(pool_t, page_table, consts):
    bs, n_pages = page_table.shape
    const = lambda a: pl.BlockSpec(a.shape, lambda b, s, pt: (0,) * a.ndim)

    def page_spec(j):
        return pl.BlockSpec((1, KV_FEATS, PAGE_SIZE), lambda b, s, pt: (pt[b, s * PAGES_PER_STEP + j], 0, 0))

    return pl.pallas_call(
        functools.partial(_compress_pages_kernel, n_pages=n_pages),
        grid_spec=pltpu.PrefetchScalarGridSpec(
            num_scalar_prefetch=1,
            grid=(bs, n_pages // PAGES_PER_STEP),
            in_specs=[page_spec(j) for j in range(PAGES_PER_STEP)] + [const(a) for a in consts],
            out_specs=pl.BlockSpec((1, 2 * NSA_KV_HEADS, n_pages, LANES), lambda b, s, pt: (b, 0, 0, 0)),
            scratch_shapes=[pltpu.VMEM((n_pages * KV_FEATS, PAGE_SIZE), F32)]),
        out_shape=jax.ShapeDtypeStruct((bs, 2 * NSA_KV_HEADS, n_pages, LANES), F32),
        compiler_params=_cparams(("arbitrary", "arbitrary")),
        name="compress_pages",
    )(page_table, *([pool_t] * PAGES_PER_STEP), *consts)


def _group_heads(q_row, hk):
    hpg = NSA_HEADS // NSA_KV_HEADS
    low = lax.broadcasted_iota(jnp.int32, (1, LANES), 1) < HEAD_DIM
    rows = []
    for r in range(hpg):
        head = hk * hpg + r
        tile = q_row[:, (head // 2) * LANES:(head // 2 + 1) * LANES]
        if head % 2 == 1:
            tile = pltpu.roll(tile, HEAD_DIM, 1)
        rows.append(jnp.where(low, tile, 0.0))
    return jnp.concatenate(rows + [jnp.zeros((SUBLANES - hpg, LANES), F32)], axis=0)


def _spread_heads(o_groups):
    hpg = NSA_HEADS // NSA_KV_HEADS
    return jnp.concatenate([o[r:r + 1, 0:HEAD_DIM] for o in o_groups for r in range(hpg)], axis=1)


def _nsa_sample_cmp_t_kernel(qc_ref, cmp_ref, ocmp_ref, idx_ref, *, n_pages):
    b = pl.program_id(0)
    nc = n_pages * BLOCKS_PER_PAGE
    scale = HEAD_DIM ** -0.5
    hpg = NSA_HEADS // NSA_KV_HEADS
    q_row = qc_ref[pl.ds(b, 1), :] * scale
    lane = lax.broadcasted_iota(jnp.int32, (1, LANES), 1)
    pos_r = lax.broadcasted_iota(jnp.int32, (1, nc), 1)
    bid_r = (pos_r % n_pages) * BLOCKS_PER_PAGE + pos_r // n_pages
    pos_c = lax.broadcasted_iota(jnp.int32, (nc, 1), 0)
    bid_c = (pos_c % n_pages) * BLOCKS_PER_PAGE + pos_c // n_pages
    o_groups = []
    for hk in range(NSA_KV_HEADS):
        kc = cmp_ref[0, hk].astype(BF16)
        vc = cmp_ref[0, NSA_KV_HEADS + hk].astype(BF16)
        qh = _group_heads(q_row, hk)
        s = jnp.concatenate(
            [lax.dot_general(pltpu.roll(qh, c * HEAD_DIM, 1).astype(BF16) if c else qh.astype(BF16), kc,
                             (((1,), (1,)), ((), ())), preferred_element_type=F32)
             for c in range(BLOCKS_PER_PAGE)], axis=1)
        ex = jnp.exp(s - jnp.max(s, axis=-1, keepdims=True))
        p = ex / jnp.sum(ex, axis=-1, keepdims=True)
        o = jnp.dot(p[:, 0:n_pages].astype(BF16), vc, preferred_element_type=F32)
        for c in range(1, BLOCKS_PER_PAGE):
            oc = jnp.dot(p[:, c * n_pages:(c + 1) * n_pages].astype(BF16), vc, preferred_element_type=F32)
            o = o + pltpu.roll(oc, LANES - c * HEAD_DIM, 1)
        o_groups.append(o)
        hrow = lax.broadcasted_iota(jnp.int32, p.shape, 0) < hpg
        imp = jnp.sum(jnp.where(hrow, p, 0.0), axis=0, keepdims=True)
        score = jnp.where((bid_r == 0) | (bid_r == nc - 1), FORCED_SCORE, imp)
        score_col = jnp.concatenate([score, jnp.zeros((LANES - 1, nc), F32)], 0).T[:, 0:1]
        beats = (score_col > score) | ((score_col == score) & (bid_c < bid_r))
        rank = jnp.sum(beats.astype(F32), axis=0, keepdims=True)
        row = jnp.zeros((1, LANES), F32)
        bid_f = bid_r.astype(F32)
        for k in range(SEL_PAST):
            blk = jnp.sum(jnp.where(rank == k, bid_f, 0.0), axis=1, keepdims=True)
            row = jnp.where(lane == k, blk, row)
        idx_ref[pl.ds(b * NSA_KV_HEADS + hk, 1), :] = row.astype(jnp.int32)
    ocmp_ref[pl.ds(b, 1), :] = _spread_heads(o_groups)


def _nsa_sample_cmp_t(qc, kvcmp_t):
    bs, _, n_pages, _ = kvcmp_t.shape
    return pl.pallas_call(
        functools.partial(_nsa_sample_cmp_t_kernel, n_pages=n_pages),
        grid=(bs,),
        in_specs=[pl.BlockSpec((bs, NSA_D), lambda b: (0, 0)),
                  pl.BlockSpec((1, 2 * NSA_KV_HEADS, n_pages, LANES), lambda b: (b, 0, 0, 0))],
        out_specs=[pl.BlockSpec((bs, NSA_D), lambda b: (0, 0)),
                   pl.BlockSpec((bs * NSA_KV_HEADS, LANES), lambda b: (0, 0))],
        out_shape=[jax.ShapeDtypeStruct((bs, NSA_D), F32),
                   jax.ShapeDtypeStruct((bs * NSA_KV_HEADS, LANES), jnp.int32)],
        compiler_params=_cparams(("arbitrary",)),
        name="nsa_sample_cmp",
    )(qc, kvcmp_t)


def _sel_block_copies(pool_ref, pt_ref, sel_ref, kbuf, vbuf, sem, b, hk, k):
    blk = sel_ref[b * NSA_KV_HEADS + hk, k]
    page = pt_ref[b, lax.shift_right_logical(blk, int(math.log2(BLOCKS_PER_PAGE)))]
    j = hk * SEL_PAST + k
    return (pltpu.make_async_copy(pool_ref.at[page, pl.ds(hk * HEAD_DIM, HEAD_DIM)], kbuf.at[j], sem),
            pltpu.make_async_copy(pool_ref.at[page, pl.ds(KV_D + hk * HEAD_DIM, HEAD_DIM)], vbuf.at[j], sem))


def _nsa_sample_attn_t_kernel(pt_ref, sel_ref, qr_ref, new_sel_ref, new_win_ref, win_ref, dtg_ref, ocmp_ref,
                              pool_ref, o_ref, kbuf, vbuf, sem):
    b = pl.program_id(0)
    for hk in range(NSA_KV_HEADS):
        for k in range(SEL_PAST):
            for cp in _sel_block_copies(pool_ref, pt_ref, sel_ref, kbuf, vbuf, sem, b, hk, k):
                cp.start()
    for hk in range(NSA_KV_HEADS):
        for k in range(SEL_PAST):
            for cp in _sel_block_copies(pool_ref, pt_ref, sel_ref, kbuf, vbuf, sem, b, hk, k):
                cp.wait()
    scale = HEAD_DIM ** -0.5
    q_row = qr_ref[pl.ds(b, 1), :] * scale
    sig = _sigmoid(dtg_ref[pl.ds(b, 1), :])
    lane = lax.broadcasted_iota(jnp.int32, (1, PAGE_SIZE), 1)
    o_slc, o_win = [], []
    for hk in range(NSA_KV_HEADS):
        qh = _group_heads(q_row, hk)[:, 0:HEAD_DIM].astype(BF16)

        def new_row(ref, kind):
            t = ref[pl.ds(b, 1), :][:, kind * KV_D:(kind + 1) * KV_D]
            if hk == 1:
                t = pltpu.roll(t, HEAD_DIM, 1)
            return t[:, 0:HEAD_DIM].astype(BF16).astype(F32)

        def attend(kt, vt, mask, new_ref, n_new):
            s = jnp.dot(qh, kt.astype(BF16), preferred_element_type=F32)
            if mask is not None:
                s = jnp.where(mask, s, NEG)
            s_new = jnp.sum(qh.astype(F32) * new_row(new_ref, 0), axis=1, keepdims=True)
            m = jnp.maximum(jnp.max(s, axis=-1, keepdims=True), s_new)
            ex = jnp.exp(s - m)
            ex_new = jnp.exp(s_new - m) * n_new
            den = jnp.sum(ex, axis=-1, keepdims=True) + ex_new
            o = lax.dot_general((ex / den).astype(BF16), vt.astype(BF16), (((1,), (1,)), ((), ())),
                                preferred_element_type=F32)
            return o + (ex_new / den).astype(BF16).astype(F32) * new_row(new_ref, 1)

        kt = jnp.concatenate([kbuf[hk * SEL_PAST + k] for k in range(SEL_PAST)], axis=1)
        vt = jnp.concatenate([vbuf[hk * SEL_PAST + k] for k in range(SEL_PAST)], axis=1)
        mask = jnp.concatenate(
            [lane // SEL_BLOCK == (sel_ref[b * NSA_KV_HEADS + hk, k] & (BLOCKS_PER_PAGE - 1))
             for k in range(SEL_PAST)], axis=1)
        o_slc.append(attend(kt, vt, mask, new_sel_ref, float(SEL_BLOCK)))
        o_win.append(attend(win_ref[0, hk * HEAD_DIM:(hk + 1) * HEAD_DIM, :],
                            win_ref[0, KV_D + hk * HEAD_DIM:KV_D + (hk + 1) * HEAD_DIM, :], None, new_win_ref, 1.0))
    gates = []
    for br in range(3):
        gates.append(jnp.concatenate(
            [jnp.broadcast_to(sig[:, GATE_COL0 + h * 3 + br:GATE_COL0 + h * 3 + br + 1], (1, HEAD_DIM))
             for h in range(NSA_HEADS)], axis=1))
    o_ref[pl.ds(b, 1), :] = (gates[0] * ocmp_ref[pl.ds(b, 1), :] + gates[1] * _spread_heads(o_slc)
                             + gates[2] * _spread_heads(o_win))


def _nsa_sample_attn_t(qr, new_sel, new_win, win_t, dtg, o_cmp, pool_sel_t, page_table, sel_idx):
    bs = qr.shape[0]
    const = lambda a: pl.BlockSpec(a.shape, lambda b, pt, sel: (0,) * a.ndim)
    n_buf = NSA_KV_HEADS * SEL_PAST
    return pl.pallas_call(
        _nsa_sample_attn_t_kernel,
        grid_spec=pltpu.PrefetchScalarGridSpec(
            num_scalar_prefetch=2,
            grid=(bs,),
            in_specs=[const(qr), const(new_sel), const(new_win),
                      pl.BlockSpec((1,) + win_t.shape[1:], lambda b, pt, sel: (b, 0, 0)),
                      const(dtg), const(o_cmp), pl.BlockSpec(memory_space=pl.ANY)],
            out_specs=pl.BlockSpec((bs, NSA_D), lambda b, pt, sel: (0, 0)),
            scratch_shapes=[pltpu.VMEM((n_buf, HEAD_DIM, PAGE_SIZE), F32), pltpu.VMEM((n_buf, HEAD_DIM, PAGE_SIZE), F32),
                            pltpu.SemaphoreType.DMA]),
        out_shape=jax.ShapeDtypeStruct((bs, NSA_D), F32),
        compiler_params=_cparams(("arbitrary",)),
        name="nsa_sample_attn",
    )(page_table, sel_idx, qr, new_sel, new_win, win_t, dtg, o_cmp, pool_sel_t)


def kernel(x_prompt, x_sample, cache_kv_cmp, cache_kv_sel, page_table, cache_kv_win, state_ssm, state_conv,
           emb_ln_g, emb_ln_b, w_in, conv_w, conv_b, dt_bias, a_log, d_skip, ssd_norm_w,
           cmp_pe, cmp_w1, cmp_b1, cmp_w2, cmp_b2, w_out, ln1_g, ln1_b,
           router_w, router_bias, exp_w_gate, exp_w_up, exp_w_down,
           sh_w_gate, sh_w_up, sh_w_down, ln2_g, ln2_b):
    bp, tp, _ = x_prompt.shape
    bs, ts, _ = x_sample.shape
    assert ts == 1 and DEPTH == 1
    n_prompt = bp * tp
    past_len = page_table.shape[1] * PAGE_SIZE
    l = 0
    w_perm = _permute_w_in(w_in[l])
    ln0_g, ln0_b = emb_ln_g[None], emb_ln_b[None]
    ssd_consts = (conv_w[l], conv_b[l][None], _pad_lanes(dt_bias[l]), _pad_lanes(a_log[l]),
                  jnp.repeat(d_skip[l], HEAD_DIM)[None], ssd_norm_w[l][None])
    cmp_consts = _compress_consts(cmp_pe[l], cmp_w1[l], cmp_b1[l], cmp_w2[l], cmp_b2[l])
    w_o = w_out[l].astype(BF16)
    w_o_ssd, w_o_nsa = w_o[:SSD_D], w_o[SSD_D:]
    ln1 = (ln1_g[l][None], ln1_b[l][None])
    kv_shape = (2, NSA_KV_HEADS, HEAD_DIM)

    hp, z, xbc, qc, qr, kvc, kvs, kvw, dtg = _inproj(
        x_prompt.reshape(n_prompt, D_MODEL), ln0_g, ln0_b, w_perm, _rope_tables(jnp.arange(tp)), 256)
    y_ssd, ssm_p, conv_p = _ssd_prompt(xbc, z, dtg, *ssd_consts, bp, tp)
    kvcmp = _compress_prompt(kvc, cmp_consts, tp)
    y_nsa = _nsa_prompt(qc, qr, dtg, kvcmp, kvs, kvw, bp, tp)
    h1p = _outproj(y_ssd, y_nsa, hp, w_o_ssd, w_o_nsa, *ln1, 256)
    n_keep = min(WINDOW, tp)
    kvc_p = kvc.reshape((1, bp, tp) + kv_shape)
    kvs_p = kvs.reshape((1, bp, tp) + kv_shape)
    kvw_p = kvw.reshape((1, bp, tp) + kv_shape)[:, :, tp - n_keep:]

    s_hs, s_z, s_xbc, s_qc, s_qr, s_kvc, s_kvs, s_kvw, s_dtg = _inproj(
        x_sample.reshape(bs, D_MODEL), ln0_g, ln0_b, w_perm, _rope_tables(jnp.full((bs,), past_len)), bs)
    s_y_ssd, ssm_s, conv_s_t = _ssd_sample(s_xbc, s_z, s_dtg, jnp.swapaxes(state_conv[l], 0, 1), state_ssm[l],
                                           *ssd_consts)
    n_pool = cache_kv_cmp.shape[1]
    feature_major = lambda c, rows: jnp.swapaxes(c.reshape(-1, rows, 2 * KV_D), 1, 2)
    s_kvcmp = _compress_pages(feature_major(cache_kv_cmp[l], PAGE_SIZE), page_table,
                              _compress_consts_t(cmp_pe[l], cmp_w1[l], cmp_b1[l], cmp_w2[l], cmp_b2[l]))
    s_o_cmp, s_sel = _nsa_sample_cmp_t(s_qc, s_kvcmp)
    buf_win = cache_kv_win[l].reshape(bs, -1, 2 * KV_D)
    s_y_nsa = _nsa_sample_attn_t(
        s_qr, s_kvs, s_kvw, feature_major(cache_kv_win[l], buf_win.shape[1]), s_dtg, s_o_cmp,
        feature_major(cache_kv_sel[l], PAGE_SIZE), page_table, s_sel)
    h1s = _outproj(s_y_ssd, s_y_nsa, s_hs, w_o_ssd, w_o_nsa, *ln1, bs)
    win_all = jnp.concatenate([buf_win, s_kvw[:, None, :]], 1)
    n_keep_s = min(WINDOW, past_len + ts)
    kvw_s = win_all[:, win_all.shape[1] - n_keep_s:].reshape((1, bs, n_keep_s) + kv_shape)
    kvc_s = s_kvc.reshape((1, bs, ts) + kv_shape)
    kvs_s = s_kvs.reshape((1, bs, ts) + kv_shape)

    n_tok = n_prompt + bs * ts
    n_pad = -(-n_tok // MOE_TOKENS) * MOE_TOKENS
    tok = jnp.concatenate([h1p, h1s, jnp.zeros((n_pad - n_tok, D_MODEL), F32)], 0)
    out = _moe_ln(tok, n_tok, router_w[l], router_bias[l], exp_w_gate[l], exp_w_up[l], exp_w_down[l],
                  sh_w_gate[l], sh_w_up[l], sh_w_down[l], ln2_g[l][None], ln2_b[l][None])
    y_prompt = out[:n_prompt].reshape(bp, tp, D_MODEL)
    y_sample = out[n_prompt:n_tok].reshape(bs, ts, D_MODEL)
    return (y_prompt, y_sample, kvc_p, kvs_p, kvw_p, ssm_p[None], conv_p[None],
            kvc_s, kvs_s, kvw_s, ssm_s[None], jnp.swapaxes(conv_s_t, 0, 1)[None])
```

```python
import functools
import math

import jax
import jax.numpy as jnp
import numpy as np
from jax import lax
from jax.experimental import pallas as pl
from jax.experimental.pallas import tpu as pltpu

D_MODEL = 1024
HEAD_DIM = 64
SSD_HEADS = 8
SSD_D = SSD_HEADS * HEAD_DIM
SSD_GROUPS = 2
SSD_STATE = 128
SSD_CONV = 4
SSD_CONV_CH = SSD_D + 2 * SSD_GROUPS * SSD_STATE
SSD_CHUNK = 128
NSA_HEADS = 8
NSA_KV_HEADS = 2
NSA_D = NSA_HEADS * HEAD_DIM
KV_D = NSA_KV_HEADS * HEAD_DIM
CMP_BLOCK = 64
CMP_HIDDEN = 128
SEL_BLOCK = 64
TOP_N = 16
WINDOW = 512
Q_BLOCK = 128
ROT_DIM = HEAD_DIM // 4
ROPE_THETA = 500000.0
N_EXPERTS = 64
TOP_K = 6
N_EXPERT_GROUPS = 8
EXPERTS_PER_GROUP = N_EXPERTS // N_EXPERT_GROUPS
TOPK_GROUPS = 4
D_EXPERT = 256
D_SHARED = 256
ROUTED_SCALE = 2.5
MOE_BLOCK = 256
DEPTH = 1
DEEPNORM_ALPHA = (2.0 * DEPTH) ** 0.25
LN_EPS = 1e-5
RMS_EPS = 1e-5
NEG = -1e30
FORCED_SCORE = 1e4
PAGE_SIZE = 128

LANES = 128
SUBLANES = 8
VMEM_LIMIT_BYTES = 56 * 1024 * 1024

U_Z = 0
U_XBC = U_Z + SSD_D
U_Q = U_XBC + SSD_CONV_CH
U_KVC = U_Q + NSA_D
U_KVS = U_KVC + 2 * KV_D
U_KVW = U_KVS + 2 * KV_D
U_DTG = U_KVW + 2 * KV_D
U_TOTAL = U_DTG + LANES
GATE_COL0 = SSD_HEADS

BF16 = jnp.bfloat16
F32 = jnp.float32


def _cparams(sem):
    return pltpu.CompilerParams(dimension_semantics=sem, vmem_limit_bytes=VMEM_LIMIT_BYTES)


def _bdot(a, b):
    return jnp.dot(a.astype(BF16), b.astype(BF16), preferred_element_type=F32)


def _bdot_nt(a, b):
    return lax.dot_general(a.astype(BF16), b.astype(BF16), (((1,), (1,)), ((), ())),
                           preferred_element_type=F32)


def _hdot(a, b):
    return jnp.dot(a, b, preferred_element_type=F32, precision=lax.Precision.HIGHEST)


def _sigmoid(x):
    return 1.0 / (1.0 + jnp.exp(-x))


def _silu(x):
    return x * _sigmoid(x)


def _layer_norm(x, g, b):
    mu = jnp.mean(x, axis=-1, keepdims=True)
    xc = x - mu
    var = jnp.mean(xc * xc, axis=-1, keepdims=True)
    return xc * lax.rsqrt(var + LN_EPS) * g + b


def _rope_tile(x, cos, sa, sb):
    return x * cos + pltpu.roll(x, LANES - ROT_DIM // 2, 1) * sa + pltpu.roll(x, ROT_DIM // 2, 1) * sb


def _inproj_kernel(x_ref, g_ref, b_ref, w_ref, rope_ref,
                   h_ref, z_ref, xbc_ref, qc_ref, qr_ref, kvc_ref, kvs_ref, kvw_ref, dtg_ref):
    h = _layer_norm(x_ref[...], g_ref[...], b_ref[...])
    h_ref[...] = h
    u = jnp.dot(h.astype(BF16), w_ref[...], preferred_element_type=F32)
    cos = rope_ref[:, 0:LANES]
    sa = rope_ref[:, LANES:2 * LANES]
    sb = rope_ref[:, 2 * LANES:3 * LANES]
    z_ref[...] = u[:, U_Z:U_XBC]
    xbc_ref[...] = u[:, U_XBC:U_Q]
    qc_ref[...] = u[:, U_Q:U_KVC]
    for c in range(NSA_D // LANES):
        qr_ref[:, c * LANES:(c + 1) * LANES] = _rope_tile(u[:, U_Q + c * LANES:U_Q + (c + 1) * LANES], cos, sa, sb)
    kvc_ref[...] = u[:, U_KVC:U_KVS]
    kvs_ref[:, 0:KV_D] = _rope_tile(u[:, U_KVS:U_KVS + KV_D], cos, sa, sb)
    kvs_ref[:, KV_D:2 * KV_D] = u[:, U_KVS + KV_D:U_KVW]
    kvw_ref[:, 0:KV_D] = _rope_tile(u[:, U_KVW:U_KVW + KV_D], cos, sa, sb)
    kvw_ref[:, KV_D:2 * KV_D] = u[:, U_KVW + KV_D:U_DTG]
    dtg_ref[...] = u[:, U_DTG:U_TOTAL]


def _rope_tables(pos):
    half = ROT_DIM // 2
    inv = ROPE_THETA ** (-jnp.arange(half, dtype=F32) / half)
    ang = pos.astype(F32)[:, None] * inv
    cos, sin = jnp.cos(ang), jnp.sin(ang)
    ones = jnp.ones((pos.shape[0], HEAD_DIM - ROT_DIM), F32)
    zeros = jnp.zeros((pos.shape[0], HEAD_DIM - ROT_DIM), F32)
    zh = jnp.zeros_like(sin)
    c = jnp.concatenate([cos, cos, ones], 1)
    sa = jnp.concatenate([-sin, zh, zeros], 1)
    sb = jnp.concatenate([zh, sin, zeros], 1)
    return jnp.concatenate([jnp.tile(t, (1, LANES // HEAD_DIM)) for t in (c, sa, sb)], 1)


def _permute_w_in(w):
    sizes = (SSD_D, SSD_CONV_CH, SSD_HEADS, NSA_D, KV_D, KV_D, KV_D, KV_D, KV_D, KV_D, 3 * NSA_HEADS)
    offs = np.concatenate([[0], np.cumsum(sizes)])
    seg = [w[:, offs[i]:offs[i + 1]] for i in range(len(sizes))]
    pad = jnp.zeros((w.shape[0], LANES - SSD_HEADS - 3 * NSA_HEADS), w.dtype)
    out = jnp.concatenate([seg[0], seg[1], seg[3], seg[4], seg[5], seg[6], seg[7], seg[8], seg[9],
                           seg[2], seg[10], pad], 1)
    return out.astype(BF16)


def _inproj(x, ln_g, ln_b, w_perm, rope_tab, tm):
    n = x.shape[0]
    nt = n // tm
    n_rope_blocks = rope_tab.shape[0] // tm
    row = lambda w: pl.BlockSpec((tm, w), lambda i: (i, 0))
    const = lambda a: pl.BlockSpec(a.shape, lambda i: (0,) * a.ndim)
    widths = (D_MODEL, SSD_D, SSD_CONV_CH, NSA_D, NSA_D, 2 * KV_D, 2 * KV_D, 2 * KV_D, LANES)
    return pl.pallas_call(
        _inproj_kernel,
        grid=(nt,),
        in_specs=[row(D_MODEL), const(ln_g), const(ln_b), const(w_perm),
                  pl.BlockSpec((tm, 3 * LANES), lambda i: (i % n_rope_blocks, 0))],
        out_specs=[row(w) for w in widths],
        out_shape=[jax.ShapeDtypeStruct((n, w), F32) for w in widths],
        compiler_params=_cparams(("parallel",)),
        name="inproj",
    )(x, ln_g, ln_b, w_perm, rope_tab)


def _softplus(x):
    return jnp.maximum(x, 0.0) + jnp.log1p(jnp.exp(-jnp.abs(x)))


def _gated_group_norm(y, z, norm_w):
    y = y * _silu(z)
    gw = SSD_D // SSD_GROUPS
    parts = []
    for g in range(SSD_GROUPS):
        yg = y[:, g * gw:(g + 1) * gw]
        ms = jnp.mean(yg * yg, axis=-1, keepdims=True)
        parts.append(yg * lax.rsqrt(ms + RMS_EPS))
    return jnp.concatenate(parts, axis=1) * norm_w


def _ssd_prompt_kernel(xbc_ref, z_ref, dtg_ref, convw_ref, convb_ref, dtb_ref, alog_ref, dskip_ref, normw_ref,
                       y_ref, state_ref, conv_ref, ext_ref, s_ref):
    c = pl.program_id(1)
    nc = pl.num_programs(1)
    L = SSD_CHUNK
    halo = SUBLANES

    @pl.when(c == 0)
    def _():
        ext_ref[0:halo, :] = jnp.zeros((halo, SSD_CONV_CH), F32)
        s_ref[...] = jnp.zeros_like(s_ref)

    xin = xbc_ref[...]
    ext_ref[halo:halo + L, :] = xin
    xc = convw_ref[SSD_CONV - 1:SSD_CONV, :] * xin
    for k in range(SSD_CONV - 1):
        off = halo - (SSD_CONV - 1) + k
        xc = xc + convw_ref[k:k + 1, :] * ext_ref[off:off + L, :]
    ext_ref[0:halo, :] = ext_ref[L:L + halo, :]
    xc = _silu(xc + convb_ref[...])
    xs = xc[:, 0:SSD_D]
    ns = SSD_GROUPS * SSD_STATE
    bm = xc[:, SSD_D:SSD_D + ns]
    cm = xc[:, SSD_D + ns:SSD_D + 2 * ns]

    dt = _softplus(dtg_ref[...] + dtb_ref[...])
    da = dt * (-jnp.exp(alog_ref[...]))
    row = lax.broadcasted_iota(jnp.int32, (L, L), 0)
    col = lax.broadcasted_iota(jnp.int32, (L, L), 1)
    tril = row >= col
    acum = _hdot(tril.astype(F32), da)
    acum_t = acum.T
    eacum = jnp.exp(acum)
    alast = acum[L - 1:L, :]
    edecay = jnp.exp(alast - acum)
    elast = jnp.exp(alast)

    dt_full = jnp.concatenate([jnp.broadcast_to(dt[:, h:h + 1], (L, HEAD_DIM)) for h in range(SSD_HEADS)], 1)
    dec_full = jnp.concatenate([jnp.broadcast_to(edecay[:, h:h + 1], (L, HEAD_DIM)) for h in range(SSD_HEADS)], 1)
    xdt = xs * dt_full
    xdec_t = (xdt * dec_full).T

    hpg = SSD_HEADS // SSD_GROUPS
    y_parts = []
    for h in range(SSD_HEADS):
        g = h // hpg
        b_g = bm[:, g * SSD_STATE:(g + 1) * SSD_STATE]
        c_g = cm[:, g * SSD_STATE:(g + 1) * SSD_STATE]
        if h % hpg == 0:
            cb = _bdot_nt(c_g, b_g)
        seg = acum[:, h:h + 1] - acum_t[h:h + 1, :]
        lmat = jnp.where(tril, jnp.exp(jnp.where(tril, seg, 0.0)), 0.0)
        xdt_h = xdt[:, h * HEAD_DIM:(h + 1) * HEAD_DIM]
        y_h = _bdot(cb * lmat, xdt_h)
        s_prev = s_ref[h]
        y_h = y_h + _bdot_nt(c_g, s_prev) * eacum[:, h:h + 1]
        y_h = y_h + dskip_ref[:, h * HEAD_DIM:(h + 1) * HEAD_DIM] * xs[:, h * HEAD_DIM:(h + 1) * HEAD_DIM]
        y_parts.append(y_h)
        s_ref[h] = elast[:, h:h + 1] * s_prev + _bdot(xdec_t[h * HEAD_DIM:(h + 1) * HEAD_DIM, :], b_g)
    y = jnp.concatenate(y_parts, axis=1)
    y_ref[...] = _gated_group_norm(y, z_ref[...], normw_ref[...])

    @pl.when(c == nc - 1)
    def _():
        state_ref[0] = s_ref[...]
        conv_ref[0] = xin[L - (SSD_CONV - 1):L, :]


def _ssd_prompt(xbc, z, dtg, conv_w, conv_b, dt_bias_pad, a_log_pad, d_skip_full, norm_w, bn, t):
    nc = t // SSD_CHUNK
    row = lambda w: pl.BlockSpec((SSD_CHUNK, w), lambda b, c: (b * nc + c, 0))
    const = lambda a: pl.BlockSpec(a.shape, lambda b, c: (0,) * a.ndim)
    return pl.pallas_call(
        _ssd_prompt_kernel,
        grid=(bn, nc),
        in_specs=[row(SSD_CONV_CH), row(SSD_D), row(LANES), const(conv_w), const(conv_b), const(dt_bias_pad),
                  const(a_log_pad), const(d_skip_full), const(norm_w)],
        out_specs=[row(SSD_D),
                   pl.BlockSpec((1, SSD_HEADS, HEAD_DIM, SSD_STATE), lambda b, c: (b, 0, 0, 0)),
                   pl.BlockSpec((1, SSD_CONV - 1, SSD_CONV_CH), lambda b, c: (b, 0, 0))],
        out_shape=[jax.ShapeDtypeStruct((bn * t, SSD_D), F32),
                   jax.ShapeDtypeStruct((bn, SSD_HEADS, HEAD_DIM, SSD_STATE), F32),
                   jax.ShapeDtypeStruct((bn, SSD_CONV - 1, SSD_CONV_CH), F32)],
        scratch_shapes=[pltpu.VMEM((SSD_CHUNK + 2 * SUBLANES, SSD_CONV_CH), F32),
                        pltpu.VMEM((SSD_HEADS, HEAD_DIM, SSD_STATE), F32)],
        compiler_params=_cparams(("parallel", "arbitrary")),
        name="ssd_prompt",
    )(xbc, z, dtg, conv_w, conv_b, dt_bias_pad, a_log_pad, d_skip_full, norm_w)


def _pad_lanes(v, fill=0.0):
    return jnp.concatenate([v.astype(F32), jnp.full((LANES - v.shape[0],), fill, F32)])[None]


def _compress_rows(k_ref, v_ref, pe_ref, w1k_ref, w1v_ref, b1_ref, w2k_ref, w2v_ref, b2_ref, nb):
    acck = jnp.zeros((nb, 2 * CMP_HIDDEN), F32)
    accv = jnp.zeros((nb, 2 * CMP_HIDDEN), F32)
    for l in range(CMP_BLOCK):
        xk = k_ref[pl.ds(l, nb, stride=CMP_BLOCK), :] + pe_ref[l:l + 1, 0:KV_D]
        xv = v_ref[pl.ds(l, nb, stride=CMP_BLOCK), :] + pe_ref[l:l + 1, KV_D:2 * KV_D]
        acck = acck + jnp.dot(xk.astype(BF16), w1k_ref[l], preferred_element_type=F32)
        accv = accv + jnp.dot(xv.astype(BF16), w1v_ref[l], preferred_element_type=F32)
    hk = _silu(acck + b1_ref[:, 0:2 * CMP_HIDDEN])
    hv = _silu(accv + b1_ref[:, 2 * CMP_HIDDEN:4 * CMP_HIDDEN])
    ok = jnp.dot(hk.astype(BF16), w2k_ref[...], preferred_element_type=F32) + b2_ref[:, 0:KV_D]
    ov = jnp.dot(hv.astype(BF16), w2v_ref[...], preferred_element_type=F32) + b2_ref[:, KV_D:2 * KV_D]
    return jnp.concatenate([ok, ov], axis=1)


def _compress_kernel(k_ref, v_ref, pe_ref, w1k_ref, w1v_ref, b1_ref, w2k_ref, w2v_ref, b2_ref, o_ref, *, nb):
    o_ref[...] = _compress_rows(k_ref, v_ref, pe_ref, w1k_ref, w1v_ref, b1_ref, w2k_ref, w2v_ref, b2_ref, nb)


def _block_diag2(w):
    z = jnp.zeros_like(w)
    return jnp.concatenate([jnp.concatenate([w, z], -1), jnp.concatenate([z, w], -1)], -2)


def _compress_consts(cmp_pe, cmp_w1, cmp_b1, cmp_w2, cmp_b2):
    pe = jnp.concatenate([cmp_pe[0], cmp_pe[0], cmp_pe[1], cmp_pe[1]], -1)
    w1k = _block_diag2(cmp_w1[0]).astype(BF16)
    w1v = _block_diag2(cmp_w1[1]).astype(BF16)
    b1 = jnp.concatenate([cmp_b1[0], cmp_b1[0], cmp_b1[1], cmp_b1[1]])[None]
    w2k = _block_diag2(cmp_w2[0]).astype(BF16)
    w2v = _block_diag2(cmp_w2[1]).astype(BF16)
    b2 = jnp.concatenate([cmp_b2[0], cmp_b2[0], cmp_b2[1], cmp_b2[1]])[None]
    return pe, w1k, w1v, b1, w2k, w2v, b2


def _compress_prompt(kvc, consts, rows_per_step):
    n = kvc.shape[0]
    nb = rows_per_step // CMP_BLOCK
    const = lambda a: pl.BlockSpec(a.shape, lambda i: (0,) * a.ndim)
    return pl.pallas_call(
        functools.partial(_compress_kernel, nb=nb),
        grid=(n // rows_per_step,),
        in_specs=[pl.BlockSpec((rows_per_step, KV_D), lambda i: (i, 0)),
                  pl.BlockSpec((rows_per_step, KV_D), lambda i: (i, 1))] + [const(a) for a in consts],
        out_specs=pl.BlockSpec((nb, 2 * KV_D), lambda i: (i, 0)),
        out_shape=jax.ShapeDtypeStruct((n // CMP_BLOCK, 2 * KV_D), F32),
        compiler_params=_cparams(("parallel",)),
        name="compress_prompt",
    )(kvc, kvc, *consts)


SEL_KEY_TILE = 512
WIN_KEYS = WINDOW + Q_BLOCK


def _dup_head(x, hk):
    sw = pltpu.roll(x, HEAD_DIM, 1)
    low = lax.broadcasted_iota(jnp.int32, x.shape, 1) < HEAD_DIM
    return jnp.where(low, x, sw) if hk == 0 else jnp.where(low, sw, x)


def _masked_softmax(s, mask):
    sm = jnp.where(mask, s, NEG)
    ex = jnp.where(mask, jnp.exp(sm - jnp.max(sm, axis=-1, keepdims=True)), 0.0)
    den = jnp.sum(ex, axis=-1, keepdims=True)
    return ex / jnp.where(den > 0.0, den, 1.0)


def _select_blocks_t(imp, cur, n_top):
    j = lax.broadcasted_iota(jnp.int32, imp.shape, 0)
    future = j > cur
    forced = (j == 0) | (j == cur) | (j == cur - 1)
    score = jnp.where(future, NEG, jnp.where(forced, FORCED_SCORE, imp))
    return ((_rank_rows(score) < n_top) & (score > 0.5 * NEG)).astype(F32)


def _nsa_prompt_kernel(qc_ref, qr_ref, dtg_ref, cmp_ref, kvs_ref, kvw_ref, o_ref,
                       cmp_d, kvs_d, kvw_d, bias_ref, qrs_ref, m_ref, l_ref, acc_ref, *, t):
    qb = pl.program_id(1)
    nbk = t // SEL_BLOCK
    tq = Q_BLOCK
    tk = SEL_KEY_TILE
    hpg = NSA_HEADS // NSA_KV_HEADS
    scale = HEAD_DIM ** -0.5

    @pl.when(qb == 0)
    def _():
        cmp_d[...] = jnp.zeros_like(cmp_d)
        for src, dst, n in ((cmp_ref, cmp_d, nbk), (kvs_ref, kvs_d, t), (kvw_ref, kvw_d, t)):
            x = src[...]
            for hk in range(NSA_KV_HEADS):
                dst[hk, 0:n, 0:KV_D] = _dup_head(x[:, 0:KV_D], hk).astype(BF16)
                dst[hk, 0:n, KV_D:2 * KV_D] = _dup_head(x[:, KV_D:2 * KV_D], hk).astype(BF16)

    t0 = qb * tq
    rows = t0 + lax.broadcasted_iota(jnp.int32, (tq, 1), 0)
    lane = lax.broadcasted_iota(jnp.int32, (tq, LANES), 1)
    half_mask = (lane < HEAD_DIM, lane >= HEAD_DIM)
    sig = _sigmoid(dtg_ref[...])
    vis = (lane + 1) * CMP_BLOCK - 1 <= rows
    cur_l = (t0 + lax.broadcasted_iota(jnp.int32, (1, tq), 1)) // SEL_BLOCK
    expand = (lax.broadcasted_iota(jnp.int32, (LANES, t), 1) // SEL_BLOCK
              == lax.broadcasted_iota(jnp.int32, (LANES, t), 0)).astype(BF16)
    win_start = pl.multiple_of(jnp.maximum(t0 - WINDOW, 0), tq)
    wpos = win_start + lax.broadcasted_iota(jnp.int32, (tq, WIN_KEYS), 1)
    win_bias = jnp.where((wpos <= rows) & (wpos >= rows - WINDOW), 0.0, NEG)
    n_kt = (t0 + tq + tk - 1) // tk

    def stack_heads(ref, hk):
        parts = []
        for hh in range(hpg):
            head = hk * hpg + hh
            p, e = head // 2, head % 2
            parts.append(jnp.where(half_mask[e], ref[:, p * LANES:(p + 1) * LANES] * scale, 0.0))
        return jnp.concatenate(parts, axis=0).astype(BF16)

    o_cmp_g = []
    for hk in range(NSA_KV_HEADS):
        qcs = stack_heads(qc_ref, hk)
        s = lax.dot_general(qcs, cmp_d[hk, :, 0:KV_D], (((1,), (1,)), ((), ())), preferred_element_type=F32)
        pc = _masked_softmax(s.reshape(hpg, tq, LANES), vis[None])
        imp = jnp.sum(pc, axis=0)
        o_cmp_g.append(jnp.dot(pc.reshape(hpg * tq, LANES).astype(BF16), cmp_d[hk, :, KV_D:2 * KV_D],
                               preferred_element_type=F32).reshape(hpg, tq, LANES))

        sel_t = _select_blocks_t(imp.T[0:nbk, :], cur_l, TOP_N)
        sel = jnp.concatenate([sel_t, jnp.zeros((LANES - nbk, tq), F32)], axis=0).T
        selk = jnp.dot(sel.astype(BF16), expand, preferred_element_type=F32)
        for kt in range(t // tk):
            @pl.when(kt < n_kt)
            def _(kt=kt, hk=hk, selk=selk):
                kpos = kt * tk + lax.broadcasted_iota(jnp.int32, (tq, tk), 1)
                bias_ref[hk, kt] = jnp.where((selk[:, kt * tk:(kt + 1) * tk] > 0.5) & (kpos <= rows), 0.0, NEG)

        qrs_ref[hk] = stack_heads(qr_ref, hk)

    m_ref[...] = jnp.full(m_ref.shape, NEG, F32)
    l_ref[...] = jnp.zeros(l_ref.shape, F32)
    acc_ref[...] = jnp.zeros(acc_ref.shape, F32)

    def sel_step(kt, carry):
        k0 = pl.multiple_of(kt * tk, tk)
        for hk in range(NSA_KV_HEADS):
            kblk = kvs_d[hk, pl.ds(k0, tk), 0:KV_D]
            vblk = kvs_d[hk, pl.ds(k0, tk), KV_D:2 * KV_D]
            s = lax.dot_general(qrs_ref[hk], kblk, (((1,), (1,)), ((), ())), preferred_element_type=F32)
            s = s.reshape(hpg, tq, tk) + bias_ref[hk, kt][None]
            m_old = m_ref[hk]
            m_new = jnp.maximum(m_old, jnp.max(s, axis=-1, keepdims=True))
            alpha = jnp.exp(m_old - m_new)
            pe = jnp.exp(s - jnp.concatenate([m_new] * (tk // LANES), axis=-1))
            l_ref[hk] = alpha * l_ref[hk] + jnp.sum(pe, axis=-1, keepdims=True)
            pv = jnp.dot(pe.reshape(hpg * tq, tk).astype(BF16), vblk, preferred_element_type=F32)
            acc_ref[hk] = alpha * acc_ref[hk] + pv.reshape(hpg, tq, LANES)
            m_ref[hk] = m_new
        return carry

    lax.fori_loop(0, n_kt, sel_step, 0)

    for hk in range(NSA_KV_HEADS):
        o_cmp = o_cmp_g[hk]
        o_slc = acc_ref[hk] / l_ref[hk]
        kw = kvw_d[hk, pl.ds(win_start, WIN_KEYS), 0:KV_D]
        vw = kvw_d[hk, pl.ds(win_start, WIN_KEYS), KV_D:2 * KV_D]
        sw = lax.dot_general(qrs_ref[hk], kw, (((1,), (1,)), ((), ())), preferred_element_type=F32)
        sw = sw.reshape(hpg, tq, WIN_KEYS) + win_bias[None]
        pw = jnp.exp(sw - jnp.max(sw, axis=-1, keepdims=True))
        den = jnp.sum(pw, axis=-1, keepdims=True)
        o_win = jnp.dot(pw.reshape(hpg * tq, WIN_KEYS).astype(BF16), vw,
                        preferred_element_type=F32).reshape(hpg, tq, LANES) / den

        for hh in range(hpg):
            head = hk * hpg + hh
            p, e = head // 2, head % 2
            c0 = GATE_COL0 + head * 3
            mix = (sig[:, c0:c0 + 1] * o_cmp[hh] + sig[:, c0 + 1:c0 + 2] * o_slc[hh]
                   + sig[:, c0 + 2:c0 + 3] * o_win[hh])
            if e == 0:
                mix_even = mix
            else:
                o_ref[:, p * LANES:(p + 1) * LANES] = jnp.where(half_mask[0], mix_even, mix)


def _nsa_prompt(qc, qr, dtg, kvcmp, kvs, kvw, bn, t):
    nq = t // Q_BLOCK
    nbk = t // SEL_BLOCK
    hpg = NSA_HEADS // NSA_KV_HEADS
    assert nbk >= TOP_N and t >= WIN_KEYS and t % SEL_KEY_TILE == 0
    qrow = lambda w: pl.BlockSpec((Q_BLOCK, w), lambda b, i: (b * nq + i, 0))
    seq = lambda r: pl.BlockSpec((r, 2 * KV_D), lambda b, i: (b, 0))
    return pl.pallas_call(
        functools.partial(_nsa_prompt_kernel, t=t),
        grid=(bn, nq),
        in_specs=[qrow(NSA_D), qrow(NSA_D), qrow(LANES), seq(nbk), seq(t), seq(t)],
        out_specs=qrow(NSA_D),
        out_shape=jax.ShapeDtypeStruct((bn * t, NSA_D), F32),
        scratch_shapes=[pltpu.VMEM((NSA_KV_HEADS, LANES, 2 * KV_D), BF16),
                        pltpu.VMEM((NSA_KV_HEADS, t, 2 * KV_D), BF16),
                        pltpu.VMEM((NSA_KV_HEADS, t, 2 * KV_D), BF16),
                        pltpu.VMEM((NSA_KV_HEADS, t // SEL_KEY_TILE, Q_BLOCK, SEL_KEY_TILE), F32),
                        pltpu.VMEM((NSA_KV_HEADS, hpg * Q_BLOCK, LANES), BF16),
                        pltpu.VMEM((NSA_KV_HEADS, hpg, Q_BLOCK, LANES), F32),
                        pltpu.VMEM((NSA_KV_HEADS, hpg, Q_BLOCK, LANES), F32),
                        pltpu.VMEM((NSA_KV_HEADS, hpg, Q_BLOCK, LANES), F32)],
        compiler_params=_cparams(("parallel", "arbitrary")),
        name="nsa_prompt",
    )(qc, qr, dtg, kvcmp, kvs, kvw)


def _outproj_kernel(ys_ref, yn_ref, h_ref, ws_ref, wn_ref, g_ref, b_ref, o_ref):
    mix = jnp.dot(ys_ref[...].astype(BF16), ws_ref[...], preferred_element_type=F32)
    mix = mix + jnp.dot(yn_ref[...].astype(BF16), wn_ref[...], preferred_element_type=F32)
    o_ref[...] = _layer_norm(DEEPNORM_ALPHA * h_ref[...] + mix, g_ref[...], b_ref[...])


def _outproj(y_ssd, y_nsa, h, w_ssd, w_nsa, ln_g, ln_b, tm):
    n = h.shape[0]
    row = lambda w: pl.BlockSpec((tm, w), lambda i: (i, 0))
    const = lambda a: pl.BlockSpec(a.shape, lambda i: (0,) * a.ndim)
    return pl.pallas_call(
        _outproj_kernel,
        grid=(n // tm,),
        in_specs=[row(SSD_D), row(NSA_D), row(D_MODEL), const(w_ssd), const(w_nsa), const(ln_g), const(ln_b)],
        out_specs=row(D_MODEL),
        out_shape=jax.ShapeDtypeStruct((n, D_MODEL), F32),
        compiler_params=_cparams(("parallel",)),
        name="outproj",
    )(y_ssd, y_nsa, h, w_ssd, w_nsa, ln_g, ln_b)


MOE_TOKENS = 256
ROUTE_ROWS = 8


def _rank_rows(x):
    n = x.shape[0]
    idx = lax.broadcasted_iota(jnp.int32, x.shape, 0)
    rank = jnp.zeros(x.shape, F32)
    for r in range(n):
        row = x[r:r + 1, :]
        rank = rank + ((row > x) | ((row == x) & (idx > r))).astype(F32)
    return rank


def _route_kernel(h_ref, rw_ref, rb_ref, slot_ref, tokinfo_ref, meta_ref, cnt_ref, carry_ref, carry_row_ref, *,
                  n_valid):
    i = pl.program_id(0)
    tm = MOE_TOKENS

    @pl.when(i == 0)
    def _():
        carry_ref[...] = jnp.zeros_like(carry_ref)
        carry_row_ref[...] = jnp.zeros_like(carry_row_ref)

    logits = lax.dot_general(rw_ref[...], h_ref[...].astype(BF16), (((1,), (1,)), ((), ())),
                             preferred_element_type=F32)
    scores = _sigmoid(logits)
    biased = scores + rb_ref[:, 0:1]
    b3 = biased.reshape(N_EXPERT_GROUPS, EXPERTS_PER_GROUP, tm)
    sidx = lax.broadcasted_iota(jnp.int32, b3.shape, 1)
    m1 = jnp.max(b3, axis=1, keepdims=True)
    first = jnp.min(jnp.where(b3 == m1, sidx, EXPERTS_PER_GROUP), axis=1, keepdims=True)
    m2 = jnp.max(jnp.where(sidx == first, -jnp.inf, b3), axis=1, keepdims=True)
    grp_score = (m1 + m2).reshape(N_EXPERT_GROUPS, tm)
    grp_keep = _rank_rows(grp_score) < TOPK_GROUPS
    masked = jnp.where(grp_keep.reshape(N_EXPERT_GROUPS, 1, tm), b3, NEG).reshape(N_EXPERTS, tm)
    rank = _rank_rows(masked)
    tok = i * tm + lax.broadcasted_iota(jnp.int32, (1, tm), 1)
    valid = tok < n_valid
    sel = (rank < TOP_K) & valid
    self32 = sel.astype(F32)
    wsel = self32 * scores
    wsum = jnp.sum(wsel, axis=0, keepdims=True)
    w = wsel / jnp.where(wsum > 0.0, wsum, 1.0) * ROUTED_SCALE

    selb = sel.astype(BF16)
    tri = lambda n, strict_upper: (
        (lax.broadcasted_iota(jnp.int32, (n, n), 0) < lax.broadcasted_iota(jnp.int32, (n, n), 1))
        if strict_upper else
        (lax.broadcasted_iota(jnp.int32, (n, n), 0) > lax.broadcasted_iota(jnp.int32, (n, n), 1))).astype(BF16)
    pad8 = lambda c: jnp.floor((c + (SUBLANES - 1.0)) * (1.0 / SUBLANES)) * SUBLANES
    pos_tile = jnp.dot(selb, tri(tm, True), preferred_element_type=F32)
    cnt_col = pad8(jnp.sum(self32, axis=1, keepdims=True))
    first_col = jnp.dot(tri(N_EXPERTS, False), jnp.broadcast_to(cnt_col, (N_EXPERTS, LANES)).astype(BF16),
                        preferred_element_type=F32)[:, 0:1]
    slot = first_col + pos_tile

    sel_pad = jnp.concatenate([selb, jnp.zeros((LANES - N_EXPERTS, tm), BF16)], axis=0)
    cnt_row = pad8(lax.dot_general(jnp.ones((SUBLANES, tm), BF16), sel_pad, (((1,), (1,)), ((), ())),
                                   preferred_element_type=F32))
    first_row = jnp.dot(cnt_row.astype(BF16), tri(LANES, True), preferred_element_type=F32)
    prev_row = carry_row_ref[...]
    meta = jnp.concatenate([cnt_row[0:1], first_row[0:1], prev_row[0:1], jnp.zeros((SUBLANES - 3, LANES), F32)], 0)
    meta_ref[0] = meta.astype(jnp.int32)
    carry_row_ref[...] = prev_row + cnt_row
    carry_ref[...] = carry_ref[...] + cnt_col

    slot_rows, w_rows = [], []
    for k in range(TOP_K):
        hit = (rank == k) & sel
        slot_rows.append(jnp.sum(jnp.where(hit, slot, 0.0), axis=0, keepdims=True))
        w_rows.append(jnp.sum(jnp.where(hit, w, 0.0), axis=0, keepdims=True))
    slot_rows = [jnp.where(valid, r, -1.0) for r in slot_rows]
    pad2 = jnp.zeros((ROUTE_ROWS - TOP_K, tm), F32)
    slot_ref[...] = jnp.concatenate(slot_rows + [pad2 - 1.0], 0).astype(jnp.int32)
    info = jnp.concatenate(w_rows + [pad2] + slot_rows + [jnp.zeros((LANES - ROUTE_ROWS - TOP_K, tm), F32)], 0)
    tokinfo_ref[...] = info.T

    @pl.when(i == pl.num_programs(0) - 1)
    def _():
        cnt_ref[...] = jnp.broadcast_to(carry_ref[:, 0:1], cnt_ref.shape)


def _route(h, router_wt, router_bias_col, n_valid):
    n = h.shape[0]
    tm = MOE_TOKENS
    const = lambda a: pl.BlockSpec(a.shape, lambda i: (0,) * a.ndim)
    return pl.pallas_call(
        functools.partial(_route_kernel, n_valid=n_valid),
        grid=(n // tm,),
        in_specs=[pl.BlockSpec((tm, D_MODEL), lambda i: (i, 0)), const(router_wt), const(router_bias_col)],
        out_specs=[pl.BlockSpec((ROUTE_ROWS, tm), lambda i: (0, i)),
                   pl.BlockSpec((tm, LANES), lambda i: (i, 0)),
                   pl.BlockSpec((1, SUBLANES, LANES), lambda i: (i, 0, 0)),
                   pl.BlockSpec((N_EXPERTS, LANES), lambda i: (0, 0))],
        out_shape=[jax.ShapeDtypeStruct((ROUTE_ROWS, n), jnp.int32),
                   jax.ShapeDtypeStruct((n, LANES), F32),
                   jax.ShapeDtypeStruct((n // tm, SUBLANES, LANES), jnp.int32),
                   jax.ShapeDtypeStruct((N_EXPERTS, LANES), F32)],
        scratch_shapes=[pltpu.VMEM((N_EXPERTS, LANES), F32), pltpu.VMEM((SUBLANES, LANES), F32)],
        compiler_params=_cparams(("arbitrary",)),
        name="moe_route",
    )(h, router_wt, router_bias_col)


TILE_SLOTS = MOE_TOKENS * TOP_K + N_EXPERTS * SUBLANES
RUN_CHUNKS = tuple(1 << b for b in range(int(math.log2(MOE_TOKENS)), int(math.log2(SUBLANES)) - 1, -1))


def _run_copy(src_ref, src_row, dst_ref, dst_row, rows, sem):
    return pltpu.make_async_copy(src_ref.at[pl.ds(pl.multiple_of(src_row, SUBLANES), rows)],
                                 dst_ref.at[pl.ds(pl.multiple_of(dst_row, SUBLANES), rows)], sem)


def _start_run(src_ref, src_row, dst_ref, dst_row, n, sem, started):
    off = jnp.int32(0)
    out = []
    for c, rows in enumerate(RUN_CHUNKS):
        take = (n & rows) != 0

        @pl.when(take)
        def _(off=off, rows=rows):
            _run_copy(src_ref, src_row + off, dst_ref, dst_row + off, rows, sem).start()

        inc = take.astype(jnp.int32)
        off = off + inc * rows
        out.append(started[c] + inc)
    return tuple(out)


def _wait_runs(src_ref, dst_ref, sem, started):
    for c, rows in enumerate(RUN_CHUNKS):
        def wait_one(j, carry, rows=rows):
            _run_copy(src_ref, 0, dst_ref, 0, rows, sem).wait()
            return carry

        lax.fori_loop(0, started[c], wait_one, 0)


def _dispatch_kernel(start_ref, cnt_ref, meta_ref, slot_ref, x_ref, xs_ref, sorted_ref, zero_ref, sem, zsem, *, cap):
    i = pl.program_id(0)
    tm = MOE_TOKENS

    @pl.when(i == 0)
    def _():
        zero_ref[...] = jnp.zeros_like(zero_ref)

        def fill_expert(e, started):
            lo = start_ref[e] + cnt_ref[e]
            hi = jnp.where(e == N_EXPERTS - 1, cap, start_ref[jnp.minimum(e + 1, N_EXPERTS - 1)])
            n_full = (hi - lo) // tm

            def fill_full(j, st):
                return _start_run(zero_ref, 0, xs_ref, lo + j * tm, jnp.int32(tm), zsem, st)

            started = lax.fori_loop(0, n_full, fill_full, started)
            return _start_run(zero_ref, 0, xs_ref, lo + n_full * tm, (hi - lo) - n_full * tm, zsem, started)

        filled = lax.fori_loop(0, N_EXPERTS, fill_expert, tuple(jnp.int32(0) for _ in RUN_CHUNKS))
        _wait_runs(zero_ref, xs_ref, zsem, filled)

    srow = lax.broadcasted_iota(jnp.int32, (TILE_SLOTS, tm), 0)
    onehot = srow == slot_ref[0:1, :]
    for k in range(1, TOP_K):
        onehot = onehot | (srow == slot_ref[k:k + 1, :])
    sorted_ref[...] = jnp.dot(onehot.astype(BF16), x_ref[...].astype(BF16), preferred_element_type=F32)

    def copy_expert(e, started):
        n = meta_ref[0, 0, e]
        return _start_run(sorted_ref, meta_ref[0, 1, e], xs_ref, start_ref[e] + meta_ref[0, 2, e], n, sem, started)

    started = lax.fori_loop(0, N_EXPERTS, copy_expert, tuple(jnp.int32(0) for _ in RUN_CHUNKS))
    _wait_runs(sorted_ref, xs_ref, sem, started)


def _dispatch(h, slot_t, meta, seg_start, counts, cap):
    n = h.shape[0]
    tm = MOE_TOKENS
    return pl.pallas_call(
        functools.partial(_dispatch_kernel, cap=cap),
        grid_spec=pltpu.PrefetchScalarGridSpec(
            num_scalar_prefetch=2,
            grid=(n // tm,),
            in_specs=[pl.BlockSpec((1, SUBLANES, LANES), lambda i, *_: (i, 0, 0), memory_space=pltpu.SMEM),
                      pl.BlockSpec((ROUTE_ROWS, tm), lambda i, *_: (0, i)),
                      pl.BlockSpec((tm, D_MODEL), lambda i, *_: (i, 0))],
            out_specs=pl.BlockSpec(memory_space=pl.ANY),
            scratch_shapes=[pltpu.VMEM((TILE_SLOTS, D_MODEL), F32), pltpu.VMEM((tm, D_MODEL), F32),
                            pltpu.SemaphoreType.DMA, pltpu.SemaphoreType.DMA]),
        out_shape=jax.ShapeDtypeStruct((cap, D_MODEL), F32),
        compiler_params=_cparams(("arbitrary",)),
        name="moe_dispatch",
    )(seg_start, counts, meta, slot_t, h)


def _swiglu(x, wg, wu, wd):
    xb = x.astype(BF16)
    g = jnp.dot(xb, wg.astype(BF16), preferred_element_type=F32)
    u = jnp.dot(xb, wu.astype(BF16), preferred_element_type=F32)
    return jnp.dot((_silu(g) * u).astype(BF16), wd.astype(BF16), preferred_element_type=F32)


EXPERT_RING = 3


def _experts_kernel(be_ref, used_ref, xs_ref, wg_ref, wu_ref, wd_ref, y_ref, xbuf, sems):
    i = pl.program_id(0)
    n = pl.num_programs(0)

    def block_copy(blk):
        slot = lax.rem(blk, EXPERT_RING)
        rows = pl.ds(pl.multiple_of(blk * MOE_BLOCK, MOE_BLOCK), MOE_BLOCK)
        return pltpu.make_async_copy(xs_ref.at[rows], xbuf.at[slot], sems.at[slot])

    @pl.when(i == 0)
    def _():
        for j in range(EXPERT_RING - 1):
            block_copy(jnp.int32(j)).start()

    @pl.when(i + EXPERT_RING - 1 < n)
    def _():
        block_copy(i + EXPERT_RING - 1).start()

    block_copy(i).wait()

    @pl.when(i < used_ref[0])
    def _():
        y_ref[...] = _swiglu(xbuf[lax.rem(i, EXPERT_RING)], wg_ref[0], wu_ref[0], wd_ref[0])

    @pl.when(i >= used_ref[0])
    def _():
        y_ref[...] = jnp.zeros_like(y_ref)


def _experts(xs, block_expert, used_blocks, w_gate, w_up, w_down):
    cap = xs.shape[0]
    return pl.pallas_call(
        _experts_kernel,
        grid_spec=pltpu.PrefetchScalarGridSpec(
            num_scalar_prefetch=2,
            grid=(cap // MOE_BLOCK,),
            in_specs=[pl.BlockSpec(memory_space=pl.ANY),
                      pl.BlockSpec((1, D_MODEL, D_EXPERT), lambda i, be, used: (be[i], 0, 0)),
                      pl.BlockSpec((1, D_MODEL, D_EXPERT), lambda i, be, used: (be[i], 0, 0)),
                      pl.BlockSpec((1, D_EXPERT, D_MODEL), lambda i, be, used: (be[i], 0, 0))],
            out_specs=pl.BlockSpec((MOE_BLOCK, D_MODEL), lambda i, be, used: (i, 0)),
            scratch_shapes=[pltpu.VMEM((EXPERT_RING, MOE_BLOCK, D_MODEL), F32),
                            pltpu.SemaphoreType.DMA((EXPERT_RING,))]),
        out_shape=jax.ShapeDtypeStruct((cap, D_MODEL), F32),
        compiler_params=_cparams(("arbitrary",)),
        name="moe_experts",
    )(block_expert, used_blocks, xs, w_gate, w_up, w_down)


def _combine_kernel(start_ref, meta_ref, h_ref, info_ref, sg_ref, su_ref, sd_ref, g_ref, b_ref, ys_ref,
                    o_ref, buf_ref, sem):
    i = pl.program_id(0)
    tm = MOE_TOKENS

    @pl.when(i == 0)
    def _():
        buf_ref[...] = jnp.zeros_like(buf_ref)

    def fetch_expert(e, started):
        n = meta_ref[0, 0, e]
        return _start_run(ys_ref, start_ref[e] + meta_ref[0, 2, e], buf_ref, meta_ref[0, 1, e], n, sem, started)

    started = lax.fori_loop(0, N_EXPERTS, fetch_expert, tuple(jnp.int32(0) for _ in RUN_CHUNKS))
    h = h_ref[...]
    f = _swiglu(h, sg_ref[...], su_ref[...], sd_ref[...])
    info = info_ref[...]
    scol = lax.broadcasted_iota(jnp.int32, (tm, TILE_SLOTS), 1).astype(F32)
    mix = jnp.zeros((tm, TILE_SLOTS), F32)
    for k in range(TOP_K):
        mix = mix + jnp.where(info[:, ROUTE_ROWS + k:ROUTE_ROWS + k + 1] == scol, info[:, k:k + 1], 0.0)
    _wait_runs(ys_ref, buf_ref, sem, started)
    acc = jnp.dot(mix.astype(BF16), buf_ref[...].astype(BF16), preferred_element_type=F32)
    o_ref[...] = _layer_norm(DEEPNORM_ALPHA * h + (acc + f), g_ref[...], b_ref[...])


def _combine(h, ys, meta, tokinfo, seg_start, sh_gate, sh_up, sh_down, ln_g, ln_b):
    n = h.shape[0]
    tm = MOE_TOKENS
    const = lambda a: pl.BlockSpec(a.shape, lambda i, *_: (0,) * a.ndim)
    return pl.pallas_call(
        _combine_kernel,
        grid_spec=pltpu.PrefetchScalarGridSpec(
            num_scalar_prefetch=1,
            grid=(n // tm,),
            in_specs=[pl.BlockSpec((1, SUBLANES, LANES), lambda i, *_: (i, 0, 0), memory_space=pltpu.SMEM),
                      pl.BlockSpec((tm, D_MODEL), lambda i, *_: (i, 0)),
                      pl.BlockSpec((tm, LANES), lambda i, *_: (i, 0)),
                      const(sh_gate), const(sh_up), const(sh_down), const(ln_g), const(ln_b),
                      pl.BlockSpec(memory_space=pl.ANY)],
            out_specs=pl.BlockSpec((tm, D_MODEL), lambda i, *_: (i, 0)),
            scratch_shapes=[pltpu.VMEM((TILE_SLOTS, D_MODEL), F32), pltpu.SemaphoreType.DMA]),
        out_shape=jax.ShapeDtypeStruct((n, D_MODEL), F32),
        compiler_params=_cparams(("arbitrary",)),
        name="moe_combine",
    )(seg_start, meta, h, tokinfo, sh_gate, sh_up, sh_down, ln_g, ln_b, ys)


def _moe_ln(h, n_valid, router_w, router_bias, w_gate, w_up, w_down, sh_gate, sh_up, sh_down, ln_g, ln_b):
    slot_t, tokinfo, meta, cnt = _route(h, router_w.T.astype(BF16),
                                        jnp.broadcast_to(router_bias.astype(F32)[:, None], (N_EXPERTS, LANES)), n_valid)
    counts = cnt[:, 0].astype(jnp.int32)
    padded = (counts + MOE_BLOCK - 1) // MOE_BLOCK * MOE_BLOCK
    seg_end = jnp.cumsum(padded)
    seg_start = seg_end - padded
    run_pad = (h.shape[0] // MOE_TOKENS) * N_EXPERTS * (SUBLANES - 1)
    n_blocks = -(-(n_valid * TOP_K + run_pad + N_EXPERTS * (MOE_BLOCK - 1)) // MOE_BLOCK)
    cap = n_blocks * MOE_BLOCK
    block_first_row = jnp.arange(n_blocks, dtype=jnp.int32) * MOE_BLOCK
    block_expert = jnp.minimum(jnp.sum((seg_end[None, :] <= block_first_row[:, None]).astype(jnp.int32), axis=1),
                               N_EXPERTS - 1)
    xs = _dispatch(h, slot_t, meta, seg_start, counts, cap)
    used_blocks = (seg_end[N_EXPERTS - 1:] // MOE_BLOCK).astype(jnp.int32)
    ys = _experts(xs, block_expert, used_blocks, w_gate, w_up, w_down)
    return _combine(h, ys, meta, tokinfo, seg_start, sh_gate.astype(BF16), sh_up.astype(BF16),
                    sh_down.astype(BF16), ln_g, ln_b)


def _ssd_sample_kernel(xbc_ref, z_ref, dtg_ref, sconv_ref, s0_ref, convw_ref, convb_ref, dtb_ref, alog_ref,
                       dskip_ref, normw_ref, y_ref, s_ref, conv_out_ref, xc_ref, dt_ref, da_ref):
    b = pl.program_id(0)

    @pl.when(b == 0)
    def _():
        xin = xbc_ref[...]
        xc = convw_ref[SSD_CONV - 1:SSD_CONV, :] * xin
        for k in range(SSD_CONV - 1):
            xc = xc + convw_ref[k:k + 1, :] * sconv_ref[k]
        xc_ref[...] = _silu(xc + convb_ref[...])
        dt = _softplus(dtg_ref[...] + dtb_ref[...])
        dt_ref[...] = dt
        da_ref[...] = jnp.exp(dt * (-jnp.exp(alog_ref[...])))
        for k in range(SSD_CONV - 2):
            conv_out_ref[k] = sconv_ref[k + 1]
        conv_out_ref[SSD_CONV - 2] = xin

    xc = xc_ref[pl.ds(b, 1), :]
    dt = dt_ref[pl.ds(b, 1), :]
    da = da_ref[pl.ds(b, 1), :]
    ns = SSD_GROUPS * SSD_STATE
    eye = (lax.broadcasted_iota(jnp.int32, (HEAD_DIM, HEAD_DIM), 0)
           == lax.broadcasted_iota(jnp.int32, (HEAD_DIM, HEAD_DIM), 1))
    hpg = SSD_HEADS // SSD_GROUPS
    y_parts = []
    for h in range(SSD_HEADS):
        g = h // hpg
        x_h = xc[:, h * HEAD_DIM:(h + 1) * HEAD_DIM]
        b_g = xc[:, SSD_D + g * SSD_STATE:SSD_D + (g + 1) * SSD_STATE]
        c_g = xc[:, SSD_D + ns + g * SSD_STATE:SSD_D + ns + (g + 1) * SSD_STATE]
        xdt_col = jnp.sum(jnp.where(eye, x_h * dt[:, h:h + 1], 0.0), axis=1, keepdims=True)
        s_new = da[:, h:h + 1] * s0_ref[0, h] + xdt_col * b_g
        s_ref[0, h] = s_new
        y_h = _bdot_nt(c_g, s_new) + dskip_ref[:, h * HEAD_DIM:(h + 1) * HEAD_DIM] * x_h
        y_parts.append(y_h)
    y = jnp.concatenate(y_parts, axis=1)
    y_ref[pl.ds(b, 1), :] = _gated_group_norm(y, z_ref[pl.ds(b, 1), :], normw_ref[...])


def _ssd_sample(xbc, z, dtg, state_conv_t, state_ssm, conv_w, conv_b, dt_bias_pad, a_log_pad, d_skip_full, norm_w):
    bs = xbc.shape[0]
    const = lambda a: pl.BlockSpec(a.shape, lambda b: (0,) * a.ndim)
    state_spec = pl.BlockSpec((1, SSD_HEADS, HEAD_DIM, SSD_STATE), lambda b: (b, 0, 0, 0))
    return pl.pallas_call(
        _ssd_sample_kernel,
        grid=(bs,),
        in_specs=[const(xbc), const(z), const(dtg), const(state_conv_t), state_spec, const(conv_w), const(conv_b),
                  const(dt_bias_pad), const(a_log_pad), const(d_skip_full), const(norm_w)],
        out_specs=[pl.BlockSpec((bs, SSD_D), lambda b: (0, 0)), state_spec,
                   pl.BlockSpec((SSD_CONV - 1, bs, SSD_CONV_CH), lambda b: (0, 0, 0))],
        out_shape=[jax.ShapeDtypeStruct((bs, SSD_D), F32),
                   jax.ShapeDtypeStruct(state_ssm.shape, F32),
                   jax.ShapeDtypeStruct((SSD_CONV - 1, bs, SSD_CONV_CH), F32)],
        scratch_shapes=[pltpu.VMEM((bs, SSD_CONV_CH), F32), pltpu.VMEM((bs, LANES), F32),
                        pltpu.VMEM((bs, LANES), F32)],
        compiler_params=_cparams(("arbitrary",)),
        name="ssd_sample",
    )(xbc, z, dtg, state_conv_t, state_ssm, conv_w, conv_b, dt_bias_pad, a_log_pad, d_skip_full, norm_w)


SEL_PAST = TOP_N - 1
BLOCKS_PER_PAGE = PAGE_SIZE // CMP_BLOCK
KV_FEATS = 2 * KV_D


def _compress_consts_t(cmp_pe, cmp_w1, cmp_b1, cmp_w2, cmp_b2):
    pe_t = jnp.stack([jnp.tile(cmp_pe[k].T, (1, BLOCKS_PER_PAGE)) for k in range(2)])
    w1_t = jnp.stack([_block_diag2(jnp.swapaxes(cmp_w1[k], 0, 1)) for k in range(2)]).astype(BF16)
    b1_t = jnp.stack([jnp.tile(cmp_b1[k], BLOCKS_PER_PAGE) for k in range(2)])[:, None, :]
    w2_t = jnp.stack([_block_diag2(cmp_w2[k]) for k in range(2)]).astype(BF16)
    b2_t = jnp.stack([jnp.tile(cmp_b2[k], BLOCKS_PER_PAGE) for k in range(2)])[:, None, :]
    return pe_t, w1_t, b1_t, w2_t, b2_t


def _compress_pages_kernel(pt_ref, pe_ref, w1_ref, b1_ref, w2_ref, b2_ref, pool_ref, o_ref, kbuf, vbuf, sems, *,
                           n_pages):
    b = pl.program_id(0)
    nb = pl.num_programs(0)
    bufs = (kbuf, vbuf)

    def half_copy(seq, kind, p):
        return pltpu.make_async_copy(pool_ref.at[pt_ref[seq, p], pl.ds(kind * KV_D, KV_D)],
                                     bufs[kind].at[pl.ds(pl.multiple_of(p * KV_D, KV_D), KV_D)], sems.at[kind])

    def start_half(seq, kind):
        lax.fori_loop(0, n_pages, lambda p, c: (half_copy(seq, kind, p).start(), c)[1], 0)

    def wait_half(seq, kind):
        lax.fori_loop(0, n_pages, lambda p, c: (half_copy(seq, kind, p).wait(), c)[1], 0)

    @pl.when(b == 0)
    def _():
        start_half(b, 0)
        start_half(b, 1)

    for kind in range(2):
        wait_half(b, kind)
        for h in range(NSA_KV_HEADS):
            def add_feature(d, acc, kind=kind, h=h):
                x = bufs[kind][pl.ds(h * HEAD_DIM + d, n_pages, stride=KV_D), :] + pe_ref[kind, pl.ds(d, 1), :]
                return acc + jnp.dot(x.astype(BF16), w1_ref[kind, d], preferred_element_type=F32)

            acc = lax.fori_loop(0, HEAD_DIM, add_feature,
                                jnp.zeros((n_pages, BLOCKS_PER_PAGE * CMP_HIDDEN), F32), unroll=8)
            hid = _silu(acc + b1_ref[kind])
            o_ref[0, kind * NSA_KV_HEADS + h] = (
                jnp.dot(hid.astype(BF16), w2_ref[kind], preferred_element_type=F32) + b2_ref[kind])

        @pl.when(b + 1 < nb)
        def _(kind=kind):
            start_half(b + 1, kind)


def _compress_pages(pool_t, page_table, consts):
    bs, n_pages = page_table.shape
    const = lambda a: pl.BlockSpec(a.shape, lambda b, pt: (0,) * a.ndim)
    return pl.pallas_call(
        functools.partial(_compress_pages_kernel, n_pages=n_pages),
        grid_spec=pltpu.PrefetchScalarGridSpec(
            num_scalar_prefetch=1,
            grid=(bs,),
            in_specs=[const(a) for a in consts] + [pl.BlockSpec(memory_space=pl.ANY)],
            out_specs=pl.BlockSpec((1, 2 * NSA_KV_HEADS, n_pages, LANES), lambda b, pt: (b, 0, 0, 0)),
            scratch_shapes=[pltpu.VMEM((n_pages * KV_D, PAGE_SIZE), F32), pltpu.VMEM((n_pages * KV_D, PAGE_SIZE), F32),
                            pltpu.SemaphoreType.DMA((2,))]),
        out_shape=jax.ShapeDtypeStruct((bs, 2 * NSA_KV_HEADS, n_pages, LANES), F32),
        compiler_params=_cparams(("arbitrary",)),
        name="compress_pages",
    )(page_table, *consts, pool_t)


def _group_heads(q_row, hk):
    hpg = NSA_HEADS // NSA_KV_HEADS
    low = lax.broadcasted_iota(jnp.int32, (1, LANES), 1) < HEAD_DIM
    rows = []
    for r in range(hpg):
        head = hk * hpg + r
        tile = q_row[:, (head // 2) * LANES:(head // 2 + 1) * LANES]
        if head % 2 == 1:
            tile = pltpu.roll(tile, HEAD_DIM, 1)
        rows.append(jnp.where(low, tile, 0.0))
    return jnp.concatenate(rows + [jnp.zeros((SUBLANES - hpg, LANES), F32)], axis=0)


def _spread_heads(o_groups):
    hpg = NSA_HEADS // NSA_KV_HEADS
    return jnp.concatenate([o[r:r + 1, 0:HEAD_DIM] for o in o_groups for r in range(hpg)], axis=1)


def _nsa_sample_cmp_t_kernel(qc_ref, cmp_ref, ocmp_ref, idx_ref, *, n_pages):
    b = pl.program_id(0)
    nc = n_pages * BLOCKS_PER_PAGE
    scale = HEAD_DIM ** -0.5
    hpg = NSA_HEADS // NSA_KV_HEADS
    q_row = qc_ref[pl.ds(b, 1), :] * scale
    lane = lax.broadcasted_iota(jnp.int32, (1, LANES), 1)
    pos_r = lax.broadcasted_iota(jnp.int32, (1, nc), 1)
    bid_r = (pos_r % n_pages) * BLOCKS_PER_PAGE + pos_r // n_pages
    pos_c = lax.broadcasted_iota(jnp.int32, (nc, 1), 0)
    bid_c = (pos_c % n_pages) * BLOCKS_PER_PAGE + pos_c // n_pages
    o_groups = []
    for hk in range(NSA_KV_HEADS):
        kc = cmp_ref[0, hk].astype(BF16)
        vc = cmp_ref[0, NSA_KV_HEADS + hk].astype(BF16)
        qh = _group_heads(q_row, hk)
        s = jnp.concatenate(
            [lax.dot_general(pltpu.roll(qh, c * HEAD_DIM, 1).astype(BF16) if c else qh.astype(BF16), kc,
                             (((1,), (1,)), ((), ())), preferred_element_type=F32)
             for c in range(BLOCKS_PER_PAGE)], axis=1)
        ex = jnp.exp(s - jnp.max(s, axis=-1, keepdims=True))
        p = ex / jnp.sum(ex, axis=-1, keepdims=True)
        o = jnp.dot(p[:, 0:n_pages].astype(BF16), vc, preferred_element_type=F32)
        for c in range(1, BLOCKS_PER_PAGE):
            oc = jnp.dot(p[:, c * n_pages:(c + 1) * n_pages].astype(BF16), vc, preferred_element_type=F32)
            o = o + pltpu.roll(oc, LANES - c * HEAD_DIM, 1)
        o_groups.append(o)
        hrow = lax.broadcasted_iota(jnp.int32, p.shape, 0) < hpg
        imp = jnp.sum(jnp.where(hrow, p, 0.0), axis=0, keepdims=True)
        score = jnp.where((bid_r == 0) | (bid_r == nc - 1), FORCED_SCORE, imp)
        score_col = jnp.concatenate([score, jnp.zeros((LANES - 1, nc), F32)], 0).T[:, 0:1]
        beats = (score_col > score) | ((score_col == score) & (bid_c < bid_r))
        rank = jnp.sum(beats.astype(F32), axis=0, keepdims=True)
        row = jnp.zeros((1, LANES), F32)
        bid_f = bid_r.astype(F32)
        for k in range(SEL_PAST):
            blk = jnp.sum(jnp.where(rank == k, bid_f, 0.0), axis=1, keepdims=True)
            row = jnp.where(lane == k, blk, row)
        idx_ref[pl.ds(b * NSA_KV_HEADS + hk, 1), :] = row.astype(jnp.int32)
    ocmp_ref[pl.ds(b, 1), :] = _spread_heads(o_groups)


def _nsa_sample_cmp_t(qc, kvcmp_t):
    bs, _, n_pages, _ = kvcmp_t.shape
    return pl.pallas_call(
        functools.partial(_nsa_sample_cmp_t_kernel, n_pages=n_pages),
        grid=(bs,),
        in_specs=[pl.BlockSpec((bs, NSA_D), lambda b: (0, 0)),
                  pl.BlockSpec((1, 2 * NSA_KV_HEADS, n_pages, LANES), lambda b: (b, 0, 0, 0))],
        out_specs=[pl.BlockSpec((bs, NSA_D), lambda b: (0, 0)),
                   pl.BlockSpec((bs * NSA_KV_HEADS, LANES), lambda b: (0, 0))],
        out_shape=[jax.ShapeDtypeStruct((bs, NSA_D), F32),
                   jax.ShapeDtypeStruct((bs * NSA_KV_HEADS, LANES), jnp.int32)],
        compiler_params=_cparams(("arbitrary",)),
        name="nsa_sample_cmp",
    )(qc, kvcmp_t)


def _sel_block_copies(pool_ref, pt_ref, sel_ref, kbuf, vbuf, sem, b, hk, k):
    blk = sel_ref[b * NSA_KV_HEADS + hk, k]
    page = pt_ref[b, lax.shift_right_logical(blk, int(math.log2(BLOCKS_PER_PAGE)))]
    j = hk * SEL_PAST + k
    return (pltpu.make_async_copy(pool_ref.at[page, pl.ds(hk * HEAD_DIM, HEAD_DIM)], kbuf.at[j], sem),
            pltpu.make_async_copy(pool_ref.at[page, pl.ds(KV_D + hk * HEAD_DIM, HEAD_DIM)], vbuf.at[j], sem))


def _nsa_sample_attn_t_kernel(pt_ref, sel_ref, qr_ref, new_sel_ref, new_win_ref, win_ref, dtg_ref, ocmp_ref,
                              pool_ref, o_ref, kbuf, vbuf, sem):
    b = pl.program_id(0)
    for hk in range(NSA_KV_HEADS):
        for k in range(SEL_PAST):
            for cp in _sel_block_copies(pool_ref, pt_ref, sel_ref, kbuf, vbuf, sem, b, hk, k):
                cp.start()
    for hk in range(NSA_KV_HEADS):
        for k in range(SEL_PAST):
            for cp in _sel_block_copies(pool_ref, pt_ref, sel_ref, kbuf, vbuf, sem, b, hk, k):
                cp.wait()
    scale = HEAD_DIM ** -0.5
    q_row = qr_ref[pl.ds(b, 1), :] * scale
    sig = _sigmoid(dtg_ref[pl.ds(b, 1), :])
    lane = lax.broadcasted_iota(jnp.int32, (1, PAGE_SIZE), 1)
    o_slc, o_win = [], []
    for hk in range(NSA_KV_HEADS):
        qh = _group_heads(q_row, hk)[:, 0:HEAD_DIM].astype(BF16)

        def new_row(ref, kind):
            t = ref[pl.ds(b, 1), :][:, kind * KV_D:(kind + 1) * KV_D]
            if hk == 1:
                t = pltpu.roll(t, HEAD_DIM, 1)
            return t[:, 0:HEAD_DIM].astype(BF16).astype(F32)

        def attend(kt, vt, mask, new_ref, n_new):
            s = jnp.dot(qh, kt.astype(BF16), preferred_element_type=F32)
            if mask is not None:
                s = jnp.where(mask, s, NEG)
            s_new = jnp.sum(qh.astype(F32) * new_row(new_ref, 0), axis=1, keepdims=True)
            m = jnp.maximum(jnp.max(s, axis=-1, keepdims=True), s_new)
            ex = jnp.exp(s - m)
            ex_new = jnp.exp(s_new - m) * n_new
            den = jnp.sum(ex, axis=-1, keepdims=True) + ex_new
            o = lax.dot_general((ex / den).astype(BF16), vt.astype(BF16), (((1,), (1,)), ((), ())),
                                preferred_element_type=F32)
            return o + (ex_new / den).astype(BF16).astype(F32) * new_row(new_ref, 1)

        kt = jnp.concatenate([kbuf[hk * SEL_PAST + k] for k in range(SEL_PAST)], axis=1)
        vt = jnp.concatenate([vbuf[hk * SEL_PAST + k] for k in range(SEL_PAST)], axis=1)
        mask = jnp.concatenate(
            [lane // SEL_BLOCK == (sel_ref[b * NSA_KV_HEADS + hk, k] & (BLOCKS_PER_PAGE - 1))
             for k in range(SEL_PAST)], axis=1)
        o_slc.append(attend(kt, vt, mask, new_sel_ref, float(SEL_BLOCK)))
        o_win.append(attend(win_ref[0, hk * HEAD_DIM:(hk + 1) * HEAD_DIM, :],
                            win_ref[0, KV_D + hk * HEAD_DIM:KV_D + (hk + 1) * HEAD_DIM, :], None, new_win_ref, 1.0))
    gates = []
    for br in range(3):
        gates.append(jnp.concatenate(
            [jnp.broadcast_to(sig[:, GATE_COL0 + h * 3 + br:GATE_COL0 + h * 3 + br + 1], (1, HEAD_DIM))
             for h in range(NSA_HEADS)], axis=1))
    o_ref[pl.ds(b, 1), :] = (gates[0] * ocmp_ref[pl.ds(b, 1), :] + gates[1] * _spread_heads(o_slc)
                             + gates[2] * _spread_heads(o_win))


def _nsa_sample_attn_t(qr, new_sel, new_win, win_t, dtg, o_cmp, pool_sel_t, page_table, sel_idx):
    bs = qr.shape[0]
    const = lambda a: pl.BlockSpec(a.shape, lambda b, pt, sel: (0,) * a.ndim)
    n_buf = NSA_KV_HEADS * SEL_PAST
    return pl.pallas_call(
        _nsa_sample_attn_t_kernel,
        grid_spec=pltpu.PrefetchScalarGridSpec(
            num_scalar_prefetch=2,
            grid=(bs,),
            in_specs=[const(qr), const(new_sel), const(new_win),
                      pl.BlockSpec((1,) + win_t.shape[1:], lambda b, pt, sel: (b, 0, 0)),
                      const(dtg), const(o_cmp), pl.BlockSpec(memory_space=pl.ANY)],
            out_specs=pl.BlockSpec((bs, NSA_D), lambda b, pt, sel: (0, 0)),
            scratch_shapes=[pltpu.VMEM((n_buf, HEAD_DIM, PAGE_SIZE), F32), pltpu.VMEM((n_buf, HEAD_DIM, PAGE_SIZE), F32),
                            pltpu.SemaphoreType.DMA]),
        out_shape=jax.ShapeDtypeStruct((bs, NSA_D), F32),
        compiler_params=_cparams(("arbitrary",)),
        name="nsa_sample_attn",
    )(page_table, sel_idx, qr, new_sel, new_win, win_t, dtg, o_cmp, pool_sel_t)


def kernel(x_prompt, x_sample, cache_kv_cmp, cache_kv_sel, page_table, cache_kv_win, state_ssm, state_conv,
           emb_ln_g, emb_ln_b, w_in, conv_w, conv_b, dt_bias, a_log, d_skip, ssd_norm_w,
           cmp_pe, cmp_w1, cmp_b1, cmp_w2, cmp_b2, w_out, ln1_g, ln1_b,
           router_w, router_bias, exp_w_gate, exp_w_up, exp_w_down,
           sh_w_gate, sh_w_up, sh_w_down, ln2_g, ln2_b):
    bp, tp, _ = x_prompt.shape
    bs, ts, _ = x_sample.shape
    assert ts == 1 and DEPTH == 1
    n_prompt = bp * tp
    past_len = page_table.shape[1] * PAGE_SIZE
    l = 0
    w_perm = _permute_w_in(w_in[l])
    ln0_g, ln0_b = emb_ln_g[None], emb_ln_b[None]
    ssd_consts = (conv_w[l], conv_b[l][None], _pad_lanes(dt_bias[l]), _pad_lanes(a_log[l]),
                  jnp.repeat(d_skip[l], HEAD_DIM)[None], ssd_norm_w[l][None])
    cmp_consts = _compress_consts(cmp_pe[l], cmp_w1[l], cmp_b1[l], cmp_w2[l], cmp_b2[l])
    w_o = w_out[l].astype(BF16)
    w_o_ssd, w_o_nsa = w_o[:SSD_D], w_o[SSD_D:]
    ln1 = (ln1_g[l][None], ln1_b[l][None])
    kv_shape = (2, NSA_KV_HEADS, HEAD_DIM)

    hp, z, xbc, qc, qr, kvc, kvs, kvw, dtg = _inproj(
        x_prompt.reshape(n_prompt, D_MODEL), ln0_g, ln0_b, w_perm, _rope_tables(jnp.arange(tp)), 256)
    y_ssd, ssm_p, conv_p = _ssd_prompt(xbc, z, dtg, *ssd_consts, bp, tp)
    kvcmp = _compress_prompt(kvc, cmp_consts, tp)
    y_nsa = _nsa_prompt(qc, qr, dtg, kvcmp, kvs, kvw, bp, tp)
    h1p = _outproj(y_ssd, y_nsa, hp, w_o_ssd, w_o_nsa, *ln1, 256)
    n_keep = min(WINDOW, tp)
    kvc_p = kvc.reshape((1, bp, tp) + kv_shape)
    kvs_p = kvs.reshape((1, bp, tp) + kv_shape)
    kvw_p = kvw.reshape((1, bp, tp) + kv_shape)[:, :, tp - n_keep:]

    s_hs, s_z, s_xbc, s_qc, s_qr, s_kvc, s_kvs, s_kvw, s_dtg = _inproj(
        x_sample.reshape(bs, D_MODEL), ln0_g, ln0_b, w_perm, _rope_tables(jnp.full((bs,), past_len)), bs)
    s_y_ssd, ssm_s, conv_s_t = _ssd_sample(s_xbc, s_z, s_dtg, jnp.swapaxes(state_conv[l], 0, 1), state_ssm[l],
                                           *ssd_consts)
    n_pool = cache_kv_cmp.shape[1]
    feature_major = lambda c, rows: jnp.swapaxes(c.reshape(-1, rows, 2 * KV_D), 1, 2)
    s_kvcmp = _compress_pages(feature_major(cache_kv_cmp[l], PAGE_SIZE), page_table,
                              _compress_consts_t(cmp_pe[l], cmp_w1[l], cmp_b1[l], cmp_w2[l], cmp_b2[l]))
    s_o_cmp, s_sel = _nsa_sample_cmp_t(s_qc, s_kvcmp)
    buf_win = cache_kv_win[l].reshape(bs, -1, 2 * KV_D)
    s_y_nsa = _nsa_sample_attn_t(
        s_qr, s_kvs, s_kvw, feature_major(cache_kv_win[l], buf_win.shape[1]), s_dtg, s_o_cmp,
        feature_major(cache_kv_sel[l], PAGE_SIZE), page_table, s_sel)
    h1s = _outproj(s_y_ssd, s_y_nsa, s_hs, w_o_ssd, w_o_nsa, *ln1, bs)
    win_all = jnp.concatenate([buf_win, s_kvw[:, None, :]], 1)
    n_keep_s = min(WINDOW, past_len + ts)
    kvw_s = win_all[:, win_all.shape[1] - n_keep_s:].reshape((1, bs, n_keep_s) + kv_shape)
    kvc_s = s_kvc.reshape((1, bs, ts) + kv_shape)
    kvs_s = s_kvs.reshape((1, bs, ts) + kv_shape)

    n_tok = n_prompt + bs * ts
    n_pad = -(-n_tok // MOE_TOKENS) * MOE_TOKENS
    tok = jnp.concatenate([h1p, h1s, jnp.zeros((n_pad - n_tok, D_MODEL), F32)], 0)
    out = _moe_ln(tok, n_tok, router_w[l], router_bias[l], exp_w_gate[l], exp_w_up[l], exp_w_down[l],
                  sh_w_gate[l], sh_w_up[l], sh_w_down[l], ln2_g[l][None], ln2_b[l][None])
    y_prompt = out[:n_prompt].reshape(bp, tp, D_MODEL)
    y_sample = out[n_prompt:n_tok].reshape(bs, ts, D_MODEL)
    return (y_prompt, y_sample, kvc_p, kvs_p, kvw_p, ssm_p[None], conv_p[None],
            kvc_s, kvs_s, kvw_s, ssm_s[None], jnp.swapaxes(conv_s_t, 0, 1)[None])
```

```python
import functools
import math

import jax
import jax.numpy as jnp
import numpy as np
from jax import lax
from jax.experimental import pallas as pl
from jax.experimental.pallas import tpu as pltpu

D_MODEL = 1024
HEAD_DIM = 64
SSD_HEADS = 8
SSD_D = SSD_HEADS * HEAD_DIM
SSD_GROUPS = 2
SSD_STATE = 128
SSD_CONV = 4
SSD_CONV_CH = SSD_D + 2 * SSD_GROUPS * SSD_STATE
SSD_CHUNK = 128
NSA_HEADS = 8
NSA_KV_HEADS = 2
NSA_D = NSA_HEADS * HEAD_DIM
KV_D = NSA_KV_HEADS * HEAD_DIM
CMP_BLOCK = 64
CMP_HIDDEN = 128
SEL_BLOCK = 64
TOP_N = 16
WINDOW = 512
Q_BLOCK = 128
ROT_DIM = HEAD_DIM // 4
ROPE_THETA = 500000.0
N_EXPERTS = 64
TOP_K = 6
N_EXPERT_GROUPS = 8
EXPERTS_PER_GROUP = N_EXPERTS // N_EXPERT_GROUPS
TOPK_GROUPS = 4
D_EXPERT = 256
D_SHARED = 256
ROUTED_SCALE = 2.5
MOE_BLOCK = 256
DEPTH = 1
DEEPNORM_ALPHA = (2.0 * DEPTH) ** 0.25
LN_EPS = 1e-5
RMS_EPS = 1e-5
NEG = -1e30
FORCED_SCORE = 1e4
PAGE_SIZE = 128

LANES = 128
SUBLANES = 8
VMEM_LIMIT_BYTES = 56 * 1024 * 1024

U_Z = 0
U_XBC = U_Z + SSD_D
U_Q = U_XBC + SSD_CONV_CH
U_KVC = U_Q + NSA_D
U_KVS = U_KVC + 2 * KV_D
U_KVW = U_KVS + 2 * KV_D
U_DTG = U_KVW + 2 * KV_D
U_TOTAL = U_DTG + LANES
GATE_COL0 = SSD_HEADS

BF16 = jnp.bfloat16
F32 = jnp.float32


def _cparams(sem):
    return pltpu.CompilerParams(dimension_semantics=sem, vmem_limit_bytes=VMEM_LIMIT_BYTES)


def _bdot(a, b):
    return jnp.dot(a.astype(BF16), b.astype(BF16), preferred_element_type=F32)


def _bdot_nt(a, b):
    return lax.dot_general(a.astype(BF16), b.astype(BF16), (((1,), (1,)), ((), ())),
                           preferred_element_type=F32)


def _hdot(a, b):
    return jnp.dot(a, b, preferred_element_type=F32, precision=lax.Precision.HIGHEST)


def _sigmoid(x):
    return 1.0 / (1.0 + jnp.exp(-x))


def _silu(x):
    return x * _sigmoid(x)


def _layer_norm(x, g, b):
    mu = jnp.mean(x, axis=-1, keepdims=True)
    xc = x - mu
    var = jnp.mean(xc * xc, axis=-1, keepdims=True)
    return xc * lax.rsqrt(var + LN_EPS) * g + b


def _rope_tile(x, cos, sa, sb):
    return x * cos + pltpu.roll(x, LANES - ROT_DIM // 2, 1) * sa + pltpu.roll(x, ROT_DIM // 2, 1) * sb


def _inproj_kernel(x_ref, g_ref, b_ref, w_ref, rope_ref,
                   h_ref, z_ref, xbc_ref, qc_ref, qr_ref, kvc_ref, kvs_ref, kvw_ref, dtg_ref):
    h = _layer_norm(x_ref[...], g_ref[...], b_ref[...])
    h_ref[...] = h
    u = jnp.dot(h.astype(BF16), w_ref[...], preferred_element_type=F32)
    cos = rope_ref[:, 0:LANES]
    sa = rope_ref[:, LANES:2 * LANES]
    sb = rope_ref[:, 2 * LANES:3 * LANES]
    z_ref[...] = u[:, U_Z:U_XBC]
    xbc_ref[...] = u[:, U_XBC:U_Q]
    qc_ref[...] = u[:, U_Q:U_KVC]
    for c in range(NSA_D // LANES):
        qr_ref[:, c * LANES:(c + 1) * LANES] = _rope_tile(u[:, U_Q + c * LANES:U_Q + (c + 1) * LANES], cos, sa, sb)
    kvc_ref[...] = u[:, U_KVC:U_KVS]
    kvs_ref[:, 0:KV_D] = _rope_tile(u[:, U_KVS:U_KVS + KV_D], cos, sa, sb)
    kvs_ref[:, KV_D:2 * KV_D] = u[:, U_KVS + KV_D:U_KVW]
    kvw_ref[:, 0:KV_D] = _rope_tile(u[:, U_KVW:U_KVW + KV_D], cos, sa, sb)
    kvw_ref[:, KV_D:2 * KV_D] = u[:, U_KVW + KV_D:U_DTG]
    dtg_ref[...] = u[:, U_DTG:U_TOTAL]


def _rope_tables(pos):
    half = ROT_DIM // 2
    inv = ROPE_THETA ** (-jnp.arange(half, dtype=F32) / half)
    ang = pos.astype(F32)[:, None] * inv
    cos, sin = jnp.cos(ang), jnp.sin(ang)
    ones = jnp.ones((pos.shape[0], HEAD_DIM - ROT_DIM), F32)
    zeros = jnp.zeros((pos.shape[0], HEAD_DIM - ROT_DIM), F32)
    zh = jnp.zeros_like(sin)
    c = jnp.concatenate([cos, cos, ones], 1)
    sa = jnp.concatenate([-sin, zh, zeros], 1)
    sb = jnp.concatenate([zh, sin, zeros], 1)
    return jnp.concatenate([jnp.tile(t, (1, LANES // HEAD_DIM)) for t in (c, sa, sb)], 1)


def _permute_w_in(w):
    sizes = (SSD_D, SSD_CONV_CH, SSD_HEADS, NSA_D, KV_D, KV_D, KV_D, KV_D, KV_D, KV_D, 3 * NSA_HEADS)
    offs = np.concatenate([[0], np.cumsum(sizes)])
    seg = [w[:, offs[i]:offs[i + 1]] for i in range(len(sizes))]
    pad = jnp.zeros((w.shape[0], LANES - SSD_HEADS - 3 * NSA_HEADS), w.dtype)
    out = jnp.concatenate([seg[0], seg[1], seg[3], seg[4], seg[5], seg[6], seg[7], seg[8], seg[9],
                           seg[2], seg[10], pad], 1)
    return out.astype(BF16)


def _inproj(x, ln_g, ln_b, w_perm, rope_tab, tm):
    n = x.shape[0]
    nt = n // tm
    n_rope_blocks = rope_tab.shape[0] // tm
    row = lambda w: pl.BlockSpec((tm, w), lambda i: (i, 0))
    const = lambda a: pl.BlockSpec(a.shape, lambda i: (0,) * a.ndim)
    widths = (D_MODEL, SSD_D, SSD_CONV_CH, NSA_D, NSA_D, 2 * KV_D, 2 * KV_D, 2 * KV_D, LANES)
    return pl.pallas_call(
        _inproj_kernel,
        grid=(nt,),
        in_specs=[row(D_MODEL), const(ln_g), const(ln_b), const(w_perm),
                  pl.BlockSpec((tm, 3 * LANES), lambda i: (i % n_rope_blocks, 0))],
        out_specs=[row(w) for w in widths],
        out_shape=[jax.ShapeDtypeStruct((n, w), F32) for w in widths],
        compiler_params=_cparams(("parallel",)),
        name="inproj",
    )(x, ln_g, ln_b, w_perm, rope_tab)


def _softplus(x):
    return jnp.maximum(x, 0.0) + jnp.log1p(jnp.exp(-jnp.abs(x)))


def _gated_group_norm(y, z, norm_w):
    y = y * _silu(z)
    gw = SSD_D // SSD_GROUPS
    parts = []
    for g in range(SSD_GROUPS):
        yg = y[:, g * gw:(g + 1) * gw]
        ms = jnp.mean(yg * yg, axis=-1, keepdims=True)
        parts.append(yg * lax.rsqrt(ms + RMS_EPS))
    return jnp.concatenate(parts, axis=1) * norm_w


def _ssd_prompt_kernel(xbc_ref, z_ref, dtg_ref, convw_ref, convb_ref, dtb_ref, alog_ref, dskip_ref, normw_ref,
                       y_ref, state_ref, conv_ref, ext_ref, s_ref):
    c = pl.program_id(1)
    nc = pl.num_programs(1)
    L = SSD_CHUNK
    halo = SUBLANES

    @pl.when(c == 0)
    def _():
        ext_ref[0:halo, :] = jnp.zeros((halo, SSD_CONV_CH), F32)
        s_ref[...] = jnp.zeros_like(s_ref)

    xin = xbc_ref[...]
    ext_ref[halo:halo + L, :] = xin
    xc = convw_ref[SSD_CONV - 1:SSD_CONV, :] * xin
    for k in range(SSD_CONV - 1):
        off = halo - (SSD_CONV - 1) + k
        xc = xc + convw_ref[k:k + 1, :] * ext_ref[off:off + L, :]
    ext_ref[0:halo, :] = ext_ref[L:L + halo, :]
    xc = _silu(xc + convb_ref[...])
    xs = xc[:, 0:SSD_D]
    ns = SSD_GROUPS * SSD_STATE
    bm = xc[:, SSD_D:SSD_D + ns]
    cm = xc[:, SSD_D + ns:SSD_D + 2 * ns]

    dt = _softplus(dtg_ref[...] + dtb_ref[...])
    da = dt * (-jnp.exp(alog_ref[...]))
    row = lax.broadcasted_iota(jnp.int32, (L, L), 0)
    col = lax.broadcasted_iota(jnp.int32, (L, L), 1)
    tril = row >= col
    acum = _hdot(tril.astype(F32), da)
    acum_t = acum.T
    eacum = jnp.exp(acum)
    alast = acum[L - 1:L, :]
    edecay = jnp.exp(alast - acum)
    elast = jnp.exp(alast)

    dt_full = jnp.concatenate([jnp.broadcast_to(dt[:, h:h + 1], (L, HEAD_DIM)) for h in range(SSD_HEADS)], 1)
    dec_full = jnp.concatenate([jnp.broadcast_to(edecay[:, h:h + 1], (L, HEAD_DIM)) for h in range(SSD_HEADS)], 1)
    xdt = xs * dt_full
    xdec_t = (xdt * dec_full).T

    hpg = SSD_HEADS // SSD_GROUPS
    y_parts = []
    for h in range(SSD_HEADS):
        g = h // hpg
        b_g = bm[:, g * SSD_STATE:(g + 1) * SSD_STATE]
        c_g = cm[:, g * SSD_STATE:(g + 1) * SSD_STATE]
        if h % hpg == 0:
            cb = _bdot_nt(c_g, b_g)
        seg = acum[:, h:h + 1] - acum_t[h:h + 1, :]
        lmat = jnp.where(tril, jnp.exp(jnp.where(tril, seg, 0.0)), 0.0)
        xdt_h = xdt[:, h * HEAD_DIM:(h + 1) * HEAD_DIM]
        y_h = _bdot(cb * lmat, xdt_h)
        s_prev = s_ref[h]
        y_h = y_h + _bdot_nt(c_g, s_prev) * eacum[:, h:h + 1]
        y_h = y_h + dskip_ref[:, h * HEAD_DIM:(h + 1) * HEAD_DIM] * xs[:, h * HEAD_DIM:(h + 1) * HEAD_DIM]
        y_parts.append(y_h)
        s_ref[h] = elast[:, h:h + 1] * s_prev + _bdot(xdec_t[h * HEAD_DIM:(h + 1) * HEAD_DIM, :], b_g)
    y = jnp.concatenate(y_parts, axis=1)
    y_ref[...] = _gated_group_norm(y, z_ref[...], normw_ref[...])

    @pl.when(c == nc - 1)
    def _():
        state_ref[0] = s_ref[...]
        conv_ref[0] = xin[L - (SSD_CONV - 1):L, :]


def _ssd_prompt(xbc, z, dtg, conv_w, conv_b, dt_bias_pad, a_log_pad, d_skip_full, norm_w, bn, t):
    nc = t // SSD_CHUNK
    row = lambda w: pl.BlockSpec((SSD_CHUNK, w), lambda b, c: (b * nc + c, 0))
    const = lambda a: pl.BlockSpec(a.shape, lambda b, c: (0,) * a.ndim)
    return pl.pallas_call(
        _ssd_prompt_kernel,
        grid=(bn, nc),
        in_specs=[row(SSD_CONV_CH), row(SSD_D), row(LANES), const(conv_w), const(conv_b), const(dt_bias_pad),
                  const(a_log_pad), const(d_skip_full), const(norm_w)],
        out_specs=[row(SSD_D),
                   pl.BlockSpec((1, SSD_HEADS, HEAD_DIM, SSD_STATE), lambda b, c: (b, 0, 0, 0)),
                   pl.BlockSpec((1, SSD_CONV - 1, SSD_CONV_CH), lambda b, c: (b, 0, 0))],
        out_shape=[jax.ShapeDtypeStruct((bn * t, SSD_D), F32),
                   jax.ShapeDtypeStruct((bn, SSD_HEADS, HEAD_DIM, SSD_STATE), F32),
                   jax.ShapeDtypeStruct((bn, SSD_CONV - 1, SSD_CONV_CH), F32)],
        scratch_shapes=[pltpu.VMEM((SSD_CHUNK + 2 * SUBLANES, SSD_CONV_CH), F32),
                        pltpu.VMEM((SSD_HEADS, HEAD_DIM, SSD_STATE), F32)],
        compiler_params=_cparams(("parallel", "arbitrary")),
        name="ssd_prompt",
    )(xbc, z, dtg, conv_w, conv_b, dt_bias_pad, a_log_pad, d_skip_full, norm_w)


def _pad_lanes(v, fill=0.0):
    return jnp.concatenate([v.astype(F32), jnp.full((LANES - v.shape[0],), fill, F32)])[None]


def _compress_rows(k_ref, v_ref, pe_ref, w1k_ref, w1v_ref, b1_ref, w2k_ref, w2v_ref, b2_ref, nb):
    acck = jnp.zeros((nb, 2 * CMP_HIDDEN), F32)
    accv = jnp.zeros((nb, 2 * CMP_HIDDEN), F32)
    for l in range(CMP_BLOCK):
        xk = k_ref[pl.ds(l, nb, stride=CMP_BLOCK), :] + pe_ref[l:l + 1, 0:KV_D]
        xv = v_ref[pl.ds(l, nb, stride=CMP_BLOCK), :] + pe_ref[l:l + 1, KV_D:2 * KV_D]
        acck = acck + jnp.dot(xk.astype(BF16), w1k_ref[l], preferred_element_type=F32)
        accv = accv + jnp.dot(xv.astype(BF16), w1v_ref[l], preferred_element_type=F32)
    hk = _silu(acck + b1_ref[:, 0:2 * CMP_HIDDEN])
    hv = _silu(accv + b1_ref[:, 2 * CMP_HIDDEN:4 * CMP_HIDDEN])
    ok = jnp.dot(hk.astype(BF16), w2k_ref[...], preferred_element_type=F32) + b2_ref[:, 0:KV_D]
    ov = jnp.dot(hv.astype(BF16), w2v_ref[...], preferred_element_type=F32) + b2_ref[:, KV_D:2 * KV_D]
    return jnp.concatenate([ok, ov], axis=1)


def _compress_kernel(k_ref, v_ref, pe_ref, w1k_ref, w1v_ref, b1_ref, w2k_ref, w2v_ref, b2_ref, o_ref, *, nb):
    o_ref[...] = _compress_rows(k_ref, v_ref, pe_ref, w1k_ref, w1v_ref, b1_ref, w2k_ref, w2v_ref, b2_ref, nb)


def _block_diag2(w):
    z = jnp.zeros_like(w)
    return jnp.concatenate([jnp.concatenate([w, z], -1), jnp.concatenate([z, w], -1)], -2)


def _compress_consts(cmp_pe, cmp_w1, cmp_b1, cmp_w2, cmp_b2):
    pe = jnp.concatenate([cmp_pe[0], cmp_pe[0], cmp_pe[1], cmp_pe[1]], -1)
    w1k = _block_diag2(cmp_w1[0]).astype(BF16)
    w1v = _block_diag2(cmp_w1[1]).astype(BF16)
    b1 = jnp.concatenate([cmp_b1[0], cmp_b1[0], cmp_b1[1], cmp_b1[1]])[None]
    w2k = _block_diag2(cmp_w2[0]).astype(BF16)
    w2v = _block_diag2(cmp_w2[1]).astype(BF16)
    b2 = jnp.concatenate([cmp_b2[0], cmp_b2[0], cmp_b2[1], cmp_b2[1]])[None]
    return pe, w1k, w1v, b1, w2k, w2v, b2


def _compress_prompt(kvc, consts, rows_per_step):
    n = kvc.shape[0]
    nb = rows_per_step // CMP_BLOCK
    const = lambda a: pl.BlockSpec(a.shape, lambda i: (0,) * a.ndim)
    return pl.pallas_call(
        functools.partial(_compress_kernel, nb=nb),
        grid=(n // rows_per_step,),
        in_specs=[pl.BlockSpec((rows_per_step, KV_D), lambda i: (i, 0)),
                  pl.BlockSpec((rows_per_step, KV_D), lambda i: (i, 1))] + [const(a) for a in consts],
        out_specs=pl.BlockSpec((nb, 2 * KV_D), lambda i: (i, 0)),
        out_shape=jax.ShapeDtypeStruct((n // CMP_BLOCK, 2 * KV_D), F32),
        compiler_params=_cparams(("parallel",)),
        name="compress_prompt",
    )(kvc, kvc, *consts)


SEL_KEY_TILE = 512
WIN_KEYS = WINDOW + Q_BLOCK


def _dup_head(x, hk):
    sw = pltpu.roll(x, HEAD_DIM, 1)
    low = lax.broadcasted_iota(jnp.int32, x.shape, 1) < HEAD_DIM
    return jnp.where(low, x, sw) if hk == 0 else jnp.where(low, sw, x)


def _masked_softmax(s, mask):
    sm = jnp.where(mask, s, NEG)
    ex = jnp.where(mask, jnp.exp(sm - jnp.max(sm, axis=-1, keepdims=True)), 0.0)
    den = jnp.sum(ex, axis=-1, keepdims=True)
    return ex / jnp.where(den > 0.0, den, 1.0)


def _select_blocks_t(imp, cur, n_top):
    j = lax.broadcasted_iota(jnp.int32, imp.shape, 0)
    future = j > cur
    forced = (j == 0) | (j == cur) | (j == cur - 1)
    score = jnp.where(future, NEG, jnp.where(forced, FORCED_SCORE, imp))
    return ((_rank_rows(score) < n_top) & (score > 0.5 * NEG)).astype(F32)


def _nsa_prompt_kernel(qc_ref, qr_ref, dtg_ref, cmp_ref, kvs_ref, kvw_ref, o_ref,
                       cmp_d, kvs_d, kvw_d, bias_ref, qrs_ref, m_ref, l_ref, acc_ref, *, t):
    qb = pl.program_id(1)
    nbk = t // SEL_BLOCK
    tq = Q_BLOCK
    tk = SEL_KEY_TILE
    hpg = NSA_HEADS // NSA_KV_HEADS
    scale = HEAD_DIM ** -0.5

    @pl.when(qb == 0)
    def _():
        cmp_d[...] = jnp.zeros_like(cmp_d)
        for src, dst, n in ((cmp_ref, cmp_d, nbk), (kvs_ref, kvs_d, t), (kvw_ref, kvw_d, t)):
            x = src[...]
            for hk in range(NSA_KV_HEADS):
                dst[hk, 0:n, 0:KV_D] = _dup_head(x[:, 0:KV_D], hk).astype(BF16)
                dst[hk, 0:n, KV_D:2 * KV_D] = _dup_head(x[:, KV_D:2 * KV_D], hk).astype(BF16)

    t0 = qb * tq
    rows = t0 + lax.broadcasted_iota(jnp.int32, (tq, 1), 0)
    lane = lax.broadcasted_iota(jnp.int32, (tq, LANES), 1)
    half_mask = (lane < HEAD_DIM, lane >= HEAD_DIM)
    sig = _sigmoid(dtg_ref[...])
    vis = (lane + 1) * CMP_BLOCK - 1 <= rows
    cur_l = (t0 + lax.broadcasted_iota(jnp.int32, (1, tq), 1)) // SEL_BLOCK
    expand = (lax.broadcasted_iota(jnp.int32, (LANES, t), 1) // SEL_BLOCK
              == lax.broadcasted_iota(jnp.int32, (LANES, t), 0)).astype(BF16)
    win_start = pl.multiple_of(jnp.maximum(t0 - WINDOW, 0), tq)
    wpos = win_start + lax.broadcasted_iota(jnp.int32, (tq, WIN_KEYS), 1)
    win_bias = jnp.where((wpos <= rows) & (wpos >= rows - WINDOW), 0.0, NEG)
    n_kt = (t0 + tq + tk - 1) // tk

    def stack_heads(ref, hk):
        parts = []
        for hh in range(hpg):
            head = hk * hpg + hh
            p, e = head // 2, head % 2
            parts.append(jnp.where(half_mask[e], ref[:, p * LANES:(p + 1) * LANES] * scale, 0.0))
        return jnp.concatenate(parts, axis=0).astype(BF16)

    o_cmp_g = []
    for hk in range(NSA_KV_HEADS):
        qcs = stack_heads(qc_ref, hk)
        s = lax.dot_general(qcs, cmp_d[hk, :, 0:KV_D], (((1,), (1,)), ((), ())), preferred_element_type=F32)
        pc = _masked_softmax(s.reshape(hpg, tq, LANES), vis[None])
        imp = jnp.sum(pc, axis=0)
        o_cmp_g.append(jnp.dot(pc.reshape(hpg * tq, LANES).astype(BF16), cmp_d[hk, :, KV_D:2 * KV_D],
                               preferred_element_type=F32).reshape(hpg, tq, LANES))

        sel_t = _select_blocks_t(imp.T[0:nbk, :], cur_l, TOP_N)
        sel = jnp.concatenate([sel_t, jnp.zeros((LANES - nbk, tq), F32)], axis=0).T
        selk = jnp.dot(sel.astype(BF16), expand, preferred_element_type=F32)
        for kt in range(t // tk):
            @pl.when(kt < n_kt)
            def _(kt=kt, hk=hk, selk=selk):
                kpos = kt * tk + lax.broadcasted_iota(jnp.int32, (tq, tk), 1)
                bias_ref[hk, kt] = jnp.where((selk[:, kt * tk:(kt + 1) * tk] > 0.5) & (kpos <= rows), 0.0, NEG)

        qrs_ref[hk] = stack_heads(qr_ref, hk)

    m_ref[...] = jnp.full(m_ref.shape, NEG, F32)
    l_ref[...] = jnp.zeros(l_ref.shape, F32)
    acc_ref[...] = jnp.zeros(acc_ref.shape, F32)

    def sel_step(kt, carry):
        k0 = pl.multiple_of(kt * tk, tk)
        for hk in range(NSA_KV_HEADS):
            kblk = kvs_d[hk, pl.ds(k0, tk), 0:KV_D]
            vblk = kvs_d[hk, pl.ds(k0, tk), KV_D:2 * KV_D]
            s = lax.dot_general(qrs_ref[hk], kblk, (((1,), (1,)), ((), ())), preferred_element_type=F32)
            s = s.reshape(hpg, tq, tk) + bias_ref[hk, kt][None]
            m_old = m_ref[hk]
            m_new = jnp.maximum(m_old, jnp.max(s, axis=-1, keepdims=True))
            alpha = jnp.exp(m_old - m_new)
            pe = jnp.exp(s - jnp.concatenate([m_new] * (tk // LANES), axis=-1))
            l_ref[hk] = alpha * l_ref[hk] + jnp.sum(pe, axis=-1, keepdims=True)
            pv = jnp.dot(pe.reshape(hpg * tq, tk).astype(BF16), vblk, preferred_element_type=F32)
            acc_ref[hk] = alpha * acc_ref[hk] + pv.reshape(hpg, tq, LANES)
            m_ref[hk] = m_new
        return carry

    lax.fori_loop(0, n_kt, sel_step, 0)

    for hk in range(NSA_KV_HEADS):
        o_cmp = o_cmp_g[hk]
        o_slc = acc_ref[hk] / l_ref[hk]
        kw = kvw_d[hk, pl.ds(win_start, WIN_KEYS), 0:KV_D]
        vw = kvw_d[hk, pl.ds(win_start, WIN_KEYS), KV_D:2 * KV_D]
        sw = lax.dot_general(qrs_ref[hk], kw, (((1,), (1,)), ((), ())), preferred_element_type=F32)
        sw = sw.reshape(hpg, tq, WIN_KEYS) + win_bias[None]
        pw = jnp.exp(sw - jnp.max(sw, axis=-1, keepdims=True))
        den = jnp.sum(pw, axis=-1, keepdims=True)
        o_win = jnp.dot(pw.reshape(hpg * tq, WIN_KEYS).astype(BF16), vw,
                        preferred_element_type=F32).reshape(hpg, tq, LANES) / den

        for hh in range(hpg):
            head = hk * hpg + hh
            p, e = head // 2, head % 2
            c0 = GATE_COL0 + head * 3
            mix = (sig[:, c0:c0 + 1] * o_cmp[hh] + sig[:, c0 + 1:c0 + 2] * o_slc[hh]
                   + sig[:, c0 + 2:c0 + 3] * o_win[hh])
            if e == 0:
                mix_even = mix
            else:
                o_ref[:, p * LANES:(p + 1) * LANES] = jnp.where(half_mask[0], mix_even, mix)


def _nsa_prompt(qc, qr, dtg, kvcmp, kvs, kvw, bn, t):
    nq = t // Q_BLOCK
    nbk = t // SEL_BLOCK
    hpg = NSA_HEADS // NSA_KV_HEADS
    assert nbk >= TOP_N and t >= WIN_KEYS and t % SEL_KEY_TILE == 0
    qrow = lambda w: pl.BlockSpec((Q_BLOCK, w), lambda b, i: (b * nq + i, 0))
    seq = lambda r: pl.BlockSpec((r, 2 * KV_D), lambda b, i: (b, 0))
    return pl.pallas_call(
        functools.partial(_nsa_prompt_kernel, t=t),
        grid=(bn, nq),
        in_specs=[qrow(NSA_D), qrow(NSA_D), qrow(LANES), seq(nbk), seq(t), seq(t)],
        out_specs=qrow(NSA_D),
        out_shape=jax.ShapeDtypeStruct((bn * t, NSA_D), F32),
        scratch_shapes=[pltpu.VMEM((NSA_KV_HEADS, LANES, 2 * KV_D), BF16),
                        pltpu.VMEM((NSA_KV_HEADS, t, 2 * KV_D), BF16),
                        pltpu.VMEM((NSA_KV_HEADS, t, 2 * KV_D), BF16),
                        pltpu.VMEM((NSA_KV_HEADS, t // SEL_KEY_TILE, Q_BLOCK, SEL_KEY_TILE), F32),
                        pltpu.VMEM((NSA_KV_HEADS, hpg * Q_BLOCK, LANES), BF16),
                        pltpu.VMEM((NSA_KV_HEADS, hpg, Q_BLOCK, LANES), F32),
                        pltpu.VMEM((NSA_KV_HEADS, hpg, Q_BLOCK, LANES), F32),
                        pltpu.VMEM((NSA_KV_HEADS, hpg, Q_BLOCK, LANES), F32)],
        compiler_params=_cparams(("parallel", "arbitrary")),
        name="nsa_prompt",
    )(qc, qr, dtg, kvcmp, kvs, kvw)


def _outproj_kernel(ys_ref, yn_ref, h_ref, ws_ref, wn_ref, g_ref, b_ref, o_ref):
    mix = jnp.dot(ys_ref[...].astype(BF16), ws_ref[...], preferred_element_type=F32)
    mix = mix + jnp.dot(yn_ref[...].astype(BF16), wn_ref[...], preferred_element_type=F32)
    o_ref[...] = _layer_norm(DEEPNORM_ALPHA * h_ref[...] + mix, g_ref[...], b_ref[...])


def _outproj(y_ssd, y_nsa, h, w_ssd, w_nsa, ln_g, ln_b, tm):
    n = h.shape[0]
    row = lambda w: pl.BlockSpec((tm, w), lambda i: (i, 0))
    const = lambda a: pl.BlockSpec(a.shape, lambda i: (0,) * a.ndim)
    return pl.pallas_call(
        _outproj_kernel,
        grid=(n // tm,),
        in_specs=[row(SSD_D), row(NSA_D), row(D_MODEL), const(w_ssd), const(w_nsa), const(ln_g), const(ln_b)],
        out_specs=row(D_MODEL),
        out_shape=jax.ShapeDtypeStruct((n, D_MODEL), F32),
        compiler_params=_cparams(("parallel",)),
        name="outproj",
    )(y_ssd, y_nsa, h, w_ssd, w_nsa, ln_g, ln_b)


MOE_TOKENS = 256
ROUTE_ROWS = 8


def _token_tile_specs(n_main_tiles, index_args=1):
    main = pl.BlockSpec((MOE_TOKENS, D_MODEL), lambda i, *_: (jnp.minimum(i, n_main_tiles - 1), 0))
    tail = pl.BlockSpec((MOE_TOKENS, D_MODEL), lambda i, *_: (0, 0))
    return main, tail


def _token_tile(i, n_main_tiles, main_ref, tail_ref):
    return jnp.where(i < n_main_tiles, main_ref[...], tail_ref[...])


def _rank_rows(x):
    n = x.shape[0]
    idx = lax.broadcasted_iota(jnp.int32, x.shape, 0)
    rank = jnp.zeros(x.shape, F32)
    for r in range(n):
        row = x[r:r + 1, :]
        rank = rank + ((row > x) | ((row == x) & (idx > r))).astype(F32)
    return rank


def _route_kernel(h_ref, ht_ref, rw_ref, rb_ref, slot_ref, tokinfo_ref, meta_ref, cnt_ref, carry_ref, carry_row_ref, *,
                  n_valid, n_main):
    i = pl.program_id(0)
    tm = MOE_TOKENS

    @pl.when(i == 0)
    def _():
        carry_ref[...] = jnp.zeros_like(carry_ref)
        carry_row_ref[...] = jnp.zeros_like(carry_row_ref)

    logits = lax.dot_general(rw_ref[...], _token_tile(i, n_main, h_ref, ht_ref).astype(BF16), (((1,), (1,)), ((), ())),
                             preferred_element_type=F32)
    scores = _sigmoid(logits)
    biased = scores + rb_ref[:, 0:1]
    b3 = biased.reshape(N_EXPERT_GROUPS, EXPERTS_PER_GROUP, tm)
    sidx = lax.broadcasted_iota(jnp.int32, b3.shape, 1)
    m1 = jnp.max(b3, axis=1, keepdims=True)
    first = jnp.min(jnp.where(b3 == m1, sidx, EXPERTS_PER_GROUP), axis=1, keepdims=True)
    m2 = jnp.max(jnp.where(sidx == first, -jnp.inf, b3), axis=1, keepdims=True)
    grp_score = (m1 + m2).reshape(N_EXPERT_GROUPS, tm)
    grp_keep = _rank_rows(grp_score) < TOPK_GROUPS
    masked = jnp.where(grp_keep.reshape(N_EXPERT_GROUPS, 1, tm), b3, NEG).reshape(N_EXPERTS, tm)
    rank = _rank_rows(masked)
    tok = i * tm + lax.broadcasted_iota(jnp.int32, (1, tm), 1)
    valid = tok < n_valid
    sel = (rank < TOP_K) & valid
    self32 = sel.astype(F32)
    wsel = self32 * scores
    wsum = jnp.sum(wsel, axis=0, keepdims=True)
    w = wsel / jnp.where(wsum > 0.0, wsum, 1.0) * ROUTED_SCALE

    selb = sel.astype(BF16)
    tri = lambda n, strict_upper: (
        (lax.broadcasted_iota(jnp.int32, (n, n), 0) < lax.broadcasted_iota(jnp.int32, (n, n), 1))
        if strict_upper else
        (lax.broadcasted_iota(jnp.int32, (n, n), 0) > lax.broadcasted_iota(jnp.int32, (n, n), 1))).astype(BF16)
    pad8 = lambda c: jnp.floor((c + (SUBLANES - 1.0)) * (1.0 / SUBLANES)) * SUBLANES
    pos_tile = jnp.dot(selb, tri(tm, True), preferred_element_type=F32)
    cnt_col = pad8(jnp.sum(self32, axis=1, keepdims=True))
    first_col = jnp.dot(tri(N_EXPERTS, False), jnp.broadcast_to(cnt_col, (N_EXPERTS, LANES)).astype(BF16),
                        preferred_element_type=F32)[:, 0:1]
    slot = first_col + pos_tile

    sel_pad = jnp.concatenate([selb, jnp.zeros((LANES - N_EXPERTS, tm), BF16)], axis=0)
    cnt_row = pad8(lax.dot_general(jnp.ones((SUBLANES, tm), BF16), sel_pad, (((1,), (1,)), ((), ())),
                                   preferred_element_type=F32))
    first_row = jnp.dot(cnt_row.astype(BF16), tri(LANES, True), preferred_element_type=F32)
    prev_row = carry_row_ref[...]
    meta = jnp.concatenate([cnt_row[0:1], first_row[0:1], prev_row[0:1], jnp.zeros((SUBLANES - 3, LANES), F32)], 0)
    meta_ref[0] = meta.astype(jnp.int32)
    carry_row_ref[...] = prev_row + cnt_row
    carry_ref[...] = carry_ref[...] + cnt_col

    slot_rows, w_rows = [], []
    for k in range(TOP_K):
        hit = (rank == k) & sel
        slot_rows.append(jnp.sum(jnp.where(hit, slot, 0.0), axis=0, keepdims=True))
        w_rows.append(jnp.sum(jnp.where(hit, w, 0.0), axis=0, keepdims=True))
    slot_rows = [jnp.where(valid, r, -1.0) for r in slot_rows]
    pad2 = jnp.zeros((ROUTE_ROWS - TOP_K, tm), F32)
    slot_ref[...] = jnp.concatenate(slot_rows + [pad2 - 1.0], 0).astype(jnp.int32)
    info = jnp.concatenate(w_rows + [pad2] + slot_rows + [jnp.zeros((LANES - ROUTE_ROWS - TOP_K, tm), F32)], 0)
    tokinfo_ref[...] = info.T

    @pl.when(i == pl.num_programs(0) - 1)
    def _():
        cnt_ref[...] = jnp.broadcast_to(carry_ref[:, 0:1], cnt_ref.shape)


def _route(h_main, h_tail, router_wt, router_bias_col, n_valid):
    tm = MOE_TOKENS
    n_main = h_main.shape[0] // tm
    n = h_main.shape[0] + tm
    const = lambda a: pl.BlockSpec(a.shape, lambda i: (0,) * a.ndim)
    return pl.pallas_call(
        functools.partial(_route_kernel, n_valid=n_valid, n_main=n_main),
        grid=(n // tm,),
        in_specs=[*_token_tile_specs(n_main), const(router_wt), const(router_bias_col)],
        out_specs=[pl.BlockSpec((ROUTE_ROWS, tm), lambda i: (0, i)),
                   pl.BlockSpec((tm, LANES), lambda i: (i, 0)),
                   pl.BlockSpec((1, SUBLANES, LANES), lambda i: (i, 0, 0)),
                   pl.BlockSpec((N_EXPERTS, LANES), lambda i: (0, 0))],
        out_shape=[jax.ShapeDtypeStruct((ROUTE_ROWS, n), jnp.int32),
                   jax.ShapeDtypeStruct((n, LANES), F32),
                   jax.ShapeDtypeStruct((n // tm, SUBLANES, LANES), jnp.int32),
                   jax.ShapeDtypeStruct((N_EXPERTS, LANES), F32)],
        scratch_shapes=[pltpu.VMEM((N_EXPERTS, LANES), F32), pltpu.VMEM((SUBLANES, LANES), F32)],
        compiler_params=_cparams(("arbitrary",)),
        name="moe_route",
    )(h_main, h_tail, router_wt, router_bias_col)


TILE_SLOTS = MOE_TOKENS * TOP_K + N_EXPERTS * SUBLANES
RUN_CHUNKS = tuple(1 << b for b in range(int(math.log2(MOE_TOKENS)), int(math.log2(SUBLANES)) - 1, -1))


def _run_copy(src_ref, src_row, dst_ref, dst_row, rows, sem):
    return pltpu.make_async_copy(src_ref.at[pl.ds(pl.multiple_of(src_row, SUBLANES), rows)],
                                 dst_ref.at[pl.ds(pl.multiple_of(dst_row, SUBLANES), rows)], sem)


def _start_run(src_ref, src_row, dst_ref, dst_row, n, sem, started):
    off = jnp.int32(0)
    out = []
    for c, rows in enumerate(RUN_CHUNKS):
        take = (n & rows) != 0

        @pl.when(take)
        def _(off=off, rows=rows):
            _run_copy(src_ref, src_row + off, dst_ref, dst_row + off, rows, sem).start()

        inc = take.astype(jnp.int32)
        off = off + inc * rows
        out.append(started[c] + inc)
    return tuple(out)


def _wait_runs(src_ref, dst_ref, sem, started):
    for c, rows in enumerate(RUN_CHUNKS):
        def wait_one(j, carry, rows=rows):
            _run_copy(src_ref, 0, dst_ref, 0, rows, sem).wait()
            return carry

        lax.fori_loop(0, started[c], wait_one, 0)


def _dispatch_kernel(start_ref, cnt_ref, meta_ref, slot_ref, x_ref, xt_ref, xs_ref, sorted_ref, zero_ref, started_ref,
                     sem, zsem, *, cap, n_main):
    i = pl.program_id(0)
    tm = MOE_TOKENS

    @pl.when(i == 0)
    def _():
        zero_ref[...] = jnp.zeros_like(zero_ref)

        def fill_expert(e, started):
            lo = start_ref[e] + cnt_ref[e]
            hi = jnp.where(e == N_EXPERTS - 1, cap, start_ref[jnp.minimum(e + 1, N_EXPERTS - 1)])
            n_full = (hi - lo) // tm

            def fill_full(j, st):
                return _start_run(zero_ref, 0, xs_ref, lo + j * tm, jnp.int32(tm), zsem, st)

            started = lax.fori_loop(0, n_full, fill_full, started)
            return _start_run(zero_ref, 0, xs_ref, lo + n_full * tm, (hi - lo) - n_full * tm, zsem, started)

        filled = lax.fori_loop(0, N_EXPERTS, fill_expert, tuple(jnp.int32(0) for _ in RUN_CHUNKS))
        _wait_runs(zero_ref, xs_ref, zsem, filled)

    slot = lax.rem(i, 2)
    srow = lax.broadcasted_iota(jnp.int32, (TILE_SLOTS, tm), 0)
    onehot = srow == slot_ref[0:1, :]
    for k in range(1, TOP_K):
        onehot = onehot | (srow == slot_ref[k:k + 1, :])
    sorted_ref[slot] = jnp.dot(onehot.astype(BF16), _token_tile(i, n_main, x_ref, xt_ref).astype(BF16),
                               preferred_element_type=F32)

    def copy_expert(e, started):
        n = meta_ref[0, 0, e]
        return _start_run(sorted_ref.at[slot], meta_ref[0, 1, e], xs_ref, start_ref[e] + meta_ref[0, 2, e], n,
                          sem.at[slot], started)

    started = lax.fori_loop(0, N_EXPERTS, copy_expert, tuple(jnp.int32(0) for _ in RUN_CHUNKS))
    for c in range(len(RUN_CHUNKS)):
        started_ref[slot, c] = started[c]

    def wait_tile(s):
        _wait_runs(sorted_ref.at[s], xs_ref, sem.at[s], tuple(started_ref[s, c] for c in range(len(RUN_CHUNKS))))

    @pl.when(i > 0)
    def _():
        wait_tile(1 - slot)

    @pl.when(i == pl.num_programs(0) - 1)
    def _():
        wait_tile(slot)


def _dispatch(h_main, h_tail, slot_t, meta, seg_start, counts, cap):
    tm = MOE_TOKENS
    n_main = h_main.shape[0] // tm
    return pl.pallas_call(
        functools.partial(_dispatch_kernel, cap=cap, n_main=n_main),
        grid_spec=pltpu.PrefetchScalarGridSpec(
            num_scalar_prefetch=2,
            grid=(n_main + 1,),
            in_specs=[pl.BlockSpec((1, SUBLANES, LANES), lambda i, *_: (i, 0, 0), memory_space=pltpu.SMEM),
                      pl.BlockSpec((ROUTE_ROWS, tm), lambda i, *_: (0, i)),
                      *_token_tile_specs(n_main)],
            out_specs=pl.BlockSpec(memory_space=pl.ANY),
            scratch_shapes=[pltpu.VMEM((2, TILE_SLOTS, D_MODEL), F32), pltpu.VMEM((tm, D_MODEL), F32),
                            pltpu.SMEM((2, len(RUN_CHUNKS)), jnp.int32),
                            pltpu.SemaphoreType.DMA((2,)), pltpu.SemaphoreType.DMA]),
        out_shape=jax.ShapeDtypeStruct((cap, D_MODEL), F32),
        compiler_params=_cparams(("arbitrary",)),
        name="moe_dispatch",
    )(seg_start, counts, meta, slot_t, h_main, h_tail)


def _swiglu(x, wg, wu, wd):
    xb = x.astype(BF16)
    g = jnp.dot(xb, wg.astype(BF16), preferred_element_type=F32)
    u = jnp.dot(xb, wu.astype(BF16), preferred_element_type=F32)
    return jnp.dot((_silu(g) * u).astype(BF16), wd.astype(BF16), preferred_element_type=F32)


EXPERT_RING = 3


def _experts_kernel(be_ref, used_ref, xs_ref, wg_ref, wu_ref, wd_ref, y_ref, xbuf, sems):
    i = pl.program_id(0)
    n = pl.num_programs(0)

    def block_copy(blk):
        slot = lax.rem(blk, EXPERT_RING)
        rows = pl.ds(pl.multiple_of(blk * MOE_BLOCK, MOE_BLOCK), MOE_BLOCK)
        return pltpu.make_async_copy(xs_ref.at[rows], xbuf.at[slot], sems.at[slot])

    @pl.when(i == 0)
    def _():
        for j in range(EXPERT_RING - 1):
            block_copy(jnp.int32(j)).start()

    @pl.when(i + EXPERT_RING - 1 < n)
    def _():
        block_copy(i + EXPERT_RING - 1).start()

    block_copy(i).wait()

    @pl.when(i < used_ref[0])
    def _():
        y_ref[...] = _swiglu(xbuf[lax.rem(i, EXPERT_RING)], wg_ref[0], wu_ref[0], wd_ref[0])

    @pl.when(i >= used_ref[0])
    def _():
        y_ref[...] = jnp.zeros_like(y_ref)


def _experts(xs, block_expert, used_blocks, w_gate, w_up, w_down):
    cap = xs.shape[0]
    return pl.pallas_call(
        _experts_kernel,
        grid_spec=pltpu.PrefetchScalarGridSpec(
            num_scalar_prefetch=2,
            grid=(cap // MOE_BLOCK,),
            in_specs=[pl.BlockSpec(memory_space=pl.ANY),
                      pl.BlockSpec((1, D_MODEL, D_EXPERT), lambda i, be, used: (be[i], 0, 0)),
                      pl.BlockSpec((1, D_MODEL, D_EXPERT), lambda i, be, used: (be[i], 0, 0)),
                      pl.BlockSpec((1, D_EXPERT, D_MODEL), lambda i, be, used: (be[i], 0, 0))],
            out_specs=pl.BlockSpec((MOE_BLOCK, D_MODEL), lambda i, be, used: (i, 0)),
            scratch_shapes=[pltpu.VMEM((EXPERT_RING, MOE_BLOCK, D_MODEL), F32),
                            pltpu.SemaphoreType.DMA((EXPERT_RING,))]),
        out_shape=jax.ShapeDtypeStruct((cap, D_MODEL), F32),
        compiler_params=_cparams(("arbitrary",)),
        name="moe_experts",
    )(block_expert, used_blocks, xs, w_gate, w_up, w_down)


def _combine_kernel(start_ref, meta_ref, next_meta_ref, h_ref, ht_ref, info_ref, sg_ref, su_ref, sd_ref, g_ref, b_ref,
                    ys_ref, o_ref, ot_ref, buf_ref, started_ref, sem, *, n_main):
    i = pl.program_id(0)
    n_steps = pl.num_programs(0)
    tm = MOE_TOKENS
    slot = lax.rem(i, 2)

    def fetch_tile(meta, s):
        def fetch_expert(e, started):
            return _start_run(ys_ref, start_ref[e] + meta[0, 2, e], buf_ref.at[s], meta[0, 1, e], meta[0, 0, e],
                              sem.at[s], started)

        started = lax.fori_loop(0, N_EXPERTS, fetch_expert, tuple(jnp.int32(0) for _ in RUN_CHUNKS))
        for c in range(len(RUN_CHUNKS)):
            started_ref[s, c] = started[c]

    @pl.when(i == 0)
    def _():
        buf_ref[...] = jnp.zeros_like(buf_ref)
        fetch_tile(meta_ref, slot)

    @pl.when(i + 1 < n_steps)
    def _():
        fetch_tile(next_meta_ref, 1 - slot)

    h = _token_tile(i, n_main, h_ref, ht_ref)
    f = _swiglu(h, sg_ref[...], su_ref[...], sd_ref[...])
    info = info_ref[...]
    scol = lax.broadcasted_iota(jnp.int32, (tm, TILE_SLOTS), 1).astype(F32)
    mix = jnp.zeros((tm, TILE_SLOTS), F32)
    for k in range(TOP_K):
        mix = mix + jnp.where(info[:, ROUTE_ROWS + k:ROUTE_ROWS + k + 1] == scol, info[:, k:k + 1], 0.0)
    _wait_runs(ys_ref, buf_ref.at[slot], sem.at[slot], tuple(started_ref[slot, c] for c in range(len(RUN_CHUNKS))))
    acc = jnp.dot(mix.astype(BF16), buf_ref[slot].astype(BF16), preferred_element_type=F32)
    out = _layer_norm(DEEPNORM_ALPHA * h + (acc + f), g_ref[...], b_ref[...])

    @pl.when(i < n_main)
    def _():
        o_ref[...] = out

    @pl.when(i >= n_main)
    def _():
        ot_ref[...] = out


def _combine(h_main, h_tail, ys, meta, tokinfo, seg_start, sh_gate, sh_up, sh_down, ln_g, ln_b):
    tm = MOE_TOKENS
    n_main = h_main.shape[0] // tm
    const = lambda a: pl.BlockSpec(a.shape, lambda i, *_: (0,) * a.ndim)
    return pl.pallas_call(
        functools.partial(_combine_kernel, n_main=n_main),
        grid_spec=pltpu.PrefetchScalarGridSpec(
            num_scalar_prefetch=1,
            grid=(n_main + 1,),
            in_specs=[pl.BlockSpec((1, SUBLANES, LANES), lambda i, *_: (i, 0, 0), memory_space=pltpu.SMEM),
                      pl.BlockSpec((1, SUBLANES, LANES), lambda i, *_: (jnp.minimum(i + 1, n_main), 0, 0),
                                   memory_space=pltpu.SMEM),
                      *_token_tile_specs(n_main),
                      pl.BlockSpec((tm, LANES), lambda i, *_: (i, 0)),
                      const(sh_gate), const(sh_up), const(sh_down), const(ln_g), const(ln_b),
                      pl.BlockSpec(memory_space=pl.ANY)],
            out_specs=list(_token_tile_specs(n_main)),
            scratch_shapes=[pltpu.VMEM((2, TILE_SLOTS, D_MODEL), F32), pltpu.SMEM((2, len(RUN_CHUNKS)), jnp.int32),
                            pltpu.SemaphoreType.DMA((2,))]),
        out_shape=[jax.ShapeDtypeStruct(h_main.shape, F32), jax.ShapeDtypeStruct((tm, D_MODEL), F32)],
        compiler_params=_cparams(("arbitrary",)),
        name="moe_combine",
    )(seg_start, meta, meta, h_main, h_tail, tokinfo, sh_gate, sh_up, sh_down, ln_g, ln_b, ys)


def _moe_ln(h_main, h_tail, n_valid, router_w, router_bias, w_gate, w_up, w_down, sh_gate, sh_up, sh_down, ln_g, ln_b):
    n_tiles = h_main.shape[0] // MOE_TOKENS + 1
    slot_t, tokinfo, meta, cnt = _route(h_main, h_tail, router_w.T.astype(BF16),
                                        jnp.broadcast_to(router_bias.astype(F32)[:, None], (N_EXPERTS, LANES)), n_valid)
    counts = cnt[:, 0].astype(jnp.int32)
    padded = (counts + MOE_BLOCK - 1) // MOE_BLOCK * MOE_BLOCK
    seg_end = jnp.cumsum(padded)
    seg_start = seg_end - padded
    run_pad = n_tiles * N_EXPERTS * (SUBLANES - 1)
    n_blocks = -(-(n_valid * TOP_K + run_pad + N_EXPERTS * (MOE_BLOCK - 1)) // MOE_BLOCK)
    cap = n_blocks * MOE_BLOCK
    block_first_row = jnp.arange(n_blocks, dtype=jnp.int32) * MOE_BLOCK
    block_expert = jnp.minimum(jnp.sum((seg_end[None, :] <= block_first_row[:, None]).astype(jnp.int32), axis=1),
                               N_EXPERTS - 1)
    xs = _dispatch(h_main, h_tail, slot_t, meta, seg_start, counts, cap)
    used_blocks = (seg_end[N_EXPERTS - 1:] // MOE_BLOCK).astype(jnp.int32)
    ys = _experts(xs, block_expert, used_blocks, w_gate, w_up, w_down)
    return _combine(h_main, h_tail, ys, meta, tokinfo, seg_start, sh_gate.astype(BF16), sh_up.astype(BF16),
                    sh_down.astype(BF16), ln_g, ln_b)


def _ssd_sample_kernel(xbc_ref, z_ref, dtg_ref, sconv_ref, s0_ref, convw_ref, convb_ref, dtb_ref, alog_ref,
                       dskip_ref, normw_ref, y_ref, s_ref, conv_out_ref, xc_ref, dt_ref, da_ref):
    b = pl.program_id(0)

    @pl.when(b == 0)
    def _():
        xin = xbc_ref[...]
        xc = convw_ref[SSD_CONV - 1:SSD_CONV, :] * xin
        for k in range(SSD_CONV - 1):
            xc = xc + convw_ref[k:k + 1, :] * sconv_ref[k]
        xc_ref[...] = _silu(xc + convb_ref[...])
        dt = _softplus(dtg_ref[...] + dtb_ref[...])
        dt_ref[...] = dt
        da_ref[...] = jnp.exp(dt * (-jnp.exp(alog_ref[...])))
        for k in range(SSD_CONV - 2):
            conv_out_ref[k] = sconv_ref[k + 1]
        conv_out_ref[SSD_CONV - 2] = xin

    xc = xc_ref[pl.ds(b, 1), :]
    dt = dt_ref[pl.ds(b, 1), :]
    da = da_ref[pl.ds(b, 1), :]
    ns = SSD_GROUPS * SSD_STATE
    eye = (lax.broadcasted_iota(jnp.int32, (HEAD_DIM, HEAD_DIM), 0)
           == lax.broadcasted_iota(jnp.int32, (HEAD_DIM, HEAD_DIM), 1))
    hpg = SSD_HEADS // SSD_GROUPS
    y_parts = []
    for h in range(SSD_HEADS):
        g = h // hpg
        x_h = xc[:, h * HEAD_DIM:(h + 1) * HEAD_DIM]
        b_g = xc[:, SSD_D + g * SSD_STATE:SSD_D + (g + 1) * SSD_STATE]
        c_g = xc[:, SSD_D + ns + g * SSD_STATE:SSD_D + ns + (g + 1) * SSD_STATE]
        xdt_col = jnp.sum(jnp.where(eye, x_h * dt[:, h:h + 1], 0.0), axis=1, keepdims=True)
        s_new = da[:, h:h + 1] * s0_ref[0, h] + xdt_col * b_g
        s_ref[0, h] = s_new
        y_h = _bdot_nt(c_g, s_new) + dskip_ref[:, h * HEAD_DIM:(h + 1) * HEAD_DIM] * x_h
        y_parts.append(y_h)
    y = jnp.concatenate(y_parts, axis=1)
    y_ref[pl.ds(b, 1), :] = _gated_group_norm(y, z_ref[pl.ds(b, 1), :], normw_ref[...])


def _ssd_sample(xbc, z, dtg, state_conv_t, state_ssm, conv_w, conv_b, dt_bias_pad, a_log_pad, d_skip_full, norm_w):
    bs = xbc.shape[0]
    const = lambda a: pl.BlockSpec(a.shape, lambda b: (0,) * a.ndim)
    state_spec = pl.BlockSpec((1, SSD_HEADS, HEAD_DIM, SSD_STATE), lambda b: (b, 0, 0, 0))
    return pl.pallas_call(
        _ssd_sample_kernel,
        grid=(bs,),
        in_specs=[const(xbc), const(z), const(dtg), const(state_conv_t), state_spec, const(conv_w), const(conv_b),
                  const(dt_bias_pad), const(a_log_pad), const(d_skip_full), const(norm_w)],
        out_specs=[pl.BlockSpec((bs, SSD_D), lambda b: (0, 0)), state_spec,
                   pl.BlockSpec((SSD_CONV - 1, bs, SSD_CONV_CH), lambda b: (0, 0, 0))],
        out_shape=[jax.ShapeDtypeStruct((bs, SSD_D), F32),
                   jax.ShapeDtypeStruct(state_ssm.shape, F32),
                   jax.ShapeDtypeStruct((SSD_CONV - 1, bs, SSD_CONV_CH), F32)],
        scratch_shapes=[pltpu.VMEM((bs, SSD_CONV_CH), F32), pltpu.VMEM((bs, LANES), F32),
                        pltpu.VMEM((bs, LANES), F32)],
        compiler_params=_cparams(("arbitrary",)),
        name="ssd_sample",
    )(xbc, z, dtg, state_conv_t, state_ssm, conv_w, conv_b, dt_bias_pad, a_log_pad, d_skip_full, norm_w)


SEL_PAST = TOP_N - 1
BLOCKS_PER_PAGE = PAGE_SIZE // CMP_BLOCK
KV_FEATS = 2 * KV_D


def _compress_consts_t(cmp_pe, cmp_w1, cmp_b1, cmp_w2, cmp_b2):
    pe_t = jnp.stack([jnp.tile(cmp_pe[k].T, (1, BLOCKS_PER_PAGE)) for k in range(2)])
    w1_t = jnp.stack([_block_diag2(jnp.swapaxes(cmp_w1[k], 0, 1)) for k in range(2)]).astype(BF16)
    b1_t = jnp.stack([jnp.tile(cmp_b1[k], BLOCKS_PER_PAGE) for k in range(2)])[:, None, :]
    w2_t = jnp.stack([_block_diag2(cmp_w2[k]) for k in range(2)]).astype(BF16)
    b2_t = jnp.stack([jnp.tile(cmp_b2[k], BLOCKS_PER_PAGE) for k in range(2)])[:, None, :]
    return pe_t, w1_t, b1_t, w2_t, b2_t


def _compress_pages_kernel(pt_ref, pe_ref, w1_ref, b1_ref, w2_ref, b2_ref, pool_ref, o_ref, kbuf, vbuf, sems, *,
                           n_pages):
    b = pl.program_id(0)
    nb = pl.num_programs(0)
    bufs = (kbuf, vbuf)

    def half_copy(seq, kind, p):
        return pltpu.make_async_copy(pool_ref.at[pt_ref[seq, p], pl.ds(kind * KV_D, KV_D)],
                                     bufs[kind].at[pl.ds(pl.multiple_of(p * KV_D, KV_D), KV_D)], sems.at[kind])

    def start_half(seq, kind):
        lax.fori_loop(0, n_pages, lambda p, c: (half_copy(seq, kind, p).start(), c)[1], 0)

    def wait_half(seq, kind):
        lax.fori_loop(0, n_pages, lambda p, c: (half_copy(seq, kind, p).wait(), c)[1], 0)

    @pl.when(b == 0)
    def _():
        start_half(b, 0)
        start_half(b, 1)

    for kind in range(2):
        wait_half(b, kind)
        for h in range(NSA_KV_HEADS):
            def add_feature(d, acc, kind=kind, h=h):
                x = bufs[kind][pl.ds(h * HEAD_DIM + d, n_pages, stride=KV_D), :] + pe_ref[kind, pl.ds(d, 1), :]
                return acc + jnp.dot(x.astype(BF16), w1_ref[kind, d], preferred_element_type=F32)

            acc = lax.fori_loop(0, HEAD_DIM, add_feature,
                                jnp.zeros((n_pages, BLOCKS_PER_PAGE * CMP_HIDDEN), F32), unroll=8)
            hid = _silu(acc + b1_ref[kind])
            o_ref[0, kind * NSA_KV_HEADS + h] = (
                jnp.dot(hid.astype(BF16), w2_ref[kind], preferred_element_type=F32) + b2_ref[kind])

        @pl.when(b + 1 < nb)
        def _(kind=kind):
            start_half(b + 1, kind)


def _compress_pages(pool_t, page_table, consts):
    bs, n_pages = page_table.shape
    const = lambda a: pl.BlockSpec(a.shape, lambda b, pt: (0,) * a.ndim)
    return pl.pallas_call(
        functools.partial(_compress_pages_kernel, n_pages=n_pages),
        grid_spec=pltpu.PrefetchScalarGridSpec(
            num_scalar_prefetch=1,
            grid=(bs,),
            in_specs=[const(a) for a in consts] + [pl.BlockSpec(memory_space=pl.ANY)],
            out_specs=pl.BlockSpec((1, 2 * NSA_KV_HEADS, n_pages, LANES), lambda b, pt: (b, 0, 0, 0)),
            scratch_shapes=[pltpu.VMEM((n_pages * KV_D, PAGE_SIZE), F32), pltpu.VMEM((n_pages * KV_D, PAGE_SIZE), F32),
                            pltpu.SemaphoreType.DMA((2,))]),
        out_shape=jax.ShapeDtypeStruct((bs, 2 * NSA_KV_HEADS, n_pages, LANES), F32),
        compiler_params=_cparams(("arbitrary",)),
        name="compress_pages",
    )(page_table, *consts, pool_t)


def _group_heads(q_row, hk):
    hpg = NSA_HEADS // NSA_KV_HEADS
    low = lax.broadcasted_iota(jnp.int32, (1, LANES), 1) < HEAD_DIM
    rows = []
    for r in range(hpg):
        head = hk * hpg + r
        tile = q_row[:, (head // 2) * LANES:(head // 2 + 1) * LANES]
        if head % 2 == 1:
            tile = pltpu.roll(tile, HEAD_DIM, 1)
        rows.append(jnp.where(low, tile, 0.0))
    return jnp.concatenate(rows + [jnp.zeros((SUBLANES - hpg, LANES), F32)], axis=0)


def _spread_heads(o_groups):
    hpg = NSA_HEADS // NSA_KV_HEADS
    return jnp.concatenate([o[r:r + 1, 0:HEAD_DIM] for o in o_groups for r in range(hpg)], axis=1)


def _nsa_sample_cmp_t_kernel(qc_ref, cmp_ref, ocmp_ref, idx_ref, *, n_pages):
    b = pl.program_id(0)
    nc = n_pages * BLOCKS_PER_PAGE
    scale = HEAD_DIM ** -0.5
    hpg = NSA_HEADS // NSA_KV_HEADS
    q_row = qc_ref[pl.ds(b, 1), :] * scale
    lane = lax.broadcasted_iota(jnp.int32, (1, LANES), 1)
    pos_r = lax.broadcasted_iota(jnp.int32, (1, nc), 1)
    bid_r = (pos_r % n_pages) * BLOCKS_PER_PAGE + pos_r // n_pages
    pos_c = lax.broadcasted_iota(jnp.int32, (nc, 1), 0)
    bid_c = (pos_c % n_pages) * BLOCKS_PER_PAGE + pos_c // n_pages
    o_groups = []
    for hk in range(NSA_KV_HEADS):
        kc = cmp_ref[0, hk].astype(BF16)
        vc = cmp_ref[0, NSA_KV_HEADS + hk].astype(BF16)
        qh = _group_heads(q_row, hk)
        s = jnp.concatenate(
            [lax.dot_general(pltpu.roll(qh, c * HEAD_DIM, 1).astype(BF16) if c else qh.astype(BF16), kc,
                             (((1,), (1,)), ((), ())), preferred_element_type=F32)
             for c in range(BLOCKS_PER_PAGE)], axis=1)
        ex = jnp.exp(s - jnp.max(s, axis=-1, keepdims=True))
        p = ex / jnp.sum(ex, axis=-1, keepdims=True)
        o = jnp.dot(p[:, 0:n_pages].astype(BF16), vc, preferred_element_type=F32)
        for c in range(1, BLOCKS_PER_PAGE):
            oc = jnp.dot(p[:, c * n_pages:(c + 1) * n_pages].astype(BF16), vc, preferred_element_type=F32)
            o = o + pltpu.roll(oc, LANES - c * HEAD_DIM, 1)
        o_groups.append(o)
        hrow = lax.broadcasted_iota(jnp.int32, p.shape, 0) < hpg
        imp = jnp.sum(jnp.where(hrow, p, 0.0), axis=0, keepdims=True)
        score = jnp.where((bid_r == 0) | (bid_r == nc - 1), FORCED_SCORE, imp)
        score_col = jnp.concatenate([score, jnp.zeros((LANES - 1, nc), F32)], 0).T[:, 0:1]
        beats = (score_col > score) | ((score_col == score) & (bid_c < bid_r))
        rank = jnp.sum(beats.astype(F32), axis=0, keepdims=True)
        row = jnp.zeros((1, LANES), F32)
        bid_f = bid_r.astype(F32)
        for k in range(SEL_PAST):
            blk = jnp.sum(jnp.where(rank == k, bid_f, 0.0), axis=1, keepdims=True)
            row = jnp.where(lane == k, blk, row)
        idx_ref[pl.ds(b * NSA_KV_HEADS + hk, 1), :] = row.astype(jnp.int32)
    ocmp_ref[pl.ds(b, 1), :] = _spread_heads(o_groups)


def _nsa_sample_cmp_t(qc, kvcmp_t):
    bs, _, n_pages, _ = kvcmp_t.shape
    return pl.pallas_call(
        functools.partial(_nsa_sample_cmp_t_kernel, n_pages=n_pages),
        grid=(bs,),
        in_specs=[pl.BlockSpec((bs, NSA_D), lambda b: (0, 0)),
                  pl.BlockSpec((1, 2 * NSA_KV_HEADS, n_pages, LANES), lambda b: (b, 0, 0, 0))],
        out_specs=[pl.BlockSpec((bs, NSA_D), lambda b: (0, 0)),
                   pl.BlockSpec((bs * NSA_KV_HEADS, LANES), lambda b: (0, 0))],
        out_shape=[jax.ShapeDtypeStruct((bs, NSA_D), F32),
                   jax.ShapeDtypeStruct((bs * NSA_KV_HEADS, LANES), jnp.int32)],
        compiler_params=_cparams(("arbitrary",)),
        name="nsa_sample_cmp",
    )(qc, kvcmp_t)


def _sel_block_copies(pool_ref, pt_ref, sel_ref, kbuf, vbuf, sem, b, hk, k):
    blk = sel_ref[b * NSA_KV_HEADS + hk, k]
    page = pt_ref[b, lax.shift_right_logical(blk, int(math.log2(BLOCKS_PER_PAGE)))]
    j = hk * SEL_PAST + k
    return (pltpu.make_async_copy(pool_ref.at[page, pl.ds(hk * HEAD_DIM, HEAD_DIM)], kbuf.at[j], sem),
            pltpu.make_async_copy(pool_ref.at[page, pl.ds(KV_D + hk * HEAD_DIM, HEAD_DIM)], vbuf.at[j], sem))


def _nsa_sample_attn_t_kernel(pt_ref, sel_ref, qr_ref, new_sel_ref, new_win_ref, win_ref, dtg_ref, ocmp_ref,
                              pool_ref, o_ref, kbuf, vbuf, sem):
    b = pl.program_id(0)
    for hk in range(NSA_KV_HEADS):
        for k in range(SEL_PAST):
            for cp in _sel_block_copies(pool_ref, pt_ref, sel_ref, kbuf, vbuf, sem, b, hk, k):
                cp.start()
    for hk in range(NSA_KV_HEADS):
        for k in range(SEL_PAST):
            for cp in _sel_block_copies(pool_ref, pt_ref, sel_ref, kbuf, vbuf, sem, b, hk, k):
                cp.wait()
    scale = HEAD_DIM ** -0.5
    q_row = qr_ref[pl.ds(b, 1), :] * scale
    sig = _sigmoid(dtg_ref[pl.ds(b, 1), :])
    lane = lax.broadcasted_iota(jnp.int32, (1, PAGE_SIZE), 1)
    o_slc, o_win = [], []
    for hk in range(NSA_KV_HEADS):
        qh = _group_heads(q_row, hk)[:, 0:HEAD_DIM].astype(BF16)

        def new_row(ref, kind):
            t = ref[pl.ds(b, 1), :][:, kind * KV_D:(kind + 1) * KV_D]
            if hk == 1:
                t = pltpu.roll(t, HEAD_DIM, 1)
            return t[:, 0:HEAD_DIM].astype(BF16).astype(F32)

        def attend(kt, vt, mask, new_ref, n_new):
            s = jnp.dot(qh, kt.astype(BF16), preferred_element_type=F32)
            if mask is not None:
                s = jnp.where(mask, s, NEG)
            s_new = jnp.sum(qh.astype(F32) * new_row(new_ref, 0), axis=1, keepdims=True)
            m = jnp.maximum(jnp.max(s, axis=-1, keepdims=True), s_new)
            ex = jnp.exp(s - m)
            ex_new = jnp.exp(s_new - m) * n_new
            den = jnp.sum(ex, axis=-1, keepdims=True) + ex_new
            o = lax.dot_general((ex / den).astype(BF16), vt.astype(BF16), (((1,), (1,)), ((), ())),
                                preferred_element_type=F32)
            return o + (ex_new / den).astype(BF16).astype(F32) * new_row(new_ref, 1)

        kt = jnp.concatenate([kbuf[hk * SEL_PAST + k] for k in range(SEL_PAST)], axis=1)
        vt = jnp.concatenate([vbuf[hk * SEL_PAST + k] for k in range(SEL_PAST)], axis=1)
        mask = jnp.concatenate(
            [lane // SEL_BLOCK == (sel_ref[b * NSA_KV_HEADS + hk, k] & (BLOCKS_PER_PAGE - 1))
             for k in range(SEL_PAST)], axis=1)
        o_slc.append(attend(kt, vt, mask, new_sel_ref, float(SEL_BLOCK)))
        o_win.append(attend(win_ref[0, hk * HEAD_DIM:(hk + 1) * HEAD_DIM, :],
                            win_ref[0, KV_D + hk * HEAD_DIM:KV_D + (hk + 1) * HEAD_DIM, :], None, new_win_ref, 1.0))
    gates = []
    for br in range(3):
        gates.append(jnp.concatenate(
            [jnp.broadcast_to(sig[:, GATE_COL0 + h * 3 + br:GATE_COL0 + h * 3 + br + 1], (1, HEAD_DIM))
             for h in range(NSA_HEADS)], axis=1))
    o_ref[pl.ds(b, 1), :] = (gates[0] * ocmp_ref[pl.ds(b, 1), :] + gates[1] * _spread_heads(o_slc)
                             + gates[2] * _spread_heads(o_win))


def _nsa_sample_attn_t(qr, new_sel, new_win, win_t, dtg, o_cmp, pool_sel_t, page_table, sel_idx):
    bs = qr.shape[0]
    const = lambda a: pl.BlockSpec(a.shape, lambda b, pt, sel: (0,) * a.ndim)
    n_buf = NSA_KV_HEADS * SEL_PAST
    return pl.pallas_call(
        _nsa_sample_attn_t_kernel,
        grid_spec=pltpu.PrefetchScalarGridSpec(
            num_scalar_prefetch=2,
            grid=(bs,),
            in_specs=[const(qr), const(new_sel), const(new_win),
                      pl.BlockSpec((1,) + win_t.shape[1:], lambda b, pt, sel: (b, 0, 0)),
                      const(dtg), const(o_cmp), pl.BlockSpec(memory_space=pl.ANY)],
            out_specs=pl.BlockSpec((bs, NSA_D), lambda b, pt, sel: (0, 0)),
            scratch_shapes=[pltpu.VMEM((n_buf, HEAD_DIM, PAGE_SIZE), F32), pltpu.VMEM((n_buf, HEAD_DIM, PAGE_SIZE), F32),
                            pltpu.SemaphoreType.DMA]),
        out_shape=jax.ShapeDtypeStruct((bs, NSA_D), F32),
        compiler_params=_cparams(("arbitrary",)),
        name="nsa_sample_attn",
    )(page_table, sel_idx, qr, new_sel, new_win, win_t, dtg, o_cmp, pool_sel_t)


def kernel(x_prompt, x_sample, cache_kv_cmp, cache_kv_sel, page_table, cache_kv_win, state_ssm, state_conv,
           emb_ln_g, emb_ln_b, w_in, conv_w, conv_b, dt_bias, a_log, d_skip, ssd_norm_w,
           cmp_pe, cmp_w1, cmp_b1, cmp_w2, cmp_b2, w_out, ln1_g, ln1_b,
           router_w, router_bias, exp_w_gate, exp_w_up, exp_w_down,
           sh_w_gate, sh_w_up, sh_w_down, ln2_g, ln2_b):
    bp, tp, _ = x_prompt.shape
    bs, ts, _ = x_sample.shape
    assert ts == 1 and DEPTH == 1
    n_prompt = bp * tp
    past_len = page_table.shape[1] * PAGE_SIZE
    l = 0
    w_perm = _permute_w_in(w_in[l])
    ln0_g, ln0_b = emb_ln_g[None], emb_ln_b[None]
    ssd_consts = (conv_w[l], conv_b[l][None], _pad_lanes(dt_bias[l]), _pad_lanes(a_log[l]),
                  jnp.repeat(d_skip[l], HEAD_DIM)[None], ssd_norm_w[l][None])
    cmp_consts = _compress_consts(cmp_pe[l], cmp_w1[l], cmp_b1[l], cmp_w2[l], cmp_b2[l])
    w_o = w_out[l].astype(BF16)
    w_o_ssd, w_o_nsa = w_o[:SSD_D], w_o[SSD_D:]
    ln1 = (ln1_g[l][None], ln1_b[l][None])
    kv_shape = (2, NSA_KV_HEADS, HEAD_DIM)

    hp, z, xbc, qc, qr, kvc, kvs, kvw, dtg = _inproj(
        x_prompt.reshape(n_prompt, D_MODEL), ln0_g, ln0_b, w_perm, _rope_tables(jnp.arange(tp)), 256)
    y_ssd, ssm_p, conv_p = _ssd_prompt(xbc, z, dtg, *ssd_consts, bp, tp)
    kvcmp = _compress_prompt(kvc, cmp_consts, tp)
    y_nsa = _nsa_prompt(qc, qr, dtg, kvcmp, kvs, kvw, bp, tp)
    h1p = _outproj(y_ssd, y_nsa, hp, w_o_ssd, w_o_nsa, *ln1, 256)
    n_keep = min(WINDOW, tp)
    kvc_p = kvc.reshape((1, bp, tp) + kv_shape)
    kvs_p = kvs.reshape((1, bp, tp) + kv_shape)
    kvw_p = kvw.reshape((1, bp, tp) + kv_shape)[:, :, tp - n_keep:]

    s_hs, s_z, s_xbc, s_qc, s_qr, s_kvc, s_kvs, s_kvw, s_dtg = _inproj(
        x_sample.reshape(bs, D_MODEL), ln0_g, ln0_b, w_perm, _rope_tables(jnp.full((bs,), past_len)), bs)
    s_y_ssd, ssm_s, conv_s_t = _ssd_sample(s_xbc, s_z, s_dtg, jnp.swapaxes(state_conv[l], 0, 1), state_ssm[l],
                                           *ssd_consts)
    n_pool = cache_kv_cmp.shape[1]
    feature_major = lambda c, rows: jnp.swapaxes(c.reshape(-1, rows, 2 * KV_D), 1, 2)
    s_kvcmp = _compress_pages(feature_major(cache_kv_cmp[l], PAGE_SIZE), page_table,
                              _compress_consts_t(cmp_pe[l], cmp_w1[l], cmp_b1[l], cmp_w2[l], cmp_b2[l]))
    s_o_cmp, s_sel = _nsa_sample_cmp_t(s_qc, s_kvcmp)
    buf_win = cache_kv_win[l].reshape(bs, -1, 2 * KV_D)
    s_y_nsa = _nsa_sample_attn_t(
        s_qr, s_kvs, s_kvw, feature_major(cache_kv_win[l], buf_win.shape[1]), s_dtg, s_o_cmp,
        feature_major(cache_kv_sel[l], PAGE_SIZE), page_table, s_sel)
    h1s = _outproj(s_y_ssd, s_y_nsa, s_hs, w_o_ssd, w_o_nsa, *ln1, bs)
    win_all = jnp.concatenate([buf_win, s_kvw[:, None, :]], 1)
    n_keep_s = min(WINDOW, past_len + ts)
    kvw_s = win_all[:, win_all.shape[1] - n_keep_s:].reshape((1, bs, n_keep_s) + kv_shape)
    kvc_s = s_kvc.reshape((1, bs, ts) + kv_shape)
    kvs_s = s_kvs.reshape((1, bs, ts) + kv_shape)

    assert n_prompt % MOE_TOKENS == 0 and bs * ts <= MOE_TOKENS
    n_tok = n_prompt + bs * ts
    tail = jnp.concatenate([h1s, jnp.zeros((MOE_TOKENS - bs * ts, D_MODEL), F32)], 0)
    out_main, out_tail = _moe_ln(h1p, tail, n_tok, router_w[l], router_bias[l], exp_w_gate[l], exp_w_up[l],
                                 exp_w_down[l], sh_w_gate[l], sh_w_up[l], sh_w_down[l], ln2_g[l][None], ln2_b[l][None])
    y_prompt = out_main.reshape(bp, tp, D_MODEL)
    y_sample = out_tail[:bs * ts].reshape(bs, ts, D_MODEL)
    return (y_prompt, y_sample, kvc_p, kvs_p, kvw_p, ssm_p[None], conv_p[None],
            kvc_s, kvs_s, kvw_s, ssm_s[None], jnp.swapaxes(conv_s_t, 0, 1)[None])
```

```python
import functools
import math

import jax
import jax.numpy as jnp
import numpy as np
from jax import lax
from jax.experimental import pallas as pl
from jax.experimental.pallas import tpu as pltpu

D_MODEL = 1024
HEAD_DIM = 64
SSD_HEADS = 8
SSD_D = SSD_HEADS * HEAD_DIM
SSD_GROUPS = 2
SSD_STATE = 128
SSD_CONV = 4
SSD_CONV_CH = SSD_D + 2 * SSD_GROUPS * SSD_STATE
SSD_CHUNK = 128
NSA_HEADS = 8
NSA_KV_HEADS = 2
NSA_D = NSA_HEADS * HEAD_DIM
KV_D = NSA_KV_HEADS * HEAD_DIM
CMP_BLOCK = 64
CMP_HIDDEN = 128
SEL_BLOCK = 64
TOP_N = 16
WINDOW = 512
Q_BLOCK = 128
ROT_DIM = HEAD_DIM // 4
ROPE_THETA = 500000.0
N_EXPERTS = 64
TOP_K = 6
N_EXPERT_GROUPS = 8
EXPERTS_PER_GROUP = N_EXPERTS // N_EXPERT_GROUPS
TOPK_GROUPS = 4
D_EXPERT = 256
D_SHARED = 256
ROUTED_SCALE = 2.5
MOE_BLOCK = 256
DEPTH = 1
DEEPNORM_ALPHA = (2.0 * DEPTH) ** 0.25
LN_EPS = 1e-5
RMS_EPS = 1e-5
NEG = -1e30
FORCED_SCORE = 1e4
PAGE_SIZE = 128

LANES = 128
SUBLANES = 8
VMEM_LIMIT_BYTES = 56 * 1024 * 1024

U_Z = 0
U_XBC = U_Z + SSD_D
U_Q = U_XBC + SSD_CONV_CH
U_KVC = U_Q + NSA_D
U_KVS = U_KVC + 2 * KV_D
U_KVW = U_KVS + 2 * KV_D
U_DTG = U_KVW + 2 * KV_D
U_TOTAL = U_DTG + LANES
GATE_COL0 = SSD_HEADS

BF16 = jnp.bfloat16
F32 = jnp.float32


def _cparams(sem):
    return pltpu.CompilerParams(dimension_semantics=sem, vmem_limit_bytes=VMEM_LIMIT_BYTES)


def _bdot(a, b):
    return jnp.dot(a.astype(BF16), b.astype(BF16), preferred_element_type=F32)


def _bdot_nt(a, b):
    return lax.dot_general(a.astype(BF16), b.astype(BF16), (((1,), (1,)), ((), ())),
                           preferred_element_type=F32)


def _hdot(a, b):
    return jnp.dot(a, b, preferred_element_type=F32, precision=lax.Precision.HIGHEST)


def _sigmoid(x):
    return 1.0 / (1.0 + jnp.exp(-x))


def _silu(x):
    return x * _sigmoid(x)


def _layer_norm(x, g, b):
    mu = jnp.mean(x, axis=-1, keepdims=True)
    xc = x - mu
    var = jnp.mean(xc * xc, axis=-1, keepdims=True)
    return xc * lax.rsqrt(var + LN_EPS) * g + b


def _rope_tile(x, cos, sa, sb):
    return x * cos + pltpu.roll(x, LANES - ROT_DIM // 2, 1) * sa + pltpu.roll(x, ROT_DIM // 2, 1) * sb


def _rope_rows(x, cos, sin):
    half = ROT_DIM // 2
    parts = []
    for hd in range(NSA_KV_HEADS):
        r0 = hd * HEAD_DIM
        x1, x2 = x[r0:r0 + half], x[r0 + half:r0 + ROT_DIM]
        parts += [x1 * cos - x2 * sin, x2 * cos + x1 * sin, x[r0 + ROT_DIM:r0 + HEAD_DIM]]
    return jnp.concatenate(parts, axis=0)


def _inproj_kernel(x_ref, g_ref, b_ref, w_ref, rope_ref, *refs, feature_major):
    if feature_major:
        wkv_t_ref, rope_t_ref = refs[:2]
        refs = refs[2:]
    h_ref, z_ref, xbc_ref, qc_ref, qr_ref, kvc_ref, kvs_ref, kvw_ref, dtg_ref = refs[:9]
    h = _layer_norm(x_ref[...], g_ref[...], b_ref[...])
    h_ref[...] = h
    hb = h.astype(BF16)
    if feature_major:
        kvc_t_ref, kvs_t_ref, kvw_t_ref = refs[9:]
        ut = lax.dot_general(wkv_t_ref[...], hb, (((1,), (1,)), ((), ())), preferred_element_type=F32)
        half = ROT_DIM // 2
        cos_t, sin_t = rope_t_ref[0:half, :], rope_t_ref[half:2 * half, :]
        kvc_t_ref[0] = ut[0:2 * KV_D]
        kvs_t_ref[0, 0:KV_D] = _rope_rows(ut[2 * KV_D:3 * KV_D], cos_t, sin_t)
        kvs_t_ref[0, KV_D:2 * KV_D] = ut[3 * KV_D:4 * KV_D]
        kvw_t_ref[0, 0:KV_D] = _rope_rows(ut[4 * KV_D:5 * KV_D], cos_t, sin_t)
        kvw_t_ref[0, KV_D:2 * KV_D] = ut[5 * KV_D:6 * KV_D]
    u = jnp.dot(hb, w_ref[...], preferred_element_type=F32)
    cos = rope_ref[:, 0:LANES]
    sa = rope_ref[:, LANES:2 * LANES]
    sb = rope_ref[:, 2 * LANES:3 * LANES]
    z_ref[...] = u[:, U_Z:U_XBC]
    xbc_ref[...] = u[:, U_XBC:U_Q]
    qc_ref[...] = u[:, U_Q:U_KVC]
    for c in range(NSA_D // LANES):
        qr_ref[:, c * LANES:(c + 1) * LANES] = _rope_tile(u[:, U_Q + c * LANES:U_Q + (c + 1) * LANES], cos, sa, sb)
    kvc_ref[...] = u[:, U_KVC:U_KVS]
    kvs_ref[:, 0:KV_D] = _rope_tile(u[:, U_KVS:U_KVS + KV_D], cos, sa, sb)
    kvs_ref[:, KV_D:2 * KV_D] = u[:, U_KVS + KV_D:U_KVW]
    kvw_ref[:, 0:KV_D] = _rope_tile(u[:, U_KVW:U_KVW + KV_D], cos, sa, sb)
    kvw_ref[:, KV_D:2 * KV_D] = u[:, U_KVW + KV_D:U_DTG]
    dtg_ref[...] = u[:, U_DTG:U_TOTAL]


def _rope_tables(pos):
    half = ROT_DIM // 2
    inv = ROPE_THETA ** (-jnp.arange(half, dtype=F32) / half)
    ang = pos.astype(F32)[:, None] * inv
    cos, sin = jnp.cos(ang), jnp.sin(ang)
    ones = jnp.ones((pos.shape[0], HEAD_DIM - ROT_DIM), F32)
    zeros = jnp.zeros((pos.shape[0], HEAD_DIM - ROT_DIM), F32)
    zh = jnp.zeros_like(sin)
    c = jnp.concatenate([cos, cos, ones], 1)
    sa = jnp.concatenate([-sin, zh, zeros], 1)
    sb = jnp.concatenate([zh, sin, zeros], 1)
    return jnp.concatenate([jnp.tile(t, (1, LANES // HEAD_DIM)) for t in (c, sa, sb)], 1)


def _permute_w_in(w):
    sizes = (SSD_D, SSD_CONV_CH, SSD_HEADS, NSA_D, KV_D, KV_D, KV_D, KV_D, KV_D, KV_D, 3 * NSA_HEADS)
    offs = np.concatenate([[0], np.cumsum(sizes)])
    seg = [w[:, offs[i]:offs[i + 1]] for i in range(len(sizes))]
    pad = jnp.zeros((w.shape[0], LANES - SSD_HEADS - 3 * NSA_HEADS), w.dtype)
    out = jnp.concatenate([seg[0], seg[1], seg[3], seg[4], seg[5], seg[6], seg[7], seg[8], seg[9],
                           seg[2], seg[10], pad], 1)
    return out.astype(BF16)


def _rope_tables_t(pos):
    half = ROT_DIM // 2
    inv = ROPE_THETA ** (-jnp.arange(half, dtype=F32) / half)
    ang = inv[:, None] * pos.astype(F32)[None, :]
    return jnp.concatenate([jnp.cos(ang), jnp.sin(ang)], 0)


def _inproj(x, ln_g, ln_b, w_perm, rope_tab, tm, rope_tab_t=None):
    n = x.shape[0]
    nt = n // tm
    t = rope_tab.shape[0]
    n_rope_blocks = t // tm
    feature_major = rope_tab_t is not None
    row = lambda w: pl.BlockSpec((tm, w), lambda i: (i, 0))
    const = lambda a: pl.BlockSpec(a.shape, lambda i: (0,) * a.ndim)
    widths = (D_MODEL, SSD_D, SSD_CONV_CH, NSA_D, NSA_D, 2 * KV_D, 2 * KV_D, 2 * KV_D, LANES)
    in_specs = [row(D_MODEL), const(ln_g), const(ln_b), const(w_perm),
                pl.BlockSpec((tm, 3 * LANES), lambda i: (i % n_rope_blocks, 0))]
    out_specs = [row(w) for w in widths]
    out_shape = [jax.ShapeDtypeStruct((n, w), F32) for w in widths]
    args = [x, ln_g, ln_b, w_perm, rope_tab]
    if feature_major:
        wkv_t = w_perm[:, U_KVC:U_DTG].T
        in_specs += [const(wkv_t), pl.BlockSpec((rope_tab_t.shape[0], tm), lambda i: (0, i % n_rope_blocks))]
        args += [wkv_t, rope_tab_t]
        out_specs += [pl.BlockSpec((1, 2 * KV_D, tm), lambda i: (i // n_rope_blocks, 0, i % n_rope_blocks))] * 3
        out_shape += [jax.ShapeDtypeStruct((n // t, 2 * KV_D, t), F32)] * 3
    return pl.pallas_call(
        functools.partial(_inproj_kernel, feature_major=feature_major),
        grid=(nt,),
        in_specs=in_specs,
        out_specs=out_specs,
        out_shape=out_shape,
        compiler_params=_cparams(("parallel",)),
        name="inproj",
    )(*args)


def _softplus(x):
    return jnp.maximum(x, 0.0) + jnp.log1p(jnp.exp(-jnp.abs(x)))


def _gated_group_norm(y, z, norm_w):
    y = y * _silu(z)
    gw = SSD_D // SSD_GROUPS
    parts = []
    for g in range(SSD_GROUPS):
        yg = y[:, g * gw:(g + 1) * gw]
        ms = jnp.mean(yg * yg, axis=-1, keepdims=True)
        parts.append(yg * lax.rsqrt(ms + RMS_EPS))
    return jnp.concatenate(parts, axis=1) * norm_w


def _ssd_prompt_kernel(xbc_ref, z_ref, dtg_ref, convw_ref, convb_ref, dtb_ref, alog_ref, dskip_ref, normw_ref,
                       y_ref, state_ref, conv_ref, ext_ref, s_ref):
    c = pl.program_id(1)
    nc = pl.num_programs(1)
    L = SSD_CHUNK
    halo = SUBLANES

    @pl.when(c == 0)
    def _():
        ext_ref[0:halo, :] = jnp.zeros((halo, SSD_CONV_CH), F32)
        s_ref[...] = jnp.zeros_like(s_ref)

    xin = xbc_ref[...]
    ext_ref[halo:halo + L, :] = xin
    xc = convw_ref[SSD_CONV - 1:SSD_CONV, :] * xin
    for k in range(SSD_CONV - 1):
        off = halo - (SSD_CONV - 1) + k
        xc = xc + convw_ref[k:k + 1, :] * ext_ref[off:off + L, :]
    ext_ref[0:halo, :] = ext_ref[L:L + halo, :]
    xc = _silu(xc + convb_ref[...])
    xs = xc[:, 0:SSD_D]
    ns = SSD_GROUPS * SSD_STATE
    bm = xc[:, SSD_D:SSD_D + ns]
    cm = xc[:, SSD_D + ns:SSD_D + 2 * ns]

    dt = _softplus(dtg_ref[...] + dtb_ref[...])
    da = dt * (-jnp.exp(alog_ref[...]))
    row = lax.broadcasted_iota(jnp.int32, (L, L), 0)
    col = lax.broadcasted_iota(jnp.int32, (L, L), 1)
    tril = row >= col
    acum = _hdot(tril.astype(F32), da)
    acum_t = acum.T
    eacum = jnp.exp(acum)
    alast = acum[L - 1:L, :]
    edecay = jnp.exp(alast - acum)
    elast = jnp.exp(alast)

    dt_full = jnp.concatenate([jnp.broadcast_to(dt[:, h:h + 1], (L, HEAD_DIM)) for h in range(SSD_HEADS)], 1)
    dec_full = jnp.concatenate([jnp.broadcast_to(edecay[:, h:h + 1], (L, HEAD_DIM)) for h in range(SSD_HEADS)], 1)
    xdt = xs * dt_full
    xdec_t = (xdt * dec_full).T

    hpg = SSD_HEADS // SSD_GROUPS
    y_parts = []
    for h in range(SSD_HEADS):
        g = h // hpg
        b_g = bm[:, g * SSD_STATE:(g + 1) * SSD_STATE]
        c_g = cm[:, g * SSD_STATE:(g + 1) * SSD_STATE]
        if h % hpg == 0:
            cb = _bdot_nt(c_g, b_g)
        seg = acum[:, h:h + 1] - acum_t[h:h + 1, :]
        lmat = jnp.where(tril, jnp.exp(jnp.where(tril, seg, 0.0)), 0.0)
        xdt_h = xdt[:, h * HEAD_DIM:(h + 1) * HEAD_DIM]
        y_h = _bdot(cb * lmat, xdt_h)
        s_prev = s_ref[h]
        y_h = y_h + _bdot_nt(c_g, s_prev) * eacum[:, h:h + 1]
        y_h = y_h + dskip_ref[:, h * HEAD_DIM:(h + 1) * HEAD_DIM] * xs[:, h * HEAD_DIM:(h + 1) * HEAD_DIM]
        y_parts.append(y_h)
        s_ref[h] = elast[:, h:h + 1] * s_prev + _bdot(xdec_t[h * HEAD_DIM:(h + 1) * HEAD_DIM, :], b_g)
    y = jnp.concatenate(y_parts, axis=1)
    y_ref[...] = _gated_group_norm(y, z_ref[...], normw_ref[...])

    @pl.when(c == nc - 1)
    def _():
        state_ref[0] = s_ref[...]
        conv_ref[0] = xin[L - (SSD_CONV - 1):L, :]


def _ssd_prompt(xbc, z, dtg, conv_w, conv_b, dt_bias_pad, a_log_pad, d_skip_full, norm_w, bn, t):
    nc = t // SSD_CHUNK
    row = lambda w: pl.BlockSpec((SSD_CHUNK, w), lambda b, c: (b * nc + c, 0))
    const = lambda a: pl.BlockSpec(a.shape, lambda b, c: (0,) * a.ndim)
    return pl.pallas_call(
        _ssd_prompt_kernel,
        grid=(bn, nc),
        in_specs=[row(SSD_CONV_CH), row(SSD_D), row(LANES), const(conv_w), const(conv_b), const(dt_bias_pad),
                  const(a_log_pad), const(d_skip_full), const(norm_w)],
        out_specs=[row(SSD_D),
                   pl.BlockSpec((1, SSD_HEADS, HEAD_DIM, SSD_STATE), lambda b, c: (b, 0, 0, 0)),
                   pl.BlockSpec((1, SSD_CONV - 1, SSD_CONV_CH), lambda b, c: (b, 0, 0))],
        out_shape=[jax.ShapeDtypeStruct((bn * t, SSD_D), F32),
                   jax.ShapeDtypeStruct((bn, SSD_HEADS, HEAD_DIM, SSD_STATE), F32),
                   jax.ShapeDtypeStruct((bn, SSD_CONV - 1, SSD_CONV_CH), F32)],
        scratch_shapes=[pltpu.VMEM((SSD_CHUNK + 2 * SUBLANES, SSD_CONV_CH), F32),
                        pltpu.VMEM((SSD_HEADS, HEAD_DIM, SSD_STATE), F32)],
        compiler_params=_cparams(("parallel", "arbitrary")),
        name="ssd_prompt",
    )(xbc, z, dtg, conv_w, conv_b, dt_bias_pad, a_log_pad, d_skip_full, norm_w)


def _pad_lanes(v, fill=0.0):
    return jnp.concatenate([v.astype(F32), jnp.full((LANES - v.shape[0],), fill, F32)])[None]


def _compress_rows(k_ref, v_ref, pe_ref, w1k_ref, w1v_ref, b1_ref, w2k_ref, w2v_ref, b2_ref, nb):
    acck = jnp.zeros((nb, 2 * CMP_HIDDEN), F32)
    accv = jnp.zeros((nb, 2 * CMP_HIDDEN), F32)
    for l in range(CMP_BLOCK):
        xk = k_ref[pl.ds(l, nb, stride=CMP_BLOCK), :] + pe_ref[l:l + 1, 0:KV_D]
        xv = v_ref[pl.ds(l, nb, stride=CMP_BLOCK), :] + pe_ref[l:l + 1, KV_D:2 * KV_D]
        acck = acck + jnp.dot(xk.astype(BF16), w1k_ref[l], preferred_element_type=F32)
        accv = accv + jnp.dot(xv.astype(BF16), w1v_ref[l], preferred_element_type=F32)
    hk = _silu(acck + b1_ref[:, 0:2 * CMP_HIDDEN])
    hv = _silu(accv + b1_ref[:, 2 * CMP_HIDDEN:4 * CMP_HIDDEN])
    ok = jnp.dot(hk.astype(BF16), w2k_ref[...], preferred_element_type=F32) + b2_ref[:, 0:KV_D]
    ov = jnp.dot(hv.astype(BF16), w2v_ref[...], preferred_element_type=F32) + b2_ref[:, KV_D:2 * KV_D]
    return jnp.concatenate([ok, ov], axis=1)


def _compress_kernel(k_ref, v_ref, pe_ref, w1k_ref, w1v_ref, b1_ref, w2k_ref, w2v_ref, b2_ref, o_ref, *, nb):
    o_ref[...] = _compress_rows(k_ref, v_ref, pe_ref, w1k_ref, w1v_ref, b1_ref, w2k_ref, w2v_ref, b2_ref, nb)


def _block_diag2(w):
    z = jnp.zeros_like(w)
    return jnp.concatenate([jnp.concatenate([w, z], -1), jnp.concatenate([z, w], -1)], -2)


def _compress_consts(cmp_pe, cmp_w1, cmp_b1, cmp_w2, cmp_b2):
    pe = jnp.concatenate([cmp_pe[0], cmp_pe[0], cmp_pe[1], cmp_pe[1]], -1)
    w1k = _block_diag2(cmp_w1[0]).astype(BF16)
    w1v = _block_diag2(cmp_w1[1]).astype(BF16)
    b1 = jnp.concatenate([cmp_b1[0], cmp_b1[0], cmp_b1[1], cmp_b1[1]])[None]
    w2k = _block_diag2(cmp_w2[0]).astype(BF16)
    w2v = _block_diag2(cmp_w2[1]).astype(BF16)
    b2 = jnp.concatenate([cmp_b2[0], cmp_b2[0], cmp_b2[1], cmp_b2[1]])[None]
    return pe, w1k, w1v, b1, w2k, w2v, b2


def _compress_prompt(kvc, consts, rows_per_step):
    n = kvc.shape[0]
    nb = rows_per_step // CMP_BLOCK
    const = lambda a: pl.BlockSpec(a.shape, lambda i: (0,) * a.ndim)
    return pl.pallas_call(
        functools.partial(_compress_kernel, nb=nb),
        grid=(n // rows_per_step,),
        in_specs=[pl.BlockSpec((rows_per_step, KV_D), lambda i: (i, 0)),
                  pl.BlockSpec((rows_per_step, KV_D), lambda i: (i, 1))] + [const(a) for a in consts],
        out_specs=pl.BlockSpec((nb, 2 * KV_D), lambda i: (i, 0)),
        out_shape=jax.ShapeDtypeStruct((n // CMP_BLOCK, 2 * KV_D), F32),
        compiler_params=_cparams(("parallel",)),
        name="compress_prompt",
    )(kvc, kvc, *consts)


SEL_KEY_TILE = 512
WIN_KEYS = WINDOW + Q_BLOCK


def _dup_head(x, hk):
    sw = pltpu.roll(x, HEAD_DIM, 1)
    low = lax.broadcasted_iota(jnp.int32, x.shape, 1) < HEAD_DIM
    return jnp.where(low, x, sw) if hk == 0 else jnp.where(low, sw, x)


def _masked_softmax(s, mask):
    sm = jnp.where(mask, s, NEG)
    ex = jnp.where(mask, jnp.exp(sm - jnp.max(sm, axis=-1, keepdims=True)), 0.0)
    den = jnp.sum(ex, axis=-1, keepdims=True)
    return ex / jnp.where(den > 0.0, den, 1.0)


def _select_blocks_t(imp, cur, n_top):
    j = lax.broadcasted_iota(jnp.int32, imp.shape, 0)
    future = j > cur
    forced = (j == 0) | (j == cur) | (j == cur - 1)
    score = jnp.where(future, NEG, jnp.where(forced, FORCED_SCORE, imp))
    return ((_rank_rows(score) < n_top) & (score > 0.5 * NEG)).astype(F32)


def _nsa_prompt_kernel(qc_ref, qr_ref, dtg_ref, cmp_ref, kvs_ref, kvw_ref, o_ref,
                       cmp_d, kvs_d, kvw_d, bias_ref, qrs_ref, m_ref, l_ref, acc_ref, *, t):
    qb = pl.program_id(1)
    nbk = t // SEL_BLOCK
    tq = Q_BLOCK
    tk = SEL_KEY_TILE
    hpg = NSA_HEADS // NSA_KV_HEADS
    scale = HEAD_DIM ** -0.5

    @pl.when(qb == 0)
    def _():
        cmp_d[...] = jnp.zeros_like(cmp_d)
        for src, dst, n in ((cmp_ref, cmp_d, nbk), (kvs_ref, kvs_d, t), (kvw_ref, kvw_d, t)):
            x = src[...]
            for hk in range(NSA_KV_HEADS):
                dst[hk, 0:n, 0:KV_D] = _dup_head(x[:, 0:KV_D], hk).astype(BF16)
                dst[hk, 0:n, KV_D:2 * KV_D] = _dup_head(x[:, KV_D:2 * KV_D], hk).astype(BF16)

    t0 = qb * tq
    rows = t0 + lax.broadcasted_iota(jnp.int32, (tq, 1), 0)
    lane = lax.broadcasted_iota(jnp.int32, (tq, LANES), 1)
    half_mask = (lane < HEAD_DIM, lane >= HEAD_DIM)
    sig = _sigmoid(dtg_ref[...])
    vis = (lane + 1) * CMP_BLOCK - 1 <= rows
    cur_l = (t0 + lax.broadcasted_iota(jnp.int32, (1, tq), 1)) // SEL_BLOCK
    expand = (lax.broadcasted_iota(jnp.int32, (LANES, t), 1) // SEL_BLOCK
              == lax.broadcasted_iota(jnp.int32, (LANES, t), 0)).astype(BF16)
    win_start = pl.multiple_of(jnp.maximum(t0 - WINDOW, 0), tq)
    wpos = win_start + lax.broadcasted_iota(jnp.int32, (tq, WIN_KEYS), 1)
    win_bias = jnp.where((wpos <= rows) & (wpos >= rows - WINDOW), 0.0, NEG)
    n_kt = (t0 + tq + tk - 1) // tk

    def stack_heads(ref, hk):
        parts = []
        for hh in range(hpg):
            head = hk * hpg + hh
            p, e = head // 2, head % 2
            parts.append(jnp.where(half_mask[e], ref[:, p * LANES:(p + 1) * LANES] * scale, 0.0))
        return jnp.concatenate(parts, axis=0).astype(BF16)

    o_cmp_g = []
    for hk in range(NSA_KV_HEADS):
        qcs = stack_heads(qc_ref, hk)
        s = lax.dot_general(qcs, cmp_d[hk, :, 0:KV_D], (((1,), (1,)), ((), ())), preferred_element_type=F32)
        pc = _masked_softmax(s.reshape(hpg, tq, LANES), vis[None])
        imp = jnp.sum(pc, axis=0)
        o_cmp_g.append(jnp.dot(pc.reshape(hpg * tq, LANES).astype(BF16), cmp_d[hk, :, KV_D:2 * KV_D],
                               preferred_element_type=F32).reshape(hpg, tq, LANES))

        sel_t = _select_blocks_t(imp.T[0:nbk, :], cur_l, TOP_N)
        sel = jnp.concatenate([sel_t, jnp.zeros((LANES - nbk, tq), F32)], axis=0).T
        selk = jnp.dot(sel.astype(BF16), expand, preferred_element_type=F32)
        for kt in range(t // tk):
            @pl.when(kt < n_kt)
            def _(kt=kt, hk=hk, selk=selk):
                kpos = kt * tk + lax.broadcasted_iota(jnp.int32, (tq, tk), 1)
                bias_ref[hk, kt] = jnp.where((selk[:, kt * tk:(kt + 1) * tk] > 0.5) & (kpos <= rows), 0.0, NEG)

        qrs_ref[hk] = stack_heads(qr_ref, hk)

    m_ref[...] = jnp.full(m_ref.shape, NEG, F32)
    l_ref[...] = jnp.zeros(l_ref.shape, F32)
    acc_ref[...] = jnp.zeros(acc_ref.shape, F32)

    def sel_step(kt, carry):
        k0 = pl.multiple_of(kt * tk, tk)
        for hk in range(NSA_KV_HEADS):
            kblk = kvs_d[hk, pl.ds(k0, tk), 0:KV_D]
            vblk = kvs_d[hk, pl.ds(k0, tk), KV_D:2 * KV_D]
            s = lax.dot_general(qrs_ref[hk], kblk, (((1,), (1,)), ((), ())), preferred_element_type=F32)
            s = s.reshape(hpg, tq, tk) + bias_ref[hk, kt][None]
            m_old = m_ref[hk]
            m_new = jnp.maximum(m_old, jnp.max(s, axis=-1, keepdims=True))
            alpha = jnp.exp(m_old - m_new)
            pe = jnp.exp(s - jnp.concatenate([m_new] * (tk // LANES), axis=-1))
            l_ref[hk] = alpha * l_ref[hk] + jnp.sum(pe, axis=-1, keepdims=True)
            pv = jnp.dot(pe.reshape(hpg * tq, tk).astype(BF16), vblk, preferred_element_type=F32)
            acc_ref[hk] = alpha * acc_ref[hk] + pv.reshape(hpg, tq, LANES)
            m_ref[hk] = m_new
        return carry

    lax.fori_loop(0, n_kt, sel_step, 0)

    for hk in range(NSA_KV_HEADS):
        o_cmp = o_cmp_g[hk]
        o_slc = acc_ref[hk] / l_ref[hk]
        kw = kvw_d[hk, pl.ds(win_start, WIN_KEYS), 0:KV_D]
        vw = kvw_d[hk, pl.ds(win_start, WIN_KEYS), KV_D:2 * KV_D]
        sw = lax.dot_general(qrs_ref[hk], kw, (((1,), (1,)), ((), ())), preferred_element_type=F32)
        sw = sw.reshape(hpg, tq, WIN_KEYS) + win_bias[None]
        pw = jnp.exp(sw - jnp.max(sw, axis=-1, keepdims=True))
        den = jnp.sum(pw, axis=-1, keepdims=True)
        o_win = jnp.dot(pw.reshape(hpg * tq, WIN_KEYS).astype(BF16), vw,
                        preferred_element_type=F32).reshape(hpg, tq, LANES) / den

        for hh in range(hpg):
            head = hk * hpg + hh
            p, e = head // 2, head % 2
            c0 = GATE_COL0 + head * 3
            mix = (sig[:, c0:c0 + 1] * o_cmp[hh] + sig[:, c0 + 1:c0 + 2] * o_slc[hh]
                   + sig[:, c0 + 2:c0 + 3] * o_win[hh])
            if e == 0:
                mix_even = mix
            else:
                o_ref[:, p * LANES:(p + 1) * LANES] = jnp.where(half_mask[0], mix_even, mix)


def _nsa_prompt(qc, qr, dtg, kvcmp, kvs, kvw, bn, t):
    nq = t // Q_BLOCK
    nbk = t // SEL_BLOCK
    hpg = NSA_HEADS // NSA_KV_HEADS
    assert nbk >= TOP_N and t >= WIN_KEYS and t % SEL_KEY_TILE == 0
    qrow = lambda w: pl.BlockSpec((Q_BLOCK, w), lambda b, i: (b * nq + i, 0))
    seq = lambda r: pl.BlockSpec((r, 2 * KV_D), lambda b, i: (b, 0))
    return pl.pallas_call(
        functools.partial(_nsa_prompt_kernel, t=t),
        grid=(bn, nq),
        in_specs=[qrow(NSA_D), qrow(NSA_D), qrow(LANES), seq(nbk), seq(t), seq(t)],
        out_specs=qrow(NSA_D),
        out_shape=jax.ShapeDtypeStruct((bn * t, NSA_D), F32),
        scratch_shapes=[pltpu.VMEM((NSA_KV_HEADS, LANES, 2 * KV_D), BF16),
                        pltpu.VMEM((NSA_KV_HEADS, t, 2 * KV_D), BF16),
                        pltpu.VMEM((NSA_KV_HEADS, t, 2 * KV_D), BF16),
                        pltpu.VMEM((NSA_KV_HEADS, t // SEL_KEY_TILE, Q_BLOCK, SEL_KEY_TILE), F32),
                        pltpu.VMEM((NSA_KV_HEADS, hpg * Q_BLOCK, LANES), BF16),
                        pltpu.VMEM((NSA_KV_HEADS, hpg, Q_BLOCK, LANES), F32),
                        pltpu.VMEM((NSA_KV_HEADS, hpg, Q_BLOCK, LANES), F32),
                        pltpu.VMEM((NSA_KV_HEADS, hpg, Q_BLOCK, LANES), F32)],
        compiler_params=_cparams(("parallel", "arbitrary")),
        name="nsa_prompt",
    )(qc, qr, dtg, kvcmp, kvs, kvw)


def _outproj_kernel(ys_ref, yn_ref, h_ref, ws_ref, wn_ref, g_ref, b_ref, o_ref):
    mix = jnp.dot(ys_ref[...].astype(BF16), ws_ref[...], preferred_element_type=F32)
    mix = mix + jnp.dot(yn_ref[...].astype(BF16), wn_ref[...], preferred_element_type=F32)
    o_ref[...] = _layer_norm(DEEPNORM_ALPHA * h_ref[...] + mix, g_ref[...], b_ref[...])


def _outproj(y_ssd, y_nsa, h, w_ssd, w_nsa, ln_g, ln_b, tm):
    n = h.shape[0]
    row = lambda w: pl.BlockSpec((tm, w), lambda i: (i, 0))
    const = lambda a: pl.BlockSpec(a.shape, lambda i: (0,) * a.ndim)
    return pl.pallas_call(
        _outproj_kernel,
        grid=(n // tm,),
        in_specs=[row(SSD_D), row(NSA_D), row(D_MODEL), const(w_ssd), const(w_nsa), const(ln_g), const(ln_b)],
        out_specs=row(D_MODEL),
        out_shape=jax.ShapeDtypeStruct((n, D_MODEL), F32),
        compiler_params=_cparams(("parallel",)),
        name="outproj",
    )(y_ssd, y_nsa, h, w_ssd, w_nsa, ln_g, ln_b)


MOE_TOKENS = 256
ROUTE_ROWS = 8


def _token_tile_specs(n_main_tiles):
    main = pl.BlockSpec((MOE_TOKENS, D_MODEL), lambda i, *_: (jnp.minimum(i, n_main_tiles - 1), 0))
    tail = pl.BlockSpec((MOE_TOKENS, D_MODEL), lambda i, *_: (0, 0))
    return main, tail


def _token_tile(i, n_main_tiles, main_ref, tail_ref):
    return jnp.where(i < n_main_tiles, main_ref[...], tail_ref[...])


def _rank_rows(x):
    n = x.shape[0]
    idx = lax.broadcasted_iota(jnp.int32, x.shape, 0)
    rank = jnp.zeros(x.shape, F32)
    for r in range(n):
        row = x[r:r + 1, :]
        rank = rank + ((row > x) | ((row == x) & (idx > r))).astype(F32)
    return rank


def _route_kernel(h_ref, ht_ref, rw_ref, rb_ref, slot_ref, tokinfo_ref, meta_ref, cnt_ref, carry_ref, carry_row_ref, *,
                  n_valid, n_main):
    i = pl.program_id(0)
    tm = MOE_TOKENS

    @pl.when(i == 0)
    def _():
        carry_ref[...] = jnp.zeros_like(carry_ref)
        carry_row_ref[...] = jnp.zeros_like(carry_row_ref)

    logits = lax.dot_general(rw_ref[...], _token_tile(i, n_main, h_ref, ht_ref).astype(BF16), (((1,), (1,)), ((), ())),
                             preferred_element_type=F32)
    scores = _sigmoid(logits)
    biased = scores + rb_ref[:, 0:1]
    b3 = biased.reshape(N_EXPERT_GROUPS, EXPERTS_PER_GROUP, tm)
    sidx = lax.broadcasted_iota(jnp.int32, b3.shape, 1)
    m1 = jnp.max(b3, axis=1, keepdims=True)
    first = jnp.min(jnp.where(b3 == m1, sidx, EXPERTS_PER_GROUP), axis=1, keepdims=True)
    m2 = jnp.max(jnp.where(sidx == first, -jnp.inf, b3), axis=1, keepdims=True)
    grp_score = (m1 + m2).reshape(N_EXPERT_GROUPS, tm)
    grp_keep = _rank_rows(grp_score) < TOPK_GROUPS
    masked = jnp.where(grp_keep.reshape(N_EXPERT_GROUPS, 1, tm), b3, NEG).reshape(N_EXPERTS, tm)
    rank = _rank_rows(masked)
    tok = i * tm + lax.broadcasted_iota(jnp.int32, (1, tm), 1)
    valid = tok < n_valid
    sel = (rank < TOP_K) & valid
    self32 = sel.astype(F32)
    wsel = self32 * scores
    wsum = jnp.sum(wsel, axis=0, keepdims=True)
    w = wsel / jnp.where(wsum > 0.0, wsum, 1.0) * ROUTED_SCALE

    selb = sel.astype(BF16)
    tri = lambda n, strict_upper: (
        (lax.broadcasted_iota(jnp.int32, (n, n), 0) < lax.broadcasted_iota(jnp.int32, (n, n), 1))
        if strict_upper else
        (lax.broadcasted_iota(jnp.int32, (n, n), 0) > lax.broadcasted_iota(jnp.int32, (n, n), 1))).astype(BF16)
    pad8 = lambda c: jnp.floor((c + (SUBLANES - 1.0)) * (1.0 / SUBLANES)) * SUBLANES
    pos_tile = jnp.dot(selb, tri(tm, True), preferred_element_type=F32)
    cnt_col = pad8(jnp.sum(self32, axis=1, keepdims=True))
    first_col = jnp.dot(tri(N_EXPERTS, False), jnp.broadcast_to(cnt_col, (N_EXPERTS, LANES)).astype(BF16),
                        preferred_element_type=F32)[:, 0:1]
    slot = first_col + pos_tile

    sel_pad = jnp.concatenate([selb, jnp.zeros((LANES - N_EXPERTS, tm), BF16)], axis=0)
    cnt_row = pad8(lax.dot_general(jnp.ones((SUBLANES, tm), BF16), sel_pad, (((1,), (1,)), ((), ())),
                                   preferred_element_type=F32))
    first_row = jnp.dot(cnt_row.astype(BF16), tri(LANES, True), preferred_element_type=F32)
    prev_row = carry_row_ref[...]
    meta = jnp.concatenate([cnt_row[0:1], first_row[0:1], prev_row[0:1], jnp.zeros((SUBLANES - 3, LANES), F32)], 0)
    meta_ref[0] = meta.astype(jnp.int32)
    carry_row_ref[...] = prev_row + cnt_row
    carry_ref[...] = carry_ref[...] + cnt_col

    slot_rows, w_rows = [], []
    for k in range(TOP_K):
        hit = (rank == k) & sel
        slot_rows.append(jnp.sum(jnp.where(hit, slot, 0.0), axis=0, keepdims=True))
        w_rows.append(jnp.sum(jnp.where(hit, w, 0.0), axis=0, keepdims=True))
    slot_rows = [jnp.where(valid, r, -1.0) for r in slot_rows]
    pad2 = jnp.zeros((ROUTE_ROWS - TOP_K, tm), F32)
    slot_ref[...] = jnp.concatenate(slot_rows + [pad2 - 1.0], 0).astype(jnp.int32)
    info = jnp.concatenate(w_rows + [pad2] + slot_rows + [jnp.zeros((LANES - ROUTE_ROWS - TOP_K, tm), F32)], 0)
    tokinfo_ref[...] = info.T

    @pl.when(i == pl.num_programs(0) - 1)
    def _():
        cnt_ref[...] = jnp.broadcast_to(carry_ref[:, 0:1], cnt_ref.shape)


def _route(h_main, h_tail, router_wt, router_bias_col, n_valid):
    tm = MOE_TOKENS
    n_main = h_main.shape[0] // tm
    n = h_main.shape[0] + tm
    const = lambda a: pl.BlockSpec(a.shape, lambda i: (0,) * a.ndim)
    return pl.pallas_call(
        functools.partial(_route_kernel, n_valid=n_valid, n_main=n_main),
        grid=(n // tm,),
        in_specs=[*_token_tile_specs(n_main), const(router_wt), const(router_bias_col)],
        out_specs=[pl.BlockSpec((ROUTE_ROWS, tm), lambda i: (0, i)),
                   pl.BlockSpec((tm, LANES), lambda i: (i, 0)),
                   pl.BlockSpec((1, SUBLANES, LANES), lambda i: (i, 0, 0)),
                   pl.BlockSpec((N_EXPERTS, LANES), lambda i: (0, 0))],
        out_shape=[jax.ShapeDtypeStruct((ROUTE_ROWS, n), jnp.int32),
                   jax.ShapeDtypeStruct((n, LANES), F32),
                   jax.ShapeDtypeStruct((n // tm, SUBLANES, LANES), jnp.int32),
                   jax.ShapeDtypeStruct((N_EXPERTS, LANES), F32)],
        scratch_shapes=[pltpu.VMEM((N_EXPERTS, LANES), F32), pltpu.VMEM((SUBLANES, LANES), F32)],
        compiler_params=_cparams(("arbitrary",)),
        name="moe_route",
    )(h_main, h_tail, router_wt, router_bias_col)


TILE_SLOTS = MOE_TOKENS * TOP_K + N_EXPERTS * SUBLANES
RUN_CHUNKS = tuple(1 << b for b in range(int(math.log2(MOE_TOKENS)), int(math.log2(SUBLANES)) - 1, -1))


def _run_copy(src_ref, src_row, dst_ref, dst_row, rows, sem):
    return pltpu.make_async_copy(src_ref.at[pl.ds(pl.multiple_of(src_row, SUBLANES), rows)],
                                 dst_ref.at[pl.ds(pl.multiple_of(dst_row, SUBLANES), rows)], sem)


def _start_run(src_ref, src_row, dst_ref, dst_row, n, sem, started):
    off = jnp.int32(0)
    out = []
    for c, rows in enumerate(RUN_CHUNKS):
        take = (n & rows) != 0

        @pl.when(take)
        def _(off=off, rows=rows):
            _run_copy(src_ref, src_row + off, dst_ref, dst_row + off, rows, sem).start()

        inc = take.astype(jnp.int32)
        off = off + inc * rows
        out.append(started[c] + inc)
    return tuple(out)


def _wait_runs(src_ref, dst_ref, sem, started):
    for c, rows in enumerate(RUN_CHUNKS):
        def wait_one(j, carry, rows=rows):
            _run_copy(src_ref, 0, dst_ref, 0, rows, sem).wait()
            return carry

        lax.fori_loop(0, started[c], wait_one, 0)


def _dispatch_kernel(start_ref, cnt_ref, meta_ref, slot_ref, x_ref, xt_ref, xs_ref, sorted_ref, zero_ref, sem, zsem, *,
                     cap, n_main):
    i = pl.program_id(0)
    tm = MOE_TOKENS

    @pl.when(i == 0)
    def _():
        zero_ref[...] = jnp.zeros_like(zero_ref)

        def fill_expert(e, started):
            lo = start_ref[e] + cnt_ref[e]
            hi = jnp.where(e == N_EXPERTS - 1, cap, start_ref[jnp.minimum(e + 1, N_EXPERTS - 1)])
            n_full = (hi - lo) // tm

            def fill_full(j, st):
                return _start_run(zero_ref, 0, xs_ref, lo + j * tm, jnp.int32(tm), zsem, st)

            started = lax.fori_loop(0, n_full, fill_full, started)
            return _start_run(zero_ref, 0, xs_ref, lo + n_full * tm, (hi - lo) - n_full * tm, zsem, started)

        filled = lax.fori_loop(0, N_EXPERTS, fill_expert, tuple(jnp.int32(0) for _ in RUN_CHUNKS))
        _wait_runs(zero_ref, xs_ref, zsem, filled)

    srow = lax.broadcasted_iota(jnp.int32, (TILE_SLOTS, tm), 0)
    onehot = srow == slot_ref[0:1, :]
    for k in range(1, TOP_K):
        onehot = onehot | (srow == slot_ref[k:k + 1, :])
    sorted_ref[...] = jnp.dot(onehot.astype(BF16), _token_tile(i, n_main, x_ref, xt_ref).astype(BF16),
                              preferred_element_type=F32)

    def copy_expert(e, started):
        n = meta_ref[0, 0, e]
        return _start_run(sorted_ref, meta_ref[0, 1, e], xs_ref, start_ref[e] + meta_ref[0, 2, e], n, sem, started)

    started = lax.fori_loop(0, N_EXPERTS, copy_expert, tuple(jnp.int32(0) for _ in RUN_CHUNKS))
    _wait_runs(sorted_ref, xs_ref, sem, started)


def _dispatch(h_main, h_tail, slot_t, meta, seg_start, counts, cap):
    tm = MOE_TOKENS
    n_main = h_main.shape[0] // tm
    return pl.pallas_call(
        functools.partial(_dispatch_kernel, cap=cap, n_main=n_main),
        grid_spec=pltpu.PrefetchScalarGridSpec(
            num_scalar_prefetch=2,
            grid=(n_main + 1,),
            in_specs=[pl.BlockSpec((1, SUBLANES, LANES), lambda i, *_: (i, 0, 0), memory_space=pltpu.SMEM),
                      pl.BlockSpec((ROUTE_ROWS, tm), lambda i, *_: (0, i)),
                      *_token_tile_specs(n_main)],
            out_specs=pl.BlockSpec(memory_space=pl.ANY),
            scratch_shapes=[pltpu.VMEM((TILE_SLOTS, D_MODEL), F32), pltpu.VMEM((tm, D_MODEL), F32),
                            pltpu.SemaphoreType.DMA, pltpu.SemaphoreType.DMA]),
        out_shape=jax.ShapeDtypeStruct((cap, D_MODEL), F32),
        compiler_params=_cparams(("arbitrary",)),
        name="moe_dispatch",
    )(seg_start, counts, meta, slot_t, h_main, h_tail)


def _swiglu(x, wg, wu, wd):
    xb = x.astype(BF16)
    g = jnp.dot(xb, wg.astype(BF16), preferred_element_type=F32)
    u = jnp.dot(xb, wu.astype(BF16), preferred_element_type=F32)
    return jnp.dot((_silu(g) * u).astype(BF16), wd.astype(BF16), preferred_element_type=F32)


EXPERT_RING = 3


def _experts_kernel(be_ref, used_ref, xs_ref, wg_ref, wu_ref, wd_ref, y_ref, xbuf, sems):
    i = pl.program_id(0)
    n = pl.num_programs(0)

    def block_copy(blk):
        slot = lax.rem(blk, EXPERT_RING)
        rows = pl.ds(pl.multiple_of(blk * MOE_BLOCK, MOE_BLOCK), MOE_BLOCK)
        return pltpu.make_async_copy(xs_ref.at[rows], xbuf.at[slot], sems.at[slot])

    @pl.when(i == 0)
    def _():
        for j in range(EXPERT_RING - 1):
            block_copy(jnp.int32(j)).start()

    @pl.when(i + EXPERT_RING - 1 < n)
    def _():
        block_copy(i + EXPERT_RING - 1).start()

    block_copy(i).wait()

    @pl.when(i < used_ref[0])
    def _():
        y_ref[...] = _swiglu(xbuf[lax.rem(i, EXPERT_RING)], wg_ref[0], wu_ref[0], wd_ref[0])

    @pl.when(i >= used_ref[0])
    def _():
        y_ref[...] = jnp.zeros_like(y_ref)


def _experts(xs, block_expert, used_blocks, w_gate, w_up, w_down):
    cap = xs.shape[0]
    return pl.pallas_call(
        _experts_kernel,
        grid_spec=pltpu.PrefetchScalarGridSpec(
            num_scalar_prefetch=2,
            grid=(cap // MOE_BLOCK,),
            in_specs=[pl.BlockSpec(memory_space=pl.ANY),
                      pl.BlockSpec((1, D_MODEL, D_EXPERT), lambda i, be, used: (be[i], 0, 0)),
                      pl.BlockSpec((1, D_MODEL, D_EXPERT), lambda i, be, used: (be[i], 0, 0)),
                      pl.BlockSpec((1, D_EXPERT, D_MODEL), lambda i, be, used: (be[i], 0, 0))],
            out_specs=pl.BlockSpec((MOE_BLOCK, D_MODEL), lambda i, be, used: (i, 0)),
            scratch_shapes=[pltpu.VMEM((EXPERT_RING, MOE_BLOCK, D_MODEL), F32),
                            pltpu.SemaphoreType.DMA((EXPERT_RING,))]),
        out_shape=jax.ShapeDtypeStruct((cap, D_MODEL), F32),
        compiler_params=_cparams(("arbitrary",)),
        name="moe_experts",
    )(block_expert, used_blocks, xs, w_gate, w_up, w_down)


def _combine_kernel(start_ref, meta_ref, h_ref, ht_ref, info_ref, sg_ref, su_ref, sd_ref, g_ref, b_ref,
                    ys_ref, o_ref, ot_ref, buf_ref, sem, *, n_main):
    i = pl.program_id(0)
    tm = MOE_TOKENS

    @pl.when(i == 0)
    def _():
        buf_ref[...] = jnp.zeros_like(buf_ref)

    def fetch_expert(e, started):
        n = meta_ref[0, 0, e]
        return _start_run(ys_ref, start_ref[e] + meta_ref[0, 2, e], buf_ref, meta_ref[0, 1, e], n, sem, started)

    started = lax.fori_loop(0, N_EXPERTS, fetch_expert, tuple(jnp.int32(0) for _ in RUN_CHUNKS))
    h = _token_tile(i, n_main, h_ref, ht_ref)
    f = _swiglu(h, sg_ref[...], su_ref[...], sd_ref[...])
    info = info_ref[...]
    scol = lax.broadcasted_iota(jnp.int32, (tm, TILE_SLOTS), 1).astype(F32)
    mix = jnp.zeros((tm, TILE_SLOTS), F32)
    for k in range(TOP_K):
        mix = mix + jnp.where(info[:, ROUTE_ROWS + k:ROUTE_ROWS + k + 1] == scol, info[:, k:k + 1], 0.0)
    _wait_runs(ys_ref, buf_ref, sem, started)
    acc = jnp.dot(mix.astype(BF16), buf_ref[...].astype(BF16), preferred_element_type=F32)
    out = _layer_norm(DEEPNORM_ALPHA * h + (acc + f), g_ref[...], b_ref[...])

    @pl.when(i < n_main)
    def _():
        o_ref[...] = out

    @pl.when(i >= n_main)
    def _():
        ot_ref[...] = out


def _combine(h_main, h_tail, ys, meta, tokinfo, seg_start, sh_gate, sh_up, sh_down, ln_g, ln_b):
    tm = MOE_TOKENS
    n_main = h_main.shape[0] // tm
    const = lambda a: pl.BlockSpec(a.shape, lambda i, *_: (0,) * a.ndim)
    return pl.pallas_call(
        functools.partial(_combine_kernel, n_main=n_main),
        grid_spec=pltpu.PrefetchScalarGridSpec(
            num_scalar_prefetch=1,
            grid=(n_main + 1,),
            in_specs=[pl.BlockSpec((1, SUBLANES, LANES), lambda i, *_: (i, 0, 0), memory_space=pltpu.SMEM),
                      *_token_tile_specs(n_main),
                      pl.BlockSpec((tm, LANES), lambda i, *_: (i, 0)),
                      const(sh_gate), const(sh_up), const(sh_down), const(ln_g), const(ln_b),
                      pl.BlockSpec(memory_space=pl.ANY)],
            out_specs=list(_token_tile_specs(n_main)),
            scratch_shapes=[pltpu.VMEM((TILE_SLOTS, D_MODEL), F32), pltpu.SemaphoreType.DMA]),
        out_shape=[jax.ShapeDtypeStruct(h_main.shape, F32), jax.ShapeDtypeStruct((tm, D_MODEL), F32)],
        compiler_params=_cparams(("arbitrary",)),
        name="moe_combine",
    )(seg_start, meta, h_main, h_tail, tokinfo, sh_gate, sh_up, sh_down, ln_g, ln_b, ys)


def _moe_ln(h_main, h_tail, n_valid, router_w, router_bias, w_gate, w_up, w_down, sh_gate, sh_up, sh_down, ln_g, ln_b):
    n_tiles = h_main.shape[0] // MOE_TOKENS + 1
    slot_t, tokinfo, meta, cnt = _route(h_main, h_tail, router_w.T.astype(BF16),
                                        jnp.broadcast_to(router_bias.astype(F32)[:, None], (N_EXPERTS, LANES)), n_valid)
    counts = cnt[:, 0].astype(jnp.int32)
    padded = (counts + MOE_BLOCK - 1) // MOE_BLOCK * MOE_BLOCK
    seg_end = jnp.cumsum(padded)
    seg_start = seg_end - padded
    run_pad = n_tiles * N_EXPERTS * (SUBLANES - 1)
    n_blocks = -(-(n_valid * TOP_K + run_pad + N_EXPERTS * (MOE_BLOCK - 1)) // MOE_BLOCK)
    cap = n_blocks * MOE_BLOCK
    block_first_row = jnp.arange(n_blocks, dtype=jnp.int32) * MOE_BLOCK
    block_expert = jnp.minimum(jnp.sum((seg_end[None, :] <= block_first_row[:, None]).astype(jnp.int32), axis=1),
                               N_EXPERTS - 1)
    xs = _dispatch(h_main, h_tail, slot_t, meta, seg_start, counts, cap)
    used_blocks = (seg_end[N_EXPERTS - 1:] // MOE_BLOCK).astype(jnp.int32)
    ys = _experts(xs, block_expert, used_blocks, w_gate, w_up, w_down)
    return _combine(h_main, h_tail, ys, meta, tokinfo, seg_start, sh_gate.astype(BF16), sh_up.astype(BF16),
                    sh_down.astype(BF16), ln_g, ln_b)


def _ssd_sample_kernel(xbc_ref, z_ref, dtg_ref, sconv_ref, s0_ref, convw_ref, convb_ref, dtb_ref, alog_ref,
                       dskip_ref, normw_ref, y_ref, s_ref, conv_out_ref, xc_ref, dt_ref, da_ref):
    b = pl.program_id(0)

    @pl.when(b == 0)
    def _():
        xin = xbc_ref[...]
        xc = convw_ref[SSD_CONV - 1:SSD_CONV, :] * xin
        for k in range(SSD_CONV - 1):
            xc = xc + convw_ref[k:k + 1, :] * sconv_ref[k]
        xc_ref[...] = _silu(xc + convb_ref[...])
        dt = _softplus(dtg_ref[...] + dtb_ref[...])
        dt_ref[...] = dt
        da_ref[...] = jnp.exp(dt * (-jnp.exp(alog_ref[...])))
        for k in range(SSD_CONV - 2):
            conv_out_ref[k] = sconv_ref[k + 1]
        conv_out_ref[SSD_CONV - 2] = xin

    xc = xc_ref[pl.ds(b, 1), :]
    dt = dt_ref[pl.ds(b, 1), :]
    da = da_ref[pl.ds(b, 1), :]
    ns = SSD_GROUPS * SSD_STATE
    eye = (lax.broadcasted_iota(jnp.int32, (HEAD_DIM, HEAD_DIM), 0)
           == lax.broadcasted_iota(jnp.int32, (HEAD_DIM, HEAD_DIM), 1))
    hpg = SSD_HEADS // SSD_GROUPS
    y_parts = []
    for h in range(SSD_HEADS):
        g = h // hpg
        x_h = xc[:, h * HEAD_DIM:(h + 1) * HEAD_DIM]
        b_g = xc[:, SSD_D + g * SSD_STATE:SSD_D + (g + 1) * SSD_STATE]
        c_g = xc[:, SSD_D + ns + g * SSD_STATE:SSD_D + ns + (g + 1) * SSD_STATE]
        xdt_col = jnp.sum(jnp.where(eye, x_h * dt[:, h:h + 1], 0.0), axis=1, keepdims=True)
        s_new = da[:, h:h + 1] * s0_ref[0, h] + xdt_col * b_g
        s_ref[0, h] = s_new
        y_h = _bdot_nt(c_g, s_new) + dskip_ref[:, h * HEAD_DIM:(h + 1) * HEAD_DIM] * x_h
        y_parts.append(y_h)
    y = jnp.concatenate(y_parts, axis=1)
    y_ref[pl.ds(b, 1), :] = _gated_group_norm(y, z_ref[pl.ds(b, 1), :], normw_ref[...])


def _ssd_sample(xbc, z, dtg, state_conv_t, state_ssm, conv_w, conv_b, dt_bias_pad, a_log_pad, d_skip_full, norm_w):
    bs = xbc.shape[0]
    const = lambda a: pl.BlockSpec(a.shape, lambda b: (0,) * a.ndim)
    state_spec = pl.BlockSpec((1, SSD_HEADS, HEAD_DIM, SSD_STATE), lambda b: (b, 0, 0, 0))
    return pl.pallas_call(
        _ssd_sample_kernel,
        grid=(bs,),
        in_specs=[const(xbc), const(z), const(dtg), const(state_conv_t), state_spec, const(conv_w), const(conv_b),
                  const(dt_bias_pad), const(a_log_pad), const(d_skip_full), const(norm_w)],
        out_specs=[pl.BlockSpec((bs, SSD_D), lambda b: (0, 0)), state_spec,
                   pl.BlockSpec((SSD_CONV - 1, bs, SSD_CONV_CH), lambda b: (0, 0, 0))],
        out_shape=[jax.ShapeDtypeStruct((bs, SSD_D), F32),
                   jax.ShapeDtypeStruct(state_ssm.shape, F32),
                   jax.ShapeDtypeStruct((SSD_CONV - 1, bs, SSD_CONV_CH), F32)],
        scratch_shapes=[pltpu.VMEM((bs, SSD_CONV_CH), F32), pltpu.VMEM((bs, LANES), F32),
                        pltpu.VMEM((bs, LANES), F32)],
        compiler_params=_cparams(("arbitrary",)),
        name="ssd_sample",
    )(xbc, z, dtg, state_conv_t, state_ssm, conv_w, conv_b, dt_bias_pad, a_log_pad, d_skip_full, norm_w)


SEL_PAST = TOP_N - 1
BLOCKS_PER_PAGE = PAGE_SIZE // CMP_BLOCK
KV_FEATS = 2 * KV_D


def _compress_consts_t(cmp_pe, cmp_w1, cmp_b1, cmp_w2, cmp_b2):
    pe_t = jnp.stack([jnp.tile(cmp_pe[k].T, (1, BLOCKS_PER_PAGE)) for k in range(2)])
    w1_t = jnp.stack([_block_diag2(jnp.swapaxes(cmp_w1[k], 0, 1)) for k in range(2)]).astype(BF16)
    b1_t = jnp.stack([jnp.tile(cmp_b1[k], BLOCKS_PER_PAGE) for k in range(2)])[:, None, :]
    w2_t = jnp.stack([_block_diag2(cmp_w2[k]) for k in range(2)]).astype(BF16)
    b2_t = jnp.stack([jnp.tile(cmp_b2[k], BLOCKS_PER_PAGE) for k in range(2)])[:, None, :]
    return pe_t, w1_t, b1_t, w2_t, b2_t


def _compress_pages_kernel(pt_ref, pe_ref, w1_ref, b1_ref, w2_ref, b2_ref, pool_ref, o_ref, kbuf, vbuf, sems, *,
                           n_pages):
    b = pl.program_id(0)
    nb = pl.num_programs(0)
    bufs = (kbuf, vbuf)

    def half_copy(seq, kind, p):
        return pltpu.make_async_copy(pool_ref.at[pt_ref[seq, p], pl.ds(kind * KV_D, KV_D)],
                                     bufs[kind].at[pl.ds(pl.multiple_of(p * KV_D, KV_D), KV_D)], sems.at[kind])

    def start_half(seq, kind):
        lax.fori_loop(0, n_pages, lambda p, c: (half_copy(seq, kind, p).start(), c)[1], 0)

    def wait_half(seq, kind):
        lax.fori_loop(0, n_pages, lambda p, c: (half_copy(seq, kind, p).wait(), c)[1], 0)

    @pl.when(b == 0)
    def _():
        start_half(b, 0)
        start_half(b, 1)

    for kind in range(2):
        wait_half(b, kind)
        for h in range(NSA_KV_HEADS):
            def add_feature(d, acc, kind=kind, h=h):
                x = bufs[kind][pl.ds(h * HEAD_DIM + d, n_pages, stride=KV_D), :] + pe_ref[kind, pl.ds(d, 1), :]
                return acc + jnp.dot(x.astype(BF16), w1_ref[kind, d], preferred_element_type=F32)

            acc = lax.fori_loop(0, HEAD_DIM, add_feature,
                                jnp.zeros((n_pages, BLOCKS_PER_PAGE * CMP_HIDDEN), F32), unroll=8)
            hid = _silu(acc + b1_ref[kind])
            o_ref[0, kind * NSA_KV_HEADS + h] = (
                jnp.dot(hid.astype(BF16), w2_ref[kind], preferred_element_type=F32) + b2_ref[kind])

        @pl.when(b + 1 < nb)
        def _(kind=kind):
            start_half(b + 1, kind)


def _compress_pages(pool_t, page_table, consts):
    bs, n_pages = page_table.shape
    const = lambda a: pl.BlockSpec(a.shape, lambda b, pt: (0,) * a.ndim)
    return pl.pallas_call(
        functools.partial(_compress_pages_kernel, n_pages=n_pages),
        grid_spec=pltpu.PrefetchScalarGridSpec(
            num_scalar_prefetch=1,
            grid=(bs,),
            in_specs=[const(a) for a in consts] + [pl.BlockSpec(memory_space=pl.ANY)],
            out_specs=pl.BlockSpec((1, 2 * NSA_KV_HEADS, n_pages, LANES), lambda b, pt: (b, 0, 0, 0)),
            scratch_shapes=[pltpu.VMEM((n_pages * KV_D, PAGE_SIZE), F32), pltpu.VMEM((n_pages * KV_D, PAGE_SIZE), F32),
                            pltpu.SemaphoreType.DMA((2,))]),
        out_shape=jax.ShapeDtypeStruct((bs, 2 * NSA_KV_HEADS, n_pages, LANES), F32),
        compiler_params=_cparams(("arbitrary",)),
        name="compress_pages",
    )(page_table, *consts, pool_t)


def _group_heads(q_row, hk):
    hpg = NSA_HEADS // NSA_KV_HEADS
    low = lax.broadcasted_iota(jnp.int32, (1, LANES), 1) < HEAD_DIM
    rows = []
    for r in range(hpg):
        head = hk * hpg + r
        tile = q_row[:, (head // 2) * LANES:(head // 2 + 1) * LANES]
        if head % 2 == 1:
            tile = pltpu.roll(tile, HEAD_DIM, 1)
        rows.append(jnp.where(low, tile, 0.0))
    return jnp.concatenate(rows + [jnp.zeros((SUBLANES - hpg, LANES), F32)], axis=0)


def _spread_heads(o_groups):
    hpg = NSA_HEADS // NSA_KV_HEADS
    return jnp.concatenate([o[r:r + 1, 0:HEAD_DIM] for o in o_groups for r in range(hpg)], axis=1)


def _nsa_sample_cmp_t_kernel(qc_ref, cmp_ref, ocmp_ref, idx_ref, *, n_pages):
    b = pl.program_id(0)
    nc = n_pages * BLOCKS_PER_PAGE
    scale = HEAD_DIM ** -0.5
    hpg = NSA_HEADS // NSA_KV_HEADS
    q_row = qc_ref[pl.ds(b, 1), :] * scale
    lane = lax.broadcasted_iota(jnp.int32, (1, LANES), 1)
    pos_r = lax.broadcasted_iota(jnp.int32, (1, nc), 1)
    bid_r = (pos_r % n_pages) * BLOCKS_PER_PAGE + pos_r // n_pages
    pos_c = lax.broadcasted_iota(jnp.int32, (nc, 1), 0)
    bid_c = (pos_c % n_pages) * BLOCKS_PER_PAGE + pos_c // n_pages
    o_groups = []
    for hk in range(NSA_KV_HEADS):
        kc = cmp_ref[0, hk].astype(BF16)
        vc = cmp_ref[0, NSA_KV_HEADS + hk].astype(BF16)
        qh = _group_heads(q_row, hk)
        s = jnp.concatenate(
            [lax.dot_general(pltpu.roll(qh, c * HEAD_DIM, 1).astype(BF16) if c else qh.astype(BF16), kc,
                             (((1,), (1,)), ((), ())), preferred_element_type=F32)
             for c in range(BLOCKS_PER_PAGE)], axis=1)
        ex = jnp.exp(s - jnp.max(s, axis=-1, keepdims=True))
        p = ex / jnp.sum(ex, axis=-1, keepdims=True)
        o = jnp.dot(p[:, 0:n_pages].astype(BF16), vc, preferred_element_type=F32)
        for c in range(1, BLOCKS_PER_PAGE):
            oc = jnp.dot(p[:, c * n_pages:(c + 1) * n_pages].astype(BF16), vc, preferred_element_type=F32)
            o = o + pltpu.roll(oc, LANES - c * HEAD_DIM, 1)
        o_groups.append(o)
        hrow = lax.broadcasted_iota(jnp.int32, p.shape, 0) < hpg
        imp = jnp.sum(jnp.where(hrow, p, 0.0), axis=0, keepdims=True)
        score = jnp.where((bid_r == 0) | (bid_r == nc - 1), FORCED_SCORE, imp)
        score_col = jnp.concatenate([score, jnp.zeros((LANES - 1, nc), F32)], 0).T[:, 0:1]
        beats = (score_col > score) | ((score_col == score) & (bid_c < bid_r))
        rank = jnp.sum(beats.astype(F32), axis=0, keepdims=True)
        row = jnp.zeros((1, LANES), F32)
        bid_f = bid_r.astype(F32)
        for k in range(SEL_PAST):
            blk = jnp.sum(jnp.where(rank == k, bid_f, 0.0), axis=1, keepdims=True)
            row = jnp.where(lane == k, blk, row)
        idx_ref[pl.ds(b * NSA_KV_HEADS + hk, 1), :] = row.astype(jnp.int32)
    ocmp_ref[pl.ds(b, 1), :] = _spread_heads(o_groups)


def _nsa_sample_cmp_t(qc, kvcmp_t):
    bs, _, n_pages, _ = kvcmp_t.shape
    return pl.pallas_call(
        functools.partial(_nsa_sample_cmp_t_kernel, n_pages=n_pages),
        grid=(bs,),
        in_specs=[pl.BlockSpec((bs, NSA_D), lambda b: (0, 0)),
                  pl.BlockSpec((1, 2 * NSA_KV_HEADS, n_pages, LANES), lambda b: (b, 0, 0, 0))],
        out_specs=[pl.BlockSpec((bs, NSA_D), lambda b: (0, 0)),
                   pl.BlockSpec((bs * NSA_KV_HEADS, LANES), lambda b: (0, 0))],
        out_shape=[jax.ShapeDtypeStruct((bs, NSA_D), F32),
                   jax.ShapeDtypeStruct((bs * NSA_KV_HEADS, LANES), jnp.int32)],
        compiler_params=_cparams(("arbitrary",)),
        name="nsa_sample_cmp",
    )(qc, kvcmp_t)


def _sel_block_copies(pool_ref, pt_ref, sel_ref, kbuf, vbuf, sem, b, hk, k):
    blk = sel_ref[b * NSA_KV_HEADS + hk, k]
    page = pt_ref[b, lax.shift_right_logical(blk, int(math.log2(BLOCKS_PER_PAGE)))]
    j = hk * SEL_PAST + k
    return (pltpu.make_async_copy(pool_ref.at[page, pl.ds(hk * HEAD_DIM, HEAD_DIM)], kbuf.at[j], sem),
            pltpu.make_async_copy(pool_ref.at[page, pl.ds(KV_D + hk * HEAD_DIM, HEAD_DIM)], vbuf.at[j], sem))


def _nsa_sample_attn_t_kernel(pt_ref, sel_ref, qr_ref, new_sel_ref, new_win_ref, win_ref, dtg_ref, ocmp_ref,
                              pool_ref, o_ref, kbuf, vbuf, sem):
    b = pl.program_id(0)
    for hk in range(NSA_KV_HEADS):
        for k in range(SEL_PAST):
            for cp in _sel_block_copies(pool_ref, pt_ref, sel_ref, kbuf, vbuf, sem, b, hk, k):
                cp.start()
    for hk in range(NSA_KV_HEADS):
        for k in range(SEL_PAST):
            for cp in _sel_block_copies(pool_ref, pt_ref, sel_ref, kbuf, vbuf, sem, b, hk, k):
                cp.wait()
    scale = HEAD_DIM ** -0.5
    q_row = qr_ref[pl.ds(b, 1), :] * scale
    sig = _sigmoid(dtg_ref[pl.ds(b, 1), :])
    lane = lax.broadcasted_iota(jnp.int32, (1, PAGE_SIZE), 1)
    o_slc, o_win = [], []
    for hk in range(NSA_KV_HEADS):
        qh = _group_heads(q_row, hk)[:, 0:HEAD_DIM].astype(BF16)

        def new_row(ref, kind):
            t = ref[pl.ds(b, 1), :][:, kind * KV_D:(kind + 1) * KV_D]
            if hk == 1:
                t = pltpu.roll(t, HEAD_DIM, 1)
            return t[:, 0:HEAD_DIM].astype(BF16).astype(F32)

        def attend(kt, vt, mask, new_ref, n_new):
            s = jnp.dot(qh, kt.astype(BF16), preferred_element_type=F32)
            if mask is not None:
                s = jnp.where(mask, s, NEG)
            s_new = jnp.sum(qh.astype(F32) * new_row(new_ref, 0), axis=1, keepdims=True)
            m = jnp.maximum(jnp.max(s, axis=-1, keepdims=True), s_new)
            ex = jnp.exp(s - m)
            ex_new = jnp.exp(s_new - m) * n_new
            den = jnp.sum(ex, axis=-1, keepdims=True) + ex_new
            o = lax.dot_general((ex / den).astype(BF16), vt.astype(BF16), (((1,), (1,)), ((), ())),
                                preferred_element_type=F32)
            return o + (ex_new / den).astype(BF16).astype(F32) * new_row(new_ref, 1)

        kt = jnp.concatenate([kbuf[hk * SEL_PAST + k] for k in range(SEL_PAST)], axis=1)
        vt = jnp.concatenate([vbuf[hk * SEL_PAST + k] for k in range(SEL_PAST)], axis=1)
        mask = jnp.concatenate(
            [lane // SEL_BLOCK == (sel_ref[b * NSA_KV_HEADS + hk, k] & (BLOCKS_PER_PAGE - 1))
             for k in range(SEL_PAST)], axis=1)
        o_slc.append(attend(kt, vt, mask, new_sel_ref, float(SEL_BLOCK)))
        o_win.append(attend(win_ref[0, hk * HEAD_DIM:(hk + 1) * HEAD_DIM, :],
                            win_ref[0, KV_D + hk * HEAD_DIM:KV_D + (hk + 1) * HEAD_DIM, :], None, new_win_ref, 1.0))
    gates = []
    for br in range(3):
        gates.append(jnp.concatenate(
            [jnp.broadcast_to(sig[:, GATE_COL0 + h * 3 + br:GATE_COL0 + h * 3 + br + 1], (1, HEAD_DIM))
             for h in range(NSA_HEADS)], axis=1))
    o_ref[pl.ds(b, 1), :] = (gates[0] * ocmp_ref[pl.ds(b, 1), :] + gates[1] * _spread_heads(o_slc)
                             + gates[2] * _spread_heads(o_win))


def _nsa_sample_attn_t(qr, new_sel, new_win, win_t, dtg, o_cmp, pool_sel_t, page_table, sel_idx):
    bs = qr.shape[0]
    const = lambda a: pl.BlockSpec(a.shape, lambda b, pt, sel: (0,) * a.ndim)
    n_buf = NSA_KV_HEADS * SEL_PAST
    return pl.pallas_call(
        _nsa_sample_attn_t_kernel,
        grid_spec=pltpu.PrefetchScalarGridSpec(
            num_scalar_prefetch=2,
            grid=(bs,),
            in_specs=[const(qr), const(new_sel), const(new_win),
                      pl.BlockSpec((1,) + win_t.shape[1:], lambda b, pt, sel: (b, 0, 0)),
                      const(dtg), const(o_cmp), pl.BlockSpec(memory_space=pl.ANY)],
            out_specs=pl.BlockSpec((bs, NSA_D), lambda b, pt, sel: (0, 0)),
            scratch_shapes=[pltpu.VMEM((n_buf, HEAD_DIM, PAGE_SIZE), F32), pltpu.VMEM((n_buf, HEAD_DIM, PAGE_SIZE), F32),
                            pltpu.SemaphoreType.DMA]),
        out_shape=jax.ShapeDtypeStruct((bs, NSA_D), F32),
        compiler_params=_cparams(("arbitrary",)),
        name="nsa_sample_attn",
    )(page_table, sel_idx, qr, new_sel, new_win, win_t, dtg, o_cmp, pool_sel_t)


def kernel(x_prompt, x_sample, cache_kv_cmp, cache_kv_sel, page_table, cache_kv_win, state_ssm, state_conv,
           emb_ln_g, emb_ln_b, w_in, conv_w, conv_b, dt_bias, a_log, d_skip, ssd_norm_w,
           cmp_pe, cmp_w1, cmp_b1, cmp_w2, cmp_b2, w_out, ln1_g, ln1_b,
           router_w, router_bias, exp_w_gate, exp_w_up, exp_w_down,
           sh_w_gate, sh_w_up, sh_w_down, ln2_g, ln2_b):
    bp, tp, _ = x_prompt.shape
    bs, ts, _ = x_sample.shape
    assert ts == 1 and DEPTH == 1
    n_prompt = bp * tp
    past_len = page_table.shape[1] * PAGE_SIZE
    l = 0
    w_perm = _permute_w_in(w_in[l])
    ln0_g, ln0_b = emb_ln_g[None], emb_ln_b[None]
    ssd_consts = (conv_w[l], conv_b[l][None], _pad_lanes(dt_bias[l]), _pad_lanes(a_log[l]),
                  jnp.repeat(d_skip[l], HEAD_DIM)[None], ssd_norm_w[l][None])
    cmp_consts = _compress_consts(cmp_pe[l], cmp_w1[l], cmp_b1[l], cmp_w2[l], cmp_b2[l])
    w_o = w_out[l].astype(BF16)
    w_o_ssd, w_o_nsa = w_o[:SSD_D], w_o[SSD_D:]
    ln1 = (ln1_g[l][None], ln1_b[l][None])
    kv_shape = (2, NSA_KV_HEADS, HEAD_DIM)

    hp, z, xbc, qc, qr, kvc, kvs, kvw, dtg, kvc_t, kvs_t, kvw_t = _inproj(
        x_prompt.reshape(n_prompt, D_MODEL), ln0_g, ln0_b, w_perm, _rope_tables(jnp.arange(tp)), 256,
        _rope_tables_t(jnp.arange(tp)))
    y_ssd, ssm_p, conv_p = _ssd_prompt(xbc, z, dtg, *ssd_consts, bp, tp)
    kvcmp = _compress_prompt(kvc, cmp_consts, tp)
    y_nsa = _nsa_prompt(qc, qr, dtg, kvcmp, kvs, kvw, bp, tp)
    h1p = _outproj(y_ssd, y_nsa, hp, w_o_ssd, w_o_nsa, *ln1, 256)
    n_keep = min(WINDOW, tp)
    cache_leaf = lambda a: jnp.transpose(a.reshape((bp,) + kv_shape + (a.shape[-1],)), (0, 4, 1, 2, 3))[None]
    kvc_p = cache_leaf(kvc_t)
    kvs_p = cache_leaf(kvs_t)
    kvw_p = cache_leaf(kvw_t[:, :, tp - n_keep:])

    s_hs, s_z, s_xbc, s_qc, s_qr, s_kvc, s_kvs, s_kvw, s_dtg = _inproj(
        x_sample.reshape(bs, D_MODEL), ln0_g, ln0_b, w_perm, _rope_tables(jnp.full((bs,), past_len)), bs)
    s_y_ssd, ssm_s, conv_s_t = _ssd_sample(s_xbc, s_z, s_dtg, jnp.swapaxes(state_conv[l], 0, 1), state_ssm[l],
                                           *ssd_consts)
    n_pool = cache_kv_cmp.shape[1]
    feature_major = lambda c, rows: jnp.swapaxes(c.reshape(-1, rows, 2 * KV_D), 1, 2)
    s_kvcmp = _compress_pages(feature_major(cache_kv_cmp[l], PAGE_SIZE), page_table,
                              _compress_consts_t(cmp_pe[l], cmp_w1[l], cmp_b1[l], cmp_w2[l], cmp_b2[l]))
    s_o_cmp, s_sel = _nsa_sample_cmp_t(s_qc, s_kvcmp)
    buf_win = cache_kv_win[l].reshape(bs, -1, 2 * KV_D)
    s_y_nsa = _nsa_sample_attn_t(
        s_qr, s_kvs, s_kvw, feature_major(cache_kv_win[l], buf_win.shape[1]), s_dtg, s_o_cmp,
        feature_major(cache_kv_sel[l], PAGE_SIZE), page_table, s_sel)
    h1s = _outproj(s_y_ssd, s_y_nsa, s_hs, w_o_ssd, w_o_nsa, *ln1, bs)
    win_all = jnp.concatenate([buf_win, s_kvw[:, None, :]], 1)
    n_keep_s = min(WINDOW, past_len + ts)
    kvw_s = win_all[:, win_all.shape[1] - n_keep_s:].reshape((1, bs, n_keep_s) + kv_shape)
    kvc_s = s_kvc.reshape((1, bs, ts) + kv_shape)
    kvs_s = s_kvs.reshape((1, bs, ts) + kv_shape)

    assert n_prompt % MOE_TOKENS == 0 and bs * ts <= MOE_TOKENS
    n_tok = n_prompt + bs * ts
    tail = jnp.concatenate([h1s, jnp.zeros((MOE_TOKENS - bs * ts, D_MODEL), F32)], 0)
    out_main, out_tail = _moe_ln(h1p, tail, n_tok, router_w[l], router_bias[l], exp_w_gate[l], exp_w_up[l],
                                 exp_w_down[l], sh_w_gate[l], sh_w_up[l], sh_w_down[l], ln2_g[l][None], ln2_b[l][None])
    y_prompt = out_main.reshape(bp, tp, D_MODEL)
    y_sample = out_tail[:bs * ts].reshape(bs, ts, D_MODEL)
    return (y_prompt, y_sample, kvc_p, kvs_p, kvw_p, ssm_p[None], conv_p[None],
            kvc_s, kvs_s, kvw_s, ssm_s[None], jnp.swapaxes(conv_s_t, 0, 1)[None])
```

```python
import functools
import math

import jax
import jax.numpy as jnp
import numpy as np
from jax import lax
from jax.experimental import pallas as pl
from jax.experimental.pallas import tpu as pltpu

D_MODEL = 1024
HEAD_DIM = 64
SSD_HEADS = 8
SSD_D = SSD_HEADS * HEAD_DIM
SSD_GROUPS = 2
SSD_STATE = 128
SSD_CONV = 4
SSD_CONV_CH = SSD_D + 2 * SSD_GROUPS * SSD_STATE
SSD_CHUNK = 128
NSA_HEADS = 8
NSA_KV_HEADS = 2
NSA_D = NSA_HEADS * HEAD_DIM
KV_D = NSA_KV_HEADS * HEAD_DIM
CMP_BLOCK = 64
CMP_HIDDEN = 128
SEL_BLOCK = 64
TOP_N = 16
WINDOW = 512
Q_BLOCK = 128
ROT_DIM = HEAD_DIM // 4
ROPE_THETA = 500000.0
N_EXPERTS = 64
TOP_K = 6
N_EXPERT_GROUPS = 8
EXPERTS_PER_GROUP = N_EXPERTS // N_EXPERT_GROUPS
TOPK_GROUPS = 4
D_EXPERT = 256
D_SHARED = 256
ROUTED_SCALE = 2.5
MOE_BLOCK = 512
DEPTH = 1
DEEPNORM_ALPHA = (2.0 * DEPTH) ** 0.25
LN_EPS = 1e-5
RMS_EPS = 1e-5
NEG = -1e30
FORCED_SCORE = 1e4
PAGE_SIZE = 128

LANES = 128
SUBLANES = 8
VMEM_LIMIT_BYTES = 56 * 1024 * 1024

U_Z = 0
U_XBC = U_Z + SSD_D
U_Q = U_XBC + SSD_CONV_CH
U_KVC = U_Q + NSA_D
U_KVS = U_KVC + 2 * KV_D
U_KVW = U_KVS + 2 * KV_D
U_DTG = U_KVW + 2 * KV_D
U_TOTAL = U_DTG + LANES
GATE_COL0 = SSD_HEADS

BF16 = jnp.bfloat16
F32 = jnp.float32


def _cparams(sem):
    return pltpu.CompilerParams(dimension_semantics=sem, vmem_limit_bytes=VMEM_LIMIT_BYTES)


def _bdot(a, b):
    return jnp.dot(a.astype(BF16), b.astype(BF16), preferred_element_type=F32)


def _bdot_nt(a, b):
    return lax.dot_general(a.astype(BF16), b.astype(BF16), (((1,), (1,)), ((), ())),
                           preferred_element_type=F32)


def _hdot(a, b):
    return jnp.dot(a, b, preferred_element_type=F32, precision=lax.Precision.HIGHEST)


def _sigmoid(x):
    return 1.0 / (1.0 + jnp.exp(-x))


def _silu(x):
    return x * _sigmoid(x)


def _layer_norm(x, g, b):
    mu = jnp.mean(x, axis=-1, keepdims=True)
    xc = x - mu
    var = jnp.mean(xc * xc, axis=-1, keepdims=True)
    return xc * lax.rsqrt(var + LN_EPS) * g + b


def _rope_tile(x, cos, sa, sb):
    return x * cos + pltpu.roll(x, LANES - ROT_DIM // 2, 1) * sa + pltpu.roll(x, ROT_DIM // 2, 1) * sb


def _rope_rows(x, cos, sin):
    half = ROT_DIM // 2
    parts = []
    for hd in range(NSA_KV_HEADS):
        r0 = hd * HEAD_DIM
        x1, x2 = x[r0:r0 + half], x[r0 + half:r0 + ROT_DIM]
        parts += [x1 * cos - x2 * sin, x2 * cos + x1 * sin, x[r0 + ROT_DIM:r0 + HEAD_DIM]]
    return jnp.concatenate(parts, axis=0)


def _inproj_kernel(x_ref, g_ref, b_ref, w_ref, rope_ref, *refs, feature_major):
    if feature_major:
        wkv_t_ref, rope_t_ref = refs[:2]
        refs = refs[2:]
    h_ref, z_ref, xbc_ref, qc_ref, qr_ref, kvc_ref, kvs_ref, kvw_ref, dtg_ref = refs[:9]
    h = _layer_norm(x_ref[...], g_ref[...], b_ref[...])
    h_ref[...] = h
    hb = h.astype(BF16)
    if feature_major:
        kvc_t_ref, kvs_t_ref, kvw_t_ref = refs[9:]
        ut = lax.dot_general(wkv_t_ref[...], hb, (((1,), (1,)), ((), ())), preferred_element_type=F32)
        half = ROT_DIM // 2
        cos_t, sin_t = rope_t_ref[0:half, :], rope_t_ref[half:2 * half, :]
        kvc_t_ref[0] = ut[0:2 * KV_D]
        kvs_t_ref[0, 0:KV_D] = _rope_rows(ut[2 * KV_D:3 * KV_D], cos_t, sin_t)
        kvs_t_ref[0, KV_D:2 * KV_D] = ut[3 * KV_D:4 * KV_D]
        kvw_t_ref[0, 0:KV_D] = _rope_rows(ut[4 * KV_D:5 * KV_D], cos_t, sin_t)
        kvw_t_ref[0, KV_D:2 * KV_D] = ut[5 * KV_D:6 * KV_D]
    u = jnp.dot(hb, w_ref[...], preferred_element_type=F32)
    cos = rope_ref[:, 0:LANES]
    sa = rope_ref[:, LANES:2 * LANES]
    sb = rope_ref[:, 2 * LANES:3 * LANES]
    z_ref[...] = u[:, U_Z:U_XBC]
    xbc_ref[...] = u[:, U_XBC:U_Q]
    qc_ref[...] = u[:, U_Q:U_KVC].astype(qc_ref.dtype)
    for c in range(NSA_D // LANES):
        qr_ref[:, c * LANES:(c + 1) * LANES] = _rope_tile(
            u[:, U_Q + c * LANES:U_Q + (c + 1) * LANES], cos, sa, sb).astype(qr_ref.dtype)
    kvc_ref[...] = u[:, U_KVC:U_KVS]
    kvs_ref[:, 0:KV_D] = _rope_tile(u[:, U_KVS:U_KVS + KV_D], cos, sa, sb)
    kvs_ref[:, KV_D:2 * KV_D] = u[:, U_KVS + KV_D:U_KVW]
    kvw_ref[:, 0:KV_D] = _rope_tile(u[:, U_KVW:U_KVW + KV_D], cos, sa, sb)
    kvw_ref[:, KV_D:2 * KV_D] = u[:, U_KVW + KV_D:U_DTG]
    dtg_ref[...] = u[:, U_DTG:U_TOTAL]


def _rope_tables(pos):
    half = ROT_DIM // 2
    inv = ROPE_THETA ** (-jnp.arange(half, dtype=F32) / half)
    ang = pos.astype(F32)[:, None] * inv
    cos, sin = jnp.cos(ang), jnp.sin(ang)
    ones = jnp.ones((pos.shape[0], HEAD_DIM - ROT_DIM), F32)
    zeros = jnp.zeros((pos.shape[0], HEAD_DIM - ROT_DIM), F32)
    zh = jnp.zeros_like(sin)
    c = jnp.concatenate([cos, cos, ones], 1)
    sa = jnp.concatenate([-sin, zh, zeros], 1)
    sb = jnp.concatenate([zh, sin, zeros], 1)
    return jnp.concatenate([jnp.tile(t, (1, LANES // HEAD_DIM)) for t in (c, sa, sb)], 1)


def _permute_w_in(w):
    sizes = (SSD_D, SSD_CONV_CH, SSD_HEADS, NSA_D, KV_D, KV_D, KV_D, KV_D, KV_D, KV_D, 3 * NSA_HEADS)
    offs = np.concatenate([[0], np.cumsum(sizes)])
    seg = [w[:, offs[i]:offs[i + 1]] for i in range(len(sizes))]
    pad = jnp.zeros((w.shape[0], LANES - SSD_HEADS - 3 * NSA_HEADS), w.dtype)
    out = jnp.concatenate([seg[0], seg[1], seg[3], seg[4], seg[5], seg[6], seg[7], seg[8], seg[9],
                           seg[2], seg[10], pad], 1)
    return out.astype(BF16)


def _rope_tables_t(pos):
    half = ROT_DIM // 2
    inv = ROPE_THETA ** (-jnp.arange(half, dtype=F32) / half)
    ang = inv[:, None] * pos.astype(F32)[None, :]
    return jnp.concatenate([jnp.cos(ang), jnp.sin(ang)], 0)


def _inproj(x, ln_g, ln_b, w_perm, rope_tab, tm, rope_tab_t=None):
    n = x.shape[0]
    nt = n // tm
    t = rope_tab.shape[0]
    n_rope_blocks = t // tm
    feature_major = rope_tab_t is not None
    row = lambda w: pl.BlockSpec((tm, w), lambda i: (i, 0))
    const = lambda a: pl.BlockSpec(a.shape, lambda i: (0,) * a.ndim)
    widths = (D_MODEL, SSD_D, SSD_CONV_CH, NSA_D, NSA_D, 2 * KV_D, 2 * KV_D, 2 * KV_D, LANES)
    in_specs = [row(D_MODEL), const(ln_g), const(ln_b), const(w_perm),
                pl.BlockSpec((tm, 3 * LANES), lambda i: (i % n_rope_blocks, 0))]
    out_specs = [row(w) for w in widths]
    dtypes = [BF16 if (feature_major and k in (3, 4)) else F32 for k in range(len(widths))]
    out_shape = [jax.ShapeDtypeStruct((n, w), d) for w, d in zip(widths, dtypes)]
    args = [x, ln_g, ln_b, w_perm, rope_tab]
    if feature_major:
        wkv_t = w_perm[:, U_KVC:U_DTG].T
        in_specs += [const(wkv_t), pl.BlockSpec((rope_tab_t.shape[0], tm), lambda i: (0, i % n_rope_blocks))]
        args += [wkv_t, rope_tab_t]
        out_specs += [pl.BlockSpec((1, 2 * KV_D, tm), lambda i: (i // n_rope_blocks, 0, i % n_rope_blocks))] * 3
        out_shape += [jax.ShapeDtypeStruct((n // t, 2 * KV_D, t), F32)] * 3
    return pl.pallas_call(
        functools.partial(_inproj_kernel, feature_major=feature_major),
        grid=(nt,),
        in_specs=in_specs,
        out_specs=out_specs,
        out_shape=out_shape,
        compiler_params=_cparams(("parallel",)),
        name="inproj",
    )(*args)


def _softplus(x):
    return jnp.maximum(x, 0.0) + jnp.log1p(jnp.exp(-jnp.abs(x)))


def _gated_group_norm(y, z, norm_w):
    y = y * _silu(z)
    gw = SSD_D // SSD_GROUPS
    parts = []
    for g in range(SSD_GROUPS):
        yg = y[:, g * gw:(g + 1) * gw]
        ms = jnp.mean(yg * yg, axis=-1, keepdims=True)
        parts.append(yg * lax.rsqrt(ms + RMS_EPS))
    return jnp.concatenate(parts, axis=1) * norm_w


def _ssd_prompt_kernel(xbc_ref, z_ref, dtg_ref, convw_ref, convb_ref, dtb_ref, alog_ref, dskip_ref, normw_ref,
                       y_ref, state_ref, conv_ref, ext_ref, s_ref):
    c = pl.program_id(1)
    nc = pl.num_programs(1)
    L = SSD_CHUNK
    halo = SUBLANES

    @pl.when(c == 0)
    def _():
        ext_ref[0:halo, :] = jnp.zeros((halo, SSD_CONV_CH), F32)
        s_ref[...] = jnp.zeros_like(s_ref)

    xin = xbc_ref[...]
    ext_ref[halo:halo + L, :] = xin
    xc = convw_ref[SSD_CONV - 1:SSD_CONV, :] * xin
    for k in range(SSD_CONV - 1):
        off = halo - (SSD_CONV - 1) + k
        xc = xc + convw_ref[k:k + 1, :] * ext_ref[off:off + L, :]
    ext_ref[0:halo, :] = ext_ref[L:L + halo, :]
    xc = _silu(xc + convb_ref[...])
    xs = xc[:, 0:SSD_D]
    ns = SSD_GROUPS * SSD_STATE
    bm = xc[:, SSD_D:SSD_D + ns]
    cm = xc[:, SSD_D + ns:SSD_D + 2 * ns]

    dt = _softplus(dtg_ref[...] + dtb_ref[...])
    da = dt * (-jnp.exp(alog_ref[...]))
    row = lax.broadcasted_iota(jnp.int32, (L, L), 0)
    col = lax.broadcasted_iota(jnp.int32, (L, L), 1)
    tril = row >= col
    acum = _hdot(tril.astype(F32), da)
    acum_t = acum.T
    eacum = jnp.exp(acum)
    alast = acum[L - 1:L, :]
    edecay = jnp.exp(alast - acum)
    elast = jnp.exp(alast)

    dt_full = jnp.concatenate([jnp.broadcast_to(dt[:, h:h + 1], (L, HEAD_DIM)) for h in range(SSD_HEADS)], 1)
    dec_full = jnp.concatenate([jnp.broadcast_to(edecay[:, h:h + 1], (L, HEAD_DIM)) for h in range(SSD_HEADS)], 1)
    xdt = xs * dt_full
    xdec_t = (xdt * dec_full).T

    hpg = SSD_HEADS // SSD_GROUPS
    y_parts = []
    for h in range(SSD_HEADS):
        g = h // hpg
        b_g = bm[:, g * SSD_STATE:(g + 1) * SSD_STATE]
        c_g = cm[:, g * SSD_STATE:(g + 1) * SSD_STATE]
        if h % hpg == 0:
            cb = _bdot_nt(c_g, b_g)
        seg = acum[:, h:h + 1] - acum_t[h:h + 1, :]
        lmat = jnp.where(tril, jnp.exp(jnp.where(tril, seg, 0.0)), 0.0)
        xdt_h = xdt[:, h * HEAD_DIM:(h + 1) * HEAD_DIM]
        y_h = _bdot(cb * lmat, xdt_h)
        s_prev = s_ref[h]
        y_h = y_h + _bdot_nt(c_g, s_prev) * eacum[:, h:h + 1]
        y_h = y_h + dskip_ref[:, h * HEAD_DIM:(h + 1) * HEAD_DIM] * xs[:, h * HEAD_DIM:(h + 1) * HEAD_DIM]
        y_parts.append(y_h)
        s_ref[h] = elast[:, h:h + 1] * s_prev + _bdot(xdec_t[h * HEAD_DIM:(h + 1) * HEAD_DIM, :], b_g)
    y = jnp.concatenate(y_parts, axis=1)
    y_ref[...] = _gated_group_norm(y, z_ref[...], normw_ref[...]).astype(y_ref.dtype)

    @pl.when(c == nc - 1)
    def _():
        state_ref[0] = s_ref[...]
        conv_ref[0] = xin[L - (SSD_CONV - 1):L, :]


def _ssd_prompt(xbc, z, dtg, conv_w, conv_b, dt_bias_pad, a_log_pad, d_skip_full, norm_w, bn, t):
    nc = t // SSD_CHUNK
    row = lambda w: pl.BlockSpec((SSD_CHUNK, w), lambda b, c: (b * nc + c, 0))
    const = lambda a: pl.BlockSpec(a.shape, lambda b, c: (0,) * a.ndim)
    return pl.pallas_call(
        _ssd_prompt_kernel,
        grid=(bn, nc),
        in_specs=[row(SSD_CONV_CH), row(SSD_D), row(LANES), const(conv_w), const(conv_b), const(dt_bias_pad),
                  const(a_log_pad), const(d_skip_full), const(norm_w)],
        out_specs=[row(SSD_D),
                   pl.BlockSpec((1, SSD_HEADS, HEAD_DIM, SSD_STATE), lambda b, c: (b, 0, 0, 0)),
                   pl.BlockSpec((1, SSD_CONV - 1, SSD_CONV_CH), lambda b, c: (b, 0, 0))],
        out_shape=[jax.ShapeDtypeStruct((bn * t, SSD_D), BF16),
                   jax.ShapeDtypeStruct((bn, SSD_HEADS, HEAD_DIM, SSD_STATE), F32),
                   jax.ShapeDtypeStruct((bn, SSD_CONV - 1, SSD_CONV_CH), F32)],
        scratch_shapes=[pltpu.VMEM((SSD_CHUNK + 2 * SUBLANES, SSD_CONV_CH), F32),
                        pltpu.VMEM((SSD_HEADS, HEAD_DIM, SSD_STATE), F32)],
        compiler_params=_cparams(("parallel", "arbitrary")),
        name="ssd_prompt",
    )(xbc, z, dtg, conv_w, conv_b, dt_bias_pad, a_log_pad, d_skip_full, norm_w)


def _pad_lanes(v, fill=0.0):
    return jnp.concatenate([v.astype(F32), jnp.full((LANES - v.shape[0],), fill, F32)])[None]


def _compress_rows(k_ref, v_ref, pe_ref, w1k_ref, w1v_ref, b1_ref, w2k_ref, w2v_ref, b2_ref, nb):
    acck = jnp.zeros((nb, 2 * CMP_HIDDEN), F32)
    accv = jnp.zeros((nb, 2 * CMP_HIDDEN), F32)
    for l in range(CMP_BLOCK):
        xk = k_ref[pl.ds(l, nb, stride=CMP_BLOCK), :] + pe_ref[l:l + 1, 0:KV_D]
        xv = v_ref[pl.ds(l, nb, stride=CMP_BLOCK), :] + pe_ref[l:l + 1, KV_D:2 * KV_D]
        acck = acck + jnp.dot(xk.astype(BF16), w1k_ref[l], preferred_element_type=F32)
        accv = accv + jnp.dot(xv.astype(BF16), w1v_ref[l], preferred_element_type=F32)
    hk = _silu(acck + b1_ref[:, 0:2 * CMP_HIDDEN])
    hv = _silu(accv + b1_ref[:, 2 * CMP_HIDDEN:4 * CMP_HIDDEN])
    ok = jnp.dot(hk.astype(BF16), w2k_ref[...], preferred_element_type=F32) + b2_ref[:, 0:KV_D]
    ov = jnp.dot(hv.astype(BF16), w2v_ref[...], preferred_element_type=F32) + b2_ref[:, KV_D:2 * KV_D]
    return jnp.concatenate([ok, ov], axis=1)


def _compress_kernel(k_ref, v_ref, pe_ref, w1k_ref, w1v_ref, b1_ref, w2k_ref, w2v_ref, b2_ref, o_ref, *, nb):
    o_ref[...] = _compress_rows(k_ref, v_ref, pe_ref, w1k_ref, w1v_ref, b1_ref, w2k_ref, w2v_ref, b2_ref, nb)


def _block_diag2(w):
    z = jnp.zeros_like(w)
    return jnp.concatenate([jnp.concatenate([w, z], -1), jnp.concatenate([z, w], -1)], -2)


def _compress_consts(cmp_pe, cmp_w1, cmp_b1, cmp_w2, cmp_b2):
    pe = jnp.concatenate([cmp_pe[0], cmp_pe[0], cmp_pe[1], cmp_pe[1]], -1)
    w1k = _block_diag2(cmp_w1[0]).astype(BF16)
    w1v = _block_diag2(cmp_w1[1]).astype(BF16)
    b1 = jnp.concatenate([cmp_b1[0], cmp_b1[0], cmp_b1[1], cmp_b1[1]])[None]
    w2k = _block_diag2(cmp_w2[0]).astype(BF16)
    w2v = _block_diag2(cmp_w2[1]).astype(BF16)
    b2 = jnp.concatenate([cmp_b2[0], cmp_b2[0], cmp_b2[1], cmp_b2[1]])[None]
    return pe, w1k, w1v, b1, w2k, w2v, b2


def _compress_prompt(kvc, consts, rows_per_step):
    n = kvc.shape[0]
    nb = rows_per_step // CMP_BLOCK
    const = lambda a: pl.BlockSpec(a.shape, lambda i: (0,) * a.ndim)
    return pl.pallas_call(
        functools.partial(_compress_kernel, nb=nb),
        grid=(n // rows_per_step,),
        in_specs=[pl.BlockSpec((rows_per_step, KV_D), lambda i: (i, 0)),
                  pl.BlockSpec((rows_per_step, KV_D), lambda i: (i, 1))] + [const(a) for a in consts],
        out_specs=pl.BlockSpec((nb, 2 * KV_D), lambda i: (i, 0)),
        out_shape=jax.ShapeDtypeStruct((n // CMP_BLOCK, 2 * KV_D), F32),
        compiler_params=_cparams(("parallel",)),
        name="compress_prompt",
    )(kvc, kvc, *consts)


SEL_KEY_TILE = 512
WIN_KEYS = WINDOW + Q_BLOCK


def _dup_head(x, hk):
    sw = pltpu.roll(x, HEAD_DIM, 1)
    low = lax.broadcasted_iota(jnp.int32, x.shape, 1) < HEAD_DIM
    return jnp.where(low, x, sw) if hk == 0 else jnp.where(low, sw, x)


def _masked_softmax(s, mask):
    sm = jnp.where(mask, s, NEG)
    ex = jnp.where(mask, jnp.exp(sm - jnp.max(sm, axis=-1, keepdims=True)), 0.0)
    den = jnp.sum(ex, axis=-1, keepdims=True)
    return ex / jnp.where(den > 0.0, den, 1.0)


def _select_blocks_t(imp, cur, n_top):
    j = lax.broadcasted_iota(jnp.int32, imp.shape, 0)
    future = j > cur
    forced = (j == 0) | (j == cur) | (j == cur - 1)
    score = jnp.where(future, NEG, jnp.where(forced, FORCED_SCORE, imp))
    return ((_rank_rows(score) < n_top) & (score > 0.5 * NEG)).astype(F32)


def _nsa_prompt_kernel(qc_ref, qr_ref, dtg_ref, cmp_ref, kvs_ref, kvw_ref, o_ref,
                       cmp_d, kvs_d, kvw_d, bias_ref, qrs_ref, m_ref, l_ref, acc_ref, *, t):
    qb = pl.program_id(1)
    nbk = t // SEL_BLOCK
    tq = Q_BLOCK
    tk = SEL_KEY_TILE
    hpg = NSA_HEADS // NSA_KV_HEADS
    scale = HEAD_DIM ** -0.5

    @pl.when(qb == 0)
    def _():
        cmp_d[...] = jnp.zeros_like(cmp_d)
        for src, dst, n in ((cmp_ref, cmp_d, nbk), (kvs_ref, kvs_d, t), (kvw_ref, kvw_d, t)):
            x = src[...]
            for hk in range(NSA_KV_HEADS):
                dst[hk, 0:n, 0:KV_D] = _dup_head(x[:, 0:KV_D], hk).astype(BF16)
                dst[hk, 0:n, KV_D:2 * KV_D] = _dup_head(x[:, KV_D:2 * KV_D], hk).astype(BF16)

    t0 = qb * tq
    rows = t0 + lax.broadcasted_iota(jnp.int32, (tq, 1), 0)
    lane = lax.broadcasted_iota(jnp.int32, (tq, LANES), 1)
    half_mask = (lane < HEAD_DIM, lane >= HEAD_DIM)
    sig = _sigmoid(dtg_ref[...])
    vis = (lane + 1) * CMP_BLOCK - 1 <= rows
    cur_l = (t0 + lax.broadcasted_iota(jnp.int32, (1, tq), 1)) // SEL_BLOCK
    expand = (lax.broadcasted_iota(jnp.int32, (LANES, t), 1) // SEL_BLOCK
              == lax.broadcasted_iota(jnp.int32, (LANES, t), 0)).astype(BF16)
    win_start = pl.multiple_of(jnp.maximum(t0 - WINDOW, 0), tq)
    wpos = win_start + lax.broadcasted_iota(jnp.int32, (tq, WIN_KEYS), 1)
    win_bias = jnp.where((wpos <= rows) & (wpos >= rows - WINDOW), 0.0, NEG)
    n_kt = (t0 + tq + tk - 1) // tk

    def stack_heads(ref, hk):
        parts = []
        for hh in range(hpg):
            head = hk * hpg + hh
            p, e = head // 2, head % 2
            parts.append(jnp.where(half_mask[e], ref[:, p * LANES:(p + 1) * LANES] * scale, 0.0))
        return jnp.concatenate(parts, axis=0).astype(BF16)

    o_cmp_g = []
    for hk in range(NSA_KV_HEADS):
        qcs = stack_heads(qc_ref, hk)
        s = lax.dot_general(qcs, cmp_d[hk, :, 0:KV_D], (((1,), (1,)), ((), ())), preferred_element_type=F32)
        pc = _masked_softmax(s.reshape(hpg, tq, LANES), vis[None])
        imp = jnp.sum(pc, axis=0)
        o_cmp_g.append(jnp.dot(pc.reshape(hpg * tq, LANES).astype(BF16), cmp_d[hk, :, KV_D:2 * KV_D],
                               preferred_element_type=F32).reshape(hpg, tq, LANES))

        sel_t = _select_blocks_t(imp.T[0:nbk, :], cur_l, TOP_N)
        sel = jnp.concatenate([sel_t, jnp.zeros((LANES - nbk, tq), F32)], axis=0).T
        selk = jnp.dot(sel.astype(BF16), expand, preferred_element_type=F32)
        for kt in range(t // tk):
            @pl.when(kt < n_kt)
            def _(kt=kt, hk=hk, selk=selk):
                kpos = kt * tk + lax.broadcasted_iota(jnp.int32, (tq, tk), 1)
                bias_ref[hk, kt] = jnp.where((selk[:, kt * tk:(kt + 1) * tk] > 0.5) & (kpos <= rows), 0.0, NEG)

        qrs_ref[hk] = stack_heads(qr_ref, hk)

    m_ref[...] = jnp.full(m_ref.shape, NEG, F32)
    l_ref[...] = jnp.zeros(l_ref.shape, F32)
    acc_ref[...] = jnp.zeros(acc_ref.shape, F32)

    def sel_step(kt, carry):
        k0 = pl.multiple_of(kt * tk, tk)
        for hk in range(NSA_KV_HEADS):
            kblk = kvs_d[hk, pl.ds(k0, tk), 0:KV_D]
            vblk = kvs_d[hk, pl.ds(k0, tk), KV_D:2 * KV_D]
            s = lax.dot_general(qrs_ref[hk], kblk, (((1,), (1,)), ((), ())), preferred_element_type=F32)
            s = s.reshape(hpg, tq, tk) + bias_ref[hk, kt][None]
            m_old = m_ref[hk]
            m_new = jnp.maximum(m_old, jnp.max(s, axis=-1, keepdims=True))
            alpha = jnp.exp(m_old - m_new)
            pe = jnp.exp(s - jnp.concatenate([m_new] * (tk // LANES), axis=-1))
            l_ref[hk] = alpha * l_ref[hk] + jnp.sum(pe, axis=-1, keepdims=True)
            pv = jnp.dot(pe.reshape(hpg * tq, tk).astype(BF16), vblk, preferred_element_type=F32)
            acc_ref[hk] = alpha * acc_ref[hk] + pv.reshape(hpg, tq, LANES)
            m_ref[hk] = m_new
        return carry

    lax.fori_loop(0, n_kt, sel_step, 0)

    for hk in range(NSA_KV_HEADS):
        o_cmp = o_cmp_g[hk]
        o_slc = acc_ref[hk] / l_ref[hk]
        kw = kvw_d[hk, pl.ds(win_start, WIN_KEYS), 0:KV_D]
        vw = kvw_d[hk, pl.ds(win_start, WIN_KEYS), KV_D:2 * KV_D]
        sw = lax.dot_general(qrs_ref[hk], kw, (((1,), (1,)), ((), ())), preferred_element_type=F32)
        sw = sw.reshape(hpg, tq, WIN_KEYS) + win_bias[None]
        pw = jnp.exp(sw - jnp.max(sw, axis=-1, keepdims=True))
        den = jnp.sum(pw, axis=-1, keepdims=True)
        o_win = jnp.dot(pw.reshape(hpg * tq, WIN_KEYS).astype(BF16), vw,
                        preferred_element_type=F32).reshape(hpg, tq, LANES) / den

        for hh in range(hpg):
            head = hk * hpg + hh
            p, e = head // 2, head % 2
            c0 = GATE_COL0 + head * 3
            mix = (sig[:, c0:c0 + 1] * o_cmp[hh] + sig[:, c0 + 1:c0 + 2] * o_slc[hh]
                   + sig[:, c0 + 2:c0 + 3] * o_win[hh])
            if e == 0:
                mix_even = mix
            else:
                o_ref[:, p * LANES:(p + 1) * LANES] = jnp.where(half_mask[0], mix_even, mix).astype(o_ref.dtype)


def _nsa_prompt(qc, qr, dtg, kvcmp, kvs, kvw, bn, t):
    nq = t // Q_BLOCK
    nbk = t // SEL_BLOCK
    hpg = NSA_HEADS // NSA_KV_HEADS
    assert nbk >= TOP_N and t >= WIN_KEYS and t % SEL_KEY_TILE == 0
    qrow = lambda w: pl.BlockSpec((Q_BLOCK, w), lambda b, i: (b * nq + i, 0))
    seq = lambda r: pl.BlockSpec((r, 2 * KV_D), lambda b, i: (b, 0))
    return pl.pallas_call(
        functools.partial(_nsa_prompt_kernel, t=t),
        grid=(bn, nq),
        in_specs=[qrow(NSA_D), qrow(NSA_D), qrow(LANES), seq(nbk), seq(t), seq(t)],
        out_specs=qrow(NSA_D),
        out_shape=jax.ShapeDtypeStruct((bn * t, NSA_D), BF16),
        scratch_shapes=[pltpu.VMEM((NSA_KV_HEADS, LANES, 2 * KV_D), BF16),
                        pltpu.VMEM((NSA_KV_HEADS, t, 2 * KV_D), BF16),
                        pltpu.VMEM((NSA_KV_HEADS, t, 2 * KV_D), BF16),
                        pltpu.VMEM((NSA_KV_HEADS, t // SEL_KEY_TILE, Q_BLOCK, SEL_KEY_TILE), F32),
                        pltpu.VMEM((NSA_KV_HEADS, hpg * Q_BLOCK, LANES), BF16),
                        pltpu.VMEM((NSA_KV_HEADS, hpg, Q_BLOCK, LANES), F32),
                        pltpu.VMEM((NSA_KV_HEADS, hpg, Q_BLOCK, LANES), F32),
                        pltpu.VMEM((NSA_KV_HEADS, hpg, Q_BLOCK, LANES), F32)],
        compiler_params=_cparams(("parallel", "arbitrary")),
        name="nsa_prompt",
    )(qc, qr, dtg, kvcmp, kvs, kvw)


def _outproj_kernel(ys_ref, yn_ref, h_ref, ws_ref, wn_ref, g_ref, b_ref, o_ref):
    mix = jnp.dot(ys_ref[...].astype(BF16), ws_ref[...], preferred_element_type=F32)
    mix = mix + jnp.dot(yn_ref[...].astype(BF16), wn_ref[...], preferred_element_type=F32)
    o_ref[...] = _layer_norm(DEEPNORM_ALPHA * h_ref[...] + mix, g_ref[...], b_ref[...])


def _outproj(y_ssd, y_nsa, h, w_ssd, w_nsa, ln_g, ln_b, tm):
    n = h.shape[0]
    row = lambda w: pl.BlockSpec((tm, w), lambda i: (i, 0))
    const = lambda a: pl.BlockSpec(a.shape, lambda i: (0,) * a.ndim)
    return pl.pallas_call(
        _outproj_kernel,
        grid=(n // tm,),
        in_specs=[row(SSD_D), row(NSA_D), row(D_MODEL), const(w_ssd), const(w_nsa), const(ln_g), const(ln_b)],
        out_specs=row(D_MODEL),
        out_shape=jax.ShapeDtypeStruct((n, D_MODEL), F32),
        compiler_params=_cparams(("parallel",)),
        name="outproj",
    )(y_ssd, y_nsa, h, w_ssd, w_nsa, ln_g, ln_b)


MOE_TOKENS = 256
ROUTE_ROWS = 8


def _token_tile_specs(n_main_tiles):
    main = pl.BlockSpec((MOE_TOKENS, D_MODEL), lambda i, *_: (jnp.minimum(i, n_main_tiles - 1), 0))
    tail = pl.BlockSpec((MOE_TOKENS, D_MODEL), lambda i, *_: (0, 0))
    return main, tail


def _token_tile(i, n_main_tiles, main_ref, tail_ref):
    return jnp.where(i < n_main_tiles, main_ref[...], tail_ref[...])


def _rank_rows(x):
    n = x.shape[0]
    idx = lax.broadcasted_iota(jnp.int32, x.shape, 0)
    rank = jnp.zeros(x.shape, F32)
    for r in range(n):
        row = x[r:r + 1, :]
        rank = rank + ((row > x) | ((row == x) & (idx > r))).astype(F32)
    return rank


def _route_kernel(h_ref, ht_ref, rw_ref, rb_ref, slot_ref, tokinfo_ref, meta_ref, cnt_ref, carry_ref, carry_row_ref, *,
                  n_valid, n_main):
    i = pl.program_id(0)
    tm = MOE_TOKENS

    @pl.when(i == 0)
    def _():
        carry_ref[...] = jnp.zeros_like(carry_ref)
        carry_row_ref[...] = jnp.zeros_like(carry_row_ref)

    logits = lax.dot_general(rw_ref[...], _token_tile(i, n_main, h_ref, ht_ref).astype(BF16), (((1,), (1,)), ((), ())),
                             preferred_element_type=F32)
    scores = _sigmoid(logits)
    biased = scores + rb_ref[:, 0:1]
    b3 = biased.reshape(N_EXPERT_GROUPS, EXPERTS_PER_GROUP, tm)
    sidx = lax.broadcasted_iota(jnp.int32, b3.shape, 1)
    m1 = jnp.max(b3, axis=1, keepdims=True)
    first = jnp.min(jnp.where(b3 == m1, sidx, EXPERTS_PER_GROUP), axis=1, keepdims=True)
    m2 = jnp.max(jnp.where(sidx == first, -jnp.inf, b3), axis=1, keepdims=True)
    grp_score = (m1 + m2).reshape(N_EXPERT_GROUPS, tm)
    grp_keep = _rank_rows(grp_score) < TOPK_GROUPS
    masked = jnp.where(grp_keep.reshape(N_EXPERT_GROUPS, 1, tm), b3, NEG).reshape(N_EXPERTS, tm)
    rank = _rank_rows(masked)
    tok = i * tm + lax.broadcasted_iota(jnp.int32, (1, tm), 1)
    valid = tok < n_valid
    sel = (rank < TOP_K) & valid
    self32 = sel.astype(F32)
    wsel = self32 * scores
    wsum = jnp.sum(wsel, axis=0, keepdims=True)
    w = wsel / jnp.where(wsum > 0.0, wsum, 1.0) * ROUTED_SCALE

    selb = sel.astype(BF16)
    tri = lambda n, strict_upper: (
        (lax.broadcasted_iota(jnp.int32, (n, n), 0) < lax.broadcasted_iota(jnp.int32, (n, n), 1))
        if strict_upper else
        (lax.broadcasted_iota(jnp.int32, (n, n), 0) > lax.broadcasted_iota(jnp.int32, (n, n), 1))).astype(BF16)
    pad8 = lambda c: jnp.floor((c + (SUBLANES - 1.0)) * (1.0 / SUBLANES)) * SUBLANES
    pos_tile = jnp.dot(selb, tri(tm, True), preferred_element_type=F32)
    cnt_col = pad8(jnp.sum(self32, axis=1, keepdims=True))
    first_col = jnp.dot(tri(N_EXPERTS, False), jnp.broadcast_to(cnt_col, (N_EXPERTS, LANES)).astype(BF16),
                        preferred_element_type=F32)[:, 0:1]
    slot = first_col + pos_tile

    sel_pad = jnp.concatenate([selb, jnp.zeros((LANES - N_EXPERTS, tm), BF16)], axis=0)
    cnt_row = pad8(lax.dot_general(jnp.ones((SUBLANES, tm), BF16), sel_pad, (((1,), (1,)), ((), ())),
                                   preferred_element_type=F32))
    first_row = jnp.dot(cnt_row.astype(BF16), tri(LANES, True), preferred_element_type=F32)
    prev_row = carry_row_ref[...]
    meta = jnp.concatenate([cnt_row[0:1], first_row[0:1], prev_row[0:1], jnp.zeros((SUBLANES - 3, LANES), F32)], 0)
    meta_ref[0] = meta.astype(jnp.int32)
    carry_row_ref[...] = prev_row + cnt_row
    carry_ref[...] = carry_ref[...] + cnt_col

    slot_rows, w_rows = [], []
    for k in range(TOP_K):
        hit = (rank == k) & sel
        slot_rows.append(jnp.sum(jnp.where(hit, slot, 0.0), axis=0, keepdims=True))
        w_rows.append(jnp.sum(jnp.where(hit, w, 0.0), axis=0, keepdims=True))
    slot_rows = [jnp.where(valid, r, -1.0) for r in slot_rows]
    pad2 = jnp.zeros((ROUTE_ROWS - TOP_K, tm), F32)
    slot_ref[...] = jnp.concatenate(slot_rows + [pad2 - 1.0], 0).astype(jnp.int32)
    info = jnp.concatenate(w_rows + [pad2] + slot_rows + [jnp.zeros((LANES - ROUTE_ROWS - TOP_K, tm), F32)], 0)
    tokinfo_ref[...] = info.T

    @pl.when(i == pl.num_programs(0) - 1)
    def _():
        cnt_ref[...] = jnp.broadcast_to(carry_ref[:, 0:1], cnt_ref.shape)


def _route(h_main, h_tail, router_wt, router_bias_col, n_valid):
    tm = MOE_TOKENS
    n_main = h_main.shape[0] // tm
    n = h_main.shape[0] + tm
    const = lambda a: pl.BlockSpec(a.shape, lambda i: (0,) * a.ndim)
    return pl.pallas_call(
        functools.partial(_route_kernel, n_valid=n_valid, n_main=n_main),
        grid=(n // tm,),
        in_specs=[*_token_tile_specs(n_main), const(router_wt), const(router_bias_col)],
        out_specs=[pl.BlockSpec((ROUTE_ROWS, tm), lambda i: (0, i)),
                   pl.BlockSpec((tm, LANES), lambda i: (i, 0)),
                   pl.BlockSpec((1, SUBLANES, LANES), lambda i: (i, 0, 0)),
                   pl.BlockSpec((N_EXPERTS, LANES), lambda i: (0, 0))],
        out_shape=[jax.ShapeDtypeStruct((ROUTE_ROWS, n), jnp.int32),
                   jax.ShapeDtypeStruct((n, LANES), F32),
                   jax.ShapeDtypeStruct((n // tm, SUBLANES, LANES), jnp.int32),
                   jax.ShapeDtypeStruct((N_EXPERTS, LANES), F32)],
        scratch_shapes=[pltpu.VMEM((N_EXPERTS, LANES), F32), pltpu.VMEM((SUBLANES, LANES), F32)],
        compiler_params=_cparams(("arbitrary",)),
        name="moe_route",
    )(h_main, h_tail, router_wt, router_bias_col)


TILE_SLOTS = MOE_TOKENS * TOP_K + N_EXPERTS * SUBLANES
RUN_CHUNKS = tuple(1 << b for b in range(int(math.log2(MOE_TOKENS)), int(math.log2(SUBLANES)) - 1, -1))


def _run_copy(src_ref, src_row, dst_ref, dst_row, rows, sem):
    return pltpu.make_async_copy(src_ref.at[pl.ds(pl.multiple_of(src_row, SUBLANES), rows)],
                                 dst_ref.at[pl.ds(pl.multiple_of(dst_row, SUBLANES), rows)], sem)


def _start_run(src_ref, src_row, dst_ref, dst_row, n, sem, started):
    off = jnp.int32(0)
    out = []
    for c, rows in enumerate(RUN_CHUNKS):
        take = (n & rows) != 0

        @pl.when(take)
        def _(off=off, rows=rows):
            _run_copy(src_ref, src_row + off, dst_ref, dst_row + off, rows, sem).start()

        inc = take.astype(jnp.int32)
        off = off + inc * rows
        out.append(started[c] + inc)
    return tuple(out)


def _wait_runs(src_ref, dst_ref, sem, started):
    for c, rows in enumerate(RUN_CHUNKS):
        def wait_one(j, carry, rows=rows):
            _run_copy(src_ref, 0, dst_ref, 0, rows, sem).wait()
            return carry

        lax.fori_loop(0, started[c], wait_one, 0)


def _dispatch_kernel(start_ref, cnt_ref, meta_ref, slot_ref, x_ref, xt_ref, xs_ref, sorted_ref, zero_ref, sem, zsem, *,
                     cap, n_main):
    i = pl.program_id(0)
    tm = MOE_TOKENS

    @pl.when(i == 0)
    def _():
        zero_ref[...] = jnp.zeros_like(zero_ref)

        def fill_expert(e, started):
            lo = start_ref[e] + cnt_ref[e]
            hi = jnp.where(e == N_EXPERTS - 1, cap, start_ref[jnp.minimum(e + 1, N_EXPERTS - 1)])
            n_full = (hi - lo) // tm

            def fill_full(j, st):
                return _start_run(zero_ref, 0, xs_ref, lo + j * tm, jnp.int32(tm), zsem, st)

            started = lax.fori_loop(0, n_full, fill_full, started)
            return _start_run(zero_ref, 0, xs_ref, lo + n_full * tm, (hi - lo) - n_full * tm, zsem, started)

        filled = lax.fori_loop(0, N_EXPERTS, fill_expert, tuple(jnp.int32(0) for _ in RUN_CHUNKS))
        _wait_runs(zero_ref, xs_ref, zsem, filled)

    srow = lax.broadcasted_iota(jnp.int32, (TILE_SLOTS, tm), 0)
    onehot = srow == slot_ref[0:1, :]
    for k in range(1, TOP_K):
        onehot = onehot | (srow == slot_ref[k:k + 1, :])
    sorted_ref[...] = jnp.dot(onehot.astype(BF16), _token_tile(i, n_main, x_ref, xt_ref).astype(BF16),
                              preferred_element_type=F32)

    def copy_expert(e, started):
        n = meta_ref[0, 0, e]
        return _start_run(sorted_ref, meta_ref[0, 1, e], xs_ref, start_ref[e] + meta_ref[0, 2, e], n, sem, started)

    started = lax.fori_loop(0, N_EXPERTS, copy_expert, tuple(jnp.int32(0) for _ in RUN_CHUNKS))
    _wait_runs(sorted_ref, xs_ref, sem, started)


def _dispatch(h_main, h_tail, slot_t, meta, seg_start, counts, cap):
    tm = MOE_TOKENS
    n_main = h_main.shape[0] // tm
    return pl.pallas_call(
        functools.partial(_dispatch_kernel, cap=cap, n_main=n_main),
        grid_spec=pltpu.PrefetchScalarGridSpec(
            num_scalar_prefetch=2,
            grid=(n_main + 1,),
            in_specs=[pl.BlockSpec((1, SUBLANES, LANES), lambda i, *_: (i, 0, 0), memory_space=pltpu.SMEM),
                      pl.BlockSpec((ROUTE_ROWS, tm), lambda i, *_: (0, i)),
                      *_token_tile_specs(n_main)],
            out_specs=pl.BlockSpec(memory_space=pl.ANY),
            scratch_shapes=[pltpu.VMEM((TILE_SLOTS, D_MODEL), F32), pltpu.VMEM((tm, D_MODEL), F32),
                            pltpu.SemaphoreType.DMA, pltpu.SemaphoreType.DMA]),
        out_shape=jax.ShapeDtypeStruct((cap, D_MODEL), F32),
        compiler_params=_cparams(("arbitrary",)),
        name="moe_dispatch",
    )(seg_start, counts, meta, slot_t, h_main, h_tail)


def _swiglu(x, wg, wu, wd):
    xb = x.astype(BF16)
    g = jnp.dot(xb, wg.astype(BF16), preferred_element_type=F32)
    u = jnp.dot(xb, wu.astype(BF16), preferred_element_type=F32)
    return jnp.dot((_silu(g) * u).astype(BF16), wd.astype(BF16), preferred_element_type=F32)


EXPERT_RING = 3


def _experts_kernel(be_ref, used_ref, xs_ref, wg_ref, wu_ref, wd_ref, y_ref, xbuf, sems):
    i = pl.program_id(0)
    n = pl.num_programs(0)

    def block_copy(blk):
        slot = lax.rem(blk, EXPERT_RING)
        rows = pl.ds(pl.multiple_of(blk * MOE_BLOCK, MOE_BLOCK), MOE_BLOCK)
        return pltpu.make_async_copy(xs_ref.at[rows], xbuf.at[slot], sems.at[slot])

    @pl.when(i == 0)
    def _():
        for j in range(EXPERT_RING - 1):
            block_copy(jnp.int32(j)).start()

    @pl.when(i + EXPERT_RING - 1 < n)
    def _():
        block_copy(i + EXPERT_RING - 1).start()

    block_copy(i).wait()

    @pl.when(i < used_ref[0])
    def _():
        y_ref[...] = _swiglu(xbuf[lax.rem(i, EXPERT_RING)], wg_ref[0], wu_ref[0], wd_ref[0])

    @pl.when(i >= used_ref[0])
    def _():
        y_ref[...] = jnp.zeros_like(y_ref)


def _experts(xs, block_expert, used_blocks, w_gate, w_up, w_down):
    cap = xs.shape[0]
    return pl.pallas_call(
        _experts_kernel,
        grid_spec=pltpu.PrefetchScalarGridSpec(
            num_scalar_prefetch=2,
            grid=(cap // MOE_BLOCK,),
            in_specs=[pl.BlockSpec(memory_space=pl.ANY),
                      pl.BlockSpec((1, D_MODEL, D_EXPERT), lambda i, be, used: (be[i], 0, 0)),
                      pl.BlockSpec((1, D_MODEL, D_EXPERT), lambda i, be, used: (be[i], 0, 0)),
                      pl.BlockSpec((1, D_EXPERT, D_MODEL), lambda i, be, used: (be[i], 0, 0))],
            out_specs=pl.BlockSpec((MOE_BLOCK, D_MODEL), lambda i, be, used: (i, 0)),
            scratch_shapes=[pltpu.VMEM((EXPERT_RING, MOE_BLOCK, D_MODEL), F32),
                            pltpu.SemaphoreType.DMA((EXPERT_RING,))]),
        out_shape=jax.ShapeDtypeStruct((cap, D_MODEL), F32),
        compiler_params=_cparams(("arbitrary",)),
        name="moe_experts",
    )(block_expert, used_blocks, xs, w_gate, w_up, w_down)


def _combine_kernel(start_ref, meta_ref, h_ref, ht_ref, info_ref, sg_ref, su_ref, sd_ref, g_ref, b_ref,
                    ys_ref, o_ref, ot_ref, buf_ref, sem, *, n_main):
    i = pl.program_id(0)
    tm = MOE_TOKENS

    @pl.when(i == 0)
    def _():
        buf_ref[...] = jnp.zeros_like(buf_ref)

    def fetch_expert(e, started):
        n = meta_ref[0, 0, e]
        return _start_run(ys_ref, start_ref[e] + meta_ref[0, 2, e], buf_ref, meta_ref[0, 1, e], n, sem, started)

    started = lax.fori_loop(0, N_EXPERTS, fetch_expert, tuple(jnp.int32(0) for _ in RUN_CHUNKS))
    h = _token_tile(i, n_main, h_ref, ht_ref)
    f = _swiglu(h, sg_ref[...], su_ref[...], sd_ref[...])
    info = info_ref[...]
    scol = lax.broadcasted_iota(jnp.int32, (tm, TILE_SLOTS), 1).astype(F32)
    mix = jnp.zeros((tm, TILE_SLOTS), F32)
    for k in range(TOP_K):
        mix = mix + jnp.where(info[:, ROUTE_ROWS + k:ROUTE_ROWS + k + 1] == scol, info[:, k:k + 1], 0.0)
    _wait_runs(ys_ref, buf_ref, sem, started)
    acc = jnp.dot(mix.astype(BF16), buf_ref[...].astype(BF16), preferred_element_type=F32)
    out = _layer_norm(DEEPNORM_ALPHA * h + (acc + f), g_ref[...], b_ref[...])

    @pl.when(i < n_main)
    def _():
        o_ref[...] = out

    @pl.when(i >= n_main)
    def _():
        ot_ref[...] = out


def _combine(h_main, h_tail, ys, meta, tokinfo, seg_start, sh_gate, sh_up, sh_down, ln_g, ln_b):
    tm = MOE_TOKENS
    n_main = h_main.shape[0] // tm
    const = lambda a: pl.BlockSpec(a.shape, lambda i, *_: (0,) * a.ndim)
    return pl.pallas_call(
        functools.partial(_combine_kernel, n_main=n_main),
        grid_spec=pltpu.PrefetchScalarGridSpec(
            num_scalar_prefetch=1,
            grid=(n_main + 1,),
            in_specs=[pl.BlockSpec((1, SUBLANES, LANES), lambda i, *_: (i, 0, 0), memory_space=pltpu.SMEM),
                      *_token_tile_specs(n_main),
                      pl.BlockSpec((tm, LANES), lambda i, *_: (i, 0)),
                      const(sh_gate), const(sh_up), const(sh_down), const(ln_g), const(ln_b),
                      pl.BlockSpec(memory_space=pl.ANY)],
            out_specs=list(_token_tile_specs(n_main)),
            scratch_shapes=[pltpu.VMEM((TILE_SLOTS, D_MODEL), F32), pltpu.SemaphoreType.DMA]),
        out_shape=[jax.ShapeDtypeStruct(h_main.shape, F32), jax.ShapeDtypeStruct((tm, D_MODEL), F32)],
        compiler_params=_cparams(("arbitrary",)),
        name="moe_combine",
    )(seg_start, meta, h_main, h_tail, tokinfo, sh_gate, sh_up, sh_down, ln_g, ln_b, ys)


def _moe_ln(h_main, h_tail, n_valid, router_w, router_bias, w_gate, w_up, w_down, sh_gate, sh_up, sh_down, ln_g, ln_b):
    n_tiles = h_main.shape[0] // MOE_TOKENS + 1
    slot_t, tokinfo, meta, cnt = _route(h_main, h_tail, router_w.T.astype(BF16),
                                        jnp.broadcast_to(router_bias.astype(F32)[:, None], (N_EXPERTS, LANES)), n_valid)
    counts = cnt[:, 0].astype(jnp.int32)
    padded = (counts + MOE_BLOCK - 1) // MOE_BLOCK * MOE_BLOCK
    seg_end = jnp.cumsum(padded)
    seg_start = seg_end - padded
    run_pad = n_tiles * N_EXPERTS * (SUBLANES - 1)
    n_blocks = -(-(n_valid * TOP_K + run_pad + N_EXPERTS * (MOE_BLOCK - 1)) // MOE_BLOCK)
    cap = n_blocks * MOE_BLOCK
    block_first_row = jnp.arange(n_blocks, dtype=jnp.int32) * MOE_BLOCK
    block_expert = jnp.minimum(jnp.sum((seg_end[None, :] <= block_first_row[:, None]).astype(jnp.int32), axis=1),
                               N_EXPERTS - 1)
    xs = _dispatch(h_main, h_tail, slot_t, meta, seg_start, counts, cap)
    used_blocks = (seg_end[N_EXPERTS - 1:] // MOE_BLOCK).astype(jnp.int32)
    ys = _experts(xs, block_expert, used_blocks, w_gate, w_up, w_down)
    return _combine(h_main, h_tail, ys, meta, tokinfo, seg_start, sh_gate.astype(BF16), sh_up.astype(BF16),
                    sh_down.astype(BF16), ln_g, ln_b)


def _ssd_sample_kernel(xbc_ref, z_ref, dtg_ref, sconv_ref, s0_ref, convw_ref, convb_ref, dtb_ref, alog_ref,
                       dskip_ref, normw_ref, y_ref, s_ref, conv_out_ref, xc_ref, dt_ref, da_ref):
    b = pl.program_id(0)

    @pl.when(b == 0)
    def _():
        xin = xbc_ref[...]
        xc = convw_ref[SSD_CONV - 1:SSD_CONV, :] * xin
        for k in range(SSD_CONV - 1):
            xc = xc + convw_ref[k:k + 1, :] * sconv_ref[k]
        xc_ref[...] = _silu(xc + convb_ref[...])
        dt = _softplus(dtg_ref[...] + dtb_ref[...])
        dt_ref[...] = dt
        da_ref[...] = jnp.exp(dt * (-jnp.exp(alog_ref[...])))
        for k in range(SSD_CONV - 2):
            conv_out_ref[k] = sconv_ref[k + 1]
        conv_out_ref[SSD_CONV - 2] = xin

    xc = xc_ref[pl.ds(b, 1), :]
    dt = dt_ref[pl.ds(b, 1), :]
    da = da_ref[pl.ds(b, 1), :]
    ns = SSD_GROUPS * SSD_STATE
    eye = (lax.broadcasted_iota(jnp.int32, (HEAD_DIM, HEAD_DIM), 0)
           == lax.broadcasted_iota(jnp.int32, (HEAD_DIM, HEAD_DIM), 1))
    hpg = SSD_HEADS // SSD_GROUPS
    y_parts = []
    for h in range(SSD_HEADS):
        g = h // hpg
        x_h = xc[:, h * HEAD_DIM:(h + 1) * HEAD_DIM]
        b_g = xc[:, SSD_D + g * SSD_STATE:SSD_D + (g + 1) * SSD_STATE]
        c_g = xc[:, SSD_D + ns + g * SSD_STATE:SSD_D + ns + (g + 1) * SSD_STATE]
        xdt_col = jnp.sum(jnp.where(eye, x_h * dt[:, h:h + 1], 0.0), axis=1, keepdims=True)
        s_new = da[:, h:h + 1] * s0_ref[0, h] + xdt_col * b_g
        s_ref[0, h] = s_new
        y_h = _bdot_nt(c_g, s_new) + dskip_ref[:, h * HEAD_DIM:(h + 1) * HEAD_DIM] * x_h
        y_parts.append(y_h)
    y = jnp.concatenate(y_parts, axis=1)
    y_ref[pl.ds(b, 1), :] = _gated_group_norm(y, z_ref[pl.ds(b, 1), :], normw_ref[...])


def _ssd_sample(xbc, z, dtg, state_conv_t, state_ssm, conv_w, conv_b, dt_bias_pad, a_log_pad, d_skip_full, norm_w):
    bs = xbc.shape[0]
    const = lambda a: pl.BlockSpec(a.shape, lambda b: (0,) * a.ndim)
    state_spec = pl.BlockSpec((1, SSD_HEADS, HEAD_DIM, SSD_STATE), lambda b: (b, 0, 0, 0))
    return pl.pallas_call(
        _ssd_sample_kernel,
        grid=(bs,),
        in_specs=[const(xbc), const(z), const(dtg), const(state_conv_t), state_spec, const(conv_w), const(conv_b),
                  const(dt_bias_pad), const(a_log_pad), const(d_skip_full), const(norm_w)],
        out_specs=[pl.BlockSpec((bs, SSD_D), lambda b: (0, 0)), state_spec,
                   pl.BlockSpec((SSD_CONV - 1, bs, SSD_CONV_CH), lambda b: (0, 0, 0))],
        out_shape=[jax.ShapeDtypeStruct((bs, SSD_D), F32),
                   jax.ShapeDtypeStruct(state_ssm.shape, F32),
                   jax.ShapeDtypeStruct((SSD_CONV - 1, bs, SSD_CONV_CH), F32)],
        scratch_shapes=[pltpu.VMEM((bs, SSD_CONV_CH), F32), pltpu.VMEM((bs, LANES), F32),
                        pltpu.VMEM((bs, LANES), F32)],
        compiler_params=_cparams(("arbitrary",)),
        name="ssd_sample",
    )(xbc, z, dtg, state_conv_t, state_ssm, conv_w, conv_b, dt_bias_pad, a_log_pad, d_skip_full, norm_w)


SEL_PAST = TOP_N - 1
BLOCKS_PER_PAGE = PAGE_SIZE // CMP_BLOCK
KV_FEATS = 2 * KV_D


def _compress_consts_t(cmp_pe, cmp_w1, cmp_b1, cmp_w2, cmp_b2):
    pe_t = jnp.stack([jnp.tile(cmp_pe[k].T, (1, BLOCKS_PER_PAGE)) for k in range(2)])
    w1_t = jnp.stack([_block_diag2(jnp.swapaxes(cmp_w1[k], 0, 1)) for k in range(2)]).astype(BF16)
    b1_t = jnp.stack([jnp.tile(cmp_b1[k], BLOCKS_PER_PAGE) for k in range(2)])[:, None, :]
    w2_t = jnp.stack([_block_diag2(cmp_w2[k]) for k in range(2)]).astype(BF16)
    b2_t = jnp.stack([jnp.tile(cmp_b2[k], BLOCKS_PER_PAGE) for k in range(2)])[:, None, :]
    return pe_t, w1_t, b1_t, w2_t, b2_t


def _compress_pages_kernel(pt_ref, pe_ref, w1_ref, b1_ref, w2_ref, b2_ref, pool_ref, o_ref, kbuf, vbuf, sems, *,
                           n_pages):
    b = pl.program_id(0)
    nb = pl.num_programs(0)
    bufs = (kbuf, vbuf)

    def half_copy(seq, kind, p):
        return pltpu.make_async_copy(pool_ref.at[pt_ref[seq, p], pl.ds(kind * KV_D, KV_D)],
                                     bufs[kind].at[pl.ds(pl.multiple_of(p * KV_D, KV_D), KV_D)], sems.at[kind])

    def start_half(seq, kind):
        lax.fori_loop(0, n_pages, lambda p, c: (half_copy(seq, kind, p).start(), c)[1], 0)

    def wait_half(seq, kind):
        lax.fori_loop(0, n_pages, lambda p, c: (half_copy(seq, kind, p).wait(), c)[1], 0)

    @pl.when(b == 0)
    def _():
        start_half(b, 0)
        start_half(b, 1)

    for kind in range(2):
        wait_half(b, kind)
        for h in range(NSA_KV_HEADS):
            def add_feature(d, acc, kind=kind, h=h):
                x = bufs[kind][pl.ds(h * HEAD_DIM + d, n_pages, stride=KV_D), :] + pe_ref[kind, pl.ds(d, 1), :]
                return acc + jnp.dot(x.astype(BF16), w1_ref[kind, d], preferred_element_type=F32)

            acc = lax.fori_loop(0, HEAD_DIM, add_feature,
                                jnp.zeros((n_pages, BLOCKS_PER_PAGE * CMP_HIDDEN), F32), unroll=8)
            hid = _silu(acc + b1_ref[kind])
            o_ref[0, kind * NSA_KV_HEADS + h] = (
                jnp.dot(hid.astype(BF16), w2_ref[kind], preferred_element_type=F32) + b2_ref[kind])

        @pl.when(b + 1 < nb)
        def _(kind=kind):
            start_half(b + 1, kind)


def _compress_pages(pool_t, page_table, consts):
    bs, n_pages = page_table.shape
    const = lambda a: pl.BlockSpec(a.shape, lambda b, pt: (0,) * a.ndim)
    return pl.pallas_call(
        functools.partial(_compress_pages_kernel, n_pages=n_pages),
        grid_spec=pltpu.PrefetchScalarGridSpec(
            num_scalar_prefetch=1,
            grid=(bs,),
            in_specs=[const(a) for a in consts] + [pl.BlockSpec(memory_space=pl.ANY)],
            out_specs=pl.BlockSpec((1, 2 * NSA_KV_HEADS, n_pages, LANES), lambda b, pt: (b, 0, 0, 0)),
            scratch_shapes=[pltpu.VMEM((n_pages * KV_D, PAGE_SIZE), F32), pltpu.VMEM((n_pages * KV_D, PAGE_SIZE), F32),
                            pltpu.SemaphoreType.DMA((2,))]),
        out_shape=jax.ShapeDtypeStruct((bs, 2 * NSA_KV_HEADS, n_pages, LANES), F32),
        compiler_params=_cparams(("arbitrary",)),
        name="compress_pages",
    )(page_table, *consts, pool_t)


def _group_heads(q_row, hk):
    hpg = NSA_HEADS // NSA_KV_HEADS
    low = lax.broadcasted_iota(jnp.int32, (1, LANES), 1) < HEAD_DIM
    rows = []
    for r in range(hpg):
        head = hk * hpg + r
        tile = q_row[:, (head // 2) * LANES:(head // 2 + 1) * LANES]
        if head % 2 == 1:
            tile = pltpu.roll(tile, HEAD_DIM, 1)
        rows.append(jnp.where(low, tile, 0.0))
    return jnp.concatenate(rows + [jnp.zeros((SUBLANES - hpg, LANES), F32)], axis=0)


def _spread_heads(o_groups):
    hpg = NSA_HEADS // NSA_KV_HEADS
    return jnp.concatenate([o[r:r + 1, 0:HEAD_DIM] for o in o_groups for r in range(hpg)], axis=1)


def _nsa_sample_cmp_t_kernel(qc_ref, cmp_ref, ocmp_ref, idx_ref, *, n_pages):
    b = pl.program_id(0)
    nc = n_pages * BLOCKS_PER_PAGE
    scale = HEAD_DIM ** -0.5
    hpg = NSA_HEADS // NSA_KV_HEADS
    q_row = qc_ref[pl.ds(b, 1), :] * scale
    lane = lax.broadcasted_iota(jnp.int32, (1, LANES), 1)
    pos_r = lax.broadcasted_iota(jnp.int32, (1, nc), 1)
    bid_r = (pos_r % n_pages) * BLOCKS_PER_PAGE + pos_r // n_pages
    pos_c = lax.broadcasted_iota(jnp.int32, (nc, 1), 0)
    bid_c = (pos_c % n_pages) * BLOCKS_PER_PAGE + pos_c // n_pages
    o_groups = []
    for hk in range(NSA_KV_HEADS):
        kc = cmp_ref[0, hk].astype(BF16)
        vc = cmp_ref[0, NSA_KV_HEADS + hk].astype(BF16)
        qh = _group_heads(q_row, hk)
        s = jnp.concatenate(
            [lax.dot_general(pltpu.roll(qh, c * HEAD_DIM, 1).astype(BF16) if c else qh.astype(BF16), kc,
                             (((1,), (1,)), ((), ())), preferred_element_type=F32)
             for c in range(BLOCKS_PER_PAGE)], axis=1)
        ex = jnp.exp(s - jnp.max(s, axis=-1, keepdims=True))
        p = ex / jnp.sum(ex, axis=-1, keepdims=True)
        o = jnp.dot(p[:, 0:n_pages].astype(BF16), vc, preferred_element_type=F32)
        for c in range(1, BLOCKS_PER_PAGE):
            oc = jnp.dot(p[:, c * n_pages:(c + 1) * n_pages].astype(BF16), vc, preferred_element_type=F32)
            o = o + pltpu.roll(oc, LANES - c * HEAD_DIM, 1)
        o_groups.append(o)
        hrow = lax.broadcasted_iota(jnp.int32, p.shape, 0) < hpg
        imp = jnp.sum(jnp.where(hrow, p, 0.0), axis=0, keepdims=True)
        score = jnp.where((bid_r == 0) | (bid_r == nc - 1), FORCED_SCORE, imp)
        score_col = jnp.concatenate([score, jnp.zeros((LANES - 1, nc), F32)], 0).T[:, 0:1]
        beats = (score_col > score) | ((score_col == score) & (bid_c < bid_r))
        rank = jnp.sum(beats.astype(F32), axis=0, keepdims=True)
        row = jnp.zeros((1, LANES), F32)
        bid_f = bid_r.astype(F32)
        for k in range(SEL_PAST):
            blk = jnp.sum(jnp.where(rank == k, bid_f, 0.0), axis=1, keepdims=True)
            row = jnp.where(lane == k, blk, row)
        idx_ref[pl.ds(b * NSA_KV_HEADS + hk, 1), :] = row.astype(jnp.int32)
    ocmp_ref[pl.ds(b, 1), :] = _spread_heads(o_groups)


def _nsa_sample_cmp_t(qc, kvcmp_t):
    bs, _, n_pages, _ = kvcmp_t.shape
    return pl.pallas_call(
        functools.partial(_nsa_sample_cmp_t_kernel, n_pages=n_pages),
        grid=(bs,),
        in_specs=[pl.BlockSpec((bs, NSA_D), lambda b: (0, 0)),
                  pl.BlockSpec((1, 2 * NSA_KV_HEADS, n_pages, LANES), lambda b: (b, 0, 0, 0))],
        out_specs=[pl.BlockSpec((bs, NSA_D), lambda b: (0, 0)),
                   pl.BlockSpec((bs * NSA_KV_HEADS, LANES), lambda b: (0, 0))],
        out_shape=[jax.ShapeDtypeStruct((bs, NSA_D), F32),
                   jax.ShapeDtypeStruct((bs * NSA_KV_HEADS, LANES), jnp.int32)],
        compiler_params=_cparams(("arbitrary",)),
        name="nsa_sample_cmp",
    )(qc, kvcmp_t)


def _sel_block_copies(pool_ref, pt_ref, sel_ref, kbuf, vbuf, sem, b, hk, k):
    blk = sel_ref[b * NSA_KV_HEADS + hk, k]
    page = pt_ref[b, lax.shift_right_logical(blk, int(math.log2(BLOCKS_PER_PAGE)))]
    j = hk * SEL_PAST + k
    return (pltpu.make_async_copy(pool_ref.at[page, pl.ds(hk * HEAD_DIM, HEAD_DIM)], kbuf.at[j], sem),
            pltpu.make_async_copy(pool_ref.at[page, pl.ds(KV_D + hk * HEAD_DIM, HEAD_DIM)], vbuf.at[j], sem))


def _nsa_sample_attn_t_kernel(pt_ref, sel_ref, qr_ref, new_sel_ref, new_win_ref, win_ref, dtg_ref, ocmp_ref,
                              pool_ref, o_ref, kbuf, vbuf, sem):
    b = pl.program_id(0)
    for hk in range(NSA_KV_HEADS):
        for k in range(SEL_PAST):
            for cp in _sel_block_copies(pool_ref, pt_ref, sel_ref, kbuf, vbuf, sem, b, hk, k):
                cp.start()
    for hk in range(NSA_KV_HEADS):
        for k in range(SEL_PAST):
            for cp in _sel_block_copies(pool_ref, pt_ref, sel_ref, kbuf, vbuf, sem, b, hk, k):
                cp.wait()
    scale = HEAD_DIM ** -0.5
    q_row = qr_ref[pl.ds(b, 1), :] * scale
    sig = _sigmoid(dtg_ref[pl.ds(b, 1), :])
    lane = lax.broadcasted_iota(jnp.int32, (1, PAGE_SIZE), 1)
    o_slc, o_win = [], []
    for hk in range(NSA_KV_HEADS):
        qh = _group_heads(q_row, hk)[:, 0:HEAD_DIM].astype(BF16)

        def new_row(ref, kind):
            t = ref[pl.ds(b, 1), :][:, kind * KV_D:(kind + 1) * KV_D]
            if hk == 1:
                t = pltpu.roll(t, HEAD_DIM, 1)
            return t[:, 0:HEAD_DIM].astype(BF16).astype(F32)

        def attend(kt, vt, mask, new_ref, n_new):
            s = jnp.dot(qh, kt.astype(BF16), preferred_element_type=F32)
            if mask is not None:
                s = jnp.where(mask, s, NEG)
            s_new = jnp.sum(qh.astype(F32) * new_row(new_ref, 0), axis=1, keepdims=True)
            m = jnp.maximum(jnp.max(s, axis=-1, keepdims=True), s_new)
            ex = jnp.exp(s - m)
            ex_new = jnp.exp(s_new - m) * n_new
            den = jnp.sum(ex, axis=-1, keepdims=True) + ex_new
            o = lax.dot_general((ex / den).astype(BF16), vt.astype(BF16), (((1,), (1,)), ((), ())),
                                preferred_element_type=F32)
            return o + (ex_new / den).astype(BF16).astype(F32) * new_row(new_ref, 1)

        kt = jnp.concatenate([kbuf[hk * SEL_PAST + k] for k in range(SEL_PAST)], axis=1)
        vt = jnp.concatenate([vbuf[hk * SEL_PAST + k] for k in range(SEL_PAST)], axis=1)
        mask = jnp.concatenate(
            [lane // SEL_BLOCK == (sel_ref[b * NSA_KV_HEADS + hk, k] & (BLOCKS_PER_PAGE - 1))
             for k in range(SEL_PAST)], axis=1)
        o_slc.append(attend(kt, vt, mask, new_sel_ref, float(SEL_BLOCK)))
        o_win.append(attend(win_ref[0, hk * HEAD_DIM:(hk + 1) * HEAD_DIM, :],
                            win_ref[0, KV_D + hk * HEAD_DIM:KV_D + (hk + 1) * HEAD_DIM, :], None, new_win_ref, 1.0))
    gates = []
    for br in range(3):
        gates.append(jnp.concatenate(
            [jnp.broadcast_to(sig[:, GATE_COL0 + h * 3 + br:GATE_COL0 + h * 3 + br + 1], (1, HEAD_DIM))
             for h in range(NSA_HEADS)], axis=1))
    o_ref[pl.ds(b, 1), :] = (gates[0] * ocmp_ref[pl.ds(b, 1), :] + gates[1] * _spread_heads(o_slc)
                             + gates[2] * _spread_heads(o_win))


def _nsa_sample_attn_t(qr, new_sel, new_win, win_t, dtg, o_cmp, pool_sel_t, page_table, sel_idx):
    bs = qr.shape[0]
    const = lambda a: pl.BlockSpec(a.shape, lambda b, pt, sel: (0,) * a.ndim)
    n_buf = NSA_KV_HEADS * SEL_PAST
    return pl.pallas_call(
        _nsa_sample_attn_t_kernel,
        grid_spec=pltpu.PrefetchScalarGridSpec(
            num_scalar_prefetch=2,
            grid=(bs,),
            in_specs=[const(qr), const(new_sel), const(new_win),
                      pl.BlockSpec((1,) + win_t.shape[1:], lambda b, pt, sel: (b, 0, 0)),
                      const(dtg), const(o_cmp), pl.BlockSpec(memory_space=pl.ANY)],
            out_specs=pl.BlockSpec((bs, NSA_D), lambda b, pt, sel: (0, 0)),
            scratch_shapes=[pltpu.VMEM((n_buf, HEAD_DIM, PAGE_SIZE), F32), pltpu.VMEM((n_buf, HEAD_DIM, PAGE_SIZE), F32),
                            pltpu.SemaphoreType.DMA]),
        out_shape=jax.ShapeDtypeStruct((bs, NSA_D), F32),
        compiler_params=_cparams(("arbitrary",)),
        name="nsa_sample_attn",
    )(page_table, sel_idx, qr, new_sel, new_win, win_t, dtg, o_cmp, pool_sel_t)


def kernel(x_prompt, x_sample, cache_kv_cmp, cache_kv_sel, page_table, cache_kv_win, state_ssm, state_conv,
           emb_ln_g, emb_ln_b, w_in, conv_w, conv_b, dt_bias, a_log, d_skip, ssd_norm_w,
           cmp_pe, cmp_w1, cmp_b1, cmp_w2, cmp_b2, w_out, ln1_g, ln1_b,
           router_w, router_bias, exp_w_gate, exp_w_up, exp_w_down,
           sh_w_gate, sh_w_up, sh_w_down, ln2_g, ln2_b):
    bp, tp, _ = x_prompt.shape
    bs, ts, _ = x_sample.shape
    assert ts == 1 and DEPTH == 1
    n_prompt = bp * tp
    past_len = page_table.shape[1] * PAGE_SIZE
    l = 0
    w_perm = _permute_w_in(w_in[l])
    ln0_g, ln0_b = emb_ln_g[None], emb_ln_b[None]
    ssd_consts = (conv_w[l], conv_b[l][None], _pad_lanes(dt_bias[l]), _pad_lanes(a_log[l]),
                  jnp.repeat(d_skip[l], HEAD_DIM)[None], ssd_norm_w[l][None])
    cmp_consts = _compress_consts(cmp_pe[l], cmp_w1[l], cmp_b1[l], cmp_w2[l], cmp_b2[l])
    w_o = w_out[l].astype(BF16)
    w_o_ssd, w_o_nsa = w_o[:SSD_D], w_o[SSD_D:]
    ln1 = (ln1_g[l][None], ln1_b[l][None])
    kv_shape = (2, NSA_KV_HEADS, HEAD_DIM)

    hp, z, xbc, qc, qr, kvc, kvs, kvw, dtg, kvc_t, kvs_t, kvw_t = _inproj(
        x_prompt.reshape(n_prompt, D_MODEL), ln0_g, ln0_b, w_perm, _rope_tables(jnp.arange(tp)), 256,
        _rope_tables_t(jnp.arange(tp)))
    y_ssd, ssm_p, conv_p = _ssd_prompt(xbc, z, dtg, *ssd_consts, bp, tp)
    kvcmp = _compress_prompt(kvc, cmp_consts, tp)
    y_nsa = _nsa_prompt(qc, qr, dtg, kvcmp, kvs, kvw, bp, tp)
    h1p = _outproj(y_ssd, y_nsa, hp, w_o_ssd, w_o_nsa, *ln1, 256)
    n_keep = min(WINDOW, tp)
    cache_leaf = lambda a: jnp.transpose(a.reshape((bp,) + kv_shape + (a.shape[-1],)), (0, 4, 1, 2, 3))[None]
    kvc_p = cache_leaf(kvc_t)
    kvs_p = cache_leaf(kvs_t)
    kvw_p = cache_leaf(kvw_t[:, :, tp - n_keep:])

    s_hs, s_z, s_xbc, s_qc, s_qr, s_kvc, s_kvs, s_kvw, s_dtg = _inproj(
        x_sample.reshape(bs, D_MODEL), ln0_g, ln0_b, w_perm, _rope_tables(jnp.full((bs,), past_len)), bs)
    s_y_ssd, ssm_s, conv_s_t = _ssd_sample(s_xbc, s_z, s_dtg, jnp.swapaxes(state_conv[l], 0, 1), state_ssm[l],
                                           *ssd_consts)
    n_pool = cache_kv_cmp.shape[1]
    feature_major = lambda c, rows: jnp.swapaxes(c.reshape(-1, rows, 2 * KV_D), 1, 2)
    s_kvcmp = _compress_pages(feature_major(cache_kv_cmp[l], PAGE_SIZE), page_table,
                              _compress_consts_t(cmp_pe[l], cmp_w1[l], cmp_b1[l], cmp_w2[l], cmp_b2[l]))
    s_o_cmp, s_sel = _nsa_sample_cmp_t(s_qc, s_kvcmp)
    buf_win = cache_kv_win[l].reshape(bs, -1, 2 * KV_D)
    s_y_nsa = _nsa_sample_attn_t(
        s_qr, s_kvs, s_kvw, feature_major(cache_kv_win[l], buf_win.shape[1]), s_dtg, s_o_cmp,
        feature_major(cache_kv_sel[l], PAGE_SIZE), page_table, s_sel)
    h1s = _outproj(s_y_ssd, s_y_nsa, s_hs, w_o_ssd, w_o_nsa, *ln1, bs)
    win_all = jnp.concatenate([buf_win, s_kvw[:, None, :]], 1)
    n_keep_s = min(WINDOW, past_len + ts)
    kvw_s = win_all[:, win_all.shape[1] - n_keep_s:].reshape((1, bs, n_keep_s) + kv_shape)
    kvc_s = s_kvc.reshape((1, bs, ts) + kv_shape)
    kvs_s = s_kvs.reshape((1, bs, ts) + kv_shape)

    assert n_prompt % MOE_TOKENS == 0 and bs * ts <= MOE_TOKENS
    n_tok = n_prompt + bs * ts
    tail = jnp.concatenate([h1s, jnp.zeros((MOE_TOKENS - bs * ts, D_MODEL), F32)], 0)
    out_main, out_tail = _moe_ln(h1p, tail, n_tok, router_w[l], router_bias[l], exp_w_gate[l], exp_w_up[l],
                                 exp_w_down[l], sh_w_gate[l], sh_w_up[l], sh_w_down[l], ln2_g[l][None], ln2_b[l][None])
    y_prompt = out_main.reshape(bp, tp, D_MODEL)
    y_sample = out_tail[:bs * ts].reshape(bs, ts, D_MODEL)
    return (y_prompt, y_sample, kvc_p, kvs_p, kvw_p, ssm_p[None], conv_p[None],
            kvc_s, kvs_s, kvw_s, ssm_s[None], jnp.swapaxes(conv_s_t, 0, 1)[None])
```

```python
import functools
import math

import jax
import jax.numpy as jnp
import numpy as np
from jax import lax
from jax.experimental import pallas as pl
from jax.experimental.pallas import tpu as pltpu

D_MODEL = 1024
HEAD_DIM = 64
SSD_HEADS = 8
SSD_D = SSD_HEADS * HEAD_DIM
SSD_GROUPS = 2
SSD_STATE = 128
SSD_CONV = 4
SSD_CONV_CH = SSD_D + 2 * SSD_GROUPS * SSD_STATE
SSD_CHUNK = 128
NSA_HEADS = 8
NSA_KV_HEADS = 2
NSA_D = NSA_HEADS * HEAD_DIM
KV_D = NSA_KV_HEADS * HEAD_DIM
CMP_BLOCK = 64
CMP_HIDDEN = 128
SEL_BLOCK = 64
TOP_N = 16
WINDOW = 512
Q_BLOCK = 128
ROT_DIM = HEAD_DIM // 4
ROPE_THETA = 500000.0
N_EXPERTS = 64
TOP_K = 6
N_EXPERT_GROUPS = 8
EXPERTS_PER_GROUP = N_EXPERTS // N_EXPERT_GROUPS
TOPK_GROUPS = 4
D_EXPERT = 256
D_SHARED = 256
ROUTED_SCALE = 2.5
MOE_BLOCK = 512
DEPTH = 1
DEEPNORM_ALPHA = (2.0 * DEPTH) ** 0.25
LN_EPS = 1e-5
RMS_EPS = 1e-5
NEG = -1e30
FORCED_SCORE = 1e4
PAGE_SIZE = 128

LANES = 128
SUBLANES = 8
VMEM_LIMIT_BYTES = 56 * 1024 * 1024

U_Z = 0
U_XBC = U_Z + SSD_D
U_Q = U_XBC + SSD_CONV_CH
U_KVC = U_Q + NSA_D
U_KVS = U_KVC + 2 * KV_D
U_KVW = U_KVS + 2 * KV_D
U_DTG = U_KVW + 2 * KV_D
U_TOTAL = U_DTG + LANES
GATE_COL0 = SSD_HEADS

BF16 = jnp.bfloat16
F32 = jnp.float32


def _cparams(sem):
    return pltpu.CompilerParams(dimension_semantics=sem, vmem_limit_bytes=VMEM_LIMIT_BYTES)


def _bdot(a, b):
    return jnp.dot(a.astype(BF16), b.astype(BF16), preferred_element_type=F32)


def _bdot_nt(a, b):
    return lax.dot_general(a.astype(BF16), b.astype(BF16), (((1,), (1,)), ((), ())),
                           preferred_element_type=F32)


def _hdot(a, b):
    return jnp.dot(a, b, preferred_element_type=F32, precision=lax.Precision.HIGHEST)


def _sigmoid(x):
    return 1.0 / (1.0 + jnp.exp(-x))


def _silu(x):
    return x * _sigmoid(x)


def _layer_norm(x, g, b):
    mu = jnp.mean(x, axis=-1, keepdims=True)
    xc = x - mu
    var = jnp.mean(xc * xc, axis=-1, keepdims=True)
    return xc * lax.rsqrt(var + LN_EPS) * g + b


def _rope_tile(x, cos, sa, sb):
    return x * cos + pltpu.roll(x, LANES - ROT_DIM // 2, 1) * sa + pltpu.roll(x, ROT_DIM // 2, 1) * sb


def _rope_rows(x, cos, sin):
    half = ROT_DIM // 2
    parts = []
    for hd in range(NSA_KV_HEADS):
        r0 = hd * HEAD_DIM
        x1, x2 = x[r0:r0 + half], x[r0 + half:r0 + ROT_DIM]
        parts += [x1 * cos - x2 * sin, x2 * cos + x1 * sin, x[r0 + ROT_DIM:r0 + HEAD_DIM]]
    return jnp.concatenate(parts, axis=0)


def _inproj_kernel(x_ref, g_ref, b_ref, w_ref, rope_ref, *refs, feature_major):
    if feature_major:
        wkv_t_ref, rope_t_ref = refs[:2]
        refs = refs[2:]
    h_ref, z_ref, xbc_ref, qc_ref, qr_ref, kvc_ref, kvs_ref, kvw_ref, dtg_ref = refs[:9]
    h = _layer_norm(x_ref[...], g_ref[...], b_ref[...])
    h_ref[...] = h
    hb = h.astype(BF16)
    if feature_major:
        kvc_t_ref, kvs_t_ref, kvw_t_ref = refs[9:]
        ut = lax.dot_general(wkv_t_ref[...], hb, (((1,), (1,)), ((), ())), preferred_element_type=F32)
        half = ROT_DIM // 2
        cos_t, sin_t = rope_t_ref[0:half, :], rope_t_ref[half:2 * half, :]
        kvc_t_ref[0] = ut[0:2 * KV_D]
        kvs_t_ref[0, 0:KV_D] = _rope_rows(ut[2 * KV_D:3 * KV_D], cos_t, sin_t)
        kvs_t_ref[0, KV_D:2 * KV_D] = ut[3 * KV_D:4 * KV_D]
        kvw_t_ref[0, 0:KV_D] = _rope_rows(ut[4 * KV_D:5 * KV_D], cos_t, sin_t)
        kvw_t_ref[0, KV_D:2 * KV_D] = ut[5 * KV_D:6 * KV_D]
    u = jnp.dot(hb, w_ref[...], preferred_element_type=F32)
    cos = rope_ref[:, 0:LANES]
    sa = rope_ref[:, LANES:2 * LANES]
    sb = rope_ref[:, 2 * LANES:3 * LANES]
    z_ref[...] = u[:, U_Z:U_XBC]
    xbc_ref[...] = u[:, U_XBC:U_Q]
    qc_ref[...] = u[:, U_Q:U_KVC].astype(qc_ref.dtype)
    for c in range(NSA_D // LANES):
        qr_ref[:, c * LANES:(c + 1) * LANES] = _rope_tile(
            u[:, U_Q + c * LANES:U_Q + (c + 1) * LANES], cos, sa, sb).astype(qr_ref.dtype)
    kvc_ref[...] = u[:, U_KVC:U_KVS]
    kvs_ref[:, 0:KV_D] = _rope_tile(u[:, U_KVS:U_KVS + KV_D], cos, sa, sb)
    kvs_ref[:, KV_D:2 * KV_D] = u[:, U_KVS + KV_D:U_KVW]
    kvw_ref[:, 0:KV_D] = _rope_tile(u[:, U_KVW:U_KVW + KV_D], cos, sa, sb)
    kvw_ref[:, KV_D:2 * KV_D] = u[:, U_KVW + KV_D:U_DTG]
    dtg_ref[...] = u[:, U_DTG:U_TOTAL]


def _rope_tables(pos):
    half = ROT_DIM // 2
    inv = ROPE_THETA ** (-jnp.arange(half, dtype=F32) / half)
    ang = pos.astype(F32)[:, None] * inv
    cos, sin = jnp.cos(ang), jnp.sin(ang)
    ones = jnp.ones((pos.shape[0], HEAD_DIM - ROT_DIM), F32)
    zeros = jnp.zeros((pos.shape[0], HEAD_DIM - ROT_DIM), F32)
    zh = jnp.zeros_like(sin)
    c = jnp.concatenate([cos, cos, ones], 1)
    sa = jnp.concatenate([-sin, zh, zeros], 1)
    sb = jnp.concatenate([zh, sin, zeros], 1)
    return jnp.concatenate([jnp.tile(t, (1, LANES // HEAD_DIM)) for t in (c, sa, sb)], 1)


def _permute_w_in(w):
    sizes = (SSD_D, SSD_CONV_CH, SSD_HEADS, NSA_D, KV_D, KV_D, KV_D, KV_D, KV_D, KV_D, 3 * NSA_HEADS)
    offs = np.concatenate([[0], np.cumsum(sizes)])
    seg = [w[:, offs[i]:offs[i + 1]] for i in range(len(sizes))]
    pad = jnp.zeros((w.shape[0], LANES - SSD_HEADS - 3 * NSA_HEADS), w.dtype)
    out = jnp.concatenate([seg[0], seg[1], seg[3], seg[4], seg[5], seg[6], seg[7], seg[8], seg[9],
                           seg[2], seg[10], pad], 1)
    return out.astype(BF16)


def _rope_tables_t(pos):
    half = ROT_DIM // 2
    inv = ROPE_THETA ** (-jnp.arange(half, dtype=F32) / half)
    ang = inv[:, None] * pos.astype(F32)[None, :]
    return jnp.concatenate([jnp.cos(ang), jnp.sin(ang)], 0)


def _inproj(x, ln_g, ln_b, w_perm, rope_tab, tm, rope_tab_t=None):
    n = x.shape[0]
    nt = n // tm
    t = rope_tab.shape[0]
    n_rope_blocks = t // tm
    feature_major = rope_tab_t is not None
    row = lambda w: pl.BlockSpec((tm, w), lambda i: (i, 0))
    const = lambda a: pl.BlockSpec(a.shape, lambda i: (0,) * a.ndim)
    widths = (D_MODEL, SSD_D, SSD_CONV_CH, NSA_D, NSA_D, 2 * KV_D, 2 * KV_D, 2 * KV_D, LANES)
    in_specs = [row(D_MODEL), const(ln_g), const(ln_b), const(w_perm),
                pl.BlockSpec((tm, 3 * LANES), lambda i: (i % n_rope_blocks, 0))]
    out_specs = [row(w) for w in widths]
    dtypes = [BF16 if (feature_major and k in (3, 4)) else F32 for k in range(len(widths))]
    out_shape = [jax.ShapeDtypeStruct((n, w), d) for w, d in zip(widths, dtypes)]
    args = [x, ln_g, ln_b, w_perm, rope_tab]
    if feature_major:
        wkv_t = w_perm[:, U_KVC:U_DTG].T
        in_specs += [const(wkv_t), pl.BlockSpec((rope_tab_t.shape[0], tm), lambda i: (0, i % n_rope_blocks))]
        args += [wkv_t, rope_tab_t]
        out_specs += [pl.BlockSpec((1, 2 * KV_D, tm), lambda i: (i // n_rope_blocks, 0, i % n_rope_blocks))] * 3
        out_shape += [jax.ShapeDtypeStruct((n // t, 2 * KV_D, t), F32)] * 3
    return pl.pallas_call(
        functools.partial(_inproj_kernel, feature_major=feature_major),
        grid=(nt,),
        in_specs=in_specs,
        out_specs=out_specs,
        out_shape=out_shape,
        compiler_params=_cparams(("parallel",)),
        name="inproj",
    )(*args)


def _softplus(x):
    return jnp.maximum(x, 0.0) + jnp.log1p(jnp.exp(-jnp.abs(x)))


def _gated_group_norm(y, z, norm_w):
    y = y * _silu(z)
    gw = SSD_D // SSD_GROUPS
    parts = []
    for g in range(SSD_GROUPS):
        yg = y[:, g * gw:(g + 1) * gw]
        ms = jnp.mean(yg * yg, axis=-1, keepdims=True)
        parts.append(yg * lax.rsqrt(ms + RMS_EPS))
    return jnp.concatenate(parts, axis=1) * norm_w


def _ssd_prompt_kernel(xbc_ref, z_ref, dtg_ref, convw_ref, convb_ref, dtb_ref, alog_ref, dskip_ref, normw_ref,
                       y_ref, state_ref, conv_ref, ext_ref, s_ref):
    c = pl.program_id(1)
    nc = pl.num_programs(1)
    L = SSD_CHUNK
    halo = SUBLANES

    @pl.when(c == 0)
    def _():
        ext_ref[0:halo, :] = jnp.zeros((halo, SSD_CONV_CH), F32)
        s_ref[...] = jnp.zeros_like(s_ref)

    xin = xbc_ref[...]
    ext_ref[halo:halo + L, :] = xin
    xc = convw_ref[SSD_CONV - 1:SSD_CONV, :] * xin
    for k in range(SSD_CONV - 1):
        off = halo - (SSD_CONV - 1) + k
        xc = xc + convw_ref[k:k + 1, :] * ext_ref[off:off + L, :]
    ext_ref[0:halo, :] = ext_ref[L:L + halo, :]
    xc = _silu(xc + convb_ref[...])
    xs = xc[:, 0:SSD_D]
    ns = SSD_GROUPS * SSD_STATE
    bm = xc[:, SSD_D:SSD_D + ns]
    cm = xc[:, SSD_D + ns:SSD_D + 2 * ns]

    dt = _softplus(dtg_ref[...] + dtb_ref[...])
    da = dt * (-jnp.exp(alog_ref[...]))
    row = lax.broadcasted_iota(jnp.int32, (L, L), 0)
    col = lax.broadcasted_iota(jnp.int32, (L, L), 1)
    tril = row >= col
    acum = _hdot(tril.astype(F32), da)
    acum_t = acum.T
    eacum = jnp.exp(acum)
    alast = acum[L - 1:L, :]
    edecay = jnp.exp(alast - acum)
    elast = jnp.exp(alast)

    dt_full = jnp.concatenate([jnp.broadcast_to(dt[:, h:h + 1], (L, HEAD_DIM)) for h in range(SSD_HEADS)], 1)
    dec_full = jnp.concatenate([jnp.broadcast_to(edecay[:, h:h + 1], (L, HEAD_DIM)) for h in range(SSD_HEADS)], 1)
    xdt = xs * dt_full
    xdec_t = (xdt * dec_full).T

    hpg = SSD_HEADS // SSD_GROUPS
    y_parts = []
    for h in range(SSD_HEADS):
        g = h // hpg
        b_g = bm[:, g * SSD_STATE:(g + 1) * SSD_STATE]
        c_g = cm[:, g * SSD_STATE:(g + 1) * SSD_STATE]
        if h % hpg == 0:
            cb = _bdot_nt(c_g, b_g)
        seg = acum[:, h:h + 1] - acum_t[h:h + 1, :]
        lmat = jnp.where(tril, jnp.exp(jnp.where(tril, seg, 0.0)), 0.0)
        xdt_h = xdt[:, h * HEAD_DIM:(h + 1) * HEAD_DIM]
        y_h = _bdot(cb * lmat, xdt_h)
        s_prev = s_ref[h]
        y_h = y_h + _bdot_nt(c_g, s_prev) * eacum[:, h:h + 1]
        y_h = y_h + dskip_ref[:, h * HEAD_DIM:(h + 1) * HEAD_DIM] * xs[:, h * HEAD_DIM:(h + 1) * HEAD_DIM]
        y_parts.append(y_h)
        s_ref[h] = elast[:, h:h + 1] * s_prev + _bdot(xdec_t[h * HEAD_DIM:(h + 1) * HEAD_DIM, :], b_g)
    y = jnp.concatenate(y_parts, axis=1)
    y_ref[...] = _gated_group_norm(y, z_ref[...], normw_ref[...]).astype(y_ref.dtype)

    @pl.when(c == nc - 1)
    def _():
        state_ref[0] = s_ref[...]
        conv_ref[0] = xin[L - (SSD_CONV - 1):L, :]


def _ssd_prompt(xbc, z, dtg, conv_w, conv_b, dt_bias_pad, a_log_pad, d_skip_full, norm_w, bn, t):
    nc = t // SSD_CHUNK
    row = lambda w: pl.BlockSpec((SSD_CHUNK, w), lambda b, c: (b * nc + c, 0))
    const = lambda a: pl.BlockSpec(a.shape, lambda b, c: (0,) * a.ndim)
    return pl.pallas_call(
        _ssd_prompt_kernel,
        grid=(bn, nc),
        in_specs=[row(SSD_CONV_CH), row(SSD_D), row(LANES), const(conv_w), const(conv_b), const(dt_bias_pad),
                  const(a_log_pad), const(d_skip_full), const(norm_w)],
        out_specs=[row(SSD_D),
                   pl.BlockSpec((1, SSD_HEADS, HEAD_DIM, SSD_STATE), lambda b, c: (b, 0, 0, 0)),
                   pl.BlockSpec((1, SSD_CONV - 1, SSD_CONV_CH), lambda b, c: (b, 0, 0))],
        out_shape=[jax.ShapeDtypeStruct((bn * t, SSD_D), BF16),
                   jax.ShapeDtypeStruct((bn, SSD_HEADS, HEAD_DIM, SSD_STATE), F32),
                   jax.ShapeDtypeStruct((bn, SSD_CONV - 1, SSD_CONV_CH), F32)],
        scratch_shapes=[pltpu.VMEM((SSD_CHUNK + 2 * SUBLANES, SSD_CONV_CH), F32),
                        pltpu.VMEM((SSD_HEADS, HEAD_DIM, SSD_STATE), F32)],
        compiler_params=_cparams(("parallel", "arbitrary")),
        name="ssd_prompt",
    )(xbc, z, dtg, conv_w, conv_b, dt_bias_pad, a_log_pad, d_skip_full, norm_w)


def _pad_lanes(v, fill=0.0):
    return jnp.concatenate([v.astype(F32), jnp.full((LANES - v.shape[0],), fill, F32)])[None]


def _compress_rows(k_ref, v_ref, pe_ref, w1k_ref, w1v_ref, b1_ref, w2k_ref, w2v_ref, b2_ref, nb):
    acck = jnp.zeros((nb, 2 * CMP_HIDDEN), F32)
    accv = jnp.zeros((nb, 2 * CMP_HIDDEN), F32)
    for l in range(CMP_BLOCK):
        xk = k_ref[pl.ds(l, nb, stride=CMP_BLOCK), :] + pe_ref[l:l + 1, 0:KV_D]
        xv = v_ref[pl.ds(l, nb, stride=CMP_BLOCK), :] + pe_ref[l:l + 1, KV_D:2 * KV_D]
        acck = acck + jnp.dot(xk.astype(BF16), w1k_ref[l], preferred_element_type=F32)
        accv = accv + jnp.dot(xv.astype(BF16), w1v_ref[l], preferred_element_type=F32)
    hk = _silu(acck + b1_ref[:, 0:2 * CMP_HIDDEN])
    hv = _silu(accv + b1_ref[:, 2 * CMP_HIDDEN:4 * CMP_HIDDEN])
    ok = jnp.dot(hk.astype(BF16), w2k_ref[...], preferred_element_type=F32) + b2_ref[:, 0:KV_D]
    ov = jnp.dot(hv.astype(BF16), w2v_ref[...], preferred_element_type=F32) + b2_ref[:, KV_D:2 * KV_D]
    return jnp.concatenate([ok, ov], axis=1)


def _compress_kernel(k_ref, v_ref, pe_ref, w1k_ref, w1v_ref, b1_ref, w2k_ref, w2v_ref, b2_ref, o_ref, *, nb):
    o_ref[...] = _compress_rows(k_ref, v_ref, pe_ref, w1k_ref, w1v_ref, b1_ref, w2k_ref, w2v_ref, b2_ref, nb)


def _block_diag2(w):
    z = jnp.zeros_like(w)
    return jnp.concatenate([jnp.concatenate([w, z], -1), jnp.concatenate([z, w], -1)], -2)


def _compress_consts(cmp_pe, cmp_w1, cmp_b1, cmp_w2, cmp_b2):
    pe = jnp.concatenate([cmp_pe[0], cmp_pe[0], cmp_pe[1], cmp_pe[1]], -1)
    w1k = _block_diag2(cmp_w1[0]).astype(BF16)
    w1v = _block_diag2(cmp_w1[1]).astype(BF16)
    b1 = jnp.concatenate([cmp_b1[0], cmp_b1[0], cmp_b1[1], cmp_b1[1]])[None]
    w2k = _block_diag2(cmp_w2[0]).astype(BF16)
    w2v = _block_diag2(cmp_w2[1]).astype(BF16)
    b2 = jnp.concatenate([cmp_b2[0], cmp_b2[0], cmp_b2[1], cmp_b2[1]])[None]
    return pe, w1k, w1v, b1, w2k, w2v, b2


def _compress_prompt(kvc, consts, rows_per_step):
    n = kvc.shape[0]
    nb = rows_per_step // CMP_BLOCK
    const = lambda a: pl.BlockSpec(a.shape, lambda i: (0,) * a.ndim)
    return pl.pallas_call(
        functools.partial(_compress_kernel, nb=nb),
        grid=(n // rows_per_step,),
        in_specs=[pl.BlockSpec((rows_per_step, KV_D), lambda i: (i, 0)),
                  pl.BlockSpec((rows_per_step, KV_D), lambda i: (i, 1))] + [const(a) for a in consts],
        out_specs=pl.BlockSpec((nb, 2 * KV_D), lambda i: (i, 0)),
        out_shape=jax.ShapeDtypeStruct((n // CMP_BLOCK, 2 * KV_D), F32),
        compiler_params=_cparams(("parallel",)),
        name="compress_prompt",
    )(kvc, kvc, *consts)


SEL_KEY_TILE = 512
WIN_KEYS = WINDOW + Q_BLOCK


def _dup_head(x, hk):
    sw = pltpu.roll(x, HEAD_DIM, 1)
    low = lax.broadcasted_iota(jnp.int32, x.shape, 1) < HEAD_DIM
    return jnp.where(low, x, sw) if hk == 0 else jnp.where(low, sw, x)


def _masked_softmax(s, mask):
    sm = jnp.where(mask, s, NEG)
    ex = jnp.where(mask, jnp.exp(sm - jnp.max(sm, axis=-1, keepdims=True)), 0.0)
    den = jnp.sum(ex, axis=-1, keepdims=True)
    return ex / jnp.where(den > 0.0, den, 1.0)


def _select_blocks_t(imp, cur, n_top):
    j = lax.broadcasted_iota(jnp.int32, imp.shape, 0)
    future = j > cur
    forced = (j == 0) | (j == cur) | (j == cur - 1)
    score = jnp.where(future, NEG, jnp.where(forced, FORCED_SCORE, imp))
    return ((_rank_rows(score) < n_top) & (score > 0.5 * NEG)).astype(F32)


def _nsa_prompt_kernel(qc_ref, qr_ref, dtg_ref, cmp_ref, kvs_ref, kvw_ref, o_ref,
                       cmp_d, kvs_d, kvw_d, bias_ref, qrs_ref, m_ref, l_ref, acc_ref, *, t):
    qb = pl.program_id(1)
    nbk = t // SEL_BLOCK
    tq = Q_BLOCK
    tk = SEL_KEY_TILE
    hpg = NSA_HEADS // NSA_KV_HEADS
    scale = HEAD_DIM ** -0.5

    @pl.when(qb == 0)
    def _():
        cmp_d[...] = jnp.zeros_like(cmp_d)
        for src, dst, n in ((cmp_ref, cmp_d, nbk), (kvs_ref, kvs_d, t), (kvw_ref, kvw_d, t)):
            x = src[...]
            for hk in range(NSA_KV_HEADS):
                dst[hk, 0:n, 0:KV_D] = _dup_head(x[:, 0:KV_D], hk).astype(BF16)
                dst[hk, 0:n, KV_D:2 * KV_D] = _dup_head(x[:, KV_D:2 * KV_D], hk).astype(BF16)

    t0 = qb * tq
    rows = t0 + lax.broadcasted_iota(jnp.int32, (tq, 1), 0)
    lane = lax.broadcasted_iota(jnp.int32, (tq, LANES), 1)
    half_mask = (lane < HEAD_DIM, lane >= HEAD_DIM)
    sig = _sigmoid(dtg_ref[...])
    vis = (lane + 1) * CMP_BLOCK - 1 <= rows
    cur_l = (t0 + lax.broadcasted_iota(jnp.int32, (1, tq), 1)) // SEL_BLOCK
    expand = (lax.broadcasted_iota(jnp.int32, (LANES, t), 1) // SEL_BLOCK
              == lax.broadcasted_iota(jnp.int32, (LANES, t), 0)).astype(BF16)
    win_start = pl.multiple_of(jnp.maximum(t0 - WINDOW, 0), tq)
    wpos = win_start + lax.broadcasted_iota(jnp.int32, (tq, WIN_KEYS), 1)
    win_bias = jnp.where((wpos <= rows) & (wpos >= rows - WINDOW), 0.0, NEG)
    n_kt = (t0 + tq + tk - 1) // tk

    def stack_heads(ref, hk):
        parts = []
        for hh in range(hpg):
            head = hk * hpg + hh
            p, e = head // 2, head % 2
            parts.append(jnp.where(half_mask[e], ref[:, p * LANES:(p + 1) * LANES] * scale, 0.0))
        return jnp.concatenate(parts, axis=0).astype(BF16)

    o_cmp_g = []
    for hk in range(NSA_KV_HEADS):
        qcs = stack_heads(qc_ref, hk)
        s = lax.dot_general(qcs, cmp_d[hk, :, 0:KV_D], (((1,), (1,)), ((), ())), preferred_element_type=F32)
        pc = _masked_softmax(s.reshape(hpg, tq, LANES), vis[None])
        imp = jnp.sum(pc, axis=0)
        o_cmp_g.append(jnp.dot(pc.reshape(hpg * tq, LANES).astype(BF16), cmp_d[hk, :, KV_D:2 * KV_D],
                               preferred_element_type=F32).reshape(hpg, tq, LANES))

        sel_t = _select_blocks_t(imp.T[0:nbk, :], cur_l, TOP_N)
        sel = jnp.concatenate([sel_t, jnp.zeros((LANES - nbk, tq), F32)], axis=0).T
        selk = jnp.dot(sel.astype(BF16), expand, preferred_element_type=F32)
        for kt in range(t // tk):
            @pl.when(kt < n_kt)
            def _(kt=kt, hk=hk, selk=selk):
                kpos = kt * tk + lax.broadcasted_iota(jnp.int32, (tq, tk), 1)
                bias_ref[hk, kt] = jnp.where((selk[:, kt * tk:(kt + 1) * tk] > 0.5) & (kpos <= rows), 0.0, NEG)

        qrs_ref[hk] = stack_heads(qr_ref, hk)

    m_ref[...] = jnp.full(m_ref.shape, NEG, F32)
    l_ref[...] = jnp.zeros(l_ref.shape, F32)
    acc_ref[...] = jnp.zeros(acc_ref.shape, F32)

    def sel_step(kt, carry):
        k0 = pl.multiple_of(kt * tk, tk)
        for hk in range(NSA_KV_HEADS):
            kblk = kvs_d[hk, pl.ds(k0, tk), 0:KV_D]
            vblk = kvs_d[hk, pl.ds(k0, tk), KV_D:2 * KV_D]
            s = lax.dot_general(qrs_ref[hk], kblk, (((1,), (1,)), ((), ())), preferred_element_type=F32)
            s = s.reshape(hpg, tq, tk) + bias_ref[hk, kt][None]
            m_old = m_ref[hk]
            m_new = jnp.maximum(m_old, jnp.max(s, axis=-1, keepdims=True))
            alpha = jnp.exp(m_old - m_new)
            pe = jnp.exp(s - jnp.concatenate([m_new] * (tk // LANES), axis=-1))
            l_ref[hk] = alpha * l_ref[hk] + jnp.sum(pe, axis=-1, keepdims=True)
            pv = jnp.dot(pe.reshape(hpg * tq, tk).astype(BF16), vblk, preferred_element_type=F32)
            acc_ref[hk] = alpha * acc_ref[hk] + pv.reshape(hpg, tq, LANES)
            m_ref[hk] = m_new
        return carry

    lax.fori_loop(0, n_kt, sel_step, 0)

    for hk in range(NSA_KV_HEADS):
        o_cmp = o_cmp_g[hk]
        o_slc = acc_ref[hk] / l_ref[hk]
        kw = kvw_d[hk, pl.ds(win_start, WIN_KEYS), 0:KV_D]
        vw = kvw_d[hk, pl.ds(win_start, WIN_KEYS), KV_D:2 * KV_D]
        sw = lax.dot_general(qrs_ref[hk], kw, (((1,), (1,)), ((), ())), preferred_element_type=F32)
        sw = sw.reshape(hpg, tq, WIN_KEYS) + win_bias[None]
        pw = jnp.exp(sw - jnp.max(sw, axis=-1, keepdims=True))
        den = jnp.sum(pw, axis=-1, keepdims=True)
        o_win = jnp.dot(pw.reshape(hpg * tq, WIN_KEYS).astype(BF16), vw,
                        preferred_element_type=F32).reshape(hpg, tq, LANES) / den

        for hh in range(hpg):
            head = hk * hpg + hh
            p, e = head // 2, head % 2
            c0 = GATE_COL0 + head * 3
            mix = (sig[:, c0:c0 + 1] * o_cmp[hh] + sig[:, c0 + 1:c0 + 2] * o_slc[hh]
                   + sig[:, c0 + 2:c0 + 3] * o_win[hh])
            if e == 0:
                mix_even = mix
            else:
                o_ref[:, p * LANES:(p + 1) * LANES] = jnp.where(half_mask[0], mix_even, mix).astype(o_ref.dtype)


def _nsa_prompt(qc, qr, dtg, kvcmp, kvs, kvw, bn, t):
    nq = t // Q_BLOCK
    nbk = t // SEL_BLOCK
    hpg = NSA_HEADS // NSA_KV_HEADS
    assert nbk >= TOP_N and t >= WIN_KEYS and t % SEL_KEY_TILE == 0
    qrow = lambda w: pl.BlockSpec((Q_BLOCK, w), lambda b, i: (b * nq + i, 0))
    seq = lambda r: pl.BlockSpec((r, 2 * KV_D), lambda b, i: (b, 0))
    return pl.pallas_call(
        functools.partial(_nsa_prompt_kernel, t=t),
        grid=(bn, nq),
        in_specs=[qrow(NSA_D), qrow(NSA_D), qrow(LANES), seq(nbk), seq(t), seq(t)],
        out_specs=qrow(NSA_D),
        out_shape=jax.ShapeDtypeStruct((bn * t, NSA_D), BF16),
        scratch_shapes=[pltpu.VMEM((NSA_KV_HEADS, LANES, 2 * KV_D), BF16),
                        pltpu.VMEM((NSA_KV_HEADS, t, 2 * KV_D), BF16),
                        pltpu.VMEM((NSA_KV_HEADS, t, 2 * KV_D), BF16),
                        pltpu.VMEM((NSA_KV_HEADS, t // SEL_KEY_TILE, Q_BLOCK, SEL_KEY_TILE), F32),
                        pltpu.VMEM((NSA_KV_HEADS, hpg * Q_BLOCK, LANES), BF16),
                        pltpu.VMEM((NSA_KV_HEADS, hpg, Q_BLOCK, LANES), F32),
                        pltpu.VMEM((NSA_KV_HEADS, hpg, Q_BLOCK, LANES), F32),
                        pltpu.VMEM((NSA_KV_HEADS, hpg, Q_BLOCK, LANES), F32)],
        compiler_params=_cparams(("parallel", "arbitrary")),
        name="nsa_prompt",
    )(qc, qr, dtg, kvcmp, kvs, kvw)


def _outproj_kernel(ys_ref, yn_ref, h_ref, ws_ref, wn_ref, g_ref, b_ref, o_ref):
    mix = jnp.dot(ys_ref[...].astype(BF16), ws_ref[...], preferred_element_type=F32)
    mix = mix + jnp.dot(yn_ref[...].astype(BF16), wn_ref[...], preferred_element_type=F32)
    o_ref[...] = _layer_norm(DEEPNORM_ALPHA * h_ref[...] + mix, g_ref[...], b_ref[...])


def _outproj(y_ssd, y_nsa, h, w_ssd, w_nsa, ln_g, ln_b, tm):
    n = h.shape[0]
    row = lambda w: pl.BlockSpec((tm, w), lambda i: (i, 0))
    const = lambda a: pl.BlockSpec(a.shape, lambda i: (0,) * a.ndim)
    return pl.pallas_call(
        _outproj_kernel,
        grid=(n // tm,),
        in_specs=[row(SSD_D), row(NSA_D), row(D_MODEL), const(w_ssd), const(w_nsa), const(ln_g), const(ln_b)],
        out_specs=row(D_MODEL),
        out_shape=jax.ShapeDtypeStruct((n, D_MODEL), F32),
        compiler_params=_cparams(("parallel",)),
        name="outproj",
    )(y_ssd, y_nsa, h, w_ssd, w_nsa, ln_g, ln_b)


MOE_TOKENS = 256
ROUTE_ROWS = 8


def _token_tile_specs(n_main_tiles):
    main = pl.BlockSpec((MOE_TOKENS, D_MODEL), lambda i, *_: (jnp.minimum(i, n_main_tiles - 1), 0))
    tail = pl.BlockSpec((MOE_TOKENS, D_MODEL), lambda i, *_: (0, 0))
    return main, tail


def _token_tile(i, n_main_tiles, main_ref, tail_ref):
    return jnp.where(i < n_main_tiles, main_ref[...], tail_ref[...])


def _rank_rows(x):
    n = x.shape[0]
    idx = lax.broadcasted_iota(jnp.int32, x.shape, 0)
    rank = jnp.zeros(x.shape, F32)
    for r in range(n):
        row = x[r:r + 1, :]
        rank = rank + ((row > x) | ((row == x) & (idx > r))).astype(F32)
    return rank


def _route_kernel(h_ref, ht_ref, rw_ref, rb_ref, slot_ref, tokinfo_ref, meta_ref, cnt_ref, carry_ref, carry_row_ref, *,
                  n_valid, n_main):
    i = pl.program_id(0)
    tm = MOE_TOKENS

    @pl.when(i == 0)
    def _():
        carry_ref[...] = jnp.zeros_like(carry_ref)
        carry_row_ref[...] = jnp.zeros_like(carry_row_ref)

    logits = lax.dot_general(rw_ref[...], _token_tile(i, n_main, h_ref, ht_ref).astype(BF16), (((1,), (1,)), ((), ())),
                             preferred_element_type=F32)
    scores = _sigmoid(logits)
    biased = scores + rb_ref[:, 0:1]
    b3 = biased.reshape(N_EXPERT_GROUPS, EXPERTS_PER_GROUP, tm)
    sidx = lax.broadcasted_iota(jnp.int32, b3.shape, 1)
    m1 = jnp.max(b3, axis=1, keepdims=True)
    first = jnp.min(jnp.where(b3 == m1, sidx, EXPERTS_PER_GROUP), axis=1, keepdims=True)
    m2 = jnp.max(jnp.where(sidx == first, -jnp.inf, b3), axis=1, keepdims=True)
    grp_score = (m1 + m2).reshape(N_EXPERT_GROUPS, tm)
    grp_keep = _rank_rows(grp_score) < TOPK_GROUPS
    masked = jnp.where(grp_keep.reshape(N_EXPERT_GROUPS, 1, tm), b3, NEG).reshape(N_EXPERTS, tm)
    rank = _rank_rows(masked)
    tok = i * tm + lax.broadcasted_iota(jnp.int32, (1, tm), 1)
    valid = tok < n_valid
    sel = (rank < TOP_K) & valid
    self32 = sel.astype(F32)
    wsel = self32 * scores
    wsum = jnp.sum(wsel, axis=0, keepdims=True)
    w = wsel / jnp.where(wsum > 0.0, wsum, 1.0) * ROUTED_SCALE

    selb = sel.astype(BF16)
    tri = lambda n, strict_upper: (
        (lax.broadcasted_iota(jnp.int32, (n, n), 0) < lax.broadcasted_iota(jnp.int32, (n, n), 1))
        if strict_upper else
        (lax.broadcasted_iota(jnp.int32, (n, n), 0) > lax.broadcasted_iota(jnp.int32, (n, n), 1))).astype(BF16)
    pad8 = lambda c: jnp.floor((c + (SUBLANES - 1.0)) * (1.0 / SUBLANES)) * SUBLANES
    pos_tile = jnp.dot(selb, tri(tm, True), preferred_element_type=F32)
    cnt_col = pad8(jnp.sum(self32, axis=1, keepdims=True))
    first_col = jnp.dot(tri(N_EXPERTS, False), jnp.broadcast_to(cnt_col, (N_EXPERTS, LANES)).astype(BF16),
                        preferred_element_type=F32)[:, 0:1]
    slot = first_col + pos_tile

    sel_pad = jnp.concatenate([selb, jnp.zeros((LANES - N_EXPERTS, tm), BF16)], axis=0)
    cnt_row = pad8(lax.dot_general(jnp.ones((SUBLANES, tm), BF16), sel_pad, (((1,), (1,)), ((), ())),
                                   preferred_element_type=F32))
    first_row = jnp.dot(cnt_row.astype(BF16), tri(LANES, True), preferred_element_type=F32)
    prev_row = carry_row_ref[...]
    meta = jnp.concatenate([cnt_row[0:1], first_row[0:1], prev_row[0:1], jnp.zeros((SUBLANES - 3, LANES), F32)], 0)
    meta_ref[0] = meta.astype(jnp.int32)
    carry_row_ref[...] = prev_row + cnt_row
    carry_ref[...] = carry_ref[...] + cnt_col

    slot_rows, w_rows = [], []
    for k in range(TOP_K):
        hit = (rank == k) & sel
        slot_rows.append(jnp.sum(jnp.where(hit, slot, 0.0), axis=0, keepdims=True))
        w_rows.append(jnp.sum(jnp.where(hit, w, 0.0), axis=0, keepdims=True))
    slot_rows = [jnp.where(valid, r, -1.0) for r in slot_rows]
    pad2 = jnp.zeros((ROUTE_ROWS - TOP_K, tm), F32)
    slot_ref[...] = jnp.concatenate(slot_rows + [pad2 - 1.0], 0).astype(jnp.int32)
    info = jnp.concatenate(w_rows + [pad2] + slot_rows + [jnp.zeros((LANES - ROUTE_ROWS - TOP_K, tm), F32)], 0)
    tokinfo_ref[...] = info.T

    @pl.when(i == pl.num_programs(0) - 1)
    def _():
        cnt_ref[...] = jnp.broadcast_to(carry_ref[:, 0:1], cnt_ref.shape)


def _route(h_main, h_tail, router_wt, router_bias_col, n_valid):
    tm = MOE_TOKENS
    n_main = h_main.shape[0] // tm
    n = h_main.shape[0] + tm
    const = lambda a: pl.BlockSpec(a.shape, lambda i: (0,) * a.ndim)
    return pl.pallas_call(
        functools.partial(_route_kernel, n_valid=n_valid, n_main=n_main),
        grid=(n // tm,),
        in_specs=[*_token_tile_specs(n_main), const(router_wt), const(router_bias_col)],
        out_specs=[pl.BlockSpec((ROUTE_ROWS, tm), lambda i: (0, i)),
                   pl.BlockSpec((tm, LANES), lambda i: (i, 0)),
                   pl.BlockSpec((1, SUBLANES, LANES), lambda i: (i, 0, 0)),
                   pl.BlockSpec((N_EXPERTS, LANES), lambda i: (0, 0))],
        out_shape=[jax.ShapeDtypeStruct((ROUTE_ROWS, n), jnp.int32),
                   jax.ShapeDtypeStruct((n, LANES), F32),
                   jax.ShapeDtypeStruct((n // tm, SUBLANES, LANES), jnp.int32),
                   jax.ShapeDtypeStruct((N_EXPERTS, LANES), F32)],
        scratch_shapes=[pltpu.VMEM((N_EXPERTS, LANES), F32), pltpu.VMEM((SUBLANES, LANES), F32)],
        compiler_params=_cparams(("arbitrary",)),
        name="moe_route",
    )(h_main, h_tail, router_wt, router_bias_col)


TILE_SLOTS = MOE_TOKENS * TOP_K + N_EXPERTS * SUBLANES
RUN_CHUNKS = tuple(1 << b for b in range(int(math.log2(MOE_TOKENS)), int(math.log2(SUBLANES)) - 1, -1))


def _run_copy(src_ref, src_row, dst_ref, dst_row, rows, sem):
    return pltpu.make_async_copy(src_ref.at[pl.ds(pl.multiple_of(src_row, SUBLANES), rows)],
                                 dst_ref.at[pl.ds(pl.multiple_of(dst_row, SUBLANES), rows)], sem)


def _start_run(src_ref, src_row, dst_ref, dst_row, n, sem, started):
    off = jnp.int32(0)
    out = []
    for c, rows in enumerate(RUN_CHUNKS):
        take = (n & rows) != 0

        @pl.when(take)
        def _(off=off, rows=rows):
            _run_copy(src_ref, src_row + off, dst_ref, dst_row + off, rows, sem).start()

        inc = take.astype(jnp.int32)
        off = off + inc * rows
        out.append(started[c] + inc)
    return tuple(out)


def _wait_runs(src_ref, dst_ref, sem, started):
    for c, rows in enumerate(RUN_CHUNKS):
        def wait_one(j, carry, rows=rows):
            _run_copy(src_ref, 0, dst_ref, 0, rows, sem).wait()
            return carry

        lax.fori_loop(0, started[c], wait_one, 0)


def _dispatch_kernel(start_ref, cnt_ref, meta_ref, slot_ref, x_ref, xt_ref, xs_ref, sorted_ref, zero_ref, sem, zsem, *,
                     cap, n_main):
    i = pl.program_id(0)
    tm = MOE_TOKENS

    @pl.when(i == 0)
    def _():
        zero_ref[...] = jnp.zeros_like(zero_ref)

        def fill_expert(e, started):
            lo = start_ref[e] + cnt_ref[e]
            hi = jnp.where(e == N_EXPERTS - 1, cap, start_ref[jnp.minimum(e + 1, N_EXPERTS - 1)])
            n_full = (hi - lo) // tm

            def fill_full(j, st):
                return _start_run(zero_ref, 0, xs_ref, lo + j * tm, jnp.int32(tm), zsem, st)

            started = lax.fori_loop(0, n_full, fill_full, started)
            return _start_run(zero_ref, 0, xs_ref, lo + n_full * tm, (hi - lo) - n_full * tm, zsem, started)

        filled = lax.fori_loop(0, N_EXPERTS, fill_expert, tuple(jnp.int32(0) for _ in RUN_CHUNKS))
        _wait_runs(zero_ref, xs_ref, zsem, filled)

    srow = lax.broadcasted_iota(jnp.int32, (TILE_SLOTS, tm), 0)
    onehot = srow == slot_ref[0:1, :]
    for k in range(1, TOP_K):
        onehot = onehot | (srow == slot_ref[k:k + 1, :])
    sorted_ref[...] = jnp.dot(onehot.astype(BF16), _token_tile(i, n_main, x_ref, xt_ref).astype(BF16),
                              preferred_element_type=F32)

    def copy_expert(e, started):
        n = meta_ref[0, 0, e]
        return _start_run(sorted_ref, meta_ref[0, 1, e], xs_ref, start_ref[e] + meta_ref[0, 2, e], n, sem, started)

    started = lax.fori_loop(0, N_EXPERTS, copy_expert, tuple(jnp.int32(0) for _ in RUN_CHUNKS))
    _wait_runs(sorted_ref, xs_ref, sem, started)


def _dispatch(h_main, h_tail, slot_t, meta, seg_start, counts, cap):
    tm = MOE_TOKENS
    n_main = h_main.shape[0] // tm
    return pl.pallas_call(
        functools.partial(_dispatch_kernel, cap=cap, n_main=n_main),
        grid_spec=pltpu.PrefetchScalarGridSpec(
            num_scalar_prefetch=2,
            grid=(n_main + 1,),
            in_specs=[pl.BlockSpec((1, SUBLANES, LANES), lambda i, *_: (i, 0, 0), memory_space=pltpu.SMEM),
                      pl.BlockSpec((ROUTE_ROWS, tm), lambda i, *_: (0, i)),
                      *_token_tile_specs(n_main)],
            out_specs=pl.BlockSpec(memory_space=pl.ANY),
            scratch_shapes=[pltpu.VMEM((TILE_SLOTS, D_MODEL), F32), pltpu.VMEM((tm, D_MODEL), F32),
                            pltpu.SemaphoreType.DMA, pltpu.SemaphoreType.DMA]),
        out_shape=jax.ShapeDtypeStruct((cap, D_MODEL), F32),
        compiler_params=_cparams(("arbitrary",)),
        name="moe_dispatch",
    )(seg_start, counts, meta, slot_t, h_main, h_tail)


def _swiglu(x, wg, wu, wd):
    xb = x.astype(BF16)
    g = jnp.dot(xb, wg.astype(BF16), preferred_element_type=F32)
    u = jnp.dot(xb, wu.astype(BF16), preferred_element_type=F32)
    return jnp.dot((_silu(g) * u).astype(BF16), wd.astype(BF16), preferred_element_type=F32)


EXPERT_RING = 3


def _experts_kernel(be_ref, used_ref, xs_ref, wg_ref, wu_ref, wd_ref, y_ref, xbuf, sems):
    i = pl.program_id(0)
    n = pl.num_programs(0)

    def block_copy(blk):
        slot = lax.rem(blk, EXPERT_RING)
        rows = pl.ds(pl.multiple_of(blk * MOE_BLOCK, MOE_BLOCK), MOE_BLOCK)
        return pltpu.make_async_copy(xs_ref.at[rows], xbuf.at[slot], sems.at[slot])

    @pl.when(i == 0)
    def _():
        for j in range(EXPERT_RING - 1):
            block_copy(jnp.int32(j)).start()

    @pl.when(i + EXPERT_RING - 1 < n)
    def _():
        block_copy(i + EXPERT_RING - 1).start()

    block_copy(i).wait()

    @pl.when(i < used_ref[0])
    def _():
        y_ref[...] = _swiglu(xbuf[lax.rem(i, EXPERT_RING)], wg_ref[0], wu_ref[0], wd_ref[0])

    @pl.when(i >= used_ref[0])
    def _():
        y_ref[...] = jnp.zeros_like(y_ref)


def _experts(xs, block_expert, used_blocks, w_gate, w_up, w_down):
    cap = xs.shape[0]
    return pl.pallas_call(
        _experts_kernel,
        grid_spec=pltpu.PrefetchScalarGridSpec(
            num_scalar_prefetch=2,
            grid=(cap // MOE_BLOCK,),
            in_specs=[pl.BlockSpec(memory_space=pl.ANY),
                      pl.BlockSpec((1, D_MODEL, D_EXPERT), lambda i, be, used: (be[i], 0, 0)),
                      pl.BlockSpec((1, D_MODEL, D_EXPERT), lambda i, be, used: (be[i], 0, 0)),
                      pl.BlockSpec((1, D_EXPERT, D_MODEL), lambda i, be, used: (be[i], 0, 0))],
            out_specs=pl.BlockSpec((MOE_BLOCK, D_MODEL), lambda i, be, used: (i, 0)),
            scratch_shapes=[pltpu.VMEM((EXPERT_RING, MOE_BLOCK, D_MODEL), F32),
                            pltpu.SemaphoreType.DMA((EXPERT_RING,))]),
        out_shape=jax.ShapeDtypeStruct((cap, D_MODEL), F32),
        compiler_params=_cparams(("arbitrary",)),
        name="moe_experts",
    )(block_expert, used_blocks, xs, w_gate, w_up, w_down)


def _combine_kernel(start_ref, meta_ref, h_ref, ht_ref, info_ref, sg_ref, su_ref, sd_ref, g_ref, b_ref,
                    ys_ref, o_ref, ot_ref, buf_ref, sem, *, n_main):
    i = pl.program_id(0)
    tm = MOE_TOKENS

    @pl.when(i == 0)
    def _():
        buf_ref[...] = jnp.zeros_like(buf_ref)

    def fetch_expert(e, started):
        n = meta_ref[0, 0, e]
        return _start_run(ys_ref, start_ref[e] + meta_ref[0, 2, e], buf_ref, meta_ref[0, 1, e], n, sem, started)

    started = lax.fori_loop(0, N_EXPERTS, fetch_expert, tuple(jnp.int32(0) for _ in RUN_CHUNKS))
    h = _token_tile(i, n_main, h_ref, ht_ref)
    f = _swiglu(h, sg_ref[...], su_ref[...], sd_ref[...])
    info = info_ref[...]
    scol = lax.broadcasted_iota(jnp.int32, (tm, TILE_SLOTS), 1).astype(F32)
    mix = jnp.zeros((tm, TILE_SLOTS), F32)
    for k in range(TOP_K):
        mix = mix + jnp.where(info[:, ROUTE_ROWS + k:ROUTE_ROWS + k + 1] == scol, info[:, k:k + 1], 0.0)
    _wait_runs(ys_ref, buf_ref, sem, started)
    acc = jnp.dot(mix.astype(BF16), buf_ref[...].astype(BF16), preferred_element_type=F32)
    out = _layer_norm(DEEPNORM_ALPHA * h + (acc + f), g_ref[...], b_ref[...])

    @pl.when(i < n_main)
    def _():
        o_ref[...] = out

    @pl.when(i >= n_main)
    def _():
        ot_ref[...] = out


def _combine(h_main, h_tail, ys, meta, tokinfo, seg_start, sh_gate, sh_up, sh_down, ln_g, ln_b):
    tm = MOE_TOKENS
    n_main = h_main.shape[0] // tm
    const = lambda a: pl.BlockSpec(a.shape, lambda i, *_: (0,) * a.ndim)
    return pl.pallas_call(
        functools.partial(_combine_kernel, n_main=n_main),
        grid_spec=pltpu.PrefetchScalarGridSpec(
            num_scalar_prefetch=1,
            grid=(n_main + 1,),
            in_specs=[pl.BlockSpec((1, SUBLANES, LANES), lambda i, *_: (i, 0, 0), memory_space=pltpu.SMEM),
                      *_token_tile_specs(n_main),
                      pl.BlockSpec((tm, LANES), lambda i, *_: (i, 0)),
                      const(sh_gate), const(sh_up), const(sh_down), const(ln_g), const(ln_b),
                      pl.BlockSpec(memory_space=pl.ANY)],
            out_specs=list(_token_tile_specs(n_main)),
            scratch_shapes=[pltpu.VMEM((TILE_SLOTS, D_MODEL), F32), pltpu.SemaphoreType.DMA]),
        out_shape=[jax.ShapeDtypeStruct(h_main.shape, F32), jax.ShapeDtypeStruct((tm, D_MODEL), F32)],
        compiler_params=_cparams(("arbitrary",)),
        name="moe_combine",
    )(seg_start, meta, h_main, h_tail, tokinfo, sh_gate, sh_up, sh_down, ln_g, ln_b, ys)


def _moe_ln(h_main, h_tail, n_valid, router_w, router_bias, w_gate, w_up, w_down, sh_gate, sh_up, sh_down, ln_g, ln_b):
    n_tiles = h_main.shape[0] // MOE_TOKENS + 1
    slot_t, tokinfo, meta, cnt = _route(h_main, h_tail, router_w.T.astype(BF16),
                                        jnp.broadcast_to(router_bias.astype(F32)[:, None], (N_EXPERTS, LANES)), n_valid)
    counts = cnt[:, 0].astype(jnp.int32)
    padded = (counts + MOE_BLOCK - 1) // MOE_BLOCK * MOE_BLOCK
    seg_end = jnp.cumsum(padded)
    seg_start = seg_end - padded
    run_pad = n_tiles * N_EXPERTS * (SUBLANES - 1)
    n_blocks = -(-(n_valid * TOP_K + run_pad + N_EXPERTS * (MOE_BLOCK - 1)) // MOE_BLOCK)
    cap = n_blocks * MOE_BLOCK
    block_first_row = jnp.arange(n_blocks, dtype=jnp.int32) * MOE_BLOCK
    block_expert = jnp.minimum(jnp.sum((seg_end[None, :] <= block_first_row[:, None]).astype(jnp.int32), axis=1),
                               N_EXPERTS - 1)
    xs = _dispatch(h_main, h_tail, slot_t, meta, seg_start, counts, cap)
    used_blocks = (seg_end[N_EXPERTS - 1:] // MOE_BLOCK).astype(jnp.int32)
    ys = _experts(xs, block_expert, used_blocks, w_gate, w_up, w_down)
    return _combine(h_main, h_tail, ys, meta, tokinfo, seg_start, sh_gate.astype(BF16), sh_up.astype(BF16),
                    sh_down.astype(BF16), ln_g, ln_b)


def _ssd_sample_kernel(xbc_ref, z_ref, dtg_ref, sconv_ref, s0_ref, convw_ref, convb_ref, dtb_ref, alog_ref,
                       dskip_ref, normw_ref, y_ref, s_ref, conv_out_ref, xc_ref, dt_ref, da_ref):
    b = pl.program_id(0)

    @pl.when(b == 0)
    def _():
        xin = xbc_ref[...]
        xc = convw_ref[SSD_CONV - 1:SSD_CONV, :] * xin
        for k in range(SSD_CONV - 1):
            xc = xc + convw_ref[k:k + 1, :] * sconv_ref[k]
        xc_ref[...] = _silu(xc + convb_ref[...])
        dt = _softplus(dtg_ref[...] + dtb_ref[...])
        dt_ref[...] = dt
        da_ref[...] = jnp.exp(dt * (-jnp.exp(alog_ref[...])))
        for k in range(SSD_CONV - 2):
            conv_out_ref[k] = sconv_ref[k + 1]
        conv_out_ref[SSD_CONV - 2] = xin

    xc = xc_ref[pl.ds(b, 1), :]
    dt = dt_ref[pl.ds(b, 1), :]
    da = da_ref[pl.ds(b, 1), :]
    ns = SSD_GROUPS * SSD_STATE
    eye = (lax.broadcasted_iota(jnp.int32, (HEAD_DIM, HEAD_DIM), 0)
           == lax.broadcasted_iota(jnp.int32, (HEAD_DIM, HEAD_DIM), 1))
    hpg = SSD_HEADS // SSD_GROUPS
    y_parts = []
    for h in range(SSD_HEADS):
        g = h // hpg
        x_h = xc[:, h * HEAD_DIM:(h + 1) * HEAD_DIM]
        b_g = xc[:, SSD_D + g * SSD_STATE:SSD_D + (g + 1) * SSD_STATE]
        c_g = xc[:, SSD_D + ns + g * SSD_STATE:SSD_D + ns + (g + 1) * SSD_STATE]
        xdt_col = jnp.sum(jnp.where(eye, x_h * dt[:, h:h + 1], 0.0), axis=1, keepdims=True)
        s_new = da[:, h:h + 1] * s0_ref[0, h] + xdt_col * b_g
        s_ref[0, h] = s_new
        y_h = _bdot_nt(c_g, s_new) + dskip_ref[:, h * HEAD_DIM:(h + 1) * HEAD_DIM] * x_h
        y_parts.append(y_h)
    y = jnp.concatenate(y_parts, axis=1)
    y_ref[pl.ds(b, 1), :] = _gated_group_norm(y, z_ref[pl.ds(b, 1), :], normw_ref[...])


def _ssd_sample(xbc, z, dtg, state_conv_t, state_ssm, conv_w, conv_b, dt_bias_pad, a_log_pad, d_skip_full, norm_w):
    bs = xbc.shape[0]
    const = lambda a: pl.BlockSpec(a.shape, lambda b: (0,) * a.ndim)
    state_spec = pl.BlockSpec((1, SSD_HEADS, HEAD_DIM, SSD_STATE), lambda b: (b, 0, 0, 0))
    return pl.pallas_call(
        _ssd_sample_kernel,
        grid=(bs,),
        in_specs=[const(xbc), const(z), const(dtg), const(state_conv_t), state_spec, const(conv_w), const(conv_b),
                  const(dt_bias_pad), const(a_log_pad), const(d_skip_full), const(norm_w)],
        out_specs=[pl.BlockSpec((bs, SSD_D), lambda b: (0, 0)), state_spec,
                   pl.BlockSpec((SSD_CONV - 1, bs, SSD_CONV_CH), lambda b: (0, 0, 0))],
        out_shape=[jax.ShapeDtypeStruct((bs, SSD_D), F32),
                   jax.ShapeDtypeStruct(state_ssm.shape, F32),
                   jax.ShapeDtypeStruct((SSD_CONV - 1, bs, SSD_CONV_CH), F32)],
        scratch_shapes=[pltpu.VMEM((bs, SSD_CONV_CH), F32), pltpu.VMEM((bs, LANES), F32),
                        pltpu.VMEM((bs, LANES), F32)],
        compiler_params=_cparams(("arbitrary",)),
        name="ssd_sample",
    )(xbc, z, dtg, state_conv_t, state_ssm, conv_w, conv_b, dt_bias_pad, a_log_pad, d_skip_full, norm_w)


SEL_PAST = TOP_N - 1
BLOCKS_PER_PAGE = PAGE_SIZE // CMP_BLOCK
KV_FEATS = 2 * KV_D


def _compress_consts_t(cmp_pe, cmp_w1, cmp_b1, cmp_w2, cmp_b2):
    pe_t = jnp.stack([jnp.tile(cmp_pe[k].T, (1, BLOCKS_PER_PAGE)) for k in range(2)])
    w1_t = jnp.stack([_block_diag2(jnp.swapaxes(cmp_w1[k], 0, 1)) for k in range(2)]).astype(BF16)
    b1_t = jnp.stack([jnp.tile(cmp_b1[k], BLOCKS_PER_PAGE) for k in range(2)])[:, None, :]
    w2_t = jnp.stack([_block_diag2(cmp_w2[k]) for k in range(2)]).astype(BF16)
    b2_t = jnp.stack([jnp.tile(cmp_b2[k], BLOCKS_PER_PAGE) for k in range(2)])[:, None, :]
    return pe_t, w1_t, b1_t, w2_t, b2_t


def _compress_pages_kernel(pt_ref, pe_ref, w1_ref, b1_ref, w2_ref, b2_ref, pool_ref, o_ref, kbuf, vbuf, sems, *,
                           n_pages):
    b = pl.program_id(0)
    nb = pl.num_programs(0)
    bufs = (kbuf, vbuf)

    def half_copy(seq, kind, p):
        return pltpu.make_async_copy(pool_ref.at[pt_ref[seq, p], pl.ds(kind * KV_D, KV_D)],
                                     bufs[kind].at[:, p], sems.at[kind])

    def start_half(seq, kind):
        lax.fori_loop(0, n_pages, lambda p, c: (half_copy(seq, kind, p).start(), c)[1], 0)

    def wait_half(seq, kind):
        lax.fori_loop(0, n_pages, lambda p, c: (half_copy(seq, kind, p).wait(), c)[1], 0)

    @pl.when(b == 0)
    def _():
        start_half(b, 0)
        start_half(b, 1)

    for kind in range(2):
        wait_half(b, kind)
        def add_feature(d, acc, kind=kind):
            x = jnp.concatenate([bufs[kind][h * HEAD_DIM + d] for h in range(NSA_KV_HEADS)], axis=0) \
                + pe_ref[kind, pl.ds(d, 1), :]
            return acc + jnp.dot(x.astype(BF16), w1_ref[kind, d], preferred_element_type=F32)

        acc = lax.fori_loop(0, HEAD_DIM, add_feature,
                            jnp.zeros((NSA_KV_HEADS * n_pages, BLOCKS_PER_PAGE * CMP_HIDDEN), F32), unroll=8)
        hid = _silu(acc + b1_ref[kind])
        out = jnp.dot(hid.astype(BF16), w2_ref[kind], preferred_element_type=F32) + b2_ref[kind]
        for h in range(NSA_KV_HEADS):
            o_ref[0, kind * NSA_KV_HEADS + h] = out[h * n_pages:(h + 1) * n_pages]

        @pl.when(b + 1 < nb)
        def _(kind=kind):
            start_half(b + 1, kind)


def _compress_pages(pool_t, page_table, consts):
    bs, n_pages = page_table.shape
    const = lambda a: pl.BlockSpec(a.shape, lambda b, pt: (0,) * a.ndim)
    return pl.pallas_call(
        functools.partial(_compress_pages_kernel, n_pages=n_pages),
        grid_spec=pltpu.PrefetchScalarGridSpec(
            num_scalar_prefetch=1,
            grid=(bs,),
            in_specs=[const(a) for a in consts] + [pl.BlockSpec(memory_space=pl.ANY)],
            out_specs=pl.BlockSpec((1, 2 * NSA_KV_HEADS, n_pages, LANES), lambda b, pt: (b, 0, 0, 0)),
            scratch_shapes=[pltpu.VMEM((KV_D, n_pages, PAGE_SIZE), F32), pltpu.VMEM((KV_D, n_pages, PAGE_SIZE), F32),
                            pltpu.SemaphoreType.DMA((2,))]),
        out_shape=jax.ShapeDtypeStruct((bs, 2 * NSA_KV_HEADS, n_pages, LANES), F32),
        compiler_params=_cparams(("arbitrary",)),
        name="compress_pages",
    )(page_table, *consts, pool_t)


def _group_heads(q_row, hk):
    hpg = NSA_HEADS // NSA_KV_HEADS
    low = lax.broadcasted_iota(jnp.int32, (1, LANES), 1) < HEAD_DIM
    rows = []
    for r in range(hpg):
        head = hk * hpg + r
        tile = q_row[:, (head // 2) * LANES:(head // 2 + 1) * LANES]
        if head % 2 == 1:
            tile = pltpu.roll(tile, HEAD_DIM, 1)
        rows.append(jnp.where(low, tile, 0.0))
    return jnp.concatenate(rows + [jnp.zeros((SUBLANES - hpg, LANES), F32)], axis=0)


def _spread_heads(o_groups):
    hpg = NSA_HEADS // NSA_KV_HEADS
    return jnp.concatenate([o[r:r + 1, 0:HEAD_DIM] for o in o_groups for r in range(hpg)], axis=1)


def _nsa_sample_cmp_t_kernel(qc_ref, cmp_ref, ocmp_ref, idx_ref, *, n_pages):
    b = pl.program_id(0)
    nc = n_pages * BLOCKS_PER_PAGE
    scale = HEAD_DIM ** -0.5
    hpg = NSA_HEADS // NSA_KV_HEADS
    q_row = qc_ref[pl.ds(b, 1), :] * scale
    lane = lax.broadcasted_iota(jnp.int32, (1, LANES), 1)
    pos_r = lax.broadcasted_iota(jnp.int32, (1, nc), 1)
    bid_r = (pos_r % n_pages) * BLOCKS_PER_PAGE + pos_r // n_pages
    pos_c = lax.broadcasted_iota(jnp.int32, (nc, 1), 0)
    bid_c = (pos_c % n_pages) * BLOCKS_PER_PAGE + pos_c // n_pages
    o_groups = []
    for hk in range(NSA_KV_HEADS):
        kc = cmp_ref[0, hk].astype(BF16)
        vc = cmp_ref[0, NSA_KV_HEADS + hk].astype(BF16)
        qh = _group_heads(q_row, hk)
        s = jnp.concatenate(
            [lax.dot_general(pltpu.roll(qh, c * HEAD_DIM, 1).astype(BF16) if c else qh.astype(BF16), kc,
                             (((1,), (1,)), ((), ())), preferred_element_type=F32)
             for c in range(BLOCKS_PER_PAGE)], axis=1)
        ex = jnp.exp(s - jnp.max(s, axis=-1, keepdims=True))
        p = ex / jnp.sum(ex, axis=-1, keepdims=True)
        o = jnp.dot(p[:, 0:n_pages].astype(BF16), vc, preferred_element_type=F32)
        for c in range(1, BLOCKS_PER_PAGE):
            oc = jnp.dot(p[:, c * n_pages:(c + 1) * n_pages].astype(BF16), vc, preferred_element_type=F32)
            o = o + pltpu.roll(oc, LANES - c * HEAD_DIM, 1)
        o_groups.append(o)
        hrow = lax.broadcasted_iota(jnp.int32, p.shape, 0) < hpg
        imp = jnp.sum(jnp.where(hrow, p, 0.0), axis=0, keepdims=True)
        score = jnp.where((bid_r == 0) | (bid_r == nc - 1), FORCED_SCORE, imp)
        score_col = jnp.concatenate([score, jnp.zeros((LANES - 1, nc), F32)], 0).T[:, 0:1]
        beats = (score_col > score) | ((score_col == score) & (bid_c < bid_r))
        rank = jnp.sum(beats.astype(F32), axis=0, keepdims=True)
        row = jnp.zeros((1, LANES), F32)
        bid_f = bid_r.astype(F32)
        for k in range(SEL_PAST):
            blk = jnp.sum(jnp.where(rank == k, bid_f, 0.0), axis=1, keepdims=True)
            row = jnp.where(lane == k, blk, row)
        idx_ref[pl.ds(b * NSA_KV_HEADS + hk, 1), :] = row.astype(jnp.int32)
    ocmp_ref[pl.ds(b, 1), :] = _spread_heads(o_groups)


def _nsa_sample_cmp_t(qc, kvcmp_t):
    bs, _, n_pages, _ = kvcmp_t.shape
    return pl.pallas_call(
        functools.partial(_nsa_sample_cmp_t_kernel, n_pages=n_pages),
        grid=(bs,),
        in_specs=[pl.BlockSpec((bs, NSA_D), lambda b: (0, 0)),
                  pl.BlockSpec((1, 2 * NSA_KV_HEADS, n_pages, LANES), lambda b: (b, 0, 0, 0))],
        out_specs=[pl.BlockSpec((bs, NSA_D), lambda b: (0, 0)),
                   pl.BlockSpec((bs * NSA_KV_HEADS, LANES), lambda b: (0, 0))],
        out_shape=[jax.ShapeDtypeStruct((bs, NSA_D), F32),
                   jax.ShapeDtypeStruct((bs * NSA_KV_HEADS, LANES), jnp.int32)],
        compiler_params=_cparams(("arbitrary",)),
        name="nsa_sample_cmp",
    )(qc, kvcmp_t)


def _sel_block_copies(pool_ref, pt_ref, sel_ref, kbuf, vbuf, sem, b, hk, k):
    blk = sel_ref[b * NSA_KV_HEADS + hk, k]
    page = pt_ref[b, lax.shift_right_logical(blk, int(math.log2(BLOCKS_PER_PAGE)))]
    j = hk * SEL_PAST + k
    return (pltpu.make_async_copy(pool_ref.at[page, pl.ds(hk * HEAD_DIM, HEAD_DIM)], kbuf.at[j], sem),
            pltpu.make_async_copy(pool_ref.at[page, pl.ds(KV_D + hk * HEAD_DIM, HEAD_DIM)], vbuf.at[j], sem))


def _nsa_sample_attn_t_kernel(pt_ref, sel_ref, qr_ref, new_sel_ref, new_win_ref, win_ref, dtg_ref, ocmp_ref,
                              pool_ref, o_ref, kbuf, vbuf, sem):
    b = pl.program_id(0)
    for hk in range(NSA_KV_HEADS):
        for k in range(SEL_PAST):
            for cp in _sel_block_copies(pool_ref, pt_ref, sel_ref, kbuf, vbuf, sem, b, hk, k):
                cp.start()
    for hk in range(NSA_KV_HEADS):
        for k in range(SEL_PAST):
            for cp in _sel_block_copies(pool_ref, pt_ref, sel_ref, kbuf, vbuf, sem, b, hk, k):
                cp.wait()
    scale = HEAD_DIM ** -0.5
    q_row = qr_ref[pl.ds(b, 1), :] * scale
    sig = _sigmoid(dtg_ref[pl.ds(b, 1), :])
    lane = lax.broadcasted_iota(jnp.int32, (1, PAGE_SIZE), 1)
    o_slc, o_win = [], []
    for hk in range(NSA_KV_HEADS):
        qh = _group_heads(q_row, hk)[:, 0:HEAD_DIM].astype(BF16)

        def new_row(ref, kind):
            t = ref[pl.ds(b, 1), :][:, kind * KV_D:(kind + 1) * KV_D]
            if hk == 1:
                t = pltpu.roll(t, HEAD_DIM, 1)
            return t[:, 0:HEAD_DIM].astype(BF16).astype(F32)

        def attend(kt, vt, mask, new_ref, n_new):
            s = jnp.dot(qh, kt.astype(BF16), preferred_element_type=F32)
            if mask is not None:
                s = jnp.where(mask, s, NEG)
            s_new = jnp.sum(qh.astype(F32) * new_row(new_ref, 0), axis=1, keepdims=True)
            m = jnp.maximum(jnp.max(s, axis=-1, keepdims=True), s_new)
            ex = jnp.exp(s - m)
            ex_new = jnp.exp(s_new - m) * n_new
            den = jnp.sum(ex, axis=-1, keepdims=True) + ex_new
            o = lax.dot_general((ex / den).astype(BF16), vt.astype(BF16), (((1,), (1,)), ((), ())),
                                preferred_element_type=F32)
            return o + (ex_new / den).astype(BF16).astype(F32) * new_row(new_ref, 1)

        kt = jnp.concatenate([kbuf[hk * SEL_PAST + k] for k in range(SEL_PAST)], axis=1)
        vt = jnp.concatenate([vbuf[hk * SEL_PAST + k] for k in range(SEL_PAST)], axis=1)
        mask = jnp.concatenate(
            [lane // SEL_BLOCK == (sel_ref[b * NSA_KV_HEADS + hk, k] & (BLOCKS_PER_PAGE - 1))
             for k in range(SEL_PAST)], axis=1)
        o_slc.append(attend(kt, vt, mask, new_sel_ref, float(SEL_BLOCK)))
        o_win.append(attend(win_ref[0, hk * HEAD_DIM:(hk + 1) * HEAD_DIM, :],
                            win_ref[0, KV_D + hk * HEAD_DIM:KV_D + (hk + 1) * HEAD_DIM, :], None, new_win_ref, 1.0))
    gates = []
    for br in range(3):
        gates.append(jnp.concatenate(
            [jnp.broadcast_to(sig[:, GATE_COL0 + h * 3 + br:GATE_COL0 + h * 3 + br + 1], (1, HEAD_DIM))
             for h in range(NSA_HEADS)], axis=1))
    o_ref[pl.ds(b, 1), :] = (gates[0] * ocmp_ref[pl.ds(b, 1), :] + gates[1] * _spread_heads(o_slc)
                             + gates[2] * _spread_heads(o_win))


def _nsa_sample_attn_t(qr, new_sel, new_win, win_t, dtg, o_cmp, pool_sel_t, page_table, sel_idx):
    bs = qr.shape[0]
    const = lambda a: pl.BlockSpec(a.shape, lambda b, pt, sel: (0,) * a.ndim)
    n_buf = NSA_KV_HEADS * SEL_PAST
    return pl.pallas_call(
        _nsa_sample_attn_t_kernel,
        grid_spec=pltpu.PrefetchScalarGridSpec(
            num_scalar_prefetch=2,
            grid=(bs,),
            in_specs=[const(qr), const(new_sel), const(new_win),
                      pl.BlockSpec((1,) + win_t.shape[1:], lambda b, pt, sel: (b, 0, 0)),
                      const(dtg), const(o_cmp), pl.BlockSpec(memory_space=pl.ANY)],
            out_specs=pl.BlockSpec((bs, NSA_D), lambda b, pt, sel: (0, 0)),
            scratch_shapes=[pltpu.VMEM((n_buf, HEAD_DIM, PAGE_SIZE), F32), pltpu.VMEM((n_buf, HEAD_DIM, PAGE_SIZE), F32),
                            pltpu.SemaphoreType.DMA]),
        out_shape=jax.ShapeDtypeStruct((bs, NSA_D), F32),
        compiler_params=_cparams(("arbitrary",)),
        name="nsa_sample_attn",
    )(page_table, sel_idx, qr, new_sel, new_win, win_t, dtg, o_cmp, pool_sel_t)


def kernel(x_prompt, x_sample, cache_kv_cmp, cache_kv_sel, page_table, cache_kv_win, state_ssm, state_conv,
           emb_ln_g, emb_ln_b, w_in, conv_w, conv_b, dt_bias, a_log, d_skip, ssd_norm_w,
           cmp_pe, cmp_w1, cmp_b1, cmp_w2, cmp_b2, w_out, ln1_g, ln1_b,
           router_w, router_bias, exp_w_gate, exp_w_up, exp_w_down,
           sh_w_gate, sh_w_up, sh_w_down, ln2_g, ln2_b):
    bp, tp, _ = x_prompt.shape
    bs, ts, _ = x_sample.shape
    assert ts == 1 and DEPTH == 1
    n_prompt = bp * tp
    past_len = page_table.shape[1] * PAGE_SIZE
    l = 0
    w_perm = _permute_w_in(w_in[l])
    ln0_g, ln0_b = emb_ln_g[None], emb_ln_b[None]
    ssd_consts = (conv_w[l], conv_b[l][None], _pad_lanes(dt_bias[l]), _pad_lanes(a_log[l]),
                  jnp.repeat(d_skip[l], HEAD_DIM)[None], ssd_norm_w[l][None])
    cmp_consts = _compress_consts(cmp_pe[l], cmp_w1[l], cmp_b1[l], cmp_w2[l], cmp_b2[l])
    w_o = w_out[l].astype(BF16)
    w_o_ssd, w_o_nsa = w_o[:SSD_D], w_o[SSD_D:]
    ln1 = (ln1_g[l][None], ln1_b[l][None])
    kv_shape = (2, NSA_KV_HEADS, HEAD_DIM)

    hp, z, xbc, qc, qr, kvc, kvs, kvw, dtg, kvc_t, kvs_t, kvw_t = _inproj(
        x_prompt.reshape(n_prompt, D_MODEL), ln0_g, ln0_b, w_perm, _rope_tables(jnp.arange(tp)), 256,
        _rope_tables_t(jnp.arange(tp)))
    y_ssd, ssm_p, conv_p = _ssd_prompt(xbc, z, dtg, *ssd_consts, bp, tp)
    kvcmp = _compress_prompt(kvc, cmp_consts, tp)
    y_nsa = _nsa_prompt(qc, qr, dtg, kvcmp, kvs, kvw, bp, tp)
    h1p = _outproj(y_ssd, y_nsa, hp, w_o_ssd, w_o_nsa, *ln1, 256)
    n_keep = min(WINDOW, tp)
    cache_leaf = lambda a: jnp.transpose(a.reshape((bp,) + kv_shape + (a.shape[-1],)), (0, 4, 1, 2, 3))[None]
    kvc_p = cache_leaf(kvc_t)
    kvs_p = cache_leaf(kvs_t)
    kvw_p = cache_leaf(kvw_t[:, :, tp - n_keep:])

    s_hs, s_z, s_xbc, s_qc, s_qr, s_kvc, s_kvs, s_kvw, s_dtg = _inproj(
        x_sample.reshape(bs, D_MODEL), ln0_g, ln0_b, w_perm, _rope_tables(jnp.full((bs,), past_len)), bs)
    s_y_ssd, ssm_s, conv_s_t = _ssd_sample(s_xbc, s_z, s_dtg, jnp.swapaxes(state_conv[l], 0, 1), state_ssm[l],
                                           *ssd_consts)
    n_pool = cache_kv_cmp.shape[1]
    feature_major = lambda c, rows: jnp.swapaxes(c.reshape(-1, rows, 2 * KV_D), 1, 2)
    s_kvcmp = _compress_pages(feature_major(cache_kv_cmp[l], PAGE_SIZE), page_table,
                              _compress_consts_t(cmp_pe[l], cmp_w1[l], cmp_b1[l], cmp_w2[l], cmp_b2[l]))
    s_o_cmp, s_sel = _nsa_sample_cmp_t(s_qc, s_kvcmp)
    buf_win = cache_kv_win[l].reshape(bs, -1, 2 * KV_D)
    s_y_nsa = _nsa_sample_attn_t(
        s_qr, s_kvs, s_kvw, feature_major(cache_kv_win[l], buf_win.shape[1]), s_dtg, s_o_cmp,
        feature_major(cache_kv_sel[l], PAGE_SIZE), page_table, s_sel)
    h1s = _outproj(s_y_ssd, s_y_nsa, s_hs, w_o_ssd, w_o_nsa, *ln1, bs)
    win_all = jnp.concatenate([buf_win, s_kvw[:, None, :]], 1)
    n_keep_s = min(WINDOW, past_len + ts)
    kvw_s = win_all[:, win_all.shape[1] - n_keep_s:].reshape((1, bs, n_keep_s) + kv_shape)
    kvc_s = s_kvc.reshape((1, bs, ts) + kv_shape)
    kvs_s = s_kvs.reshape((1, bs, ts) + kv_shape)

    assert n_prompt % MOE_TOKENS == 0 and bs * ts <= MOE_TOKENS
    n_tok = n_prompt + bs * ts
    tail = jnp.concatenate([h1s, jnp.zeros((MOE_TOKENS - bs * ts, D_MODEL), F32)], 0)
    out_main, out_tail = _moe_ln(h1p, tail, n_tok, router_w[l], router_bias[l], exp_w_gate[l], exp_w_up[l],
                                 exp_w_down[l], sh_w_gate[l], sh_w_up[l], sh_w_down[l], ln2_g[l][None], ln2_b[l][None])
    y_prompt = out_main.reshape(bp, tp, D_MODEL)
    y_sample = out_tail[:bs * ts].reshape(bs, ts, D_MODEL)
    return (y_prompt, y_sample, kvc_p, kvs_p, kvw_p, ssm_p[None], conv_p[None],
            kvc_s, kvs_s, kvw_s, ssm_s[None], jnp.swapaxes(conv_s_t, 0, 1)[None])
```

```python
import functools
import math

import jax
import jax.numpy as jnp
import numpy as np
from jax import lax
from jax.experimental import pallas as pl
from jax.experimental.pallas import tpu as pltpu

D_MODEL = 1024
HEAD_DIM = 64
SSD_HEADS = 8
SSD_D = SSD_HEADS * HEAD_DIM
SSD_GROUPS = 2
SSD_STATE = 128
SSD_CONV = 4
SSD_CONV_CH = SSD_D + 2 * SSD_GROUPS * SSD_STATE
SSD_CHUNK = 128
NSA_HEADS = 8
NSA_KV_HEADS = 2
NSA_D = NSA_HEADS * HEAD_DIM
KV_D = NSA_KV_HEADS * HEAD_DIM
CMP_BLOCK = 64
CMP_HIDDEN = 128
SEL_BLOCK = 64
TOP_N = 16
WINDOW = 512
Q_BLOCK = 128
ROT_DIM = HEAD_DIM // 4
ROPE_THETA = 500000.0
N_EXPERTS = 64
TOP_K = 6
N_EXPERT_GROUPS = 8
EXPERTS_PER_GROUP = N_EXPERTS // N_EXPERT_GROUPS
TOPK_GROUPS = 4
D_EXPERT = 256
D_SHARED = 256
ROUTED_SCALE = 2.5
MOE_BLOCK = 512
DEPTH = 1
DEEPNORM_ALPHA = (2.0 * DEPTH) ** 0.25
LN_EPS = 1e-5
RMS_EPS = 1e-5
NEG = -1e30
FORCED_SCORE = 1e4
PAGE_SIZE = 128

LANES = 128
SUBLANES = 8
VMEM_LIMIT_BYTES = 56 * 1024 * 1024

U_Z = 0
U_XBC = U_Z + SSD_D
U_Q = U_XBC + SSD_CONV_CH
U_KVC = U_Q + NSA_D
U_KVS = U_KVC + 2 * KV_D
U_KVW = U_KVS + 2 * KV_D
U_DTG = U_KVW + 2 * KV_D
U_TOTAL = U_DTG + LANES
GATE_COL0 = SSD_HEADS

BF16 = jnp.bfloat16
F32 = jnp.float32


def _cparams(sem):
    return pltpu.CompilerParams(dimension_semantics=sem, vmem_limit_bytes=VMEM_LIMIT_BYTES)


def _bdot(a, b):
    return jnp.dot(a.astype(BF16), b.astype(BF16), preferred_element_type=F32)


def _bdot_nt(a, b):
    return lax.dot_general(a.astype(BF16), b.astype(BF16), (((1,), (1,)), ((), ())),
                           preferred_element_type=F32)


def _hdot(a, b):
    return jnp.dot(a, b, preferred_element_type=F32, precision=lax.Precision.HIGHEST)


def _sigmoid(x):
    return 1.0 / (1.0 + jnp.exp(-x))


def _silu(x):
    return x * _sigmoid(x)


def _layer_norm(x, g, b):
    mu = jnp.mean(x, axis=-1, keepdims=True)
    xc = x - mu
    var = jnp.mean(xc * xc, axis=-1, keepdims=True)
    return xc * lax.rsqrt(var + LN_EPS) * g + b


def _rope_tile(x, cos, sa, sb):
    return x * cos + pltpu.roll(x, LANES - ROT_DIM // 2, 1) * sa + pltpu.roll(x, ROT_DIM // 2, 1) * sb


def _rope_rows(x, cos, sin):
    half = ROT_DIM // 2
    parts = []
    for hd in range(NSA_KV_HEADS):
        r0 = hd * HEAD_DIM
        x1, x2 = x[r0:r0 + half], x[r0 + half:r0 + ROT_DIM]
        parts += [x1 * cos - x2 * sin, x2 * cos + x1 * sin, x[r0 + ROT_DIM:r0 + HEAD_DIM]]
    return jnp.concatenate(parts, axis=0)


def _inproj_kernel(x_ref, g_ref, b_ref, w_ref, rope_ref, *refs, feature_major):
    if feature_major:
        wkv_t_ref, rope_t_ref = refs[:2]
        refs = refs[2:]
    h_ref, z_ref, xbc_ref, qc_ref, qr_ref, kvc_ref, kvs_ref, kvw_ref, dtg_ref = refs[:9]
    h = _layer_norm(x_ref[...], g_ref[...], b_ref[...])
    h_ref[...] = h
    hb = h.astype(BF16)
    if feature_major:
        kvc_t_ref, kvs_t_ref, kvw_t_ref = refs[9:]
        ut = lax.dot_general(wkv_t_ref[...], hb, (((1,), (1,)), ((), ())), preferred_element_type=F32)
        half = ROT_DIM // 2
        cos_t, sin_t = rope_t_ref[0:half, :], rope_t_ref[half:2 * half, :]
        kvc_t_ref[0] = ut[0:2 * KV_D]
        kvs_t_ref[0, 0:KV_D] = _rope_rows(ut[2 * KV_D:3 * KV_D], cos_t, sin_t)
        kvs_t_ref[0, KV_D:2 * KV_D] = ut[3 * KV_D:4 * KV_D]
        kvw_t_ref[0, 0:KV_D] = _rope_rows(ut[4 * KV_D:5 * KV_D], cos_t, sin_t)
        kvw_t_ref[0, KV_D:2 * KV_D] = ut[5 * KV_D:6 * KV_D]
    u = jnp.dot(hb, w_ref[...], preferred_element_type=F32)
    cos = rope_ref[:, 0:LANES]
    sa = rope_ref[:, LANES:2 * LANES]
    sb = rope_ref[:, 2 * LANES:3 * LANES]
    z_ref[...] = u[:, U_Z:U_XBC]
    xbc_ref[...] = u[:, U_XBC:U_Q]
    qc_ref[...] = u[:, U_Q:U_KVC].astype(qc_ref.dtype)
    for c in range(NSA_D // LANES):
        qr_ref[:, c * LANES:(c + 1) * LANES] = _rope_tile(
            u[:, U_Q + c * LANES:U_Q + (c + 1) * LANES], cos, sa, sb).astype(qr_ref.dtype)
    kvc_ref[...] = u[:, U_KVC:U_KVS]
    kvs_ref[:, 0:KV_D] = _rope_tile(u[:, U_KVS:U_KVS + KV_D], cos, sa, sb)
    kvs_ref[:, KV_D:2 * KV_D] = u[:, U_KVS + KV_D:U_KVW]
    kvw_ref[:, 0:KV_D] = _rope_tile(u[:, U_KVW:U_KVW + KV_D], cos, sa, sb)
    kvw_ref[:, KV_D:2 * KV_D] = u[:, U_KVW + KV_D:U_DTG]
    dtg_ref[...] = u[:, U_DTG:U_TOTAL]


def _rope_tables(pos):
    half = ROT_DIM // 2
    inv = ROPE_THETA ** (-jnp.arange(half, dtype=F32) / half)
    ang = pos.astype(F32)[:, None] * inv
    cos, sin = jnp.cos(ang), jnp.sin(ang)
    ones = jnp.ones((pos.shape[0], HEAD_DIM - ROT_DIM), F32)
    zeros = jnp.zeros((pos.shape[0], HEAD_DIM - ROT_DIM), F32)
    zh = jnp.zeros_like(sin)
    c = jnp.concatenate([cos, cos, ones], 1)
    sa = jnp.concatenate([-sin, zh, zeros], 1)
    sb = jnp.concatenate([zh, sin, zeros], 1)
    return jnp.concatenate([jnp.tile(t, (1, LANES // HEAD_DIM)) for t in (c, sa, sb)], 1)


def _permute_w_in(w):
    sizes = (SSD_D, SSD_CONV_CH, SSD_HEADS, NSA_D, KV_D, KV_D, KV_D, KV_D, KV_D, KV_D, 3 * NSA_HEADS)
    offs = np.concatenate([[0], np.cumsum(sizes)])
    seg = [w[:, offs[i]:offs[i + 1]] for i in range(len(sizes))]
    pad = jnp.zeros((w.shape[0], LANES - SSD_HEADS - 3 * NSA_HEADS), w.dtype)
    out = jnp.concatenate([seg[0], seg[1], seg[3], seg[4], seg[5], seg[6], seg[7], seg[8], seg[9],
                           seg[2], seg[10], pad], 1)
    return out.astype(BF16)


def _rope_tables_t(pos):
    half = ROT_DIM // 2
    inv = ROPE_THETA ** (-jnp.arange(half, dtype=F32) / half)
    ang = inv[:, None] * pos.astype(F32)[None, :]
    return jnp.concatenate([jnp.cos(ang), jnp.sin(ang)], 0)


def _inproj(x, ln_g, ln_b, w_perm, rope_tab, tm, rope_tab_t=None):
    n = x.shape[0]
    nt = n // tm
    t = rope_tab.shape[0]
    n_rope_blocks = t // tm
    feature_major = rope_tab_t is not None
    row = lambda w: pl.BlockSpec((tm, w), lambda i: (i, 0))
    const = lambda a: pl.BlockSpec(a.shape, lambda i: (0,) * a.ndim)
    widths = (D_MODEL, SSD_D, SSD_CONV_CH, NSA_D, NSA_D, 2 * KV_D, 2 * KV_D, 2 * KV_D, LANES)
    in_specs = [row(D_MODEL), const(ln_g), const(ln_b), const(w_perm),
                pl.BlockSpec((tm, 3 * LANES), lambda i: (i % n_rope_blocks, 0))]
    out_specs = [row(w) for w in widths]
    dtypes = [BF16 if (feature_major and k in (3, 4)) else F32 for k in range(len(widths))]
    out_shape = [jax.ShapeDtypeStruct((n, w), d) for w, d in zip(widths, dtypes)]
    args = [x, ln_g, ln_b, w_perm, rope_tab]
    if feature_major:
        wkv_t = w_perm[:, U_KVC:U_DTG].T
        in_specs += [const(wkv_t), pl.BlockSpec((rope_tab_t.shape[0], tm), lambda i: (0, i % n_rope_blocks))]
        args += [wkv_t, rope_tab_t]
        out_specs += [pl.BlockSpec((1, 2 * KV_D, tm), lambda i: (i // n_rope_blocks, 0, i % n_rope_blocks))] * 3
        out_shape += [jax.ShapeDtypeStruct((n // t, 2 * KV_D, t), F32)] * 3
    return pl.pallas_call(
        functools.partial(_inproj_kernel, feature_major=feature_major),
        grid=(nt,),
        in_specs=in_specs,
        out_specs=out_specs,
        out_shape=out_shape,
        compiler_params=_cparams(("parallel",)),
        name="inproj",
    )(*args)


def _softplus(x):
    return jnp.maximum(x, 0.0) + jnp.log1p(jnp.exp(-jnp.abs(x)))


def _gated_group_norm(y, z, norm_w):
    y = y * _silu(z)
    gw = SSD_D // SSD_GROUPS
    parts = []
    for g in range(SSD_GROUPS):
        yg = y[:, g * gw:(g + 1) * gw]
        ms = jnp.mean(yg * yg, axis=-1, keepdims=True)
        parts.append(yg * lax.rsqrt(ms + RMS_EPS))
    return jnp.concatenate(parts, axis=1) * norm_w


def _ssd_prompt_kernel(xbc_ref, z_ref, dtg_ref, convw_ref, convb_ref, dtb_ref, alog_ref, dskip_ref, normw_ref,
                       y_ref, state_ref, conv_ref, ext_ref, s_ref):
    c = pl.program_id(1)
    nc = pl.num_programs(1)
    L = SSD_CHUNK
    halo = SUBLANES

    @pl.when(c == 0)
    def _():
        ext_ref[0:halo, :] = jnp.zeros((halo, SSD_CONV_CH), F32)
        s_ref[...] = jnp.zeros_like(s_ref)

    xin = xbc_ref[...]
    ext_ref[halo:halo + L, :] = xin
    xc = convw_ref[SSD_CONV - 1:SSD_CONV, :] * xin
    for k in range(SSD_CONV - 1):
        off = halo - (SSD_CONV - 1) + k
        xc = xc + convw_ref[k:k + 1, :] * ext_ref[off:off + L, :]
    ext_ref[0:halo, :] = ext_ref[L:L + halo, :]
    xc = _silu(xc + convb_ref[...])
    xs = xc[:, 0:SSD_D]
    ns = SSD_GROUPS * SSD_STATE
    bm = xc[:, SSD_D:SSD_D + ns]
    cm = xc[:, SSD_D + ns:SSD_D + 2 * ns]

    dt = _softplus(dtg_ref[...] + dtb_ref[...])
    da = dt * (-jnp.exp(alog_ref[...]))
    row = lax.broadcasted_iota(jnp.int32, (L, L), 0)
    col = lax.broadcasted_iota(jnp.int32, (L, L), 1)
    tril = row >= col
    acum = _hdot(tril.astype(F32), da)
    acum_t = acum.T
    eacum = jnp.exp(acum)
    alast = acum[L - 1:L, :]
    edecay = jnp.exp(alast - acum)
    elast = jnp.exp(alast)

    dt_full = jnp.concatenate([jnp.broadcast_to(dt[:, h:h + 1], (L, HEAD_DIM)) for h in range(SSD_HEADS)], 1)
    dec_full = jnp.concatenate([jnp.broadcast_to(edecay[:, h:h + 1], (L, HEAD_DIM)) for h in range(SSD_HEADS)], 1)
    xdt = xs * dt_full
    xdec_t = (xdt * dec_full).T

    hpg = SSD_HEADS // SSD_GROUPS
    y_parts = []
    for h in range(SSD_HEADS):
        g = h // hpg
        b_g = bm[:, g * SSD_STATE:(g + 1) * SSD_STATE]
        c_g = cm[:, g * SSD_STATE:(g + 1) * SSD_STATE]
        if h % hpg == 0:
            cb = _bdot_nt(c_g, b_g)
        seg = acum[:, h:h + 1] - acum_t[h:h + 1, :]
        lmat = jnp.where(tril, jnp.exp(jnp.where(tril, seg, 0.0)), 0.0)
        xdt_h = xdt[:, h * HEAD_DIM:(h + 1) * HEAD_DIM]
        y_h = _bdot(cb * lmat, xdt_h)
        s_prev = s_ref[h]
        y_h = y_h + _bdot_nt(c_g, s_prev) * eacum[:, h:h + 1]
        y_h = y_h + dskip_ref[:, h * HEAD_DIM:(h + 1) * HEAD_DIM] * xs[:, h * HEAD_DIM:(h + 1) * HEAD_DIM]
        y_parts.append(y_h)
        s_ref[h] = elast[:, h:h + 1] * s_prev + _bdot(xdec_t[h * HEAD_DIM:(h + 1) * HEAD_DIM, :], b_g)
    y = jnp.concatenate(y_parts, axis=1)
    y_ref[...] = _gated_group_norm(y, z_ref[...], normw_ref[...]).astype(y_ref.dtype)

    @pl.when(c == nc - 1)
    def _():
        state_ref[0] = s_ref[...]
        conv_ref[0] = xin[L - (SSD_CONV - 1):L, :]


def _ssd_prompt(xbc, z, dtg, conv_w, conv_b, dt_bias_pad, a_log_pad, d_skip_full, norm_w, bn, t):
    nc = t // SSD_CHUNK
    row = lambda w: pl.BlockSpec((SSD_CHUNK, w), lambda b, c: (b * nc + c, 0))
    const = lambda a: pl.BlockSpec(a.shape, lambda b, c: (0,) * a.ndim)
    return pl.pallas_call(
        _ssd_prompt_kernel,
        grid=(bn, nc),
        in_specs=[row(SSD_CONV_CH), row(SSD_D), row(LANES), const(conv_w), const(conv_b), const(dt_bias_pad),
                  const(a_log_pad), const(d_skip_full), const(norm_w)],
        out_specs=[row(SSD_D),
                   pl.BlockSpec((1, SSD_HEADS, HEAD_DIM, SSD_STATE), lambda b, c: (b, 0, 0, 0)),
                   pl.BlockSpec((1, SSD_CONV - 1, SSD_CONV_CH), lambda b, c: (b, 0, 0))],
        out_shape=[jax.ShapeDtypeStruct((bn * t, SSD_D), BF16),
                   jax.ShapeDtypeStruct((bn, SSD_HEADS, HEAD_DIM, SSD_STATE), F32),
                   jax.ShapeDtypeStruct((bn, SSD_CONV - 1, SSD_CONV_CH), F32)],
        scratch_shapes=[pltpu.VMEM((SSD_CHUNK + 2 * SUBLANES, SSD_CONV_CH), F32),
                        pltpu.VMEM((SSD_HEADS, HEAD_DIM, SSD_STATE), F32)],
        compiler_params=_cparams(("parallel", "arbitrary")),
        name="ssd_prompt",
    )(xbc, z, dtg, conv_w, conv_b, dt_bias_pad, a_log_pad, d_skip_full, norm_w)


def _pad_lanes(v, fill=0.0):
    return jnp.concatenate([v.astype(F32), jnp.full((LANES - v.shape[0],), fill, F32)])[None]


def _compress_rows(k_ref, v_ref, pe_ref, w1k_ref, w1v_ref, b1_ref, w2k_ref, w2v_ref, b2_ref, nb):
    acck = jnp.zeros((nb, 2 * CMP_HIDDEN), F32)
    accv = jnp.zeros((nb, 2 * CMP_HIDDEN), F32)
    for l in range(CMP_BLOCK):
        xk = k_ref[pl.ds(l, nb, stride=CMP_BLOCK), :] + pe_ref[l:l + 1, 0:KV_D]
        xv = v_ref[pl.ds(l, nb, stride=CMP_BLOCK), :] + pe_ref[l:l + 1, KV_D:2 * KV_D]
        acck = acck + jnp.dot(xk.astype(BF16), w1k_ref[l], preferred_element_type=F32)
        accv = accv + jnp.dot(xv.astype(BF16), w1v_ref[l], preferred_element_type=F32)
    hk = _silu(acck + b1_ref[:, 0:2 * CMP_HIDDEN])
    hv = _silu(accv + b1_ref[:, 2 * CMP_HIDDEN:4 * CMP_HIDDEN])
    ok = jnp.dot(hk.astype(BF16), w2k_ref[...], preferred_element_type=F32) + b2_ref[:, 0:KV_D]
    ov = jnp.dot(hv.astype(BF16), w2v_ref[...], preferred_element_type=F32) + b2_ref[:, KV_D:2 * KV_D]
    return jnp.concatenate([ok, ov], axis=1)


def _compress_kernel(k_ref, v_ref, pe_ref, w1k_ref, w1v_ref, b1_ref, w2k_ref, w2v_ref, b2_ref, o_ref, *, nb):
    o_ref[...] = _compress_rows(k_ref, v_ref, pe_ref, w1k_ref, w1v_ref, b1_ref, w2k_ref, w2v_ref, b2_ref, nb)


def _block_diag2(w):
    z = jnp.zeros_like(w)
    return jnp.concatenate([jnp.concatenate([w, z], -1), jnp.concatenate([z, w], -1)], -2)


def _compress_consts(cmp_pe, cmp_w1, cmp_b1, cmp_w2, cmp_b2):
    pe = jnp.concatenate([cmp_pe[0], cmp_pe[0], cmp_pe[1], cmp_pe[1]], -1)
    w1k = _block_diag2(cmp_w1[0]).astype(BF16)
    w1v = _block_diag2(cmp_w1[1]).astype(BF16)
    b1 = jnp.concatenate([cmp_b1[0], cmp_b1[0], cmp_b1[1], cmp_b1[1]])[None]
    w2k = _block_diag2(cmp_w2[0]).astype(BF16)
    w2v = _block_diag2(cmp_w2[1]).astype(BF16)
    b2 = jnp.concatenate([cmp_b2[0], cmp_b2[0], cmp_b2[1], cmp_b2[1]])[None]
    return pe, w1k, w1v, b1, w2k, w2v, b2


def _compress_prompt(kvc, consts, rows_per_step):
    n = kvc.shape[0]
    nb = rows_per_step // CMP_BLOCK
    const = lambda a: pl.BlockSpec(a.shape, lambda i: (0,) * a.ndim)
    return pl.pallas_call(
        functools.partial(_compress_kernel, nb=nb),
        grid=(n // rows_per_step,),
        in_specs=[pl.BlockSpec((rows_per_step, KV_D), lambda i: (i, 0)),
                  pl.BlockSpec((rows_per_step, KV_D), lambda i: (i, 1))] + [const(a) for a in consts],
        out_specs=pl.BlockSpec((nb, 2 * KV_D), lambda i: (i, 0)),
        out_shape=jax.ShapeDtypeStruct((n // CMP_BLOCK, 2 * KV_D), F32),
        compiler_params=_cparams(("parallel",)),
        name="compress_prompt",
    )(kvc, kvc, *consts)


SEL_KEY_TILE = 512
WIN_KEYS = WINDOW + Q_BLOCK


def _dup_head(x, hk):
    sw = pltpu.roll(x, HEAD_DIM, 1)
    low = lax.broadcasted_iota(jnp.int32, x.shape, 1) < HEAD_DIM
    return jnp.where(low, x, sw) if hk == 0 else jnp.where(low, sw, x)


def _masked_softmax(s, mask):
    sm = jnp.where(mask, s, NEG)
    ex = jnp.where(mask, jnp.exp(sm - jnp.max(sm, axis=-1, keepdims=True)), 0.0)
    den = jnp.sum(ex, axis=-1, keepdims=True)
    return ex / jnp.where(den > 0.0, den, 1.0)


def _select_blocks_t(imp, cur, n_top):
    j = lax.broadcasted_iota(jnp.int32, imp.shape, 0)
    future = j > cur
    forced = (j == 0) | (j == cur) | (j == cur - 1)
    score = jnp.where(future, NEG, jnp.where(forced, FORCED_SCORE, imp))
    return ((_rank_rows(score) < n_top) & (score > 0.5 * NEG)).astype(F32)


def _nsa_prompt_kernel(qc_ref, qr_ref, dtg_ref, cmp_ref, kvs_ref, kvw_ref, o_ref,
                       cmp_d, kvs_d, kvw_d, bias_ref, qrs_ref, m_ref, l_ref, acc_ref, *, t):
    qb = pl.program_id(1)
    nbk = t // SEL_BLOCK
    tq = Q_BLOCK
    tk = SEL_KEY_TILE
    hpg = NSA_HEADS // NSA_KV_HEADS
    scale = HEAD_DIM ** -0.5

    @pl.when(qb == 0)
    def _():
        cmp_d[...] = jnp.zeros_like(cmp_d)
        for src, dst, n in ((cmp_ref, cmp_d, nbk), (kvs_ref, kvs_d, t), (kvw_ref, kvw_d, t)):
            x = src[...]
            for hk in range(NSA_KV_HEADS):
                dst[hk, 0:n, 0:KV_D] = _dup_head(x[:, 0:KV_D], hk).astype(BF16)
                dst[hk, 0:n, KV_D:2 * KV_D] = _dup_head(x[:, KV_D:2 * KV_D], hk).astype(BF16)

    t0 = qb * tq
    rows = t0 + lax.broadcasted_iota(jnp.int32, (tq, 1), 0)
    lane = lax.broadcasted_iota(jnp.int32, (tq, LANES), 1)
    half_mask = (lane < HEAD_DIM, lane >= HEAD_DIM)
    sig = _sigmoid(dtg_ref[...])
    vis = (lane + 1) * CMP_BLOCK - 1 <= rows
    cur_l = (t0 + lax.broadcasted_iota(jnp.int32, (1, tq), 1)) // SEL_BLOCK
    expand = (lax.broadcasted_iota(jnp.int32, (LANES, t), 1) // SEL_BLOCK
              == lax.broadcasted_iota(jnp.int32, (LANES, t), 0)).astype(BF16)
    win_start = pl.multiple_of(jnp.maximum(t0 - WINDOW, 0), tq)
    wpos = win_start + lax.broadcasted_iota(jnp.int32, (tq, WIN_KEYS), 1)
    win_bias = jnp.where((wpos <= rows) & (wpos >= rows - WINDOW), 0.0, NEG)
    n_kt = (t0 + tq + tk - 1) // tk

    def stack_heads(ref, hk):
        parts = []
        for hh in range(hpg):
            head = hk * hpg + hh
            p, e = head // 2, head % 2
            parts.append(jnp.where(half_mask[e], ref[:, p * LANES:(p + 1) * LANES] * scale, 0.0))
        return jnp.concatenate(parts, axis=0).astype(BF16)

    o_cmp_g = []
    for hk in range(NSA_KV_HEADS):
        qcs = stack_heads(qc_ref, hk)
        s = lax.dot_general(qcs, cmp_d[hk, :, 0:KV_D], (((1,), (1,)), ((), ())), preferred_element_type=F32)
        pc = _masked_softmax(s.reshape(hpg, tq, LANES), vis[None])
        imp = jnp.sum(pc, axis=0)
        o_cmp_g.append(jnp.dot(pc.reshape(hpg * tq, LANES).astype(BF16), cmp_d[hk, :, KV_D:2 * KV_D],
                               preferred_element_type=F32).reshape(hpg, tq, LANES))

        sel_t = _select_blocks_t(imp.T[0:nbk, :], cur_l, TOP_N)
        sel = jnp.concatenate([sel_t, jnp.zeros((LANES - nbk, tq), F32)], axis=0).T
        selk = jnp.dot(sel.astype(BF16), expand, preferred_element_type=F32)
        for kt in range(t // tk):
            @pl.when(kt < n_kt)
            def _(kt=kt, hk=hk, selk=selk):
                kpos = kt * tk + lax.broadcasted_iota(jnp.int32, (tq, tk), 1)
                bias_ref[hk, kt] = jnp.where((selk[:, kt * tk:(kt + 1) * tk] > 0.5) & (kpos <= rows), 0.0, NEG)

        qrs_ref[hk] = stack_heads(qr_ref, hk)

    m_ref[...] = jnp.full(m_ref.shape, NEG, F32)
    l_ref[...] = jnp.zeros(l_ref.shape, F32)
    acc_ref[...] = jnp.zeros(acc_ref.shape, F32)

    def sel_step(kt, carry):
        k0 = pl.multiple_of(kt * tk, tk)
        for hk in range(NSA_KV_HEADS):
            kblk = kvs_d[hk, pl.ds(k0, tk), 0:KV_D]
            vblk = kvs_d[hk, pl.ds(k0, tk), KV_D:2 * KV_D]
            s = lax.dot_general(qrs_ref[hk], kblk, (((1,), (1,)), ((), ())), preferred_element_type=F32)
            s = s.reshape(hpg, tq, tk) + bias_ref[hk, kt][None]
            m_old = m_ref[hk]
            m_new = jnp.maximum(m_old, jnp.max(s, axis=-1, keepdims=True))
            alpha = jnp.exp(m_old - m_new)
            pe = jnp.exp(s - jnp.concatenate([m_new] * (tk // LANES), axis=-1))
            l_ref[hk] = alpha * l_ref[hk] + jnp.sum(pe, axis=-1, keepdims=True)
            pv = jnp.dot(pe.reshape(hpg * tq, tk).astype(BF16), vblk, preferred_element_type=F32)
            acc_ref[hk] = alpha * acc_ref[hk] + pv.reshape(hpg, tq, LANES)
            m_ref[hk] = m_new
        return carry

    lax.fori_loop(0, n_kt, sel_step, 0)

    for hk in range(NSA_KV_HEADS):
        o_cmp = o_cmp_g[hk]
        o_slc = acc_ref[hk] / l_ref[hk]
        kw = kvw_d[hk, pl.ds(win_start, WIN_KEYS), 0:KV_D]
        vw = kvw_d[hk, pl.ds(win_start, WIN_KEYS), KV_D:2 * KV_D]
        sw = lax.dot_general(qrs_ref[hk], kw, (((1,), (1,)), ((), ())), preferred_element_type=F32)
        sw = sw.reshape(hpg, tq, WIN_KEYS) + win_bias[None]
        pw = jnp.exp(sw - jnp.max(sw, axis=-1, keepdims=True))
        den = jnp.sum(pw, axis=-1, keepdims=True)
        o_win = jnp.dot(pw.reshape(hpg * tq, WIN_KEYS).astype(BF16), vw,
                        preferred_element_type=F32).reshape(hpg, tq, LANES) / den

        for hh in range(hpg):
            head = hk * hpg + hh
            p, e = head // 2, head % 2
            c0 = GATE_COL0 + head * 3
            mix = (sig[:, c0:c0 + 1] * o_cmp[hh] + sig[:, c0 + 1:c0 + 2] * o_slc[hh]
                   + sig[:, c0 + 2:c0 + 3] * o_win[hh])
            if e == 0:
                mix_even = mix
            else:
                o_ref[:, p * LANES:(p + 1) * LANES] = jnp.where(half_mask[0], mix_even, mix).astype(o_ref.dtype)


def _nsa_prompt(qc, qr, dtg, kvcmp, kvs, kvw, bn, t):
    nq = t // Q_BLOCK
    nbk = t // SEL_BLOCK
    hpg = NSA_HEADS // NSA_KV_HEADS
    assert nbk >= TOP_N and t >= WIN_KEYS and t % SEL_KEY_TILE == 0
    qrow = lambda w: pl.BlockSpec((Q_BLOCK, w), lambda b, i: (b * nq + i, 0))
    seq = lambda r: pl.BlockSpec((r, 2 * KV_D), lambda b, i: (b, 0))
    return pl.pallas_call(
        functools.partial(_nsa_prompt_kernel, t=t),
        grid=(bn, nq),
        in_specs=[qrow(NSA_D), qrow(NSA_D), qrow(LANES), seq(nbk), seq(t), seq(t)],
        out_specs=qrow(NSA_D),
        out_shape=jax.ShapeDtypeStruct((bn * t, NSA_D), BF16),
        scratch_shapes=[pltpu.VMEM((NSA_KV_HEADS, LANES, 2 * KV_D), BF16),
                        pltpu.VMEM((NSA_KV_HEADS, t, 2 * KV_D), BF16),
                        pltpu.VMEM((NSA_KV_HEADS, t, 2 * KV_D), BF16),
                        pltpu.VMEM((NSA_KV_HEADS, t // SEL_KEY_TILE, Q_BLOCK, SEL_KEY_TILE), F32),
                        pltpu.VMEM((NSA_KV_HEADS, hpg * Q_BLOCK, LANES), BF16),
                        pltpu.VMEM((NSA_KV_HEADS, hpg, Q_BLOCK, LANES), F32),
                        pltpu.VMEM((NSA_KV_HEADS, hpg, Q_BLOCK, LANES), F32),
                        pltpu.VMEM((NSA_KV_HEADS, hpg, Q_BLOCK, LANES), F32)],
        compiler_params=_cparams(("parallel", "arbitrary")),
        name="nsa_prompt",
    )(qc, qr, dtg, kvcmp, kvs, kvw)


def _outproj_kernel(ys_ref, yn_ref, h_ref, ws_ref, wn_ref, g_ref, b_ref, o_ref):
    mix = jnp.dot(ys_ref[...].astype(BF16), ws_ref[...], preferred_element_type=F32)
    mix = mix + jnp.dot(yn_ref[...].astype(BF16), wn_ref[...], preferred_element_type=F32)
    o_ref[...] = _layer_norm(DEEPNORM_ALPHA * h_ref[...] + mix, g_ref[...], b_ref[...])


def _outproj(y_ssd, y_nsa, h, w_ssd, w_nsa, ln_g, ln_b, tm):
    n = h.shape[0]
    row = lambda w: pl.BlockSpec((tm, w), lambda i: (i, 0))
    const = lambda a: pl.BlockSpec(a.shape, lambda i: (0,) * a.ndim)
    return pl.pallas_call(
        _outproj_kernel,
        grid=(n // tm,),
        in_specs=[row(SSD_D), row(NSA_D), row(D_MODEL), const(w_ssd), const(w_nsa), const(ln_g), const(ln_b)],
        out_specs=row(D_MODEL),
        out_shape=jax.ShapeDtypeStruct((n, D_MODEL), F32),
        compiler_params=_cparams(("parallel",)),
        name="outproj",
    )(y_ssd, y_nsa, h, w_ssd, w_nsa, ln_g, ln_b)


MOE_TOKENS = 256
ROUTE_ROWS = 8


def _token_tile_specs(n_main_tiles):
    main = pl.BlockSpec((MOE_TOKENS, D_MODEL), lambda i, *_: (jnp.minimum(i, n_main_tiles - 1), 0))
    tail = pl.BlockSpec((MOE_TOKENS, D_MODEL), lambda i, *_: (0, 0))
    return main, tail


def _token_tile(i, n_main_tiles, main_ref, tail_ref):
    return jnp.where(i < n_main_tiles, main_ref[...], tail_ref[...])


def _rank_rows(x):
    n = x.shape[0]
    idx = lax.broadcasted_iota(jnp.int32, x.shape, 0)
    rank = jnp.zeros(x.shape, F32)
    for r in range(n):
        row = x[r:r + 1, :]
        rank = rank + ((row > x) | ((row == x) & (idx > r))).astype(F32)
    return rank


def _route_kernel(h_ref, ht_ref, rw_ref, rb_ref, slot_ref, tokinfo_ref, meta_ref, cnt_ref, carry_ref, carry_row_ref, *,
                  n_valid, n_main):
    i = pl.program_id(0)
    tm = MOE_TOKENS

    @pl.when(i == 0)
    def _():
        carry_ref[...] = jnp.zeros_like(carry_ref)
        carry_row_ref[...] = jnp.zeros_like(carry_row_ref)

    logits = lax.dot_general(rw_ref[...], _token_tile(i, n_main, h_ref, ht_ref).astype(BF16), (((1,), (1,)), ((), ())),
                             preferred_element_type=F32)
    scores = _sigmoid(logits)
    biased = scores + rb_ref[:, 0:1]
    b3 = biased.reshape(N_EXPERT_GROUPS, EXPERTS_PER_GROUP, tm)
    sidx = lax.broadcasted_iota(jnp.int32, b3.shape, 1)
    m1 = jnp.max(b3, axis=1, keepdims=True)
    first = jnp.min(jnp.where(b3 == m1, sidx, EXPERTS_PER_GROUP), axis=1, keepdims=True)
    m2 = jnp.max(jnp.where(sidx == first, -jnp.inf, b3), axis=1, keepdims=True)
    grp_score = (m1 + m2).reshape(N_EXPERT_GROUPS, tm)
    grp_keep = _rank_rows(grp_score) < TOPK_GROUPS
    masked = jnp.where(grp_keep.reshape(N_EXPERT_GROUPS, 1, tm), b3, NEG).reshape(N_EXPERTS, tm)
    rank = _rank_rows(masked)
    tok = i * tm + lax.broadcasted_iota(jnp.int32, (1, tm), 1)
    valid = tok < n_valid
    sel = (rank < TOP_K) & valid
    self32 = sel.astype(F32)
    wsel = self32 * scores
    wsum = jnp.sum(wsel, axis=0, keepdims=True)
    w = wsel / jnp.where(wsum > 0.0, wsum, 1.0) * ROUTED_SCALE

    selb = sel.astype(BF16)
    tri = lambda n, strict_upper: (
        (lax.broadcasted_iota(jnp.int32, (n, n), 0) < lax.broadcasted_iota(jnp.int32, (n, n), 1))
        if strict_upper else
        (lax.broadcasted_iota(jnp.int32, (n, n), 0) > lax.broadcasted_iota(jnp.int32, (n, n), 1))).astype(BF16)
    pad8 = lambda c: jnp.floor((c + (SUBLANES - 1.0)) * (1.0 / SUBLANES)) * SUBLANES
    pos_tile = jnp.dot(selb, tri(tm, True), preferred_element_type=F32)
    cnt_col = pad8(jnp.sum(self32, axis=1, keepdims=True))
    first_col = jnp.dot(tri(N_EXPERTS, False), jnp.broadcast_to(cnt_col, (N_EXPERTS, LANES)).astype(BF16),
                        preferred_element_type=F32)[:, 0:1]
    slot = first_col + pos_tile

    sel_pad = jnp.concatenate([selb, jnp.zeros((LANES - N_EXPERTS, tm), BF16)], axis=0)
    cnt_row = pad8(lax.dot_general(jnp.ones((SUBLANES, tm), BF16), sel_pad, (((1,), (1,)), ((), ())),
                                   preferred_element_type=F32))
    first_row = jnp.dot(cnt_row.astype(BF16), tri(LANES, True), preferred_element_type=F32)
    prev_row = carry_row_ref[...]
    meta = jnp.concatenate([cnt_row[0:1], first_row[0:1], prev_row[0:1], jnp.zeros((SUBLANES - 3, LANES), F32)], 0)
    meta_ref[0] = meta.astype(jnp.int32)
    carry_row_ref[...] = prev_row + cnt_row
    carry_ref[...] = carry_ref[...] + cnt_col

    slot_rows, w_rows = [], []
    for k in range(TOP_K):
        hit = (rank == k) & sel
        slot_rows.append(jnp.sum(jnp.where(hit, slot, 0.0), axis=0, keepdims=True))
        w_rows.append(jnp.sum(jnp.where(hit, w, 0.0), axis=0, keepdims=True))
    slot_rows = [jnp.where(valid, r, -1.0) for r in slot_rows]
    pad2 = jnp.zeros((ROUTE_ROWS - TOP_K, tm), F32)
    slot_ref[...] = jnp.concatenate(slot_rows + [pad2 - 1.0], 0).astype(jnp.int32)
    info = jnp.concatenate(w_rows + [pad2] + slot_rows + [jnp.zeros((LANES - ROUTE_ROWS - TOP_K, tm), F32)], 0)
    tokinfo_ref[...] = info.T

    @pl.when(i == pl.num_programs(0) - 1)
    def _():
        cnt_ref[...] = jnp.broadcast_to(carry_ref[:, 0:1], cnt_ref.shape)


def _route(h_main, h_tail, router_wt, router_bias_col, n_valid):
    tm = MOE_TOKENS
    n_main = h_main.shape[0] // tm
    n = h_main.shape[0] + tm
    const = lambda a: pl.BlockSpec(a.shape, lambda i: (0,) * a.ndim)
    return pl.pallas_call(
        functools.partial(_route_kernel, n_valid=n_valid, n_main=n_main),
        grid=(n // tm,),
        in_specs=[*_token_tile_specs(n_main), const(router_wt), const(router_bias_col)],
        out_specs=[pl.BlockSpec((ROUTE_ROWS, tm), lambda i: (0, i)),
                   pl.BlockSpec((tm, LANES), lambda i: (i, 0)),
                   pl.BlockSpec((1, SUBLANES, LANES), lambda i: (i, 0, 0)),
                   pl.BlockSpec((N_EXPERTS, LANES), lambda i: (0, 0))],
        out_shape=[jax.ShapeDtypeStruct((ROUTE_ROWS, n), jnp.int32),
                   jax.ShapeDtypeStruct((n, LANES), F32),
                   jax.ShapeDtypeStruct((n // tm, SUBLANES, LANES), jnp.int32),
                   jax.ShapeDtypeStruct((N_EXPERTS, LANES), F32)],
        scratch_shapes=[pltpu.VMEM((N_EXPERTS, LANES), F32), pltpu.VMEM((SUBLANES, LANES), F32)],
        compiler_params=_cparams(("arbitrary",)),
        name="moe_route",
    )(h_main, h_tail, router_wt, router_bias_col)


PACKED_D = D_MODEL // 2
U32 = jnp.uint32


def _pack_bf16_pairs(x):
    hi = lax.bitcast_convert_type(x[:, 0:PACKED_D], U32) & jnp.uint32(0xFFFF0000)
    lo = lax.shift_right_logical(lax.bitcast_convert_type(x[:, PACKED_D:D_MODEL], U32), jnp.uint32(16))
    return hi | lo


def _unpack_bf16_pairs(w):
    hi = lax.bitcast_convert_type(w & jnp.uint32(0xFFFF0000), F32)
    lo = lax.bitcast_convert_type(lax.shift_left(w, jnp.uint32(16)), F32)
    return jnp.concatenate([hi, lo], axis=1).astype(BF16)


def _round_bf16(x):
    return x.astype(BF16).astype(F32)


TILE_SLOTS = MOE_TOKENS * TOP_K + N_EXPERTS * SUBLANES
RUN_CHUNKS = tuple(1 << b for b in range(int(math.log2(MOE_TOKENS)), int(math.log2(SUBLANES)) - 1, -1))


def _run_copy(src_ref, src_row, dst_ref, dst_row, rows, sem):
    return pltpu.make_async_copy(src_ref.at[pl.ds(pl.multiple_of(src_row, SUBLANES), rows)],
                                 dst_ref.at[pl.ds(pl.multiple_of(dst_row, SUBLANES), rows)], sem)


def _start_run(src_ref, src_row, dst_ref, dst_row, n, sem, started):
    off = jnp.int32(0)
    out = []
    for c, rows in enumerate(RUN_CHUNKS):
        take = (n & rows) != 0

        @pl.when(take)
        def _(off=off, rows=rows):
            _run_copy(src_ref, src_row + off, dst_ref, dst_row + off, rows, sem).start()

        inc = take.astype(jnp.int32)
        off = off + inc * rows
        out.append(started[c] + inc)
    return tuple(out)


def _wait_runs(src_ref, dst_ref, sem, started):
    for c, rows in enumerate(RUN_CHUNKS):
        def wait_one(j, carry, rows=rows):
            _run_copy(src_ref, 0, dst_ref, 0, rows, sem).wait()
            return carry

        lax.fori_loop(0, started[c], wait_one, 0)


def _dispatch_kernel(start_ref, cnt_ref, meta_ref, slot_ref, x_ref, xt_ref, xs_ref, sorted_ref, zero_ref, sem, zsem, *,
                     cap, n_main):
    i = pl.program_id(0)
    tm = MOE_TOKENS

    @pl.when(i == 0)
    def _():
        zero_ref[...] = jnp.zeros_like(zero_ref)

        def fill_expert(e, started):
            lo = start_ref[e] + cnt_ref[e]
            hi = jnp.where(e == N_EXPERTS - 1, cap, start_ref[jnp.minimum(e + 1, N_EXPERTS - 1)])
            n_full = (hi - lo) // tm

            def fill_full(j, st):
                return _start_run(zero_ref, 0, xs_ref, lo + j * tm, jnp.int32(tm), zsem, st)

            started = lax.fori_loop(0, n_full, fill_full, started)
            return _start_run(zero_ref, 0, xs_ref, lo + n_full * tm, (hi - lo) - n_full * tm, zsem, started)

        filled = lax.fori_loop(0, N_EXPERTS, fill_expert, tuple(jnp.int32(0) for _ in RUN_CHUNKS))
        _wait_runs(zero_ref, xs_ref, zsem, filled)

    srow = lax.broadcasted_iota(jnp.int32, (TILE_SLOTS, tm), 0)
    onehot = srow == slot_ref[0:1, :]
    for k in range(1, TOP_K):
        onehot = onehot | (srow == slot_ref[k:k + 1, :])
    sorted_ref[...] = _pack_bf16_pairs(jnp.dot(onehot.astype(BF16), _token_tile(i, n_main, x_ref, xt_ref).astype(BF16),
                                               preferred_element_type=F32))

    def copy_expert(e, started):
        n = meta_ref[0, 0, e]
        return _start_run(sorted_ref, meta_ref[0, 1, e], xs_ref, start_ref[e] + meta_ref[0, 2, e], n, sem, started)

    started = lax.fori_loop(0, N_EXPERTS, copy_expert, tuple(jnp.int32(0) for _ in RUN_CHUNKS))
    _wait_runs(sorted_ref, xs_ref, sem, started)


def _dispatch(h_main, h_tail, slot_t, meta, seg_start, counts, cap):
    tm = MOE_TOKENS
    n_main = h_main.shape[0] // tm
    return pl.pallas_call(
        functools.partial(_dispatch_kernel, cap=cap, n_main=n_main),
        grid_spec=pltpu.PrefetchScalarGridSpec(
            num_scalar_prefetch=2,
            grid=(n_main + 1,),
            in_specs=[pl.BlockSpec((1, SUBLANES, LANES), lambda i, *_: (i, 0, 0), memory_space=pltpu.SMEM),
                      pl.BlockSpec((ROUTE_ROWS, tm), lambda i, *_: (0, i)),
                      *_token_tile_specs(n_main)],
            out_specs=pl.BlockSpec(memory_space=pl.ANY),
            scratch_shapes=[pltpu.VMEM((TILE_SLOTS, PACKED_D), U32), pltpu.VMEM((tm, PACKED_D), U32),
                            pltpu.SemaphoreType.DMA, pltpu.SemaphoreType.DMA]),
        out_shape=jax.ShapeDtypeStruct((cap, PACKED_D), U32),
        compiler_params=_cparams(("arbitrary",)),
        name="moe_dispatch",
    )(seg_start, counts, meta, slot_t, h_main, h_tail)


def _swiglu(x, wg, wu, wd):
    xb = x.astype(BF16)
    g = jnp.dot(xb, wg.astype(BF16), preferred_element_type=F32)
    u = jnp.dot(xb, wu.astype(BF16), preferred_element_type=F32)
    return jnp.dot((_silu(g) * u).astype(BF16), wd.astype(BF16), preferred_element_type=F32)


EXPERT_RING = 3


def _experts_kernel(be_ref, used_ref, xs_ref, wg_ref, wu_ref, wd_ref, y_ref, xbuf, sems):
    i = pl.program_id(0)
    n = pl.num_programs(0)

    def block_copy(blk):
        slot = lax.rem(blk, EXPERT_RING)
        rows = pl.ds(pl.multiple_of(blk * MOE_BLOCK, MOE_BLOCK), MOE_BLOCK)
        return pltpu.make_async_copy(xs_ref.at[rows], xbuf.at[slot], sems.at[slot])

    @pl.when(i == 0)
    def _():
        for j in range(EXPERT_RING - 1):
            block_copy(jnp.int32(j)).start()

    @pl.when(i + EXPERT_RING - 1 < n)
    def _():
        block_copy(i + EXPERT_RING - 1).start()

    block_copy(i).wait()

    @pl.when(i < used_ref[0])
    def _():
        y = _swiglu(_unpack_bf16_pairs(xbuf[lax.rem(i, EXPERT_RING)]), wg_ref[0], wu_ref[0], wd_ref[0])
        y_ref[...] = _pack_bf16_pairs(_round_bf16(y))

    @pl.when(i >= used_ref[0])
    def _():
        y_ref[...] = jnp.zeros_like(y_ref)


def _experts(xs, block_expert, used_blocks, w_gate, w_up, w_down):
    cap = xs.shape[0]
    return pl.pallas_call(
        _experts_kernel,
        grid_spec=pltpu.PrefetchScalarGridSpec(
            num_scalar_prefetch=2,
            grid=(cap // MOE_BLOCK,),
            in_specs=[pl.BlockSpec(memory_space=pl.ANY),
                      pl.BlockSpec((1, D_MODEL, D_EXPERT), lambda i, be, used: (be[i], 0, 0)),
                      pl.BlockSpec((1, D_MODEL, D_EXPERT), lambda i, be, used: (be[i], 0, 0)),
                      pl.BlockSpec((1, D_EXPERT, D_MODEL), lambda i, be, used: (be[i], 0, 0))],
            out_specs=pl.BlockSpec((MOE_BLOCK, PACKED_D), lambda i, be, used: (i, 0)),
            scratch_shapes=[pltpu.VMEM((EXPERT_RING, MOE_BLOCK, PACKED_D), U32),
                            pltpu.SemaphoreType.DMA((EXPERT_RING,))]),
        out_shape=jax.ShapeDtypeStruct((cap, PACKED_D), U32),
        compiler_params=_cparams(("arbitrary",)),
        name="moe_experts",
    )(block_expert, used_blocks, xs, w_gate, w_up, w_down)


def _combine_kernel(start_ref, meta_ref, h_ref, ht_ref, info_ref, sg_ref, su_ref, sd_ref, g_ref, b_ref,
                    ys_ref, o_ref, ot_ref, buf_ref, sem, *, n_main):
    i = pl.program_id(0)
    tm = MOE_TOKENS

    @pl.when(i == 0)
    def _():
        buf_ref[...] = jnp.zeros_like(buf_ref)

    def fetch_expert(e, started):
        n = meta_ref[0, 0, e]
        return _start_run(ys_ref, start_ref[e] + meta_ref[0, 2, e], buf_ref, meta_ref[0, 1, e], n, sem, started)

    started = lax.fori_loop(0, N_EXPERTS, fetch_expert, tuple(jnp.int32(0) for _ in RUN_CHUNKS))
    h = _token_tile(i, n_main, h_ref, ht_ref)
    f = _swiglu(h, sg_ref[...], su_ref[...], sd_ref[...])
    info = info_ref[...]
    scol = lax.broadcasted_iota(jnp.int32, (tm, TILE_SLOTS), 1).astype(F32)
    mix = jnp.zeros((tm, TILE_SLOTS), F32)
    for k in range(TOP_K):
        mix = jnp.where(info[:, ROUTE_ROWS + k:ROUTE_ROWS + k + 1] == scol, info[:, k:k + 1], mix)
    _wait_runs(ys_ref, buf_ref, sem, started)
    acc = jnp.dot(mix.astype(BF16), _unpack_bf16_pairs(buf_ref[...]), preferred_element_type=F32)
    out = _layer_norm(DEEPNORM_ALPHA * h + (acc + f), g_ref[...], b_ref[...])

    @pl.when(i < n_main)
    def _():
        o_ref[...] = out

    @pl.when(i >= n_main)
    def _():
        ot_ref[...] = out


def _combine(h_main, h_tail, ys, meta, tokinfo, seg_start, sh_gate, sh_up, sh_down, ln_g, ln_b):
    tm = MOE_TOKENS
    n_main = h_main.shape[0] // tm
    const = lambda a: pl.BlockSpec(a.shape, lambda i, *_: (0,) * a.ndim)
    return pl.pallas_call(
        functools.partial(_combine_kernel, n_main=n_main),
        grid_spec=pltpu.PrefetchScalarGridSpec(
            num_scalar_prefetch=1,
            grid=(n_main + 1,),
            in_specs=[pl.BlockSpec((1, SUBLANES, LANES), lambda i, *_: (i, 0, 0), memory_space=pltpu.SMEM),
                      *_token_tile_specs(n_main),
                      pl.BlockSpec((tm, LANES), lambda i, *_: (i, 0)),
                      const(sh_gate), const(sh_up), const(sh_down), const(ln_g), const(ln_b),
                      pl.BlockSpec(memory_space=pl.ANY)],
            out_specs=list(_token_tile_specs(n_main)),
            scratch_shapes=[pltpu.VMEM((TILE_SLOTS, PACKED_D), U32), pltpu.SemaphoreType.DMA]),
        out_shape=[jax.ShapeDtypeStruct(h_main.shape, F32), jax.ShapeDtypeStruct((tm, D_MODEL), F32)],
        compiler_params=_cparams(("arbitrary",)),
        name="moe_combine",
    )(seg_start, meta, h_main, h_tail, tokinfo, sh_gate, sh_up, sh_down, ln_g, ln_b, ys)


def _moe_ln(h_main, h_tail, n_valid, router_w, router_bias, w_gate, w_up, w_down, sh_gate, sh_up, sh_down, ln_g, ln_b):
    n_tiles = h_main.shape[0] // MOE_TOKENS + 1
    slot_t, tokinfo, meta, cnt = _route(h_main, h_tail, router_w.T.astype(BF16),
                                        jnp.broadcast_to(router_bias.astype(F32)[:, None], (N_EXPERTS, LANES)), n_valid)
    counts = cnt[:, 0].astype(jnp.int32)
    padded = (counts + MOE_BLOCK - 1) // MOE_BLOCK * MOE_BLOCK
    seg_end = jnp.cumsum(padded)
    seg_start = seg_end - padded
    run_pad = n_tiles * N_EXPERTS * (SUBLANES - 1)
    n_blocks = -(-(n_valid * TOP_K + run_pad + N_EXPERTS * (MOE_BLOCK - 1)) // MOE_BLOCK)
    cap = n_blocks * MOE_BLOCK
    block_first_row = jnp.arange(n_blocks, dtype=jnp.int32) * MOE_BLOCK
    block_expert = jnp.minimum(jnp.sum((seg_end[None, :] <= block_first_row[:, None]).astype(jnp.int32), axis=1),
                               N_EXPERTS - 1)
    xs = _dispatch(h_main, h_tail, slot_t, meta, seg_start, counts, cap)
    used_blocks = (seg_end[N_EXPERTS - 1:] // MOE_BLOCK).astype(jnp.int32)
    ys = _experts(xs, block_expert, used_blocks, w_gate, w_up, w_down)
    return _combine(h_main, h_tail, ys, meta, tokinfo, seg_start, sh_gate.astype(BF16), sh_up.astype(BF16),
                    sh_down.astype(BF16), ln_g, ln_b)


def _ssd_sample_kernel(xbc_ref, z_ref, dtg_ref, sconv_ref, s0_ref, convw_ref, convb_ref, dtb_ref, alog_ref,
                       dskip_ref, normw_ref, y_ref, s_ref, conv_out_ref, xc_ref, dt_ref, da_ref):
    b = pl.program_id(0)

    @pl.when(b == 0)
    def _():
        xin = xbc_ref[...]
        xc = convw_ref[SSD_CONV - 1:SSD_CONV, :] * xin
        for k in range(SSD_CONV - 1):
            xc = xc + convw_ref[k:k + 1, :] * sconv_ref[k]
        xc_ref[...] = _silu(xc + convb_ref[...])
        dt = _softplus(dtg_ref[...] + dtb_ref[...])
        dt_ref[...] = dt
        da_ref[...] = jnp.exp(dt * (-jnp.exp(alog_ref[...])))
        for k in range(SSD_CONV - 2):
            conv_out_ref[k] = sconv_ref[k + 1]
        conv_out_ref[SSD_CONV - 2] = xin

    xc = xc_ref[pl.ds(b, 1), :]
    dt = dt_ref[pl.ds(b, 1), :]
    da = da_ref[pl.ds(b, 1), :]
    ns = SSD_GROUPS * SSD_STATE
    eye = (lax.broadcasted_iota(jnp.int32, (HEAD_DIM, HEAD_DIM), 0)
           == lax.broadcasted_iota(jnp.int32, (HEAD_DIM, HEAD_DIM), 1))
    hpg = SSD_HEADS // SSD_GROUPS
    y_parts = []
    for h in range(SSD_HEADS):
        g = h // hpg
        x_h = xc[:, h * HEAD_DIM:(h + 1) * HEAD_DIM]
        b_g = xc[:, SSD_D + g * SSD_STATE:SSD_D + (g + 1) * SSD_STATE]
        c_g = xc[:, SSD_D + ns + g * SSD_STATE:SSD_D + ns + (g + 1) * SSD_STATE]
        xdt_col = jnp.sum(jnp.where(eye, x_h * dt[:, h:h + 1], 0.0), axis=1, keepdims=True)
        s_new = da[:, h:h + 1] * s0_ref[0, h] + xdt_col * b_g
        s_ref[0, h] = s_new
        y_h = _bdot_nt(c_g, s_new) + dskip_ref[:, h * HEAD_DIM:(h + 1) * HEAD_DIM] * x_h
        y_parts.append(y_h)
    y = jnp.concatenate(y_parts, axis=1)
    y_ref[pl.ds(b, 1), :] = _gated_group_norm(y, z_ref[pl.ds(b, 1), :], normw_ref[...])


def _ssd_sample(xbc, z, dtg, state_conv_t, state_ssm, conv_w, conv_b, dt_bias_pad, a_log_pad, d_skip_full, norm_w):
    bs = xbc.shape[0]
    const = lambda a: pl.BlockSpec(a.shape, lambda b: (0,) * a.ndim)
    state_spec = pl.BlockSpec((1, SSD_HEADS, HEAD_DIM, SSD_STATE), lambda b: (b, 0, 0, 0))
    return pl.pallas_call(
        _ssd_sample_kernel,
        grid=(bs,),
        in_specs=[const(xbc), const(z), const(dtg), const(state_conv_t), state_spec, const(conv_w), const(conv_b),
                  const(dt_bias_pad), const(a_log_pad), const(d_skip_full), const(norm_w)],
        out_specs=[pl.BlockSpec((bs, SSD_D), lambda b: (0, 0)), state_spec,
                   pl.BlockSpec((SSD_CONV - 1, bs, SSD_CONV_CH), lambda b: (0, 0, 0))],
        out_shape=[jax.ShapeDtypeStruct((bs, SSD_D), F32),
                   jax.ShapeDtypeStruct(state_ssm.shape, F32),
                   jax.ShapeDtypeStruct((SSD_CONV - 1, bs, SSD_CONV_CH), F32)],
        scratch_shapes=[pltpu.VMEM((bs, SSD_CONV_CH), F32), pltpu.VMEM((bs, LANES), F32),
                        pltpu.VMEM((bs, LANES), F32)],
        compiler_params=_cparams(("arbitrary",)),
        name="ssd_sample",
    )(xbc, z, dtg, state_conv_t, state_ssm, conv_w, conv_b, dt_bias_pad, a_log_pad, d_skip_full, norm_w)


SEL_PAST = TOP_N - 1
BLOCKS_PER_PAGE = PAGE_SIZE // CMP_BLOCK
KV_FEATS = 2 * KV_D


def _compress_consts_t(cmp_pe, cmp_w1, cmp_b1, cmp_w2, cmp_b2):
    pe_t = jnp.stack([jnp.tile(cmp_pe[k].T, (1, BLOCKS_PER_PAGE)) for k in range(2)])
    w1_t = jnp.stack([_block_diag2(jnp.swapaxes(cmp_w1[k], 0, 1)) for k in range(2)]).astype(BF16)
    b1_t = jnp.stack([jnp.tile(cmp_b1[k], BLOCKS_PER_PAGE) for k in range(2)])[:, None, :]
    w2_t = jnp.stack([_block_diag2(cmp_w2[k]) for k in range(2)]).astype(BF16)
    b2_t = jnp.stack([jnp.tile(cmp_b2[k], BLOCKS_PER_PAGE) for k in range(2)])[:, None, :]
    return pe_t, w1_t, b1_t, w2_t, b2_t


def _compress_pages_kernel(pt_ref, pe_ref, w1_ref, b1_ref, w2_ref, b2_ref, pool_ref, o_ref, kbuf, vbuf, sems, *,
                           n_pages):
    b = pl.program_id(0)
    nb = pl.num_programs(0)
    bufs = (kbuf, vbuf)

    def half_copy(seq, kind, p):
        return pltpu.make_async_copy(pool_ref.at[pt_ref[seq, p], pl.ds(kind * KV_D, KV_D)],
                                     bufs[kind].at[:, p], sems.at[kind])

    def start_half(seq, kind):
        lax.fori_loop(0, n_pages, lambda p, c: (half_copy(seq, kind, p).start(), c)[1], 0)

    def wait_half(seq, kind):
        lax.fori_loop(0, n_pages, lambda p, c: (half_copy(seq, kind, p).wait(), c)[1], 0)

    @pl.when(b == 0)
    def _():
        start_half(b, 0)
        start_half(b, 1)

    for kind in range(2):
        wait_half(b, kind)
        def add_feature(d, acc, kind=kind):
            x = jnp.concatenate([bufs[kind][h * HEAD_DIM + d] for h in range(NSA_KV_HEADS)], axis=0) \
                + pe_ref[kind, pl.ds(d, 1), :]
            return acc + jnp.dot(x.astype(BF16), w1_ref[kind, d], preferred_element_type=F32)

        acc = lax.fori_loop(0, HEAD_DIM, add_feature,
                            jnp.zeros((NSA_KV_HEADS * n_pages, BLOCKS_PER_PAGE * CMP_HIDDEN), F32), unroll=8)
        hid = _silu(acc + b1_ref[kind])
        out = jnp.dot(hid.astype(BF16), w2_ref[kind], preferred_element_type=F32) + b2_ref[kind]
        for h in range(NSA_KV_HEADS):
            o_ref[0, kind * NSA_KV_HEADS + h] = out[h * n_pages:(h + 1) * n_pages]

        @pl.when(b + 1 < nb)
        def _(kind=kind):
            start_half(b + 1, kind)


def _compress_pages(pool_t, page_table, consts):
    bs, n_pages = page_table.shape
    const = lambda a: pl.BlockSpec(a.shape, lambda b, pt: (0,) * a.ndim)
    return pl.pallas_call(
        functools.partial(_compress_pages_kernel, n_pages=n_pages),
        grid_spec=pltpu.PrefetchScalarGridSpec(
            num_scalar_prefetch=1,
            grid=(bs,),
            in_specs=[const(a) for a in consts] + [pl.BlockSpec(memory_space=pl.ANY)],
            out_specs=pl.BlockSpec((1, 2 * NSA_KV_HEADS, n_pages, LANES), lambda b, pt: (b, 0, 0, 0)),
            scratch_shapes=[pltpu.VMEM((KV_D, n_pages, PAGE_SIZE), F32), pltpu.VMEM((KV_D, n_pages, PAGE_SIZE), F32),
                            pltpu.SemaphoreType.DMA((2,))]),
        out_shape=jax.ShapeDtypeStruct((bs, 2 * NSA_KV_HEADS, n_pages, LANES), F32),
        compiler_params=_cparams(("arbitrary",)),
        name="compress_pages",
    )(page_table, *consts, pool_t)


def _group_heads(q_row, hk):
    hpg = NSA_HEADS // NSA_KV_HEADS
    low = lax.broadcasted_iota(jnp.int32, (1, LANES), 1) < HEAD_DIM
    rows = []
    for r in range(hpg):
        head = hk * hpg + r
        tile = q_row[:, (head // 2) * LANES:(head // 2 + 1) * LANES]
        if head % 2 == 1:
            tile = pltpu.roll(tile, HEAD_DIM, 1)
        rows.append(jnp.where(low, tile, 0.0))
    return jnp.concatenate(rows + [jnp.zeros((SUBLANES - hpg, LANES), F32)], axis=0)


def _spread_heads(o_groups):
    hpg = NSA_HEADS // NSA_KV_HEADS
    return jnp.concatenate([o[r:r + 1, 0:HEAD_DIM] for o in o_groups for r in range(hpg)], axis=1)


def _nsa_sample_cmp_t_kernel(qc_ref, cmp_ref, ocmp_ref, idx_ref, *, n_pages):
    b = pl.program_id(0)
    nc = n_pages * BLOCKS_PER_PAGE
    scale = HEAD_DIM ** -0.5
    hpg = NSA_HEADS // NSA_KV_HEADS
    q_row = qc_ref[pl.ds(b, 1), :] * scale
    lane = lax.broadcasted_iota(jnp.int32, (1, LANES), 1)
    pos_r = lax.broadcasted_iota(jnp.int32, (1, nc), 1)
    bid_r = (pos_r % n_pages) * BLOCKS_PER_PAGE + pos_r // n_pages
    pos_c = lax.broadcasted_iota(jnp.int32, (nc, 1), 0)
    bid_c = (pos_c % n_pages) * BLOCKS_PER_PAGE + pos_c // n_pages
    o_groups = []
    for hk in range(NSA_KV_HEADS):
        kc = cmp_ref[0, hk].astype(BF16)
        vc = cmp_ref[0, NSA_KV_HEADS + hk].astype(BF16)
        qh = _group_heads(q_row, hk)
        s = jnp.concatenate(
            [lax.dot_general(pltpu.roll(qh, c * HEAD_DIM, 1).astype(BF16) if c else qh.astype(BF16), kc,
                             (((1,), (1,)), ((), ())), preferred_element_type=F32)
             for c in range(BLOCKS_PER_PAGE)], axis=1)
        ex = jnp.exp(s - jnp.max(s, axis=-1, keepdims=True))
        p = ex / jnp.sum(ex, axis=-1, keepdims=True)
        o = jnp.dot(p[:, 0:n_pages].astype(BF16), vc, preferred_element_type=F32)
        for c in range(1, BLOCKS_PER_PAGE):
            oc = jnp.dot(p[:, c * n_pages:(c + 1) * n_pages].astype(BF16), vc, preferred_element_type=F32)
            o = o + pltpu.roll(oc, LANES - c * HEAD_DIM, 1)
        o_groups.append(o)
        hrow = lax.broadcasted_iota(jnp.int32, p.shape, 0) < hpg
        imp = jnp.sum(jnp.where(hrow, p, 0.0), axis=0, keepdims=True)
        score = jnp.where((bid_r == 0) | (bid_r == nc - 1), FORCED_SCORE, imp)
        score_col = jnp.concatenate([score, jnp.zeros((LANES - 1, nc), F32)], 0).T[:, 0:1]
        beats = (score_col > score) | ((score_col == score) & (bid_c < bid_r))
        rank = jnp.sum(beats.astype(F32), axis=0, keepdims=True)
        row = jnp.zeros((1, LANES), F32)
        bid_f = bid_r.astype(F32)
        for k in range(SEL_PAST):
            blk = jnp.sum(jnp.where(rank == k, bid_f, 0.0), axis=1, keepdims=True)
            row = jnp.where(lane == k, blk, row)
        idx_ref[pl.ds(b * NSA_KV_HEADS + hk, 1), :] = row.astype(jnp.int32)
    ocmp_ref[pl.ds(b, 1), :] = _spread_heads(o_groups)


def _nsa_sample_cmp_t(qc, kvcmp_t):
    bs, _, n_pages, _ = kvcmp_t.shape
    return pl.pallas_call(
        functools.partial(_nsa_sample_cmp_t_kernel, n_pages=n_pages),
        grid=(bs,),
        in_specs=[pl.BlockSpec((bs, NSA_D), lambda b: (0, 0)),
                  pl.BlockSpec((1, 2 * NSA_KV_HEADS, n_pages, LANES), lambda b: (b, 0, 0, 0))],
        out_specs=[pl.BlockSpec((bs, NSA_D), lambda b: (0, 0)),
                   pl.BlockSpec((bs * NSA_KV_HEADS, LANES), lambda b: (0, 0))],
        out_shape=[jax.ShapeDtypeStruct((bs, NSA_D), F32),
                   jax.ShapeDtypeStruct((bs * NSA_KV_HEADS, LANES), jnp.int32)],
        compiler_params=_cparams(("arbitrary",)),
        name="nsa_sample_cmp",
    )(qc, kvcmp_t)


def _sel_block_copies(pool_ref, pt_ref, sel_ref, kbuf, vbuf, sem, b, hk, k):
    blk = sel_ref[b * NSA_KV_HEADS + hk, k]
    page = pt_ref[b, lax.shift_right_logical(blk, int(math.log2(BLOCKS_PER_PAGE)))]
    j = hk * SEL_PAST + k
    return (pltpu.make_async_copy(pool_ref.at[page, pl.ds(hk * HEAD_DIM, HEAD_DIM)], kbuf.at[j], sem),
            pltpu.make_async_copy(pool_ref.at[page, pl.ds(KV_D + hk * HEAD_DIM, HEAD_DIM)], vbuf.at[j], sem))


def _nsa_sample_attn_t_kernel(pt_ref, sel_ref, qr_ref, new_sel_ref, new_win_ref, win_ref, dtg_ref, ocmp_ref,
                              pool_ref, o_ref, kbuf, vbuf, sem):
    b = pl.program_id(0)
    for hk in range(NSA_KV_HEADS):
        for k in range(SEL_PAST):
            for cp in _sel_block_copies(pool_ref, pt_ref, sel_ref, kbuf, vbuf, sem, b, hk, k):
                cp.start()
    for hk in range(NSA_KV_HEADS):
        for k in range(SEL_PAST):
            for cp in _sel_block_copies(pool_ref, pt_ref, sel_ref, kbuf, vbuf, sem, b, hk, k):
                cp.wait()
    scale = HEAD_DIM ** -0.5
    q_row = qr_ref[pl.ds(b, 1), :] * scale
    sig = _sigmoid(dtg_ref[pl.ds(b, 1), :])
    lane = lax.broadcasted_iota(jnp.int32, (1, PAGE_SIZE), 1)
    o_slc, o_win = [], []
    for hk in range(NSA_KV_HEADS):
        qh = _group_heads(q_row, hk)[:, 0:HEAD_DIM].astype(BF16)

        def new_row(ref, kind):
            t = ref[pl.ds(b, 1), :][:, kind * KV_D:(kind + 1) * KV_D]
            if hk == 1:
                t = pltpu.roll(t, HEAD_DIM, 1)
            return t[:, 0:HEAD_DIM].astype(BF16).astype(F32)

        def attend(kt, vt, mask, new_ref, n_new):
            s = jnp.dot(qh, kt.astype(BF16), preferred_element_type=F32)
            if mask is not None:
                s = jnp.where(mask, s, NEG)
            s_new = jnp.sum(qh.astype(F32) * new_row(new_ref, 0), axis=1, keepdims=True)
            m = jnp.maximum(jnp.max(s, axis=-1, keepdims=True), s_new)
            ex = jnp.exp(s - m)
            ex_new = jnp.exp(s_new - m) * n_new
            den = jnp.sum(ex, axis=-1, keepdims=True) + ex_new
            o = lax.dot_general((ex / den).astype(BF16), vt.astype(BF16), (((1,), (1,)), ((), ())),
                                preferred_element_type=F32)
            return o + (ex_new / den).astype(BF16).astype(F32) * new_row(new_ref, 1)

        kt = jnp.concatenate([kbuf[hk * SEL_PAST + k] for k in range(SEL_PAST)], axis=1)
        vt = jnp.concatenate([vbuf[hk * SEL_PAST + k] for k in range(SEL_PAST)], axis=1)
        mask = jnp.concatenate(
            [lane // SEL_BLOCK == (sel_ref[b * NSA_KV_HEADS + hk, k] & (BLOCKS_PER_PAGE - 1))
             for k in range(SEL_PAST)], axis=1)
        o_slc.append(attend(kt, vt, mask, new_sel_ref, float(SEL_BLOCK)))
        o_win.append(attend(win_ref[0, hk * HEAD_DIM:(hk + 1) * HEAD_DIM, :],
                            win_ref[0, KV_D + hk * HEAD_DIM:KV_D + (hk + 1) * HEAD_DIM, :], None, new_win_ref, 1.0))
    gates = []
    for br in range(3):
        gates.append(jnp.concatenate(
            [jnp.broadcast_to(sig[:, GATE_COL0 + h * 3 + br:GATE_COL0 + h * 3 + br + 1], (1, HEAD_DIM))
             for h in range(NSA_HEADS)], axis=1))
    o_ref[pl.ds(b, 1), :] = (gates[0] * ocmp_ref[pl.ds(b, 1), :] + gates[1] * _spread_heads(o_slc)
                             + gates[2] * _spread_heads(o_win))


def _nsa_sample_attn_t(qr, new_sel, new_win, win_t, dtg, o_cmp, pool_sel_t, page_table, sel_idx):
    bs = qr.shape[0]
    const = lambda a: pl.BlockSpec(a.shape, lambda b, pt, sel: (0,) * a.ndim)
    n_buf = NSA_KV_HEADS * SEL_PAST
    return pl.pallas_call(
        _nsa_sample_attn_t_kernel,
        grid_spec=pltpu.PrefetchScalarGridSpec(
            num_scalar_prefetch=2,
            grid=(bs,),
            in_specs=[const(qr), const(new_sel), const(new_win),
                      pl.BlockSpec((1,) + win_t.shape[1:], lambda b, pt, sel: (b, 0, 0)),
                      const(dtg), const(o_cmp), pl.BlockSpec(memory_space=pl.ANY)],
            out_specs=pl.BlockSpec((bs, NSA_D), lambda b, pt, sel: (0, 0)),
            scratch_shapes=[pltpu.VMEM((n_buf, HEAD_DIM, PAGE_SIZE), F32), pltpu.VMEM((n_buf, HEAD_DIM, PAGE_SIZE), F32),
                            pltpu.SemaphoreType.DMA]),
        out_shape=jax.ShapeDtypeStruct((bs, NSA_D), F32),
        compiler_params=_cparams(("arbitrary",)),
        name="nsa_sample_attn",
    )(page_table, sel_idx, qr, new_sel, new_win, win_t, dtg, o_cmp, pool_sel_t)


def kernel(x_prompt, x_sample, cache_kv_cmp, cache_kv_sel, page_table, cache_kv_win, state_ssm, state_conv,
           emb_ln_g, emb_ln_b, w_in, conv_w, conv_b, dt_bias, a_log, d_skip, ssd_norm_w,
           cmp_pe, cmp_w1, cmp_b1, cmp_w2, cmp_b2, w_out, ln1_g, ln1_b,
           router_w, router_bias, exp_w_gate, exp_w_up, exp_w_down,
           sh_w_gate, sh_w_up, sh_w_down, ln2_g, ln2_b):
    bp, tp, _ = x_prompt.shape
    bs, ts, _ = x_sample.shape
    assert ts == 1 and DEPTH == 1
    n_prompt = bp * tp
    past_len = page_table.shape[1] * PAGE_SIZE
    l = 0
    w_perm = _permute_w_in(w_in[l])
    ln0_g, ln0_b = emb_ln_g[None], emb_ln_b[None]
    ssd_consts = (conv_w[l], conv_b[l][None], _pad_lanes(dt_bias[l]), _pad_lanes(a_log[l]),
                  jnp.repeat(d_skip[l], HEAD_DIM)[None], ssd_norm_w[l][None])
    cmp_consts = _compress_consts(cmp_pe[l], cmp_w1[l], cmp_b1[l], cmp_w2[l], cmp_b2[l])
    w_o = w_out[l].astype(BF16)
    w_o_ssd, w_o_nsa = w_o[:SSD_D], w_o[SSD_D:]
    ln1 = (ln1_g[l][None], ln1_b[l][None])
    kv_shape = (2, NSA_KV_HEADS, HEAD_DIM)

    hp, z, xbc, qc, qr, kvc, kvs, kvw, dtg, kvc_t, kvs_t, kvw_t = _inproj(
        x_prompt.reshape(n_prompt, D_MODEL), ln0_g, ln0_b, w_perm, _rope_tables(jnp.arange(tp)), 256,
        _rope_tables_t(jnp.arange(tp)))
    y_ssd, ssm_p, conv_p = _ssd_prompt(xbc, z, dtg, *ssd_consts, bp, tp)
    kvcmp = _compress_prompt(kvc, cmp_consts, tp)
    y_nsa = _nsa_prompt(qc, qr, dtg, kvcmp, kvs, kvw, bp, tp)
    h1p = _outproj(y_ssd, y_nsa, hp, w_o_ssd, w_o_nsa, *ln1, 512)
    n_keep = min(WINDOW, tp)
    cache_leaf = lambda a: jnp.transpose(a.reshape((bp,) + kv_shape + (a.shape[-1],)), (0, 4, 1, 2, 3))[None]
    kvc_p = cache_leaf(kvc_t)
    kvs_p = cache_leaf(kvs_t)
    kvw_p = cache_leaf(kvw_t[:, :, tp - n_keep:])

    s_hs, s_z, s_xbc, s_qc, s_qr, s_kvc, s_kvs, s_kvw, s_dtg = _inproj(
        x_sample.reshape(bs, D_MODEL), ln0_g, ln0_b, w_perm, _rope_tables(jnp.full((bs,), past_len)), bs)
    s_y_ssd, ssm_s, conv_s_t = _ssd_sample(s_xbc, s_z, s_dtg, jnp.swapaxes(state_conv[l], 0, 1), state_ssm[l],
                                           *ssd_consts)
    n_pool = cache_kv_cmp.shape[1]
    feature_major = lambda c, rows: jnp.swapaxes(c.reshape(-1, rows, 2 * KV_D), 1, 2)
    s_kvcmp = _compress_pages(feature_major(cache_kv_cmp[l], PAGE_SIZE), page_table,
                              _compress_consts_t(cmp_pe[l], cmp_w1[l], cmp_b1[l], cmp_w2[l], cmp_b2[l]))
    s_o_cmp, s_sel = _nsa_sample_cmp_t(s_qc, s_kvcmp)
    buf_win = cache_kv_win[l].reshape(bs, -1, 2 * KV_D)
    s_y_nsa = _nsa_sample_attn_t(
        s_qr, s_kvs, s_kvw, feature_major(cache_kv_win[l], buf_win.shape[1]), s_dtg, s_o_cmp,
        feature_major(cache_kv_sel[l], PAGE_SIZE), page_table, s_sel)
    h1s = _outproj(s_y_ssd, s_y_nsa, s_hs, w_o_ssd, w_o_nsa, *ln1, bs)
    win_all = jnp.concatenate([buf_win, s_kvw[:, None, :]], 1)
    n_keep_s = min(WINDOW, past_len + ts)
    kvw_s = win_all[:, win_all.shape[1] - n_keep_s:].reshape((1, bs, n_keep_s) + kv_shape)
    kvc_s = s_kvc.reshape((1, bs, ts) + kv_shape)
    kvs_s = s_kvs.reshape((1, bs, ts) + kv_shape)

    assert n_prompt % MOE_TOKENS == 0 and bs * ts <= MOE_TOKENS
    n_tok = n_prompt + bs * ts
    tail = jnp.concatenate([h1s, jnp.zeros((MOE_TOKENS - bs * ts, D_MODEL), F32)], 0)
    out_main, out_tail = _moe_ln(h1p, tail, n_tok, router_w[l], router_bias[l], exp_w_gate[l], exp_w_up[l],
                                 exp_w_down[l], sh_w_gate[l], sh_w_up[l], sh_w_down[l], ln2_g[l][None], ln2_b[l][None])
    y_prompt = out_main.reshape(bp, tp, D_MODEL)
    y_sample = out_tail[:bs * ts].reshape(bs, ts, D_MODEL)
    return (y_prompt, y_sample, kvc_p, kvs_p, kvw_p, ssm_p[None], conv_p[None],
            kvc_s, kvs_s, kvw_s, ssm_s[None], jnp.swapaxes(conv_s_t, 0, 1)[None])
```

```python
import functools
import math

import jax
import jax.numpy as jnp
import numpy as np
from jax import lax
from jax.experimental import pallas as pl
from jax.experimental.pallas import tpu as pltpu

D_MODEL = 1024
HEAD_DIM = 64
SSD_HEADS = 8
SSD_D = SSD_HEADS * HEAD_DIM
SSD_GROUPS = 2
SSD_STATE = 128
SSD_CONV = 4
SSD_CONV_CH = SSD_D + 2 * SSD_GROUPS * SSD_STATE
SSD_CHUNK = 128
NSA_HEADS = 8
NSA_KV_HEADS = 2
NSA_D = NSA_HEADS * HEAD_DIM
KV_D = NSA_KV_HEADS * HEAD_DIM
CMP_BLOCK = 64
CMP_HIDDEN = 128
SEL_BLOCK = 64
TOP_N = 16
WINDOW = 512
Q_BLOCK = 128
ROT_DIM = HEAD_DIM // 4
ROPE_THETA = 500000.0
N_EXPERTS = 64
TOP_K = 6
N_EXPERT_GROUPS = 8
EXPERTS_PER_GROUP = N_EXPERTS // N_EXPERT_GROUPS
TOPK_GROUPS = 4
D_EXPERT = 256
D_SHARED = 256
ROUTED_SCALE = 2.5
MOE_BLOCK = 512
DEPTH = 1
DEEPNORM_ALPHA = (2.0 * DEPTH) ** 0.25
LN_EPS = 1e-5
RMS_EPS = 1e-5
NEG = -1e30
FORCED_SCORE = 1e4
PAGE_SIZE = 128

LANES = 128
SUBLANES = 8
VMEM_LIMIT_BYTES = 56 * 1024 * 1024

U_Z = 0
U_XBC = U_Z + SSD_D
U_Q = U_XBC + SSD_CONV_CH
U_KVC = U_Q + NSA_D
U_KVS = U_KVC + 2 * KV_D
U_KVW = U_KVS + 2 * KV_D
U_DTG = U_KVW + 2 * KV_D
U_TOTAL = U_DTG + LANES
GATE_COL0 = SSD_HEADS

BF16 = jnp.bfloat16
F32 = jnp.float32


def _cparams(sem):
    return pltpu.CompilerParams(dimension_semantics=sem, vmem_limit_bytes=VMEM_LIMIT_BYTES)


def _bdot(a, b):
    return jnp.dot(a.astype(BF16), b.astype(BF16), preferred_element_type=F32)


def _bdot_nt(a, b):
    return lax.dot_general(a.astype(BF16), b.astype(BF16), (((1,), (1,)), ((), ())),
                           preferred_element_type=F32)


def _hdot(a, b):
    return jnp.dot(a, b, preferred_element_type=F32, precision=lax.Precision.HIGHEST)


def _sigmoid(x):
    return 1.0 / (1.0 + jnp.exp(-x))


def _silu(x):
    return x * _sigmoid(x)


def _layer_norm(x, g, b):
    mu = jnp.mean(x, axis=-1, keepdims=True)
    xc = x - mu
    var = jnp.mean(xc * xc, axis=-1, keepdims=True)
    return xc * lax.rsqrt(var + LN_EPS) * g + b


def _rope_tile(x, cos, sa, sb):
    return x * cos + pltpu.roll(x, LANES - ROT_DIM // 2, 1) * sa + pltpu.roll(x, ROT_DIM // 2, 1) * sb


def _rope_rows(x, cos, sin):
    half = ROT_DIM // 2
    parts = []
    for hd in range(NSA_KV_HEADS):
        r0 = hd * HEAD_DIM
        x1, x2 = x[r0:r0 + half], x[r0 + half:r0 + ROT_DIM]
        parts += [x1 * cos - x2 * sin, x2 * cos + x1 * sin, x[r0 + ROT_DIM:r0 + HEAD_DIM]]
    return jnp.concatenate(parts, axis=0)


def _inproj_kernel(x_ref, g_ref, b_ref, w_ref, rope_ref, *refs, feature_major):
    if feature_major:
        wkv_t_ref, rope_t_ref = refs[:2]
        refs = refs[2:]
    h_ref, z_ref, xbc_ref, qc_ref, qr_ref, kvc_ref, kvs_ref, kvw_ref, dtg_ref = refs[:9]
    h = _layer_norm(x_ref[...], g_ref[...], b_ref[...])
    h_ref[...] = h
    hb = h.astype(BF16)
    if feature_major:
        kvc_t_ref, kvs_t_ref, kvw_t_ref = refs[9:]
        ut = lax.dot_general(wkv_t_ref[...], hb, (((1,), (1,)), ((), ())), preferred_element_type=F32)
        half = ROT_DIM // 2
        cos_t, sin_t = rope_t_ref[0:half, :], rope_t_ref[half:2 * half, :]
        kvc_t_ref[0] = ut[0:2 * KV_D]
        kvs_t_ref[0, 0:KV_D] = _rope_rows(ut[2 * KV_D:3 * KV_D], cos_t, sin_t)
        kvs_t_ref[0, KV_D:2 * KV_D] = ut[3 * KV_D:4 * KV_D]
        kvw_t_ref[0, 0:KV_D] = _rope_rows(ut[4 * KV_D:5 * KV_D], cos_t, sin_t)
        kvw_t_ref[0, KV_D:2 * KV_D] = ut[5 * KV_D:6 * KV_D]
    u = jnp.dot(hb, w_ref[...], preferred_element_type=F32)
    cos = rope_ref[:, 0:LANES]
    sa = rope_ref[:, LANES:2 * LANES]
    sb = rope_ref[:, 2 * LANES:3 * LANES]
    z_ref[...] = u[:, U_Z:U_XBC]
    xbc_ref[...] = u[:, U_XBC:U_Q]
    qc_ref[...] = u[:, U_Q:U_KVC].astype(qc_ref.dtype)
    for c in range(NSA_D // LANES):
        qr_ref[:, c * LANES:(c + 1) * LANES] = _rope_tile(
            u[:, U_Q + c * LANES:U_Q + (c + 1) * LANES], cos, sa, sb).astype(qr_ref.dtype)
    kvc_ref[...] = u[:, U_KVC:U_KVS]
    kvs_ref[:, 0:KV_D] = _rope_tile(u[:, U_KVS:U_KVS + KV_D], cos, sa, sb)
    kvs_ref[:, KV_D:2 * KV_D] = u[:, U_KVS + KV_D:U_KVW]
    kvw_ref[:, 0:KV_D] = _rope_tile(u[:, U_KVW:U_KVW + KV_D], cos, sa, sb)
    kvw_ref[:, KV_D:2 * KV_D] = u[:, U_KVW + KV_D:U_DTG]
    dtg_ref[...] = u[:, U_DTG:U_TOTAL]


def _rope_tables(pos):
    half = ROT_DIM // 2
    inv = ROPE_THETA ** (-jnp.arange(half, dtype=F32) / half)
    ang = pos.astype(F32)[:, None] * inv
    cos, sin = jnp.cos(ang), jnp.sin(ang)
    ones = jnp.ones((pos.shape[0], HEAD_DIM - ROT_DIM), F32)
    zeros = jnp.zeros((pos.shape[0], HEAD_DIM - ROT_DIM), F32)
    zh = jnp.zeros_like(sin)
    c = jnp.concatenate([cos, cos, ones], 1)
    sa = jnp.concatenate([-sin, zh, zeros], 1)
    sb = jnp.concatenate([zh, sin, zeros], 1)
    return jnp.concatenate([jnp.tile(t, (1, LANES // HEAD_DIM)) for t in (c, sa, sb)], 1)


def _permute_w_in(w):
    sizes = (SSD_D, SSD_CONV_CH, SSD_HEADS, NSA_D, KV_D, KV_D, KV_D, KV_D, KV_D, KV_D, 3 * NSA_HEADS)
    offs = np.concatenate([[0], np.cumsum(sizes)])
    seg = [w[:, offs[i]:offs[i + 1]] for i in range(len(sizes))]
    pad = jnp.zeros((w.shape[0], LANES - SSD_HEADS - 3 * NSA_HEADS), w.dtype)
    out = jnp.concatenate([seg[0], seg[1], seg[3], seg[4], seg[5], seg[6], seg[7], seg[8], seg[9],
                           seg[2], seg[10], pad], 1)
    return out.astype(BF16)


def _rope_tables_t(pos):
    half = ROT_DIM // 2
    inv = ROPE_THETA ** (-jnp.arange(half, dtype=F32) / half)
    ang = inv[:, None] * pos.astype(F32)[None, :]
    return jnp.concatenate([jnp.cos(ang), jnp.sin(ang)], 0)


def _inproj(x, ln_g, ln_b, w_perm, rope_tab, tm, rope_tab_t=None):
    n = x.shape[0]
    nt = n // tm
    t = rope_tab.shape[0]
    n_rope_blocks = t // tm
    feature_major = rope_tab_t is not None
    row = lambda w: pl.BlockSpec((tm, w), lambda i: (i, 0))
    const = lambda a: pl.BlockSpec(a.shape, lambda i: (0,) * a.ndim)
    widths = (D_MODEL, SSD_D, SSD_CONV_CH, NSA_D, NSA_D, 2 * KV_D, 2 * KV_D, 2 * KV_D, LANES)
    in_specs = [row(D_MODEL), const(ln_g), const(ln_b), const(w_perm),
                pl.BlockSpec((tm, 3 * LANES), lambda i: (i % n_rope_blocks, 0))]
    out_specs = [row(w) for w in widths]
    dtypes = [BF16 if (feature_major and k in (3, 4)) else F32 for k in range(len(widths))]
    out_shape = [jax.ShapeDtypeStruct((n, w), d) for w, d in zip(widths, dtypes)]
    args = [x, ln_g, ln_b, w_perm, rope_tab]
    if feature_major:
        wkv_t = w_perm[:, U_KVC:U_DTG].T
        in_specs += [const(wkv_t), pl.BlockSpec((rope_tab_t.shape[0], tm), lambda i: (0, i % n_rope_blocks))]
        args += [wkv_t, rope_tab_t]
        out_specs += [pl.BlockSpec((1, 2 * KV_D, tm), lambda i: (i // n_rope_blocks, 0, i % n_rope_blocks))] * 3
        out_shape += [jax.ShapeDtypeStruct((n // t, 2 * KV_D, t), F32)] * 3
    return pl.pallas_call(
        functools.partial(_inproj_kernel, feature_major=feature_major),
        grid=(nt,),
        in_specs=in_specs,
        out_specs=out_specs,
        out_shape=out_shape,
        compiler_params=_cparams(("parallel",)),
        name="inproj",
    )(*args)


def _softplus(x):
    return jnp.maximum(x, 0.0) + jnp.log1p(jnp.exp(-jnp.abs(x)))


def _gated_group_norm(y, z, norm_w):
    y = y * _silu(z)
    gw = SSD_D // SSD_GROUPS
    parts = []
    for g in range(SSD_GROUPS):
        yg = y[:, g * gw:(g + 1) * gw]
        ms = jnp.mean(yg * yg, axis=-1, keepdims=True)
        parts.append(yg * lax.rsqrt(ms + RMS_EPS))
    return jnp.concatenate(parts, axis=1) * norm_w


SSD_CHUNKS_PER_STEP = 2


def _ssd_prompt_kernel(xbc_ref, z_ref, dtg_ref, convw_ref, convb_ref, dtb_ref, alog_ref, dskip_ref, normw_ref,
                       y_ref, state_ref, conv_ref, ext_ref, s_ref):
    c = pl.program_id(1)
    nc = pl.num_programs(1)
    L = SSD_CHUNK

    @pl.when(c == 0)
    def _():
        ext_ref[0:SUBLANES, :] = jnp.zeros((SUBLANES, SSD_CONV_CH), F32)
        s_ref[...] = jnp.zeros_like(s_ref)

    for sub in range(SSD_CHUNKS_PER_STEP):
        rows = pl.ds(sub * L, L)
        xin = _ssd_chunk(xbc_ref.at[rows], z_ref.at[rows], dtg_ref.at[rows], convw_ref, convb_ref, dtb_ref, alog_ref,
                         dskip_ref, normw_ref, y_ref.at[rows], ext_ref, s_ref)

    @pl.when(c == nc - 1)
    def _():
        state_ref[0] = s_ref[...]
        conv_ref[0] = xin[L - (SSD_CONV - 1):L, :]


def _ssd_chunk(xbc_ref, z_ref, dtg_ref, convw_ref, convb_ref, dtb_ref, alog_ref, dskip_ref, normw_ref, y_ref, ext_ref,
               s_ref):
    L = SSD_CHUNK
    halo = SUBLANES
    xin = xbc_ref[...]
    ext_ref[halo:halo + L, :] = xin
    xc = convw_ref[SSD_CONV - 1:SSD_CONV, :] * xin
    for k in range(SSD_CONV - 1):
        off = halo - (SSD_CONV - 1) + k
        xc = xc + convw_ref[k:k + 1, :] * ext_ref[off:off + L, :]
    ext_ref[0:halo, :] = ext_ref[L:L + halo, :]
    xc = _silu(xc + convb_ref[...])
    xs = xc[:, 0:SSD_D]
    ns = SSD_GROUPS * SSD_STATE
    bm = xc[:, SSD_D:SSD_D + ns]
    cm = xc[:, SSD_D + ns:SSD_D + 2 * ns]

    dt = _softplus(dtg_ref[...] + dtb_ref[...])
    da = dt * (-jnp.exp(alog_ref[...]))
    row = lax.broadcasted_iota(jnp.int32, (L, L), 0)
    col = lax.broadcasted_iota(jnp.int32, (L, L), 1)
    tril = row >= col
    acum = _hdot(tril.astype(F32), da)
    acum_t = acum.T
    eacum = jnp.exp(acum)
    alast = acum[L - 1:L, :]
    edecay = jnp.exp(alast - acum)
    elast = jnp.exp(alast)

    dt_full = jnp.concatenate([jnp.broadcast_to(dt[:, h:h + 1], (L, HEAD_DIM)) for h in range(SSD_HEADS)], 1)
    dec_full = jnp.concatenate([jnp.broadcast_to(edecay[:, h:h + 1], (L, HEAD_DIM)) for h in range(SSD_HEADS)], 1)
    xdt = xs * dt_full
    xdec_t = (xdt * dec_full).T

    hpg = SSD_HEADS // SSD_GROUPS
    y_parts = []
    for h in range(SSD_HEADS):
        g = h // hpg
        b_g = bm[:, g * SSD_STATE:(g + 1) * SSD_STATE]
        c_g = cm[:, g * SSD_STATE:(g + 1) * SSD_STATE]
        if h % hpg == 0:
            cb = _bdot_nt(c_g, b_g)
        seg = acum[:, h:h + 1] - acum_t[h:h + 1, :]
        lmat = jnp.where(tril, jnp.exp(jnp.where(tril, seg, 0.0)), 0.0)
        xdt_h = xdt[:, h * HEAD_DIM:(h + 1) * HEAD_DIM]
        y_h = _bdot(cb * lmat, xdt_h)
        s_prev = s_ref[h]
        y_h = y_h + _bdot_nt(c_g, s_prev) * eacum[:, h:h + 1]
        y_h = y_h + dskip_ref[:, h * HEAD_DIM:(h + 1) * HEAD_DIM] * xs[:, h * HEAD_DIM:(h + 1) * HEAD_DIM]
        y_parts.append(y_h)
        s_ref[h] = elast[:, h:h + 1] * s_prev + _bdot(xdec_t[h * HEAD_DIM:(h + 1) * HEAD_DIM, :], b_g)
    y = jnp.concatenate(y_parts, axis=1)
    y_ref[...] = _gated_group_norm(y, z_ref[...], normw_ref[...]).astype(y_ref.dtype)
    return xin


def _ssd_prompt(xbc, z, dtg, conv_w, conv_b, dt_bias_pad, a_log_pad, d_skip_full, norm_w, bn, t):
    step_rows = SSD_CHUNK * SSD_CHUNKS_PER_STEP
    nc = t // step_rows
    row = lambda w: pl.BlockSpec((step_rows, w), lambda b, c: (b * nc + c, 0))
    const = lambda a: pl.BlockSpec(a.shape, lambda b, c: (0,) * a.ndim)
    return pl.pallas_call(
        _ssd_prompt_kernel,
        grid=(bn, nc),
        in_specs=[row(SSD_CONV_CH), row(SSD_D), row(LANES), const(conv_w), const(conv_b), const(dt_bias_pad),
                  const(a_log_pad), const(d_skip_full), const(norm_w)],
        out_specs=[row(SSD_D),
                   pl.BlockSpec((1, SSD_HEADS, HEAD_DIM, SSD_STATE), lambda b, c: (b, 0, 0, 0)),
                   pl.BlockSpec((1, SSD_CONV - 1, SSD_CONV_CH), lambda b, c: (b, 0, 0))],
        out_shape=[jax.ShapeDtypeStruct((bn * t, SSD_D), BF16),
                   jax.ShapeDtypeStruct((bn, SSD_HEADS, HEAD_DIM, SSD_STATE), F32),
                   jax.ShapeDtypeStruct((bn, SSD_CONV - 1, SSD_CONV_CH), F32)],
        scratch_shapes=[pltpu.VMEM((SSD_CHUNK + 2 * SUBLANES, SSD_CONV_CH), F32),
                        pltpu.VMEM((SSD_HEADS, HEAD_DIM, SSD_STATE), F32)],
        compiler_params=_cparams(("parallel", "arbitrary")),
        name="ssd_prompt",
    )(xbc, z, dtg, conv_w, conv_b, dt_bias_pad, a_log_pad, d_skip_full, norm_w)


def _pad_lanes(v, fill=0.0):
    return jnp.concatenate([v.astype(F32), jnp.full((LANES - v.shape[0],), fill, F32)])[None]


def _compress_rows(k_ref, v_ref, pe_ref, w1k_ref, w1v_ref, b1_ref, w2k_ref, w2v_ref, b2_ref, nb):
    acck = jnp.zeros((nb, 2 * CMP_HIDDEN), F32)
    accv = jnp.zeros((nb, 2 * CMP_HIDDEN), F32)
    for l in range(CMP_BLOCK):
        xk = k_ref[pl.ds(l, nb, stride=CMP_BLOCK), :] + pe_ref[l:l + 1, 0:KV_D]
        xv = v_ref[pl.ds(l, nb, stride=CMP_BLOCK), :] + pe_ref[l:l + 1, KV_D:2 * KV_D]
        acck = acck + jnp.dot(xk.astype(BF16), w1k_ref[l], preferred_element_type=F32)
        accv = accv + jnp.dot(xv.astype(BF16), w1v_ref[l], preferred_element_type=F32)
    hk = _silu(acck + b1_ref[:, 0:2 * CMP_HIDDEN])
    hv = _silu(accv + b1_ref[:, 2 * CMP_HIDDEN:4 * CMP_HIDDEN])
    ok = jnp.dot(hk.astype(BF16), w2k_ref[...], preferred_element_type=F32) + b2_ref[:, 0:KV_D]
    ov = jnp.dot(hv.astype(BF16), w2v_ref[...], preferred_element_type=F32) + b2_ref[:, KV_D:2 * KV_D]
    return jnp.concatenate([ok, ov], axis=1)


def _compress_kernel(k_ref, v_ref, pe_ref, w1k_ref, w1v_ref, b1_ref, w2k_ref, w2v_ref, b2_ref, o_ref, *, nb):
    o_ref[...] = _compress_rows(k_ref, v_ref, pe_ref, w1k_ref, w1v_ref, b1_ref, w2k_ref, w2v_ref, b2_ref, nb)


def _block_diag2(w):
    z = jnp.zeros_like(w)
    return jnp.concatenate([jnp.concatenate([w, z], -1), jnp.concatenate([z, w], -1)], -2)


def _compress_consts(cmp_pe, cmp_w1, cmp_b1, cmp_w2, cmp_b2):
    pe = jnp.concatenate([cmp_pe[0], cmp_pe[0], cmp_pe[1], cmp_pe[1]], -1)
    w1k = _block_diag2(cmp_w1[0]).astype(BF16)
    w1v = _block_diag2(cmp_w1[1]).astype(BF16)
    b1 = jnp.concatenate([cmp_b1[0], cmp_b1[0], cmp_b1[1], cmp_b1[1]])[None]
    w2k = _block_diag2(cmp_w2[0]).astype(BF16)
    w2v = _block_diag2(cmp_w2[1]).astype(BF16)
    b2 = jnp.concatenate([cmp_b2[0], cmp_b2[0], cmp_b2[1], cmp_b2[1]])[None]
    return pe, w1k, w1v, b1, w2k, w2v, b2


def _compress_prompt(kvc, consts, rows_per_step):
    n = kvc.shape[0]
    nb = rows_per_step // CMP_BLOCK
    const = lambda a: pl.BlockSpec(a.shape, lambda i: (0,) * a.ndim)
    return pl.pallas_call(
        functools.partial(_compress_kernel, nb=nb),
        grid=(n // rows_per_step,),
        in_specs=[pl.BlockSpec((rows_per_step, KV_D), lambda i: (i, 0)),
                  pl.BlockSpec((rows_per_step, KV_D), lambda i: (i, 1))] + [const(a) for a in consts],
        out_specs=pl.BlockSpec((nb, 2 * KV_D), lambda i: (i, 0)),
        out_shape=jax.ShapeDtypeStruct((n // CMP_BLOCK, 2 * KV_D), F32),
        compiler_params=_cparams(("parallel",)),
        name="compress_prompt",
    )(kvc, kvc, *consts)


SEL_KEY_TILE = 512
WIN_KEYS = WINDOW + Q_BLOCK


def _dup_head(x, hk):
    sw = pltpu.roll(x, HEAD_DIM, 1)
    low = lax.broadcasted_iota(jnp.int32, x.shape, 1) < HEAD_DIM
    return jnp.where(low, x, sw) if hk == 0 else jnp.where(low, sw, x)


def _masked_softmax(s, mask):
    sm = jnp.where(mask, s, NEG)
    ex = jnp.where(mask, jnp.exp(sm - jnp.max(sm, axis=-1, keepdims=True)), 0.0)
    den = jnp.sum(ex, axis=-1, keepdims=True)
    return ex / jnp.where(den > 0.0, den, 1.0)


def _select_blocks_t(imp, cur, n_top):
    j = lax.broadcasted_iota(jnp.int32, imp.shape, 0)
    future = j > cur
    forced = (j == 0) | (j == cur) | (j == cur - 1)
    score = jnp.where(future, NEG, jnp.where(forced, FORCED_SCORE, imp))
    return ((_rank_rows(score) < n_top) & (score > 0.5 * NEG)).astype(F32)


def _nsa_prompt_kernel(qc_ref, qr_ref, dtg_ref, cmp_ref, kvs_ref, kvw_ref, o_ref,
                       cmp_d, kvs_d, kvw_d, expand_ref, bias_ref, qrs_ref, m_ref, l_ref, acc_ref, *, t):
    qb = pl.program_id(1)
    nbk = t // SEL_BLOCK
    tq = Q_BLOCK
    tk = SEL_KEY_TILE
    hpg = NSA_HEADS // NSA_KV_HEADS
    scale = HEAD_DIM ** -0.5

    @pl.when(qb == 0)
    def _():
        expand_ref[...] = (lax.broadcasted_iota(jnp.int32, (LANES, t), 1) // SEL_BLOCK
                           == lax.broadcasted_iota(jnp.int32, (LANES, t), 0)).astype(BF16)
        cmp_d[...] = jnp.zeros_like(cmp_d)
        for src, dst, n in ((cmp_ref, cmp_d, nbk), (kvs_ref, kvs_d, t), (kvw_ref, kvw_d, t)):
            x = src[...]
            for hk in range(NSA_KV_HEADS):
                dst[hk, 0:n, 0:KV_D] = _dup_head(x[:, 0:KV_D], hk).astype(BF16)
                dst[hk, 0:n, KV_D:2 * KV_D] = _dup_head(x[:, KV_D:2 * KV_D], hk).astype(BF16)

    t0 = qb * tq
    rows = t0 + lax.broadcasted_iota(jnp.int32, (tq, 1), 0)
    lane = lax.broadcasted_iota(jnp.int32, (tq, LANES), 1)
    half_mask = (lane < HEAD_DIM, lane >= HEAD_DIM)
    sig = _sigmoid(dtg_ref[...])
    vis = (lane + 1) * CMP_BLOCK - 1 <= rows
    cur_l = (t0 + lax.broadcasted_iota(jnp.int32, (1, tq), 1)) // SEL_BLOCK
    win_start = pl.multiple_of(jnp.maximum(t0 - WINDOW, 0), tq)
    wpos = win_start + lax.broadcasted_iota(jnp.int32, (tq, WIN_KEYS), 1)
    win_bias = jnp.where((wpos <= rows) & (wpos >= rows - WINDOW), 0.0, NEG)
    n_kt = (t0 + tq + tk - 1) // tk

    def stack_heads(ref, hk):
        parts = []
        for hh in range(hpg):
            head = hk * hpg + hh
            p, e = head // 2, head % 2
            parts.append(jnp.where(half_mask[e], ref[:, p * LANES:(p + 1) * LANES] * scale, 0.0))
        return jnp.concatenate(parts, axis=0).astype(BF16)

    o_cmp_g = []
    for hk in range(NSA_KV_HEADS):
        qcs = stack_heads(qc_ref, hk)
        s = lax.dot_general(qcs, cmp_d[hk, :, 0:KV_D], (((1,), (1,)), ((), ())), preferred_element_type=F32)
        pc = _masked_softmax(s.reshape(hpg, tq, LANES), vis[None])
        imp = jnp.sum(pc, axis=0)
        o_cmp_g.append(jnp.dot(pc.reshape(hpg * tq, LANES).astype(BF16), cmp_d[hk, :, KV_D:2 * KV_D],
                               preferred_element_type=F32).reshape(hpg, tq, LANES))

        sel_t = _select_blocks_t(imp.T[0:nbk, :], cur_l, TOP_N)
        sel = jnp.concatenate([sel_t, jnp.zeros((LANES - nbk, tq), F32)], axis=0).T
        selk = jnp.dot(sel.astype(BF16), expand_ref[...], preferred_element_type=F32)
        for kt in range(t // tk):
            @pl.when(kt < n_kt)
            def _(kt=kt, hk=hk, selk=selk):
                kpos = kt * tk + lax.broadcasted_iota(jnp.int32, (tq, tk), 1)
                bias_ref[hk, kt] = jnp.where((selk[:, kt * tk:(kt + 1) * tk] > 0.5) & (kpos <= rows), 0.0, NEG)

        qrs_ref[hk] = stack_heads(qr_ref, hk)

    m_ref[...] = jnp.full(m_ref.shape, NEG, F32)
    l_ref[...] = jnp.zeros(l_ref.shape, F32)
    acc_ref[...] = jnp.zeros(acc_ref.shape, F32)

    def sel_step(kt, carry):
        k0 = pl.multiple_of(kt * tk, tk)
        for hk in range(NSA_KV_HEADS):
            kblk = kvs_d[hk, pl.ds(k0, tk), 0:KV_D]
            vblk = kvs_d[hk, pl.ds(k0, tk), KV_D:2 * KV_D]
            s = lax.dot_general(qrs_ref[hk], kblk, (((1,), (1,)), ((), ())), preferred_element_type=F32)
            s = s.reshape(hpg, tq, tk) + bias_ref[hk, kt][None]
            m_old = m_ref[hk]
            m_new = jnp.maximum(m_old, jnp.max(s, axis=-1, keepdims=True))
            alpha = jnp.exp(m_old - m_new)
            pe = jnp.exp(s - jnp.concatenate([m_new] * (tk // LANES), axis=-1))
            l_ref[hk] = alpha * l_ref[hk] + jnp.sum(pe, axis=-1, keepdims=True)
            pv = jnp.dot(pe.reshape(hpg * tq, tk).astype(BF16), vblk, preferred_element_type=F32)
            acc_ref[hk] = alpha * acc_ref[hk] + pv.reshape(hpg, tq, LANES)
            m_ref[hk] = m_new
        return carry

    lax.fori_loop(0, n_kt, sel_step, 0)

    for hk in range(NSA_KV_HEADS):
        o_cmp = o_cmp_g[hk]
        o_slc = acc_ref[hk] / l_ref[hk]
        kw = kvw_d[hk, pl.ds(win_start, WIN_KEYS), 0:KV_D]
        vw = kvw_d[hk, pl.ds(win_start, WIN_KEYS), KV_D:2 * KV_D]
        sw = lax.dot_general(qrs_ref[hk], kw, (((1,), (1,)), ((), ())), preferred_element_type=F32)
        sw = sw.reshape(hpg, tq, WIN_KEYS) + win_bias[None]
        pw = jnp.exp(sw - jnp.max(sw, axis=-1, keepdims=True))
        den = jnp.sum(pw, axis=-1, keepdims=True)
        o_win = jnp.dot(pw.reshape(hpg * tq, WIN_KEYS).astype(BF16), vw,
                        preferred_element_type=F32).reshape(hpg, tq, LANES) / den

        for hh in range(hpg):
            head = hk * hpg + hh
            p, e = head // 2, head % 2
            c0 = GATE_COL0 + head * 3
            mix = (sig[:, c0:c0 + 1] * o_cmp[hh] + sig[:, c0 + 1:c0 + 2] * o_slc[hh]
                   + sig[:, c0 + 2:c0 + 3] * o_win[hh])
            if e == 0:
                mix_even = mix
            else:
                o_ref[:, p * LANES:(p + 1) * LANES] = jnp.where(half_mask[0], mix_even, mix).astype(o_ref.dtype)


def _nsa_prompt(qc, qr, dtg, kvcmp, kvs, kvw, bn, t):
    nq = t // Q_BLOCK
    nbk = t // SEL_BLOCK
    hpg = NSA_HEADS // NSA_KV_HEADS
    assert nbk >= TOP_N and t >= WIN_KEYS and t % SEL_KEY_TILE == 0
    qrow = lambda w: pl.BlockSpec((Q_BLOCK, w), lambda b, i: (b * nq + i, 0))
    seq = lambda r: pl.BlockSpec((r, 2 * KV_D), lambda b, i: (b, 0))
    return pl.pallas_call(
        functools.partial(_nsa_prompt_kernel, t=t),
        grid=(bn, nq),
        in_specs=[qrow(NSA_D), qrow(NSA_D), qrow(LANES), seq(nbk), seq(t), seq(t)],
        out_specs=qrow(NSA_D),
        out_shape=jax.ShapeDtypeStruct((bn * t, NSA_D), BF16),
        scratch_shapes=[pltpu.VMEM((NSA_KV_HEADS, LANES, 2 * KV_D), BF16),
                        pltpu.VMEM((NSA_KV_HEADS, t, 2 * KV_D), BF16),
                        pltpu.VMEM((NSA_KV_HEADS, t, 2 * KV_D), BF16),
                        pltpu.VMEM((LANES, t), BF16),
                        pltpu.VMEM((NSA_KV_HEADS, t // SEL_KEY_TILE, Q_BLOCK, SEL_KEY_TILE), F32),
                        pltpu.VMEM((NSA_KV_HEADS, hpg * Q_BLOCK, LANES), BF16),
                        pltpu.VMEM((NSA_KV_HEADS, hpg, Q_BLOCK, LANES), F32),
                        pltpu.VMEM((NSA_KV_HEADS, hpg, Q_BLOCK, LANES), F32),
                        pltpu.VMEM((NSA_KV_HEADS, hpg, Q_BLOCK, LANES), F32)],
        compiler_params=_cparams(("parallel", "arbitrary")),
        name="nsa_prompt",
    )(qc, qr, dtg, kvcmp, kvs, kvw)


def _outproj_kernel(ys_ref, yn_ref, h_ref, ws_ref, wn_ref, g_ref, b_ref, o_ref):
    mix = jnp.dot(ys_ref[...].astype(BF16), ws_ref[...], preferred_element_type=F32)
    mix = mix + jnp.dot(yn_ref[...].astype(BF16), wn_ref[...], preferred_element_type=F32)
    o_ref[...] = _layer_norm(DEEPNORM_ALPHA * h_ref[...] + mix, g_ref[...], b_ref[...])


def _outproj(y_ssd, y_nsa, h, w_ssd, w_nsa, ln_g, ln_b, tm):
    n = h.shape[0]
    row = lambda w: pl.BlockSpec((tm, w), lambda i: (i, 0))
    const = lambda a: pl.BlockSpec(a.shape, lambda i: (0,) * a.ndim)
    return pl.pallas_call(
        _outproj_kernel,
        grid=(n // tm,),
        in_specs=[row(SSD_D), row(NSA_D), row(D_MODEL), const(w_ssd), const(w_nsa), const(ln_g), const(ln_b)],
        out_specs=row(D_MODEL),
        out_shape=jax.ShapeDtypeStruct((n, D_MODEL), F32),
        compiler_params=_cparams(("parallel",)),
        name="outproj",
    )(y_ssd, y_nsa, h, w_ssd, w_nsa, ln_g, ln_b)


MOE_TOKENS = 256
ROUTE_ROWS = 8


def _token_tile_specs(n_main_tiles):
    main = pl.BlockSpec((MOE_TOKENS, D_MODEL), lambda i, *_: (jnp.minimum(i, n_main_tiles - 1), 0))
    tail = pl.BlockSpec((MOE_TOKENS, D_MODEL), lambda i, *_: (0, 0))
    return main, tail


def _token_tile(i, n_main_tiles, main_ref, tail_ref):
    return jnp.where(i < n_main_tiles, main_ref[...], tail_ref[...])


def _rank_rows(x):
    n = x.shape[0]
    idx = lax.broadcasted_iota(jnp.int32, x.shape, 0)
    rank = jnp.zeros(x.shape, F32)
    for r in range(n):
        row = x[r:r + 1, :]
        rank = rank + ((row > x) | ((row == x) & (idx > r))).astype(F32)
    return rank


def _route_kernel(h_ref, ht_ref, rw_ref, rb_ref, slot_ref, tokinfo_ref, meta_ref, cnt_ref, carry_ref, carry_row_ref, *,
                  n_valid, n_main):
    i = pl.program_id(0)
    tm = MOE_TOKENS

    @pl.when(i == 0)
    def _():
        carry_ref[...] = jnp.zeros_like(carry_ref)
        carry_row_ref[...] = jnp.zeros_like(carry_row_ref)

    logits = lax.dot_general(rw_ref[...], _token_tile(i, n_main, h_ref, ht_ref).astype(BF16), (((1,), (1,)), ((), ())),
                             preferred_element_type=F32)
    scores = _sigmoid(logits)
    biased = scores + rb_ref[:, 0:1]
    b3 = biased.reshape(N_EXPERT_GROUPS, EXPERTS_PER_GROUP, tm)
    sidx = lax.broadcasted_iota(jnp.int32, b3.shape, 1)
    m1 = jnp.max(b3, axis=1, keepdims=True)
    first = jnp.min(jnp.where(b3 == m1, sidx, EXPERTS_PER_GROUP), axis=1, keepdims=True)
    m2 = jnp.max(jnp.where(sidx == first, -jnp.inf, b3), axis=1, keepdims=True)
    grp_score = (m1 + m2).reshape(N_EXPERT_GROUPS, tm)
    grp_keep = _rank_rows(grp_score) < TOPK_GROUPS
    masked = jnp.where(grp_keep.reshape(N_EXPERT_GROUPS, 1, tm), b3, NEG).reshape(N_EXPERTS, tm)
    rank = _rank_rows(masked)
    tok = i * tm + lax.broadcasted_iota(jnp.int32, (1, tm), 1)
    valid = tok < n_valid
    sel = (rank < TOP_K) & valid
    self32 = sel.astype(F32)
    wsel = self32 * scores
    wsum = jnp.sum(wsel, axis=0, keepdims=True)
    w = wsel / jnp.where(wsum > 0.0, wsum, 1.0) * ROUTED_SCALE

    selb = sel.astype(BF16)
    tri = lambda n, strict_upper: (
        (lax.broadcasted_iota(jnp.int32, (n, n), 0) < lax.broadcasted_iota(jnp.int32, (n, n), 1))
        if strict_upper else
        (lax.broadcasted_iota(jnp.int32, (n, n), 0) > lax.broadcasted_iota(jnp.int32, (n, n), 1))).astype(BF16)
    pad8 = lambda c: jnp.floor((c + (SUBLANES - 1.0)) * (1.0 / SUBLANES)) * SUBLANES
    pos_tile = jnp.dot(selb, tri(tm, True), preferred_element_type=F32)
    cnt_col = pad8(jnp.sum(self32, axis=1, keepdims=True))
    first_col = jnp.dot(tri(N_EXPERTS, False), jnp.broadcast_to(cnt_col, (N_EXPERTS, LANES)).astype(BF16),
                        preferred_element_type=F32)[:, 0:1]
    slot = first_col + pos_tile

    sel_pad = jnp.concatenate([selb, jnp.zeros((LANES - N_EXPERTS, tm), BF16)], axis=0)
    cnt_row = pad8(lax.dot_general(jnp.ones((SUBLANES, tm), BF16), sel_pad, (((1,), (1,)), ((), ())),
                                   preferred_element_type=F32))
    first_row = jnp.dot(cnt_row.astype(BF16), tri(LANES, True), preferred_element_type=F32)
    prev_row = carry_row_ref[...]
    meta = jnp.concatenate([cnt_row[0:1], first_row[0:1], prev_row[0:1], jnp.zeros((SUBLANES - 3, LANES), F32)], 0)
    meta_ref[0] = meta.astype(jnp.int32)
    carry_row_ref[...] = prev_row + cnt_row
    carry_ref[...] = carry_ref[...] + cnt_col

    slot_rows, w_rows = [], []
    for k in range(TOP_K):
        hit = (rank == k) & sel
        slot_rows.append(jnp.sum(jnp.where(hit, slot, 0.0), axis=0, keepdims=True))
        w_rows.append(jnp.sum(jnp.where(hit, w, 0.0), axis=0, keepdims=True))
    slot_rows = [jnp.where(valid, r, -1.0) for r in slot_rows]
    pad2 = jnp.zeros((ROUTE_ROWS - TOP_K, tm), F32)
    slot_ref[...] = jnp.concatenate(slot_rows + [pad2 - 1.0], 0).astype(jnp.int32)
    info = jnp.concatenate(w_rows + [pad2] + slot_rows + [jnp.zeros((LANES - ROUTE_ROWS - TOP_K, tm), F32)], 0)
    tokinfo_ref[...] = info.T

    @pl.when(i == pl.num_programs(0) - 1)
    def _():
        cnt_ref[...] = jnp.broadcast_to(carry_ref[:, 0:1], cnt_ref.shape)


def _route(h_main, h_tail, router_wt, router_bias_col, n_valid):
    tm = MOE_TOKENS
    n_main = h_main.shape[0] // tm
    n = h_main.shape[0] + tm
    const = lambda a: pl.BlockSpec(a.shape, lambda i: (0,) * a.ndim)
    return pl.pallas_call(
        functools.partial(_route_kernel, n_valid=n_valid, n_main=n_main),
        grid=(n // tm,),
        in_specs=[*_token_tile_specs(n_main), const(router_wt), const(router_bias_col)],
        out_specs=[pl.BlockSpec((ROUTE_ROWS, tm), lambda i: (0, i)),
                   pl.BlockSpec((tm, LANES), lambda i: (i, 0)),
                   pl.BlockSpec((1, SUBLANES, LANES), lambda i: (i, 0, 0)),
                   pl.BlockSpec((N_EXPERTS, LANES), lambda i: (0, 0))],
        out_shape=[jax.ShapeDtypeStruct((ROUTE_ROWS, n), jnp.int32),
                   jax.ShapeDtypeStruct((n, LANES), F32),
                   jax.ShapeDtypeStruct((n // tm, SUBLANES, LANES), jnp.int32),
                   jax.ShapeDtypeStruct((N_EXPERTS, LANES), F32)],
        scratch_shapes=[pltpu.VMEM((N_EXPERTS, LANES), F32), pltpu.VMEM((SUBLANES, LANES), F32)],
        compiler_params=_cparams(("arbitrary",)),
        name="moe_route",
    )(h_main, h_tail, router_wt, router_bias_col)


PACKED_D = D_MODEL // 2
U32 = jnp.uint32


def _pack_bf16_pairs(x):
    hi = lax.bitcast_convert_type(x[:, 0:PACKED_D], U32) & jnp.uint32(0xFFFF0000)
    lo = lax.shift_right_logical(lax.bitcast_convert_type(x[:, PACKED_D:D_MODEL], U32), jnp.uint32(16))
    return hi | lo


def _unpack_bf16_pairs(w):
    hi = lax.bitcast_convert_type(w & jnp.uint32(0xFFFF0000), F32)
    lo = lax.bitcast_convert_type(lax.shift_left(w, jnp.uint32(16)), F32)
    return jnp.concatenate([hi, lo], axis=1).astype(BF16)


def _round_bf16(x):
    return x.astype(BF16).astype(F32)


TILE_SLOTS = MOE_TOKENS * TOP_K + N_EXPERTS * SUBLANES
RUN_CHUNKS = tuple(1 << b for b in range(int(math.log2(MOE_TOKENS)), int(math.log2(SUBLANES)) - 1, -1))


def _run_copy(src_ref, src_row, dst_ref, dst_row, rows, sem):
    return pltpu.make_async_copy(src_ref.at[pl.ds(pl.multiple_of(src_row, SUBLANES), rows)],
                                 dst_ref.at[pl.ds(pl.multiple_of(dst_row, SUBLANES), rows)], sem)


def _start_run(src_ref, src_row, dst_ref, dst_row, n, sem, started):
    off = jnp.int32(0)
    out = []
    for c, rows in enumerate(RUN_CHUNKS):
        take = (n & rows) != 0

        @pl.when(take)
        def _(off=off, rows=rows):
            _run_copy(src_ref, src_row + off, dst_ref, dst_row + off, rows, sem).start()

        inc = take.astype(jnp.int32)
        off = off + inc * rows
        out.append(started[c] + inc)
    return tuple(out)


def _wait_runs(src_ref, dst_ref, sem, started):
    for c, rows in enumerate(RUN_CHUNKS):
        def wait_one(j, carry, rows=rows):
            _run_copy(src_ref, 0, dst_ref, 0, rows, sem).wait()
            return carry

        lax.fori_loop(0, started[c], wait_one, 0)


def _dispatch_kernel(start_ref, cnt_ref, meta_ref, slot_ref, x_ref, xt_ref, xs_ref, sorted_ref, zero_ref, sem, zsem, *,
                     cap, n_main):
    i = pl.program_id(0)
    tm = MOE_TOKENS

    @pl.when(i == 0)
    def _():
        zero_ref[...] = jnp.zeros_like(zero_ref)

        def fill_expert(e, started):
            lo = start_ref[e] + cnt_ref[e]
            hi = jnp.where(e == N_EXPERTS - 1, cap, start_ref[jnp.minimum(e + 1, N_EXPERTS - 1)])
            n_full = (hi - lo) // tm

            def fill_full(j, st):
                return _start_run(zero_ref, 0, xs_ref, lo + j * tm, jnp.int32(tm), zsem, st)

            started = lax.fori_loop(0, n_full, fill_full, started)
            return _start_run(zero_ref, 0, xs_ref, lo + n_full * tm, (hi - lo) - n_full * tm, zsem, started)

        filled = lax.fori_loop(0, N_EXPERTS, fill_expert, tuple(jnp.int32(0) for _ in RUN_CHUNKS))
        _wait_runs(zero_ref, xs_ref, zsem, filled)

    srow = lax.broadcasted_iota(jnp.int32, (TILE_SLOTS, tm), 0)
    onehot = srow == slot_ref[0:1, :]
    for k in range(1, TOP_K):
        onehot = onehot | (srow == slot_ref[k:k + 1, :])
    sorted_ref[...] = _pack_bf16_pairs(jnp.dot(onehot.astype(BF16), _token_tile(i, n_main, x_ref, xt_ref).astype(BF16),
                                               preferred_element_type=F32))

    def copy_expert(e, started):
        n = meta_ref[0, 0, e]
        return _start_run(sorted_ref, meta_ref[0, 1, e], xs_ref, start_ref[e] + meta_ref[0, 2, e], n, sem, started)

    started = lax.fori_loop(0, N_EXPERTS, copy_expert, tuple(jnp.int32(0) for _ in RUN_CHUNKS))
    _wait_runs(sorted_ref, xs_ref, sem, started)


def _dispatch(h_main, h_tail, slot_t, meta, seg_start, counts, cap):
    tm = MOE_TOKENS
    n_main = h_main.shape[0] // tm
    return pl.pallas_call(
        functools.partial(_dispatch_kernel, cap=cap, n_main=n_main),
        grid_spec=pltpu.PrefetchScalarGridSpec(
            num_scalar_prefetch=2,
            grid=(n_main + 1,),
            in_specs=[pl.BlockSpec((1, SUBLANES, LANES), lambda i, *_: (i, 0, 0), memory_space=pltpu.SMEM),
                      pl.BlockSpec((ROUTE_ROWS, tm), lambda i, *_: (0, i)),
                      *_token_tile_specs(n_main)],
            out_specs=pl.BlockSpec(memory_space=pl.ANY),
            scratch_shapes=[pltpu.VMEM((TILE_SLOTS, PACKED_D), U32), pltpu.VMEM((tm, PACKED_D), U32),
                            pltpu.SemaphoreType.DMA, pltpu.SemaphoreType.DMA]),
        out_shape=jax.ShapeDtypeStruct((cap, PACKED_D), U32),
        compiler_params=_cparams(("arbitrary",)),
        name="moe_dispatch",
    )(seg_start, counts, meta, slot_t, h_main, h_tail)


def _swiglu(x, wg, wu, wd):
    xb = x.astype(BF16)
    g = jnp.dot(xb, wg.astype(BF16), preferred_element_type=F32)
    u = jnp.dot(xb, wu.astype(BF16), preferred_element_type=F32)
    return jnp.dot((_silu(g) * u).astype(BF16), wd.astype(BF16), preferred_element_type=F32)


EXPERT_RING = 3


def _experts_kernel(be_ref, used_ref, xs_ref, wg_ref, wu_ref, wd_ref, y_ref, xbuf, sems):
    i = pl.program_id(0)
    n = pl.num_programs(0)

    def block_copy(blk):
        slot = lax.rem(blk, EXPERT_RING)
        rows = pl.ds(pl.multiple_of(blk * MOE_BLOCK, MOE_BLOCK), MOE_BLOCK)
        return pltpu.make_async_copy(xs_ref.at[rows], xbuf.at[slot], sems.at[slot])

    @pl.when(i == 0)
    def _():
        for j in range(EXPERT_RING - 1):
            block_copy(jnp.int32(j)).start()

    @pl.when(i + EXPERT_RING - 1 < n)
    def _():
        block_copy(i + EXPERT_RING - 1).start()

    block_copy(i).wait()

    @pl.when(i < used_ref[0])
    def _():
        y = _swiglu(_unpack_bf16_pairs(xbuf[lax.rem(i, EXPERT_RING)]), wg_ref[0], wu_ref[0], wd_ref[0])
        y_ref[...] = _pack_bf16_pairs(_round_bf16(y))

    @pl.when(i >= used_ref[0])
    def _():
        y_ref[...] = jnp.zeros_like(y_ref)


def _experts(xs, block_expert, used_blocks, w_gate, w_up, w_down):
    cap = xs.shape[0]
    return pl.pallas_call(
        _experts_kernel,
        grid_spec=pltpu.PrefetchScalarGridSpec(
            num_scalar_prefetch=2,
            grid=(cap // MOE_BLOCK,),
            in_specs=[pl.BlockSpec(memory_space=pl.ANY),
                      pl.BlockSpec((1, D_MODEL, D_EXPERT), lambda i, be, used: (be[i], 0, 0)),
                      pl.BlockSpec((1, D_MODEL, D_EXPERT), lambda i, be, used: (be[i], 0, 0)),
                      pl.BlockSpec((1, D_EXPERT, D_MODEL), lambda i, be, used: (be[i], 0, 0))],
            out_specs=pl.BlockSpec((MOE_BLOCK, PACKED_D), lambda i, be, used: (i, 0)),
            scratch_shapes=[pltpu.VMEM((EXPERT_RING, MOE_BLOCK, PACKED_D), U32),
                            pltpu.SemaphoreType.DMA((EXPERT_RING,))]),
        out_shape=jax.ShapeDtypeStruct((cap, PACKED_D), U32),
        compiler_params=_cparams(("arbitrary",)),
        name="moe_experts",
    )(block_expert, used_blocks, xs, w_gate, w_up, w_down)


def _combine_kernel(start_ref, meta_ref, h_ref, ht_ref, info_ref, sg_ref, su_ref, sd_ref, g_ref, b_ref,
                    ys_ref, o_ref, ot_ref, buf_ref, sem, *, n_main):
    i = pl.program_id(0)
    tm = MOE_TOKENS

    @pl.when(i == 0)
    def _():
        buf_ref[...] = jnp.zeros_like(buf_ref)

    def fetch_expert(e, started):
        n = meta_ref[0, 0, e]
        return _start_run(ys_ref, start_ref[e] + meta_ref[0, 2, e], buf_ref, meta_ref[0, 1, e], n, sem, started)

    started = lax.fori_loop(0, N_EXPERTS, fetch_expert, tuple(jnp.int32(0) for _ in RUN_CHUNKS))
    h = _token_tile(i, n_main, h_ref, ht_ref)
    f = _swiglu(h, sg_ref[...], su_ref[...], sd_ref[...])
    info = info_ref[...]
    scol = lax.broadcasted_iota(jnp.int32, (tm, TILE_SLOTS), 1).astype(F32)
    mix = jnp.zeros((tm, TILE_SLOTS), F32)
    for k in range(TOP_K):
        mix = jnp.where(info[:, ROUTE_ROWS + k:ROUTE_ROWS + k + 1] == scol, info[:, k:k + 1], mix)
    _wait_runs(ys_ref, buf_ref, sem, started)
    acc = jnp.dot(mix.astype(BF16), _unpack_bf16_pairs(buf_ref[...]), preferred_element_type=F32)
    out = _layer_norm(DEEPNORM_ALPHA * h + (acc + f), g_ref[...], b_ref[...])

    @pl.when(i < n_main)
    def _():
        o_ref[...] = out

    @pl.when(i >= n_main)
    def _():
        ot_ref[...] = out


def _combine(h_main, h_tail, ys, meta, tokinfo, seg_start, sh_gate, sh_up, sh_down, ln_g, ln_b):
    tm = MOE_TOKENS
    n_main = h_main.shape[0] // tm
    const = lambda a: pl.BlockSpec(a.shape, lambda i, *_: (0,) * a.ndim)
    return pl.pallas_call(
        functools.partial(_combine_kernel, n_main=n_main),
        grid_spec=pltpu.PrefetchScalarGridSpec(
            num_scalar_prefetch=1,
            grid=(n_main + 1,),
            in_specs=[pl.BlockSpec((1, SUBLANES, LANES), lambda i, *_: (i, 0, 0), memory_space=pltpu.SMEM),
                      *_token_tile_specs(n_main),
                      pl.BlockSpec((tm, LANES), lambda i, *_: (i, 0)),
                      const(sh_gate), const(sh_up), const(sh_down), const(ln_g), const(ln_b),
                      pl.BlockSpec(memory_space=pl.ANY)],
            out_specs=list(_token_tile_specs(n_main)),
            scratch_shapes=[pltpu.VMEM((TILE_SLOTS, PACKED_D), U32), pltpu.SemaphoreType.DMA]),
        out_shape=[jax.ShapeDtypeStruct(h_main.shape, F32), jax.ShapeDtypeStruct((tm, D_MODEL), F32)],
        compiler_params=_cparams(("arbitrary",)),
        name="moe_combine",
    )(seg_start, meta, h_main, h_tail, tokinfo, sh_gate, sh_up, sh_down, ln_g, ln_b, ys)


def _moe_ln(h_main, h_tail, n_valid, router_w, router_bias, w_gate, w_up, w_down, sh_gate, sh_up, sh_down, ln_g, ln_b):
    n_tiles = h_main.shape[0] // MOE_TOKENS + 1
    slot_t, tokinfo, meta, cnt = _route(h_main, h_tail, router_w.T.astype(BF16),
                                        jnp.broadcast_to(router_bias.astype(F32)[:, None], (N_EXPERTS, LANES)), n_valid)
    counts = cnt[:, 0].astype(jnp.int32)
    padded = (counts + MOE_BLOCK - 1) // MOE_BLOCK * MOE_BLOCK
    seg_end = jnp.cumsum(padded)
    seg_start = seg_end - padded
    run_pad = n_tiles * N_EXPERTS * (SUBLANES - 1)
    n_blocks = -(-(n_valid * TOP_K + run_pad + N_EXPERTS * (MOE_BLOCK - 1)) // MOE_BLOCK)
    cap = n_blocks * MOE_BLOCK
    block_first_row = jnp.arange(n_blocks, dtype=jnp.int32) * MOE_BLOCK
    block_expert = jnp.minimum(jnp.sum((seg_end[None, :] <= block_first_row[:, None]).astype(jnp.int32), axis=1),
                               N_EXPERTS - 1)
    xs = _dispatch(h_main, h_tail, slot_t, meta, seg_start, counts, cap)
    used_blocks = (seg_end[N_EXPERTS - 1:] // MOE_BLOCK).astype(jnp.int32)
    ys = _experts(xs, block_expert, used_blocks, w_gate, w_up, w_down)
    return _combine(h_main, h_tail, ys, meta, tokinfo, seg_start, sh_gate.astype(BF16), sh_up.astype(BF16),
                    sh_down.astype(BF16), ln_g, ln_b)


def _ssd_sample_kernel(xbc_ref, z_ref, dtg_ref, sconv_ref, s0_ref, convw_ref, convb_ref, dtb_ref, alog_ref,
                       dskip_ref, normw_ref, y_ref, s_ref, conv_out_ref, xc_ref, dt_ref, da_ref):
    b = pl.program_id(0)

    @pl.when(b == 0)
    def _():
        xin = xbc_ref[...]
        xc = convw_ref[SSD_CONV - 1:SSD_CONV, :] * xin
        for k in range(SSD_CONV - 1):
            xc = xc + convw_ref[k:k + 1, :] * sconv_ref[k]
        xc_ref[...] = _silu(xc + convb_ref[...])
        dt = _softplus(dtg_ref[...] + dtb_ref[...])
        dt_ref[...] = dt
        da_ref[...] = jnp.exp(dt * (-jnp.exp(alog_ref[...])))
        for k in range(SSD_CONV - 2):
            conv_out_ref[k] = sconv_ref[k + 1]
        conv_out_ref[SSD_CONV - 2] = xin

    xc = xc_ref[pl.ds(b, 1), :]
    dt = dt_ref[pl.ds(b, 1), :]
    da = da_ref[pl.ds(b, 1), :]
    ns = SSD_GROUPS * SSD_STATE
    eye = (lax.broadcasted_iota(jnp.int32, (HEAD_DIM, HEAD_DIM), 0)
           == lax.broadcasted_iota(jnp.int32, (HEAD_DIM, HEAD_DIM), 1))
    hpg = SSD_HEADS // SSD_GROUPS
    y_parts = []
    for h in range(SSD_HEADS):
        g = h // hpg
        x_h = xc[:, h * HEAD_DIM:(h + 1) * HEAD_DIM]
        b_g = xc[:, SSD_D + g * SSD_STATE:SSD_D + (g + 1) * SSD_STATE]
        c_g = xc[:, SSD_D + ns + g * SSD_STATE:SSD_D + ns + (g + 1) * SSD_STATE]
        xdt_col = jnp.sum(jnp.where(eye, x_h * dt[:, h:h + 1], 0.0), axis=1, keepdims=True)
        s_new = da[:, h:h + 1] * s0_ref[0, h] + xdt_col * b_g
        s_ref[0, h] = s_new
        y_h = _bdot_nt(c_g, s_new) + dskip_ref[:, h * HEAD_DIM:(h + 1) * HEAD_DIM] * x_h
        y_parts.append(y_h)
    y = jnp.concatenate(y_parts, axis=1)
    y_ref[pl.ds(b, 1), :] = _gated_group_norm(y, z_ref[pl.ds(b, 1), :], normw_ref[...])


def _ssd_sample(xbc, z, dtg, state_conv_t, state_ssm, conv_w, conv_b, dt_bias_pad, a_log_pad, d_skip_full, norm_w):
    bs = xbc.shape[0]
    const = lambda a: pl.BlockSpec(a.shape, lambda b: (0,) * a.ndim)
    state_spec = pl.BlockSpec((1, SSD_HEADS, HEAD_DIM, SSD_STATE), lambda b: (b, 0, 0, 0))
    return pl.pallas_call(
        _ssd_sample_kernel,
        grid=(bs,),
        in_specs=[const(xbc), const(z), const(dtg), const(state_conv_t), state_spec, const(conv_w), const(conv_b),
                  const(dt_bias_pad), const(a_log_pad), const(d_skip_full), const(norm_w)],
        out_specs=[pl.BlockSpec((bs, SSD_D), lambda b: (0, 0)), state_spec,
                   pl.BlockSpec((SSD_CONV - 1, bs, SSD_CONV_CH), lambda b: (0, 0, 0))],
        out_shape=[jax.ShapeDtypeStruct((bs, SSD_D), F32),
                   jax.ShapeDtypeStruct(state_ssm.shape, F32),
                   jax.ShapeDtypeStruct((SSD_CONV - 1, bs, SSD_CONV_CH), F32)],
        scratch_shapes=[pltpu.VMEM((bs, SSD_CONV_CH), F32), pltpu.VMEM((bs, LANES), F32),
                        pltpu.VMEM((bs, LANES), F32)],
        compiler_params=_cparams(("arbitrary",)),
        name="ssd_sample",
    )(xbc, z, dtg, state_conv_t, state_ssm, conv_w, conv_b, dt_bias_pad, a_log_pad, d_skip_full, norm_w)


SEL_PAST = TOP_N - 1
BLOCKS_PER_PAGE = PAGE_SIZE // CMP_BLOCK
KV_FEATS = 2 * KV_D


def _compress_consts_t(cmp_pe, cmp_w1, cmp_b1, cmp_w2, cmp_b2):
    pe_t = jnp.stack([jnp.tile(cmp_pe[k].T, (1, BLOCKS_PER_PAGE)) for k in range(2)])
    w1_t = jnp.stack([_block_diag2(jnp.swapaxes(cmp_w1[k], 0, 1)) for k in range(2)]).astype(BF16)
    b1_t = jnp.stack([jnp.tile(cmp_b1[k], BLOCKS_PER_PAGE) for k in range(2)])[:, None, :]
    w2_t = jnp.stack([_block_diag2(cmp_w2[k]) for k in range(2)]).astype(BF16)
    b2_t = jnp.stack([jnp.tile(cmp_b2[k], BLOCKS_PER_PAGE) for k in range(2)])[:, None, :]
    return pe_t, w1_t, b1_t, w2_t, b2_t


def _compress_pages_kernel(pt_ref, pe_ref, w1_ref, b1_ref, w2_ref, b2_ref, pool_ref, o_ref, kbuf, vbuf, sems, *,
                           n_pages):
    b = pl.program_id(0)
    nb = pl.num_programs(0)
    bufs = (kbuf, vbuf)

    def half_copy(seq, kind, p):
        return pltpu.make_async_copy(pool_ref.at[pt_ref[seq, p], pl.ds(kind * KV_D, KV_D)],
                                     bufs[kind].at[:, p], sems.at[kind])

    def start_half(seq, kind):
        lax.fori_loop(0, n_pages, lambda p, c: (half_copy(seq, kind, p).start(), c)[1], 0)

    def wait_half(seq, kind):
        lax.fori_loop(0, n_pages, lambda p, c: (half_copy(seq, kind, p).wait(), c)[1], 0)

    @pl.when(b == 0)
    def _():
        start_half(b, 0)
        start_half(b, 1)

    for kind in range(2):
        wait_half(b, kind)
        def add_feature(d, acc, kind=kind):
            x = jnp.concatenate([bufs[kind][h * HEAD_DIM + d] for h in range(NSA_KV_HEADS)], axis=0) \
                + pe_ref[kind, pl.ds(d, 1), :]
            return acc + jnp.dot(x.astype(BF16), w1_ref[kind, d], preferred_element_type=F32)

        acc = lax.fori_loop(0, HEAD_DIM, add_feature,
                            jnp.zeros((NSA_KV_HEADS * n_pages, BLOCKS_PER_PAGE * CMP_HIDDEN), F32), unroll=8)
        hid = _silu(acc + b1_ref[kind])
        out = jnp.dot(hid.astype(BF16), w2_ref[kind], preferred_element_type=F32) + b2_ref[kind]
        for h in range(NSA_KV_HEADS):
            o_ref[0, kind * NSA_KV_HEADS + h] = out[h * n_pages:(h + 1) * n_pages]

        @pl.when(b + 1 < nb)
        def _(kind=kind):
            start_half(b + 1, kind)


def _compress_pages(pool_t, page_table, consts):
    bs, n_pages = page_table.shape
    const = lambda a: pl.BlockSpec(a.shape, lambda b, pt: (0,) * a.ndim)
    return pl.pallas_call(
        functools.partial(_compress_pages_kernel, n_pages=n_pages),
        grid_spec=pltpu.PrefetchScalarGridSpec(
            num_scalar_prefetch=1,
            grid=(bs,),
            in_specs=[const(a) for a in consts] + [pl.BlockSpec(memory_space=pl.ANY)],
            out_specs=pl.BlockSpec((1, 2 * NSA_KV_HEADS, n_pages, LANES), lambda b, pt: (b, 0, 0, 0)),
            scratch_shapes=[pltpu.VMEM((KV_D, n_pages, PAGE_SIZE), F32), pltpu.VMEM((KV_D, n_pages, PAGE_SIZE), F32),
                            pltpu.SemaphoreType.DMA((2,))]),
        out_shape=jax.ShapeDtypeStruct((bs, 2 * NSA_KV_HEADS, n_pages, LANES), F32),
        compiler_params=_cparams(("arbitrary",)),
        name="compress_pages",
    )(page_table, *consts, pool_t)


def _group_heads(q_row, hk):
    hpg = NSA_HEADS // NSA_KV_HEADS
    low = lax.broadcasted_iota(jnp.int32, (1, LANES), 1) < HEAD_DIM
    rows = []
    for r in range(hpg):
        head = hk * hpg + r
        tile = q_row[:, (head // 2) * LANES:(head // 2 + 1) * LANES]
        if head % 2 == 1:
            tile = pltpu.roll(tile, HEAD_DIM, 1)
        rows.append(jnp.where(low, tile, 0.0))
    return jnp.concatenate(rows + [jnp.zeros((SUBLANES - hpg, LANES), F32)], axis=0)


def _spread_heads(o_groups):
    hpg = NSA_HEADS // NSA_KV_HEADS
    return jnp.concatenate([o[r:r + 1, 0:HEAD_DIM] for o in o_groups for r in range(hpg)], axis=1)


def _nsa_sample_cmp_t_kernel(qc_ref, cmp_ref, ocmp_ref, idx_ref, *, n_pages):
    b = pl.program_id(0)
    nc = n_pages * BLOCKS_PER_PAGE
    scale = HEAD_DIM ** -0.5
    hpg = NSA_HEADS // NSA_KV_HEADS
    q_row = qc_ref[pl.ds(b, 1), :] * scale
    lane = lax.broadcasted_iota(jnp.int32, (1, LANES), 1)
    pos_r = lax.broadcasted_iota(jnp.int32, (1, nc), 1)
    bid_r = (pos_r % n_pages) * BLOCKS_PER_PAGE + pos_r // n_pages
    pos_c = lax.broadcasted_iota(jnp.int32, (nc, 1), 0)
    bid_c = (pos_c % n_pages) * BLOCKS_PER_PAGE + pos_c // n_pages
    o_groups = []
    for hk in range(NSA_KV_HEADS):
        kc = cmp_ref[0, hk].astype(BF16)
        vc = cmp_ref[0, NSA_KV_HEADS + hk].astype(BF16)
        qh = _group_heads(q_row, hk)
        s = jnp.concatenate(
            [lax.dot_general(pltpu.roll(qh, c * HEAD_DIM, 1).astype(BF16) if c else qh.astype(BF16), kc,
                             (((1,), (1,)), ((), ())), preferred_element_type=F32)
             for c in range(BLOCKS_PER_PAGE)], axis=1)
        ex = jnp.exp(s - jnp.max(s, axis=-1, keepdims=True))
        p = ex / jnp.sum(ex, axis=-1, keepdims=True)
        o = jnp.dot(p[:, 0:n_pages].astype(BF16), vc, preferred_element_type=F32)
        for c in range(1, BLOCKS_PER_PAGE):
            oc = jnp.dot(p[:, c * n_pages:(c + 1) * n_pages].astype(BF16), vc, preferred_element_type=F32)
            o = o + pltpu.roll(oc, LANES - c * HEAD_DIM, 1)
        o_groups.append(o)
        hrow = lax.broadcasted_iota(jnp.int32, p.shape, 0) < hpg
        imp = jnp.sum(jnp.where(hrow, p, 0.0), axis=0, keepdims=True)
        score = jnp.where((bid_r == 0) | (bid_r == nc - 1), FORCED_SCORE, imp)
        score_col = jnp.concatenate([score, jnp.zeros((LANES - 1, nc), F32)], 0).T[:, 0:1]
        beats = (score_col > score) | ((score_col == score) & (bid_c < bid_r))
        rank = jnp.sum(beats.astype(F32), axis=0, keepdims=True)
        row = jnp.zeros((1, LANES), F32)
        bid_f = bid_r.astype(F32)
        for k in range(SEL_PAST):
            blk = jnp.sum(jnp.where(rank == k, bid_f, 0.0), axis=1, keepdims=True)
            row = jnp.where(lane == k, blk, row)
        idx_ref[pl.ds(b * NSA_KV_HEADS + hk, 1), :] = row.astype(jnp.int32)
    ocmp_ref[pl.ds(b, 1), :] = _spread_heads(o_groups)


def _nsa_sample_cmp_t(qc, kvcmp_t):
    bs, _, n_pages, _ = kvcmp_t.shape
    return pl.pallas_call(
        functools.partial(_nsa_sample_cmp_t_kernel, n_pages=n_pages),
        grid=(bs,),
        in_specs=[pl.BlockSpec((bs, NSA_D), lambda b: (0, 0)),
                  pl.BlockSpec((1, 2 * NSA_KV_HEADS, n_pages, LANES), lambda b: (b, 0, 0, 0))],
        out_specs=[pl.BlockSpec((bs, NSA_D), lambda b: (0, 0)),
                   pl.BlockSpec((bs * NSA_KV_HEADS, LANES), lambda b: (0, 0))],
        out_shape=[jax.ShapeDtypeStruct((bs, NSA_D), F32),
                   jax.ShapeDtypeStruct((bs * NSA_KV_HEADS, LANES), jnp.int32)],
        compiler_params=_cparams(("arbitrary",)),
        name="nsa_sample_cmp",
    )(qc, kvcmp_t)


def _sel_block_copies(pool_ref, pt_ref, sel_ref, kbuf, vbuf, sem, b, hk, k):
    blk = sel_ref[b * NSA_KV_HEADS + hk, k]
    page = pt_ref[b, lax.shift_right_logical(blk, int(math.log2(BLOCKS_PER_PAGE)))]
    j = hk * SEL_PAST + k
    return (pltpu.make_async_copy(pool_ref.at[page, pl.ds(hk * HEAD_DIM, HEAD_DIM)], kbuf.at[j], sem),
            pltpu.make_async_copy(pool_ref.at[page, pl.ds(KV_D + hk * HEAD_DIM, HEAD_DIM)], vbuf.at[j], sem))


def _nsa_sample_attn_t_kernel(pt_ref, sel_ref, qr_ref, new_sel_ref, new_win_ref, win_ref, dtg_ref, ocmp_ref,
                              pool_ref, o_ref, kbuf, vbuf, sem):
    b = pl.program_id(0)
    for hk in range(NSA_KV_HEADS):
        for k in range(SEL_PAST):
            for cp in _sel_block_copies(pool_ref, pt_ref, sel_ref, kbuf, vbuf, sem, b, hk, k):
                cp.start()
    for hk in range(NSA_KV_HEADS):
        for k in range(SEL_PAST):
            for cp in _sel_block_copies(pool_ref, pt_ref, sel_ref, kbuf, vbuf, sem, b, hk, k):
                cp.wait()
    scale = HEAD_DIM ** -0.5
    q_row = qr_ref[pl.ds(b, 1), :] * scale
    sig = _sigmoid(dtg_ref[pl.ds(b, 1), :])
    lane = lax.broadcasted_iota(jnp.int32, (1, PAGE_SIZE), 1)
    o_slc, o_win = [], []
    for hk in range(NSA_KV_HEADS):
        qh = _group_heads(q_row, hk)[:, 0:HEAD_DIM].astype(BF16)

        def new_row(ref, kind):
            t = ref[pl.ds(b, 1), :][:, kind * KV_D:(kind + 1) * KV_D]
            if hk == 1:
                t = pltpu.roll(t, HEAD_DIM, 1)
            return t[:, 0:HEAD_DIM].astype(BF16).astype(F32)

        def attend(kt, vt, mask, new_ref, n_new):
            s = jnp.dot(qh, kt.astype(BF16), preferred_element_type=F32)
            if mask is not None:
                s = jnp.where(mask, s, NEG)
            s_new = jnp.sum(qh.astype(F32) * new_row(new_ref, 0), axis=1, keepdims=True)
            m = jnp.maximum(jnp.max(s, axis=-1, keepdims=True), s_new)
            ex = jnp.exp(s - m)
            ex_new = jnp.exp(s_new - m) * n_new
            den = jnp.sum(ex, axis=-1, keepdims=True) + ex_new
            o = lax.dot_general((ex / den).astype(BF16), vt.astype(BF16), (((1,), (1,)), ((), ())),
                                preferred_element_type=F32)
            return o + (ex_new / den).astype(BF16).astype(F32) * new_row(new_ref, 1)

        kt = jnp.concatenate([kbuf[hk * SEL_PAST + k] for k in range(SEL_PAST)], axis=1)
        vt = jnp.concatenate([vbuf[hk * SEL_PAST + k] for k in range(SEL_PAST)], axis=1)
        mask = jnp.concatenate(
            [lane // SEL_BLOCK == (sel_ref[b * NSA_KV_HEADS + hk, k] & (BLOCKS_PER_PAGE - 1))
             for k in range(SEL_PAST)], axis=1)
        o_slc.append(attend(kt, vt, mask, new_sel_ref, float(SEL_BLOCK)))
        o_win.append(attend(win_ref[0, hk * HEAD_DIM:(hk + 1) * HEAD_DIM, :],
                            win_ref[0, KV_D + hk * HEAD_DIM:KV_D + (hk + 1) * HEAD_DIM, :], None, new_win_ref, 1.0))
    gates = []
    for br in range(3):
        gates.append(jnp.concatenate(
            [jnp.broadcast_to(sig[:, GATE_COL0 + h * 3 + br:GATE_COL0 + h * 3 + br + 1], (1, HEAD_DIM))
             for h in range(NSA_HEADS)], axis=1))
    o_ref[pl.ds(b, 1), :] = (gates[0] * ocmp_ref[pl.ds(b, 1), :] + gates[1] * _spread_heads(o_slc)
                             + gates[2] * _spread_heads(o_win))


def _nsa_sample_attn_t(qr, new_sel, new_win, win_t, dtg, o_cmp, pool_sel_t, page_table, sel_idx):
    bs = qr.shape[0]
    const = lambda a: pl.BlockSpec(a.shape, lambda b, pt, sel: (0,) * a.ndim)
    n_buf = NSA_KV_HEADS * SEL_PAST
    return pl.pallas_call(
        _nsa_sample_attn_t_kernel,
        grid_spec=pltpu.PrefetchScalarGridSpec(
            num_scalar_prefetch=2,
            grid=(bs,),
            in_specs=[const(qr), const(new_sel), const(new_win),
                      pl.BlockSpec((1,) + win_t.shape[1:], lambda b, pt, sel: (b, 0, 0)),
                      const(dtg), const(o_cmp), pl.BlockSpec(memory_space=pl.ANY)],
            out_specs=pl.BlockSpec((bs, NSA_D), lambda b, pt, sel: (0, 0)),
            scratch_shapes=[pltpu.VMEM((n_buf, HEAD_DIM, PAGE_SIZE), F32), pltpu.VMEM((n_buf, HEAD_DIM, PAGE_SIZE), F32),
                            pltpu.SemaphoreType.DMA]),
        out_shape=jax.ShapeDtypeStruct((bs, NSA_D), F32),
        compiler_params=_cparams(("arbitrary",)),
        name="nsa_sample_attn",
    )(page_table, sel_idx, qr, new_sel, new_win, win_t, dtg, o_cmp, pool_sel_t)


def kernel(x_prompt, x_sample, cache_kv_cmp, cache_kv_sel, page_table, cache_kv_win, state_ssm, state_conv,
           emb_ln_g, emb_ln_b, w_in, conv_w, conv_b, dt_bias, a_log, d_skip, ssd_norm_w,
           cmp_pe, cmp_w1, cmp_b1, cmp_w2, cmp_b2, w_out, ln1_g, ln1_b,
           router_w, router_bias, exp_w_gate, exp_w_up, exp_w_down,
           sh_w_gate, sh_w_up, sh_w_down, ln2_g, ln2_b):
    bp, tp, _ = x_prompt.shape
    bs, ts, _ = x_sample.shape
    assert ts == 1 and DEPTH == 1
    n_prompt = bp * tp
    past_len = page_table.shape[1] * PAGE_SIZE
    l = 0
    w_perm = _permute_w_in(w_in[l])
    ln0_g, ln0_b = emb_ln_g[None], emb_ln_b[None]
    ssd_consts = (conv_w[l], conv_b[l][None], _pad_lanes(dt_bias[l]), _pad_lanes(a_log[l]),
                  jnp.repeat(d_skip[l], HEAD_DIM)[None], ssd_norm_w[l][None])
    cmp_consts = _compress_consts(cmp_pe[l], cmp_w1[l], cmp_b1[l], cmp_w2[l], cmp_b2[l])
    w_o = w_out[l].astype(BF16)
    w_o_ssd, w_o_nsa = w_o[:SSD_D], w_o[SSD_D:]
    ln1 = (ln1_g[l][None], ln1_b[l][None])
    kv_shape = (2, NSA_KV_HEADS, HEAD_DIM)

    hp, z, xbc, qc, qr, kvc, kvs, kvw, dtg, kvc_t, kvs_t, kvw_t = _inproj(
        x_prompt.reshape(n_prompt, D_MODEL), ln0_g, ln0_b, w_perm, _rope_tables(jnp.arange(tp)), 256,
        _rope_tables_t(jnp.arange(tp)))
    y_ssd, ssm_p, conv_p = _ssd_prompt(xbc, z, dtg, *ssd_consts, bp, tp)
    kvcmp = _compress_prompt(kvc, cmp_consts, tp)
    y_nsa = _nsa_prompt(qc, qr, dtg, kvcmp, kvs, kvw, bp, tp)
    h1p = _outproj(y_ssd, y_nsa, hp, w_o_ssd, w_o_nsa, *ln1, 512)
    n_keep = min(WINDOW, tp)
    cache_leaf = lambda a: jnp.transpose(a.reshape((bp,) + kv_shape + (a.shape[-1],)), (0, 4, 1, 2, 3))[None]
    kvc_p = cache_leaf(kvc_t)
    kvs_p = cache_leaf(kvs_t)
    kvw_p = cache_leaf(kvw_t[:, :, tp - n_keep:])

    s_hs, s_z, s_xbc, s_qc, s_qr, s_kvc, s_kvs, s_kvw, s_dtg = _inproj(
        x_sample.reshape(bs, D_MODEL), ln0_g, ln0_b, w_perm, _rope_tables(jnp.full((bs,), past_len)), bs)
    s_y_ssd, ssm_s, conv_s_t = _ssd_sample(s_xbc, s_z, s_dtg, jnp.swapaxes(state_conv[l], 0, 1), state_ssm[l],
                                           *ssd_consts)
    n_pool = cache_kv_cmp.shape[1]
    feature_major = lambda c, rows: jnp.swapaxes(c.reshape(-1, rows, 2 * KV_D), 1, 2)
    s_kvcmp = _compress_pages(feature_major(cache_kv_cmp[l], PAGE_SIZE), page_table,
                              _compress_consts_t(cmp_pe[l], cmp_w1[l], cmp_b1[l], cmp_w2[l], cmp_b2[l]))
    s_o_cmp, s_sel = _nsa_sample_cmp_t(s_qc, s_kvcmp)
    buf_win = cache_kv_win[l].reshape(bs, -1, 2 * KV_D)
    s_y_nsa = _nsa_sample_attn_t(
        s_qr, s_kvs, s_kvw, feature_major(cache_kv_win[l], buf_win.shape[1]), s_dtg, s_o_cmp,
        feature_major(cache_kv_sel[l], PAGE_SIZE), page_table, s_sel)
    h1s = _outproj(s_y_ssd, s_y_nsa, s_hs, w_o_ssd, w_o_nsa, *ln1, bs)
    win_all = jnp.concatenate([buf_win, s_kvw[:, None, :]], 1)
    n_keep_s = min(WINDOW, past_len + ts)
    kvw_s = win_all[:, win_all.shape[1] - n_keep_s:].reshape((1, bs, n_keep_s) + kv_shape)
    kvc_s = s_kvc.reshape((1, bs, ts) + kv_shape)
    kvs_s = s_kvs.reshape((1, bs, ts) + kv_shape)

    assert n_prompt % MOE_TOKENS == 0 and bs * ts <= MOE_TOKENS
    n_tok = n_prompt + bs * ts
    tail = jnp.concatenate([h1s, jnp.zeros((MOE_TOKENS - bs * ts, D_MODEL), F32)], 0)
    out_main, out_tail = _moe_ln(h1p, tail, n_tok, router_w[l], router_bias[l], exp_w_gate[l], exp_w_up[l],
                                 exp_w_down[l], sh_w_gate[l], sh_w_up[l], sh_w_down[l], ln2_g[l][None], ln2_b[l][None])
    y_prompt = out_main.reshape(bp, tp, D_MODEL)
    y_sample = out_tail[:bs * ts].reshape(bs, ts, D_MODEL)
    return (y_prompt, y_sample, kvc_p, kvs_p, kvw_p, ssm_p[None], conv_p[None],
            kvc_s, kvs_s, kvw_s, ssm_s[None], jnp.swapaxes(conv_s_t, 0, 1)[None])
```

```python
import functools
import math

import jax
import jax.numpy as jnp
import numpy as np
from jax import lax
from jax.experimental import pallas as pl
from jax.experimental.pallas import tpu as pltpu

D_MODEL = 1024
HEAD_DIM = 64
SSD_HEADS = 8
SSD_D = SSD_HEADS * HEAD_DIM
SSD_GROUPS = 2
SSD_STATE = 128
SSD_CONV = 4
SSD_CONV_CH = SSD_D + 2 * SSD_GROUPS * SSD_STATE
SSD_CHUNK = 128
NSA_HEADS = 8
NSA_KV_HEADS = 2
NSA_D = NSA_HEADS * HEAD_DIM
KV_D = NSA_KV_HEADS * HEAD_DIM
CMP_BLOCK = 64
CMP_HIDDEN = 128
SEL_BLOCK = 64
TOP_N = 16
WINDOW = 512
Q_BLOCK = 128
ROT_DIM = HEAD_DIM // 4
ROPE_THETA = 500000.0
N_EXPERTS = 64
TOP_K = 6
N_EXPERT_GROUPS = 8
EXPERTS_PER_GROUP = N_EXPERTS // N_EXPERT_GROUPS
TOPK_GROUPS = 4
D_EXPERT = 256
D_SHARED = 256
ROUTED_SCALE = 2.5
MOE_BLOCK = 1024
DEPTH = 1
DEEPNORM_ALPHA = (2.0 * DEPTH) ** 0.25
LN_EPS = 1e-5
RMS_EPS = 1e-5
NEG = -1e30
FORCED_SCORE = 1e4
PAGE_SIZE = 128

LANES = 128
SUBLANES = 8
VMEM_LIMIT_BYTES = 56 * 1024 * 1024

U_Z = 0
U_XBC = U_Z + SSD_D
U_Q = U_XBC + SSD_CONV_CH
U_KVC = U_Q + NSA_D
U_KVS = U_KVC + 2 * KV_D
U_KVW = U_KVS + 2 * KV_D
U_DTG = U_KVW + 2 * KV_D
U_TOTAL = U_DTG + LANES
GATE_COL0 = SSD_HEADS

BF16 = jnp.bfloat16
F32 = jnp.float32


def _cparams(sem):
    return pltpu.CompilerParams(dimension_semantics=sem, vmem_limit_bytes=VMEM_LIMIT_BYTES)


def _bdot(a, b):
    return jnp.dot(a.astype(BF16), b.astype(BF16), preferred_element_type=F32)


def _bdot_nt(a, b):
    return lax.dot_general(a.astype(BF16), b.astype(BF16), (((1,), (1,)), ((), ())),
                           preferred_element_type=F32)


def _hdot(a, b):
    return jnp.dot(a, b, preferred_element_type=F32, precision=lax.Precision.HIGHEST)


def _sigmoid(x):
    return 1.0 / (1.0 + jnp.exp(-x))


def _silu(x):
    return x * _sigmoid(x)


def _layer_norm(x, g, b):
    mu = jnp.mean(x, axis=-1, keepdims=True)
    xc = x - mu
    var = jnp.mean(xc * xc, axis=-1, keepdims=True)
    return xc * lax.rsqrt(var + LN_EPS) * g + b


def _rope_tile(x, cos, sa, sb):
    return x * cos + pltpu.roll(x, LANES - ROT_DIM // 2, 1) * sa + pltpu.roll(x, ROT_DIM // 2, 1) * sb


def _rope_rows(x, cos, sin):
    half = ROT_DIM // 2
    parts = []
    for hd in range(NSA_KV_HEADS):
        r0 = hd * HEAD_DIM
        x1, x2 = x[r0:r0 + half], x[r0 + half:r0 + ROT_DIM]
        parts += [x1 * cos - x2 * sin, x2 * cos + x1 * sin, x[r0 + ROT_DIM:r0 + HEAD_DIM]]
    return jnp.concatenate(parts, axis=0)


def _inproj_kernel(x_ref, g_ref, b_ref, w_ref, rope_ref, *refs, feature_major):
    if feature_major:
        wkv_t_ref, rope_t_ref = refs[:2]
        refs = refs[2:]
    h_ref, z_ref, xbc_ref, qc_ref, qr_ref, kvc_ref, kvs_ref, kvw_ref, dtg_ref = refs[:9]
    h = _layer_norm(x_ref[...], g_ref[...], b_ref[...])
    h_ref[...] = h
    hb = h.astype(BF16)
    if feature_major:
        kvc_t_ref, kvs_t_ref, kvw_t_ref = refs[9:]
        ut = lax.dot_general(wkv_t_ref[...], hb, (((1,), (1,)), ((), ())), preferred_element_type=F32)
        half = ROT_DIM // 2
        cos_t, sin_t = rope_t_ref[0:half, :], rope_t_ref[half:2 * half, :]
        kvc_t_ref[0] = ut[0:2 * KV_D]
        kvs_t_ref[0, 0:KV_D] = _rope_rows(ut[2 * KV_D:3 * KV_D], cos_t, sin_t)
        kvs_t_ref[0, KV_D:2 * KV_D] = ut[3 * KV_D:4 * KV_D]
        kvw_t_ref[0, 0:KV_D] = _rope_rows(ut[4 * KV_D:5 * KV_D], cos_t, sin_t)
        kvw_t_ref[0, KV_D:2 * KV_D] = ut[5 * KV_D:6 * KV_D]
    u = jnp.dot(hb, w_ref[...], preferred_element_type=F32)
    cos = rope_ref[:, 0:LANES]
    sa = rope_ref[:, LANES:2 * LANES]
    sb = rope_ref[:, 2 * LANES:3 * LANES]
    z_ref[...] = u[:, U_Z:U_XBC]
    xbc_ref[...] = u[:, U_XBC:U_Q]
    qc_ref[...] = u[:, U_Q:U_KVC].astype(qc_ref.dtype)
    for c in range(NSA_D // LANES):
        qr_ref[:, c * LANES:(c + 1) * LANES] = _rope_tile(
            u[:, U_Q + c * LANES:U_Q + (c + 1) * LANES], cos, sa, sb).astype(qr_ref.dtype)
    kvc_ref[...] = u[:, U_KVC:U_KVS]
    kvs_ref[:, 0:KV_D] = _rope_tile(u[:, U_KVS:U_KVS + KV_D], cos, sa, sb)
    kvs_ref[:, KV_D:2 * KV_D] = u[:, U_KVS + KV_D:U_KVW]
    kvw_ref[:, 0:KV_D] = _rope_tile(u[:, U_KVW:U_KVW + KV_D], cos, sa, sb)
    kvw_ref[:, KV_D:2 * KV_D] = u[:, U_KVW + KV_D:U_DTG]
    dtg_ref[...] = u[:, U_DTG:U_TOTAL]


def _rope_tables(pos):
    half = ROT_DIM // 2
    inv = ROPE_THETA ** (-jnp.arange(half, dtype=F32) / half)
    ang = pos.astype(F32)[:, None] * inv
    cos, sin = jnp.cos(ang), jnp.sin(ang)
    ones = jnp.ones((pos.shape[0], HEAD_DIM - ROT_DIM), F32)
    zeros = jnp.zeros((pos.shape[0], HEAD_DIM - ROT_DIM), F32)
    zh = jnp.zeros_like(sin)
    c = jnp.concatenate([cos, cos, ones], 1)
    sa = jnp.concatenate([-sin, zh, zeros], 1)
    sb = jnp.concatenate([zh, sin, zeros], 1)
    return jnp.concatenate([jnp.tile(t, (1, LANES // HEAD_DIM)) for t in (c, sa, sb)], 1)


def _permute_w_in(w):
    sizes = (SSD_D, SSD_CONV_CH, SSD_HEADS, NSA_D, KV_D, KV_D, KV_D, KV_D, KV_D, KV_D, 3 * NSA_HEADS)
    offs = np.concatenate([[0], np.cumsum(sizes)])
    seg = [w[:, offs[i]:offs[i + 1]] for i in range(len(sizes))]
    pad = jnp.zeros((w.shape[0], LANES - SSD_HEADS - 3 * NSA_HEADS), w.dtype)
    out = jnp.concatenate([seg[0], seg[1], seg[3], seg[4], seg[5], seg[6], seg[7], seg[8], seg[9],
                           seg[2], seg[10], pad], 1)
    return out.astype(BF16)


def _rope_tables_t(pos):
    half = ROT_DIM // 2
    inv = ROPE_THETA ** (-jnp.arange(half, dtype=F32) / half)
    ang = inv[:, None] * pos.astype(F32)[None, :]
    return jnp.concatenate([jnp.cos(ang), jnp.sin(ang)], 0)


def _inproj(x, ln_g, ln_b, w_perm, rope_tab, tm, rope_tab_t=None):
    n = x.shape[0]
    nt = n // tm
    t = rope_tab.shape[0]
    n_rope_blocks = t // tm
    feature_major = rope_tab_t is not None
    row = lambda w: pl.BlockSpec((tm, w), lambda i: (i, 0))
    const = lambda a: pl.BlockSpec(a.shape, lambda i: (0,) * a.ndim)
    widths = (D_MODEL, SSD_D, SSD_CONV_CH, NSA_D, NSA_D, 2 * KV_D, 2 * KV_D, 2 * KV_D, LANES)
    in_specs = [row(D_MODEL), const(ln_g), const(ln_b), const(w_perm),
                pl.BlockSpec((tm, 3 * LANES), lambda i: (i % n_rope_blocks, 0))]
    out_specs = [row(w) for w in widths]
    dtypes = [BF16 if (feature_major and k in (3, 4)) else F32 for k in range(len(widths))]
    out_shape = [jax.ShapeDtypeStruct((n, w), d) for w, d in zip(widths, dtypes)]
    args = [x, ln_g, ln_b, w_perm, rope_tab]
    if feature_major:
        wkv_t = w_perm[:, U_KVC:U_DTG].T
        in_specs += [const(wkv_t), pl.BlockSpec((rope_tab_t.shape[0], tm), lambda i: (0, i % n_rope_blocks))]
        args += [wkv_t, rope_tab_t]
        out_specs += [pl.BlockSpec((1, 2 * KV_D, tm), lambda i: (i // n_rope_blocks, 0, i % n_rope_blocks))] * 3
        out_shape += [jax.ShapeDtypeStruct((n // t, 2 * KV_D, t), F32)] * 3
    return pl.pallas_call(
        functools.partial(_inproj_kernel, feature_major=feature_major),
        grid=(nt,),
        in_specs=in_specs,
        out_specs=out_specs,
        out_shape=out_shape,
        compiler_params=_cparams(("parallel",)),
        name="inproj",
    )(*args)


def _softplus(x):
    return jnp.maximum(x, 0.0) + jnp.log1p(jnp.exp(-jnp.abs(x)))


def _gated_group_norm(y, z, norm_w):
    y = y * _silu(z)
    gw = SSD_D // SSD_GROUPS
    parts = []
    for g in range(SSD_GROUPS):
        yg = y[:, g * gw:(g + 1) * gw]
        ms = jnp.mean(yg * yg, axis=-1, keepdims=True)
        parts.append(yg * lax.rsqrt(ms + RMS_EPS))
    return jnp.concatenate(parts, axis=1) * norm_w


SSD_CHUNKS_PER_STEP = 2


def _ssd_prompt_kernel(xbc_ref, z_ref, dtg_ref, convw_ref, convb_ref, dtb_ref, alog_ref, dskip_ref, normw_ref,
                       y_ref, state_ref, conv_ref, ext_ref, s_ref):
    c = pl.program_id(1)
    nc = pl.num_programs(1)
    L = SSD_CHUNK

    @pl.when(c == 0)
    def _():
        ext_ref[0:SUBLANES, :] = jnp.zeros((SUBLANES, SSD_CONV_CH), F32)
        s_ref[...] = jnp.zeros_like(s_ref)

    for sub in range(SSD_CHUNKS_PER_STEP):
        rows = pl.ds(sub * L, L)
        xin = _ssd_chunk(xbc_ref.at[rows], z_ref.at[rows], dtg_ref.at[rows], convw_ref, convb_ref, dtb_ref, alog_ref,
                         dskip_ref, normw_ref, y_ref.at[rows], ext_ref, s_ref)

    @pl.when(c == nc - 1)
    def _():
        state_ref[0] = s_ref[...]
        conv_ref[0] = xin[L - (SSD_CONV - 1):L, :]


def _ssd_chunk(xbc_ref, z_ref, dtg_ref, convw_ref, convb_ref, dtb_ref, alog_ref, dskip_ref, normw_ref, y_ref, ext_ref,
               s_ref):
    L = SSD_CHUNK
    halo = SUBLANES
    xin = xbc_ref[...]
    ext_ref[halo:halo + L, :] = xin
    xc = convw_ref[SSD_CONV - 1:SSD_CONV, :] * xin
    for k in range(SSD_CONV - 1):
        off = halo - (SSD_CONV - 1) + k
        xc = xc + convw_ref[k:k + 1, :] * ext_ref[off:off + L, :]
    ext_ref[0:halo, :] = ext_ref[L:L + halo, :]
    xc = _silu(xc + convb_ref[...])
    xs = xc[:, 0:SSD_D]
    ns = SSD_GROUPS * SSD_STATE
    bm = xc[:, SSD_D:SSD_D + ns]
    cm = xc[:, SSD_D + ns:SSD_D + 2 * ns]

    dt = _softplus(dtg_ref[...] + dtb_ref[...])
    da = dt * (-jnp.exp(alog_ref[...]))
    row = lax.broadcasted_iota(jnp.int32, (L, L), 0)
    col = lax.broadcasted_iota(jnp.int32, (L, L), 1)
    tril = row >= col
    acum = _hdot(tril.astype(F32), da)
    acum_t = acum.T
    eacum = jnp.exp(acum)
    alast = acum[L - 1:L, :]
    edecay = jnp.exp(alast - acum)
    elast = jnp.exp(alast)

    dt_full = jnp.concatenate([jnp.broadcast_to(dt[:, h:h + 1], (L, HEAD_DIM)) for h in range(SSD_HEADS)], 1)
    dec_full = jnp.concatenate([jnp.broadcast_to(edecay[:, h:h + 1], (L, HEAD_DIM)) for h in range(SSD_HEADS)], 1)
    xdt = xs * dt_full
    xdec_t = (xdt * dec_full).T

    hpg = SSD_HEADS // SSD_GROUPS
    y_parts = []
    for h in range(SSD_HEADS):
        g = h // hpg
        b_g = bm[:, g * SSD_STATE:(g + 1) * SSD_STATE]
        c_g = cm[:, g * SSD_STATE:(g + 1) * SSD_STATE]
        if h % hpg == 0:
            cb = _bdot_nt(c_g, b_g)
        seg = acum[:, h:h + 1] - acum_t[h:h + 1, :]
        lmat = jnp.where(tril, jnp.exp(jnp.where(tril, seg, 0.0)), 0.0)
        xdt_h = xdt[:, h * HEAD_DIM:(h + 1) * HEAD_DIM]
        y_h = _bdot(cb * lmat, xdt_h)
        s_prev = s_ref[h]
        y_h = y_h + _bdot_nt(c_g, s_prev) * eacum[:, h:h + 1]
        y_h = y_h + dskip_ref[:, h * HEAD_DIM:(h + 1) * HEAD_DIM] * xs[:, h * HEAD_DIM:(h + 1) * HEAD_DIM]
        y_parts.append(y_h)
        s_ref[h] = elast[:, h:h + 1] * s_prev + _bdot(xdec_t[h * HEAD_DIM:(h + 1) * HEAD_DIM, :], b_g)
    y = jnp.concatenate(y_parts, axis=1)
    y_ref[...] = _gated_group_norm(y, z_ref[...], normw_ref[...]).astype(y_ref.dtype)
    return xin


def _ssd_prompt(xbc, z, dtg, conv_w, conv_b, dt_bias_pad, a_log_pad, d_skip_full, norm_w, bn, t):
    step_rows = SSD_CHUNK * SSD_CHUNKS_PER_STEP
    nc = t // step_rows
    row = lambda w: pl.BlockSpec((step_rows, w), lambda b, c: (b * nc + c, 0))
    const = lambda a: pl.BlockSpec(a.shape, lambda b, c: (0,) * a.ndim)
    return pl.pallas_call(
        _ssd_prompt_kernel,
        grid=(bn, nc),
        in_specs=[row(SSD_CONV_CH), row(SSD_D), row(LANES), const(conv_w), const(conv_b), const(dt_bias_pad),
                  const(a_log_pad), const(d_skip_full), const(norm_w)],
        out_specs=[row(SSD_D),
                   pl.BlockSpec((1, SSD_HEADS, HEAD_DIM, SSD_STATE), lambda b, c: (b, 0, 0, 0)),
                   pl.BlockSpec((1, SSD_CONV - 1, SSD_CONV_CH), lambda b, c: (b, 0, 0))],
        out_shape=[jax.ShapeDtypeStruct((bn * t, SSD_D), BF16),
                   jax.ShapeDtypeStruct((bn, SSD_HEADS, HEAD_DIM, SSD_STATE), F32),
                   jax.ShapeDtypeStruct((bn, SSD_CONV - 1, SSD_CONV_CH), F32)],
        scratch_shapes=[pltpu.VMEM((SSD_CHUNK + 2 * SUBLANES, SSD_CONV_CH), F32),
                        pltpu.VMEM((SSD_HEADS, HEAD_DIM, SSD_STATE), F32)],
        compiler_params=_cparams(("parallel", "arbitrary")),
        name="ssd_prompt",
    )(xbc, z, dtg, conv_w, conv_b, dt_bias_pad, a_log_pad, d_skip_full, norm_w)


def _pad_lanes(v, fill=0.0):
    return jnp.concatenate([v.astype(F32), jnp.full((LANES - v.shape[0],), fill, F32)])[None]


def _compress_rows(k_ref, v_ref, pe_ref, w1k_ref, w1v_ref, b1_ref, w2k_ref, w2v_ref, b2_ref, nb):
    acck = jnp.zeros((nb, 2 * CMP_HIDDEN), F32)
    accv = jnp.zeros((nb, 2 * CMP_HIDDEN), F32)
    for l in range(CMP_BLOCK):
        xk = k_ref[pl.ds(l, nb, stride=CMP_BLOCK), :] + pe_ref[l:l + 1, 0:KV_D]
        xv = v_ref[pl.ds(l, nb, stride=CMP_BLOCK), :] + pe_ref[l:l + 1, KV_D:2 * KV_D]
        acck = acck + jnp.dot(xk.astype(BF16), w1k_ref[l], preferred_element_type=F32)
        accv = accv + jnp.dot(xv.astype(BF16), w1v_ref[l], preferred_element_type=F32)
    hk = _silu(acck + b1_ref[:, 0:2 * CMP_HIDDEN])
    hv = _silu(accv + b1_ref[:, 2 * CMP_HIDDEN:4 * CMP_HIDDEN])
    ok = jnp.dot(hk.astype(BF16), w2k_ref[...], preferred_element_type=F32) + b2_ref[:, 0:KV_D]
    ov = jnp.dot(hv.astype(BF16), w2v_ref[...], preferred_element_type=F32) + b2_ref[:, KV_D:2 * KV_D]
    return jnp.concatenate([ok, ov], axis=1)


def _compress_kernel(k_ref, v_ref, pe_ref, w1k_ref, w1v_ref, b1_ref, w2k_ref, w2v_ref, b2_ref, o_ref, *, nb):
    o_ref[...] = _compress_rows(k_ref, v_ref, pe_ref, w1k_ref, w1v_ref, b1_ref, w2k_ref, w2v_ref, b2_ref, nb)


def _block_diag2(w):
    z = jnp.zeros_like(w)
    return jnp.concatenate([jnp.concatenate([w, z], -1), jnp.concatenate([z, w], -1)], -2)


def _compress_consts(cmp_pe, cmp_w1, cmp_b1, cmp_w2, cmp_b2):
    pe = jnp.concatenate([cmp_pe[0], cmp_pe[0], cmp_pe[1], cmp_pe[1]], -1)
    w1k = _block_diag2(cmp_w1[0]).astype(BF16)
    w1v = _block_diag2(cmp_w1[1]).astype(BF16)
    b1 = jnp.concatenate([cmp_b1[0], cmp_b1[0], cmp_b1[1], cmp_b1[1]])[None]
    w2k = _block_diag2(cmp_w2[0]).astype(BF16)
    w2v = _block_diag2(cmp_w2[1]).astype(BF16)
    b2 = jnp.concatenate([cmp_b2[0], cmp_b2[0], cmp_b2[1], cmp_b2[1]])[None]
    return pe, w1k, w1v, b1, w2k, w2v, b2


def _compress_prompt(kvc, consts, rows_per_step):
    n = kvc.shape[0]
    nb = rows_per_step // CMP_BLOCK
    const = lambda a: pl.BlockSpec(a.shape, lambda i: (0,) * a.ndim)
    return pl.pallas_call(
        functools.partial(_compress_kernel, nb=nb),
        grid=(n // rows_per_step,),
        in_specs=[pl.BlockSpec((rows_per_step, KV_D), lambda i: (i, 0)),
                  pl.BlockSpec((rows_per_step, KV_D), lambda i: (i, 1))] + [const(a) for a in consts],
        out_specs=pl.BlockSpec((nb, 2 * KV_D), lambda i: (i, 0)),
        out_shape=jax.ShapeDtypeStruct((n // CMP_BLOCK, 2 * KV_D), F32),
        compiler_params=_cparams(("parallel",)),
        name="compress_prompt",
    )(kvc, kvc, *consts)


SEL_KEY_TILE = 512
WIN_KEYS = WINDOW + Q_BLOCK


def _dup_head(x, hk):
    sw = pltpu.roll(x, HEAD_DIM, 1)
    low = lax.broadcasted_iota(jnp.int32, x.shape, 1) < HEAD_DIM
    return jnp.where(low, x, sw) if hk == 0 else jnp.where(low, sw, x)


def _masked_softmax(s, mask):
    sm = jnp.where(mask, s, NEG)
    ex = jnp.where(mask, jnp.exp(sm - jnp.max(sm, axis=-1, keepdims=True)), 0.0)
    den = jnp.sum(ex, axis=-1, keepdims=True)
    return ex / jnp.where(den > 0.0, den, 1.0)


def _select_blocks_t(imp, cur, n_top):
    j = lax.broadcasted_iota(jnp.int32, imp.shape, 0)
    future = j > cur
    forced = (j == 0) | (j == cur) | (j == cur - 1)
    score = jnp.where(future, NEG, jnp.where(forced, FORCED_SCORE, imp))
    return ((_rank_rows(score) < n_top) & (score > 0.5 * NEG)).astype(F32)


def _nsa_prompt_kernel(qc_ref, qr_ref, dtg_ref, cmp_ref, kvs_ref, kvw_ref, o_ref,
                       cmp_d, kvs_d, kvw_d, expand_ref, bias_ref, qrs_ref, m_ref, l_ref, acc_ref, *, t):
    qb = pl.program_id(1)
    nbk = t // SEL_BLOCK
    tq = Q_BLOCK
    tk = SEL_KEY_TILE
    hpg = NSA_HEADS // NSA_KV_HEADS
    scale = HEAD_DIM ** -0.5

    @pl.when(qb == 0)
    def _():
        expand_ref[...] = (lax.broadcasted_iota(jnp.int32, (LANES, t), 1) // SEL_BLOCK
                           == lax.broadcasted_iota(jnp.int32, (LANES, t), 0)).astype(BF16)
        cmp_d[...] = jnp.zeros_like(cmp_d)
        for src, dst, n in ((cmp_ref, cmp_d, nbk), (kvs_ref, kvs_d, t), (kvw_ref, kvw_d, t)):
            x = src[...]
            for hk in range(NSA_KV_HEADS):
                dst[hk, 0:n, 0:KV_D] = _dup_head(x[:, 0:KV_D], hk).astype(BF16)
                dst[hk, 0:n, KV_D:2 * KV_D] = _dup_head(x[:, KV_D:2 * KV_D], hk).astype(BF16)

    t0 = qb * tq
    rows = t0 + lax.broadcasted_iota(jnp.int32, (tq, 1), 0)
    lane = lax.broadcasted_iota(jnp.int32, (tq, LANES), 1)
    half_mask = (lane < HEAD_DIM, lane >= HEAD_DIM)
    sig = _sigmoid(dtg_ref[...])
    vis = (lane + 1) * CMP_BLOCK - 1 <= rows
    cur_l = (t0 + lax.broadcasted_iota(jnp.int32, (1, tq), 1)) // SEL_BLOCK
    win_start = pl.multiple_of(jnp.maximum(t0 - WINDOW, 0), tq)
    wpos = win_start + lax.broadcasted_iota(jnp.int32, (tq, WIN_KEYS), 1)
    win_bias = jnp.where((wpos <= rows) & (wpos >= rows - WINDOW), 0.0, NEG)
    n_kt = (t0 + tq + tk - 1) // tk

    def stack_heads(ref, hk):
        parts = []
        for hh in range(hpg):
            head = hk * hpg + hh
            p, e = head // 2, head % 2
            parts.append(jnp.where(half_mask[e], ref[:, p * LANES:(p + 1) * LANES] * scale, 0.0))
        return jnp.concatenate(parts, axis=0).astype(BF16)

    o_cmp_g = []
    for hk in range(NSA_KV_HEADS):
        qcs = stack_heads(qc_ref, hk)
        s = lax.dot_general(qcs, cmp_d[hk, :, 0:KV_D], (((1,), (1,)), ((), ())), preferred_element_type=F32)
        pc = _masked_softmax(s.reshape(hpg, tq, LANES), vis[None])
        imp = jnp.sum(pc, axis=0)
        o_cmp_g.append(jnp.dot(pc.reshape(hpg * tq, LANES).astype(BF16), cmp_d[hk, :, KV_D:2 * KV_D],
                               preferred_element_type=F32).reshape(hpg, tq, LANES))

        sel_t = _select_blocks_t(imp.T[0:nbk, :], cur_l, TOP_N)
        sel = jnp.concatenate([sel_t, jnp.zeros((LANES - nbk, tq), F32)], axis=0).T
        selk = jnp.dot(sel.astype(BF16), expand_ref[...], preferred_element_type=F32)
        for kt in range(t // tk):
            @pl.when(kt < n_kt)
            def _(kt=kt, hk=hk, selk=selk):
                kpos = kt * tk + lax.broadcasted_iota(jnp.int32, (tq, tk), 1)
                bias_ref[hk, kt] = jnp.where((selk[:, kt * tk:(kt + 1) * tk] > 0.5) & (kpos <= rows), 0.0, NEG)

        qrs_ref[hk] = stack_heads(qr_ref, hk)

    m_ref[...] = jnp.full(m_ref.shape, NEG, F32)
    l_ref[...] = jnp.zeros(l_ref.shape, F32)
    acc_ref[...] = jnp.zeros(acc_ref.shape, F32)

    def sel_step(kt, carry):
        k0 = pl.multiple_of(kt * tk, tk)
        for hk in range(NSA_KV_HEADS):
            kblk = kvs_d[hk, pl.ds(k0, tk), 0:KV_D]
            vblk = kvs_d[hk, pl.ds(k0, tk), KV_D:2 * KV_D]
            s = lax.dot_general(qrs_ref[hk], kblk, (((1,), (1,)), ((), ())), preferred_element_type=F32)
            s = s.reshape(hpg, tq, tk) + bias_ref[hk, kt][None]
            m_old = m_ref[hk]
            m_new = jnp.maximum(m_old, jnp.max(s, axis=-1, keepdims=True))
            alpha = jnp.exp(m_old - m_new)
            pe = jnp.exp(s - jnp.concatenate([m_new] * (tk // LANES), axis=-1))
            l_ref[hk] = alpha * l_ref[hk] + jnp.sum(pe, axis=-1, keepdims=True)
            pv = jnp.dot(pe.reshape(hpg * tq, tk).astype(BF16), vblk, preferred_element_type=F32)
            acc_ref[hk] = alpha * acc_ref[hk] + pv.reshape(hpg, tq, LANES)
            m_ref[hk] = m_new
        return carry

    lax.fori_loop(0, n_kt, sel_step, 0)

    for hk in range(NSA_KV_HEADS):
        o_cmp = o_cmp_g[hk]
        o_slc = acc_ref[hk] / l_ref[hk]
        kw = kvw_d[hk, pl.ds(win_start, WIN_KEYS), 0:KV_D]
        vw = kvw_d[hk, pl.ds(win_start, WIN_KEYS), KV_D:2 * KV_D]
        sw = lax.dot_general(qrs_ref[hk], kw, (((1,), (1,)), ((), ())), preferred_element_type=F32)
        sw = sw.reshape(hpg, tq, WIN_KEYS) + win_bias[None]
        pw = jnp.exp(sw - jnp.max(sw, axis=-1, keepdims=True))
        den = jnp.sum(pw, axis=-1, keepdims=True)
        o_win = jnp.dot(pw.reshape(hpg * tq, WIN_KEYS).astype(BF16), vw,
                        preferred_element_type=F32).reshape(hpg, tq, LANES) / den

        for hh in range(hpg):
            head = hk * hpg + hh
            p, e = head // 2, head % 2
            c0 = GATE_COL0 + head * 3
            mix = (sig[:, c0:c0 + 1] * o_cmp[hh] + sig[:, c0 + 1:c0 + 2] * o_slc[hh]
                   + sig[:, c0 + 2:c0 + 3] * o_win[hh])
            if e == 0:
                mix_even = mix
            else:
                o_ref[:, p * LANES:(p + 1) * LANES] = jnp.where(half_mask[0], mix_even, mix).astype(o_ref.dtype)


def _nsa_prompt(qc, qr, dtg, kvcmp, kvs, kvw, bn, t):
    nq = t // Q_BLOCK
    nbk = t // SEL_BLOCK
    hpg = NSA_HEADS // NSA_KV_HEADS
    assert nbk >= TOP_N and t >= WIN_KEYS and t % SEL_KEY_TILE == 0
    qrow = lambda w: pl.BlockSpec((Q_BLOCK, w), lambda b, i: (b * nq + i, 0))
    seq = lambda r: pl.BlockSpec((r, 2 * KV_D), lambda b, i: (b, 0))
    return pl.pallas_call(
        functools.partial(_nsa_prompt_kernel, t=t),
        grid=(bn, nq),
        in_specs=[qrow(NSA_D), qrow(NSA_D), qrow(LANES), seq(nbk), seq(t), seq(t)],
        out_specs=qrow(NSA_D),
        out_shape=jax.ShapeDtypeStruct((bn * t, NSA_D), BF16),
        scratch_shapes=[pltpu.VMEM((NSA_KV_HEADS, LANES, 2 * KV_D), BF16),
                        pltpu.VMEM((NSA_KV_HEADS, t, 2 * KV_D), BF16),
                        pltpu.VMEM((NSA_KV_HEADS, t, 2 * KV_D), BF16),
                        pltpu.VMEM((LANES, t), BF16),
                        pltpu.VMEM((NSA_KV_HEADS, t // SEL_KEY_TILE, Q_BLOCK, SEL_KEY_TILE), F32),
                        pltpu.VMEM((NSA_KV_HEADS, hpg * Q_BLOCK, LANES), BF16),
                        pltpu.VMEM((NSA_KV_HEADS, hpg, Q_BLOCK, LANES), F32),
                        pltpu.VMEM((NSA_KV_HEADS, hpg, Q_BLOCK, LANES), F32),
                        pltpu.VMEM((NSA_KV_HEADS, hpg, Q_BLOCK, LANES), F32)],
        compiler_params=_cparams(("parallel", "arbitrary")),
        name="nsa_prompt",
    )(qc, qr, dtg, kvcmp, kvs, kvw)


def _outproj_kernel(ys_ref, yn_ref, h_ref, ws_ref, wn_ref, g_ref, b_ref, o_ref):
    mix = jnp.dot(ys_ref[...].astype(BF16), ws_ref[...], preferred_element_type=F32)
    mix = mix + jnp.dot(yn_ref[...].astype(BF16), wn_ref[...], preferred_element_type=F32)
    o_ref[...] = _layer_norm(DEEPNORM_ALPHA * h_ref[...] + mix, g_ref[...], b_ref[...])


def _outproj(y_ssd, y_nsa, h, w_ssd, w_nsa, ln_g, ln_b, tm):
    n = h.shape[0]
    row = lambda w: pl.BlockSpec((tm, w), lambda i: (i, 0))
    const = lambda a: pl.BlockSpec(a.shape, lambda i: (0,) * a.ndim)
    return pl.pallas_call(
        _outproj_kernel,
        grid=(n // tm,),
        in_specs=[row(SSD_D), row(NSA_D), row(D_MODEL), const(w_ssd), const(w_nsa), const(ln_g), const(ln_b)],
        out_specs=row(D_MODEL),
        out_shape=jax.ShapeDtypeStruct((n, D_MODEL), F32),
        compiler_params=_cparams(("parallel",)),
        name="outproj",
    )(y_ssd, y_nsa, h, w_ssd, w_nsa, ln_g, ln_b)


MOE_TOKENS = 256
ROUTE_ROWS = 8


def _token_tile_specs(n_main_tiles):
    main = pl.BlockSpec((MOE_TOKENS, D_MODEL), lambda i, *_: (jnp.minimum(i, n_main_tiles - 1), 0))
    tail = pl.BlockSpec((MOE_TOKENS, D_MODEL), lambda i, *_: (0, 0))
    return main, tail


def _token_tile(i, n_main_tiles, main_ref, tail_ref):
    return jnp.where(i < n_main_tiles, main_ref[...], tail_ref[...])


def _rank_rows(x):
    n = x.shape[0]
    idx = lax.broadcasted_iota(jnp.int32, x.shape, 0)
    rank = jnp.zeros(x.shape, F32)
    for r in range(n):
        row = x[r:r + 1, :]
        rank = rank + ((row > x) | ((row == x) & (idx > r))).astype(F32)
    return rank


def _route_kernel(h_ref, ht_ref, rw_ref, rb_ref, slot_ref, tokinfo_ref, meta_ref, cnt_ref, carry_ref, carry_row_ref, *,
                  n_valid, n_main):
    i = pl.program_id(0)
    tm = MOE_TOKENS

    @pl.when(i == 0)
    def _():
        carry_ref[...] = jnp.zeros_like(carry_ref)
        carry_row_ref[...] = jnp.zeros_like(carry_row_ref)

    logits = lax.dot_general(rw_ref[...], _token_tile(i, n_main, h_ref, ht_ref).astype(BF16), (((1,), (1,)), ((), ())),
                             preferred_element_type=F32)
    scores = _sigmoid(logits)
    biased = scores + rb_ref[:, 0:1]
    b3 = biased.reshape(N_EXPERT_GROUPS, EXPERTS_PER_GROUP, tm)
    sidx = lax.broadcasted_iota(jnp.int32, b3.shape, 1)
    m1 = jnp.max(b3, axis=1, keepdims=True)
    first = jnp.min(jnp.where(b3 == m1, sidx, EXPERTS_PER_GROUP), axis=1, keepdims=True)
    m2 = jnp.max(jnp.where(sidx == first, -jnp.inf, b3), axis=1, keepdims=True)
    grp_score = (m1 + m2).reshape(N_EXPERT_GROUPS, tm)
    grp_keep = _rank_rows(grp_score) < TOPK_GROUPS
    masked = jnp.where(grp_keep.reshape(N_EXPERT_GROUPS, 1, tm), b3, NEG).reshape(N_EXPERTS, tm)
    rank = _rank_rows(masked)
    tok = i * tm + lax.broadcasted_iota(jnp.int32, (1, tm), 1)
    valid = tok < n_valid
    sel = (rank < TOP_K) & valid
    self32 = sel.astype(F32)
    wsel = self32 * scores
    wsum = jnp.sum(wsel, axis=0, keepdims=True)
    w = wsel / jnp.where(wsum > 0.0, wsum, 1.0) * ROUTED_SCALE

    selb = sel.astype(BF16)
    tri = lambda n, strict_upper: (
        (lax.broadcasted_iota(jnp.int32, (n, n), 0) < lax.broadcasted_iota(jnp.int32, (n, n), 1))
        if strict_upper else
        (lax.broadcasted_iota(jnp.int32, (n, n), 0) > lax.broadcasted_iota(jnp.int32, (n, n), 1))).astype(BF16)
    pad8 = lambda c: jnp.floor((c + (SUBLANES - 1.0)) * (1.0 / SUBLANES)) * SUBLANES
    pos_tile = jnp.dot(selb, tri(tm, True), preferred_element_type=F32)
    cnt_col = pad8(jnp.sum(self32, axis=1, keepdims=True))
    first_col = jnp.dot(tri(N_EXPERTS, False), jnp.broadcast_to(cnt_col, (N_EXPERTS, LANES)).astype(BF16),
                        preferred_element_type=F32)[:, 0:1]
    slot = first_col + pos_tile

    sel_pad = jnp.concatenate([selb, jnp.zeros((LANES - N_EXPERTS, tm), BF16)], axis=0)
    cnt_row = pad8(lax.dot_general(jnp.ones((SUBLANES, tm), BF16), sel_pad, (((1,), (1,)), ((), ())),
                                   preferred_element_type=F32))
    first_row = jnp.dot(cnt_row.astype(BF16), tri(LANES, True), preferred_element_type=F32)
    prev_row = carry_row_ref[...]
    meta = jnp.concatenate([cnt_row[0:1], first_row[0:1], prev_row[0:1], jnp.zeros((SUBLANES - 3, LANES), F32)], 0)
    meta_ref[0] = meta.astype(jnp.int32)
    carry_row_ref[...] = prev_row + cnt_row
    carry_ref[...] = carry_ref[...] + cnt_col

    slot_rows, w_rows = [], []
    for k in range(TOP_K):
        hit = (rank == k) & sel
        slot_rows.append(jnp.sum(jnp.where(hit, slot, 0.0), axis=0, keepdims=True))
        w_rows.append(jnp.sum(jnp.where(hit, w, 0.0), axis=0, keepdims=True))
    slot_rows = [jnp.where(valid, r, -1.0) for r in slot_rows]
    pad2 = jnp.zeros((ROUTE_ROWS - TOP_K, tm), F32)
    slot_ref[...] = jnp.concatenate(slot_rows + [pad2 - 1.0], 0).astype(jnp.int32)
    info = jnp.concatenate(w_rows + [pad2] + slot_rows + [jnp.zeros((LANES - ROUTE_ROWS - TOP_K, tm), F32)], 0)
    tokinfo_ref[...] = info.T

    @pl.when(i == pl.num_programs(0) - 1)
    def _():
        cnt_ref[...] = jnp.broadcast_to(carry_ref[:, 0:1], cnt_ref.shape)


def _route(h_main, h_tail, router_wt, router_bias_col, n_valid):
    tm = MOE_TOKENS
    n_main = h_main.shape[0] // tm
    n = h_main.shape[0] + tm
    const = lambda a: pl.BlockSpec(a.shape, lambda i: (0,) * a.ndim)
    return pl.pallas_call(
        functools.partial(_route_kernel, n_valid=n_valid, n_main=n_main),
        grid=(n // tm,),
        in_specs=[*_token_tile_specs(n_main), const(router_wt), const(router_bias_col)],
        out_specs=[pl.BlockSpec((ROUTE_ROWS, tm), lambda i: (0, i)),
                   pl.BlockSpec((tm, LANES), lambda i: (i, 0)),
                   pl.BlockSpec((1, SUBLANES, LANES), lambda i: (i, 0, 0)),
                   pl.BlockSpec((N_EXPERTS, LANES), lambda i: (0, 0))],
        out_shape=[jax.ShapeDtypeStruct((ROUTE_ROWS, n), jnp.int32),
                   jax.ShapeDtypeStruct((n, LANES), F32),
                   jax.ShapeDtypeStruct((n // tm, SUBLANES, LANES), jnp.int32),
                   jax.ShapeDtypeStruct((N_EXPERTS, LANES), F32)],
        scratch_shapes=[pltpu.VMEM((N_EXPERTS, LANES), F32), pltpu.VMEM((SUBLANES, LANES), F32)],
        compiler_params=_cparams(("arbitrary",)),
        name="moe_route",
    )(h_main, h_tail, router_wt, router_bias_col)


PACKED_D = D_MODEL // 2
U32 = jnp.uint32


def _pack_bf16_pairs(x):
    hi = lax.bitcast_convert_type(x[:, 0:PACKED_D], U32) & jnp.uint32(0xFFFF0000)
    lo = lax.shift_right_logical(lax.bitcast_convert_type(x[:, PACKED_D:D_MODEL], U32), jnp.uint32(16))
    return hi | lo


def _unpack_bf16_pairs(w):
    hi = lax.bitcast_convert_type(w & jnp.uint32(0xFFFF0000), F32)
    lo = lax.bitcast_convert_type(lax.shift_left(w, jnp.uint32(16)), F32)
    return jnp.concatenate([hi, lo], axis=1).astype(BF16)


def _round_bf16(x):
    return x.astype(BF16).astype(F32)


TILE_SLOTS = MOE_TOKENS * TOP_K + N_EXPERTS * SUBLANES
RUN_CHUNKS = tuple(1 << b for b in range(int(math.log2(MOE_TOKENS)), int(math.log2(SUBLANES)) - 1, -1))


def _run_copy(src_ref, src_row, dst_ref, dst_row, rows, sem):
    return pltpu.make_async_copy(src_ref.at[pl.ds(pl.multiple_of(src_row, SUBLANES), rows)],
                                 dst_ref.at[pl.ds(pl.multiple_of(dst_row, SUBLANES), rows)], sem)


def _start_run(src_ref, src_row, dst_ref, dst_row, n, sem, started):
    off = jnp.int32(0)
    out = []
    for c, rows in enumerate(RUN_CHUNKS):
        take = (n & rows) != 0

        @pl.when(take)
        def _(off=off, rows=rows):
            _run_copy(src_ref, src_row + off, dst_ref, dst_row + off, rows, sem).start()

        inc = take.astype(jnp.int32)
        off = off + inc * rows
        out.append(started[c] + inc)
    return tuple(out)


def _wait_runs(src_ref, dst_ref, sem, started):
    for c, rows in enumerate(RUN_CHUNKS):
        def wait_one(j, carry, rows=rows):
            _run_copy(src_ref, 0, dst_ref, 0, rows, sem).wait()
            return carry

        lax.fori_loop(0, started[c], wait_one, 0)


def _dispatch_kernel(start_ref, cnt_ref, meta_ref, slot_ref, x_ref, xt_ref, xs_ref, sorted_ref, zero_ref, sem, zsem, *,
                     cap, n_main):
    i = pl.program_id(0)
    tm = MOE_TOKENS

    @pl.when(i == 0)
    def _():
        zero_ref[...] = jnp.zeros_like(zero_ref)

        def fill_expert(e, started):
            lo = start_ref[e] + cnt_ref[e]
            hi = jnp.where(e == N_EXPERTS - 1, cap, start_ref[jnp.minimum(e + 1, N_EXPERTS - 1)])
            n_full = (hi - lo) // tm

            def fill_full(j, st):
                return _start_run(zero_ref, 0, xs_ref, lo + j * tm, jnp.int32(tm), zsem, st)

            started = lax.fori_loop(0, n_full, fill_full, started)
            return _start_run(zero_ref, 0, xs_ref, lo + n_full * tm, (hi - lo) - n_full * tm, zsem, started)

        filled = lax.fori_loop(0, N_EXPERTS, fill_expert, tuple(jnp.int32(0) for _ in RUN_CHUNKS))
        _wait_runs(zero_ref, xs_ref, zsem, filled)

    srow = lax.broadcasted_iota(jnp.int32, (TILE_SLOTS, tm), 0)
    onehot = srow == slot_ref[0:1, :]
    for k in range(1, TOP_K):
        onehot = onehot | (srow == slot_ref[k:k + 1, :])
    sorted_ref[...] = _pack_bf16_pairs(jnp.dot(onehot.astype(BF16), _token_tile(i, n_main, x_ref, xt_ref).astype(BF16),
                                               preferred_element_type=F32))

    def copy_expert(e, started):
        n = meta_ref[0, 0, e]
        return _start_run(sorted_ref, meta_ref[0, 1, e], xs_ref, start_ref[e] + meta_ref[0, 2, e], n, sem, started)

    started = lax.fori_loop(0, N_EXPERTS, copy_expert, tuple(jnp.int32(0) for _ in RUN_CHUNKS))
    _wait_runs(sorted_ref, xs_ref, sem, started)


def _dispatch(h_main, h_tail, slot_t, meta, seg_start, counts, cap):
    tm = MOE_TOKENS
    n_main = h_main.shape[0] // tm
    return pl.pallas_call(
        functools.partial(_dispatch_kernel, cap=cap, n_main=n_main),
        grid_spec=pltpu.PrefetchScalarGridSpec(
            num_scalar_prefetch=2,
            grid=(n_main + 1,),
            in_specs=[pl.BlockSpec((1, SUBLANES, LANES), lambda i, *_: (i, 0, 0), memory_space=pltpu.SMEM),
                      pl.BlockSpec((ROUTE_ROWS, tm), lambda i, *_: (0, i)),
                      *_token_tile_specs(n_main)],
            out_specs=pl.BlockSpec(memory_space=pl.ANY),
            scratch_shapes=[pltpu.VMEM((TILE_SLOTS, PACKED_D), U32), pltpu.VMEM((tm, PACKED_D), U32),
                            pltpu.SemaphoreType.DMA, pltpu.SemaphoreType.DMA]),
        out_shape=jax.ShapeDtypeStruct((cap, PACKED_D), U32),
        compiler_params=_cparams(("arbitrary",)),
        name="moe_dispatch",
    )(seg_start, counts, meta, slot_t, h_main, h_tail)


def _swiglu(x, wg, wu, wd):
    xb = x.astype(BF16)
    g = jnp.dot(xb, wg.astype(BF16), preferred_element_type=F32)
    u = jnp.dot(xb, wu.astype(BF16), preferred_element_type=F32)
    return jnp.dot((_silu(g) * u).astype(BF16), wd.astype(BF16), preferred_element_type=F32)


EXPERT_RING = 3


def _experts_kernel(be_ref, used_ref, xs_ref, wg_ref, wu_ref, wd_ref, y_ref, xbuf, sems):
    i = pl.program_id(0)
    n = pl.num_programs(0)

    def block_copy(blk):
        slot = lax.rem(blk, EXPERT_RING)
        rows = pl.ds(pl.multiple_of(blk * MOE_BLOCK, MOE_BLOCK), MOE_BLOCK)
        return pltpu.make_async_copy(xs_ref.at[rows], xbuf.at[slot], sems.at[slot])

    @pl.when(i == 0)
    def _():
        for j in range(EXPERT_RING - 1):
            block_copy(jnp.int32(j)).start()

    @pl.when(i + EXPERT_RING - 1 < n)
    def _():
        block_copy(i + EXPERT_RING - 1).start()

    block_copy(i).wait()

    @pl.when(i < used_ref[0])
    def _():
        y = _swiglu(_unpack_bf16_pairs(xbuf[lax.rem(i, EXPERT_RING)]), wg_ref[0], wu_ref[0], wd_ref[0])
        y_ref[...] = _pack_bf16_pairs(_round_bf16(y))

    @pl.when(i >= used_ref[0])
    def _():
        y_ref[...] = jnp.zeros_like(y_ref)


def _experts(xs, block_expert, used_blocks, w_gate, w_up, w_down):
    cap = xs.shape[0]
    return pl.pallas_call(
        _experts_kernel,
        grid_spec=pltpu.PrefetchScalarGridSpec(
            num_scalar_prefetch=2,
            grid=(cap // MOE_BLOCK,),
            in_specs=[pl.BlockSpec(memory_space=pl.ANY),
                      pl.BlockSpec((1, D_MODEL, D_EXPERT), lambda i, be, used: (be[i], 0, 0)),
                      pl.BlockSpec((1, D_MODEL, D_EXPERT), lambda i, be, used: (be[i], 0, 0)),
                      pl.BlockSpec((1, D_EXPERT, D_MODEL), lambda i, be, used: (be[i], 0, 0))],
            out_specs=pl.BlockSpec((MOE_BLOCK, PACKED_D), lambda i, be, used: (i, 0)),
            scratch_shapes=[pltpu.VMEM((EXPERT_RING, MOE_BLOCK, PACKED_D), U32),
                            pltpu.SemaphoreType.DMA((EXPERT_RING,))]),
        out_shape=jax.ShapeDtypeStruct((cap, PACKED_D), U32),
        compiler_params=_cparams(("arbitrary",)),
        name="moe_experts",
    )(block_expert, used_blocks, xs, w_gate, w_up, w_down)


def _combine_kernel(start_ref, meta_ref, h_ref, ht_ref, info_ref, sg_ref, su_ref, sd_ref, g_ref, b_ref,
                    ys_ref, o_ref, ot_ref, buf_ref, sem, *, n_main):
    i = pl.program_id(0)
    tm = MOE_TOKENS

    @pl.when(i == 0)
    def _():
        buf_ref[...] = jnp.zeros_like(buf_ref)

    def fetch_expert(e, started):
        n = meta_ref[0, 0, e]
        return _start_run(ys_ref, start_ref[e] + meta_ref[0, 2, e], buf_ref, meta_ref[0, 1, e], n, sem, started)

    started = lax.fori_loop(0, N_EXPERTS, fetch_expert, tuple(jnp.int32(0) for _ in RUN_CHUNKS))
    h = _token_tile(i, n_main, h_ref, ht_ref)
    f = _swiglu(h, sg_ref[...], su_ref[...], sd_ref[...])
    info = info_ref[...]
    scol = lax.broadcasted_iota(jnp.int32, (tm, TILE_SLOTS), 1).astype(F32)
    mix = jnp.zeros((tm, TILE_SLOTS), F32)
    for k in range(TOP_K):
        mix = jnp.where(info[:, ROUTE_ROWS + k:ROUTE_ROWS + k + 1] == scol, info[:, k:k + 1], mix)
    _wait_runs(ys_ref, buf_ref, sem, started)
    acc = jnp.dot(mix.astype(BF16), _unpack_bf16_pairs(buf_ref[...]), preferred_element_type=F32)
    out = _layer_norm(DEEPNORM_ALPHA * h + (acc + f), g_ref[...], b_ref[...])

    @pl.when(i < n_main)
    def _():
        o_ref[...] = out

    @pl.when(i >= n_main)
    def _():
        ot_ref[...] = out


def _combine(h_main, h_tail, ys, meta, tokinfo, seg_start, sh_gate, sh_up, sh_down, ln_g, ln_b):
    tm = MOE_TOKENS
    n_main = h_main.shape[0] // tm
    const = lambda a: pl.BlockSpec(a.shape, lambda i, *_: (0,) * a.ndim)
    return pl.pallas_call(
        functools.partial(_combine_kernel, n_main=n_main),
        grid_spec=pltpu.PrefetchScalarGridSpec(
            num_scalar_prefetch=1,
            grid=(n_main + 1,),
            in_specs=[pl.BlockSpec((1, SUBLANES, LANES), lambda i, *_: (i, 0, 0), memory_space=pltpu.SMEM),
                      *_token_tile_specs(n_main),
                      pl.BlockSpec((tm, LANES), lambda i, *_: (i, 0)),
                      const(sh_gate), const(sh_up), const(sh_down), const(ln_g), const(ln_b),
                      pl.BlockSpec(memory_space=pl.ANY)],
            out_specs=list(_token_tile_specs(n_main)),
            scratch_shapes=[pltpu.VMEM((TILE_SLOTS, PACKED_D), U32), pltpu.SemaphoreType.DMA]),
        out_shape=[jax.ShapeDtypeStruct(h_main.shape, F32), jax.ShapeDtypeStruct((tm, D_MODEL), F32)],
        compiler_params=_cparams(("arbitrary",)),
        name="moe_combine",
    )(seg_start, meta, h_main, h_tail, tokinfo, sh_gate, sh_up, sh_down, ln_g, ln_b, ys)


def _moe_ln(h_main, h_tail, n_valid, router_w, router_bias, w_gate, w_up, w_down, sh_gate, sh_up, sh_down, ln_g, ln_b):
    n_tiles = h_main.shape[0] // MOE_TOKENS + 1
    slot_t, tokinfo, meta, cnt = _route(h_main, h_tail, router_w.T.astype(BF16),
                                        jnp.broadcast_to(router_bias.astype(F32)[:, None], (N_EXPERTS, LANES)), n_valid)
    counts = cnt[:, 0].astype(jnp.int32)
    padded = (counts + MOE_BLOCK - 1) // MOE_BLOCK * MOE_BLOCK
    seg_end = jnp.cumsum(padded)
    seg_start = seg_end - padded
    run_pad = n_tiles * N_EXPERTS * (SUBLANES - 1)
    n_blocks = -(-(n_valid * TOP_K + run_pad + N_EXPERTS * (MOE_BLOCK - 1)) // MOE_BLOCK)
    cap = n_blocks * MOE_BLOCK
    block_first_row = jnp.arange(n_blocks, dtype=jnp.int32) * MOE_BLOCK
    block_expert = jnp.minimum(jnp.sum((seg_end[None, :] <= block_first_row[:, None]).astype(jnp.int32), axis=1),
                               N_EXPERTS - 1)
    xs = _dispatch(h_main, h_tail, slot_t, meta, seg_start, counts, cap)
    used_blocks = (seg_end[N_EXPERTS - 1:] // MOE_BLOCK).astype(jnp.int32)
    ys = _experts(xs, block_expert, used_blocks, w_gate, w_up, w_down)
    return _combine(h_main, h_tail, ys, meta, tokinfo, seg_start, sh_gate.astype(BF16), sh_up.astype(BF16),
                    sh_down.astype(BF16), ln_g, ln_b)


def _ssd_sample_kernel(xbc_ref, z_ref, dtg_ref, sconv_ref, s0_ref, convw_ref, convb_ref, dtb_ref, alog_ref,
                       dskip_ref, normw_ref, y_ref, s_ref, conv_out_ref, xc_ref, dt_ref, da_ref):
    b = pl.program_id(0)

    @pl.when(b == 0)
    def _():
        xin = xbc_ref[...]
        xc = convw_ref[SSD_CONV - 1:SSD_CONV, :] * xin
        for k in range(SSD_CONV - 1):
            xc = xc + convw_ref[k:k + 1, :] * sconv_ref[k]
        xc_ref[...] = _silu(xc + convb_ref[...])
        dt = _softplus(dtg_ref[...] + dtb_ref[...])
        dt_ref[...] = dt
        da_ref[...] = jnp.exp(dt * (-jnp.exp(alog_ref[...])))
        for k in range(SSD_CONV - 2):
            conv_out_ref[k] = sconv_ref[k + 1]
        conv_out_ref[SSD_CONV - 2] = xin

    xc = xc_ref[pl.ds(b, 1), :]
    dt = dt_ref[pl.ds(b, 1), :]
    da = da_ref[pl.ds(b, 1), :]
    ns = SSD_GROUPS * SSD_STATE
    eye = (lax.broadcasted_iota(jnp.int32, (HEAD_DIM, HEAD_DIM), 0)
           == lax.broadcasted_iota(jnp.int32, (HEAD_DIM, HEAD_DIM), 1))
    hpg = SSD_HEADS // SSD_GROUPS
    y_parts = []
    for h in range(SSD_HEADS):
        g = h // hpg
        x_h = xc[:, h * HEAD_DIM:(h + 1) * HEAD_DIM]
        b_g = xc[:, SSD_D + g * SSD_STATE:SSD_D + (g + 1) * SSD_STATE]
        c_g = xc[:, SSD_D + ns + g * SSD_STATE:SSD_D + ns + (g + 1) * SSD_STATE]
        xdt_col = jnp.sum(jnp.where(eye, x_h * dt[:, h:h + 1], 0.0), axis=1, keepdims=True)
        s_new = da[:, h:h + 1] * s0_ref[0, h] + xdt_col * b_g
        s_ref[0, h] = s_new
        y_h = _bdot_nt(c_g, s_new) + dskip_ref[:, h * HEAD_DIM:(h + 1) * HEAD_DIM] * x_h
        y_parts.append(y_h)
    y = jnp.concatenate(y_parts, axis=1)
    y_ref[pl.ds(b, 1), :] = _gated_group_norm(y, z_ref[pl.ds(b, 1), :], normw_ref[...])


def _ssd_sample(xbc, z, dtg, state_conv_t, state_ssm, conv_w, conv_b, dt_bias_pad, a_log_pad, d_skip_full, norm_w):
    bs = xbc.shape[0]
    const = lambda a: pl.BlockSpec(a.shape, lambda b: (0,) * a.ndim)
    state_spec = pl.BlockSpec((1, SSD_HEADS, HEAD_DIM, SSD_STATE), lambda b: (b, 0, 0, 0))
    return pl.pallas_call(
        _ssd_sample_kernel,
        grid=(bs,),
        in_specs=[const(xbc), const(z), const(dtg), const(state_conv_t), state_spec, const(conv_w), const(conv_b),
                  const(dt_bias_pad), const(a_log_pad), const(d_skip_full), const(norm_w)],
        out_specs=[pl.BlockSpec((bs, SSD_D), lambda b: (0, 0)), state_spec,
                   pl.BlockSpec((SSD_CONV - 1, bs, SSD_CONV_CH), lambda b: (0, 0, 0))],
        out_shape=[jax.ShapeDtypeStruct((bs, SSD_D), F32),
                   jax.ShapeDtypeStruct(state_ssm.shape, F32),
                   jax.ShapeDtypeStruct((SSD_CONV - 1, bs, SSD_CONV_CH), F32)],
        scratch_shapes=[pltpu.VMEM((bs, SSD_CONV_CH), F32), pltpu.VMEM((bs, LANES), F32),
                        pltpu.VMEM((bs, LANES), F32)],
        compiler_params=_cparams(("arbitrary",)),
        name="ssd_sample",
    )(xbc, z, dtg, state_conv_t, state_ssm, conv_w, conv_b, dt_bias_pad, a_log_pad, d_skip_full, norm_w)


SEL_PAST = TOP_N - 1
BLOCKS_PER_PAGE = PAGE_SIZE // CMP_BLOCK
KV_FEATS = 2 * KV_D


def _compress_consts_t(cmp_pe, cmp_w1, cmp_b1, cmp_w2, cmp_b2):
    pe_t = jnp.stack([jnp.tile(cmp_pe[k].T, (1, BLOCKS_PER_PAGE)) for k in range(2)])
    w1_t = jnp.stack([_block_diag2(jnp.swapaxes(cmp_w1[k], 0, 1)) for k in range(2)]).astype(BF16)
    b1_t = jnp.stack([jnp.tile(cmp_b1[k], BLOCKS_PER_PAGE) for k in range(2)])[:, None, :]
    w2_t = jnp.stack([_block_diag2(cmp_w2[k]) for k in range(2)]).astype(BF16)
    b2_t = jnp.stack([jnp.tile(cmp_b2[k], BLOCKS_PER_PAGE) for k in range(2)])[:, None, :]
    return pe_t, w1_t, b1_t, w2_t, b2_t


def _compress_pages_kernel(pt_ref, pe_ref, w1_ref, b1_ref, w2_ref, b2_ref, pool_ref, o_ref, kbuf, vbuf, sems, *,
                           n_pages):
    b = pl.program_id(0)
    nb = pl.num_programs(0)
    bufs = (kbuf, vbuf)

    def half_copy(seq, kind, p):
        return pltpu.make_async_copy(pool_ref.at[pt_ref[seq, p], pl.ds(kind * KV_D, KV_D)],
                                     bufs[kind].at[:, p], sems.at[kind])

    def start_half(seq, kind):
        def start_pair(q, c):
            half_copy(seq, kind, 2 * q).start(priority=0)
            half_copy(seq, kind, 2 * q + 1).start(priority=1)
            return c

        lax.fori_loop(0, n_pages // 2, start_pair, 0)

    def wait_half(seq, kind):
        lax.fori_loop(0, n_pages, lambda p, c: (half_copy(seq, kind, p).wait(), c)[1], 0)

    @pl.when(b == 0)
    def _():
        start_half(b, 0)
        start_half(b, 1)

    for kind in range(2):
        wait_half(b, kind)
        def add_feature(d, acc, kind=kind):
            x = jnp.concatenate([bufs[kind][h * HEAD_DIM + d] for h in range(NSA_KV_HEADS)], axis=0) \
                + pe_ref[kind, pl.ds(d, 1), :]
            return acc + jnp.dot(x.astype(BF16), w1_ref[kind, d], preferred_element_type=F32)

        acc = lax.fori_loop(0, HEAD_DIM, add_feature,
                            jnp.zeros((NSA_KV_HEADS * n_pages, BLOCKS_PER_PAGE * CMP_HIDDEN), F32), unroll=8)
        hid = _silu(acc + b1_ref[kind])
        out = jnp.dot(hid.astype(BF16), w2_ref[kind], preferred_element_type=F32) + b2_ref[kind]
        for h in range(NSA_KV_HEADS):
            o_ref[0, kind * NSA_KV_HEADS + h] = out[h * n_pages:(h + 1) * n_pages]

        @pl.when(b + 1 < nb)
        def _(kind=kind):
            start_half(b + 1, kind)


def _compress_pages(pool_t, page_table, consts):
    bs, n_pages = page_table.shape
    const = lambda a: pl.BlockSpec(a.shape, lambda b, pt: (0,) * a.ndim)
    return pl.pallas_call(
        functools.partial(_compress_pages_kernel, n_pages=n_pages),
        grid_spec=pltpu.PrefetchScalarGridSpec(
            num_scalar_prefetch=1,
            grid=(bs,),
            in_specs=[const(a) for a in consts] + [pl.BlockSpec(memory_space=pl.ANY)],
            out_specs=pl.BlockSpec((1, 2 * NSA_KV_HEADS, n_pages, LANES), lambda b, pt: (b, 0, 0, 0)),
            scratch_shapes=[pltpu.VMEM((KV_D, n_pages, PAGE_SIZE), F32), pltpu.VMEM((KV_D, n_pages, PAGE_SIZE), F32),
                            pltpu.SemaphoreType.DMA((2,))]),
        out_shape=jax.ShapeDtypeStruct((bs, 2 * NSA_KV_HEADS, n_pages, LANES), F32),
        compiler_params=_cparams(("arbitrary",)),
        name="compress_pages",
    )(page_table, *consts, pool_t)


def _group_heads(q_row, hk):
    hpg = NSA_HEADS // NSA_KV_HEADS
    low = lax.broadcasted_iota(jnp.int32, (1, LANES), 1) < HEAD_DIM
    rows = []
    for r in range(hpg):
        head = hk * hpg + r
        tile = q_row[:, (head // 2) * LANES:(head // 2 + 1) * LANES]
        if head % 2 == 1:
            tile = pltpu.roll(tile, HEAD_DIM, 1)
        rows.append(jnp.where(low, tile, 0.0))
    return jnp.concatenate(rows + [jnp.zeros((SUBLANES - hpg, LANES), F32)], axis=0)


def _spread_heads(o_groups):
    hpg = NSA_HEADS // NSA_KV_HEADS
    return jnp.concatenate([o[r:r + 1, 0:HEAD_DIM] for o in o_groups for r in range(hpg)], axis=1)


def _nsa_sample_cmp_t_kernel(qc_ref, cmp_ref, ocmp_ref, idx_ref, *, n_pages):
    b = pl.program_id(0)
    nc = n_pages * BLOCKS_PER_PAGE
    scale = HEAD_DIM ** -0.5
    hpg = NSA_HEADS // NSA_KV_HEADS
    q_row = qc_ref[pl.ds(b, 1), :] * scale
    lane = lax.broadcasted_iota(jnp.int32, (1, LANES), 1)
    pos_r = lax.broadcasted_iota(jnp.int32, (1, nc), 1)
    bid_r = (pos_r % n_pages) * BLOCKS_PER_PAGE + pos_r // n_pages
    pos_c = lax.broadcasted_iota(jnp.int32, (nc, 1), 0)
    bid_c = (pos_c % n_pages) * BLOCKS_PER_PAGE + pos_c // n_pages
    o_groups = []
    for hk in range(NSA_KV_HEADS):
        kc = cmp_ref[0, hk].astype(BF16)
        vc = cmp_ref[0, NSA_KV_HEADS + hk].astype(BF16)
        qh = _group_heads(q_row, hk)
        s = jnp.concatenate(
            [lax.dot_general(pltpu.roll(qh, c * HEAD_DIM, 1).astype(BF16) if c else qh.astype(BF16), kc,
                             (((1,), (1,)), ((), ())), preferred_element_type=F32)
             for c in range(BLOCKS_PER_PAGE)], axis=1)
        ex = jnp.exp(s - jnp.max(s, axis=-1, keepdims=True))
        p = ex / jnp.sum(ex, axis=-1, keepdims=True)
        o = jnp.dot(p[:, 0:n_pages].astype(BF16), vc, preferred_element_type=F32)
        for c in range(1, BLOCKS_PER_PAGE):
            oc = jnp.dot(p[:, c * n_pages:(c + 1) * n_pages].astype(BF16), vc, preferred_element_type=F32)
            o = o + pltpu.roll(oc, LANES - c * HEAD_DIM, 1)
        o_groups.append(o)
        hrow = lax.broadcasted_iota(jnp.int32, p.shape, 0) < hpg
        imp = jnp.sum(jnp.where(hrow, p, 0.0), axis=0, keepdims=True)
        score = jnp.where((bid_r == 0) | (bid_r == nc - 1), FORCED_SCORE, imp)
        score_col = jnp.concatenate([score, jnp.zeros((LANES - 1, nc), F32)], 0).T[:, 0:1]
        beats = (score_col > score) | ((score_col == score) & (bid_c < bid_r))
        rank = jnp.sum(beats.astype(F32), axis=0, keepdims=True)
        row = jnp.zeros((1, LANES), F32)
        bid_f = bid_r.astype(F32)
        for k in range(SEL_PAST):
            blk = jnp.sum(jnp.where(rank == k, bid_f, 0.0), axis=1, keepdims=True)
            row = jnp.where(lane == k, blk, row)
        idx_ref[pl.ds(b * NSA_KV_HEADS + hk, 1), :] = row.astype(jnp.int32)
    ocmp_ref[pl.ds(b, 1), :] = _spread_heads(o_groups)


def _nsa_sample_cmp_t(qc, kvcmp_t):
    bs, _, n_pages, _ = kvcmp_t.shape
    return pl.pallas_call(
        functools.partial(_nsa_sample_cmp_t_kernel, n_pages=n_pages),
        grid=(bs,),
        in_specs=[pl.BlockSpec((bs, NSA_D), lambda b: (0, 0)),
                  pl.BlockSpec((1, 2 * NSA_KV_HEADS, n_pages, LANES), lambda b: (b, 0, 0, 0))],
        out_specs=[pl.BlockSpec((bs, NSA_D), lambda b: (0, 0)),
                   pl.BlockSpec((bs * NSA_KV_HEADS, LANES), lambda b: (0, 0))],
        out_shape=[jax.ShapeDtypeStruct((bs, NSA_D), F32),
                   jax.ShapeDtypeStruct((bs * NSA_KV_HEADS, LANES), jnp.int32)],
        compiler_params=_cparams(("arbitrary",)),
        name="nsa_sample_cmp",
    )(qc, kvcmp_t)


def _sel_block_copies(pool_ref, pt_ref, sel_ref, kbuf, vbuf, sem, b, hk, k):
    blk = sel_ref[b * NSA_KV_HEADS + hk, k]
    page = pt_ref[b, lax.shift_right_logical(blk, int(math.log2(BLOCKS_PER_PAGE)))]
    j = hk * SEL_PAST + k
    return (pltpu.make_async_copy(pool_ref.at[page, pl.ds(hk * HEAD_DIM, HEAD_DIM)], kbuf.at[j], sem),
            pltpu.make_async_copy(pool_ref.at[page, pl.ds(KV_D + hk * HEAD_DIM, HEAD_DIM)], vbuf.at[j], sem))


def _nsa_sample_attn_t_kernel(pt_ref, sel_ref, qr_ref, new_sel_ref, new_win_ref, win_ref, dtg_ref, ocmp_ref,
                              pool_ref, o_ref, kbuf, vbuf, sem):
    b = pl.program_id(0)
    for hk in range(NSA_KV_HEADS):
        for k in range(SEL_PAST):
            for cp in _sel_block_copies(pool_ref, pt_ref, sel_ref, kbuf, vbuf, sem, b, hk, k):
                cp.start()
    for hk in range(NSA_KV_HEADS):
        for k in range(SEL_PAST):
            for cp in _sel_block_copies(pool_ref, pt_ref, sel_ref, kbuf, vbuf, sem, b, hk, k):
                cp.wait()
    scale = HEAD_DIM ** -0.5
    q_row = qr_ref[pl.ds(b, 1), :] * scale
    sig = _sigmoid(dtg_ref[pl.ds(b, 1), :])
    lane = lax.broadcasted_iota(jnp.int32, (1, PAGE_SIZE), 1)
    o_slc, o_win = [], []
    for hk in range(NSA_KV_HEADS):
        qh = _group_heads(q_row, hk)[:, 0:HEAD_DIM].astype(BF16)

        def new_row(ref, kind):
            t = ref[pl.ds(b, 1), :][:, kind * KV_D:(kind + 1) * KV_D]
            if hk == 1:
                t = pltpu.roll(t, HEAD_DIM, 1)
            return t[:, 0:HEAD_DIM].astype(BF16).astype(F32)

        def attend(kt, vt, mask, new_ref, n_new):
            s = jnp.dot(qh, kt.astype(BF16), preferred_element_type=F32)
            if mask is not None:
                s = jnp.where(mask, s, NEG)
            s_new = jnp.sum(qh.astype(F32) * new_row(new_ref, 0), axis=1, keepdims=True)
            m = jnp.maximum(jnp.max(s, axis=-1, keepdims=True), s_new)
            ex = jnp.exp(s - m)
            ex_new = jnp.exp(s_new - m) * n_new
            den = jnp.sum(ex, axis=-1, keepdims=True) + ex_new
            o = lax.dot_general((ex / den).astype(BF16), vt.astype(BF16), (((1,), (1,)), ((), ())),
                                preferred_element_type=F32)
            return o + (ex_new / den).astype(BF16).astype(F32) * new_row(new_ref, 1)

        kt = jnp.concatenate([kbuf[hk * SEL_PAST + k] for k in range(SEL_PAST)], axis=1)
        vt = jnp.concatenate([vbuf[hk * SEL_PAST + k] for k in range(SEL_PAST)], axis=1)
        mask = jnp.concatenate(
            [lane // SEL_BLOCK == (sel_ref[b * NSA_KV_HEADS + hk, k] & (BLOCKS_PER_PAGE - 1))
             for k in range(SEL_PAST)], axis=1)
        o_slc.append(attend(kt, vt, mask, new_sel_ref, float(SEL_BLOCK)))
        o_win.append(attend(win_ref[0, hk * HEAD_DIM:(hk + 1) * HEAD_DIM, :],
                            win_ref[0, KV_D + hk * HEAD_DIM:KV_D + (hk + 1) * HEAD_DIM, :], None, new_win_ref, 1.0))
    gates = []
    for br in range(3):
        gates.append(jnp.concatenate(
            [jnp.broadcast_to(sig[:, GATE_COL0 + h * 3 + br:GATE_COL0 + h * 3 + br + 1], (1, HEAD_DIM))
             for h in range(NSA_HEADS)], axis=1))
    o_ref[pl.ds(b, 1), :] = (gates[0] * ocmp_ref[pl.ds(b, 1), :] + gates[1] * _spread_heads(o_slc)
                             + gates[2] * _spread_heads(o_win))


def _nsa_sample_attn_t(qr, new_sel, new_win, win_t, dtg, o_cmp, pool_sel_t, page_table, sel_idx):
    bs = qr.shape[0]
    const = lambda a: pl.BlockSpec(a.shape, lambda b, pt, sel: (0,) * a.ndim)
    n_buf = NSA_KV_HEADS * SEL_PAST
    return pl.pallas_call(
        _nsa_sample_attn_t_kernel,
        grid_spec=pltpu.PrefetchScalarGridSpec(
            num_scalar_prefetch=2,
            grid=(bs,),
            in_specs=[const(qr), const(new_sel), const(new_win),
                      pl.BlockSpec((1,) + win_t.shape[1:], lambda b, pt, sel: (b, 0, 0)),
                      const(dtg), const(o_cmp), pl.BlockSpec(memory_space=pl.ANY)],
            out_specs=pl.BlockSpec((bs, NSA_D), lambda b, pt, sel: (0, 0)),
            scratch_shapes=[pltpu.VMEM((n_buf, HEAD_DIM, PAGE_SIZE), F32), pltpu.VMEM((n_buf, HEAD_DIM, PAGE_SIZE), F32),
                            pltpu.SemaphoreType.DMA]),
        out_shape=jax.ShapeDtypeStruct((bs, NSA_D), F32),
        compiler_params=_cparams(("arbitrary",)),
        name="nsa_sample_attn",
    )(page_table, sel_idx, qr, new_sel, new_win, win_t, dtg, o_cmp, pool_sel_t)


def kernel(x_prompt, x_sample, cache_kv_cmp, cache_kv_sel, page_table, cache_kv_win, state_ssm, state_conv,
           emb_ln_g, emb_ln_b, w_in, conv_w, conv_b, dt_bias, a_log, d_skip, ssd_norm_w,
           cmp_pe, cmp_w1, cmp_b1, cmp_w2, cmp_b2, w_out, ln1_g, ln1_b,
           router_w, router_bias, exp_w_gate, exp_w_up, exp_w_down,
           sh_w_gate, sh_w_up, sh_w_down, ln2_g, ln2_b):
    bp, tp, _ = x_prompt.shape
    bs, ts, _ = x_sample.shape
    assert ts == 1 and DEPTH == 1
    n_prompt = bp * tp
    past_len = page_table.shape[1] * PAGE_SIZE
    l = 0
    w_perm = _permute_w_in(w_in[l])
    ln0_g, ln0_b = emb_ln_g[None], emb_ln_b[None]
    ssd_consts = (conv_w[l], conv_b[l][None], _pad_lanes(dt_bias[l]), _pad_lanes(a_log[l]),
                  jnp.repeat(d_skip[l], HEAD_DIM)[None], ssd_norm_w[l][None])
    cmp_consts = _compress_consts(cmp_pe[l], cmp_w1[l], cmp_b1[l], cmp_w2[l], cmp_b2[l])
    w_o = w_out[l].astype(BF16)
    w_o_ssd, w_o_nsa = w_o[:SSD_D], w_o[SSD_D:]
    ln1 = (ln1_g[l][None], ln1_b[l][None])
    kv_shape = (2, NSA_KV_HEADS, HEAD_DIM)

    hp, z, xbc, qc, qr, kvc, kvs, kvw, dtg, kvc_t, kvs_t, kvw_t = _inproj(
        x_prompt.reshape(n_prompt, D_MODEL), ln0_g, ln0_b, w_perm, _rope_tables(jnp.arange(tp)), 256,
        _rope_tables_t(jnp.arange(tp)))
    y_ssd, ssm_p, conv_p = _ssd_prompt(xbc, z, dtg, *ssd_consts, bp, tp)
    kvcmp = _compress_prompt(kvc, cmp_consts, tp)
    y_nsa = _nsa_prompt(qc, qr, dtg, kvcmp, kvs, kvw, bp, tp)
    h1p = _outproj(y_ssd, y_nsa, hp, w_o_ssd, w_o_nsa, *ln1, 512)
    n_keep = min(WINDOW, tp)
    cache_leaf = lambda a: jnp.transpose(a.reshape((bp,) + kv_shape + (a.shape[-1],)), (0, 4, 1, 2, 3))[None]
    kvc_p = cache_leaf(kvc_t)
    kvs_p = cache_leaf(kvs_t)
    kvw_p = cache_leaf(kvw_t[:, :, tp - n_keep:])

    s_hs, s_z, s_xbc, s_qc, s_qr, s_kvc, s_kvs, s_kvw, s_dtg = _inproj(
        x_sample.reshape(bs, D_MODEL), ln0_g, ln0_b, w_perm, _rope_tables(jnp.full((bs,), past_len)), bs)
    s_y_ssd, ssm_s, conv_s_t = _ssd_sample(s_xbc, s_z, s_dtg, jnp.swapaxes(state_conv[l], 0, 1), state_ssm[l],
                                           *ssd_consts)
    n_pool = cache_kv_cmp.shape[1]
    feature_major = lambda c, rows: jnp.swapaxes(c.reshape(-1, rows, 2 * KV_D), 1, 2)
    s_kvcmp = _compress_pages(feature_major(cache_kv_cmp[l], PAGE_SIZE), page_table,
                              _compress_consts_t(cmp_pe[l], cmp_w1[l], cmp_b1[l], cmp_w2[l], cmp_b2[l]))
    s_o_cmp, s_sel = _nsa_sample_cmp_t(s_qc, s_kvcmp)
    buf_win = cache_kv_win[l].reshape(bs, -1, 2 * KV_D)
    s_y_nsa = _nsa_sample_attn_t(
        s_qr, s_kvs, s_kvw, feature_major(cache_kv_win[l], buf_win.shape[1]), s_dtg, s_o_cmp,
        feature_major(cache_kv_sel[l], PAGE_SIZE), page_table, s_sel)
    h1s = _outproj(s_y_ssd, s_y_nsa, s_hs, w_o_ssd, w_o_nsa, *ln1, bs)
    win_all = jnp.concatenate([buf_win, s_kvw[:, None, :]], 1)
    n_keep_s = min(WINDOW, past_len + ts)
    kvw_s = win_all[:, win_all.shape[1] - n_keep_s:].reshape((1, bs, n_keep_s) + kv_shape)
    kvc_s = s_kvc.reshape((1, bs, ts) + kv_shape)
    kvs_s = s_kvs.reshape((1, bs, ts) + kv_shape)

    assert n_prompt % MOE_TOKENS == 0 and bs * ts <= MOE_TOKENS
    n_tok = n_prompt + bs * ts
    tail = jnp.concatenate([h1s, jnp.zeros((MOE_TOKENS - bs * ts, D_MODEL), F32)], 0)
    out_main, out_tail = _moe_ln(h1p, tail, n_tok, router_w[l], router_bias[l], exp_w_gate[l], exp_w_up[l],
                                 exp_w_down[l], sh_w_gate[l], sh_w_up[l], sh_w_down[l], ln2_g[l][None], ln2_b[l][None])
    y_prompt = out_main.reshape(bp, tp, D_MODEL)
    y_sample = out_tail[:bs * ts].reshape(bs, ts, D_MODEL)
    return (y_prompt, y_sample, kvc_p, kvs_p, kvw_p, ssm_p[None], conv_p[None],
            kvc_s, kvs_s, kvw_s, ssm_s[None], jnp.swapaxes(conv_s_t, 0, 1)[None])
```

```python
import functools
import math

import jax
import jax.numpy as jnp
import numpy as np
from jax import lax
from jax.experimental import pallas as pl
from jax.experimental.pallas import tpu as pltpu

D_MODEL = 1024
HEAD_DIM = 64
SSD_HEADS = 8
SSD_D = SSD_HEADS * HEAD_DIM
SSD_GROUPS = 2
SSD_STATE = 128
SSD_CONV = 4
SSD_CONV_CH = SSD_D + 2 * SSD_GROUPS * SSD_STATE
SSD_CHUNK = 128
NSA_HEADS = 8
NSA_KV_HEADS = 2
NSA_D = NSA_HEADS * HEAD_DIM
KV_D = NSA_KV_HEADS * HEAD_DIM
CMP_BLOCK = 64
CMP_HIDDEN = 128
SEL_BLOCK = 64
TOP_N = 16
WINDOW = 512
Q_BLOCK = 128
ROT_DIM = HEAD_DIM // 4
ROPE_THETA = 500000.0
N_EXPERTS = 64
TOP_K = 6
N_EXPERT_GROUPS = 8
EXPERTS_PER_GROUP = N_EXPERTS // N_EXPERT_GROUPS
TOPK_GROUPS = 4
D_EXPERT = 256
D_SHARED = 256
ROUTED_SCALE = 2.5
MOE_BLOCK = 1024
DEPTH = 1
DEEPNORM_ALPHA = (2.0 * DEPTH) ** 0.25
LN_EPS = 1e-5
RMS_EPS = 1e-5
NEG = -1e30
FORCED_SCORE = 1e4
PAGE_SIZE = 128

LANES = 128
SUBLANES = 8
VMEM_LIMIT_BYTES = 56 * 1024 * 1024

U_Z = 0
U_XBC = U_Z + SSD_D
U_Q = U_XBC + SSD_CONV_CH
U_KVC = U_Q + NSA_D
U_KVS = U_KVC + 2 * KV_D
U_KVW = U_KVS + 2 * KV_D
U_DTG = U_KVW + 2 * KV_D
U_TOTAL = U_DTG + LANES
GATE_COL0 = SSD_HEADS

BF16 = jnp.bfloat16
F32 = jnp.float32


def _cparams(sem):
    return pltpu.CompilerParams(dimension_semantics=sem, vmem_limit_bytes=VMEM_LIMIT_BYTES)


def _bdot(a, b):
    return jnp.dot(a.astype(BF16), b.astype(BF16), preferred_element_type=F32)


def _bdot_nt(a, b):
    return lax.dot_general(a.astype(BF16), b.astype(BF16), (((1,), (1,)), ((), ())),
                           preferred_element_type=F32)


def _hdot(a, b):
    return jnp.dot(a, b, preferred_element_type=F32, precision=lax.Precision.HIGHEST)


def _sigmoid(x):
    return 1.0 / (1.0 + jnp.exp(-x))


def _silu(x):
    return x * _sigmoid(x)


def _layer_norm(x, g, b):
    mu = jnp.mean(x, axis=-1, keepdims=True)
    xc = x - mu
    var = jnp.mean(xc * xc, axis=-1, keepdims=True)
    return xc * lax.rsqrt(var + LN_EPS) * g + b


def _rope_tile(x, cos, sa, sb):
    return x * cos + pltpu.roll(x, LANES - ROT_DIM // 2, 1) * sa + pltpu.roll(x, ROT_DIM // 2, 1) * sb


def _rope_rows(x, cos, sin):
    half = ROT_DIM // 2
    parts = []
    for hd in range(NSA_KV_HEADS):
        r0 = hd * HEAD_DIM
        x1, x2 = x[r0:r0 + half], x[r0 + half:r0 + ROT_DIM]
        parts += [x1 * cos - x2 * sin, x2 * cos + x1 * sin, x[r0 + ROT_DIM:r0 + HEAD_DIM]]
    return jnp.concatenate(parts, axis=0)


def _inproj_kernel(x_ref, g_ref, b_ref, w_ref, rope_ref, *refs, feature_major):
    if feature_major:
        wkv_t_ref, rope_t_ref = refs[:2]
        refs = refs[2:]
    h_ref, z_ref, xbc_ref, qc_ref, qr_ref, kvc_ref, kvs_ref, kvw_ref, dtg_ref = refs[:9]
    h = _layer_norm(x_ref[...], g_ref[...], b_ref[...])
    h_ref[...] = h
    hb = h.astype(BF16)
    if feature_major:
        kvc_t_ref, kvs_t_ref, kvw_t_ref = refs[9:]
        ut = lax.dot_general(wkv_t_ref[...], hb, (((1,), (1,)), ((), ())), preferred_element_type=F32)
        half = ROT_DIM // 2
        cos_t, sin_t = rope_t_ref[0:half, :], rope_t_ref[half:2 * half, :]
        kvc_t_ref[0] = ut[0:2 * KV_D]
        kvs_t_ref[0, 0:KV_D] = _rope_rows(ut[2 * KV_D:3 * KV_D], cos_t, sin_t)
        kvs_t_ref[0, KV_D:2 * KV_D] = ut[3 * KV_D:4 * KV_D]
        kvw_t_ref[0, 0:KV_D] = _rope_rows(ut[4 * KV_D:5 * KV_D], cos_t, sin_t)
        kvw_t_ref[0, KV_D:2 * KV_D] = ut[5 * KV_D:6 * KV_D]
    u = jnp.dot(hb, w_ref[...], preferred_element_type=F32)
    cos = rope_ref[:, 0:LANES]
    sa = rope_ref[:, LANES:2 * LANES]
    sb = rope_ref[:, 2 * LANES:3 * LANES]
    z_ref[...] = u[:, U_Z:U_XBC]
    xbc_ref[...] = u[:, U_XBC:U_Q]
    qc_ref[...] = u[:, U_Q:U_KVC].astype(qc_ref.dtype)
    for c in range(NSA_D // LANES):
        qr_ref[:, c * LANES:(c + 1) * LANES] = _rope_tile(
            u[:, U_Q + c * LANES:U_Q + (c + 1) * LANES], cos, sa, sb).astype(qr_ref.dtype)
    kvc_ref[...] = u[:, U_KVC:U_KVS]
    kvs_ref[:, 0:KV_D] = _rope_tile(u[:, U_KVS:U_KVS + KV_D], cos, sa, sb)
    kvs_ref[:, KV_D:2 * KV_D] = u[:, U_KVS + KV_D:U_KVW]
    kvw_ref[:, 0:KV_D] = _rope_tile(u[:, U_KVW:U_KVW + KV_D], cos, sa, sb)
    kvw_ref[:, KV_D:2 * KV_D] = u[:, U_KVW + KV_D:U_DTG]
    dtg_ref[...] = u[:, U_DTG:U_TOTAL]


def _rope_tables(pos):
    half = ROT_DIM // 2
    inv = ROPE_THETA ** (-jnp.arange(half, dtype=F32) / half)
    ang = pos.astype(F32)[:, None] * inv
    cos, sin = jnp.cos(ang), jnp.sin(ang)
    ones = jnp.ones((pos.shape[0], HEAD_DIM - ROT_DIM), F32)
    zeros = jnp.zeros((pos.shape[0], HEAD_DIM - ROT_DIM), F32)
    zh = jnp.zeros_like(sin)
    c = jnp.concatenate([cos, cos, ones], 1)
    sa = jnp.concatenate([-sin, zh, zeros], 1)
    sb = jnp.concatenate([zh, sin, zeros], 1)
    return jnp.concatenate([jnp.tile(t, (1, LANES // HEAD_DIM)) for t in (c, sa, sb)], 1)


def _permute_w_in(w):
    sizes = (SSD_D, SSD_CONV_CH, SSD_HEADS, NSA_D, KV_D, KV_D, KV_D, KV_D, KV_D, KV_D, 3 * NSA_HEADS)
    offs = np.concatenate([[0], np.cumsum(sizes)])
    seg = [w[:, offs[i]:offs[i + 1]] for i in range(len(sizes))]
    pad = jnp.zeros((w.shape[0], LANES - SSD_HEADS - 3 * NSA_HEADS), w.dtype)
    out = jnp.concatenate([seg[0], seg[1], seg[3], seg[4], seg[5], seg[6], seg[7], seg[8], seg[9],
                           seg[2], seg[10], pad], 1)
    return out.astype(BF16)


def _rope_tables_t(pos):
    half = ROT_DIM // 2
    inv = ROPE_THETA ** (-jnp.arange(half, dtype=F32) / half)
    ang = inv[:, None] * pos.astype(F32)[None, :]
    return jnp.concatenate([jnp.cos(ang), jnp.sin(ang)], 0)


def _inproj(x, ln_g, ln_b, w_perm, rope_tab, tm, rope_tab_t=None):
    n = x.shape[0]
    nt = n // tm
    t = rope_tab.shape[0]
    n_rope_blocks = t // tm
    feature_major = rope_tab_t is not None
    row = lambda w: pl.BlockSpec((tm, w), lambda i: (i, 0))
    const = lambda a: pl.BlockSpec(a.shape, lambda i: (0,) * a.ndim)
    widths = (D_MODEL, SSD_D, SSD_CONV_CH, NSA_D, NSA_D, 2 * KV_D, 2 * KV_D, 2 * KV_D, LANES)
    in_specs = [row(D_MODEL), const(ln_g), const(ln_b), const(w_perm),
                pl.BlockSpec((tm, 3 * LANES), lambda i: (i % n_rope_blocks, 0))]
    out_specs = [row(w) for w in widths]
    dtypes = [BF16 if (feature_major and k in (3, 4)) else F32 for k in range(len(widths))]
    out_shape = [jax.ShapeDtypeStruct((n, w), d) for w, d in zip(widths, dtypes)]
    args = [x, ln_g, ln_b, w_perm, rope_tab]
    if feature_major:
        wkv_t = w_perm[:, U_KVC:U_DTG].T
        in_specs += [const(wkv_t), pl.BlockSpec((rope_tab_t.shape[0], tm), lambda i: (0, i % n_rope_blocks))]
        args += [wkv_t, rope_tab_t]
        out_specs += [pl.BlockSpec((1, 2 * KV_D, tm), lambda i: (i // n_rope_blocks, 0, i % n_rope_blocks))] * 3
        out_shape += [jax.ShapeDtypeStruct((n // t, 2 * KV_D, t), F32)] * 3
    return pl.pallas_call(
        functools.partial(_inproj_kernel, feature_major=feature_major),
        grid=(nt,),
        in_specs=in_specs,
        out_specs=out_specs,
        out_shape=out_shape,
        compiler_params=_cparams(("parallel",)),
        name="inproj",
    )(*args)


def _softplus(x):
    return jnp.maximum(x, 0.0) + jnp.log1p(jnp.exp(-jnp.abs(x)))


def _gated_group_norm(y, z, norm_w):
    y = y * _silu(z)
    gw = SSD_D // SSD_GROUPS
    parts = []
    for g in range(SSD_GROUPS):
        yg = y[:, g * gw:(g + 1) * gw]
        ms = jnp.mean(yg * yg, axis=-1, keepdims=True)
        parts.append(yg * lax.rsqrt(ms + RMS_EPS))
    return jnp.concatenate(parts, axis=1) * norm_w


SSD_CHUNKS_PER_STEP = 2


def _ssd_prompt_kernel(xbc_ref, z_ref, dtg_ref, convw_ref, convb_ref, dtb_ref, alog_ref, dskip_ref, normw_ref,
                       y_ref, state_ref, conv_ref, ext_ref, s_ref):
    c = pl.program_id(1)
    nc = pl.num_programs(1)
    L = SSD_CHUNK

    @pl.when(c == 0)
    def _():
        ext_ref[0:SUBLANES, :] = jnp.zeros((SUBLANES, SSD_CONV_CH), F32)
        s_ref[...] = jnp.zeros_like(s_ref)

    for sub in range(SSD_CHUNKS_PER_STEP):
        rows = pl.ds(sub * L, L)
        xin = _ssd_chunk(xbc_ref.at[rows], z_ref.at[rows], dtg_ref.at[rows], convw_ref, convb_ref, dtb_ref, alog_ref,
                         dskip_ref, normw_ref, y_ref.at[rows], ext_ref, s_ref)

    @pl.when(c == nc - 1)
    def _():
        state_ref[0] = s_ref[...]
        conv_ref[0] = xin[L - (SSD_CONV - 1):L, :]


def _ssd_chunk(xbc_ref, z_ref, dtg_ref, convw_ref, convb_ref, dtb_ref, alog_ref, dskip_ref, normw_ref, y_ref, ext_ref,
               s_ref):
    L = SSD_CHUNK
    halo = SUBLANES
    xin = xbc_ref[...]
    ext_ref[halo:halo + L, :] = xin
    xc = convw_ref[SSD_CONV - 1:SSD_CONV, :] * xin
    for k in range(SSD_CONV - 1):
        off = halo - (SSD_CONV - 1) + k
        xc = xc + convw_ref[k:k + 1, :] * ext_ref[off:off + L, :]
    ext_ref[0:halo, :] = ext_ref[L:L + halo, :]
    xc = _silu(xc + convb_ref[...])
    xs = xc[:, 0:SSD_D]
    ns = SSD_GROUPS * SSD_STATE
    bm = xc[:, SSD_D:SSD_D + ns]
    cm = xc[:, SSD_D + ns:SSD_D + 2 * ns]

    dt = _softplus(dtg_ref[...] + dtb_ref[...])
    da = dt * (-jnp.exp(alog_ref[...]))
    row = lax.broadcasted_iota(jnp.int32, (L, L), 0)
    col = lax.broadcasted_iota(jnp.int32, (L, L), 1)
    tril = row >= col
    acum = _hdot(tril.astype(F32), da)
    acum_t = acum.T
    eacum = jnp.exp(acum)
    alast = acum[L - 1:L, :]
    edecay = jnp.exp(alast - acum)
    elast = jnp.exp(alast)

    dt_full = jnp.concatenate([jnp.broadcast_to(dt[:, h:h + 1], (L, HEAD_DIM)) for h in range(SSD_HEADS)], 1)
    dec_full = jnp.concatenate([jnp.broadcast_to(edecay[:, h:h + 1], (L, HEAD_DIM)) for h in range(SSD_HEADS)], 1)
    xdt = xs * dt_full
    xdec_t = (xdt * dec_full).T

    hpg = SSD_HEADS // SSD_GROUPS
    y_parts = []
    for h in range(SSD_HEADS):
        g = h // hpg
        b_g = bm[:, g * SSD_STATE:(g + 1) * SSD_STATE]
        c_g = cm[:, g * SSD_STATE:(g + 1) * SSD_STATE]
        if h % hpg == 0:
            cb = _bdot_nt(c_g, b_g)
        seg = acum[:, h:h + 1] - acum_t[h:h + 1, :]
        lmat = jnp.where(tril, jnp.exp(jnp.where(tril, seg, 0.0)), 0.0)
        xdt_h = xdt[:, h * HEAD_DIM:(h + 1) * HEAD_DIM]
        y_h = _bdot(cb * lmat, xdt_h)
        s_prev = s_ref[h]
        y_h = y_h + _bdot_nt(c_g, s_prev) * eacum[:, h:h + 1]
        y_h = y_h + dskip_ref[:, h * HEAD_DIM:(h + 1) * HEAD_DIM] * xs[:, h * HEAD_DIM:(h + 1) * HEAD_DIM]
        y_parts.append(y_h)
        s_ref[h] = elast[:, h:h + 1] * s_prev + _bdot(xdec_t[h * HEAD_DIM:(h + 1) * HEAD_DIM, :], b_g)
    y = jnp.concatenate(y_parts, axis=1)
    y_ref[...] = _gated_group_norm(y, z_ref[...], normw_ref[...]).astype(y_ref.dtype)
    return xin


def _ssd_prompt(xbc, z, dtg, conv_w, conv_b, dt_bias_pad, a_log_pad, d_skip_full, norm_w, bn, t):
    step_rows = SSD_CHUNK * SSD_CHUNKS_PER_STEP
    nc = t // step_rows
    row = lambda w: pl.BlockSpec((step_rows, w), lambda b, c: (b * nc + c, 0))
    const = lambda a: pl.BlockSpec(a.shape, lambda b, c: (0,) * a.ndim)
    return pl.pallas_call(
        _ssd_prompt_kernel,
        grid=(bn, nc),
        in_specs=[row(SSD_CONV_CH), row(SSD_D), row(LANES), const(conv_w), const(conv_b), const(dt_bias_pad),
                  const(a_log_pad), const(d_skip_full), const(norm_w)],
        out_specs=[row(SSD_D),
                   pl.BlockSpec((1, SSD_HEADS, HEAD_DIM, SSD_STATE), lambda b, c: (b, 0, 0, 0)),
                   pl.BlockSpec((1, SSD_CONV - 1, SSD_CONV_CH), lambda b, c: (b, 0, 0))],
        out_shape=[jax.ShapeDtypeStruct((bn * t, SSD_D), BF16),
                   jax.ShapeDtypeStruct((bn, SSD_HEADS, HEAD_DIM, SSD_STATE), F32),
                   jax.ShapeDtypeStruct((bn, SSD_CONV - 1, SSD_CONV_CH), F32)],
        scratch_shapes=[pltpu.VMEM((SSD_CHUNK + 2 * SUBLANES, SSD_CONV_CH), F32),
                        pltpu.VMEM((SSD_HEADS, HEAD_DIM, SSD_STATE), F32)],
        compiler_params=_cparams(("parallel", "arbitrary")),
        name="ssd_prompt",
    )(xbc, z, dtg, conv_w, conv_b, dt_bias_pad, a_log_pad, d_skip_full, norm_w)


def _pad_lanes(v, fill=0.0):
    return jnp.concatenate([v.astype(F32), jnp.full((LANES - v.shape[0],), fill, F32)])[None]


def _compress_rows(k_ref, v_ref, pe_ref, w1k_ref, w1v_ref, b1_ref, w2k_ref, w2v_ref, b2_ref, nb):
    acck = jnp.zeros((nb, 2 * CMP_HIDDEN), F32)
    accv = jnp.zeros((nb, 2 * CMP_HIDDEN), F32)
    for l in range(CMP_BLOCK):
        xk = k_ref[pl.ds(l, nb, stride=CMP_BLOCK), :] + pe_ref[l:l + 1, 0:KV_D]
        xv = v_ref[pl.ds(l, nb, stride=CMP_BLOCK), :] + pe_ref[l:l + 1, KV_D:2 * KV_D]
        acck = acck + jnp.dot(xk.astype(BF16), w1k_ref[l], preferred_element_type=F32)
        accv = accv + jnp.dot(xv.astype(BF16), w1v_ref[l], preferred_element_type=F32)
    hk = _silu(acck + b1_ref[:, 0:2 * CMP_HIDDEN])
    hv = _silu(accv + b1_ref[:, 2 * CMP_HIDDEN:4 * CMP_HIDDEN])
    ok = jnp.dot(hk.astype(BF16), w2k_ref[...], preferred_element_type=F32) + b2_ref[:, 0:KV_D]
    ov = jnp.dot(hv.astype(BF16), w2v_ref[...], preferred_element_type=F32) + b2_ref[:, KV_D:2 * KV_D]
    return jnp.concatenate([ok, ov], axis=1)


def _compress_kernel(k_ref, v_ref, pe_ref, w1k_ref, w1v_ref, b1_ref, w2k_ref, w2v_ref, b2_ref, o_ref, *, nb):
    o_ref[...] = _compress_rows(k_ref, v_ref, pe_ref, w1k_ref, w1v_ref, b1_ref, w2k_ref, w2v_ref, b2_ref, nb)


def _block_diag2(w):
    z = jnp.zeros_like(w)
    return jnp.concatenate([jnp.concatenate([w, z], -1), jnp.concatenate([z, w], -1)], -2)


def _compress_consts(cmp_pe, cmp_w1, cmp_b1, cmp_w2, cmp_b2):
    pe = jnp.concatenate([cmp_pe[0], cmp_pe[0], cmp_pe[1], cmp_pe[1]], -1)
    w1k = _block_diag2(cmp_w1[0]).astype(BF16)
    w1v = _block_diag2(cmp_w1[1]).astype(BF16)
    b1 = jnp.concatenate([cmp_b1[0], cmp_b1[0], cmp_b1[1], cmp_b1[1]])[None]
    w2k = _block_diag2(cmp_w2[0]).astype(BF16)
    w2v = _block_diag2(cmp_w2[1]).astype(BF16)
    b2 = jnp.concatenate([cmp_b2[0], cmp_b2[0], cmp_b2[1], cmp_b2[1]])[None]
    return pe, w1k, w1v, b1, w2k, w2v, b2


def _compress_prompt(kvc, consts, rows_per_step):
    n = kvc.shape[0]
    nb = rows_per_step // CMP_BLOCK
    const = lambda a: pl.BlockSpec(a.shape, lambda i: (0,) * a.ndim)
    return pl.pallas_call(
        functools.partial(_compress_kernel, nb=nb),
        grid=(n // rows_per_step,),
        in_specs=[pl.BlockSpec((rows_per_step, KV_D), lambda i: (i, 0)),
                  pl.BlockSpec((rows_per_step, KV_D), lambda i: (i, 1))] + [const(a) for a in consts],
        out_specs=pl.BlockSpec((nb, 2 * KV_D), lambda i: (i, 0)),
        out_shape=jax.ShapeDtypeStruct((n // CMP_BLOCK, 2 * KV_D), F32),
        compiler_params=_cparams(("parallel",)),
        name="compress_prompt",
    )(kvc, kvc, *consts)


SEL_KEY_TILE = 512
WIN_KEYS = WINDOW + Q_BLOCK


def _dup_head(x, hk):
    sw = pltpu.roll(x, HEAD_DIM, 1)
    low = lax.broadcasted_iota(jnp.int32, x.shape, 1) < HEAD_DIM
    return jnp.where(low, x, sw) if hk == 0 else jnp.where(low, sw, x)


def _masked_softmax(s, mask):
    sm = jnp.where(mask, s, NEG)
    ex = jnp.where(mask, jnp.exp(sm - jnp.max(sm, axis=-1, keepdims=True)), 0.0)
    den = jnp.sum(ex, axis=-1, keepdims=True)
    return ex / jnp.where(den > 0.0, den, 1.0)


def _select_blocks_t(imp, cur, n_top):
    j = lax.broadcasted_iota(jnp.int32, imp.shape, 0)
    future = j > cur
    forced = (j == 0) | (j == cur) | (j == cur - 1)
    score = jnp.where(future, NEG, jnp.where(forced, FORCED_SCORE, imp))
    return ((_rank_rows(score) < n_top) & (score > 0.5 * NEG)).astype(F32)


def _nsa_prompt_kernel(qc_ref, qr_ref, dtg_ref, cmp_ref, kvs_ref, kvw_ref, o_ref,
                       cmp_d, kvs_d, kvw_d, expand_ref, bias_ref, qrs_ref, m_ref, l_ref, acc_ref, *, t):
    qb = pl.program_id(1)
    nbk = t // SEL_BLOCK
    tq = Q_BLOCK
    tk = SEL_KEY_TILE
    hpg = NSA_HEADS // NSA_KV_HEADS
    scale = HEAD_DIM ** -0.5

    @pl.when(qb == 0)
    def _():
        expand_ref[...] = (lax.broadcasted_iota(jnp.int32, (LANES, t), 1) // SEL_BLOCK
                           == lax.broadcasted_iota(jnp.int32, (LANES, t), 0)).astype(BF16)
        cmp_d[...] = jnp.zeros_like(cmp_d)
        for src, dst, n in ((cmp_ref, cmp_d, nbk), (kvs_ref, kvs_d, t), (kvw_ref, kvw_d, t)):
            x = src[...]
            for hk in range(NSA_KV_HEADS):
                dst[hk, 0:n, 0:KV_D] = _dup_head(x[:, 0:KV_D], hk).astype(BF16)
                dst[hk, 0:n, KV_D:2 * KV_D] = _dup_head(x[:, KV_D:2 * KV_D], hk).astype(BF16)

    t0 = qb * tq
    rows = t0 + lax.broadcasted_iota(jnp.int32, (tq, 1), 0)
    lane = lax.broadcasted_iota(jnp.int32, (tq, LANES), 1)
    half_mask = (lane < HEAD_DIM, lane >= HEAD_DIM)
    sig = _sigmoid(dtg_ref[...])
    vis = (lane + 1) * CMP_BLOCK - 1 <= rows
    cur_l = (t0 + lax.broadcasted_iota(jnp.int32, (1, tq), 1)) // SEL_BLOCK
    win_start = pl.multiple_of(jnp.maximum(t0 - WINDOW, 0), tq)
    wpos = win_start + lax.broadcasted_iota(jnp.int32, (tq, WIN_KEYS), 1)
    win_bias = jnp.where((wpos <= rows) & (wpos >= rows - WINDOW), 0.0, NEG)
    n_kt = (t0 + tq + tk - 1) // tk

    def stack_heads(ref, hk):
        parts = []
        for hh in range(hpg):
            head = hk * hpg + hh
            p, e = head // 2, head % 2
            parts.append(jnp.where(half_mask[e], ref[:, p * LANES:(p + 1) * LANES] * scale, 0.0))
        return jnp.concatenate(parts, axis=0).astype(BF16)

    o_cmp_g = []
    for hk in range(NSA_KV_HEADS):
        qcs = stack_heads(qc_ref, hk)
        s = lax.dot_general(qcs, cmp_d[hk, :, 0:KV_D], (((1,), (1,)), ((), ())), preferred_element_type=F32)
        pc = _masked_softmax(s.reshape(hpg, tq, LANES), vis[None])
        imp = jnp.sum(pc, axis=0)
        o_cmp_g.append(jnp.dot(pc.reshape(hpg * tq, LANES).astype(BF16), cmp_d[hk, :, KV_D:2 * KV_D],
                               preferred_element_type=F32).reshape(hpg, tq, LANES))

        sel_t = _select_blocks_t(imp.T[0:nbk, :], cur_l, TOP_N)
        sel = jnp.concatenate([sel_t, jnp.zeros((LANES - nbk, tq), F32)], axis=0).T
        selk = jnp.dot(sel.astype(BF16), expand_ref[...], preferred_element_type=F32)
        for kt in range(t // tk):
            @pl.when(kt < n_kt)
            def _(kt=kt, hk=hk, selk=selk):
                kpos = kt * tk + lax.broadcasted_iota(jnp.int32, (tq, tk), 1)
                bias_ref[hk, kt] = jnp.where((selk[:, kt * tk:(kt + 1) * tk] > 0.5) & (kpos <= rows), 0.0, NEG)

        qrs_ref[hk] = stack_heads(qr_ref, hk)

    m_ref[...] = jnp.full(m_ref.shape, NEG, F32)
    l_ref[...] = jnp.zeros(l_ref.shape, F32)
    acc_ref[...] = jnp.zeros(acc_ref.shape, F32)

    def sel_step(kt, carry):
        k0 = pl.multiple_of(kt * tk, tk)
        for hk in range(NSA_KV_HEADS):
            kblk = kvs_d[hk, pl.ds(k0, tk), 0:KV_D]
            vblk = kvs_d[hk, pl.ds(k0, tk), KV_D:2 * KV_D]
            s = lax.dot_general(qrs_ref[hk], kblk, (((1,), (1,)), ((), ())), preferred_element_type=F32)
            s = s.reshape(hpg, tq, tk) + bias_ref[hk, kt][None]
            m_old = m_ref[hk]
            m_new = jnp.maximum(m_old, jnp.max(s, axis=-1, keepdims=True))
            alpha = jnp.exp(m_old - m_new)
            pe = jnp.exp(s - jnp.concatenate([m_new] * (tk // LANES), axis=-1))
            l_ref[hk] = alpha * l_ref[hk] + jnp.sum(pe, axis=-1, keepdims=True)
            pv = jnp.dot(pe.reshape(hpg * tq, tk).astype(BF16), vblk, preferred_element_type=F32)
            acc_ref[hk] = alpha * acc_ref[hk] + pv.reshape(hpg, tq, LANES)
            m_ref[hk] = m_new
        return carry

    lax.fori_loop(0, n_kt, sel_step, 0)

    for hk in range(NSA_KV_HEADS):
        o_cmp = o_cmp_g[hk]
        o_slc = acc_ref[hk] / l_ref[hk]
        kw = kvw_d[hk, pl.ds(win_start, WIN_KEYS), 0:KV_D]
        vw = kvw_d[hk, pl.ds(win_start, WIN_KEYS), KV_D:2 * KV_D]
        sw = lax.dot_general(qrs_ref[hk], kw, (((1,), (1,)), ((), ())), preferred_element_type=F32)
        sw = sw.reshape(hpg, tq, WIN_KEYS) + win_bias[None]
        pw = jnp.exp(sw - jnp.max(sw, axis=-1, keepdims=True))
        den = jnp.sum(pw, axis=-1, keepdims=True)
        o_win = jnp.dot(pw.reshape(hpg * tq, WIN_KEYS).astype(BF16), vw,
                        preferred_element_type=F32).reshape(hpg, tq, LANES) / den

        for hh in range(hpg):
            head = hk * hpg + hh
            p, e = head // 2, head % 2
            c0 = GATE_COL0 + head * 3
            mix = (sig[:, c0:c0 + 1] * o_cmp[hh] + sig[:, c0 + 1:c0 + 2] * o_slc[hh]
                   + sig[:, c0 + 2:c0 + 3] * o_win[hh])
            if e == 0:
                mix_even = mix
            else:
                o_ref[:, p * LANES:(p + 1) * LANES] = jnp.where(half_mask[0], mix_even, mix).astype(o_ref.dtype)


def _nsa_prompt(qc, qr, dtg, kvcmp, kvs, kvw, bn, t):
    nq = t // Q_BLOCK
    nbk = t // SEL_BLOCK
    hpg = NSA_HEADS // NSA_KV_HEADS
    assert nbk >= TOP_N and t >= WIN_KEYS and t % SEL_KEY_TILE == 0
    qrow = lambda w: pl.BlockSpec((Q_BLOCK, w), lambda b, i: (b * nq + i, 0))
    seq = lambda r: pl.BlockSpec((r, 2 * KV_D), lambda b, i: (b, 0))
    return pl.pallas_call(
        functools.partial(_nsa_prompt_kernel, t=t),
        grid=(bn, nq),
        in_specs=[qrow(NSA_D), qrow(NSA_D), qrow(LANES), seq(nbk), seq(t), seq(t)],
        out_specs=qrow(NSA_D),
        out_shape=jax.ShapeDtypeStruct((bn * t, NSA_D), BF16),
        scratch_shapes=[pltpu.VMEM((NSA_KV_HEADS, LANES, 2 * KV_D), BF16),
                        pltpu.VMEM((NSA_KV_HEADS, t, 2 * KV_D), BF16),
                        pltpu.VMEM((NSA_KV_HEADS, t, 2 * KV_D), BF16),
                        pltpu.VMEM((LANES, t), BF16),
                        pltpu.VMEM((NSA_KV_HEADS, t // SEL_KEY_TILE, Q_BLOCK, SEL_KEY_TILE), F32),
                        pltpu.VMEM((NSA_KV_HEADS, hpg * Q_BLOCK, LANES), BF16),
                        pltpu.VMEM((NSA_KV_HEADS, hpg, Q_BLOCK, LANES), F32),
                        pltpu.VMEM((NSA_KV_HEADS, hpg, Q_BLOCK, LANES), F32),
                        pltpu.VMEM((NSA_KV_HEADS, hpg, Q_BLOCK, LANES), F32)],
        compiler_params=_cparams(("parallel", "arbitrary")),
        name="nsa_prompt",
    )(qc, qr, dtg, kvcmp, kvs, kvw)


def _outproj_kernel(ys_ref, yn_ref, h_ref, ws_ref, wn_ref, g_ref, b_ref, o_ref):
    mix = jnp.dot(ys_ref[...].astype(BF16), ws_ref[...], preferred_element_type=F32)
    mix = mix + jnp.dot(yn_ref[...].astype(BF16), wn_ref[...], preferred_element_type=F32)
    o_ref[...] = _layer_norm(DEEPNORM_ALPHA * h_ref[...] + mix, g_ref[...], b_ref[...])


def _outproj(y_ssd, y_nsa, h, w_ssd, w_nsa, ln_g, ln_b, tm):
    n = h.shape[0]
    row = lambda w: pl.BlockSpec((tm, w), lambda i: (i, 0))
    const = lambda a: pl.BlockSpec(a.shape, lambda i: (0,) * a.ndim)
    return pl.pallas_call(
        _outproj_kernel,
        grid=(n // tm,),
        in_specs=[row(SSD_D), row(NSA_D), row(D_MODEL), const(w_ssd), const(w_nsa), const(ln_g), const(ln_b)],
        out_specs=row(D_MODEL),
        out_shape=jax.ShapeDtypeStruct((n, D_MODEL), F32),
        compiler_params=_cparams(("parallel",)),
        name="outproj",
    )(y_ssd, y_nsa, h, w_ssd, w_nsa, ln_g, ln_b)


MOE_TOKENS = 256
ROUTE_ROWS = 8


def _token_tile_specs(n_main_tiles):
    main = pl.BlockSpec((MOE_TOKENS, D_MODEL), lambda i, *_: (jnp.minimum(i, n_main_tiles - 1), 0))
    tail = pl.BlockSpec((MOE_TOKENS, D_MODEL), lambda i, *_: (0, 0))
    return main, tail


def _token_tile(i, n_main_tiles, main_ref, tail_ref):
    return jnp.where(i < n_main_tiles, main_ref[...], tail_ref[...])


def _rank_rows(x):
    n = x.shape[0]
    idx = lax.broadcasted_iota(jnp.int32, x.shape, 0)
    rank = jnp.zeros(x.shape, F32)
    for r in range(n):
        row = x[r:r + 1, :]
        rank = rank + ((row > x) | ((row == x) & (idx > r))).astype(F32)
    return rank


def _route_kernel(h_ref, ht_ref, rw_ref, rb_ref, slot_ref, tokinfo_ref, meta_ref, cnt_ref, carry_ref, carry_row_ref, *,
                  n_valid, n_main):
    i = pl.program_id(0)
    tm = MOE_TOKENS

    @pl.when(i == 0)
    def _():
        carry_ref[...] = jnp.zeros_like(carry_ref)
        carry_row_ref[...] = jnp.zeros_like(carry_row_ref)

    logits = lax.dot_general(rw_ref[...], _token_tile(i, n_main, h_ref, ht_ref).astype(BF16), (((1,), (1,)), ((), ())),
                             preferred_element_type=F32)
    scores = _sigmoid(logits)
    biased = scores + rb_ref[:, 0:1]
    b3 = biased.reshape(N_EXPERT_GROUPS, EXPERTS_PER_GROUP, tm)
    sidx = lax.broadcasted_iota(jnp.int32, b3.shape, 1)
    m1 = jnp.max(b3, axis=1, keepdims=True)
    first = jnp.min(jnp.where(b3 == m1, sidx, EXPERTS_PER_GROUP), axis=1, keepdims=True)
    m2 = jnp.max(jnp.where(sidx == first, -jnp.inf, b3), axis=1, keepdims=True)
    grp_score = (m1 + m2).reshape(N_EXPERT_GROUPS, tm)
    grp_keep = _rank_rows(grp_score) < TOPK_GROUPS
    masked = jnp.where(grp_keep.reshape(N_EXPERT_GROUPS, 1, tm), b3, NEG).reshape(N_EXPERTS, tm)
    rank = _rank_rows(masked)
    tok = i * tm + lax.broadcasted_iota(jnp.int32, (1, tm), 1)
    valid = tok < n_valid
    sel = (rank < TOP_K) & valid
    self32 = sel.astype(F32)
    wsel = self32 * scores
    wsum = jnp.sum(wsel, axis=0, keepdims=True)
    w = wsel / jnp.where(wsum > 0.0, wsum, 1.0) * ROUTED_SCALE

    selb = sel.astype(BF16)
    tri = lambda n, strict_upper: (
        (lax.broadcasted_iota(jnp.int32, (n, n), 0) < lax.broadcasted_iota(jnp.int32, (n, n), 1))
        if strict_upper else
        (lax.broadcasted_iota(jnp.int32, (n, n), 0) > lax.broadcasted_iota(jnp.int32, (n, n), 1))).astype(BF16)
    pad8 = lambda c: jnp.floor((c + (SUBLANES - 1.0)) * (1.0 / SUBLANES)) * SUBLANES
    pos_tile = jnp.dot(selb, tri(tm, True), preferred_element_type=F32)
    cnt_col = pad8(jnp.sum(self32, axis=1, keepdims=True))
    first_col = jnp.dot(tri(N_EXPERTS, False), jnp.broadcast_to(cnt_col, (N_EXPERTS, LANES)).astype(BF16),
                        preferred_element_type=F32)[:, 0:1]
    slot = first_col + pos_tile

    sel_pad = jnp.concatenate([selb, jnp.zeros((LANES - N_EXPERTS, tm), BF16)], axis=0)
    cnt_row = pad8(lax.dot_general(jnp.ones((SUBLANES, tm), BF16), sel_pad, (((1,), (1,)), ((), ())),
                                   preferred_element_type=F32))
    first_row = jnp.dot(cnt_row.astype(BF16), tri(LANES, True), preferred_element_type=F32)
    prev_row = carry_row_ref[...]
    meta = jnp.concatenate([cnt_row[0:1], first_row[0:1], prev_row[0:1], jnp.zeros((SUBLANES - 3, LANES), F32)], 0)
    meta_ref[0] = meta.astype(jnp.int32)
    carry_row_ref[...] = prev_row + cnt_row
    carry_ref[...] = carry_ref[...] + cnt_col

    slot_rows, w_rows = [], []
    for k in range(TOP_K):
        hit = (rank == k) & sel
        slot_rows.append(jnp.sum(jnp.where(hit, slot, 0.0), axis=0, keepdims=True))
        w_rows.append(jnp.sum(jnp.where(hit, w, 0.0), axis=0, keepdims=True))
    slot_rows = [jnp.where(valid, r, -1.0) for r in slot_rows]
    pad2 = jnp.zeros((ROUTE_ROWS - TOP_K, tm), F32)
    slot_ref[...] = jnp.concatenate(slot_rows + [pad2 - 1.0], 0).astype(jnp.int32)
    info = jnp.concatenate(w_rows + [pad2] + slot_rows + [jnp.zeros((LANES - ROUTE_ROWS - TOP_K, tm), F32)], 0)
    tokinfo_ref[...] = info.T

    @pl.when(i == pl.num_programs(0) - 1)
    def _():
        cnt_ref[...] = jnp.broadcast_to(carry_ref[:, 0:1], cnt_ref.shape)


def _route(h_main, h_tail, router_wt, router_bias_col, n_valid):
    tm = MOE_TOKENS
    n_main = h_main.shape[0] // tm
    n = h_main.shape[0] + tm
    const = lambda a: pl.BlockSpec(a.shape, lambda i: (0,) * a.ndim)
    return pl.pallas_call(
        functools.partial(_route_kernel, n_valid=n_valid, n_main=n_main),
        grid=(n // tm,),
        in_specs=[*_token_tile_specs(n_main), const(router_wt), const(router_bias_col)],
        out_specs=[pl.BlockSpec((ROUTE_ROWS, tm), lambda i: (0, i)),
                   pl.BlockSpec((tm, LANES), lambda i: (i, 0)),
                   pl.BlockSpec((1, SUBLANES, LANES), lambda i: (i, 0, 0)),
                   pl.BlockSpec((N_EXPERTS, LANES), lambda i: (0, 0))],
        out_shape=[jax.ShapeDtypeStruct((ROUTE_ROWS, n), jnp.int32),
                   jax.ShapeDtypeStruct((n, LANES), F32),
                   jax.ShapeDtypeStruct((n // tm, SUBLANES, LANES), jnp.int32),
                   jax.ShapeDtypeStruct((N_EXPERTS, LANES), F32)],
        scratch_shapes=[pltpu.VMEM((N_EXPERTS, LANES), F32), pltpu.VMEM((SUBLANES, LANES), F32)],
        compiler_params=_cparams(("arbitrary",)),
        name="moe_route",
    )(h_main, h_tail, router_wt, router_bias_col)


PACKED_D = D_MODEL // 2
U32 = jnp.uint32


def _pack_bf16_pairs(x):
    hi = lax.bitcast_convert_type(x[:, 0:PACKED_D], U32) & jnp.uint32(0xFFFF0000)
    lo = lax.shift_right_logical(lax.bitcast_convert_type(x[:, PACKED_D:D_MODEL], U32), jnp.uint32(16))
    return hi | lo


def _unpack_bf16_pairs(w):
    hi = lax.bitcast_convert_type(w & jnp.uint32(0xFFFF0000), F32)
    lo = lax.bitcast_convert_type(lax.shift_left(w, jnp.uint32(16)), F32)
    return jnp.concatenate([hi, lo], axis=1).astype(BF16)


def _round_bf16(x):
    return x.astype(BF16).astype(F32)


TILE_SLOTS = MOE_TOKENS * TOP_K + N_EXPERTS * SUBLANES
RUN_CHUNKS = tuple(1 << b for b in range(int(math.log2(MOE_TOKENS)), int(math.log2(SUBLANES)) - 1, -1))


def _run_copy(src_ref, src_row, dst_ref, dst_row, rows, sem):
    return pltpu.make_async_copy(src_ref.at[pl.ds(pl.multiple_of(src_row, SUBLANES), rows)],
                                 dst_ref.at[pl.ds(pl.multiple_of(dst_row, SUBLANES), rows)], sem)


def _start_run(src_ref, src_row, dst_ref, dst_row, n, sem, started):
    off = jnp.int32(0)
    out = []
    for c, rows in enumerate(RUN_CHUNKS):
        take = (n & rows) != 0

        @pl.when(take)
        def _(off=off, rows=rows):
            _run_copy(src_ref, src_row + off, dst_ref, dst_row + off, rows, sem).start()

        inc = take.astype(jnp.int32)
        off = off + inc * rows
        out.append(started[c] + inc)
    return tuple(out)


def _wait_runs(src_ref, dst_ref, sem, started):
    for c, rows in enumerate(RUN_CHUNKS):
        def wait_one(j, carry, rows=rows):
            _run_copy(src_ref, 0, dst_ref, 0, rows, sem).wait()
            return carry

        lax.fori_loop(0, started[c], wait_one, 0)


def _dispatch_kernel(start_ref, cnt_ref, meta_ref, slot_ref, x_ref, xt_ref, xs_ref, sorted_ref, zero_ref, filled_ref, sem,
                     zsem, *, cap, n_main):
    i = pl.program_id(0)
    tm = MOE_TOKENS

    @pl.when(i == 0)
    def _():
        zero_ref[...] = jnp.zeros_like(zero_ref)

        def fill_expert(e, started):
            lo = start_ref[e] + cnt_ref[e]
            hi = jnp.where(e == N_EXPERTS - 1, cap, start_ref[jnp.minimum(e + 1, N_EXPERTS - 1)])
            n_full = (hi - lo) // tm

            def fill_full(j, st):
                return _start_run(zero_ref, 0, xs_ref, lo + j * tm, jnp.int32(tm), zsem, st)

            started = lax.fori_loop(0, n_full, fill_full, started)
            return _start_run(zero_ref, 0, xs_ref, lo + n_full * tm, (hi - lo) - n_full * tm, zsem, started)

        filled = lax.fori_loop(0, N_EXPERTS, fill_expert, tuple(jnp.int32(0) for _ in RUN_CHUNKS))
        for c in range(len(RUN_CHUNKS)):
            filled_ref[c] = filled[c]

    srow = lax.broadcasted_iota(jnp.int32, (TILE_SLOTS, tm), 0)
    onehot = srow == slot_ref[0:1, :]
    for k in range(1, TOP_K):
        onehot = onehot | (srow == slot_ref[k:k + 1, :])
    sorted_ref[...] = _pack_bf16_pairs(jnp.dot(onehot.astype(BF16), _token_tile(i, n_main, x_ref, xt_ref).astype(BF16),
                                               preferred_element_type=F32))

    def copy_expert(e, started):
        n = meta_ref[0, 0, e]
        return _start_run(sorted_ref, meta_ref[0, 1, e], xs_ref, start_ref[e] + meta_ref[0, 2, e], n, sem, started)

    started = lax.fori_loop(0, N_EXPERTS, copy_expert, tuple(jnp.int32(0) for _ in RUN_CHUNKS))
    _wait_runs(sorted_ref, xs_ref, sem, started)

    @pl.when(i == pl.num_programs(0) - 1)
    def _():
        _wait_runs(zero_ref, xs_ref, zsem, tuple(filled_ref[c] for c in range(len(RUN_CHUNKS))))


def _dispatch(h_main, h_tail, slot_t, meta, seg_start, counts, cap):
    tm = MOE_TOKENS
    n_main = h_main.shape[0] // tm
    return pl.pallas_call(
        functools.partial(_dispatch_kernel, cap=cap, n_main=n_main),
        grid_spec=pltpu.PrefetchScalarGridSpec(
            num_scalar_prefetch=2,
            grid=(n_main + 1,),
            in_specs=[pl.BlockSpec((1, SUBLANES, LANES), lambda i, *_: (i, 0, 0), memory_space=pltpu.SMEM),
                      pl.BlockSpec((ROUTE_ROWS, tm), lambda i, *_: (0, i)),
                      *_token_tile_specs(n_main)],
            out_specs=pl.BlockSpec(memory_space=pl.ANY),
            scratch_shapes=[pltpu.VMEM((TILE_SLOTS, PACKED_D), U32), pltpu.VMEM((tm, PACKED_D), U32),
                            pltpu.SMEM((len(RUN_CHUNKS),), jnp.int32),
                            pltpu.SemaphoreType.DMA, pltpu.SemaphoreType.DMA]),
        out_shape=jax.ShapeDtypeStruct((cap, PACKED_D), U32),
        compiler_params=_cparams(("arbitrary",)),
        name="moe_dispatch",
    )(seg_start, counts, meta, slot_t, h_main, h_tail)


def _swiglu(x, wg, wu, wd):
    xb = x.astype(BF16)
    g = jnp.dot(xb, wg.astype(BF16), preferred_element_type=F32)
    u = jnp.dot(xb, wu.astype(BF16), preferred_element_type=F32)
    return jnp.dot((_silu(g) * u).astype(BF16), wd.astype(BF16), preferred_element_type=F32)


EXPERT_RING = 3


def _experts_kernel(be_ref, used_ref, xs_ref, wg_ref, wu_ref, wd_ref, y_ref, xbuf, sems):
    i = pl.program_id(0)
    n = pl.num_programs(0)

    def block_copy(blk):
        slot = lax.rem(blk, EXPERT_RING)
        rows = pl.ds(pl.multiple_of(blk * MOE_BLOCK, MOE_BLOCK), MOE_BLOCK)
        return pltpu.make_async_copy(xs_ref.at[rows], xbuf.at[slot], sems.at[slot])

    @pl.when(i == 0)
    def _():
        for j in range(EXPERT_RING - 1):
            block_copy(jnp.int32(j)).start()

    @pl.when(i + EXPERT_RING - 1 < n)
    def _():
        block_copy(i + EXPERT_RING - 1).start()

    block_copy(i).wait()

    @pl.when(i < used_ref[0])
    def _():
        y = _swiglu(_unpack_bf16_pairs(xbuf[lax.rem(i, EXPERT_RING)]), wg_ref[0], wu_ref[0], wd_ref[0])
        y_ref[...] = _pack_bf16_pairs(_round_bf16(y))

    @pl.when(i >= used_ref[0])
    def _():
        y_ref[...] = jnp.zeros_like(y_ref)


def _experts(xs, block_expert, used_blocks, w_gate, w_up, w_down):
    cap = xs.shape[0]
    return pl.pallas_call(
        _experts_kernel,
        grid_spec=pltpu.PrefetchScalarGridSpec(
            num_scalar_prefetch=2,
            grid=(cap // MOE_BLOCK,),
            in_specs=[pl.BlockSpec(memory_space=pl.ANY),
                      pl.BlockSpec((1, D_MODEL, D_EXPERT), lambda i, be, used: (be[i], 0, 0)),
                      pl.BlockSpec((1, D_MODEL, D_EXPERT), lambda i, be, used: (be[i], 0, 0)),
                      pl.BlockSpec((1, D_EXPERT, D_MODEL), lambda i, be, used: (be[i], 0, 0))],
            out_specs=pl.BlockSpec((MOE_BLOCK, PACKED_D), lambda i, be, used: (i, 0)),
            scratch_shapes=[pltpu.VMEM((EXPERT_RING, MOE_BLOCK, PACKED_D), U32),
                            pltpu.SemaphoreType.DMA((EXPERT_RING,))]),
        out_shape=jax.ShapeDtypeStruct((cap, PACKED_D), U32),
        compiler_params=_cparams(("arbitrary",)),
        name="moe_experts",
    )(block_expert, used_blocks, xs, w_gate, w_up, w_down)


def _combine_kernel(start_ref, meta_ref, h_ref, ht_ref, info_ref, sg_ref, su_ref, sd_ref, g_ref, b_ref,
                    ys_ref, o_ref, ot_ref, buf_ref, sem, *, n_main):
    i = pl.program_id(0)
    tm = MOE_TOKENS

    @pl.when(i == 0)
    def _():
        buf_ref[...] = jnp.zeros_like(buf_ref)

    def fetch_expert(e, started):
        n = meta_ref[0, 0, e]
        return _start_run(ys_ref, start_ref[e] + meta_ref[0, 2, e], buf_ref, meta_ref[0, 1, e], n, sem, started)

    started = lax.fori_loop(0, N_EXPERTS, fetch_expert, tuple(jnp.int32(0) for _ in RUN_CHUNKS))
    h = _token_tile(i, n_main, h_ref, ht_ref)
    f = _swiglu(h, sg_ref[...], su_ref[...], sd_ref[...])
    info = info_ref[...]
    scol = lax.broadcasted_iota(jnp.int32, (tm, TILE_SLOTS), 1).astype(F32)
    mix = jnp.zeros((tm, TILE_SLOTS), F32)
    for k in range(TOP_K):
        mix = jnp.where(info[:, ROUTE_ROWS + k:ROUTE_ROWS + k + 1] == scol, info[:, k:k + 1], mix)
    _wait_runs(ys_ref, buf_ref, sem, started)
    acc = jnp.dot(mix.astype(BF16), _unpack_bf16_pairs(buf_ref[...]), preferred_element_type=F32)
    out = _layer_norm(DEEPNORM_ALPHA * h + (acc + f), g_ref[...], b_ref[...])

    @pl.when(i < n_main)
    def _():
        o_ref[...] = out

    @pl.when(i >= n_main)
    def _():
        ot_ref[...] = out


def _combine(h_main, h_tail, ys, meta, tokinfo, seg_start, sh_gate, sh_up, sh_down, ln_g, ln_b):
    tm = MOE_TOKENS
    n_main = h_main.shape[0] // tm
    const = lambda a: pl.BlockSpec(a.shape, lambda i, *_: (0,) * a.ndim)
    return pl.pallas_call(
        functools.partial(_combine_kernel, n_main=n_main),
        grid_spec=pltpu.PrefetchScalarGridSpec(
            num_scalar_prefetch=1,
            grid=(n_main + 1,),
            in_specs=[pl.BlockSpec((1, SUBLANES, LANES), lambda i, *_: (i, 0, 0), memory_space=pltpu.SMEM),
                      *_token_tile_specs(n_main),
                      pl.BlockSpec((tm, LANES), lambda i, *_: (i, 0)),
                      const(sh_gate), const(sh_up), const(sh_down), const(ln_g), const(ln_b),
                      pl.BlockSpec(memory_space=pl.ANY)],
            out_specs=list(_token_tile_specs(n_main)),
            scratch_shapes=[pltpu.VMEM((TILE_SLOTS, PACKED_D), U32), pltpu.SemaphoreType.DMA]),
        out_shape=[jax.ShapeDtypeStruct(h_main.shape, F32), jax.ShapeDtypeStruct((tm, D_MODEL), F32)],
        compiler_params=_cparams(("arbitrary",)),
        name="moe_combine",
    )(seg_start, meta, h_main, h_tail, tokinfo, sh_gate, sh_up, sh_down, ln_g, ln_b, ys)


def _moe_ln(h_main, h_tail, n_valid, router_w, router_bias, w_gate, w_up, w_down, sh_gate, sh_up, sh_down, ln_g, ln_b):
    n_tiles = h_main.shape[0] // MOE_TOKENS + 1
    slot_t, tokinfo, meta, cnt = _route(h_main, h_tail, router_w.T.astype(BF16),
                                        jnp.broadcast_to(router_bias.astype(F32)[:, None], (N_EXPERTS, LANES)), n_valid)
    counts = cnt[:, 0].astype(jnp.int32)
    padded = (counts + MOE_BLOCK - 1) // MOE_BLOCK * MOE_BLOCK
    seg_end = jnp.cumsum(padded)
    seg_start = seg_end - padded
    run_pad = n_tiles * N_EXPERTS * (SUBLANES - 1)
    n_blocks = -(-(n_valid * TOP_K + run_pad + N_EXPERTS * (MOE_BLOCK - 1)) // MOE_BLOCK)
    cap = n_blocks * MOE_BLOCK
    block_first_row = jnp.arange(n_blocks, dtype=jnp.int32) * MOE_BLOCK
    block_expert = jnp.minimum(jnp.sum((seg_end[None, :] <= block_first_row[:, None]).astype(jnp.int32), axis=1),
                               N_EXPERTS - 1)
    xs = _dispatch(h_main, h_tail, slot_t, meta, seg_start, counts, cap)
    used_blocks = (seg_end[N_EXPERTS - 1:] // MOE_BLOCK).astype(jnp.int32)
    ys = _experts(xs, block_expert, used_blocks, w_gate, w_up, w_down)
    return _combine(h_main, h_tail, ys, meta, tokinfo, seg_start, sh_gate.astype(BF16), sh_up.astype(BF16),
                    sh_down.astype(BF16), ln_g, ln_b)


def _ssd_sample_kernel(xbc_ref, z_ref, dtg_ref, sconv_ref, s0_ref, convw_ref, convb_ref, dtb_ref, alog_ref,
                       dskip_ref, normw_ref, y_ref, s_ref, conv_out_ref, xc_ref, dt_ref, da_ref):
    b = pl.program_id(0)

    @pl.when(b == 0)
    def _():
        xin = xbc_ref[...]
        xc = convw_ref[SSD_CONV - 1:SSD_CONV, :] * xin
        for k in range(SSD_CONV - 1):
            xc = xc + convw_ref[k:k + 1, :] * sconv_ref[k]
        xc_ref[...] = _silu(xc + convb_ref[...])
        dt = _softplus(dtg_ref[...] + dtb_ref[...])
        dt_ref[...] = dt
        da_ref[...] = jnp.exp(dt * (-jnp.exp(alog_ref[...])))
        for k in range(SSD_CONV - 2):
            conv_out_ref[k] = sconv_ref[k + 1]
        conv_out_ref[SSD_CONV - 2] = xin

    xc = xc_ref[pl.ds(b, 1), :]
    dt = dt_ref[pl.ds(b, 1), :]
    da = da_ref[pl.ds(b, 1), :]
    ns = SSD_GROUPS * SSD_STATE
    eye = (lax.broadcasted_iota(jnp.int32, (HEAD_DIM, HEAD_DIM), 0)
           == lax.broadcasted_iota(jnp.int32, (HEAD_DIM, HEAD_DIM), 1))
    hpg = SSD_HEADS // SSD_GROUPS
    y_parts = []
    for h in range(SSD_HEADS):
        g = h // hpg
        x_h = xc[:, h * HEAD_DIM:(h + 1) * HEAD_DIM]
        b_g = xc[:, SSD_D + g * SSD_STATE:SSD_D + (g + 1) * SSD_STATE]
        c_g = xc[:, SSD_D + ns + g * SSD_STATE:SSD_D + ns + (g + 1) * SSD_STATE]
        xdt_col = jnp.sum(jnp.where(eye, x_h * dt[:, h:h + 1], 0.0), axis=1, keepdims=True)
        s_new = da[:, h:h + 1] * s0_ref[0, h] + xdt_col * b_g
        s_ref[0, h] = s_new
        y_h = _bdot_nt(c_g, s_new) + dskip_ref[:, h * HEAD_DIM:(h + 1) * HEAD_DIM] * x_h
        y_parts.append(y_h)
    y = jnp.concatenate(y_parts, axis=1)
    y_ref[pl.ds(b, 1), :] = _gated_group_norm(y, z_ref[pl.ds(b, 1), :], normw_ref[...])


def _ssd_sample(xbc, z, dtg, state_conv_t, state_ssm, conv_w, conv_b, dt_bias_pad, a_log_pad, d_skip_full, norm_w):
    bs = xbc.shape[0]
    const = lambda a: pl.BlockSpec(a.shape, lambda b: (0,) * a.ndim)
    state_spec = pl.BlockSpec((1, SSD_HEADS, HEAD_DIM, SSD_STATE), lambda b: (b, 0, 0, 0))
    return pl.pallas_call(
        _ssd_sample_kernel,
        grid=(bs,),
        in_specs=[const(xbc), const(z), const(dtg), const(state_conv_t), state_spec, const(conv_w), const(conv_b),
                  const(dt_bias_pad), const(a_log_pad), const(d_skip_full), const(norm_w)],
        out_specs=[pl.BlockSpec((bs, SSD_D), lambda b: (0, 0)), state_spec,
                   pl.BlockSpec((SSD_CONV - 1, bs, SSD_CONV_CH), lambda b: (0, 0, 0))],
        out_shape=[jax.ShapeDtypeStruct((bs, SSD_D), F32),
                   jax.ShapeDtypeStruct(state_ssm.shape, F32),
                   jax.ShapeDtypeStruct((SSD_CONV - 1, bs, SSD_CONV_CH), F32)],
        scratch_shapes=[pltpu.VMEM((bs, SSD_CONV_CH), F32), pltpu.VMEM((bs, LANES), F32),
                        pltpu.VMEM((bs, LANES), F32)],
        compiler_params=_cparams(("arbitrary",)),
        name="ssd_sample",
    )(xbc, z, dtg, state_conv_t, state_ssm, conv_w, conv_b, dt_bias_pad, a_log_pad, d_skip_full, norm_w)


SEL_PAST = TOP_N - 1
BLOCKS_PER_PAGE = PAGE_SIZE // CMP_BLOCK
KV_FEATS = 2 * KV_D


def _compress_consts_t(cmp_pe, cmp_w1, cmp_b1, cmp_w2, cmp_b2):
    pe_t = jnp.stack([jnp.tile(cmp_pe[k].T, (1, BLOCKS_PER_PAGE)) for k in range(2)])
    w1_t = jnp.stack([_block_diag2(jnp.swapaxes(cmp_w1[k], 0, 1)) for k in range(2)]).astype(BF16)
    b1_t = jnp.stack([jnp.tile(cmp_b1[k], BLOCKS_PER_PAGE) for k in range(2)])[:, None, :]
    w2_t = jnp.stack([_block_diag2(cmp_w2[k]) for k in range(2)]).astype(BF16)
    b2_t = jnp.stack([jnp.tile(cmp_b2[k], BLOCKS_PER_PAGE) for k in range(2)])[:, None, :]
    return pe_t, w1_t, b1_t, w2_t, b2_t


def _compress_pages_kernel(pt_ref, pe_ref, w1_ref, b1_ref, w2_ref, b2_ref, pool_ref, o_ref, kbuf, vbuf, sems, *,
                           n_pages):
    b = pl.program_id(0)
    nb = pl.num_programs(0)
    bufs = (kbuf, vbuf)

    def half_copy(seq, kind, p):
        return pltpu.make_async_copy(pool_ref.at[pt_ref[seq, p], pl.ds(kind * KV_D, KV_D)],
                                     bufs[kind].at[:, p], sems.at[kind])

    def start_half(seq, kind):
        def start_pair(q, c):
            half_copy(seq, kind, 2 * q).start(priority=0)
            half_copy(seq, kind, 2 * q + 1).start(priority=1)
            return c

        lax.fori_loop(0, n_pages // 2, start_pair, 0)

    def wait_half(seq, kind):
        lax.fori_loop(0, n_pages, lambda p, c: (half_copy(seq, kind, p).wait(), c)[1], 0)

    @pl.when(b == 0)
    def _():
        start_half(b, 0)
        start_half(b, 1)

    for kind in range(2):
        wait_half(b, kind)
        def add_feature(d, acc, kind=kind):
            x = jnp.concatenate([bufs[kind][h * HEAD_DIM + d] for h in range(NSA_KV_HEADS)], axis=0) \
                + pe_ref[kind, pl.ds(d, 1), :]
            return acc + jnp.dot(x.astype(BF16), w1_ref[kind, d], preferred_element_type=F32)

        acc = lax.fori_loop(0, HEAD_DIM, add_feature,
                            jnp.zeros((NSA_KV_HEADS * n_pages, BLOCKS_PER_PAGE * CMP_HIDDEN), F32), unroll=8)
        hid = _silu(acc + b1_ref[kind])
        out = jnp.dot(hid.astype(BF16), w2_ref[kind], preferred_element_type=F32) + b2_ref[kind]
        for h in range(NSA_KV_HEADS):
            o_ref[0, kind * NSA_KV_HEADS + h] = out[h * n_pages:(h + 1) * n_pages]

        @pl.when(b + 1 < nb)
        def _(kind=kind):
            start_half(b + 1, kind)


def _compress_pages(pool_t, page_table, consts):
    bs, n_pages = page_table.shape
    const = lambda a: pl.BlockSpec(a.shape, lambda b, pt: (0,) * a.ndim)
    return pl.pallas_call(
        functools.partial(_compress_pages_kernel, n_pages=n_pages),
        grid_spec=pltpu.PrefetchScalarGridSpec(
            num_scalar_prefetch=1,
            grid=(bs,),
            in_specs=[const(a) for a in consts] + [pl.BlockSpec(memory_space=pl.ANY)],
            out_specs=pl.BlockSpec((1, 2 * NSA_KV_HEADS, n_pages, LANES), lambda b, pt: (b, 0, 0, 0)),
            scratch_shapes=[pltpu.VMEM((KV_D, n_pages, PAGE_SIZE), F32), pltpu.VMEM((KV_D, n_pages, PAGE_SIZE), F32),
                            pltpu.SemaphoreType.DMA((2,))]),
        out_shape=jax.ShapeDtypeStruct((bs, 2 * NSA_KV_HEADS, n_pages, LANES), F32),
        compiler_params=_cparams(("arbitrary",)),
        name="compress_pages",
    )(page_table, *consts, pool_t)


def _group_heads(q_row, hk):
    hpg = NSA_HEADS // NSA_KV_HEADS
    low = lax.broadcasted_iota(jnp.int32, (1, LANES), 1) < HEAD_DIM
    rows = []
    for r in range(hpg):
        head = hk * hpg + r
        tile = q_row[:, (head // 2) * LANES:(head // 2 + 1) * LANES]
        if head % 2 == 1:
            tile = pltpu.roll(tile, HEAD_DIM, 1)
        rows.append(jnp.where(low, tile, 0.0))
    return jnp.concatenate(rows + [jnp.zeros((SUBLANES - hpg, LANES), F32)], axis=0)


def _spread_heads(o_groups):
    hpg = NSA_HEADS // NSA_KV_HEADS
    return jnp.concatenate([o[r:r + 1, 0:HEAD_DIM] for o in o_groups for r in range(hpg)], axis=1)


def _nsa_sample_cmp_t_kernel(qc_ref, cmp_ref, ocmp_ref, idx_ref, *, n_pages):
    b = pl.program_id(0)
    nc = n_pages * BLOCKS_PER_PAGE
    scale = HEAD_DIM ** -0.5
    hpg = NSA_HEADS // NSA_KV_HEADS
    q_row = qc_ref[pl.ds(b, 1), :] * scale
    lane = lax.broadcasted_iota(jnp.int32, (1, LANES), 1)
    pos_r = lax.broadcasted_iota(jnp.int32, (1, nc), 1)
    bid_r = (pos_r % n_pages) * BLOCKS_PER_PAGE + pos_r // n_pages
    pos_c = lax.broadcasted_iota(jnp.int32, (nc, 1), 0)
    bid_c = (pos_c % n_pages) * BLOCKS_PER_PAGE + pos_c // n_pages
    o_groups = []
    for hk in range(NSA_KV_HEADS):
        kc = cmp_ref[0, hk].astype(BF16)
        vc = cmp_ref[0, NSA_KV_HEADS + hk].astype(BF16)
        qh = _group_heads(q_row, hk)
        s = jnp.concatenate(
            [lax.dot_general(pltpu.roll(qh, c * HEAD_DIM, 1).astype(BF16) if c else qh.astype(BF16), kc,
                             (((1,), (1,)), ((), ())), preferred_element_type=F32)
             for c in range(BLOCKS_PER_PAGE)], axis=1)
        ex = jnp.exp(s - jnp.max(s, axis=-1, keepdims=True))
        p = ex / jnp.sum(ex, axis=-1, keepdims=True)
        o = jnp.dot(p[:, 0:n_pages].astype(BF16), vc, preferred_element_type=F32)
        for c in range(1, BLOCKS_PER_PAGE):
            oc = jnp.dot(p[:, c * n_pages:(c + 1) * n_pages].astype(BF16), vc, preferred_element_type=F32)
            o = o + pltpu.roll(oc, LANES - c * HEAD_DIM, 1)
        o_groups.append(o)
        hrow = lax.broadcasted_iota(jnp.int32, p.shape, 0) < hpg
        imp = jnp.sum(jnp.where(hrow, p, 0.0), axis=0, keepdims=True)
        score = jnp.where((bid_r == 0) | (bid_r == nc - 1), FORCED_SCORE, imp)
        score_col = jnp.concatenate([score, jnp.zeros((LANES - 1, nc), F32)], 0).T[:, 0:1]
        beats = (score_col > score) | ((score_col == score) & (bid_c < bid_r))
        rank = jnp.sum(beats.astype(F32), axis=0, keepdims=True)
        row = jnp.zeros((1, LANES), F32)
        bid_f = bid_r.astype(F32)
        for k in range(SEL_PAST):
            blk = jnp.sum(jnp.where(rank == k, bid_f, 0.0), axis=1, keepdims=True)
            row = jnp.where(lane == k, blk, row)
        idx_ref[pl.ds(b * NSA_KV_HEADS + hk, 1), :] = row.astype(jnp.int32)
    ocmp_ref[pl.ds(b, 1), :] = _spread_heads(o_groups)


def _nsa_sample_cmp_t(qc, kvcmp_t):
    bs, _, n_pages, _ = kvcmp_t.shape
    return pl.pallas_call(
        functools.partial(_nsa_sample_cmp_t_kernel, n_pages=n_pages),
        grid=(bs,),
        in_specs=[pl.BlockSpec((bs, NSA_D), lambda b: (0, 0)),
                  pl.BlockSpec((1, 2 * NSA_KV_HEADS, n_pages, LANES), lambda b: (b, 0, 0, 0))],
        out_specs=[pl.BlockSpec((bs, NSA_D), lambda b: (0, 0)),
                   pl.BlockSpec((bs * NSA_KV_HEADS, LANES), lambda b: (0, 0))],
        out_shape=[jax.ShapeDtypeStruct((bs, NSA_D), F32),
                   jax.ShapeDtypeStruct((bs * NSA_KV_HEADS, LANES), jnp.int32)],
        compiler_params=_cparams(("arbitrary",)),
        name="nsa_sample_cmp",
    )(qc, kvcmp_t)


def _sel_block_copies(pool_ref, pt_ref, sel_ref, kbuf, vbuf, sem, b, hk, k):
    blk = sel_ref[b * NSA_KV_HEADS + hk, k]
    page = pt_ref[b, lax.shift_right_logical(blk, int(math.log2(BLOCKS_PER_PAGE)))]
    j = hk * SEL_PAST + k
    return (pltpu.make_async_copy(pool_ref.at[page, pl.ds(hk * HEAD_DIM, HEAD_DIM)], kbuf.at[j], sem),
            pltpu.make_async_copy(pool_ref.at[page, pl.ds(KV_D + hk * HEAD_DIM, HEAD_DIM)], vbuf.at[j], sem))


def _nsa_sample_attn_t_kernel(pt_ref, sel_ref, qr_ref, new_sel_ref, new_win_ref, win_ref, dtg_ref, ocmp_ref,
                              pool_ref, o_ref, kbuf, vbuf, sem):
    b = pl.program_id(0)
    for hk in range(NSA_KV_HEADS):
        for k in range(SEL_PAST):
            for cp in _sel_block_copies(pool_ref, pt_ref, sel_ref, kbuf, vbuf, sem, b, hk, k):
                cp.start()
    for hk in range(NSA_KV_HEADS):
        for k in range(SEL_PAST):
            for cp in _sel_block_copies(pool_ref, pt_ref, sel_ref, kbuf, vbuf, sem, b, hk, k):
                cp.wait()
    scale = HEAD_DIM ** -0.5
    q_row = qr_ref[pl.ds(b, 1), :] * scale
    sig = _sigmoid(dtg_ref[pl.ds(b, 1), :])
    lane = lax.broadcasted_iota(jnp.int32, (1, PAGE_SIZE), 1)
    o_slc, o_win = [], []
    for hk in range(NSA_KV_HEADS):
        qh = _group_heads(q_row, hk)[:, 0:HEAD_DIM].astype(BF16)

        def new_row(ref, kind):
            t = ref[pl.ds(b, 1), :][:, kind * KV_D:(kind + 1) * KV_D]
            if hk == 1:
                t = pltpu.roll(t, HEAD_DIM, 1)
            return t[:, 0:HEAD_DIM].astype(BF16).astype(F32)

        def attend(kt, vt, mask, new_ref, n_new):
            s = jnp.dot(qh, kt.astype(BF16), preferred_element_type=F32)
            if mask is not None:
                s = jnp.where(mask, s, NEG)
            s_new = jnp.sum(qh.astype(F32) * new_row(new_ref, 0), axis=1, keepdims=True)
            m = jnp.maximum(jnp.max(s, axis=-1, keepdims=True), s_new)
            ex = jnp.exp(s - m)
            ex_new = jnp.exp(s_new - m) * n_new
            den = jnp.sum(ex, axis=-1, keepdims=True) + ex_new
            o = lax.dot_general((ex / den).astype(BF16), vt.astype(BF16), (((1,), (1,)), ((), ())),
                                preferred_element_type=F32)
            return o + (ex_new / den).astype(BF16).astype(F32) * new_row(new_ref, 1)

        kt = jnp.concatenate([kbuf[hk * SEL_PAST + k] for k in range(SEL_PAST)], axis=1)
        vt = jnp.concatenate([vbuf[hk * SEL_PAST + k] for k in range(SEL_PAST)], axis=1)
        mask = jnp.concatenate(
            [lane // SEL_BLOCK == (sel_ref[b * NSA_KV_HEADS + hk, k] & (BLOCKS_PER_PAGE - 1))
             for k in range(SEL_PAST)], axis=1)
        o_slc.append(attend(kt, vt, mask, new_sel_ref, float(SEL_BLOCK)))
        o_win.append(attend(win_ref[0, hk * HEAD_DIM:(hk + 1) * HEAD_DIM, :],
                            win_ref[0, KV_D + hk * HEAD_DIM:KV_D + (hk + 1) * HEAD_DIM, :], None, new_win_ref, 1.0))
    gates = []
    for br in range(3):
        gates.append(jnp.concatenate(
            [jnp.broadcast_to(sig[:, GATE_COL0 + h * 3 + br:GATE_COL0 + h * 3 + br + 1], (1, HEAD_DIM))
             for h in range(NSA_HEADS)], axis=1))
    o_ref[pl.ds(b, 1), :] = (gates[0] * ocmp_ref[pl.ds(b, 1), :] + gates[1] * _spread_heads(o_slc)
                             + gates[2] * _spread_heads(o_win))


def _nsa_sample_attn_t(qr, new_sel, new_win, win_t, dtg, o_cmp, pool_sel_t, page_table, sel_idx):
    bs = qr.shape[0]
    const = lambda a: pl.BlockSpec(a.shape, lambda b, pt, sel: (0,) * a.ndim)
    n_buf = NSA_KV_HEADS * SEL_PAST
    return pl.pallas_call(
        _nsa_sample_attn_t_kernel,
        grid_spec=pltpu.PrefetchScalarGridSpec(
            num_scalar_prefetch=2,
            grid=(bs,),
            in_specs=[const(qr), const(new_sel), const(new_win),
                      pl.BlockSpec((1,) + win_t.shape[1:], lambda b, pt, sel: (b, 0, 0)),
                      const(dtg), const(o_cmp), pl.BlockSpec(memory_space=pl.ANY)],
            out_specs=pl.BlockSpec((bs, NSA_D), lambda b, pt, sel: (0, 0)),
            scratch_shapes=[pltpu.VMEM((n_buf, HEAD_DIM, PAGE_SIZE), F32), pltpu.VMEM((n_buf, HEAD_DIM, PAGE_SIZE), F32),
                            pltpu.SemaphoreType.DMA]),
        out_shape=jax.ShapeDtypeStruct((bs, NSA_D), F32),
        compiler_params=_cparams(("arbitrary",)),
        name="nsa_sample_attn",
    )(page_table, sel_idx, qr, new_sel, new_win, win_t, dtg, o_cmp, pool_sel_t)


def kernel(x_prompt, x_sample, cache_kv_cmp, cache_kv_sel, page_table, cache_kv_win, state_ssm, state_conv,
           emb_ln_g, emb_ln_b, w_in, conv_w, conv_b, dt_bias, a_log, d_skip, ssd_norm_w,
           cmp_pe, cmp_w1, cmp_b1, cmp_w2, cmp_b2, w_out, ln1_g, ln1_b,
           router_w, router_bias, exp_w_gate, exp_w_up, exp_w_down,
           sh_w_gate, sh_w_up, sh_w_down, ln2_g, ln2_b):
    bp, tp, _ = x_prompt.shape
    bs, ts, _ = x_sample.shape
    assert ts == 1 and DEPTH == 1
    n_prompt = bp * tp
    past_len = page_table.shape[1] * PAGE_SIZE
    l = 0
    w_perm = _permute_w_in(w_in[l])
    ln0_g, ln0_b = emb_ln_g[None], emb_ln_b[None]
    ssd_consts = (conv_w[l], conv_b[l][None], _pad_lanes(dt_bias[l]), _pad_lanes(a_log[l]),
                  jnp.repeat(d_skip[l], HEAD_DIM)[None], ssd_norm_w[l][None])
    cmp_consts = _compress_consts(cmp_pe[l], cmp_w1[l], cmp_b1[l], cmp_w2[l], cmp_b2[l])
    w_o = w_out[l].astype(BF16)
    w_o_ssd, w_o_nsa = w_o[:SSD_D], w_o[SSD_D:]
    ln1 = (ln1_g[l][None], ln1_b[l][None])
    kv_shape = (2, NSA_KV_HEADS, HEAD_DIM)

    hp, z, xbc, qc, qr, kvc, kvs, kvw, dtg, kvc_t, kvs_t, kvw_t = _inproj(
        x_prompt.reshape(n_prompt, D_MODEL), ln0_g, ln0_b, w_perm, _rope_tables(jnp.arange(tp)), 256,
        _rope_tables_t(jnp.arange(tp)))
    y_ssd, ssm_p, conv_p = _ssd_prompt(xbc, z, dtg, *ssd_consts, bp, tp)
    kvcmp = _compress_prompt(kvc, cmp_consts, tp)
    y_nsa = _nsa_prompt(qc, qr, dtg, kvcmp, kvs, kvw, bp, tp)
    h1p = _outproj(y_ssd, y_nsa, hp, w_o_ssd, w_o_nsa, *ln1, 512)
    n_keep = min(WINDOW, tp)
    cache_leaf = lambda a: jnp.transpose(a.reshape((bp,) + kv_shape + (a.shape[-1],)), (0, 4, 1, 2, 3))[None]
    kvc_p = cache_leaf(kvc_t)
    kvs_p = cache_leaf(kvs_t)
    kvw_p = cache_leaf(kvw_t[:, :, tp - n_keep:])

    s_hs, s_z, s_xbc, s_qc, s_qr, s_kvc, s_kvs, s_kvw, s_dtg = _inproj(
        x_sample.reshape(bs, D_MODEL), ln0_g, ln0_b, w_perm, _rope_tables(jnp.full((bs,), past_len)), bs)
    s_y_ssd, ssm_s, conv_s_t = _ssd_sample(s_xbc, s_z, s_dtg, jnp.swapaxes(state_conv[l], 0, 1), state_ssm[l],
                                           *ssd_consts)
    n_pool = cache_kv_cmp.shape[1]
    feature_major = lambda c, rows: jnp.swapaxes(c.reshape(-1, rows, 2 * KV_D), 1, 2)
    s_kvcmp = _compress_pages(feature_major(cache_kv_cmp[l], PAGE_SIZE), page_table,
                              _compress_consts_t(cmp_pe[l], cmp_w1[l], cmp_b1[l], cmp_w2[l], cmp_b2[l]))
    s_o_cmp, s_sel = _nsa_sample_cmp_t(s_qc, s_kvcmp)
    buf_win = cache_kv_win[l].reshape(bs, -1, 2 * KV_D)
    s_y_nsa = _nsa_sample_attn_t(
        s_qr, s_kvs, s_kvw, feature_major(cache_kv_win[l], buf_win.shape[1]), s_dtg, s_o_cmp,
        feature_major(cache_kv_sel[l], PAGE_SIZE), page_table, s_sel)
    h1s = _outproj(s_y_ssd, s_y_nsa, s_hs, w_o_ssd, w_o_nsa, *ln1, bs)
    win_all = jnp.concatenate([buf_win, s_kvw[:, None, :]], 1)
    n_keep_s = min(WINDOW, past_len + ts)
    kvw_s = win_all[:, win_all.shape[1] - n_keep_s:].reshape((1, bs, n_keep_s) + kv_shape)
    kvc_s = s_kvc.reshape((1, bs, ts) + kv_shape)
    kvs_s = s_kvs.reshape((1, bs, ts) + kv_shape)

    assert n_prompt % MOE_TOKENS == 0 and bs * ts <= MOE_TOKENS
    n_tok = n_prompt + bs * ts
    tail = jnp.concatenate([h1s, jnp.zeros((MOE_TOKENS - bs * ts, D_MODEL), F32)], 0)
    out_main, out_tail = _moe_ln(h1p, tail, n_tok, router_w[l], router_bias[l], exp_w_gate[l], exp_w_up[l],
                                 exp_w_down[l], sh_w_gate[l], sh_w_up[l], sh_w_down[l], ln2_g[l][None], ln2_b[l][None])
    y_prompt = out_main.reshape(bp, tp, D_MODEL)
    y_sample = out_tail[:bs * ts].reshape(bs, ts, D_MODEL)
    return (y_prompt, y_sample, kvc_p, kvs_p, kvw_p, ssm_p[None], conv_p[None],
            kvc_s, kvs_s, kvw_s, ssm_s[None], jnp.swapaxes(conv_s_t, 0, 1)[None])
```

```python
import functools
import math

import jax
import jax.numpy as jnp
import numpy as np
from jax import lax
from jax.experimental import pallas as pl
from jax.experimental.pallas import tpu as pltpu

D_MODEL = 1024
HEAD_DIM = 64
SSD_HEADS = 8
SSD_D = SSD_HEADS * HEAD_DIM
SSD_GROUPS = 2
SSD_STATE = 128
SSD_CONV = 4
SSD_CONV_CH = SSD_D + 2 * SSD_GROUPS * SSD_STATE
SSD_CHUNK = 128
NSA_HEADS = 8
NSA_KV_HEADS = 2
NSA_D = NSA_HEADS * HEAD_DIM
KV_D = NSA_KV_HEADS * HEAD_DIM
CMP_BLOCK = 64
CMP_HIDDEN = 128
SEL_BLOCK = 64
TOP_N = 16
WINDOW = 512
Q_BLOCK = 128
ROT_DIM = HEAD_DIM // 4
ROPE_THETA = 500000.0
N_EXPERTS = 64
TOP_K = 6
N_EXPERT_GROUPS = 8
EXPERTS_PER_GROUP = N_EXPERTS // N_EXPERT_GROUPS
TOPK_GROUPS = 4
D_EXPERT = 256
D_SHARED = 256
ROUTED_SCALE = 2.5
MOE_BLOCK = 1024
DEPTH = 1
DEEPNORM_ALPHA = (2.0 * DEPTH) ** 0.25
LN_EPS = 1e-5
RMS_EPS = 1e-5
NEG = -1e30
FORCED_SCORE = 1e4
PAGE_SIZE = 128

LANES = 128
SUBLANES = 8
VMEM_LIMIT_BYTES = 56 * 1024 * 1024

U_Z = 0
U_XBC = U_Z + SSD_D
U_Q = U_XBC + SSD_CONV_CH
U_KVC = U_Q + NSA_D
U_KVS = U_KVC + 2 * KV_D
U_KVW = U_KVS + 2 * KV_D
U_DTG = U_KVW + 2 * KV_D
U_TOTAL = U_DTG + LANES
GATE_COL0 = SSD_HEADS

BF16 = jnp.bfloat16
F32 = jnp.float32


def _cparams(sem):
    return pltpu.CompilerParams(dimension_semantics=sem, vmem_limit_bytes=VMEM_LIMIT_BYTES)


def _bdot(a, b):
    return jnp.dot(a.astype(BF16), b.astype(BF16), preferred_element_type=F32)


def _bdot_nt(a, b):
    return lax.dot_general(a.astype(BF16), b.astype(BF16), (((1,), (1,)), ((), ())),
                           preferred_element_type=F32)


def _hdot(a, b):
    return jnp.dot(a, b, preferred_element_type=F32, precision=lax.Precision.HIGHEST)


def _sigmoid(x):
    return 1.0 / (1.0 + jnp.exp(-x))


def _silu(x):
    return x * _sigmoid(x)


def _layer_norm(x, g, b):
    mu = jnp.mean(x, axis=-1, keepdims=True)
    xc = x - mu
    var = jnp.mean(xc * xc, axis=-1, keepdims=True)
    return xc * lax.rsqrt(var + LN_EPS) * g + b


def _rope_tile(x, cos, sa, sb):
    return x * cos + pltpu.roll(x, LANES - ROT_DIM // 2, 1) * sa + pltpu.roll(x, ROT_DIM // 2, 1) * sb


def _rope_rows(x, cos, sin):
    half = ROT_DIM // 2
    parts = []
    for hd in range(NSA_KV_HEADS):
        r0 = hd * HEAD_DIM
        x1, x2 = x[r0:r0 + half], x[r0 + half:r0 + ROT_DIM]
        parts += [x1 * cos - x2 * sin, x2 * cos + x1 * sin, x[r0 + ROT_DIM:r0 + HEAD_DIM]]
    return jnp.concatenate(parts, axis=0)


def _inproj_kernel(x_ref, g_ref, b_ref, w_ref, rope_ref, *refs, feature_major):
    if feature_major:
        wkv_t_ref, rope_t_ref = refs[:2]
        refs = refs[2:]
    h_ref, z_ref, xbc_ref, qc_ref, qr_ref, kvc_ref, kvs_ref, kvw_ref, dtg_ref = refs[:9]
    h = _layer_norm(x_ref[...], g_ref[...], b_ref[...])
    h_ref[...] = h
    hb = h.astype(BF16)
    if feature_major:
        kvc_t_ref, kvs_t_ref, kvw_t_ref = refs[9:]
        ut = lax.dot_general(wkv_t_ref[...], hb, (((1,), (1,)), ((), ())), preferred_element_type=F32)
        half = ROT_DIM // 2
        cos_t, sin_t = rope_t_ref[0:half, :], rope_t_ref[half:2 * half, :]
        kvc_t_ref[0] = ut[0:2 * KV_D]
        kvs_t_ref[0, 0:KV_D] = _rope_rows(ut[2 * KV_D:3 * KV_D], cos_t, sin_t)
        kvs_t_ref[0, KV_D:2 * KV_D] = ut[3 * KV_D:4 * KV_D]
        kvw_t_ref[0, 0:KV_D] = _rope_rows(ut[4 * KV_D:5 * KV_D], cos_t, sin_t)
        kvw_t_ref[0, KV_D:2 * KV_D] = ut[5 * KV_D:6 * KV_D]
    u = jnp.dot(hb, w_ref[...], preferred_element_type=F32)
    cos = rope_ref[:, 0:LANES]
    sa = rope_ref[:, LANES:2 * LANES]
    sb = rope_ref[:, 2 * LANES:3 * LANES]
    z_ref[...] = u[:, U_Z:U_XBC]
    xbc_ref[...] = u[:, U_XBC:U_Q]
    qc_ref[...] = u[:, U_Q:U_KVC].astype(qc_ref.dtype)
    for c in range(NSA_D // LANES):
        qr_ref[:, c * LANES:(c + 1) * LANES] = _rope_tile(
            u[:, U_Q + c * LANES:U_Q + (c + 1) * LANES], cos, sa, sb).astype(qr_ref.dtype)
    kvc_ref[...] = u[:, U_KVC:U_KVS]
    kvs_ref[:, 0:KV_D] = _rope_tile(u[:, U_KVS:U_KVS + KV_D], cos, sa, sb)
    kvs_ref[:, KV_D:2 * KV_D] = u[:, U_KVS + KV_D:U_KVW]
    kvw_ref[:, 0:KV_D] = _rope_tile(u[:, U_KVW:U_KVW + KV_D], cos, sa, sb)
    kvw_ref[:, KV_D:2 * KV_D] = u[:, U_KVW + KV_D:U_DTG]
    dtg_ref[...] = u[:, U_DTG:U_TOTAL]


def _rope_tables(pos):
    half = ROT_DIM // 2
    inv = ROPE_THETA ** (-jnp.arange(half, dtype=F32) / half)
    ang = pos.astype(F32)[:, None] * inv
    cos, sin = jnp.cos(ang), jnp.sin(ang)
    ones = jnp.ones((pos.shape[0], HEAD_DIM - ROT_DIM), F32)
    zeros = jnp.zeros((pos.shape[0], HEAD_DIM - ROT_DIM), F32)
    zh = jnp.zeros_like(sin)
    c = jnp.concatenate([cos, cos, ones], 1)
    sa = jnp.concatenate([-sin, zh, zeros], 1)
    sb = jnp.concatenate([zh, sin, zeros], 1)
    return jnp.concatenate([jnp.tile(t, (1, LANES // HEAD_DIM)) for t in (c, sa, sb)], 1)


def _permute_w_in(w):
    sizes = (SSD_D, SSD_CONV_CH, SSD_HEADS, NSA_D, KV_D, KV_D, KV_D, KV_D, KV_D, KV_D, 3 * NSA_HEADS)
    offs = np.concatenate([[0], np.cumsum(sizes)])
    seg = [w[:, offs[i]:offs[i + 1]] for i in range(len(sizes))]
    pad = jnp.zeros((w.shape[0], LANES - SSD_HEADS - 3 * NSA_HEADS), w.dtype)
    out = jnp.concatenate([seg[0], seg[1], seg[3], seg[4], seg[5], seg[6], seg[7], seg[8], seg[9],
                           seg[2], seg[10], pad], 1)
    return out.astype(BF16)


def _rope_tables_t(pos):
    half = ROT_DIM // 2
    inv = ROPE_THETA ** (-jnp.arange(half, dtype=F32) / half)
    ang = inv[:, None] * pos.astype(F32)[None, :]
    return jnp.concatenate([jnp.cos(ang), jnp.sin(ang)], 0)


def _inproj(x, ln_g, ln_b, w_perm, rope_tab, tm, rope_tab_t=None):
    n = x.shape[0]
    nt = n // tm
    t = rope_tab.shape[0]
    n_rope_blocks = t // tm
    feature_major = rope_tab_t is not None
    row = lambda w: pl.BlockSpec((tm, w), lambda i: (i, 0))
    const = lambda a: pl.BlockSpec(a.shape, lambda i: (0,) * a.ndim)
    widths = (D_MODEL, SSD_D, SSD_CONV_CH, NSA_D, NSA_D, 2 * KV_D, 2 * KV_D, 2 * KV_D, LANES)
    in_specs = [row(D_MODEL), const(ln_g), const(ln_b), const(w_perm),
                pl.BlockSpec((tm, 3 * LANES), lambda i: (i % n_rope_blocks, 0))]
    out_specs = [row(w) for w in widths]
    dtypes = [BF16 if (feature_major and k in (3, 4)) else F32 for k in range(len(widths))]
    out_shape = [jax.ShapeDtypeStruct((n, w), d) for w, d in zip(widths, dtypes)]
    args = [x, ln_g, ln_b, w_perm, rope_tab]
    if feature_major:
        wkv_t = w_perm[:, U_KVC:U_DTG].T
        in_specs += [const(wkv_t), pl.BlockSpec((rope_tab_t.shape[0], tm), lambda i: (0, i % n_rope_blocks))]
        args += [wkv_t, rope_tab_t]
        out_specs += [pl.BlockSpec((1, 2 * KV_D, tm), lambda i: (i // n_rope_blocks, 0, i % n_rope_blocks))] * 3
        out_shape += [jax.ShapeDtypeStruct((n // t, 2 * KV_D, t), F32)] * 3
    return pl.pallas_call(
        functools.partial(_inproj_kernel, feature_major=feature_major),
        grid=(nt,),
        in_specs=in_specs,
        out_specs=out_specs,
        out_shape=out_shape,
        compiler_params=_cparams(("parallel",)),
        name="inproj",
    )(*args)


def _softplus(x):
    return jnp.maximum(x, 0.0) + jnp.log1p(jnp.exp(-jnp.abs(x)))


def _gated_group_norm(y, z, norm_w):
    y = y * _silu(z)
    gw = SSD_D // SSD_GROUPS
    parts = []
    for g in range(SSD_GROUPS):
        yg = y[:, g * gw:(g + 1) * gw]
        ms = jnp.mean(yg * yg, axis=-1, keepdims=True)
        parts.append(yg * lax.rsqrt(ms + RMS_EPS))
    return jnp.concatenate(parts, axis=1) * norm_w


SSD_CHUNKS_PER_STEP = 2


def _ssd_prompt_kernel(xbc_ref, z_ref, dtg_ref, convw_ref, convb_ref, dtb_ref, alog_ref, dskip_ref, normw_ref,
                       y_ref, state_ref, conv_ref, ext_ref, s_ref):
    c = pl.program_id(1)
    nc = pl.num_programs(1)
    L = SSD_CHUNK

    @pl.when(c == 0)
    def _():
        ext_ref[0:SUBLANES, :] = jnp.zeros((SUBLANES, SSD_CONV_CH), F32)
        s_ref[...] = jnp.zeros_like(s_ref)

    for sub in range(SSD_CHUNKS_PER_STEP):
        rows = pl.ds(sub * L, L)
        xin = _ssd_chunk(xbc_ref.at[rows], z_ref.at[rows], dtg_ref.at[rows], convw_ref, convb_ref, dtb_ref, alog_ref,
                         dskip_ref, normw_ref, y_ref.at[rows], ext_ref, s_ref)

    @pl.when(c == nc - 1)
    def _():
        state_ref[0] = s_ref[...]
        conv_ref[0] = xin[L - (SSD_CONV - 1):L, :]


def _ssd_chunk(xbc_ref, z_ref, dtg_ref, convw_ref, convb_ref, dtb_ref, alog_ref, dskip_ref, normw_ref, y_ref, ext_ref,
               s_ref):
    L = SSD_CHUNK
    halo = SUBLANES
    xin = xbc_ref[...]
    ext_ref[halo:halo + L, :] = xin
    xc = convw_ref[SSD_CONV - 1:SSD_CONV, :] * xin
    for k in range(SSD_CONV - 1):
        off = halo - (SSD_CONV - 1) + k
        xc = xc + convw_ref[k:k + 1, :] * ext_ref[off:off + L, :]
    ext_ref[0:halo, :] = ext_ref[L:L + halo, :]
    xc = _silu(xc + convb_ref[...])
    xs = xc[:, 0:SSD_D]
    ns = SSD_GROUPS * SSD_STATE
    bm = xc[:, SSD_D:SSD_D + ns]
    cm = xc[:, SSD_D + ns:SSD_D + 2 * ns]

    dt = _softplus(dtg_ref[...] + dtb_ref[...])
    da = dt * (-jnp.exp(alog_ref[...]))
    row = lax.broadcasted_iota(jnp.int32, (L, L), 0)
    col = lax.broadcasted_iota(jnp.int32, (L, L), 1)
    tril = row >= col
    acum = _hdot(tril.astype(F32), da)
    acum_t = acum.T
    eacum = jnp.exp(acum)
    alast = acum[L - 1:L, :]
    edecay = jnp.exp(alast - acum)
    elast = jnp.exp(alast)

    dt_full = jnp.concatenate([jnp.broadcast_to(dt[:, h:h + 1], (L, HEAD_DIM)) for h in range(SSD_HEADS)], 1)
    dec_full = jnp.concatenate([jnp.broadcast_to(edecay[:, h:h + 1], (L, HEAD_DIM)) for h in range(SSD_HEADS)], 1)
    xdt = xs * dt_full
    xdec_t = (xdt * dec_full).T

    hpg = SSD_HEADS // SSD_GROUPS
    y_parts = []
    for h in range(SSD_HEADS):
        g = h // hpg
        b_g = bm[:, g * SSD_STATE:(g + 1) * SSD_STATE]
        c_g = cm[:, g * SSD_STATE:(g + 1) * SSD_STATE]
        if h % hpg == 0:
            cb = _bdot_nt(c_g, b_g)
        seg = acum[:, h:h + 1] - acum_t[h:h + 1, :]
        lmat = jnp.where(tril, jnp.exp(jnp.where(tril, seg, 0.0)), 0.0)
        xdt_h = xdt[:, h * HEAD_DIM:(h + 1) * HEAD_DIM]
        y_h = _bdot(cb * lmat, xdt_h)
        s_prev = s_ref[h]
        y_h = y_h + _bdot_nt(c_g, s_prev) * eacum[:, h:h + 1]
        y_h = y_h + dskip_ref[:, h * HEAD_DIM:(h + 1) * HEAD_DIM] * xs[:, h * HEAD_DIM:(h + 1) * HEAD_DIM]
        y_parts.append(y_h)
        s_ref[h] = elast[:, h:h + 1] * s_prev + _bdot(xdec_t[h * HEAD_DIM:(h + 1) * HEAD_DIM, :], b_g)
    y = jnp.concatenate(y_parts, axis=1)
    y_ref[...] = _gated_group_norm(y, z_ref[...], normw_ref[...]).astype(y_ref.dtype)
    return xin


def _ssd_prompt(xbc, z, dtg, conv_w, conv_b, dt_bias_pad, a_log_pad, d_skip_full, norm_w, bn, t):
    step_rows = SSD_CHUNK * SSD_CHUNKS_PER_STEP
    nc = t // step_rows
    row = lambda w: pl.BlockSpec((step_rows, w), lambda b, c: (b * nc + c, 0))
    const = lambda a: pl.BlockSpec(a.shape, lambda b, c: (0,) * a.ndim)
    return pl.pallas_call(
        _ssd_prompt_kernel,
        grid=(bn, nc),
        in_specs=[row(SSD_CONV_CH), row(SSD_D), row(LANES), const(conv_w), const(conv_b), const(dt_bias_pad),
                  const(a_log_pad), const(d_skip_full), const(norm_w)],
        out_specs=[row(SSD_D),
                   pl.BlockSpec((1, SSD_HEADS, HEAD_DIM, SSD_STATE), lambda b, c: (b, 0, 0, 0)),
                   pl.BlockSpec((1, SSD_CONV - 1, SSD_CONV_CH), lambda b, c: (b, 0, 0))],
        out_shape=[jax.ShapeDtypeStruct((bn * t, SSD_D), BF16),
                   jax.ShapeDtypeStruct((bn, SSD_HEADS, HEAD_DIM, SSD_STATE), F32),
                   jax.ShapeDtypeStruct((bn, SSD_CONV - 1, SSD_CONV_CH), F32)],
        scratch_shapes=[pltpu.VMEM((SSD_CHUNK + 2 * SUBLANES, SSD_CONV_CH), F32),
                        pltpu.VMEM((SSD_HEADS, HEAD_DIM, SSD_STATE), F32)],
        compiler_params=_cparams(("parallel", "arbitrary")),
        name="ssd_prompt",
    )(xbc, z, dtg, conv_w, conv_b, dt_bias_pad, a_log_pad, d_skip_full, norm_w)


def _pad_lanes(v, fill=0.0):
    return jnp.concatenate([v.astype(F32), jnp.full((LANES - v.shape[0],), fill, F32)])[None]


def _compress_rows(k_ref, v_ref, pe_ref, w1k_ref, w1v_ref, b1_ref, w2k_ref, w2v_ref, b2_ref, nb):
    acck = jnp.zeros((nb, 2 * CMP_HIDDEN), F32)
    accv = jnp.zeros((nb, 2 * CMP_HIDDEN), F32)
    for l in range(CMP_BLOCK):
        xk = k_ref[pl.ds(l, nb, stride=CMP_BLOCK), :] + pe_ref[l:l + 1, 0:KV_D]
        xv = v_ref[pl.ds(l, nb, stride=CMP_BLOCK), :] + pe_ref[l:l + 1, KV_D:2 * KV_D]
        acck = acck + jnp.dot(xk.astype(BF16), w1k_ref[l], preferred_element_type=F32)
        accv = accv + jnp.dot(xv.astype(BF16), w1v_ref[l], preferred_element_type=F32)
    hk = _silu(acck + b1_ref[:, 0:2 * CMP_HIDDEN])
    hv = _silu(accv + b1_ref[:, 2 * CMP_HIDDEN:4 * CMP_HIDDEN])
    ok = jnp.dot(hk.astype(BF16), w2k_ref[...], preferred_element_type=F32) + b2_ref[:, 0:KV_D]
    ov = jnp.dot(hv.astype(BF16), w2v_ref[...], preferred_element_type=F32) + b2_ref[:, KV_D:2 * KV_D]
    return jnp.concatenate([ok, ov], axis=1)


def _compress_kernel(k_ref, v_ref, pe_ref, w1k_ref, w1v_ref, b1_ref, w2k_ref, w2v_ref, b2_ref, o_ref, *, nb):
    o_ref[...] = _compress_rows(k_ref, v_ref, pe_ref, w1k_ref, w1v_ref, b1_ref, w2k_ref, w2v_ref, b2_ref, nb)


def _block_diag2(w):
    z = jnp.zeros_like(w)
    return jnp.concatenate([jnp.concatenate([w, z], -1), jnp.concatenate([z, w], -1)], -2)


def _compress_consts(cmp_pe, cmp_w1, cmp_b1, cmp_w2, cmp_b2):
    pe = jnp.concatenate([cmp_pe[0], cmp_pe[0], cmp_pe[1], cmp_pe[1]], -1)
    w1k = _block_diag2(cmp_w1[0]).astype(BF16)
    w1v = _block_diag2(cmp_w1[1]).astype(BF16)
    b1 = jnp.concatenate([cmp_b1[0], cmp_b1[0], cmp_b1[1], cmp_b1[1]])[None]
    w2k = _block_diag2(cmp_w2[0]).astype(BF16)
    w2v = _block_diag2(cmp_w2[1]).astype(BF16)
    b2 = jnp.concatenate([cmp_b2[0], cmp_b2[0], cmp_b2[1], cmp_b2[1]])[None]
    return pe, w1k, w1v, b1, w2k, w2v, b2


def _compress_prompt(kvc, consts, rows_per_step):
    n = kvc.shape[0]
    nb = rows_per_step // CMP_BLOCK
    const = lambda a: pl.BlockSpec(a.shape, lambda i: (0,) * a.ndim)
    return pl.pallas_call(
        functools.partial(_compress_kernel, nb=nb),
        grid=(n // rows_per_step,),
        in_specs=[pl.BlockSpec((rows_per_step, KV_D), lambda i: (i, 0)),
                  pl.BlockSpec((rows_per_step, KV_D), lambda i: (i, 1))] + [const(a) for a in consts],
        out_specs=pl.BlockSpec((nb, 2 * KV_D), lambda i: (i, 0)),
        out_shape=jax.ShapeDtypeStruct((n // CMP_BLOCK, 2 * KV_D), F32),
        compiler_params=_cparams(("parallel",)),
        name="compress_prompt",
    )(kvc, kvc, *consts)


SEL_KEY_TILE = 512
WIN_KEYS = WINDOW + Q_BLOCK


def _dup_head(x, hk):
    sw = pltpu.roll(x, HEAD_DIM, 1)
    low = lax.broadcasted_iota(jnp.int32, x.shape, 1) < HEAD_DIM
    return jnp.where(low, x, sw) if hk == 0 else jnp.where(low, sw, x)


def _masked_softmax(s, mask):
    sm = jnp.where(mask, s, NEG)
    ex = jnp.where(mask, jnp.exp(sm - jnp.max(sm, axis=-1, keepdims=True)), 0.0)
    den = jnp.sum(ex, axis=-1, keepdims=True)
    return ex / jnp.where(den > 0.0, den, 1.0)


def _select_blocks_t(imp, cur, n_top):
    j = lax.broadcasted_iota(jnp.int32, imp.shape, 0)
    future = j > cur
    forced = (j == 0) | (j == cur) | (j == cur - 1)
    score = jnp.where(future, NEG, jnp.where(forced, FORCED_SCORE, imp))
    return ((_rank_rows(score) < n_top) & (score > 0.5 * NEG)).astype(F32)


def _nsa_prompt_kernel(qc_ref, qr_ref, dtg_ref, cmp_ref, kvs_ref, kvw_ref, o_ref,
                       cmp_d, kvs_d, kvw_d, bias_ref, qrs_ref, m_ref, l_ref, acc_ref, *, t):
    qb = pl.program_id(1)
    nbk = t // SEL_BLOCK
    tq = Q_BLOCK
    tk = SEL_KEY_TILE
    hpg = NSA_HEADS // NSA_KV_HEADS
    scale = HEAD_DIM ** -0.5

    @pl.when(qb == 0)
    def _():
        cmp_d[...] = jnp.zeros_like(cmp_d)
        for src, dst, n in ((cmp_ref, cmp_d, nbk), (kvs_ref, kvs_d, t), (kvw_ref, kvw_d, t)):
            x = src[...]
            for hk in range(NSA_KV_HEADS):
                dst[hk, 0:n, 0:KV_D] = _dup_head(x[:, 0:KV_D], hk).astype(BF16)
                dst[hk, 0:n, KV_D:2 * KV_D] = _dup_head(x[:, KV_D:2 * KV_D], hk).astype(BF16)

    t0 = qb * tq
    rows = t0 + lax.broadcasted_iota(jnp.int32, (tq, 1), 0)
    lane = lax.broadcasted_iota(jnp.int32, (tq, LANES), 1)
    half_mask = (lane < HEAD_DIM, lane >= HEAD_DIM)
    sig = _sigmoid(dtg_ref[...])
    vis = (lane + 1) * CMP_BLOCK - 1 <= rows
    cur_l = (t0 + lax.broadcasted_iota(jnp.int32, (1, tq), 1)) // SEL_BLOCK
    expand = (lax.broadcasted_iota(jnp.int32, (LANES, t), 1) // SEL_BLOCK
              == lax.broadcasted_iota(jnp.int32, (LANES, t), 0)).astype(BF16)
    win_start = pl.multiple_of(jnp.maximum(t0 - WINDOW, 0), tq)
    wpos = win_start + lax.broadcasted_iota(jnp.int32, (tq, WIN_KEYS), 1)
    win_bias = jnp.where((wpos <= rows) & (wpos >= rows - WINDOW), 0.0, NEG)
    n_kt = (t0 + tq + tk - 1) // tk

    def stack_heads(ref, hk):
        parts = []
        for hh in range(hpg):
            head = hk * hpg + hh
            p, e = head // 2, head % 2
            parts.append(jnp.where(half_mask[e], ref[:, p * LANES:(p + 1) * LANES] * scale, 0.0))
        return jnp.concatenate(parts, axis=0).astype(BF16)

    o_cmp_g = []
    for hk in range(NSA_KV_HEADS):
        qcs = stack_heads(qc_ref, hk)
        s = lax.dot_general(qcs, cmp_d[hk, :, 0:KV_D], (((1,), (1,)), ((), ())), preferred_element_type=F32)
        pc = _masked_softmax(s.reshape(hpg, tq, LANES), vis[None])
        imp = jnp.sum(pc, axis=0)
        o_cmp_g.append(jnp.dot(pc.reshape(hpg * tq, LANES).astype(BF16), cmp_d[hk, :, KV_D:2 * KV_D],
                               preferred_element_type=F32).reshape(hpg, tq, LANES))

        sel_t = _select_blocks_t(imp.T[0:nbk, :], cur_l, TOP_N)
        sel = jnp.concatenate([sel_t, jnp.zeros((LANES - nbk, tq), F32)], axis=0).T
        selk = jnp.dot(sel.astype(BF16), expand, preferred_element_type=F32)
        for kt in range(t // tk):
            @pl.when(kt < n_kt)
            def _(kt=kt, hk=hk, selk=selk):
                kpos = kt * tk + lax.broadcasted_iota(jnp.int32, (tq, tk), 1)
                bias_ref[hk, kt] = jnp.where((selk[:, kt * tk:(kt + 1) * tk] > 0.5) & (kpos <= rows), 0.0, NEG)

        qrs_ref[hk] = stack_heads(qr_ref, hk)

    m_ref[...] = jnp.full(m_ref.shape, NEG, F32)
    l_ref[...] = jnp.zeros(l_ref.shape, F32)
    acc_ref[...] = jnp.zeros(acc_ref.shape, F32)

    def sel_step(kt, carry):
        k0 = pl.multiple_of(kt * tk, tk)
        for hk in range(NSA_KV_HEADS):
            kblk = kvs_d[hk, pl.ds(k0, tk), 0:KV_D]
            vblk = kvs_d[hk, pl.ds(k0, tk), KV_D:2 * KV_D]
            s = lax.dot_general(qrs_ref[hk], kblk, (((1,), (1,)), ((), ())), preferred_element_type=F32)
            s = s.reshape(hpg, tq, tk) + bias_ref[hk, kt][None]
            m_old = m_ref[hk]
            m_new = jnp.maximum(m_old, jnp.max(s, axis=-1, keepdims=True))
            alpha = jnp.exp(m_old - m_new)
            pe = jnp.exp(s - jnp.concatenate([m_new] * (tk // LANES), axis=-1))
            l_ref[hk] = alpha * l_ref[hk] + jnp.sum(pe, axis=-1, keepdims=True)
            pv = jnp.dot(pe.reshape(hpg * tq, tk).astype(BF16), vblk, preferred_element_type=F32)
            acc_ref[hk] = alpha * acc_ref[hk] + pv.reshape(hpg, tq, LANES)
            m_ref[hk] = m_new
        return carry

    lax.fori_loop(0, n_kt, sel_step, 0)

    for hk in range(NSA_KV_HEADS):
        o_cmp = o_cmp_g[hk]
        o_slc = acc_ref[hk] / l_ref[hk]
        kw = kvw_d[hk, pl.ds(win_start, WIN_KEYS), 0:KV_D]
        vw = kvw_d[hk, pl.ds(win_start, WIN_KEYS), KV_D:2 * KV_D]
        sw = lax.dot_general(qrs_ref[hk], kw, (((1,), (1,)), ((), ())), preferred_element_type=F32)
        sw = sw.reshape(hpg, tq, WIN_KEYS) + win_bias[None]
        pw = jnp.exp(sw - jnp.max(sw, axis=-1, keepdims=True))
        den = jnp.sum(pw, axis=-1, keepdims=True)
        o_win = jnp.dot(pw.reshape(hpg * tq, WIN_KEYS).astype(BF16), vw,
                        preferred_element_type=F32).reshape(hpg, tq, LANES) / den

        for hh in range(hpg):
            head = hk * hpg + hh
            p, e = head // 2, head % 2
            c0 = GATE_COL0 + head * 3
            mix = (sig[:, c0:c0 + 1] * o_cmp[hh] + sig[:, c0 + 1:c0 + 2] * o_slc[hh]
                   + sig[:, c0 + 2:c0 + 3] * o_win[hh])
            if e == 0:
                mix_even = mix
            else:
                o_ref[:, p * LANES:(p + 1) * LANES] = jnp.where(half_mask[0], mix_even, mix).astype(o_ref.dtype)


def _nsa_prompt(qc, qr, dtg, kvcmp, kvs, kvw, bn, t):
    nq = t // Q_BLOCK
    nbk = t // SEL_BLOCK
    hpg = NSA_HEADS // NSA_KV_HEADS
    assert nbk >= TOP_N and t >= WIN_KEYS and t % SEL_KEY_TILE == 0
    qrow = lambda w: pl.BlockSpec((Q_BLOCK, w), lambda b, i: (b * nq + i, 0))
    seq = lambda r: pl.BlockSpec((r, 2 * KV_D), lambda b, i: (b, 0))
    return pl.pallas_call(
        functools.partial(_nsa_prompt_kernel, t=t),
        grid=(bn, nq),
        in_specs=[qrow(NSA_D), qrow(NSA_D), qrow(LANES), seq(nbk), seq(t), seq(t)],
        out_specs=qrow(NSA_D),
        out_shape=jax.ShapeDtypeStruct((bn * t, NSA_D), BF16),
        scratch_shapes=[pltpu.VMEM((NSA_KV_HEADS, LANES, 2 * KV_D), BF16),
                        pltpu.VMEM((NSA_KV_HEADS, t, 2 * KV_D), BF16),
                        pltpu.VMEM((NSA_KV_HEADS, t, 2 * KV_D), BF16),
                        pltpu.VMEM((NSA_KV_HEADS, t // SEL_KEY_TILE, Q_BLOCK, SEL_KEY_TILE), F32),
                        pltpu.VMEM((NSA_KV_HEADS, hpg * Q_BLOCK, LANES), BF16),
                        pltpu.VMEM((NSA_KV_HEADS, hpg, Q_BLOCK, LANES), F32),
                        pltpu.VMEM((NSA_KV_HEADS, hpg, Q_BLOCK, LANES), F32),
                        pltpu.VMEM((NSA_KV_HEADS, hpg, Q_BLOCK, LANES), F32)],
        compiler_params=_cparams(("parallel", "arbitrary")),
        name="nsa_prompt",
    )(qc, qr, dtg, kvcmp, kvs, kvw)


def _outproj_kernel(ys_ref, yn_ref, h_ref, ws_ref, wn_ref, g_ref, b_ref, o_ref):
    mix = jnp.dot(ys_ref[...].astype(BF16), ws_ref[...], preferred_element_type=F32)
    mix = mix + jnp.dot(yn_ref[...].astype(BF16), wn_ref[...], preferred_element_type=F32)
    o_ref[...] = _layer_norm(DEEPNORM_ALPHA * h_ref[...] + mix, g_ref[...], b_ref[...])


def _outproj(y_ssd, y_nsa, h, w_ssd, w_nsa, ln_g, ln_b, tm):
    n = h.shape[0]
    row = lambda w: pl.BlockSpec((tm, w), lambda i: (i, 0))
    const = lambda a: pl.BlockSpec(a.shape, lambda i: (0,) * a.ndim)
    return pl.pallas_call(
        _outproj_kernel,
        grid=(n // tm,),
        in_specs=[row(SSD_D), row(NSA_D), row(D_MODEL), const(w_ssd), const(w_nsa), const(ln_g), const(ln_b)],
        out_specs=row(D_MODEL),
        out_shape=jax.ShapeDtypeStruct((n, D_MODEL), F32),
        compiler_params=_cparams(("parallel",)),
        name="outproj",
    )(y_ssd, y_nsa, h, w_ssd, w_nsa, ln_g, ln_b)


MOE_TOKENS = 256
ROUTE_ROWS = 8


def _token_tile_specs(n_main_tiles):
    main = pl.BlockSpec((MOE_TOKENS, D_MODEL), lambda i, *_: (jnp.minimum(i, n_main_tiles - 1), 0))
    tail = pl.BlockSpec((MOE_TOKENS, D_MODEL), lambda i, *_: (0, 0))
    return main, tail


def _token_tile(i, n_main_tiles, main_ref, tail_ref):
    return jnp.where(i < n_main_tiles, main_ref[...], tail_ref[...])


def _rank_rows(x):
    n = x.shape[0]
    idx = lax.broadcasted_iota(jnp.int32, x.shape, 0)
    rank = jnp.zeros(x.shape, F32)
    for r in range(n):
        row = x[r:r + 1, :]
        rank = rank + ((row > x) | ((row == x) & (idx > r))).astype(F32)
    return rank


def _route_kernel(h_ref, ht_ref, rw_ref, rb_ref, slot_ref, tokinfo_ref, meta_ref, cnt_ref, carry_ref, carry_row_ref, *,
                  n_valid, n_main):
    i = pl.program_id(0)
    tm = MOE_TOKENS

    @pl.when(i == 0)
    def _():
        carry_ref[...] = jnp.zeros_like(carry_ref)
        carry_row_ref[...] = jnp.zeros_like(carry_row_ref)

    logits = lax.dot_general(rw_ref[...], _token_tile(i, n_main, h_ref, ht_ref).astype(BF16), (((1,), (1,)), ((), ())),
                             preferred_element_type=F32)
    scores = _sigmoid(logits)
    biased = scores + rb_ref[:, 0:1]
    b3 = biased.reshape(N_EXPERT_GROUPS, EXPERTS_PER_GROUP, tm)
    sidx = lax.broadcasted_iota(jnp.int32, b3.shape, 1)
    m1 = jnp.max(b3, axis=1, keepdims=True)
    first = jnp.min(jnp.where(b3 == m1, sidx, EXPERTS_PER_GROUP), axis=1, keepdims=True)
    m2 = jnp.max(jnp.where(sidx == first, -jnp.inf, b3), axis=1, keepdims=True)
    grp_score = (m1 + m2).reshape(N_EXPERT_GROUPS, tm)
    grp_keep = _rank_rows(grp_score) < TOPK_GROUPS
    masked = jnp.where(grp_keep.reshape(N_EXPERT_GROUPS, 1, tm), b3, NEG).reshape(N_EXPERTS, tm)
    rank = _rank_rows(masked)
    tok = i * tm + lax.broadcasted_iota(jnp.int32, (1, tm), 1)
    valid = tok < n_valid
    sel = (rank < TOP_K) & valid
    self32 = sel.astype(F32)
    wsel = self32 * scores
    wsum = jnp.sum(wsel, axis=0, keepdims=True)
    w = wsel / jnp.where(wsum > 0.0, wsum, 1.0) * ROUTED_SCALE

    selb = sel.astype(BF16)
    tri = lambda n, strict_upper: (
        (lax.broadcasted_iota(jnp.int32, (n, n), 0) < lax.broadcasted_iota(jnp.int32, (n, n), 1))
        if strict_upper else
        (lax.broadcasted_iota(jnp.int32, (n, n), 0) > lax.broadcasted_iota(jnp.int32, (n, n), 1))).astype(BF16)
    pad8 = lambda c: jnp.floor((c + (SUBLANES - 1.0)) * (1.0 / SUBLANES)) * SUBLANES
    pos_tile = jnp.dot(selb, tri(tm, True), preferred_element_type=F32)
    cnt_col = pad8(jnp.sum(self32, axis=1, keepdims=True))
    first_col = jnp.dot(tri(N_EXPERTS, False), jnp.broadcast_to(cnt_col, (N_EXPERTS, LANES)).astype(BF16),
                        preferred_element_type=F32)[:, 0:1]
    slot = first_col + pos_tile

    sel_pad = jnp.concatenate([selb, jnp.zeros((LANES - N_EXPERTS, tm), BF16)], axis=0)
    cnt_row = pad8(lax.dot_general(jnp.ones((SUBLANES, tm), BF16), sel_pad, (((1,), (1,)), ((), ())),
                                   preferred_element_type=F32))
    first_row = jnp.dot(cnt_row.astype(BF16), tri(LANES, True), preferred_element_type=F32)
    prev_row = carry_row_ref[...]
    meta = jnp.concatenate([cnt_row[0:1], first_row[0:1], prev_row[0:1], jnp.zeros((SUBLANES - 3, LANES), F32)], 0)
    meta_ref[0] = meta.astype(jnp.int32)
    carry_row_ref[...] = prev_row + cnt_row
    carry_ref[...] = carry_ref[...] + cnt_col

    slot_rows, w_rows = [], []
    for k in range(TOP_K):
        hit = (rank == k) & sel
        slot_rows.append(jnp.sum(jnp.where(hit, slot, 0.0), axis=0, keepdims=True))
        w_rows.append(jnp.sum(jnp.where(hit, w, 0.0), axis=0, keepdims=True))
    slot_rows = [jnp.where(valid, r, -1.0) for r in slot_rows]
    pad2 = jnp.zeros((ROUTE_ROWS - TOP_K, tm), F32)
    slot_ref[...] = jnp.concatenate(slot_rows + [pad2 - 1.0], 0).astype(jnp.int32)
    info = jnp.concatenate(w_rows + [pad2] + slot_rows + [jnp.zeros((LANES - ROUTE_ROWS - TOP_K, tm), F32)], 0)
    tokinfo_ref[...] = info.T

    @pl.when(i == pl.num_programs(0) - 1)
    def _():
        cnt_ref[...] = jnp.broadcast_to(carry_ref[:, 0:1], cnt_ref.shape)


def _route(h_main, h_tail, router_wt, router_bias_col, n_valid):
    tm = MOE_TOKENS
    n_main = h_main.shape[0] // tm
    n = h_main.shape[0] + tm
    const = lambda a: pl.BlockSpec(a.shape, lambda i: (0,) * a.ndim)
    return pl.pallas_call(
        functools.partial(_route_kernel, n_valid=n_valid, n_main=n_main),
        grid=(n // tm,),
        in_specs=[*_token_tile_specs(n_main), const(router_wt), const(router_bias_col)],
        out_specs=[pl.BlockSpec((ROUTE_ROWS, tm), lambda i: (0, i)),
                   pl.BlockSpec((tm, LANES), lambda i: (i, 0)),
                   pl.BlockSpec((1, SUBLANES, LANES), lambda i: (i, 0, 0)),
                   pl.BlockSpec((N_EXPERTS, LANES), lambda i: (0, 0))],
        out_shape=[jax.ShapeDtypeStruct((ROUTE_ROWS, n), jnp.int32),
                   jax.ShapeDtypeStruct((n, LANES), F32),
                   jax.ShapeDtypeStruct((n // tm, SUBLANES, LANES), jnp.int32),
                   jax.ShapeDtypeStruct((N_EXPERTS, LANES), F32)],
        scratch_shapes=[pltpu.VMEM((N_EXPERTS, LANES), F32), pltpu.VMEM((SUBLANES, LANES), F32)],
        compiler_params=_cparams(("arbitrary",)),
        name="moe_route",
    )(h_main, h_tail, router_wt, router_bias_col)


PACKED_D = D_MODEL // 2
U32 = jnp.uint32


def _pack_bf16_pairs(x):
    hi = lax.bitcast_convert_type(x[:, 0:PACKED_D], U32) & jnp.uint32(0xFFFF0000)
    lo = lax.shift_right_logical(lax.bitcast_convert_type(x[:, PACKED_D:D_MODEL], U32), jnp.uint32(16))
    return hi | lo


def _unpack_bf16_pairs(w):
    hi = lax.bitcast_convert_type(w & jnp.uint32(0xFFFF0000), F32)
    lo = lax.bitcast_convert_type(lax.shift_left(w, jnp.uint32(16)), F32)
    return jnp.concatenate([hi, lo], axis=1).astype(BF16)


def _round_bf16(x):
    return x.astype(BF16).astype(F32)


TILE_SLOTS = MOE_TOKENS * TOP_K + N_EXPERTS * SUBLANES
RUN_CHUNKS = tuple(1 << b for b in range(int(math.log2(MOE_TOKENS)), int(math.log2(SUBLANES)) - 1, -1))


def _run_copy(src_ref, src_row, dst_ref, dst_row, rows, sem):
    return pltpu.make_async_copy(src_ref.at[pl.ds(pl.multiple_of(src_row, SUBLANES), rows)],
                                 dst_ref.at[pl.ds(pl.multiple_of(dst_row, SUBLANES), rows)], sem)


def _start_run(src_ref, src_row, dst_ref, dst_row, n, sem, started):
    off = jnp.int32(0)
    out = []
    for c, rows in enumerate(RUN_CHUNKS):
        take = (n & rows) != 0

        @pl.when(take)
        def _(off=off, rows=rows):
            _run_copy(src_ref, src_row + off, dst_ref, dst_row + off, rows, sem).start()

        inc = take.astype(jnp.int32)
        off = off + inc * rows
        out.append(started[c] + inc)
    return tuple(out)


def _wait_runs(src_ref, dst_ref, sem, started):
    for c, rows in enumerate(RUN_CHUNKS):
        def wait_one(j, carry, rows=rows):
            _run_copy(src_ref, 0, dst_ref, 0, rows, sem).wait()
            return carry

        lax.fori_loop(0, started[c], wait_one, 0)


def _dispatch_kernel(start_ref, cnt_ref, meta_ref, slot_ref, x_ref, xt_ref, xs_ref, sorted_ref, zero_ref, sem, zsem, *,
                     cap, n_main):
    i = pl.program_id(0)
    tm = MOE_TOKENS

    @pl.when(i == 0)
    def _():
        zero_ref[...] = jnp.zeros_like(zero_ref)

        def fill_expert(e, started):
            lo = start_ref[e] + cnt_ref[e]
            hi = jnp.where(e == N_EXPERTS - 1, cap, start_ref[jnp.minimum(e + 1, N_EXPERTS - 1)])
            n_full = (hi - lo) // tm

            def fill_full(j, st):
                return _start_run(zero_ref, 0, xs_ref, lo + j * tm, jnp.int32(tm), zsem, st)

            started = lax.fori_loop(0, n_full, fill_full, started)
            return _start_run(zero_ref, 0, xs_ref, lo + n_full * tm, (hi - lo) - n_full * tm, zsem, started)

        filled = lax.fori_loop(0, N_EXPERTS, fill_expert, tuple(jnp.int32(0) for _ in RUN_CHUNKS))
        _wait_runs(zero_ref, xs_ref, zsem, filled)

    srow = lax.broadcasted_iota(jnp.int32, (TILE_SLOTS, tm), 0)
    onehot = srow == slot_ref[0:1, :]
    for k in range(1, TOP_K):
        onehot = onehot | (srow == slot_ref[k:k + 1, :])
    sorted_ref[...] = _pack_bf16_pairs(jnp.dot(onehot.astype(BF16), _token_tile(i, n_main, x_ref, xt_ref).astype(BF16),
                                               preferred_element_type=F32))

    def copy_expert(e, started):
        n = meta_ref[0, 0, e]
        return _start_run(sorted_ref, meta_ref[0, 1, e], xs_ref, start_ref[e] + meta_ref[0, 2, e], n, sem, started)

    started = lax.fori_loop(0, N_EXPERTS, copy_expert, tuple(jnp.int32(0) for _ in RUN_CHUNKS))
    _wait_runs(sorted_ref, xs_ref, sem, started)


def _dispatch(h_main, h_tail, slot_t, meta, seg_start, counts, cap):
    tm = MOE_TOKENS
    n_main = h_main.shape[0] // tm
    return pl.pallas_call(
        functools.partial(_dispatch_kernel, cap=cap, n_main=n_main),
        grid_spec=pltpu.PrefetchScalarGridSpec(
            num_scalar_prefetch=2,
            grid=(n_main + 1,),
            in_specs=[pl.BlockSpec((1, SUBLANES, LANES), lambda i, *_: (i, 0, 0), memory_space=pltpu.SMEM),
                      pl.BlockSpec((ROUTE_ROWS, tm), lambda i, *_: (0, i)),
                      *_token_tile_specs(n_main)],
            out_specs=pl.BlockSpec(memory_space=pl.ANY),
            scratch_shapes=[pltpu.VMEM((TILE_SLOTS, PACKED_D), U32), pltpu.VMEM((tm, PACKED_D), U32),
                            pltpu.SemaphoreType.DMA, pltpu.SemaphoreType.DMA]),
        out_shape=jax.ShapeDtypeStruct((cap, PACKED_D), U32),
        compiler_params=_cparams(("arbitrary",)),
        name="moe_dispatch",
    )(seg_start, counts, meta, slot_t, h_main, h_tail)


def _swiglu(x, wg, wu, wd):
    xb = x.astype(BF16)
    g = jnp.dot(xb, wg.astype(BF16), preferred_element_type=F32)
    u = jnp.dot(xb, wu.astype(BF16), preferred_element_type=F32)
    return jnp.dot((_silu(g) * u).astype(BF16), wd.astype(BF16), preferred_element_type=F32)


EXPERT_RING = 3


def _experts_kernel(be_ref, used_ref, xs_ref, wg_ref, wu_ref, wd_ref, y_ref, xbuf, sems):
    i = pl.program_id(0)
    n = pl.num_programs(0)

    def block_copy(blk):
        slot = lax.rem(blk, EXPERT_RING)
        rows = pl.ds(pl.multiple_of(blk * MOE_BLOCK, MOE_BLOCK), MOE_BLOCK)
        return pltpu.make_async_copy(xs_ref.at[rows], xbuf.at[slot], sems.at[slot])

    @pl.when(i == 0)
    def _():
        for j in range(EXPERT_RING - 1):
            block_copy(jnp.int32(j)).start()

    @pl.when(i + EXPERT_RING - 1 < n)
    def _():
        block_copy(i + EXPERT_RING - 1).start()

    block_copy(i).wait()

    @pl.when(i < used_ref[0])
    def _():
        y = _swiglu(_unpack_bf16_pairs(xbuf[lax.rem(i, EXPERT_RING)]), wg_ref[0], wu_ref[0], wd_ref[0])
        y_ref[...] = _pack_bf16_pairs(_round_bf16(y))

    @pl.when(i >= used_ref[0])
    def _():
        y_ref[...] = jnp.zeros_like(y_ref)


def _experts(xs, block_expert, used_blocks, w_gate, w_up, w_down):
    cap = xs.shape[0]
    return pl.pallas_call(
        _experts_kernel,
        grid_spec=pltpu.PrefetchScalarGridSpec(
            num_scalar_prefetch=2,
            grid=(cap // MOE_BLOCK,),
            in_specs=[pl.BlockSpec(memory_space=pl.ANY),
                      pl.BlockSpec((1, D_MODEL, D_EXPERT), lambda i, be, used: (be[i], 0, 0)),
                      pl.BlockSpec((1, D_MODEL, D_EXPERT), lambda i, be, used: (be[i], 0, 0)),
                      pl.BlockSpec((1, D_EXPERT, D_MODEL), lambda i, be, used: (be[i], 0, 0))],
            out_specs=pl.BlockSpec((MOE_BLOCK, PACKED_D), lambda i, be, used: (i, 0)),
            scratch_shapes=[pltpu.VMEM((EXPERT_RING, MOE_BLOCK, PACKED_D), U32),
                            pltpu.SemaphoreType.DMA((EXPERT_RING,))]),
        out_shape=jax.ShapeDtypeStruct((cap, PACKED_D), U32),
        compiler_params=_cparams(("arbitrary",)),
        name="moe_experts",
    )(block_expert, used_blocks, xs, w_gate, w_up, w_down)


def _combine_kernel(start_ref, meta_ref, h_ref, ht_ref, info_ref, sg_ref, su_ref, sd_ref, g_ref, b_ref,
                    ys_ref, o_ref, ot_ref, buf_ref, sem, *, n_main):
    i = pl.program_id(0)
    tm = MOE_TOKENS

    @pl.when(i == 0)
    def _():
        buf_ref[...] = jnp.zeros_like(buf_ref)

    def fetch_expert(e, started):
        n = meta_ref[0, 0, e]
        return _start_run(ys_ref, start_ref[e] + meta_ref[0, 2, e], buf_ref, meta_ref[0, 1, e], n, sem, started)

    started = lax.fori_loop(0, N_EXPERTS, fetch_expert, tuple(jnp.int32(0) for _ in RUN_CHUNKS))
    h = _token_tile(i, n_main, h_ref, ht_ref)
    f = _swiglu(h, sg_ref[...], su_ref[...], sd_ref[...])
    info = info_ref[...]
    scol = lax.broadcasted_iota(jnp.int32, (tm, TILE_SLOTS), 1).astype(F32)
    mix = jnp.zeros((tm, TILE_SLOTS), F32)
    for k in range(TOP_K):
        mix = jnp.where(info[:, ROUTE_ROWS + k:ROUTE_ROWS + k + 1] == scol, info[:, k:k + 1], mix)
    _wait_runs(ys_ref, buf_ref, sem, started)
    acc = jnp.dot(mix.astype(BF16), _unpack_bf16_pairs(buf_ref[...]), preferred_element_type=F32)
    out = _layer_norm(DEEPNORM_ALPHA * h + (acc + f), g_ref[...], b_ref[...])

    @pl.when(i < n_main)
    def _():
        o_ref[...] = out

    @pl.when(i >= n_main)
    def _():
        ot_ref[...] = out


def _combine(h_main, h_tail, ys, meta, tokinfo, seg_start, sh_gate, sh_up, sh_down, ln_g, ln_b):
    tm = MOE_TOKENS
    n_main = h_main.shape[0] // tm
    const = lambda a: pl.BlockSpec(a.shape, lambda i, *_: (0,) * a.ndim)
    return pl.pallas_call(
        functools.partial(_combine_kernel, n_main=n_main),
        grid_spec=pltpu.PrefetchScalarGridSpec(
            num_scalar_prefetch=1,
            grid=(n_main + 1,),
            in_specs=[pl.BlockSpec((1, SUBLANES, LANES), lambda i, *_: (i, 0, 0), memory_space=pltpu.SMEM),
                      *_token_tile_specs(n_main),
                      pl.BlockSpec((tm, LANES), lambda i, *_: (i, 0)),
                      const(sh_gate), const(sh_up), const(sh_down), const(ln_g), const(ln_b),
                      pl.BlockSpec(memory_space=pl.ANY)],
            out_specs=list(_token_tile_specs(n_main)),
            scratch_shapes=[pltpu.VMEM((TILE_SLOTS, PACKED_D), U32), pltpu.SemaphoreType.DMA]),
        out_shape=[jax.ShapeDtypeStruct(h_main.shape, F32), jax.ShapeDtypeStruct((tm, D_MODEL), F32)],
        compiler_params=_cparams(("arbitrary",)),
        name="moe_combine",
    )(seg_start, meta, h_main, h_tail, tokinfo, sh_gate, sh_up, sh_down, ln_g, ln_b, ys)


def _moe_ln(h_main, h_tail, n_valid, router_w, router_bias, w_gate, w_up, w_down, sh_gate, sh_up, sh_down, ln_g, ln_b):
    n_tiles = h_main.shape[0] // MOE_TOKENS + 1
    slot_t, tokinfo, meta, cnt = _route(h_main, h_tail, router_w.T.astype(BF16),
                                        jnp.broadcast_to(router_bias.astype(F32)[:, None], (N_EXPERTS, LANES)), n_valid)
    counts = cnt[:, 0].astype(jnp.int32)
    padded = (counts + MOE_BLOCK - 1) // MOE_BLOCK * MOE_BLOCK
    seg_end = jnp.cumsum(padded)
    seg_start = seg_end - padded
    run_pad = n_tiles * N_EXPERTS * (SUBLANES - 1)
    n_blocks = -(-(n_valid * TOP_K + run_pad + N_EXPERTS * (MOE_BLOCK - 1)) // MOE_BLOCK)
    cap = n_blocks * MOE_BLOCK
    block_first_row = jnp.arange(n_blocks, dtype=jnp.int32) * MOE_BLOCK
    block_expert = jnp.minimum(jnp.sum((seg_end[None, :] <= block_first_row[:, None]).astype(jnp.int32), axis=1),
                               N_EXPERTS - 1)
    xs = _dispatch(h_main, h_tail, slot_t, meta, seg_start, counts, cap)
    used_blocks = (seg_end[N_EXPERTS - 1:] // MOE_BLOCK).astype(jnp.int32)
    ys = _experts(xs, block_expert, used_blocks, w_gate, w_up, w_down)
    return _combine(h_main, h_tail, ys, meta, tokinfo, seg_start, sh_gate.astype(BF16), sh_up.astype(BF16),
                    sh_down.astype(BF16), ln_g, ln_b)


def _ssd_sample_kernel(xbc_ref, z_ref, dtg_ref, sconv_ref, s0_ref, convw_ref, convb_ref, dtb_ref, alog_ref,
                       dskip_ref, normw_ref, y_ref, s_ref, conv_out_ref, xc_ref, dt_ref, da_ref):
    b = pl.program_id(0)

    @pl.when(b == 0)
    def _():
        xin = xbc_ref[...]
        xc = convw_ref[SSD_CONV - 1:SSD_CONV, :] * xin
        for k in range(SSD_CONV - 1):
            xc = xc + convw_ref[k:k + 1, :] * sconv_ref[k]
        xc_ref[...] = _silu(xc + convb_ref[...])
        dt = _softplus(dtg_ref[...] + dtb_ref[...])
        dt_ref[...] = dt
        da_ref[...] = jnp.exp(dt * (-jnp.exp(alog_ref[...])))
        for k in range(SSD_CONV - 2):
            conv_out_ref[k] = sconv_ref[k + 1]
        conv_out_ref[SSD_CONV - 2] = xin

    xc = xc_ref[pl.ds(b, 1), :]
    dt = dt_ref[pl.ds(b, 1), :]
    da = da_ref[pl.ds(b, 1), :]
    ns = SSD_GROUPS * SSD_STATE
    eye = (lax.broadcasted_iota(jnp.int32, (HEAD_DIM, HEAD_DIM), 0)
           == lax.broadcasted_iota(jnp.int32, (HEAD_DIM, HEAD_DIM), 1))
    hpg = SSD_HEADS // SSD_GROUPS
    y_parts = []
    for h in range(SSD_HEADS):
        g = h // hpg
        x_h = xc[:, h * HEAD_DIM:(h + 1) * HEAD_DIM]
        b_g = xc[:, SSD_D + g * SSD_STATE:SSD_D + (g + 1) * SSD_STATE]
        c_g = xc[:, SSD_D + ns + g * SSD_STATE:SSD_D + ns + (g + 1) * SSD_STATE]
        xdt_col = jnp.sum(jnp.where(eye, x_h * dt[:, h:h + 1], 0.0), axis=1, keepdims=True)
        s_new = da[:, h:h + 1] * s0_ref[0, h] + xdt_col * b_g
        s_ref[0, h] = s_new
        y_h = _bdot_nt(c_g, s_new) + dskip_ref[:, h * HEAD_DIM:(h + 1) * HEAD_DIM] * x_h
        y_parts.append(y_h)
    y = jnp.concatenate(y_parts, axis=1)
    y_ref[pl.ds(b, 1), :] = _gated_group_norm(y, z_ref[pl.ds(b, 1), :], normw_ref[...])


def _ssd_sample(xbc, z, dtg, state_conv_t, state_ssm, conv_w, conv_b, dt_bias_pad, a_log_pad, d_skip_full, norm_w):
    bs = xbc.shape[0]
    const = lambda a: pl.BlockSpec(a.shape, lambda b: (0,) * a.ndim)
    state_spec = pl.BlockSpec((1, SSD_HEADS, HEAD_DIM, SSD_STATE), lambda b: (b, 0, 0, 0))
    return pl.pallas_call(
        _ssd_sample_kernel,
        grid=(bs,),
        in_specs=[const(xbc), const(z), const(dtg), const(state_conv_t), state_spec, const(conv_w), const(conv_b),
                  const(dt_bias_pad), const(a_log_pad), const(d_skip_full), const(norm_w)],
        out_specs=[pl.BlockSpec((bs, SSD_D), lambda b: (0, 0)), state_spec,
                   pl.BlockSpec((SSD_CONV - 1, bs, SSD_CONV_CH), lambda b: (0, 0, 0))],
        out_shape=[jax.ShapeDtypeStruct((bs, SSD_D), F32),
                   jax.ShapeDtypeStruct(state_ssm.shape, F32),
                   jax.ShapeDtypeStruct((SSD_CONV - 1, bs, SSD_CONV_CH), F32)],
        scratch_shapes=[pltpu.VMEM((bs, SSD_CONV_CH), F32), pltpu.VMEM((bs, LANES), F32),
                        pltpu.VMEM((bs, LANES), F32)],
        compiler_params=_cparams(("arbitrary",)),
        name="ssd_sample",
    )(xbc, z, dtg, state_conv_t, state_ssm, conv_w, conv_b, dt_bias_pad, a_log_pad, d_skip_full, norm_w)


SEL_PAST = TOP_N - 1
BLOCKS_PER_PAGE = PAGE_SIZE // CMP_BLOCK
KV_FEATS = 2 * KV_D


def _compress_consts_t(cmp_pe, cmp_w1, cmp_b1, cmp_w2, cmp_b2):
    pe_t = jnp.stack([jnp.tile(cmp_pe[k].T, (1, BLOCKS_PER_PAGE)) for k in range(2)])
    w1_t = jnp.stack([_block_diag2(jnp.swapaxes(cmp_w1[k], 0, 1)) for k in range(2)]).astype(BF16)
    b1_t = jnp.stack([jnp.tile(cmp_b1[k], BLOCKS_PER_PAGE) for k in range(2)])[:, None, :]
    w2_t = jnp.stack([_block_diag2(cmp_w2[k]) for k in range(2)]).astype(BF16)
    b2_t = jnp.stack([jnp.tile(cmp_b2[k], BLOCKS_PER_PAGE) for k in range(2)])[:, None, :]
    return pe_t, w1_t, b1_t, w2_t, b2_t


def _compress_pages_kernel(pt_ref, pe_ref, w1_ref, b1_ref, w2_ref, b2_ref, pool_ref, o_ref, kbuf, vbuf, sems, *,
                           n_pages):
    b = pl.program_id(0)
    nb = pl.num_programs(0)
    bufs = (kbuf, vbuf)

    def half_copy(seq, kind, p):
        return pltpu.make_async_copy(pool_ref.at[pt_ref[seq, p], pl.ds(kind * KV_D, KV_D)],
                                     bufs[kind].at[:, p], sems.at[kind])

    def start_half(seq, kind):
        def start_pair(q, c):
            half_copy(seq, kind, 2 * q).start(priority=0)
            half_copy(seq, kind, 2 * q + 1).start(priority=1)
            return c

        lax.fori_loop(0, n_pages // 2, start_pair, 0)

    def wait_half(seq, kind):
        lax.fori_loop(0, n_pages, lambda p, c: (half_copy(seq, kind, p).wait(), c)[1], 0)

    @pl.when(b == 0)
    def _():
        start_half(b, 0)
        start_half(b, 1)

    for kind in range(2):
        wait_half(b, kind)
        def add_feature(d, acc, kind=kind):
            x = jnp.concatenate([bufs[kind][h * HEAD_DIM + d] for h in range(NSA_KV_HEADS)], axis=0) \
                + pe_ref[kind, pl.ds(d, 1), :]
            return acc + jnp.dot(x.astype(BF16), w1_ref[kind, d], preferred_element_type=F32)

        acc = lax.fori_loop(0, HEAD_DIM, add_feature,
                            jnp.zeros((NSA_KV_HEADS * n_pages, BLOCKS_PER_PAGE * CMP_HIDDEN), F32), unroll=8)
        hid = _silu(acc + b1_ref[kind])
        out = jnp.dot(hid.astype(BF16), w2_ref[kind], preferred_element_type=F32) + b2_ref[kind]
        for h in range(NSA_KV_HEADS):
            o_ref[0, kind * NSA_KV_HEADS + h] = out[h * n_pages:(h + 1) * n_pages]

        @pl.when(b + 1 < nb)
        def _(kind=kind):
            start_half(b + 1, kind)


def _compress_pages(pool_t, page_table, consts):
    bs, n_pages = page_table.shape
    const = lambda a: pl.BlockSpec(a.shape, lambda b, pt: (0,) * a.ndim)
    return pl.pallas_call(
        functools.partial(_compress_pages_kernel, n_pages=n_pages),
        grid_spec=pltpu.PrefetchScalarGridSpec(
            num_scalar_prefetch=1,
            grid=(bs,),
            in_specs=[const(a) for a in consts] + [pl.BlockSpec(memory_space=pl.ANY)],
            out_specs=pl.BlockSpec((1, 2 * NSA_KV_HEADS, n_pages, LANES), lambda b, pt: (b, 0, 0, 0)),
            scratch_shapes=[pltpu.VMEM((KV_D, n_pages, PAGE_SIZE), F32), pltpu.VMEM((KV_D, n_pages, PAGE_SIZE), F32),
                            pltpu.SemaphoreType.DMA((2,))]),
        out_shape=jax.ShapeDtypeStruct((bs, 2 * NSA_KV_HEADS, n_pages, LANES), F32),
        compiler_params=_cparams(("arbitrary",)),
        name="compress_pages",
    )(page_table, *consts, pool_t)


def _group_heads(q_row, hk):
    hpg = NSA_HEADS // NSA_KV_HEADS
    low = lax.broadcasted_iota(jnp.int32, (1, LANES), 1) < HEAD_DIM
    rows = []
    for r in range(hpg):
        head = hk * hpg + r
        tile = q_row[:, (head // 2) * LANES:(head // 2 + 1) * LANES]
        if head % 2 == 1:
            tile = pltpu.roll(tile, HEAD_DIM, 1)
        rows.append(jnp.where(low, tile, 0.0))
    return jnp.concatenate(rows + [jnp.zeros((SUBLANES - hpg, LANES), F32)], axis=0)


def _spread_heads(o_groups):
    hpg = NSA_HEADS // NSA_KV_HEADS
    return jnp.concatenate([o[r:r + 1, 0:HEAD_DIM] for o in o_groups for r in range(hpg)], axis=1)


def _nsa_sample_cmp_t_kernel(qc_ref, cmp_ref, ocmp_ref, idx_ref, *, n_pages):
    b = pl.program_id(0)
    nc = n_pages * BLOCKS_PER_PAGE
    scale = HEAD_DIM ** -0.5
    hpg = NSA_HEADS // NSA_KV_HEADS
    q_row = qc_ref[pl.ds(b, 1), :] * scale
    lane = lax.broadcasted_iota(jnp.int32, (1, LANES), 1)
    pos_r = lax.broadcasted_iota(jnp.int32, (1, nc), 1)
    bid_r = (pos_r % n_pages) * BLOCKS_PER_PAGE + pos_r // n_pages
    pos_c = lax.broadcasted_iota(jnp.int32, (nc, 1), 0)
    bid_c = (pos_c % n_pages) * BLOCKS_PER_PAGE + pos_c // n_pages
    o_groups = []
    for hk in range(NSA_KV_HEADS):
        kc = cmp_ref[0, hk].astype(BF16)
        vc = cmp_ref[0, NSA_KV_HEADS + hk].astype(BF16)
        qh = _group_heads(q_row, hk)
        s = jnp.concatenate(
            [lax.dot_general(pltpu.roll(qh, c * HEAD_DIM, 1).astype(BF16) if c else qh.astype(BF16), kc,
                             (((1,), (1,)), ((), ())), preferred_element_type=F32)
             for c in range(BLOCKS_PER_PAGE)], axis=1)
        ex = jnp.exp(s - jnp.max(s, axis=-1, keepdims=True))
        p = ex / jnp.sum(ex, axis=-1, keepdims=True)
        o = jnp.dot(p[:, 0:n_pages].astype(BF16), vc, preferred_element_type=F32)
        for c in range(1, BLOCKS_PER_PAGE):
            oc = jnp.dot(p[:, c * n_pages:(c + 1) * n_pages].astype(BF16), vc, preferred_element_type=F32)
            o = o + pltpu.roll(oc, LANES - c * HEAD_DIM, 1)
        o_groups.append(o)
        hrow = lax.broadcasted_iota(jnp.int32, p.shape, 0) < hpg
        imp = jnp.sum(jnp.where(hrow, p, 0.0), axis=0, keepdims=True)
        score = jnp.where((bid_r == 0) | (bid_r == nc - 1), FORCED_SCORE, imp)
        score_col = jnp.concatenate([score, jnp.zeros((LANES - 1, nc), F32)], 0).T[:, 0:1]
        beats = (score_col > score) | ((score_col == score) & (bid_c < bid_r))
        rank = jnp.sum(beats.astype(F32), axis=0, keepdims=True)
        row = jnp.zeros((1, LANES), F32)
        bid_f = bid_r.astype(F32)
        for k in range(SEL_PAST):
            blk = jnp.sum(jnp.where(rank == k, bid_f, 0.0), axis=1, keepdims=True)
            row = jnp.where(lane == k, blk, row)
        idx_ref[pl.ds(b * NSA_KV_HEADS + hk, 1), :] = row.astype(jnp.int32)
    ocmp_ref[pl.ds(b, 1), :] = _spread_heads(o_groups)


def _nsa_sample_cmp_t(qc, kvcmp_t):
    bs, _, n_pages, _ = kvcmp_t.shape
    return pl.pallas_call(
        functools.partial(_nsa_sample_cmp_t_kernel, n_pages=n_pages),
        grid=(bs,),
        in_specs=[pl.BlockSpec((bs, NSA_D), lambda b: (0, 0)),
                  pl.BlockSpec((1, 2 * NSA_KV_HEADS, n_pages, LANES), lambda b: (b, 0, 0, 0))],
        out_specs=[pl.BlockSpec((bs, NSA_D), lambda b: (0, 0)),
                   pl.BlockSpec((bs * NSA_KV_HEADS, LANES), lambda b: (0, 0))],
        out_shape=[jax.ShapeDtypeStruct((bs, NSA_D), F32),
                   jax.ShapeDtypeStruct((bs * NSA_KV_HEADS, LANES), jnp.int32)],
        compiler_params=_cparams(("arbitrary",)),
        name="nsa_sample_cmp",
    )(qc, kvcmp_t)


def _sel_block_copies(pool_ref, pt_ref, sel_ref, kbuf, vbuf, sem, b, hk, k):
    blk = sel_ref[b * NSA_KV_HEADS + hk, k]
    page = pt_ref[b, lax.shift_right_logical(blk, int(math.log2(BLOCKS_PER_PAGE)))]
    j = hk * SEL_PAST + k
    return (pltpu.make_async_copy(pool_ref.at[page, pl.ds(hk * HEAD_DIM, HEAD_DIM)], kbuf.at[j], sem),
            pltpu.make_async_copy(pool_ref.at[page, pl.ds(KV_D + hk * HEAD_DIM, HEAD_DIM)], vbuf.at[j], sem))


def _nsa_sample_attn_t_kernel(pt_ref, sel_ref, qr_ref, new_sel_ref, new_win_ref, win_ref, dtg_ref, ocmp_ref,
                              pool_ref, o_ref, kbuf, vbuf, sem):
    b = pl.program_id(0)
    for hk in range(NSA_KV_HEADS):
        for k in range(SEL_PAST):
            for cp in _sel_block_copies(pool_ref, pt_ref, sel_ref, kbuf, vbuf, sem, b, hk, k):
                cp.start()
    for hk in range(NSA_KV_HEADS):
        for k in range(SEL_PAST):
            for cp in _sel_block_copies(pool_ref, pt_ref, sel_ref, kbuf, vbuf, sem, b, hk, k):
                cp.wait()
    scale = HEAD_DIM ** -0.5
    q_row = qr_ref[pl.ds(b, 1), :] * scale
    sig = _sigmoid(dtg_ref[pl.ds(b, 1), :])
    lane = lax.broadcasted_iota(jnp.int32, (1, PAGE_SIZE), 1)
    o_slc, o_win = [], []
    for hk in range(NSA_KV_HEADS):
        qh = _group_heads(q_row, hk)[:, 0:HEAD_DIM].astype(BF16)

        def new_row(ref, kind):
            t = ref[pl.ds(b, 1), :][:, kind * KV_D:(kind + 1) * KV_D]
            if hk == 1:
                t = pltpu.roll(t, HEAD_DIM, 1)
            return t[:, 0:HEAD_DIM].astype(BF16).astype(F32)

        def attend(kt, vt, mask, new_ref, n_new):
            s = jnp.dot(qh, kt.astype(BF16), preferred_element_type=F32)
            if mask is not None:
                s = jnp.where(mask, s, NEG)
            s_new = jnp.sum(qh.astype(F32) * new_row(new_ref, 0), axis=1, keepdims=True)
            m = jnp.maximum(jnp.max(s, axis=-1, keepdims=True), s_new)
            ex = jnp.exp(s - m)
            ex_new = jnp.exp(s_new - m) * n_new
            den = jnp.sum(ex, axis=-1, keepdims=True) + ex_new
            o = lax.dot_general((ex / den).astype(BF16), vt.astype(BF16), (((1,), (1,)), ((), ())),
                                preferred_element_type=F32)
            return o + (ex_new / den).astype(BF16).astype(F32) * new_row(new_ref, 1)

        kt = jnp.concatenate([kbuf[hk * SEL_PAST + k] for k in range(SEL_PAST)], axis=1)
        vt = jnp.concatenate([vbuf[hk * SEL_PAST + k] for k in range(SEL_PAST)], axis=1)
        mask = jnp.concatenate(
            [lane // SEL_BLOCK == (sel_ref[b * NSA_KV_HEADS + hk, k] & (BLOCKS_PER_PAGE - 1))
             for k in range(SEL_PAST)], axis=1)
        o_slc.append(attend(kt, vt, mask, new_sel_ref, float(SEL_BLOCK)))
        o_win.append(attend(win_ref[0, hk * HEAD_DIM:(hk + 1) * HEAD_DIM, :],
                            win_ref[0, KV_D + hk * HEAD_DIM:KV_D + (hk + 1) * HEAD_DIM, :], None, new_win_ref, 1.0))
    gates = []
    for br in range(3):
        gates.append(jnp.concatenate(
            [jnp.broadcast_to(sig[:, GATE_COL0 + h * 3 + br:GATE_COL0 + h * 3 + br + 1], (1, HEAD_DIM))
             for h in range(NSA_HEADS)], axis=1))
    o_ref[pl.ds(b, 1), :] = (gates[0] * ocmp_ref[pl.ds(b, 1), :] + gates[1] * _spread_heads(o_slc)
                             + gates[2] * _spread_heads(o_win))


def _nsa_sample_attn_t(qr, new_sel, new_win, win_t, dtg, o_cmp, pool_sel_t, page_table, sel_idx):
    bs = qr.shape[0]
    const = lambda a: pl.BlockSpec(a.shape, lambda b, pt, sel: (0,) * a.ndim)
    n_buf = NSA_KV_HEADS * SEL_PAST
    return pl.pallas_call(
        _nsa_sample_attn_t_kernel,
        grid_spec=pltpu.PrefetchScalarGridSpec(
            num_scalar_prefetch=2,
            grid=(bs,),
            in_specs=[const(qr), const(new_sel), const(new_win),
                      pl.BlockSpec((1,) + win_t.shape[1:], lambda b, pt, sel: (b, 0, 0)),
                      const(dtg), const(o_cmp), pl.BlockSpec(memory_space=pl.ANY)],
            out_specs=pl.BlockSpec((bs, NSA_D), lambda b, pt, sel: (0, 0)),
            scratch_shapes=[pltpu.VMEM((n_buf, HEAD_DIM, PAGE_SIZE), F32), pltpu.VMEM((n_buf, HEAD_DIM, PAGE_SIZE), F32),
                            pltpu.SemaphoreType.DMA]),
        out_shape=jax.ShapeDtypeStruct((bs, NSA_D), F32),
        compiler_params=_cparams(("arbitrary",)),
        name="nsa_sample_attn",
    )(page_table, sel_idx, qr, new_sel, new_win, win_t, dtg, o_cmp, pool_sel_t)


def kernel(x_prompt, x_sample, cache_kv_cmp, cache_kv_sel, page_table, cache_kv_win, state_ssm, state_conv,
           emb_ln_g, emb_ln_b, w_in, conv_w, conv_b, dt_bias, a_log, d_skip, ssd_norm_w,
           cmp_pe, cmp_w1, cmp_b1, cmp_w2, cmp_b2, w_out, ln1_g, ln1_b,
           router_w, router_bias, exp_w_gate, exp_w_up, exp_w_down,
           sh_w_gate, sh_w_up, sh_w_down, ln2_g, ln2_b):
    bp, tp, _ = x_prompt.shape
    bs, ts, _ = x_sample.shape
    assert ts == 1 and DEPTH == 1
    n_prompt = bp * tp
    past_len = page_table.shape[1] * PAGE_SIZE
    l = 0
    w_perm = _permute_w_in(w_in[l])
    ln0_g, ln0_b = emb_ln_g[None], emb_ln_b[None]
    ssd_consts = (conv_w[l], conv_b[l][None], _pad_lanes(dt_bias[l]), _pad_lanes(a_log[l]),
                  jnp.repeat(d_skip[l], HEAD_DIM)[None], ssd_norm_w[l][None])
    cmp_consts = _compress_consts(cmp_pe[l], cmp_w1[l], cmp_b1[l], cmp_w2[l], cmp_b2[l])
    w_o = w_out[l].astype(BF16)
    w_o_ssd, w_o_nsa = w_o[:SSD_D], w_o[SSD_D:]
    ln1 = (ln1_g[l][None], ln1_b[l][None])
    kv_shape = (2, NSA_KV_HEADS, HEAD_DIM)

    hp, z, xbc, qc, qr, kvc, kvs, kvw, dtg, kvc_t, kvs_t, kvw_t = _inproj(
        x_prompt.reshape(n_prompt, D_MODEL), ln0_g, ln0_b, w_perm, _rope_tables(jnp.arange(tp)), 256,
        _rope_tables_t(jnp.arange(tp)))
    y_ssd, ssm_p, conv_p = _ssd_prompt(xbc, z, dtg, *ssd_consts, bp, tp)
    kvcmp = _compress_prompt(kvc, cmp_consts, tp)
    y_nsa = _nsa_prompt(qc, qr, dtg, kvcmp, kvs, kvw, bp, tp)
    h1p = _outproj(y_ssd, y_nsa, hp, w_o_ssd, w_o_nsa, *ln1, 512)
    n_keep = min(WINDOW, tp)
    cache_leaf = lambda a: jnp.transpose(a.reshape((bp,) + kv_shape + (a.shape[-1],)), (0, 4, 1, 2, 3))[None]
    kvc_p = cache_leaf(kvc_t)
    kvs_p = cache_leaf(kvs_t)
    kvw_p = cache_leaf(kvw_t[:, :, tp - n_keep:])

    s_hs, s_z, s_xbc, s_qc, s_qr, s_kvc, s_kvs, s_kvw, s_dtg = _inproj(
        x_sample.reshape(bs, D_MODEL), ln0_g, ln0_b, w_perm, _rope_tables(jnp.full((bs,), past_len)), bs)
    s_y_ssd, ssm_s, conv_s_t = _ssd_sample(s_xbc, s_z, s_dtg, jnp.swapaxes(state_conv[l], 0, 1), state_ssm[l],
                                           *ssd_consts)
    n_pool = cache_kv_cmp.shape[1]
    feature_major = lambda c, rows: jnp.swapaxes(c.reshape(-1, rows, 2 * KV_D), 1, 2)
    s_kvcmp = _compress_pages(feature_major(cache_kv_cmp[l], PAGE_SIZE), page_table,
                              _compress_consts_t(cmp_pe[l], cmp_w1[l], cmp_b1[l], cmp_w2[l], cmp_b2[l]))
    s_o_cmp, s_sel = _nsa_sample_cmp_t(s_qc, s_kvcmp)
    buf_win = cache_kv_win[l].reshape(bs, -1, 2 * KV_D)
    s_y_nsa = _nsa_sample_attn_t(
        s_qr, s_kvs, s_kvw, feature_major(cache_kv_win[l], buf_win.shape[1]), s_dtg, s_o_cmp,
        feature_major(cache_kv_sel[l], PAGE_SIZE), page_table, s_sel)
    h1s = _outproj(s_y_ssd, s_y_nsa, s_hs, w_o_ssd, w_o_nsa, *ln1, bs)
    win_all = jnp.concatenate([buf_win, s_kvw[:, None, :]], 1)
    n_keep_s = min(WINDOW, past_len + ts)
    kvw_s = win_all[:, win_all.shape[1] - n_keep_s:].reshape((1, bs, n_keep_s) + kv_shape)
    kvc_s = s_kvc.reshape((1, bs, ts) + kv_shape)
    kvs_s = s_kvs.reshape((1, bs, ts) + kv_shape)

    assert n_prompt % MOE_TOKENS == 0 and bs * ts <= MOE_TOKENS
    n_tok = n_prompt + bs * ts
    tail = jnp.concatenate([h1s, jnp.zeros((MOE_TOKENS - bs * ts, D_MODEL), F32)], 0)
    out_main, out_tail = _moe_ln(h1p, tail, n_tok, router_w[l], router_bias[l], exp_w_gate[l], exp_w_up[l],
                                 exp_w_down[l], sh_w_gate[l], sh_w_up[l], sh_w_down[l], ln2_g[l][None], ln2_b[l][None])
    y_prompt = out_main.reshape(bp, tp, D_MODEL)
    y_sample = out_tail[:bs * ts].reshape(bs, ts, D_MODEL)
    return (y_prompt, y_sample, kvc_p, kvs_p, kvw_p, ssm_p[None], conv_p[None],
            kvc_s, kvs_s, kvw_s, ssm_s[None], jnp.swapaxes(conv_s_t, 0, 1)[None])
```

```python
import functools
import math

import jax
import jax.numpy as jnp
import numpy as np
from jax import lax
from jax.experimental import pallas as pl
from jax.experimental.pallas import tpu as pltpu

D_MODEL = 1024
HEAD_DIM = 64
SSD_HEADS = 8
SSD_D = SSD_HEADS * HEAD_DIM
SSD_GROUPS = 2
SSD_STATE = 128
SSD_CONV = 4
SSD_CONV_CH = SSD_D + 2 * SSD_GROUPS * SSD_STATE
SSD_CHUNK = 128
NSA_HEADS = 8
NSA_KV_HEADS = 2
NSA_D = NSA_HEADS * HEAD_DIM
KV_D = NSA_KV_HEADS * HEAD_DIM
CMP_BLOCK = 64
CMP_HIDDEN = 128
SEL_BLOCK = 64
TOP_N = 16
WINDOW = 512
Q_BLOCK = 128
ROT_DIM = HEAD_DIM // 4
ROPE_THETA = 500000.0
N_EXPERTS = 64
TOP_K = 6
N_EXPERT_GROUPS = 8
EXPERTS_PER_GROUP = N_EXPERTS // N_EXPERT_GROUPS
TOPK_GROUPS = 4
D_EXPERT = 256
D_SHARED = 256
ROUTED_SCALE = 2.5
MOE_BLOCK = 1024
DEPTH = 1
DEEPNORM_ALPHA = (2.0 * DEPTH) ** 0.25
LN_EPS = 1e-5
RMS_EPS = 1e-5
NEG = -1e30
FORCED_SCORE = 1e4
PAGE_SIZE = 128

LANES = 128
SUBLANES = 8
VMEM_LIMIT_BYTES = 56 * 1024 * 1024

U_Z = 0
U_XBC = U_Z + SSD_D
U_Q = U_XBC + SSD_CONV_CH
U_KVC = U_Q + NSA_D
U_KVS = U_KVC + 2 * KV_D
U_KVW = U_KVS + 2 * KV_D
U_DTG = U_KVW + 2 * KV_D
U_TOTAL = U_DTG + LANES
GATE_COL0 = SSD_HEADS

BF16 = jnp.bfloat16
F32 = jnp.float32


def _cparams(sem):
    return pltpu.CompilerParams(dimension_semantics=sem, vmem_limit_bytes=VMEM_LIMIT_BYTES)


def _bdot(a, b):
    return jnp.dot(a.astype(BF16), b.astype(BF16), preferred_element_type=F32)


def _bdot_nt(a, b):
    return lax.dot_general(a.astype(BF16), b.astype(BF16), (((1,), (1,)), ((), ())),
                           preferred_element_type=F32)


def _hdot(a, b):
    return jnp.dot(a, b, preferred_element_type=F32, precision=lax.Precision.HIGHEST)


def _sigmoid(x):
    return 1.0 / (1.0 + jnp.exp(-x))


def _silu(x):
    return x * _sigmoid(x)


def _layer_norm(x, g, b):
    mu = jnp.mean(x, axis=-1, keepdims=True)
    xc = x - mu
    var = jnp.mean(xc * xc, axis=-1, keepdims=True)
    return xc * lax.rsqrt(var + LN_EPS) * g + b


def _rope_tile(x, cos, sa, sb):
    return x * cos + pltpu.roll(x, LANES - ROT_DIM // 2, 1) * sa + pltpu.roll(x, ROT_DIM // 2, 1) * sb


def _rope_rows(x, cos, sin):
    half = ROT_DIM // 2
    parts = []
    for hd in range(NSA_KV_HEADS):
        r0 = hd * HEAD_DIM
        x1, x2 = x[r0:r0 + half], x[r0 + half:r0 + ROT_DIM]
        parts += [x1 * cos - x2 * sin, x2 * cos + x1 * sin, x[r0 + ROT_DIM:r0 + HEAD_DIM]]
    return jnp.concatenate(parts, axis=0)


def _inproj_kernel(x_ref, g_ref, b_ref, w_ref, rope_ref, *refs, feature_major):
    if feature_major:
        wkv_t_ref, rope_t_ref = refs[:2]
        refs = refs[2:]
    h_ref, z_ref, xbc_ref, qc_ref, qr_ref, kvc_ref, kvs_ref, kvw_ref, dtg_ref = refs[:9]
    h = _layer_norm(x_ref[...], g_ref[...], b_ref[...])
    h_ref[...] = h
    hb = h.astype(BF16)
    if feature_major:
        kvc_t_ref, kvs_t_ref, kvw_t_ref = refs[9:]
        ut = lax.dot_general(wkv_t_ref[...], hb, (((1,), (1,)), ((), ())), preferred_element_type=F32)
        half = ROT_DIM // 2
        cos_t, sin_t = rope_t_ref[0:half, :], rope_t_ref[half:2 * half, :]
        kvc_t_ref[0] = ut[0:2 * KV_D]
        kvs_t_ref[0, 0:KV_D] = _rope_rows(ut[2 * KV_D:3 * KV_D], cos_t, sin_t)
        kvs_t_ref[0, KV_D:2 * KV_D] = ut[3 * KV_D:4 * KV_D]
        kvw_t_ref[0, 0:KV_D] = _rope_rows(ut[4 * KV_D:5 * KV_D], cos_t, sin_t)
        kvw_t_ref[0, KV_D:2 * KV_D] = ut[5 * KV_D:6 * KV_D]
    u = jnp.dot(hb, w_ref[...], preferred_element_type=F32)
    cos = rope_ref[:, 0:LANES]
    sa = rope_ref[:, LANES:2 * LANES]
    sb = rope_ref[:, 2 * LANES:3 * LANES]
    z_ref[...] = u[:, U_Z:U_XBC]
    xbc_ref[...] = u[:, U_XBC:U_Q]
    qc_ref[...] = u[:, U_Q:U_KVC].astype(qc_ref.dtype)
    for c in range(NSA_D // LANES):
        qr_ref[:, c * LANES:(c + 1) * LANES] = _rope_tile(
            u[:, U_Q + c * LANES:U_Q + (c + 1) * LANES], cos, sa, sb).astype(qr_ref.dtype)
    kvc_ref[...] = u[:, U_KVC:U_KVS]
    kvs_ref[:, 0:KV_D] = _rope_tile(u[:, U_KVS:U_KVS + KV_D], cos, sa, sb)
    kvs_ref[:, KV_D:2 * KV_D] = u[:, U_KVS + KV_D:U_KVW]
    kvw_ref[:, 0:KV_D] = _rope_tile(u[:, U_KVW:U_KVW + KV_D], cos, sa, sb)
    kvw_ref[:, KV_D:2 * KV_D] = u[:, U_KVW + KV_D:U_DTG]
    dtg_ref[...] = u[:, U_DTG:U_TOTAL]


def _rope_tables(pos):
    half = ROT_DIM // 2
    inv = ROPE_THETA ** (-jnp.arange(half, dtype=F32) / half)
    ang = pos.astype(F32)[:, None] * inv
    cos, sin = jnp.cos(ang), jnp.sin(ang)
    ones = jnp.ones((pos.shape[0], HEAD_DIM - ROT_DIM), F32)
    zeros = jnp.zeros((pos.shape[0], HEAD_DIM - ROT_DIM), F32)
    zh = jnp.zeros_like(sin)
    c = jnp.concatenate([cos, cos, ones], 1)
    sa = jnp.concatenate([-sin, zh, zeros], 1)
    sb = jnp.concatenate([zh, sin, zeros], 1)
    return jnp.concatenate([jnp.tile(t, (1, LANES // HEAD_DIM)) for t in (c, sa, sb)], 1)


def _permute_w_in(w):
    sizes = (SSD_D, SSD_CONV_CH, SSD_HEADS, NSA_D, KV_D, KV_D, KV_D, KV_D, KV_D, KV_D, 3 * NSA_HEADS)
    offs = np.concatenate([[0], np.cumsum(sizes)])
    seg = [w[:, offs[i]:offs[i + 1]] for i in range(len(sizes))]
    pad = jnp.zeros((w.shape[0], LANES - SSD_HEADS - 3 * NSA_HEADS), w.dtype)
    out = jnp.concatenate([seg[0], seg[1], seg[3], seg[4], seg[5], seg[6], seg[7], seg[8], seg[9],
                           seg[2], seg[10], pad], 1)
    return out.astype(BF16)


def _rope_tables_t(pos):
    half = ROT_DIM // 2
    inv = ROPE_THETA ** (-jnp.arange(half, dtype=F32) / half)
    ang = inv[:, None] * pos.astype(F32)[None, :]
    return jnp.concatenate([jnp.cos(ang), jnp.sin(ang)], 0)


def _inproj(x, ln_g, ln_b, w_perm, rope_tab, tm, rope_tab_t=None):
    n = x.shape[0]
    nt = n // tm
    t = rope_tab.shape[0]
    n_rope_blocks = t // tm
    feature_major = rope_tab_t is not None
    row = lambda w: pl.BlockSpec((tm, w), lambda i: (i, 0))
    const = lambda a: pl.BlockSpec(a.shape, lambda i: (0,) * a.ndim)
    widths = (D_MODEL, SSD_D, SSD_CONV_CH, NSA_D, NSA_D, 2 * KV_D, 2 * KV_D, 2 * KV_D, LANES)
    in_specs = [row(D_MODEL), const(ln_g), const(ln_b), const(w_perm),
                pl.BlockSpec((tm, 3 * LANES), lambda i: (i % n_rope_blocks, 0))]
    out_specs = [row(w) for w in widths]
    dtypes = [BF16 if (feature_major and k in (3, 4)) else F32 for k in range(len(widths))]
    out_shape = [jax.ShapeDtypeStruct((n, w), d) for w, d in zip(widths, dtypes)]
    args = [x, ln_g, ln_b, w_perm, rope_tab]
    if feature_major:
        wkv_t = w_perm[:, U_KVC:U_DTG].T
        in_specs += [const(wkv_t), pl.BlockSpec((rope_tab_t.shape[0], tm), lambda i: (0, i % n_rope_blocks))]
        args += [wkv_t, rope_tab_t]
        out_specs += [pl.BlockSpec((1, 2 * KV_D, tm), lambda i: (i // n_rope_blocks, 0, i % n_rope_blocks))] * 3
        out_shape += [jax.ShapeDtypeStruct((n // t, 2 * KV_D, t), F32)] * 3
    return pl.pallas_call(
        functools.partial(_inproj_kernel, feature_major=feature_major),
        grid=(nt,),
        in_specs=in_specs,
        out_specs=out_specs,
        out_shape=out_shape,
        compiler_params=_cparams(("parallel",)),
        name="inproj",
    )(*args)


def _softplus(x):
    return jnp.maximum(x, 0.0) + jnp.log1p(jnp.exp(-jnp.abs(x)))


def _gated_group_norm(y, z, norm_w):
    y = y * _silu(z)
    gw = SSD_D // SSD_GROUPS
    parts = []
    for g in range(SSD_GROUPS):
        yg = y[:, g * gw:(g + 1) * gw]
        ms = jnp.mean(yg * yg, axis=-1, keepdims=True)
        parts.append(yg * lax.rsqrt(ms + RMS_EPS))
    return jnp.concatenate(parts, axis=1) * norm_w


SSD_CHUNKS_PER_STEP = 2


def _ssd_prompt_kernel(xbc_ref, z_ref, dtg_ref, convw_ref, convb_ref, dtb_ref, alog_ref, dskip_ref, normw_ref,
                       y_ref, state_ref, conv_ref, ext_ref, s_ref):
    c = pl.program_id(1)
    nc = pl.num_programs(1)
    L = SSD_CHUNK

    @pl.when(c == 0)
    def _():
        ext_ref[0:SUBLANES, :] = jnp.zeros((SUBLANES, SSD_CONV_CH), F32)
        s_ref[...] = jnp.zeros_like(s_ref)

    for sub in range(SSD_CHUNKS_PER_STEP):
        rows = pl.ds(sub * L, L)
        xin = _ssd_chunk(xbc_ref.at[rows], z_ref.at[rows], dtg_ref.at[rows], convw_ref, convb_ref, dtb_ref, alog_ref,
                         dskip_ref, normw_ref, y_ref.at[rows], ext_ref, s_ref)

    @pl.when(c == nc - 1)
    def _():
        state_ref[0] = s_ref[...]
        conv_ref[0] = xin[L - (SSD_CONV - 1):L, :]


def _ssd_chunk(xbc_ref, z_ref, dtg_ref, convw_ref, convb_ref, dtb_ref, alog_ref, dskip_ref, normw_ref, y_ref, ext_ref,
               s_ref):
    L = SSD_CHUNK
    halo = SUBLANES
    xin = xbc_ref[...]
    ext_ref[halo:halo + L, :] = xin
    xc = convw_ref[SSD_CONV - 1:SSD_CONV, :] * xin
    for k in range(SSD_CONV - 1):
        off = halo - (SSD_CONV - 1) + k
        xc = xc + convw_ref[k:k + 1, :] * ext_ref[off:off + L, :]
    ext_ref[0:halo, :] = ext_ref[L:L + halo, :]
    xc = _silu(xc + convb_ref[...])
    xs = xc[:, 0:SSD_D]
    ns = SSD_GROUPS * SSD_STATE
    bm = xc[:, SSD_D:SSD_D + ns]
    cm = xc[:, SSD_D + ns:SSD_D + 2 * ns]

    dt = _softplus(dtg_ref[...] + dtb_ref[...])
    da = dt * (-jnp.exp(alog_ref[...]))
    row = lax.broadcasted_iota(jnp.int32, (L, L), 0)
    col = lax.broadcasted_iota(jnp.int32, (L, L), 1)
    tril = row >= col
    acum = _hdot(tril.astype(F32), da)
    acum_t = acum.T
    eacum = jnp.exp(acum)
    alast = acum[L - 1:L, :]
    edecay = jnp.exp(alast - acum)
    elast = jnp.exp(alast)

    dt_full = jnp.concatenate([jnp.broadcast_to(dt[:, h:h + 1], (L, HEAD_DIM)) for h in range(SSD_HEADS)], 1)
    dec_full = jnp.concatenate([jnp.broadcast_to(edecay[:, h:h + 1], (L, HEAD_DIM)) for h in range(SSD_HEADS)], 1)
    xdt = xs * dt_full
    xdec_t = (xdt * dec_full).T

    hpg = SSD_HEADS // SSD_GROUPS
    y_parts = []
    for h in range(SSD_HEADS):
        g = h // hpg
        b_g = bm[:, g * SSD_STATE:(g + 1) * SSD_STATE]
        c_g = cm[:, g * SSD_STATE:(g + 1) * SSD_STATE]
        if h % hpg == 0:
            cb = _bdot_nt(c_g, b_g)
        seg = acum[:, h:h + 1] - acum_t[h:h + 1, :]
        lmat = jnp.where(tril, jnp.exp(jnp.where(tril, seg, 0.0)), 0.0)
        xdt_h = xdt[:, h * HEAD_DIM:(h + 1) * HEAD_DIM]
        y_h = _bdot(cb * lmat, xdt_h)
        s_prev = s_ref[h]
        y_h = y_h + _bdot_nt(c_g, s_prev) * eacum[:, h:h + 1]
        y_h = y_h + dskip_ref[:, h * HEAD_DIM:(h + 1) * HEAD_DIM] * xs[:, h * HEAD_DIM:(h + 1) * HEAD_DIM]
        y_parts.append(y_h)
        s_ref[h] = elast[:, h:h + 1] * s_prev + _bdot(xdec_t[h * HEAD_DIM:(h + 1) * HEAD_DIM, :], b_g)
    y = jnp.concatenate(y_parts, axis=1)
    y_ref[...] = _gated_group_norm(y, z_ref[...], normw_ref[...]).astype(y_ref.dtype)
    return xin


def _ssd_prompt(xbc, z, dtg, conv_w, conv_b, dt_bias_pad, a_log_pad, d_skip_full, norm_w, bn, t):
    step_rows = SSD_CHUNK * SSD_CHUNKS_PER_STEP
    nc = t // step_rows
    row = lambda w: pl.BlockSpec((step_rows, w), lambda b, c: (b * nc + c, 0))
    const = lambda a: pl.BlockSpec(a.shape, lambda b, c: (0,) * a.ndim)
    return pl.pallas_call(
        _ssd_prompt_kernel,
        grid=(bn, nc),
        in_specs=[row(SSD_CONV_CH), row(SSD_D), row(LANES), const(conv_w), const(conv_b), const(dt_bias_pad),
                  const(a_log_pad), const(d_skip_full), const(norm_w)],
        out_specs=[row(SSD_D),
                   pl.BlockSpec((1, SSD_HEADS, HEAD_DIM, SSD_STATE), lambda b, c: (b, 0, 0, 0)),
                   pl.BlockSpec((1, SSD_CONV - 1, SSD_CONV_CH), lambda b, c: (b, 0, 0))],
        out_shape=[jax.ShapeDtypeStruct((bn * t, SSD_D), BF16),
                   jax.ShapeDtypeStruct((bn, SSD_HEADS, HEAD_DIM, SSD_STATE), F32),
                   jax.ShapeDtypeStruct((bn, SSD_CONV - 1, SSD_CONV_CH), F32)],
        scratch_shapes=[pltpu.VMEM((SSD_CHUNK + 2 * SUBLANES, SSD_CONV_CH), F32),
                        pltpu.VMEM((SSD_HEADS, HEAD_DIM, SSD_STATE), F32)],
        compiler_params=_cparams(("parallel", "arbitrary")),
        name="ssd_prompt",
    )(xbc, z, dtg, conv_w, conv_b, dt_bias_pad, a_log_pad, d_skip_full, norm_w)


def _pad_lanes(v, fill=0.0):
    return jnp.concatenate([v.astype(F32), jnp.full((LANES - v.shape[0],), fill, F32)])[None]


def _compress_rows(k_ref, v_ref, pe_ref, w1k_ref, w1v_ref, b1_ref, w2k_ref, w2v_ref, b2_ref, nb):
    acck = jnp.zeros((nb, 2 * CMP_HIDDEN), F32)
    accv = jnp.zeros((nb, 2 * CMP_HIDDEN), F32)
    for l in range(CMP_BLOCK):
        xk = k_ref[pl.ds(l, nb, stride=CMP_BLOCK), :] + pe_ref[l:l + 1, 0:KV_D]
        xv = v_ref[pl.ds(l, nb, stride=CMP_BLOCK), :] + pe_ref[l:l + 1, KV_D:2 * KV_D]
        acck = acck + jnp.dot(xk.astype(BF16), w1k_ref[l], preferred_element_type=F32)
        accv = accv + jnp.dot(xv.astype(BF16), w1v_ref[l], preferred_element_type=F32)
    hk = _silu(acck + b1_ref[:, 0:2 * CMP_HIDDEN])
    hv = _silu(accv + b1_ref[:, 2 * CMP_HIDDEN:4 * CMP_HIDDEN])
    ok = jnp.dot(hk.astype(BF16), w2k_ref[...], preferred_element_type=F32) + b2_ref[:, 0:KV_D]
    ov = jnp.dot(hv.astype(BF16), w2v_ref[...], preferred_element_type=F32) + b2_ref[:, KV_D:2 * KV_D]
    return jnp.concatenate([ok, ov], axis=1)


def _compress_kernel(k_ref, v_ref, pe_ref, w1k_ref, w1v_ref, b1_ref, w2k_ref, w2v_ref, b2_ref, o_ref, *, nb):
    o_ref[...] = _compress_rows(k_ref, v_ref, pe_ref, w1k_ref, w1v_ref, b1_ref, w2k_ref, w2v_ref, b2_ref, nb)


def _block_diag2(w):
    z = jnp.zeros_like(w)
    return jnp.concatenate([jnp.concatenate([w, z], -1), jnp.concatenate([z, w], -1)], -2)


def _compress_consts(cmp_pe, cmp_w1, cmp_b1, cmp_w2, cmp_b2):
    pe = jnp.concatenate([cmp_pe[0], cmp_pe[0], cmp_pe[1], cmp_pe[1]], -1)
    w1k = _block_diag2(cmp_w1[0]).astype(BF16)
    w1v = _block_diag2(cmp_w1[1]).astype(BF16)
    b1 = jnp.concatenate([cmp_b1[0], cmp_b1[0], cmp_b1[1], cmp_b1[1]])[None]
    w2k = _block_diag2(cmp_w2[0]).astype(BF16)
    w2v = _block_diag2(cmp_w2[1]).astype(BF16)
    b2 = jnp.concatenate([cmp_b2[0], cmp_b2[0], cmp_b2[1], cmp_b2[1]])[None]
    return pe, w1k, w1v, b1, w2k, w2v, b2


def _compress_prompt(kvc, consts, rows_per_step):
    n = kvc.shape[0]
    nb = rows_per_step // CMP_BLOCK
    const = lambda a: pl.BlockSpec(a.shape, lambda i: (0,) * a.ndim)
    return pl.pallas_call(
        functools.partial(_compress_kernel, nb=nb),
        grid=(n // rows_per_step,),
        in_specs=[pl.BlockSpec((rows_per_step, KV_D), lambda i: (i, 0)),
                  pl.BlockSpec((rows_per_step, KV_D), lambda i: (i, 1))] + [const(a) for a in consts],
        out_specs=pl.BlockSpec((nb, 2 * KV_D), lambda i: (i, 0)),
        out_shape=jax.ShapeDtypeStruct((n // CMP_BLOCK, 2 * KV_D), F32),
        compiler_params=_cparams(("parallel",)),
        name="compress_prompt",
    )(kvc, kvc, *consts)


SEL_KEY_TILE = 512
WIN_KEYS = WINDOW + Q_BLOCK


def _dup_head(x, hk):
    sw = pltpu.roll(x, HEAD_DIM, 1)
    low = lax.broadcasted_iota(jnp.int32, x.shape, 1) < HEAD_DIM
    return jnp.where(low, x, sw) if hk == 0 else jnp.where(low, sw, x)


def _masked_softmax(s, mask):
    sm = jnp.where(mask, s, NEG)
    ex = jnp.where(mask, jnp.exp(sm - jnp.max(sm, axis=-1, keepdims=True)), 0.0)
    den = jnp.sum(ex, axis=-1, keepdims=True)
    return ex / jnp.where(den > 0.0, den, 1.0)


def _select_blocks_t(imp, cur, n_top):
    j = lax.broadcasted_iota(jnp.int32, imp.shape, 0)
    future = j > cur
    forced = (j == 0) | (j == cur) | (j == cur - 1)
    score = jnp.where(future, NEG, jnp.where(forced, FORCED_SCORE, imp))
    return ((_rank_rows(score) < n_top) & (score > 0.5 * NEG)).astype(F32)


def _nsa_prompt_kernel(qc_ref, qr_ref, dtg_ref, cmp_ref, kvs_ref, kvw_ref, o_ref,
                       cmp_d, kvs_d, kvw_d, bias_ref, qrs_ref, m_ref, l_ref, acc_ref, *, t):
    qb = pl.program_id(1)
    nbk = t // SEL_BLOCK
    tq = Q_BLOCK
    tk = SEL_KEY_TILE
    hpg = NSA_HEADS // NSA_KV_HEADS
    scale = HEAD_DIM ** -0.5

    @pl.when(qb == 0)
    def _():
        cmp_d[...] = jnp.zeros_like(cmp_d)
        for src, dst, n in ((cmp_ref, cmp_d, nbk), (kvs_ref, kvs_d, t), (kvw_ref, kvw_d, t)):
            x = src[...]
            for hk in range(NSA_KV_HEADS):
                dst[hk, 0:n, 0:KV_D] = _dup_head(x[:, 0:KV_D], hk).astype(BF16)
                dst[hk, 0:n, KV_D:2 * KV_D] = _dup_head(x[:, KV_D:2 * KV_D], hk).astype(BF16)

    t0 = qb * tq
    rows = t0 + lax.broadcasted_iota(jnp.int32, (tq, 1), 0)
    lane = lax.broadcasted_iota(jnp.int32, (tq, LANES), 1)
    half_mask = (lane < HEAD_DIM, lane >= HEAD_DIM)
    sig = _sigmoid(dtg_ref[...])
    vis = (lane + 1) * CMP_BLOCK - 1 <= rows
    cur_l = (t0 + lax.broadcasted_iota(jnp.int32, (1, tq), 1)) // SEL_BLOCK
    expand = (lax.broadcasted_iota(jnp.int32, (LANES, t), 1) // SEL_BLOCK
              == lax.broadcasted_iota(jnp.int32, (LANES, t), 0)).astype(BF16)
    win_start = pl.multiple_of(jnp.maximum(t0 - WINDOW, 0), tq)
    wpos = win_start + lax.broadcasted_iota(jnp.int32, (tq, WIN_KEYS), 1)
    win_bias = jnp.where((wpos <= rows) & (wpos >= rows - WINDOW), 0.0, NEG)
    n_kt = (t0 + tq + tk - 1) // tk

    def stack_heads(ref, hk):
        parts = []
        for hh in range(hpg):
            head = hk * hpg + hh
            p, e = head // 2, head % 2
            parts.append(jnp.where(half_mask[e], ref[:, p * LANES:(p + 1) * LANES] * scale, 0.0))
        return jnp.concatenate(parts, axis=0).astype(BF16)

    o_cmp_g = []
    for hk in range(NSA_KV_HEADS):
        qcs = stack_heads(qc_ref, hk)
        s = lax.dot_general(qcs, cmp_d[hk, :, 0:KV_D], (((1,), (1,)), ((), ())), preferred_element_type=F32)
        pc = _masked_softmax(s.reshape(hpg, tq, LANES), vis[None])
        imp = jnp.sum(pc, axis=0)
        o_cmp_g.append(jnp.dot(pc.reshape(hpg * tq, LANES).astype(BF16), cmp_d[hk, :, KV_D:2 * KV_D],
                               preferred_element_type=F32).reshape(hpg, tq, LANES))

        sel_t = _select_blocks_t(imp.T[0:nbk, :], cur_l, TOP_N)
        sel = jnp.concatenate([sel_t, jnp.zeros((LANES - nbk, tq), F32)], axis=0).T
        selk = jnp.dot(sel.astype(BF16), expand, preferred_element_type=F32)
        for kt in range(t // tk):
            @pl.when(kt < n_kt)
            def _(kt=kt, hk=hk, selk=selk):
                kpos = kt * tk + lax.broadcasted_iota(jnp.int32, (tq, tk), 1)
                bias_ref[hk, kt] = jnp.where((selk[:, kt * tk:(kt + 1) * tk] > 0.5) & (kpos <= rows), 0.0, NEG)

        qrs_ref[hk] = stack_heads(qr_ref, hk)

    m_ref[...] = jnp.full(m_ref.shape, NEG, F32)
    l_ref[...] = jnp.zeros(l_ref.shape, F32)
    acc_ref[...] = jnp.zeros(acc_ref.shape, F32)

    def sel_step(kt, carry):
        k0 = pl.multiple_of(kt * tk, tk)
        for hk in range(NSA_KV_HEADS):
            kblk = kvs_d[hk, pl.ds(k0, tk), 0:KV_D]
            vblk = kvs_d[hk, pl.ds(k0, tk), KV_D:2 * KV_D]
            s = lax.dot_general(qrs_ref[hk], kblk, (((1,), (1,)), ((), ())), preferred_element_type=F32)
            s = s.reshape(hpg, tq, tk) + bias_ref[hk, kt][None]
            m_old = m_ref[hk]
            m_new = jnp.maximum(m_old, jnp.max(s, axis=-1, keepdims=True))
            alpha = jnp.exp(m_old - m_new)
            pe = jnp.exp(s - jnp.concatenate([m_new] * (tk // LANES), axis=-1))
            l_ref[hk] = alpha * l_ref[hk] + jnp.sum(pe, axis=-1, keepdims=True)
            pv = jnp.dot(pe.reshape(hpg * tq, tk).astype(BF16), vblk, preferred_element_type=F32)
            acc_ref[hk] = alpha * acc_ref[hk] + pv.reshape(hpg, tq, LANES)
            m_ref[hk] = m_new
        return carry

    lax.fori_loop(0, n_kt, sel_step, 0)

    for hk in range(NSA_KV_HEADS):
        o_cmp = o_cmp_g[hk]
        o_slc = acc_ref[hk] / l_ref[hk]
        kw = kvw_d[hk, pl.ds(win_start, WIN_KEYS), 0:KV_D]
        vw = kvw_d[hk, pl.ds(win_start, WIN_KEYS), KV_D:2 * KV_D]
        sw = lax.dot_general(qrs_ref[hk], kw, (((1,), (1,)), ((), ())), preferred_element_type=F32)
        sw = sw.reshape(hpg, tq, WIN_KEYS) + win_bias[None]
        pw = jnp.exp(sw - jnp.max(sw, axis=-1, keepdims=True))
        den = jnp.sum(pw, axis=-1, keepdims=True)
        o_win = jnp.dot(pw.reshape(hpg * tq, WIN_KEYS).astype(BF16), vw,
                        preferred_element_type=F32).reshape(hpg, tq, LANES) / den

        for hh in range(hpg):
            head = hk * hpg + hh
            p, e = head // 2, head % 2
            c0 = GATE_COL0 + head * 3
            mix = (sig[:, c0:c0 + 1] * o_cmp[hh] + sig[:, c0 + 1:c0 + 2] * o_slc[hh]
                   + sig[:, c0 + 2:c0 + 3] * o_win[hh])
            if e == 0:
                mix_even = mix
            else:
                o_ref[:, p * LANES:(p + 1) * LANES] = jnp.where(half_mask[0], mix_even, mix).astype(o_ref.dtype)


def _nsa_prompt(qc, qr, dtg, kvcmp, kvs, kvw, bn, t):
    nq = t // Q_BLOCK
    nbk = t // SEL_BLOCK
    hpg = NSA_HEADS // NSA_KV_HEADS
    assert nbk >= TOP_N and t >= WIN_KEYS and t % SEL_KEY_TILE == 0
    qrow = lambda w: pl.BlockSpec((Q_BLOCK, w), lambda b, i: (b * nq + i, 0))
    seq = lambda r: pl.BlockSpec((r, 2 * KV_D), lambda b, i: (b, 0))
    return pl.pallas_call(
        functools.partial(_nsa_prompt_kernel, t=t),
        grid=(bn, nq),
        in_specs=[qrow(NSA_D), qrow(NSA_D), qrow(LANES), seq(nbk), seq(t), seq(t)],
        out_specs=qrow(NSA_D),
        out_shape=jax.ShapeDtypeStruct((bn * t, NSA_D), BF16),
        scratch_shapes=[pltpu.VMEM((NSA_KV_HEADS, LANES, 2 * KV_D), BF16),
                        pltpu.VMEM((NSA_KV_HEADS, t, 2 * KV_D), BF16),
                        pltpu.VMEM((NSA_KV_HEADS, t, 2 * KV_D), BF16),
                        pltpu.VMEM((NSA_KV_HEADS, t // SEL_KEY_TILE, Q_BLOCK, SEL_KEY_TILE), F32),
                        pltpu.VMEM((NSA_KV_HEADS, hpg * Q_BLOCK, LANES), BF16),
                        pltpu.VMEM((NSA_KV_HEADS, hpg, Q_BLOCK, LANES), F32),
                        pltpu.VMEM((NSA_KV_HEADS, hpg, Q_BLOCK, LANES), F32),
                        pltpu.VMEM((NSA_KV_HEADS, hpg, Q_BLOCK, LANES), F32)],
        compiler_params=_cparams(("parallel", "arbitrary")),
        name="nsa_prompt",
    )(qc, qr, dtg, kvcmp, kvs, kvw)


def _outproj_kernel(ys_ref, yn_ref, h_ref, ws_ref, wn_ref, g_ref, b_ref, o_ref):
    mix = jnp.dot(ys_ref[...].astype(BF16), ws_ref[...], preferred_element_type=F32)
    mix = mix + jnp.dot(yn_ref[...].astype(BF16), wn_ref[...], preferred_element_type=F32)
    o_ref[...] = _layer_norm(DEEPNORM_ALPHA * h_ref[...] + mix, g_ref[...], b_ref[...])


def _outproj(y_ssd, y_nsa, h, w_ssd, w_nsa, ln_g, ln_b, tm):
    n = h.shape[0]
    row = lambda w: pl.BlockSpec((tm, w), lambda i: (i, 0))
    const = lambda a: pl.BlockSpec(a.shape, lambda i: (0,) * a.ndim)
    return pl.pallas_call(
        _outproj_kernel,
        grid=(n // tm,),
        in_specs=[row(SSD_D), row(NSA_D), row(D_MODEL), const(w_ssd), const(w_nsa), const(ln_g), const(ln_b)],
        out_specs=row(D_MODEL),
        out_shape=jax.ShapeDtypeStruct((n, D_MODEL), F32),
        compiler_params=_cparams(("parallel",)),
        name="outproj",
    )(y_ssd, y_nsa, h, w_ssd, w_nsa, ln_g, ln_b)


MOE_TOKENS = 256
ROUTE_ROWS = 8


def _token_tile_specs(n_main_tiles):
    main = pl.BlockSpec((MOE_TOKENS, D_MODEL), lambda i, *_: (jnp.minimum(i, n_main_tiles - 1), 0))
    tail = pl.BlockSpec((MOE_TOKENS, D_MODEL), lambda i, *_: (0, 0))
    return main, tail


def _token_tile(i, n_main_tiles, main_ref, tail_ref):
    return jnp.where(i < n_main_tiles, main_ref[...], tail_ref[...])


def _rank_rows(x):
    n = x.shape[0]
    idx = lax.broadcasted_iota(jnp.int32, x.shape, 0)
    rank = jnp.zeros(x.shape, F32)
    for r in range(n):
        row = x[r:r + 1, :]
        rank = rank + ((row > x) | ((row == x) & (idx > r))).astype(F32)
    return rank


def _route_kernel(h_ref, ht_ref, rw_ref, rb_ref, slot_ref, tokinfo_ref, meta_ref, cnt_ref, carry_ref, carry_row_ref, *,
                  n_valid, n_main):
    i = pl.program_id(0)
    tm = MOE_TOKENS

    @pl.when(i == 0)
    def _():
        carry_ref[...] = jnp.zeros_like(carry_ref)
        carry_row_ref[...] = jnp.zeros_like(carry_row_ref)

    logits = lax.dot_general(rw_ref[...], _token_tile(i, n_main, h_ref, ht_ref).astype(BF16), (((1,), (1,)), ((), ())),
                             preferred_element_type=F32)
    scores = _sigmoid(logits)
    biased = scores + rb_ref[:, 0:1]
    b3 = biased.reshape(N_EXPERT_GROUPS, EXPERTS_PER_GROUP, tm)
    sidx = lax.broadcasted_iota(jnp.int32, b3.shape, 1)
    m1 = jnp.max(b3, axis=1, keepdims=True)
    first = jnp.min(jnp.where(b3 == m1, sidx, EXPERTS_PER_GROUP), axis=1, keepdims=True)
    m2 = jnp.max(jnp.where(sidx == first, -jnp.inf, b3), axis=1, keepdims=True)
    grp_score = (m1 + m2).reshape(N_EXPERT_GROUPS, tm)
    grp_keep = _rank_rows(grp_score) < TOPK_GROUPS
    masked = jnp.where(grp_keep.reshape(N_EXPERT_GROUPS, 1, tm), b3, NEG).reshape(N_EXPERTS, tm)
    rank = _rank_rows(masked)
    tok = i * tm + lax.broadcasted_iota(jnp.int32, (1, tm), 1)
    valid = tok < n_valid
    sel = (rank < TOP_K) & valid
    self32 = sel.astype(F32)
    wsel = self32 * scores
    wsum = jnp.sum(wsel, axis=0, keepdims=True)
    w = wsel / jnp.where(wsum > 0.0, wsum, 1.0) * ROUTED_SCALE

    selb = sel.astype(BF16)
    tri = lambda n, strict_upper: (
        (lax.broadcasted_iota(jnp.int32, (n, n), 0) < lax.broadcasted_iota(jnp.int32, (n, n), 1))
        if strict_upper else
        (lax.broadcasted_iota(jnp.int32, (n, n), 0) > lax.broadcasted_iota(jnp.int32, (n, n), 1))).astype(BF16)
    pad8 = lambda c: jnp.floor((c + (SUBLANES - 1.0)) * (1.0 / SUBLANES)) * SUBLANES
    pos_tile = jnp.dot(selb, tri(tm, True), preferred_element_type=F32)
    cnt_col = pad8(jnp.sum(self32, axis=1, keepdims=True))
    first_col = jnp.dot(tri(N_EXPERTS, False), jnp.broadcast_to(cnt_col, (N_EXPERTS, LANES)).astype(BF16),
                        preferred_element_type=F32)[:, 0:1]
    slot = first_col + pos_tile

    sel_pad = jnp.concatenate([selb, jnp.zeros((LANES - N_EXPERTS, tm), BF16)], axis=0)
    cnt_row = pad8(lax.dot_general(jnp.ones((SUBLANES, tm), BF16), sel_pad, (((1,), (1,)), ((), ())),
                                   preferred_element_type=F32))
    first_row = jnp.dot(cnt_row.astype(BF16), tri(LANES, True), preferred_element_type=F32)
    prev_row = carry_row_ref[...]
    meta = jnp.concatenate([cnt_row[0:1], first_row[0:1], prev_row[0:1], jnp.zeros((SUBLANES - 3, LANES), F32)], 0)
    meta_ref[0] = meta.astype(jnp.int32)
    carry_row_ref[...] = prev_row + cnt_row
    carry_ref[...] = carry_ref[...] + cnt_col

    slot_rows, w_rows = [], []
    for k in range(TOP_K):
        hit = (rank == k) & sel
        slot_rows.append(jnp.sum(jnp.where(hit, slot, 0.0), axis=0, keepdims=True))
        w_rows.append(jnp.sum(jnp.where(hit, w, 0.0), axis=0, keepdims=True))
    slot_rows = [jnp.where(valid, r, -1.0) for r in slot_rows]
    pad2 = jnp.zeros((ROUTE_ROWS - TOP_K, tm), F32)
    slot_ref[...] = jnp.concatenate(slot_rows + [pad2 - 1.0], 0).astype(jnp.int32)
    info = jnp.concatenate(w_rows + [pad2] + slot_rows + [jnp.zeros((LANES - ROUTE_ROWS - TOP_K, tm), F32)], 0)
    tokinfo_ref[...] = info.T

    @pl.when(i == pl.num_programs(0) - 1)
    def _():
        cnt_ref[...] = jnp.broadcast_to(carry_ref[:, 0:1], cnt_ref.shape)


def _route(h_main, h_tail, router_wt, router_bias_col, n_valid):
    tm = MOE_TOKENS
    n_main = h_main.shape[0] // tm
    n = h_main.shape[0] + tm
    const = lambda a: pl.BlockSpec(a.shape, lambda i: (0,) * a.ndim)
    return pl.pallas_call(
        functools.partial(_route_kernel, n_valid=n_valid, n_main=n_main),
        grid=(n // tm,),
        in_specs=[*_token_tile_specs(n_main), const(router_wt), const(router_bias_col)],
        out_specs=[pl.BlockSpec((ROUTE_ROWS, tm), lambda i: (0, i)),
                   pl.BlockSpec((tm, LANES), lambda i: (i, 0)),
                   pl.BlockSpec((1, SUBLANES, LANES), lambda i: (i, 0, 0)),
                   pl.BlockSpec((N_EXPERTS, LANES), lambda i: (0, 0))],
        out_shape=[jax.ShapeDtypeStruct((ROUTE_ROWS, n), jnp.int32),
                   jax.ShapeDtypeStruct((n, LANES), F32),
                   jax.ShapeDtypeStruct((n // tm, SUBLANES, LANES), jnp.int32),
                   jax.ShapeDtypeStruct((N_EXPERTS, LANES), F32)],
        scratch_shapes=[pltpu.VMEM((N_EXPERTS, LANES), F32), pltpu.VMEM((SUBLANES, LANES), F32)],
        compiler_params=_cparams(("arbitrary",)),
        name="moe_route",
    )(h_main, h_tail, router_wt, router_bias_col)


PACKED_D = D_MODEL // 2
U32 = jnp.uint32


def _pack_bf16_pairs(x):
    hi = lax.bitcast_convert_type(x[:, 0:PACKED_D], U32) & jnp.uint32(0xFFFF0000)
    lo = lax.shift_right_logical(lax.bitcast_convert_type(x[:, PACKED_D:D_MODEL], U32), jnp.uint32(16))
    return hi | lo


def _unpack_bf16_pairs(w):
    hi = lax.bitcast_convert_type(w & jnp.uint32(0xFFFF0000), F32)
    lo = lax.bitcast_convert_type(lax.shift_left(w, jnp.uint32(16)), F32)
    return jnp.concatenate([hi, lo], axis=1).astype(BF16)


def _round_bf16(x):
    return x.astype(BF16).astype(F32)


TILE_SLOTS = MOE_TOKENS * TOP_K + N_EXPERTS * SUBLANES
RUN_CHUNKS = tuple(1 << b for b in range(int(math.log2(MOE_TOKENS)), int(math.log2(SUBLANES)) - 1, -1))


def _run_copy(src_ref, src_row, dst_ref, dst_row, rows, sem):
    return pltpu.make_async_copy(src_ref.at[pl.ds(pl.multiple_of(src_row, SUBLANES), rows)],
                                 dst_ref.at[pl.ds(pl.multiple_of(dst_row, SUBLANES), rows)], sem)


def _start_run(src_ref, src_row, dst_ref, dst_row, n, sem, started):
    off = jnp.int32(0)
    out = []
    for c, rows in enumerate(RUN_CHUNKS):
        take = (n & rows) != 0

        @pl.when(take)
        def _(off=off, rows=rows):
            _run_copy(src_ref, src_row + off, dst_ref, dst_row + off, rows, sem).start()

        inc = take.astype(jnp.int32)
        off = off + inc * rows
        out.append(started[c] + inc)
    return tuple(out)


def _wait_runs(src_ref, dst_ref, sem, started):
    for c, rows in enumerate(RUN_CHUNKS):
        def wait_one(j, carry, rows=rows):
            _run_copy(src_ref, 0, dst_ref, 0, rows, sem).wait()
            return carry

        lax.fori_loop(0, started[c], wait_one, 0)


def _dispatch_kernel(start_ref, cnt_ref, meta_ref, slot_ref, x_ref, xt_ref, xs_ref, sorted_ref, zero_ref, sem, zsem, *,
                     cap, n_main):
    i = pl.program_id(0)
    tm = MOE_TOKENS

    @pl.when(i == 0)
    def _():
        zero_ref[...] = jnp.zeros_like(zero_ref)

        def fill_expert(e, started):
            lo = start_ref[e] + cnt_ref[e]
            hi = jnp.where(e == N_EXPERTS - 1, cap, start_ref[jnp.minimum(e + 1, N_EXPERTS - 1)])
            n_full = (hi - lo) // tm

            def fill_full(j, st):
                return _start_run(zero_ref, 0, xs_ref, lo + j * tm, jnp.int32(tm), zsem, st)

            started = lax.fori_loop(0, n_full, fill_full, started)
            return _start_run(zero_ref, 0, xs_ref, lo + n_full * tm, (hi - lo) - n_full * tm, zsem, started)

        filled = lax.fori_loop(0, N_EXPERTS, fill_expert, tuple(jnp.int32(0) for _ in RUN_CHUNKS))
        _wait_runs(zero_ref, xs_ref, zsem, filled)

    srow = lax.broadcasted_iota(jnp.int32, (TILE_SLOTS, tm), 0)
    onehot = srow == slot_ref[0:1, :]
    for k in range(1, TOP_K):
        onehot = onehot | (srow == slot_ref[k:k + 1, :])
    sorted_ref[...] = _pack_bf16_pairs(jnp.dot(onehot.astype(BF16), _token_tile(i, n_main, x_ref, xt_ref).astype(BF16),
                                               preferred_element_type=F32))

    def copy_expert(e, started):
        n = meta_ref[0, 0, e]
        return _start_run(sorted_ref, meta_ref[0, 1, e], xs_ref, start_ref[e] + meta_ref[0, 2, e], n, sem, started)

    started = lax.fori_loop(0, N_EXPERTS, copy_expert, tuple(jnp.int32(0) for _ in RUN_CHUNKS))
    _wait_runs(sorted_ref, xs_ref, sem, started)


def _dispatch(h_main, h_tail, slot_t, meta, seg_start, counts, cap):
    tm = MOE_TOKENS
    n_main = h_main.shape[0] // tm
    return pl.pallas_call(
        functools.partial(_dispatch_kernel, cap=cap, n_main=n_main),
        grid_spec=pltpu.PrefetchScalarGridSpec(
            num_scalar_prefetch=2,
            grid=(n_main + 1,),
            in_specs=[pl.BlockSpec((1, SUBLANES, LANES), lambda i, *_: (i, 0, 0), memory_space=pltpu.SMEM),
                      pl.BlockSpec((ROUTE_ROWS, tm), lambda i, *_: (0, i)),
                      *_token_tile_specs(n_main)],
            out_specs=pl.BlockSpec(memory_space=pl.ANY),
            scratch_shapes=[pltpu.VMEM((TILE_SLOTS, PACKED_D), U32), pltpu.VMEM((tm, PACKED_D), U32),
                            pltpu.SemaphoreType.DMA, pltpu.SemaphoreType.DMA]),
        out_shape=jax.ShapeDtypeStruct((cap, PACKED_D), U32),
        compiler_params=_cparams(("arbitrary",)),
        name="moe_dispatch",
    )(seg_start, counts, meta, slot_t, h_main, h_tail)


def _swiglu(x, wg, wu, wd):
    xb = x.astype(BF16)
    g = jnp.dot(xb, wg.astype(BF16), preferred_element_type=F32)
    u = jnp.dot(xb, wu.astype(BF16), preferred_element_type=F32)
    return jnp.dot((_silu(g) * u).astype(BF16), wd.astype(BF16), preferred_element_type=F32)


EXPERT_RING = 3


def _experts_kernel(be_ref, used_ref, xs_ref, wg_ref, wu_ref, wd_ref, y_ref, xbuf, sems):
    i = pl.program_id(0)
    n = pl.num_programs(0)

    def block_copy(blk):
        slot = lax.rem(blk, EXPERT_RING)
        rows = pl.ds(pl.multiple_of(blk * MOE_BLOCK, MOE_BLOCK), MOE_BLOCK)
        return pltpu.make_async_copy(xs_ref.at[rows], xbuf.at[slot], sems.at[slot])

    @pl.when(i == 0)
    def _():
        for j in range(EXPERT_RING - 1):
            block_copy(jnp.int32(j)).start()

    @pl.when(i + EXPERT_RING - 1 < n)
    def _():
        block_copy(i + EXPERT_RING - 1).start()

    block_copy(i).wait()

    @pl.when(i < used_ref[0])
    def _():
        y = _swiglu(_unpack_bf16_pairs(xbuf[lax.rem(i, EXPERT_RING)]), wg_ref[0], wu_ref[0], wd_ref[0])
        y_ref[...] = _pack_bf16_pairs(_round_bf16(y))

    @pl.when(i >= used_ref[0])
    def _():
        y_ref[...] = jnp.zeros_like(y_ref)


def _experts(xs, block_expert, used_blocks, w_gate, w_up, w_down):
    cap = xs.shape[0]
    return pl.pallas_call(
        _experts_kernel,
        grid_spec=pltpu.PrefetchScalarGridSpec(
            num_scalar_prefetch=2,
            grid=(cap // MOE_BLOCK,),
            in_specs=[pl.BlockSpec(memory_space=pl.ANY),
                      pl.BlockSpec((1, D_MODEL, D_EXPERT), lambda i, be, used: (be[i], 0, 0)),
                      pl.BlockSpec((1, D_MODEL, D_EXPERT), lambda i, be, used: (be[i], 0, 0)),
                      pl.BlockSpec((1, D_EXPERT, D_MODEL), lambda i, be, used: (be[i], 0, 0))],
            out_specs=pl.BlockSpec((MOE_BLOCK, PACKED_D), lambda i, be, used: (i, 0)),
            scratch_shapes=[pltpu.VMEM((EXPERT_RING, MOE_BLOCK, PACKED_D), U32),
                            pltpu.SemaphoreType.DMA((EXPERT_RING,))]),
        out_shape=jax.ShapeDtypeStruct((cap, PACKED_D), U32),
        compiler_params=_cparams(("arbitrary",)),
        name="moe_experts",
    )(block_expert, used_blocks, xs, w_gate, w_up, w_down)


def _combine_kernel(start_ref, meta_ref, h_ref, ht_ref, info_ref, sg_ref, su_ref, sd_ref, g_ref, b_ref,
                    ys_ref, o_ref, ot_ref, buf_ref, sem, *, n_main):
    i = pl.program_id(0)
    tm = MOE_TOKENS

    @pl.when(i == 0)
    def _():
        buf_ref[...] = jnp.zeros_like(buf_ref)

    def fetch_expert(e, started):
        n = meta_ref[0, 0, e]
        return _start_run(ys_ref, start_ref[e] + meta_ref[0, 2, e], buf_ref, meta_ref[0, 1, e], n, sem, started)

    started = lax.fori_loop(0, N_EXPERTS, fetch_expert, tuple(jnp.int32(0) for _ in RUN_CHUNKS))
    h = _token_tile(i, n_main, h_ref, ht_ref)
    f = _swiglu(h, sg_ref[...], su_ref[...], sd_ref[...])
    info = info_ref[...]
    scol = lax.broadcasted_iota(jnp.int32, (tm, TILE_SLOTS), 1).astype(F32)
    mix = jnp.zeros((tm, TILE_SLOTS), F32)
    for k in range(TOP_K):
        mix = jnp.where(info[:, ROUTE_ROWS + k:ROUTE_ROWS + k + 1] == scol, info[:, k:k + 1], mix)
    _wait_runs(ys_ref, buf_ref, sem, started)
    acc = jnp.dot(mix.astype(BF16), _unpack_bf16_pairs(buf_ref[...]), preferred_element_type=F32)
    out = _layer_norm(DEEPNORM_ALPHA * h + (acc + f), g_ref[...], b_ref[...])

    @pl.when(i < n_main)
    def _():
        o_ref[...] = out

    @pl.when(i >= n_main)
    def _():
        ot_ref[...] = out


def _combine(h_main, h_tail, ys, meta, tokinfo, seg_start, sh_gate, sh_up, sh_down, ln_g, ln_b):
    tm = MOE_TOKENS
    n_main = h_main.shape[0] // tm
    const = lambda a: pl.BlockSpec(a.shape, lambda i, *_: (0,) * a.ndim)
    return pl.pallas_call(
        functools.partial(_combine_kernel, n_main=n_main),
        grid_spec=pltpu.PrefetchScalarGridSpec(
            num_scalar_prefetch=1,
            grid=(n_main + 1,),
            in_specs=[pl.BlockSpec((1, SUBLANES, LANES), lambda i, *_: (i, 0, 0), memory_space=pltpu.SMEM),
                      *_token_tile_specs(n_main),
                      pl.BlockSpec((tm, LANES), lambda i, *_: (i, 0)),
                      const(sh_gate), const(sh_up), const(sh_down), const(ln_g), const(ln_b),
                      pl.BlockSpec(memory_space=pl.ANY)],
            out_specs=list(_token_tile_specs(n_main)),
            scratch_shapes=[pltpu.VMEM((TILE_SLOTS, PACKED_D), U32), pltpu.SemaphoreType.DMA]),
        out_shape=[jax.ShapeDtypeStruct(h_main.shape, F32), jax.ShapeDtypeStruct((tm, D_MODEL), F32)],
        compiler_params=_cparams(("arbitrary",)),
        name="moe_combine",
    )(seg_start, meta, h_main, h_tail, tokinfo, sh_gate, sh_up, sh_down, ln_g, ln_b, ys)


def _moe_ln(h_main, h_tail, n_valid, router_w, router_bias, w_gate, w_up, w_down, sh_gate, sh_up, sh_down, ln_g, ln_b):
    n_tiles = h_main.shape[0] // MOE_TOKENS + 1
    slot_t, tokinfo, meta, cnt = _route(h_main, h_tail, router_w.T.astype(BF16),
                                        jnp.broadcast_to(router_bias.astype(F32)[:, None], (N_EXPERTS, LANES)), n_valid)
    counts = cnt[:, 0].astype(jnp.int32)
    padded = (counts + MOE_BLOCK - 1) // MOE_BLOCK * MOE_BLOCK
    seg_end = jnp.cumsum(padded)
    seg_start = seg_end - padded
    run_pad = n_tiles * N_EXPERTS * (SUBLANES - 1)
    n_blocks = -(-(n_valid * TOP_K + run_pad + N_EXPERTS * (MOE_BLOCK - 1)) // MOE_BLOCK)
    cap = n_blocks * MOE_BLOCK
    block_first_row = jnp.arange(n_blocks, dtype=jnp.int32) * MOE_BLOCK
    block_expert = jnp.minimum(jnp.sum((seg_end[None, :] <= block_first_row[:, None]).astype(jnp.int32), axis=1),
                               N_EXPERTS - 1)
    xs = _dispatch(h_main, h_tail, slot_t, meta, seg_start, counts, cap)
    used_blocks = (seg_end[N_EXPERTS - 1:] // MOE_BLOCK).astype(jnp.int32)
    ys = _experts(xs, block_expert, used_blocks, w_gate, w_up, w_down)
    return _combine(h_main, h_tail, ys, meta, tokinfo, seg_start, sh_gate.astype(BF16), sh_up.astype(BF16),
                    sh_down.astype(BF16), ln_g, ln_b)


def _ssd_sample_kernel(xbc_ref, z_ref, dtg_ref, sconv_ref, s0_ref, convw_ref, convb_ref, dtb_ref, alog_ref,
                       dskip_ref, normw_ref, y_ref, s_ref, conv_out_ref, xc_ref, dt_ref, da_ref):
    b = pl.program_id(0)

    @pl.when(b == 0)
    def _():
        xin = xbc_ref[...]
        xc = convw_ref[SSD_CONV - 1:SSD_CONV, :] * xin
        for k in range(SSD_CONV - 1):
            xc = xc + convw_ref[k:k + 1, :] * sconv_ref[k]
        xc_ref[...] = _silu(xc + convb_ref[...])
        dt = _softplus(dtg_ref[...] + dtb_ref[...])
        dt_ref[...] = dt
        da_ref[...] = jnp.exp(dt * (-jnp.exp(alog_ref[...])))
        for k in range(SSD_CONV - 2):
            conv_out_ref[k] = sconv_ref[k + 1]
        conv_out_ref[SSD_CONV - 2] = xin

    xc = xc_ref[pl.ds(b, 1), :]
    dt = dt_ref[pl.ds(b, 1), :]
    da = da_ref[pl.ds(b, 1), :]
    ns = SSD_GROUPS * SSD_STATE
    eye = (lax.broadcasted_iota(jnp.int32, (HEAD_DIM, HEAD_DIM), 0)
           == lax.broadcasted_iota(jnp.int32, (HEAD_DIM, HEAD_DIM), 1))
    hpg = SSD_HEADS // SSD_GROUPS
    y_parts = []
    for h in range(SSD_HEADS):
        g = h // hpg
        x_h = xc[:, h * HEAD_DIM:(h + 1) * HEAD_DIM]
        b_g = xc[:, SSD_D + g * SSD_STATE:SSD_D + (g + 1) * SSD_STATE]
        c_g = xc[:, SSD_D + ns + g * SSD_STATE:SSD_D + ns + (g + 1) * SSD_STATE]
        xdt_col = jnp.sum(jnp.where(eye, x_h * dt[:, h:h + 1], 0.0), axis=1, keepdims=True)
        s_new = da[:, h:h + 1] * s0_ref[0, h] + xdt_col * b_g
        s_ref[0, h] = s_new
        y_h = _bdot_nt(c_g, s_new) + dskip_ref[:, h * HEAD_DIM:(h + 1) * HEAD_DIM] * x_h
        y_parts.append(y_h)
    y = jnp.concatenate(y_parts, axis=1)
    y_ref[pl.ds(b, 1), :] = _gated_group_norm(y, z_ref[pl.ds(b, 1), :], normw_ref[...])


def _ssd_sample(xbc, z, dtg, state_conv_t, state_ssm, conv_w, conv_b, dt_bias_pad, a_log_pad, d_skip_full, norm_w):
    bs = xbc.shape[0]
    const = lambda a: pl.BlockSpec(a.shape, lambda b: (0,) * a.ndim)
    state_spec = pl.BlockSpec((1, SSD_HEADS, HEAD_DIM, SSD_STATE), lambda b: (b, 0, 0, 0))
    return pl.pallas_call(
        _ssd_sample_kernel,
        grid=(bs,),
        in_specs=[const(xbc), const(z), const(dtg), const(state_conv_t), state_spec, const(conv_w), const(conv_b),
                  const(dt_bias_pad), const(a_log_pad), const(d_skip_full), const(norm_w)],
        out_specs=[pl.BlockSpec((bs, SSD_D), lambda b: (0, 0)), state_spec,
                   pl.BlockSpec((SSD_CONV - 1, bs, SSD_CONV_CH), lambda b: (0, 0, 0))],
        out_shape=[jax.ShapeDtypeStruct((bs, SSD_D), F32),
                   jax.ShapeDtypeStruct(state_ssm.shape, F32),
                   jax.ShapeDtypeStruct((SSD_CONV - 1, bs, SSD_CONV_CH), F32)],
        scratch_shapes=[pltpu.VMEM((bs, SSD_CONV_CH), F32), pltpu.VMEM((bs, LANES), F32),
                        pltpu.VMEM((bs, LANES), F32)],
        compiler_params=_cparams(("arbitrary",)),
        name="ssd_sample",
    )(xbc, z, dtg, state_conv_t, state_ssm, conv_w, conv_b, dt_bias_pad, a_log_pad, d_skip_full, norm_w)


SEL_PAST = TOP_N - 1
BLOCKS_PER_PAGE = PAGE_SIZE // CMP_BLOCK
KV_FEATS = 2 * KV_D


def _compress_consts_t(cmp_pe, cmp_w1, cmp_b1, cmp_w2, cmp_b2):
    pe_t = jnp.stack([jnp.tile(cmp_pe[k].T, (1, BLOCKS_PER_PAGE)) for k in range(2)])
    w1_t = jnp.stack([_block_diag2(jnp.swapaxes(cmp_w1[k], 0, 1)) for k in range(2)]).astype(BF16)
    b1_t = jnp.stack([jnp.tile(cmp_b1[k], BLOCKS_PER_PAGE) for k in range(2)])[:, None, :]
    w2_t = jnp.stack([_block_diag2(cmp_w2[k]) for k in range(2)]).astype(BF16)
    b2_t = jnp.stack([jnp.tile(cmp_b2[k], BLOCKS_PER_PAGE) for k in range(2)])[:, None, :]
    return pe_t, w1_t, b1_t, w2_t, b2_t


def _compress_pages_kernel(pt_ref, pe_ref, w1_ref, b1_ref, w2_ref, b2_ref, pool_ref, o_ref, kbuf, vbuf, sems, *,
                           n_pages):
    b = pl.program_id(0)
    nb = pl.num_programs(0)
    bufs = (kbuf, vbuf)

    def half_copy(seq, kind, p):
        return pltpu.make_async_copy(pool_ref.at[pt_ref[seq, p], pl.ds(kind * KV_D, KV_D)],
                                     bufs[kind].at[:, p], sems.at[kind])

    def start_half(seq, kind):
        def start_pair(q, c):
            half_copy(seq, kind, 2 * q).start(priority=0)
            half_copy(seq, kind, 2 * q + 1).start(priority=1)
            return c

        lax.fori_loop(0, n_pages // 2, start_pair, 0)

    def wait_half(seq, kind):
        lax.fori_loop(0, n_pages, lambda p, c: (half_copy(seq, kind, p).wait(), c)[1], 0)

    @pl.when(b == 0)
    def _():
        start_half(b, 0)
        start_half(b, 1)

    for kind in range(2):
        wait_half(b, kind)
        def add_feature(d, acc, kind=kind):
            x = jnp.concatenate([bufs[kind][h * HEAD_DIM + d] for h in range(NSA_KV_HEADS)], axis=0) \
                + pe_ref[kind, pl.ds(d, 1), :]
            return acc + jnp.dot(x.astype(BF16), w1_ref[kind, d], preferred_element_type=F32)

        acc = lax.fori_loop(0, HEAD_DIM, add_feature,
                            jnp.zeros((NSA_KV_HEADS * n_pages, BLOCKS_PER_PAGE * CMP_HIDDEN), F32), unroll=8)
        hid = _silu(acc + b1_ref[kind])
        out = jnp.dot(hid.astype(BF16), w2_ref[kind], preferred_element_type=F32) + b2_ref[kind]
        for h in range(NSA_KV_HEADS):
            o_ref[0, kind * NSA_KV_HEADS + h] = out[h * n_pages:(h + 1) * n_pages]

        @pl.when(b + 1 < nb)
        def _(kind=kind):
            start_half(b + 1, kind)


def _compress_pages(pool_t, page_table, consts):
    bs, n_pages = page_table.shape
    const = lambda a: pl.BlockSpec(a.shape, lambda b, pt: (0,) * a.ndim)
    return pl.pallas_call(
        functools.partial(_compress_pages_kernel, n_pages=n_pages),
        grid_spec=pltpu.PrefetchScalarGridSpec(
            num_scalar_prefetch=1,
            grid=(bs,),
            in_specs=[const(a) for a in consts] + [pl.BlockSpec(memory_space=pl.ANY)],
            out_specs=pl.BlockSpec((1, 2 * NSA_KV_HEADS, n_pages, LANES), lambda b, pt: (b, 0, 0, 0)),
            scratch_shapes=[pltpu.VMEM((KV_D, n_pages, PAGE_SIZE), F32), pltpu.VMEM((KV_D, n_pages, PAGE_SIZE), F32),
                            pltpu.SemaphoreType.DMA((2,))]),
        out_shape=jax.ShapeDtypeStruct((bs, 2 * NSA_KV_HEADS, n_pages, LANES), F32),
        compiler_params=_cparams(("arbitrary",)),
        name="compress_pages",
    )(page_table, *consts, pool_t)


def _group_heads(q_row, hk):
    hpg = NSA_HEADS // NSA_KV_HEADS
    low = lax.broadcasted_iota(jnp.int32, (1, LANES), 1) < HEAD_DIM
    rows = []
    for r in range(hpg):
        head = hk * hpg + r
        tile = q_row[:, (head // 2) * LANES:(head // 2 + 1) * LANES]
        if head % 2 == 1:
            tile = pltpu.roll(tile, HEAD_DIM, 1)
        rows.append(jnp.where(low, tile, 0.0))
    return jnp.concatenate(rows + [jnp.zeros((SUBLANES - hpg, LANES), F32)], axis=0)


def _spread_heads(o_groups):
    hpg = NSA_HEADS // NSA_KV_HEADS
    return jnp.concatenate([o[r:r + 1, 0:HEAD_DIM] for o in o_groups for r in range(hpg)], axis=1)


def _nsa_sample_cmp_t_kernel(qc_ref, cmp_ref, ocmp_ref, idx_ref, *, n_pages):
    b = pl.program_id(0)
    nc = n_pages * BLOCKS_PER_PAGE
    scale = HEAD_DIM ** -0.5
    hpg = NSA_HEADS // NSA_KV_HEADS
    q_row = qc_ref[pl.ds(b, 1), :] * scale
    lane = lax.broadcasted_iota(jnp.int32, (1, LANES), 1)
    pos_r = lax.broadcasted_iota(jnp.int32, (1, nc), 1)
    bid_r = (pos_r % n_pages) * BLOCKS_PER_PAGE + pos_r // n_pages
    pos_c = lax.broadcasted_iota(jnp.int32, (nc, 1), 0)
    bid_c = (pos_c % n_pages) * BLOCKS_PER_PAGE + pos_c // n_pages
    o_groups = []
    for hk in range(NSA_KV_HEADS):
        kc = cmp_ref[0, hk].astype(BF16)
        vc = cmp_ref[0, NSA_KV_HEADS + hk].astype(BF16)
        qh = _group_heads(q_row, hk)
        s = jnp.concatenate(
            [lax.dot_general(pltpu.roll(qh, c * HEAD_DIM, 1).astype(BF16) if c else qh.astype(BF16), kc,
                             (((1,), (1,)), ((), ())), preferred_element_type=F32)
             for c in range(BLOCKS_PER_PAGE)], axis=1)
        ex = jnp.exp(s - jnp.max(s, axis=-1, keepdims=True))
        p = ex / jnp.sum(ex, axis=-1, keepdims=True)
        o = jnp.dot(p[:, 0:n_pages].astype(BF16), vc, preferred_element_type=F32)
        for c in range(1, BLOCKS_PER_PAGE):
            oc = jnp.dot(p[:, c * n_pages:(c + 1) * n_pages].astype(BF16), vc, preferred_element_type=F32)
            o = o + pltpu.roll(oc, LANES - c * HEAD_DIM, 1)
        o_groups.append(o)
        hrow = lax.broadcasted_iota(jnp.int32, p.shape, 0) < hpg
        imp = jnp.sum(jnp.where(hrow, p, 0.0), axis=0, keepdims=True)
        score = jnp.where((bid_r == 0) | (bid_r == nc - 1), FORCED_SCORE, imp)
        score_col = jnp.concatenate([score, jnp.zeros((LANES - 1, nc), F32)], 0).T[:, 0:1]
        beats = (score_col > score) | ((score_col == score) & (bid_c < bid_r))
        rank = jnp.sum(beats.astype(F32), axis=0, keepdims=True)
        row = jnp.zeros((1, LANES), F32)
        bid_f = bid_r.astype(F32)
        for k in range(SEL_PAST):
            blk = jnp.sum(jnp.where(rank == k, bid_f, 0.0), axis=1, keepdims=True)
            row = jnp.where(lane == k, blk, row)
        idx_ref[pl.ds(b * NSA_KV_HEADS + hk, 1), :] = row.astype(jnp.int32)
    ocmp_ref[pl.ds(b, 1), :] = _spread_heads(o_groups)


def _nsa_sample_cmp_t(qc, kvcmp_t):
    bs, _, n_pages, _ = kvcmp_t.shape
    return pl.pallas_call(
        functools.partial(_nsa_sample_cmp_t_kernel, n_pages=n_pages),
        grid=(bs,),
        in_specs=[pl.BlockSpec((bs, NSA_D), lambda b: (0, 0)),
                  pl.BlockSpec((1, 2 * NSA_KV_HEADS, n_pages, LANES), lambda b: (b, 0, 0, 0))],
        out_specs=[pl.BlockSpec((bs, NSA_D), lambda b: (0, 0)),
                   pl.BlockSpec((bs * NSA_KV_HEADS, LANES), lambda b: (0, 0))],
        out_shape=[jax.ShapeDtypeStruct((bs, NSA_D), F32),
                   jax.ShapeDtypeStruct((bs * NSA_KV_HEADS, LANES), jnp.int32)],
        compiler_params=_cparams(("arbitrary",)),
        name="nsa_sample_cmp",
    )(qc, kvcmp_t)


def _sel_block_copies(pool_ref, pt_ref, sel_ref, kbuf, vbuf, sem, b, hk, k):
    blk = sel_ref[b * NSA_KV_HEADS + hk, k]
    page = pt_ref[b, lax.shift_right_logical(blk, int(math.log2(BLOCKS_PER_PAGE)))]
    j = hk * SEL_PAST + k
    return (pltpu.make_async_copy(pool_ref.at[page, pl.ds(hk * HEAD_DIM, HEAD_DIM)], kbuf.at[j], sem),
            pltpu.make_async_copy(pool_ref.at[page, pl.ds(KV_D + hk * HEAD_DIM, HEAD_DIM)], vbuf.at[j], sem))


def _nsa_sample_attn_t_kernel(pt_ref, sel_ref, qr_ref, new_sel_ref, new_win_ref, win_ref, dtg_ref, ocmp_ref,
                              pool_ref, o_ref, kbuf, vbuf, sem):
    b = pl.program_id(0)
    for hk in range(NSA_KV_HEADS):
        for k in range(SEL_PAST):
            for cp in _sel_block_copies(pool_ref, pt_ref, sel_ref, kbuf, vbuf, sem, b, hk, k):
                cp.start()
    for hk in range(NSA_KV_HEADS):
        for k in range(SEL_PAST):
            for cp in _sel_block_copies(pool_ref, pt_ref, sel_ref, kbuf, vbuf, sem, b, hk, k):
                cp.wait()
    scale = HEAD_DIM ** -0.5
    q_row = qr_ref[pl.ds(b, 1), :] * scale
    sig = _sigmoid(dtg_ref[pl.ds(b, 1), :])
    lane = lax.broadcasted_iota(jnp.int32, (1, PAGE_SIZE), 1)
    o_slc, o_win = [], []
    for hk in range(NSA_KV_HEADS):
        qh = _group_heads(q_row, hk)[:, 0:HEAD_DIM].astype(BF16)

        def new_row(ref, kind):
            t = ref[pl.ds(b, 1), :][:, kind * KV_D:(kind + 1) * KV_D]
            if hk == 1:
                t = pltpu.roll(t, HEAD_DIM, 1)
            return t[:, 0:HEAD_DIM].astype(BF16).astype(F32)

        def attend(kt, vt, mask, new_ref, n_new):
            s = jnp.dot(qh, kt.astype(BF16), preferred_element_type=F32)
            if mask is not None:
                s = jnp.where(mask, s, NEG)
            s_new = jnp.sum(qh.astype(F32) * new_row(new_ref, 0), axis=1, keepdims=True)
            m = jnp.maximum(jnp.max(s, axis=-1, keepdims=True), s_new)
            ex = jnp.exp(s - m)
            ex_new = jnp.exp(s_new - m) * n_new
            den = jnp.sum(ex, axis=-1, keepdims=True) + ex_new
            o = lax.dot_general((ex / den).astype(BF16), vt.astype(BF16), (((1,), (1,)), ((), ())),
                                preferred_element_type=F32)
            return o + (ex_new / den).astype(BF16).astype(F32) * new_row(new_ref, 1)

        kt = jnp.concatenate([kbuf[hk * SEL_PAST + k] for k in range(SEL_PAST)], axis=1)
        vt = jnp.concatenate([vbuf[hk * SEL_PAST + k] for k in range(SEL_PAST)], axis=1)
        mask = jnp.concatenate(
            [lane // SEL_BLOCK == (sel_ref[b * NSA_KV_HEADS + hk, k] & (BLOCKS_PER_PAGE - 1))
             for k in range(SEL_PAST)], axis=1)
        o_slc.append(attend(kt, vt, mask, new_sel_ref, float(SEL_BLOCK)))
        o_win.append(attend(win_ref[0, hk * HEAD_DIM:(hk + 1) * HEAD_DIM, :],
                            win_ref[0, KV_D + hk * HEAD_DIM:KV_D + (hk + 1) * HEAD_DIM, :], None, new_win_ref, 1.0))
    gates = []
    for br in range(3):
        gates.append(jnp.concatenate(
            [jnp.broadcast_to(sig[:, GATE_COL0 + h * 3 + br:GATE_COL0 + h * 3 + br + 1], (1, HEAD_DIM))
             for h in range(NSA_HEADS)], axis=1))
    o_ref[pl.ds(b, 1), :] = (gates[0] * ocmp_ref[pl.ds(b, 1), :] + gates[1] * _spread_heads(o_slc)
                             + gates[2] * _spread_heads(o_win))


def _nsa_sample_attn_t(qr, new_sel, new_win, win_t, dtg, o_cmp, pool_sel_t, page_table, sel_idx):
    bs = qr.shape[0]
    const = lambda a: pl.BlockSpec(a.shape, lambda b, pt, sel: (0,) * a.ndim)
    n_buf = NSA_KV_HEADS * SEL_PAST
    return pl.pallas_call(
        _nsa_sample_attn_t_kernel,
        grid_spec=pltpu.PrefetchScalarGridSpec(
            num_scalar_prefetch=2,
            grid=(bs,),
            in_specs=[const(qr), const(new_sel), const(new_win),
                      pl.BlockSpec((1,) + win_t.shape[1:], lambda b, pt, sel: (b, 0, 0)),
                      const(dtg), const(o_cmp), pl.BlockSpec(memory_space=pl.ANY)],
            out_specs=pl.BlockSpec((bs, NSA_D), lambda b, pt, sel: (0, 0)),
            scratch_shapes=[pltpu.VMEM((n_buf, HEAD_DIM, PAGE_SIZE), F32), pltpu.VMEM((n_buf, HEAD_DIM, PAGE_SIZE), F32),
                            pltpu.SemaphoreType.DMA]),
        out_shape=jax.ShapeDtypeStruct((bs, NSA_D), F32),
        compiler_params=_cparams(("arbitrary",)),
        name="nsa_sample_attn",
    )(page_table, sel_idx, qr, new_sel, new_win, win_t, dtg, o_cmp, pool_sel_t)


def kernel(x_prompt, x_sample, cache_kv_cmp, cache_kv_sel, page_table, cache_kv_win, state_ssm, state_conv,
           emb_ln_g, emb_ln_b, w_in, conv_w, conv_b, dt_bias, a_log, d_skip, ssd_norm_w,
           cmp_pe, cmp_w1, cmp_b1, cmp_w2, cmp_b2, w_out, ln1_g, ln1_b,
           router_w, router_bias, exp_w_gate, exp_w_up, exp_w_down,
           sh_w_gate, sh_w_up, sh_w_down, ln2_g, ln2_b):
    bp, tp, _ = x_prompt.shape
    bs, ts, _ = x_sample.shape
    assert ts == 1 and DEPTH == 1
    n_prompt = bp * tp
    past_len = page_table.shape[1] * PAGE_SIZE
    l = 0
    w_perm = _permute_w_in(w_in[l])
    ln0_g, ln0_b = emb_ln_g[None], emb_ln_b[None]
    ssd_consts = (conv_w[l], conv_b[l][None], _pad_lanes(dt_bias[l]), _pad_lanes(a_log[l]),
                  jnp.repeat(d_skip[l], HEAD_DIM)[None], ssd_norm_w[l][None])
    cmp_consts = _compress_consts(cmp_pe[l], cmp_w1[l], cmp_b1[l], cmp_w2[l], cmp_b2[l])
    w_o = w_out[l].astype(BF16)
    w_o_ssd, w_o_nsa = w_o[:SSD_D], w_o[SSD_D:]
    ln1 = (ln1_g[l][None], ln1_b[l][None])
    kv_shape = (2, NSA_KV_HEADS, HEAD_DIM)

    hp, z, xbc, qc, qr, kvc, kvs, kvw, dtg, kvc_t, kvs_t, kvw_t = _inproj(
        x_prompt.reshape(n_prompt, D_MODEL), ln0_g, ln0_b, w_perm, _rope_tables(jnp.arange(tp)), 512,
        _rope_tables_t(jnp.arange(tp)))
    y_ssd, ssm_p, conv_p = _ssd_prompt(xbc, z, dtg, *ssd_consts, bp, tp)
    kvcmp = _compress_prompt(kvc, cmp_consts, tp)
    y_nsa = _nsa_prompt(qc, qr, dtg, kvcmp, kvs, kvw, bp, tp)
    h1p = _outproj(y_ssd, y_nsa, hp, w_o_ssd, w_o_nsa, *ln1, 512)
    n_keep = min(WINDOW, tp)
    cache_leaf = lambda a: jnp.transpose(a.reshape((bp,) + kv_shape + (a.shape[-1],)), (0, 4, 1, 2, 3))[None]
    kvc_p = cache_leaf(kvc_t)
    kvs_p = cache_leaf(kvs_t)
    kvw_p = cache_leaf(kvw_t[:, :, tp - n_keep:])

    s_hs, s_z, s_xbc, s_qc, s_qr, s_kvc, s_kvs, s_kvw, s_dtg = _inproj(
        x_sample.reshape(bs, D_MODEL), ln0_g, ln0_b, w_perm, _rope_tables(jnp.full((bs,), past_len)), bs)
    s_y_ssd, ssm_s, conv_s_t = _ssd_sample(s_xbc, s_z, s_dtg, jnp.swapaxes(state_conv[l], 0, 1), state_ssm[l],
                                           *ssd_consts)
    n_pool = cache_kv_cmp.shape[1]
    feature_major = lambda c, rows: jnp.swapaxes(c.reshape(-1, rows, 2 * KV_D), 1, 2)
    s_kvcmp = _compress_pages(feature_major(cache_kv_cmp[l], PAGE_SIZE), page_table,
                              _compress_consts_t(cmp_pe[l], cmp_w1[l], cmp_b1[l], cmp_w2[l], cmp_b2[l]))
    s_o_cmp, s_sel = _nsa_sample_cmp_t(s_qc, s_kvcmp)
    buf_win = cache_kv_win[l].reshape(bs, -1, 2 * KV_D)
    s_y_nsa = _nsa_sample_attn_t(
        s_qr, s_kvs, s_kvw, feature_major(cache_kv_win[l], buf_win.shape[1]), s_dtg, s_o_cmp,
        feature_major(cache_kv_sel[l], PAGE_SIZE), page_table, s_sel)
    h1s = _outproj(s_y_ssd, s_y_nsa, s_hs, w_o_ssd, w_o_nsa, *ln1, bs)
    win_all = jnp.concatenate([buf_win, s_kvw[:, None, :]], 1)
    n_keep_s = min(WINDOW, past_len + ts)
    kvw_s = win_all[:, win_all.shape[1] - n_keep_s:].reshape((1, bs, n_keep_s) + kv_shape)
    kvc_s = s_kvc.reshape((1, bs, ts) + kv_shape)
    kvs_s = s_kvs.reshape((1, bs, ts) + kv_shape)

    assert n_prompt % MOE_TOKENS == 0 and bs * ts <= MOE_TOKENS
    n_tok = n_prompt + bs * ts
    tail = jnp.concatenate([h1s, jnp.zeros((MOE_TOKENS - bs * ts, D_MODEL), F32)], 0)
    out_main, out_tail = _moe_ln(h1p, tail, n_tok, router_w[l], router_bias[l], exp_w_gate[l], exp_w_up[l],
                                 exp_w_down[l], sh_w_gate[l], sh_w_up[l], sh_w_down[l], ln2_g[l][None], ln2_b[l][None])
    y_prompt = out_main.reshape(bp, tp, D_MODEL)
    y_sample = out_tail[:bs * ts].reshape(bs, ts, D_MODEL)
    return (y_prompt, y_sample, kvc_p, kvs_p, kvw_p, ssm_p[None], conv_p[None],
            kvc_s, kvs_s, kvw_s, ssm_s[None], jnp.swapaxes(conv_s_t, 0, 1)[None])
```
